```python
import math
import jax
import jax.numpy as jnp
from jax import lax
import numpy as np

D_MODEL = 2048
BATCH = 4
SEQ = 2048
DEPTH = 2

HEAD_DIM = 64
Q_BLOCK = 128
NEG_INF = -1e30
EPS = 1e-6

NSA_HEADS = 8
NSA_KV_HEADS = 2
NSA_GQA = NSA_HEADS // NSA_KV_HEADS
NSA_WIDTH = NSA_HEADS * HEAD_DIM
CMP_BLOCK = 32
CMP_STRIDE = 16
SEL_BLOCK = 64
SEL_TOPK = 8
FORCE_SCORE = 1e3
WINDOW = 512

SB_HEADS = 8
SB_WIDTH = SB_HEADS * HEAD_DIM

SSD_HEADS = 16
SSD_HEAD_DIM = 64
SSD_INNER = SSD_HEADS * SSD_HEAD_DIM
SSD_GROUPS = 2
SSD_HEADS_PER_GROUP = SSD_HEADS // SSD_GROUPS
SSD_STATE = 128
SSD_CONV = 4
SSD_CHUNK = 128
SSD_CONV_DIM = SSD_INNER + 2 * SSD_GROUPS * SSD_STATE

D_MIX = NSA_WIDTH + SB_WIDTH + SSD_INNER

D_FF = 5632

REL_BUCKETS = 32
REL_MAX_DIST = 128

IN_COLS = (NSA_WIDTH,
           3 * 2 * NSA_KV_HEADS * HEAD_DIM,
           3 * NSA_HEADS,
           3 * SB_WIDTH,
           SSD_INNER,
           SSD_CONV_DIM,
           SSD_HEADS)
D_IN_PROJ = sum(IN_COLS)

kernel_name = 'hybrid_nsa_stickbreak_ssd_macaron'


def rmsnorm(x, g):
    xf = x.astype(jnp.float32)
    y = xf * lax.rsqrt(jnp.mean(xf * xf, axis=-1, keepdims=True) + EPS)
    return (y * g.astype(jnp.float32)).astype(x.dtype)


def swiglu(x, w_gate, w_up, w_down):
    return (jax.nn.silu(x @ w_gate) * (x @ w_up)) @ w_down


def rel_bucket(dist):
    dist = jnp.maximum(dist, 0)
    max_exact = REL_BUCKETS // 2
    log_ratio = jnp.log(jnp.maximum(dist, 1).astype(jnp.float32) / max_exact) / math.log(REL_MAX_DIST / max_exact)
    large = jnp.minimum(max_exact + (log_ratio * (REL_BUCKETS - max_exact)).astype(jnp.int32), REL_BUCKETS - 1)
    return jnp.where(dist < max_exact, dist, large)


def compress_blocks(u, pos, w):
    B_, T, Hk, D = u.shape
    n_cmp = (T - CMP_BLOCK) // CMP_STRIDE + 1
    idx = jnp.arange(n_cmp)[:, None] * CMP_STRIDE + jnp.arange(CMP_BLOCK)[None, :]
    blk = u[:, idx] + pos[:, None, :]
    blk = blk.transpose(0, 1, 3, 2, 4).reshape(B_, n_cmp, Hk, CMP_BLOCK * D)
    return blk @ w


def native_sparse_attention(q, kv, gate_logits, cmp_pos, cmp_w, rel_bias):
    B_, T, _, D = q.shape
    Hk, G = NSA_KV_HEADS, NSA_GQA
    scale = D ** -0.5
    n_qb = T // Q_BLOCK
    t_pos = jnp.arange(T)
    qg = q.reshape(B_, T, Hk, G, D)
    bias_tab = rel_bias.reshape(REL_BUCKETS, Hk, G)

    k_cmp = compress_blocks(kv[:, :, 0, 0], cmp_pos[0], cmp_w[0])
    v_cmp = compress_blocks(kv[:, :, 0, 1], cmp_pos[1], cmp_w[1])
    n_cmp = k_cmp.shape[1]
    cmp_start = jnp.arange(n_cmp) * CMP_STRIDE
    cmp_end = cmp_start + CMP_BLOCK - 1
    dist_c = t_pos[:, None] - cmp_end[None, :]
    valid_c = dist_c >= 0
    bias_c = bias_tab[rel_bucket(dist_c)].transpose(2, 3, 0, 1)
    s_c = jnp.einsum('bthgd,bchd->bhgtc', qg, k_cmp, preferred_element_type=jnp.float32) * scale
    p_c = jax.nn.softmax(jnp.where(valid_c, s_c + bias_c, NEG_INF), axis=-1)
    p_c = jnp.where(jnp.any(valid_c, axis=-1)[:, None], p_c, 0.0)
    o_cmp = jnp.einsum('bhgtc,bchd->bthgd', p_c.astype(v_cmp.dtype), v_cmp)

    n_sel = T // SEL_BLOCK
    top_k = min(SEL_TOPK, n_sel)
    sel_start = jnp.arange(n_sel) * SEL_BLOCK
    overlap = jnp.maximum(jnp.minimum(cmp_end[:, None], sel_start[None, :] + SEL_BLOCK - 1)
                          - jnp.maximum(cmp_start[:, None], sel_start[None, :]) + 1, 0).astype(jnp.float32) / CMP_BLOCK
    p_sel = jnp.einsum('bhgtc,cj->bhtj', p_c, overlap)
    cur = t_pos // SEL_BLOCK
    jj = jnp.arange(n_sel)[None, :]
    eligible = sel_start[None, :] <= t_pos[:, None]
    forced = (jj == 0) | (jj == cur[:, None]) | (jj == cur[:, None] - 1)
    score = jnp.where(eligible, p_sel + FORCE_SCORE * forced.astype(jnp.float32), NEG_INF)
    top_val, top_idx = lax.top_k(score, top_k)
    top_ok = top_val > 0.5 * NEG_INF

    k_sel = kv[:, :, 1, 0].transpose(0, 2, 1, 3).reshape(B_, Hk, n_sel, SEL_BLOCK, D)
    v_sel = kv[:, :, 1, 1].transpose(0, 2, 1, 3).reshape(B_, Hk, n_sel, SEL_BLOCK, D)
    b_ix = jnp.arange(B_)[:, None, None, None]
    h_ix = jnp.arange(Hk)[None, :, None, None]

    def sel_block(args):
        q_blk, idx_blk, ok_blk, i = args
        k_g = k_sel[b_ix, h_ix, idx_blk]
        v_g = v_sel[b_ix, h_ix, idx_blk]
        t_q = i * Q_BLOCK + jnp.arange(Q_BLOCK)
        key_pos = idx_blk[..., None] * SEL_BLOCK + jnp.arange(SEL_BLOCK)
        dist = t_q[:, None, None] - key_pos
        ok = (dist >= 0) & ok_blk[..., None]
        bias = bias_tab[rel_bucket(dist), h_ix[..., None]].transpose(0, 1, 5, 2, 3, 4)
        s = jnp.einsum('bhgqd,bhqkld->bhgqkl', q_blk, k_g, preferred_element_type=jnp.float32) * scale
        s = jnp.where(ok[:, :, None], s + bias, NEG_INF).reshape(B_, Hk, G, Q_BLOCK, top_k * SEL_BLOCK)
        p = jax.nn.softmax(s, axis=-1).reshape(B_, Hk, G, Q_BLOCK, top_k, SEL_BLOCK)
        return jnp.einsum('bhgqkl,bhqkld->bqhgd', p.astype(v_g.dtype), v_g)

    q_chunks = qg.reshape(B_, n_qb, Q_BLOCK, Hk, G, D).transpose(1, 0, 3, 4, 2, 5)
    idx_chunks = top_idx.reshape(B_, Hk, n_qb, Q_BLOCK, top_k).transpose(2, 0, 1, 3, 4)
    ok_chunks = top_ok.reshape(B_, Hk, n_qb, Q_BLOCK, top_k).transpose(2, 0, 1, 3, 4)
    o_sel = lax.map(sel_block, (q_chunks, idx_chunks, ok_chunks, jnp.arange(n_qb)))
    o_sel = o_sel.transpose(1, 0, 2, 3, 4, 5).reshape(B_, T, Hk, G, D)

    n_wb = WINDOW // Q_BLOCK

    def band(u):
        up = jnp.pad(u, ((0, 0), (WINDOW, 0), (0, 0), (0, 0)))
        ub = up.reshape(B_, n_qb + n_wb, Q_BLOCK, Hk, D)
        return jnp.concatenate([ub[:, j:j + n_qb] for j in range(n_wb + 1)], axis=2)

    k_win = band(kv[:, :, 2, 0])
    v_win = band(kv[:, :, 2, 1])
    r_ix = jnp.arange(Q_BLOCK)
    m_ix = jnp.arange(WINDOW + Q_BLOCK)
    dist_w = WINDOW + r_ix[:, None] - m_ix[None, :]
    key_pos_w = jnp.arange(n_qb)[:, None] * Q_BLOCK - WINDOW + m_ix[None, :]
    ok_w = (dist_w >= 0) & (dist_w < WINDOW) & (key_pos_w >= 0)[:, None, :]
    bias_w = bias_tab[rel_bucket(dist_w)].transpose(2, 3, 0, 1)[:, :, None]
    q_w = qg.reshape(B_, n_qb, Q_BLOCK, Hk, G, D)
    s_w = jnp.einsum('bnqhgd,bnmhd->bhgnqm', q_w, k_win, preferred_element_type=jnp.float32) * scale
    p_w = jax.nn.softmax(jnp.where(ok_w, s_w + bias_w, NEG_INF), axis=-1)
    o_win = jnp.einsum('bhgnqm,bnmhd->bnqhgd', p_w.astype(v_win.dtype), v_win).reshape(B_, T, Hk, G, D)

    g = jax.nn.sigmoid(gate_logits.astype(jnp.float32)).reshape(B_, T, Hk, G, 3)
    o = g[..., 0:1] * o_cmp + g[..., 1:2] * o_sel + g[..., 2:3] * o_win
    return o.reshape(B_, T, NSA_WIDTH).astype(q.dtype)


def stick_breaking_attention(q, k, v):
    B_, T, H, D = q.shape
    n_qb = T // Q_BLOCK
    scale = D ** -0.5
    key_pos = jnp.arange(T)
    q_chunks = q.reshape(B_, n_qb, Q_BLOCK, H, D).transpose(1, 0, 2, 3, 4)

    def block(args):
        q_blk, i = args
        z = jnp.einsum('bqhd,bshd->bhqs', q_blk, k, preferred_element_type=jnp.float32) * scale
        t_q = i * Q_BLOCK + jnp.arange(Q_BLOCK)
        mask = key_pos[None, :] < t_q[:, None]
        log_not = jnp.where(mask, jax.nn.log_sigmoid(-z), 0.0)
        between = lax.cumsum(log_not, axis=3, reverse=True) - log_not
        a = jnp.where(mask, jnp.exp(jax.nn.log_sigmoid(z) + between), 0.0)
        return jnp.einsum('bhqs,bshd->bqhd', a.astype(v.dtype), v)

    out = lax.map(block, (q_chunks, jnp.arange(n_qb)))
    return out.transpose(1, 0, 2, 3, 4).reshape(B_, T, H * D)


def causal_depthwise_conv(u, w, b):
    out = lax.conv_general_dilated(u, w[:, None, :], window_strides=(1,), padding=[(SSD_CONV - 1, 0)],
                                   dimension_numbers=('NWC', 'WIO', 'NWC'), feature_group_count=u.shape[-1])
    return out + b


def ssd_chunked_scan(x, a_dt, bm, cm):
    B_, T, G, E, P = x.shape
    N = bm.shape[-1]
    nc, L = T // SSD_CHUNK, SSD_CHUNK
    xc = x.reshape(B_, nc, L, G, E, P)
    bc = bm.reshape(B_, nc, L, G, N)
    cc = cm.reshape(B_, nc, L, G, N)
    a_cs = jnp.cumsum(a_dt.reshape(B_, nc, L, G, E).transpose(0, 3, 4, 1, 2), axis=-1)
    li = jnp.arange(L)
    causal = li[:, None] >= li[None, :]
    seg = jnp.exp(jnp.where(causal, a_cs[..., :, None] - a_cs[..., None, :], -jnp.inf))
    cb = jnp.einsum('bclgn,bcsgn->bgcls', cc, bc)
    y_diag = jnp.einsum('bgcls,bgecls,bcsgep->bclgep', cb, seg, xc)
    decay_to_end = jnp.exp(a_cs[..., -1:] - a_cs)
    chunk_states = jnp.einsum('bclgn,bgecl,bclgep->bcgepn', bc, decay_to_end, xc)
    chunk_decay = jnp.exp(a_cs[..., -1])

    def step(h, inp):
        st, dec = inp
        return h * dec[..., None, None] + st, h

    h0 = jnp.zeros((B_, G, E, P, N), chunk_states.dtype)
    _, h_in = lax.scan(step, h0, (chunk_states.transpose(1, 0, 2, 3, 4, 5), chunk_decay.transpose(3, 0, 1, 2)))
    h_in = h_in.transpose(1, 0, 2, 3, 4, 5)
    y_off = jnp.einsum('bclgn,bcgepn,bgecl->bclgep', cc, h_in, jnp.exp(a_cs))
    return (y_diag + y_off).reshape(B_, T, G, E, P)


def mamba2_ssd(z, xbc, dt, conv_w, conv_b, dt_bias, a_log, d_skip, norm_g):
    B_, T, _ = xbc.shape
    G, E, P, N = SSD_GROUPS, SSD_HEADS_PER_GROUP, SSD_HEAD_DIM, SSD_STATE
    xbc = jax.nn.silu(causal_depthwise_conv(xbc, conv_w, conv_b))
    xs, bm, cm = jnp.split(xbc, [SSD_INNER, SSD_INNER + G * N], axis=-1)
    xs = xs.reshape(B_, T, G, E, P).astype(jnp.float32)
    bm = bm.reshape(B_, T, G, N).astype(jnp.float32)
    cm = cm.reshape(B_, T, G, N).astype(jnp.float32)
    dt = jax.nn.softplus(dt.astype(jnp.float32) + dt_bias.astype(jnp.float32)).reshape(B_, T, G, E)
    a = -jnp.exp(a_log.astype(jnp.float32)).reshape(G, E)
    y = ssd_chunked_scan(xs * dt[..., None], dt * a, bm, cm)
    y = y + d_skip.astype(jnp.float32).reshape(G, E)[:, :, None] * xs
    y = y.reshape(B_, T, SSD_INNER) * jax.nn.silu(z.astype(jnp.float32))
    yg = y.reshape(B_, T, G, SSD_INNER // G)
    yg = yg * lax.rsqrt(jnp.mean(yg * yg, axis=-1, keepdims=True) + EPS)
    return (yg.reshape(B_, T, SSD_INNER) * norm_g.astype(jnp.float32)).astype(z.dtype)


def hybrid_mixer(h, rel_bias, w_in, w_out, cmp_pos, cmp_w, nsa_norm, sb_norm,
                 conv_w, conv_b, dt_bias, a_log, d_skip, ssd_norm):
    B_, T, _ = h.shape
    cuts = np.cumsum(IN_COLS)[:-1].tolist()
    nsa_q, nsa_kv, nsa_gate, sb_qkv, ssd_z, ssd_xbc, ssd_dt = jnp.split(h @ w_in, cuts, axis=-1)
    o_nsa = native_sparse_attention(nsa_q.reshape(B_, T, NSA_HEADS, HEAD_DIM),
                                    nsa_kv.reshape(B_, T, 3, 2, NSA_KV_HEADS, HEAD_DIM),
                                    nsa_gate.reshape(B_, T, NSA_HEADS, 3), cmp_pos, cmp_w, rel_bias)
    sb = sb_qkv.reshape(B_, T, 3, SB_HEADS, HEAD_DIM)
    o_sb = stick_breaking_attention(sb[:, :, 0], sb[:, :, 1], sb[:, :, 2])
    o_ssd = mamba2_ssd(ssd_z, ssd_xbc, ssd_dt, conv_w, conv_b, dt_bias, a_log, d_skip, ssd_norm)
    mixed = jnp.concatenate([rmsnorm(o_nsa, nsa_norm), rmsnorm(o_sb, sb_norm), o_ssd], axis=-1)
    return mixed @ w_out


def setup_inputs(seed: int = 0) -> dict:
    key = jax.random.key(seed)
    ks = iter(jax.random.split(key, 32))

    def nrm(shape, scale):
        return jax.random.normal(next(ks), shape, jnp.float32) * scale

    def gain(shape):
        return 1.0 + 0.01 * jax.random.normal(next(ks), shape, jnp.float32)

    dt_init = jnp.exp(math.log(1e-3) + jax.random.uniform(next(ks), (DEPTH, SSD_HEADS), jnp.float32)
                      * (math.log(1e-1) - math.log(1e-3)))
    return {
        'x': nrm((BATCH, SEQ, D_MODEL), 1.0),
        'rel_bias': nrm((REL_BUCKETS, NSA_HEADS), 0.5),
        'ffn1_norm': gain((DEPTH, D_MODEL)),
        'ffn1_w_gate': nrm((DEPTH, D_MODEL, D_FF), D_MODEL ** -0.5),
        'ffn1_w_up': nrm((DEPTH, D_MODEL, D_FF), D_MODEL ** -0.5),
        'ffn1_w_down': nrm((DEPTH, D_FF, D_MODEL), D_FF ** -0.5),
        'mix_norm': gain((DEPTH, D_MODEL)),
        'w_in': nrm((DEPTH, D_MODEL, D_IN_PROJ), D_MODEL ** -0.5),
        'w_out': nrm((DEPTH, D_MIX, D_MODEL), D_MIX ** -0.5),
        'nsa_cmp_pos': nrm((DEPTH, 2, CMP_BLOCK, HEAD_DIM), 0.1),
        'nsa_cmp_w': nrm((DEPTH, 2, CMP_BLOCK * HEAD_DIM, HEAD_DIM), (CMP_BLOCK * HEAD_DIM) ** -0.5),
        'nsa_out_norm': gain((DEPTH, NSA_WIDTH)),
        'sb_out_norm': gain((DEPTH, SB_WIDTH)),
        'ssd_conv_w': nrm((DEPTH, SSD_CONV, SSD_CONV_DIM), SSD_CONV ** -0.5),
        'ssd_conv_b': nrm((DEPTH, SSD_CONV_DIM), 0.01),
        'ssd_dt_bias': dt_init + jnp.log(-jnp.expm1(-dt_init)),
        'ssd_a_log': jnp.log(jax.random.uniform(next(ks), (DEPTH, SSD_HEADS), jnp.float32, 1.0, 16.0)),
        'ssd_d': gain((DEPTH, SSD_HEADS)),
        'ssd_out_norm': gain((DEPTH, SSD_INNER)),
        'ffn2_norm': gain((DEPTH, D_MODEL)),
        'ffn2_w_gate': nrm((DEPTH, D_MODEL, D_FF), D_MODEL ** -0.5),
        'ffn2_w_up': nrm((DEPTH, D_MODEL, D_FF), D_MODEL ** -0.5),
        'ffn2_w_down': nrm((DEPTH, D_FF, D_MODEL), D_FF ** -0.5),
        'final_norm': gain((D_MODEL,)),
    }


def reference(x, rel_bias, ffn1_norm, ffn1_w_gate, ffn1_w_up, ffn1_w_down, mix_norm, w_in, w_out,
              nsa_cmp_pos, nsa_cmp_w, nsa_out_norm, sb_out_norm, ssd_conv_w, ssd_conv_b, ssd_dt_bias,
              ssd_a_log, ssd_d, ssd_out_norm, ffn2_norm, ffn2_w_gate, ffn2_w_up, ffn2_w_down, final_norm):
    for l in range(DEPTH):
        x = x + 0.5 * swiglu(rmsnorm(x, ffn1_norm[l]), ffn1_w_gate[l], ffn1_w_up[l], ffn1_w_down[l])
        x = x + hybrid_mixer(rmsnorm(x, mix_norm[l]), rel_bias, w_in[l], w_out[l], nsa_cmp_pos[l], nsa_cmp_w[l],
                             nsa_out_norm[l], sb_out_norm[l], ssd_conv_w[l], ssd_conv_b[l], ssd_dt_bias[l],
                             ssd_a_log[l], ssd_d[l], ssd_out_norm[l])
        x = x + 0.5 * swiglu(rmsnorm(x, ffn2_norm[l]), ffn2_w_gate[l], ffn2_w_up[l], ffn2_w_down[l])
    return rmsnorm(x, final_norm)
```

```python
import functools
import math

import jax
import jax.numpy as jnp
from jax import lax
from jax.experimental import pallas as pl
from jax.experimental.pallas import tpu as pltpu

D_MODEL = 2048
D_FF = 5632
HEAD_DIM = 64
QB = 128
NEG_INF = -1e30
EPS = 1e-6

NSA_HEADS = 8
NSA_KV_HEADS = 2
NSA_GQA = NSA_HEADS // NSA_KV_HEADS
NSA_WIDTH = NSA_HEADS * HEAD_DIM
CMP_BLOCK = 32
CMP_STRIDE = 16
SEL_BLOCK = 64
SEL_TOPK = 8
FORCE_SCORE = 1e3
WINDOW = 512
SB_HEADS = 8
SB_WIDTH = SB_HEADS * HEAD_DIM
SSD_HEADS = 16
SSD_INNER = SSD_HEADS * HEAD_DIM
SSD_GROUPS = 2
SSD_HPG = SSD_HEADS // SSD_GROUPS
SSD_STATE = 128
SSD_CONV = 4
REL_BUCKETS = 32
REL_MAX_DIST = 128

PB_NSA_Q, PB_SEL, PB_WIN, PB_SB_Q, PB_SB_K, PB_SB_V, PB_COLS = 0, 512, 768, 1024, 1536, 2048, 2560
PF_Z, PF_XS, PF_BC, PF_CMP, PF_MISC, PF_COLS = 0, 1024, 2048, 2560, 2816, 3072
MISC_DT = 16

VMEM_LIMIT = 52 * 1024 * 1024

BF16 = jnp.bfloat16
F32 = jnp.float32


def _dot(a, b):
    return jnp.dot(a, b, preferred_element_type=F32)


def _dot_nt(a, b):
    return lax.dot_general(a, b, (((1,), (1,)), ((), ())), preferred_element_type=F32)


def _split2_dot(x, m):
    hi = x.astype(BF16)
    lo = (x - hi.astype(F32)).astype(BF16)
    return _dot(hi, m) + _dot(lo, m)


def _split3_dot_left(m, x):
    hi = x.astype(BF16)
    r1 = x - hi.astype(F32)
    mid = r1.astype(BF16)
    lo = (r1 - mid.astype(F32)).astype(BF16)
    return _dot(m, hi) + _dot(m, mid) + _dot(m, lo)


def _rms(x, g):
    return x * lax.rsqrt(jnp.mean(x * x, axis=-1, keepdims=True) + EPS) * g


def _silu(x):
    return x / (1.0 + jnp.exp(-x))


def _sigmoid(x):
    return 1.0 / (1.0 + jnp.exp(-x))


def _params(sem):
    return pltpu.CompilerParams(dimension_semantics=sem, vmem_limit_bytes=VMEM_LIMIT)


def _ffn_kernel(x_ref, g_ref, wg_ref, wu_ref, wd_ref, fg_ref, o_ref, h_ref, acc_ref, *, final):
    f = pl.program_id(1)

    @pl.when(f == 0)
    def _():
        h_ref[...] = _rms(x_ref[...], g_ref[...]).astype(BF16)
        acc_ref[...] = jnp.zeros_like(acc_ref)

    h = h_ref[...]
    gate = _dot(h, wg_ref[...])
    up = _dot(h, wu_ref[...])
    acc_ref[...] += _dot((_silu(gate) * up).astype(BF16), wd_ref[...])

    @pl.when(f == pl.num_programs(1) - 1)
    def _():
        y = x_ref[...] + 0.5 * acc_ref[...]
        if final:
            y = _rms(y, fg_ref[...])
        o_ref[...] = y


def _ffn(x, g, wg, wu, wd, final_g, *, final, tm=512, tf=512):
    n, d = x.shape
    dff = wg.shape[1]
    return pl.pallas_call(
        functools.partial(_ffn_kernel, final=final),
        grid=(n // tm, dff // tf),
        in_specs=[
            pl.BlockSpec((tm, d), lambda i, f: (i, 0)),
            pl.BlockSpec((1, d), lambda i, f: (0, 0)),
            pl.BlockSpec((d, tf), lambda i, f: (0, f)),
            pl.BlockSpec((d, tf), lambda i, f: (0, f)),
            pl.BlockSpec((tf, d), lambda i, f: (f, 0)),
            pl.BlockSpec((1, d), lambda i, f: (0, 0)),
        ],
        out_specs=pl.BlockSpec((tm, d), lambda i, f: (i, 0)),
        out_shape=jax.ShapeDtypeStruct((n, d), F32),
        scratch_shapes=[pltpu.VMEM((tm, d), BF16), pltpu.VMEM((tm, d), F32)],
        compiler_params=_params(("parallel", "arbitrary")),
        name="ffn",
    )(x, g.reshape(1, d), wg, wu, wd, final_g.reshape(1, d))


def _rms_matmul_kernel(x_ref, g_ref, w_ref, o_ref, h_ref):
    @pl.when(pl.program_id(1) == 0)
    def _():
        h_ref[...] = _rms(x_ref[...], g_ref[...]).astype(BF16)

    o_ref[...] = _dot(h_ref[...], w_ref[...]).astype(o_ref.dtype)


def _rms_matmul(x, g, w, out_dtype, *, tm=1024, tn=512, name):
    n, d = x.shape
    c = w.shape[1]
    return pl.pallas_call(
        _rms_matmul_kernel,
        grid=(n // tm, c // tn),
        in_specs=[
            pl.BlockSpec((tm, d), lambda i, j: (i, 0)),
            pl.BlockSpec((1, d), lambda i, j: (0, 0)),
            pl.BlockSpec((d, tn), lambda i, j: (0, j)),
        ],
        out_specs=pl.BlockSpec((tm, tn), lambda i, j: (i, j)),
        out_shape=jax.ShapeDtypeStruct((n, c), out_dtype),
        scratch_shapes=[pltpu.VMEM((tm, d), BF16)],
        compiler_params=_params(("parallel", "arbitrary")),
        name=name,
    )(x, g.reshape(1, d), w)


def _sb_kernel(q_ref, k_ref, v_ref, o_ref):
    i = pl.program_id(2)
    r = lax.broadcasted_iota(jnp.int32, (QB, QB), 0)
    c = lax.broadcasted_iota(jnp.int32, (QB, QB), 1)
    later = (r > c).astype(BF16)
    for hh in range(2):
        lanes = slice(hh * HEAD_DIM, (hh + 1) * HEAD_DIM)
        q = q_ref[:, lanes]

        def body(step, carry):
            tail, acc = carry
            j = i - step
            rows = pl.ds(pl.multiple_of(j * QB, QB), QB)
            z = _dot_nt(q, k_ref[rows, lanes])
            mask = (j * QB + c) < (i * QB + r)
            soft = jnp.log1p(jnp.exp(-jnp.abs(z)))
            log_not = jnp.where(mask, -(jnp.maximum(z, 0.0) + soft), 0.0)
            log_beta = jnp.minimum(z, 0.0) - soft
            between = tail + _split2_dot(log_not, later)
            a = jnp.where(mask, jnp.exp(log_beta + between), 0.0)
            acc = acc + _dot(a.astype(BF16), v_ref[rows, lanes])
            tail = tail + jnp.sum(log_not, axis=-1, keepdims=True)
            return tail, acc

        _, acc = lax.fori_loop(0, i + 1, body,
                               (jnp.zeros((QB, 1), F32), jnp.zeros((QB, HEAD_DIM), F32)))
        o_ref[:, lanes] = acc


def _sb_attention(pb, batch, seq):
    nqb = seq // QB
    qc, kc, vc = PB_SB_Q // 128, PB_SB_K // 128, PB_SB_V // 128
    return pl.pallas_call(
        _sb_kernel,
        grid=(batch, SB_HEADS // 2, nqb),
        in_specs=[
            pl.BlockSpec((QB, 128), lambda b, p, i: (b * nqb + i, qc + p)),
            pl.BlockSpec((seq, 128), lambda b, p, i: (b, kc + p)),
            pl.BlockSpec((seq, 128), lambda b, p, i: (b, vc + p)),
        ],
        out_specs=pl.BlockSpec((QB, 128), lambda b, p, i: (b * nqb + i, p)),
        out_shape=jax.ShapeDtypeStruct((batch * seq, SB_WIDTH), F32),
        compiler_params=_params(("parallel", "parallel", "arbitrary")),
        name="sb_attention",
    )(pb, pb, pb)


def _nsa_cmp_kernel(q_ref, uk_ref, uv_ref, pos_ref, w_ref, bias_ref, ov_ref, ocmp_ref, sel_ref,
                    kc_ref, vc_ref, *, n_cmp, n_sel):
    i = pl.program_id(2)
    half = CMP_STRIDE * HEAD_DIM

    @pl.when(i == 0)
    def _():
        for kv, (u_ref, dst) in enumerate(((uk_ref, kc_ref), (uv_ref, vc_ref))):
            u = u_ref[0, 0]
            top = _dot((u + pos_ref[kv, 0:1, :]).astype(BF16), w_ref[kv, 0:half, :])
            bot = _dot((u + pos_ref[kv, 1:2, :]).astype(BF16), w_ref[kv, half:2 * half, :])
            dst[...] = (top + pltpu.roll(bot, QB - 1, 0)).astype(BF16)

    rows = i * QB + lax.broadcasted_iota(jnp.int32, (QB, QB), 0)
    cols = lax.broadcasted_iota(jnp.int32, (QB, QB), 1)
    valid = (rows >= cols * CMP_STRIDE + (CMP_BLOCK - 1)) & (cols < n_cmp)
    any_valid = rows >= CMP_BLOCK - 1
    kc = kc_ref[...]
    vc = vc_ref[...]
    p_sum = jnp.zeros((QB, QB), F32)
    for g in range(NSA_GQA):
        lanes = slice(g * HEAD_DIM, (g + 1) * HEAD_DIM)
        s = jnp.where(valid, _dot_nt(q_ref[:, lanes], kc) + bias_ref[g], NEG_INF)
        e = jnp.exp(s - jnp.max(s, axis=-1, keepdims=True))
        p = jnp.where(any_valid, e / jnp.sum(e, axis=-1, keepdims=True), 0.0)
        ocmp_ref[:, lanes] = _dot(p.astype(BF16), vc)
        p_sum = p_sum + p

    p_sel = _split2_dot(p_sum, ov_ref[...])
    cur = rows // SEL_BLOCK
    eligible = (cols * SEL_BLOCK <= rows) & (cols < n_sel)
    forced = (cols == 0) | (cols == cur) | (cols == cur - 1)
    score = jnp.where(eligible, p_sel + jnp.where(forced, FORCE_SCORE, 0.0), NEG_INF)
    rank = jnp.zeros((QB, QB), F32)
    for j in range(n_sel):
        other = score[:, j:j + 1]
        ahead = (other > score) | ((other == score) & (cols > j))
        rank = rank + jnp.where(ahead, 1.0, 0.0)
    sel_ref[0, 0] = jnp.where(eligible & (rank < SEL_TOPK), 1.0, 0.0).astype(BF16)


def _nsa_cmp(pb, ucmp, pos2, cmp_w, bias_c, overlap, batch, seq):
    nqb = seq // QB
    n_cmp = (seq - CMP_BLOCK) // CMP_STRIDE + 1
    n_sel = seq // SEL_BLOCK
    ng = seq // CMP_STRIDE
    assert ng == QB and n_sel <= QB
    wide = CMP_STRIDE * HEAD_DIM
    return pl.pallas_call(
        functools.partial(_nsa_cmp_kernel, n_cmp=n_cmp, n_sel=n_sel),
        grid=(batch, NSA_KV_HEADS, nqb),
        in_specs=[
            pl.BlockSpec((QB, 256), lambda b, h, i: (b * nqb + i, h)),
            pl.BlockSpec((1, 1, ng, wide), lambda b, h, i: (b, h, 0, 0)),
            pl.BlockSpec((1, 1, ng, wide), lambda b, h, i: (b, NSA_KV_HEADS + h, 0, 0)),
            pl.BlockSpec((2, 2, wide), lambda b, h, i: (0, 0, 0)),
            pl.BlockSpec((2, 2 * wide, HEAD_DIM), lambda b, h, i: (0, 0, 0)),
            pl.BlockSpec((NSA_GQA, QB, QB), lambda b, h, i: (h, i, 0)),
            pl.BlockSpec((QB, QB), lambda b, h, i: (0, 0)),
        ],
        out_specs=[
            pl.BlockSpec((QB, 256), lambda b, h, i: (b * nqb + i, h)),
            pl.BlockSpec((1, 1, QB, QB), lambda b, h, i: (b, h, i, 0)),
        ],
        out_shape=[
            jax.ShapeDtypeStruct((batch * seq, NSA_WIDTH), F32),
            jax.ShapeDtypeStruct((batch, NSA_KV_HEADS, seq, QB), BF16),
        ],
        scratch_shapes=[pltpu.VMEM((QB, HEAD_DIM), BF16), pltpu.VMEM((QB, HEAD_DIM), BF16)],
        compiler_params=_params(("parallel", "parallel", "arbitrary")),
        name="nsa_compressed",
    )(pb, ucmp, ucmp, pos2, cmp_w, bias_c, overlap)


def _nsa_sw_kernel(q_ref, ks_ref, kw_ref, ocmp_ref, misc_ref, sel_ref, tb_ref, far_ref, o_ref,
                   m_ref, l_ref, acc_ref):
    i = pl.program_id(2)
    r = lax.broadcasted_iota(jnp.int32, (QB, QB), 0)
    c = lax.broadcasted_iota(jnp.int32, (QB, QB), 1)
    causal = c <= r
    G = NSA_GQA
    qs = [q_ref[:, g * HEAD_DIM:(g + 1) * HEAD_DIM] for g in range(G)]
    sel = sel_ref[0, 0]

    m_ref[...] = jnp.full(m_ref.shape, NEG_INF, F32)
    l_ref[...] = jnp.zeros_like(l_ref)
    acc_ref[...] = jnp.zeros_like(acc_ref)

    def step(base, kv_ref, j, bias, mask):
        rows = pl.ds(pl.multiple_of(j * QB, QB), QB)
        kt = kv_ref[rows, 0:HEAD_DIM]
        vt = kv_ref[rows, HEAD_DIM:2 * HEAD_DIM]
        for g in range(G):
            s = _dot_nt(qs[g], kt) + bias(g)
            if mask is not None:
                s = jnp.where(mask, s, NEG_INF)
            m_old = m_ref[base + g]
            m_new = jnp.maximum(m_old, jnp.max(s, axis=-1, keepdims=True))
            p = jnp.exp(s - m_new)
            if mask is not None:
                p = jnp.where(mask, p, 0.0)
            alpha = jnp.exp(m_old - m_new)
            l_ref[base + g] = alpha * l_ref[base + g] + jnp.sum(p, axis=-1, keepdims=True)
            acc_ref[base + g] = alpha * acc_ref[base + g] + _dot(p.astype(BF16), vt)
            m_ref[base + g] = m_new

    def far_bias(g):
        return far_ref[0, g:g + 1, :]

    def sel_mask(j):
        expand = (r == 2 * j + c // SEL_BLOCK).astype(BF16)
        return _dot(sel, expand) > 0.5

    def sel_far(j, carry):
        step(0, ks_ref, j, far_bias, sel_mask(j))
        return carry

    lax.fori_loop(0, jnp.maximum(i - 1, 0), sel_far, 0)

    @pl.when(i >= 1)
    def _():
        step(0, ks_ref, i - 1, lambda g: tb_ref[1, g], sel_mask(i - 1))

    step(0, ks_ref, i, lambda g: tb_ref[0, g], sel_mask(i) & causal)

    nwb = WINDOW // QB

    @pl.when(i >= nwb)
    def _():
        step(G, kw_ref, i - nwb, far_bias, c > r)

    def win_far(j, carry):
        step(G, kw_ref, j, far_bias, None)
        return carry

    lax.fori_loop(jnp.maximum(i - nwb + 1, 0), jnp.maximum(i - 1, 0), win_far, 0)

    @pl.when(i >= 1)
    def _():
        step(G, kw_ref, i - 1, lambda g: tb_ref[1, g], None)

    step(G, kw_ref, i, lambda g: tb_ref[0, g], causal)

    misc = misc_ref[...]
    for g in range(G):
        lanes = slice(g * HEAD_DIM, (g + 1) * HEAD_DIM)
        gate = _sigmoid(misc[:, 3 * g:3 * g + 3])
        o_sel = acc_ref[g] / l_ref[g]
        o_win = acc_ref[G + g] / l_ref[G + g]
        o_ref[:, lanes] = gate[:, 0:1] * ocmp_ref[:, lanes] + gate[:, 1:2] * o_sel + gate[:, 2:3] * o_win


def _nsa_sel_win(pb, pf, o_cmp, sel, tb, far, batch, seq):
    nqb = seq // QB
    G = NSA_GQA
    return pl.pallas_call(
        _nsa_sw_kernel,
        grid=(batch, NSA_KV_HEADS, nqb),
        in_specs=[
            pl.BlockSpec((QB, 256), lambda b, h, i: (b * nqb + i, h)),
            pl.BlockSpec((seq, 128), lambda b, h, i: (b, PB_SEL // 128 + h)),
            pl.BlockSpec((seq, 128), lambda b, h, i: (b, PB_WIN // 128 + h)),
            pl.BlockSpec((QB, 256), lambda b, h, i: (b * nqb + i, h)),
            pl.BlockSpec((QB, 128), lambda b, h, i: (b * nqb + i, PF_MISC // 128 + h)),
            pl.BlockSpec((1, 1, QB, QB), lambda b, h, i: (b, h, i, 0)),
            pl.BlockSpec((2, G, QB, QB), lambda b, h, i: (0, h, 0, 0)),
            pl.BlockSpec((1, G, QB), lambda b, h, i: (h, 0, 0)),
        ],
        out_specs=pl.BlockSpec((QB, 256), lambda b, h, i: (b * nqb + i, h)),
        out_shape=jax.ShapeDtypeStruct((batch * seq, NSA_WIDTH), F32),
        scratch_shapes=[
            pltpu.VMEM((2 * G, QB, 1), F32),
            pltpu.VMEM((2 * G, QB, 1), F32),
            pltpu.VMEM((2 * G, QB, HEAD_DIM), F32),
        ],
        compiler_params=_params(("parallel", "parallel", "arbitrary")),
        name="nsa_selected_window",
    )(pb, pb, pb, o_cmp, pf, sel, tb, far)


def _ssd_kernel(z_ref, xs_ref, bc_ref, misc_ref, cwx_ref, cwb_ref, cbx_ref, cbb_ref, hp_ref, ng_ref,
                o_ref, xbuf, bbuf, state, ybuf):
    ci = pl.program_id(1)
    L = QB
    P = HEAD_DIM
    GN = SSD_GROUPS * SSD_STATE

    @pl.when(ci == 0)
    def _():
        xbuf[0:8, :] = jnp.zeros((8, SSD_INNER), F32)
        bbuf[0:8, :] = jnp.zeros((8, 2 * GN), F32)
        state[...] = jnp.zeros_like(state)

    def conv_silu(buf, src_ref, w_ref, b_ref):
        buf[8:8 + L, :] = src_ref[...]
        out = b_ref[...]
        for k in range(SSD_CONV):
            lo = 8 - (SSD_CONV - 1) + k
            out = out + w_ref[k:k + 1, :] * buf[lo:lo + L, :]
        buf[0:8, :] = buf[L:L + 8, :]
        return _silu(out)

    xs = conv_silu(xbuf, xs_ref, cwx_ref, cbx_ref)
    bcs = conv_silu(bbuf, bc_ref, cwb_ref, cbb_ref)

    misc = misc_ref[...]
    pre = misc + hp_ref[0:1, :]
    dt = jnp.maximum(pre, 0.0) + jnp.log1p(jnp.exp(-jnp.abs(pre)))
    a_dt = dt * (-jnp.exp(hp_ref[1:2, :]))
    r = lax.broadcasted_iota(jnp.int32, (L, L), 0)
    c = lax.broadcasted_iota(jnp.int32, (L, L), 1)
    causal = r >= c
    a_cs = _split3_dot_left(causal.astype(BF16), a_dt)
    a_cs_t = a_cs.T
    d_skip = hp_ref[2:3, :]

    for g in range(SSD_GROUPS):
        bm = bcs[:, g * SSD_STATE:(g + 1) * SSD_STATE]
        cm = bcs[:, GN + g * SSD_STATE:GN + (g + 1) * SSD_STATE]
        cm16 = cm.astype(BF16)
        cb = _dot_nt(cm16, bm.astype(BF16))
        bm_t16 = bm.T.astype(BF16)
        for e in range(SSD_HPG):
            hd = g * SSD_HPG + e
            lanes = slice(hd * P, (hd + 1) * P)
            col = MISC_DT + hd
            x = xs[:, lanes]
            xdt = x * dt[:, col:col + 1]
            acs = a_cs[:, col:col + 1]
            seg = jnp.exp(jnp.where(causal, acs - a_cs_t[col:col + 1, :], -jnp.inf))
            y = _dot((cb * seg).astype(BF16), xdt.astype(BF16))
            h_in = state[hd]
            y = y + jnp.exp(acs) * _dot(cm16, h_in.astype(BF16))
            total = a_cs[L - 1:L, col:col + 1]
            to_end = jnp.exp(total - acs)
            state[hd] = h_in * jnp.exp(total) + _dot(bm_t16, (xdt * to_end).astype(BF16))
            y = y + d_skip[:, col:col + 1] * x
            ybuf[:, lanes] = y * _silu(z_ref[:, lanes])

    width = SSD_INNER // SSD_GROUPS
    for g in range(SSD_GROUPS):
        lanes = slice(g * width, (g + 1) * width)
        y = ybuf[:, lanes]
        o_ref[:, lanes] = y * lax.rsqrt(jnp.mean(y * y, axis=-1, keepdims=True) + EPS) * ng_ref[:, lanes]


def _ssd(pf, conv_w, conv_b, head_params, norm_g, batch, seq):
    nc = seq // QB
    GN = SSD_GROUPS * SSD_STATE
    cwx, cwb = conv_w[:, :SSD_INNER], conv_w[:, SSD_INNER:]
    cbx, cbb = conv_b[:SSD_INNER].reshape(1, -1), conv_b[SSD_INNER:].reshape(1, -1)
    full = lambda shape: pl.BlockSpec(shape, lambda b, ci: (0,) * len(shape))
    return pl.pallas_call(
        _ssd_kernel,
        grid=(batch, nc),
        in_specs=[
            pl.BlockSpec((QB, SSD_INNER), lambda b, ci: (b * nc + ci, PF_Z // SSD_INNER)),
            pl.BlockSpec((QB, SSD_INNER), lambda b, ci: (b * nc + ci, PF_XS // SSD_INNER)),
            pl.BlockSpec((QB, 2 * GN), lambda b, ci: (b * nc + ci, PF_BC // (2 * GN))),
            pl.BlockSpec((QB, 128), lambda b, ci: (b * nc + ci, PF_MISC // 128)),
            full((SSD_CONV, SSD_INNER)),
            full((SSD_CONV, 2 * GN)),
            full((1, SSD_INNER)),
            full((1, 2 * GN)),
            full((8, 128)),
            full((1, SSD_INNER)),
        ],
        out_specs=pl.BlockSpec((QB, SSD_INNER), lambda b, ci: (b * nc + ci, 0)),
        out_shape=jax.ShapeDtypeStruct((batch * seq, SSD_INNER), F32),
        scratch_shapes=[
            pltpu.VMEM((QB + 8, SSD_INNER), F32),
            pltpu.VMEM((QB + 8, 2 * GN), F32),
            pltpu.VMEM((SSD_HEADS, SSD_STATE, HEAD_DIM), F32),
            pltpu.VMEM((QB, SSD_INNER), F32),
        ],
        compiler_params=_params(("parallel", "arbitrary")),
        name="ssd",
    )(pf, pf, pf, pf, cwx, cwb, cbx, cbb, head_params, norm_g.reshape(1, -1))


def _out_proj_kernel(x_ref, nsa_ref, sb_ref, ssd_ref, gn_ref, gs_ref, w_ref, o_ref, mix_ref):
    @pl.when(pl.program_id(1) == 0)
    def _():
        mix_ref[:, 0:NSA_WIDTH] = _rms(nsa_ref[...], gn_ref[...]).astype(BF16)
        mix_ref[:, NSA_WIDTH:NSA_WIDTH + SB_WIDTH] = _rms(sb_ref[...], gs_ref[...]).astype(BF16)
        mix_ref[:, NSA_WIDTH + SB_WIDTH:] = ssd_ref[...].astype(BF16)

    o_ref[...] = x_ref[...] + _dot(mix_ref[...], w_ref[...])


def _out_proj(x, o_nsa, o_sb, o_ssd, g_nsa, g_sb, w, *, tm=1024, tn=512):
    n, d = x.shape
    dm = w.shape[0]
    return pl.pallas_call(
        _out_proj_kernel,
        grid=(n // tm, d // tn),
        in_specs=[
            pl.BlockSpec((tm, tn), lambda i, j: (i, j)),
            pl.BlockSpec((tm, NSA_WIDTH), lambda i, j: (i, 0)),
            pl.BlockSpec((tm, SB_WIDTH), lambda i, j: (i, 0)),
            pl.BlockSpec((tm, SSD_INNER), lambda i, j: (i, 0)),
            pl.BlockSpec((1, NSA_WIDTH), lambda i, j: (0, 0)),
            pl.BlockSpec((1, SB_WIDTH), lambda i, j: (0, 0)),
            pl.BlockSpec((dm, tn), lambda i, j: (0, j)),
        ],
        out_specs=pl.BlockSpec((tm, tn), lambda i, j: (i, j)),
        out_shape=jax.ShapeDtypeStruct((n, d), F32),
        scratch_shapes=[pltpu.VMEM((tm, dm), BF16)],
        compiler_params=_params(("parallel", "arbitrary")),
        name="out_proj",
    )(x, o_nsa, o_sb, o_ssd, g_nsa.reshape(1, -1), g_sb.reshape(1, -1), w)


def _rel_bucket(dist):
    dist = jnp.maximum(dist, 0)
    max_exact = REL_BUCKETS // 2
    log_ratio = jnp.log(jnp.maximum(dist, 1).astype(F32) / max_exact) / math.log(REL_MAX_DIST / max_exact)
    large = jnp.minimum(max_exact + (log_ratio * (REL_BUCKETS - max_exact)).astype(jnp.int32), REL_BUCKETS - 1)
    return jnp.where(dist < max_exact, dist, large)


def _bias_tables(rel_bias, seq):
    assert QB >= REL_MAX_DIST
    tab = rel_bias.T
    t = jnp.arange(seq)[:, None]
    cend = jnp.arange(QB)[None, :] * CMP_STRIDE + CMP_BLOCK - 1
    bias_c = tab[:, _rel_bucket(t - cend)]
    r = jnp.arange(QB)[:, None]
    m = jnp.arange(QB)[None, :]
    near = jnp.stack([tab[:, _rel_bucket(r - m)], tab[:, _rel_bucket(QB + r - m)]])
    far = jnp.broadcast_to(tab[:, REL_BUCKETS - 1].reshape(NSA_KV_HEADS, NSA_GQA, 1),
                           (NSA_KV_HEADS, NSA_GQA, QB))
    return bias_c, near, far


def _overlap_table(seq):
    n_cmp = (seq - CMP_BLOCK) // CMP_STRIDE + 1
    cs = jnp.arange(QB)[:, None] * CMP_STRIDE
    ce = cs + CMP_BLOCK - 1
    ss = jnp.arange(QB)[None, :] * SEL_BLOCK
    ov = jnp.maximum(jnp.minimum(ce, ss + SEL_BLOCK - 1) - jnp.maximum(cs, ss) + 1, 0).astype(F32) / CMP_BLOCK
    keep = (jnp.arange(QB)[:, None] < n_cmp) & (jnp.arange(QB)[None, :] < seq // SEL_BLOCK)
    return jnp.where(keep, ov, 0.0).astype(BF16)


def _in_proj_weights(w):
    scale = HEAD_DIM ** -0.5
    o_q, o_kv, o_gate = 0, NSA_WIDTH, NSA_WIDTH + 768
    o_sb = o_gate + 3 * NSA_HEADS
    o_z = o_sb + 3 * SB_WIDTH
    o_xbc = o_z + SSD_INNER
    o_dt = o_xbc + SSD_INNER + 2 * SSD_GROUPS * SSD_STATE
    col = lambda a, n: w[:, a:a + n]
    kv = lambda br, which, h: col(o_kv + br * 256 + which * 128 + h * HEAD_DIM, HEAD_DIM)
    zeros = lambda n: jnp.zeros((w.shape[0], n), w.dtype)
    branch = lambda br: [kv(br, 0, 0), kv(br, 1, 0), kv(br, 0, 1), kv(br, 1, 1)]
    wb = jnp.concatenate(
        [col(o_q, NSA_WIDTH) * scale] + branch(1) + branch(2)
        + [col(o_sb, SB_WIDTH) * scale, col(o_sb + SB_WIDTH, 2 * SB_WIDTH)], axis=1)
    gates = 3 * NSA_GQA
    wf = jnp.concatenate(
        [col(o_z, SSD_INNER), col(o_xbc, SSD_INNER + 2 * SSD_GROUPS * SSD_STATE), col(o_kv, 256),
         col(o_gate, gates), zeros(MISC_DT - gates), col(o_dt, SSD_HEADS), zeros(128 - MISC_DT - SSD_HEADS),
         col(o_gate + gates, gates), zeros(128 - gates)], axis=1)
    assert wb.shape[1] == PB_COLS and wf.shape[1] == PF_COLS
    return wb.astype(BF16), wf.astype(BF16)


def _head_params(dt_bias, a_log, d_skip):
    rows = jnp.stack([dt_bias, a_log, d_skip]).astype(F32)
    return jnp.zeros((8, 128), F32).at[0:3, MISC_DT:MISC_DT + SSD_HEADS].set(rows)


def _mixer(x, l, tables, batch, seq, mix_norm, w_in, w_out, cmp_pos, cmp_w, nsa_norm, sb_norm,
           conv_w, conv_b, dt_bias, a_log, d_skip, ssd_norm):
    bias_c, near, far, overlap = tables
    wb, wf = _in_proj_weights(w_in[l])
    pb = _rms_matmul(x, mix_norm[l], wb, BF16, name="in_proj_bf16")
    pf = _rms_matmul(x, mix_norm[l], wf, F32, name="in_proj_f32")

    ng = seq // CMP_STRIDE
    ucmp = pf[:, PF_CMP:PF_CMP + 256].reshape(batch, ng, CMP_STRIDE, 4, HEAD_DIM)
    ucmp = ucmp.transpose(0, 3, 1, 2, 4).reshape(batch, 4, ng, CMP_STRIDE * HEAD_DIM)
    pos2 = cmp_pos[l].reshape(2, 2, CMP_STRIDE * HEAD_DIM)
    o_cmp, sel = _nsa_cmp(pb, ucmp, pos2, cmp_w[l].astype(BF16), bias_c, overlap, batch, seq)
    o_nsa = _nsa_sel_win(pb, pf, o_cmp, sel, near, far, batch, seq)
    o_sb = _sb_attention(pb, batch, seq)
    o_ssd = _ssd(pf, conv_w[l], conv_b[l], _head_params(dt_bias[l], a_log[l], d_skip[l]), ssd_norm[l],
                 batch, seq)
    return _out_proj(x, o_nsa, o_sb, o_ssd, nsa_norm[l], sb_norm[l], w_out[l].astype(BF16))


def kernel(x, rel_bias, ffn1_norm, ffn1_w_gate, ffn1_w_up, ffn1_w_down, mix_norm, w_in, w_out, nsa_cmp_pos, nsa_cmp_w, nsa_out_norm, sb_out_norm, ssd_conv_w, ssd_conv_b, ssd_dt_bias, ssd_a_log, ssd_d, ssd_out_norm, ffn2_norm, ffn2_w_gate, ffn2_w_up, ffn2_w_down, final_norm):
    batch, seq, d = x.shape
    depth = w_in.shape[0]
    tables = _bias_tables(rel_bias, seq) + (_overlap_table(seq),)
    h = x.reshape(batch * seq, d)
    for l in range(depth):
        h = _ffn(h, ffn1_norm[l], ffn1_w_gate[l].astype(BF16), ffn1_w_up[l].astype(BF16),
                 ffn1_w_down[l].astype(BF16), final_norm, final=False)
        h = _mixer(h, l, tables, batch, seq, mix_norm, w_in, w_out, nsa_cmp_pos, nsa_cmp_w, nsa_out_norm,
                   sb_out_norm, ssd_conv_w, ssd_conv_b, ssd_dt_bias, ssd_a_log, ssd_d, ssd_out_norm)
        h = _ffn(h, ffn2_norm[l], ffn2_w_gate[l].astype(BF16), ffn2_w_up[l].astype(BF16),
                 ffn2_w_down[l].astype(BF16), final_norm, final=(l == depth - 1))
    return h.reshape(batch, seq, d)
```

```python
import functools
import math

import jax
import jax.numpy as jnp
from jax import lax
from jax.experimental import pallas as pl
from jax.experimental.pallas import tpu as pltpu

D_MODEL = 2048
D_FF = 5632
HEAD_DIM = 64
QB = 128
NEG_INF = -1e30
EPS = 1e-6

NSA_HEADS = 8
NSA_KV_HEADS = 2
NSA_GQA = NSA_HEADS // NSA_KV_HEADS
NSA_WIDTH = NSA_HEADS * HEAD_DIM
CMP_BLOCK = 32
CMP_STRIDE = 16
SEL_BLOCK = 64
SEL_TOPK = 8
FORCE_SCORE = 1e3
WINDOW = 512
SB_HEADS = 8
SB_WIDTH = SB_HEADS * HEAD_DIM
SSD_HEADS = 16
SSD_INNER = SSD_HEADS * HEAD_DIM
SSD_GROUPS = 2
SSD_HPG = SSD_HEADS // SSD_GROUPS
SSD_STATE = 128
SSD_CONV = 4
REL_BUCKETS = 32
REL_MAX_DIST = 128

PB_NSA_Q, PB_SEL, PB_WIN, PB_SB_Q, PB_SB_K, PB_SB_V, PB_COLS = 0, 1024, 1280, 1536, 2048, 2560, 3072
PF_Z, PF_XS, PF_BC, PF_CMP, PF_MISC, PF_COLS = 0, 1024, 2048, 2560, 2816, 3072
MISC_DT = 16

VMEM_LIMIT = 52 * 1024 * 1024

BF16 = jnp.bfloat16
F32 = jnp.float32


def _dot(a, b):
    return jnp.dot(a, b, preferred_element_type=F32)


def _dot_nt(a, b):
    return lax.dot_general(a, b, (((1,), (1,)), ((), ())), preferred_element_type=F32)


def _split2_dot(x, m):
    hi = x.astype(BF16)
    lo = (x - hi.astype(F32)).astype(BF16)
    return _dot(hi, m) + _dot(lo, m)


def _split3_dot_left(m, x):
    hi = x.astype(BF16)
    r1 = x - hi.astype(F32)
    mid = r1.astype(BF16)
    lo = (r1 - mid.astype(F32)).astype(BF16)
    return _dot(m, hi) + _dot(m, mid) + _dot(m, lo)


def _rms(x, g):
    return x * lax.rsqrt(jnp.mean(x * x, axis=-1, keepdims=True) + EPS) * g


def _silu(x):
    return x / (1.0 + jnp.exp(-x))


def _sigmoid(x):
    return 1.0 / (1.0 + jnp.exp(-x))


def _params(sem):
    return pltpu.CompilerParams(dimension_semantics=sem, vmem_limit_bytes=VMEM_LIMIT)


def _ffn_kernel(x_ref, g_ref, wg_ref, wu_ref, wd_ref, fg_ref, o_ref, h_ref, acc_ref, *, final):
    f = pl.program_id(1)

    @pl.when(f == 0)
    def _():
        h_ref[...] = _rms(x_ref[...], g_ref[...]).astype(BF16)
        acc_ref[...] = jnp.zeros_like(acc_ref)

    h = h_ref[...]
    gate = _dot(h, wg_ref[...])
    up = _dot(h, wu_ref[...])
    acc_ref[...] += _dot((_silu(gate) * up).astype(BF16), wd_ref[...])

    @pl.when(f == pl.num_programs(1) - 1)
    def _():
        y = x_ref[...] + 0.5 * acc_ref[...]
        if final:
            y = _rms(y, fg_ref[...])
        o_ref[...] = y


def _ffn(x, g, wg, wu, wd, final_g, *, final, tm=512, tf=512):
    n, d = x.shape
    dff = wg.shape[1]
    return pl.pallas_call(
        functools.partial(_ffn_kernel, final=final),
        grid=(n // tm, dff // tf),
        in_specs=[
            pl.BlockSpec((tm, d), lambda i, f: (i, 0)),
            pl.BlockSpec((1, d), lambda i, f: (0, 0)),
            pl.BlockSpec((d, tf), lambda i, f: (0, f)),
            pl.BlockSpec((d, tf), lambda i, f: (0, f)),
            pl.BlockSpec((tf, d), lambda i, f: (f, 0)),
            pl.BlockSpec((1, d), lambda i, f: (0, 0)),
        ],
        out_specs=pl.BlockSpec((tm, d), lambda i, f: (i, 0)),
        out_shape=jax.ShapeDtypeStruct((n, d), F32),
        scratch_shapes=[pltpu.VMEM((tm, d), BF16), pltpu.VMEM((tm, d), F32)],
        compiler_params=_params(("parallel", "arbitrary")),
        name="ffn",
    )(x, g.reshape(1, d), wg, wu, wd, final_g.reshape(1, d))


def _rms_matmul_kernel(x_ref, g_ref, w_ref, o_ref, h_ref):
    @pl.when(pl.program_id(1) == 0)
    def _():
        h_ref[...] = _rms(x_ref[...], g_ref[...]).astype(BF16)

    o_ref[...] = _dot(h_ref[...], w_ref[...]).astype(o_ref.dtype)


def _rms_matmul(x, g, w, out_dtype, *, tm=1024, tn=512, name):
    n, d = x.shape
    c = w.shape[1]
    return pl.pallas_call(
        _rms_matmul_kernel,
        grid=(n // tm, c // tn),
        in_specs=[
            pl.BlockSpec((tm, d), lambda i, j: (i, 0)),
            pl.BlockSpec((1, d), lambda i, j: (0, 0)),
            pl.BlockSpec((d, tn), lambda i, j: (0, j)),
        ],
        out_specs=pl.BlockSpec((tm, tn), lambda i, j: (i, j)),
        out_shape=jax.ShapeDtypeStruct((n, c), out_dtype),
        scratch_shapes=[pltpu.VMEM((tm, d), BF16)],
        compiler_params=_params(("parallel", "arbitrary")),
        name=name,
    )(x, g.reshape(1, d), w)


SB_GROUP = 8


def _sb_kernel(q_ref, k_ref, v_ref, o_ref):
    i = pl.program_id(2)
    pairs = SB_GROUP // 2
    r = lax.broadcasted_iota(jnp.int32, (QB, QB), 0)
    c = lax.broadcasted_iota(jnp.int32, (QB, QB), 1)
    strict = c < r
    low = c < HEAD_DIM
    later_ones = jnp.concatenate([(r > c).astype(BF16), jnp.ones((QB, QB), BF16)], axis=1)
    later_ones = jnp.concatenate([later_ones, later_ones], axis=0)
    zero16 = jnp.zeros((QB, QB), BF16)

    def block_diag(x):
        return jnp.concatenate([jnp.where(low, x, zero16), jnp.where(low, zero16, x)], axis=0)

    def tile(j, tails, accs, diagonal):
        rows = pl.ds(pl.multiple_of(j * QB, QB), QB)
        zs, softs, parts = [], [], []
        for h in range(SB_GROUP):
            pp = h // 2
            if h % 2 == 0:
                lanes = slice(pp * 128, (pp + 1) * 128)
                z_pair = _dot_nt(q_ref[:, lanes], block_diag(k_ref[rows, lanes]))
            z = z_pair[:, (h % 2) * QB:(h % 2 + 1) * QB]
            soft = jnp.maximum(z, 0.0) + jnp.log(1.0 + jnp.exp(-jnp.abs(z)))
            if diagonal:
                soft = jnp.where(strict, soft, 0.0)
            hi = soft.astype(BF16)
            parts.append(jnp.concatenate([hi, (soft - hi.astype(F32)).astype(BF16)], axis=1))
            zs.append(z)
            softs.append(soft)
        sums = _dot(jnp.concatenate(parts, axis=0), later_ones)
        new_tails, new_accs = [], []
        for pp in range(pairs):
            a_pair = []
            for hh in range(2):
                h = 2 * pp + hh
                sm = sums[h * QB:(h + 1) * QB]
                a = jnp.exp((zs[h] - softs[h]) - (tails[h] + sm[:, 0:QB]))
                if diagonal:
                    a = jnp.where(strict, a, 0.0)
                a_pair.append(a.astype(BF16))
                new_tails.append(tails[h] + sm[:, QB:2 * QB])
            v_bd = block_diag(v_ref[rows, pp * 128:(pp + 1) * 128])
            new_accs.append(accs[pp] + _dot(jnp.concatenate(a_pair, axis=1), v_bd))
        return new_tails, new_accs

    zeros = jnp.zeros((QB, QB), F32)
    tails, accs = tile(i, [zeros] * SB_GROUP, [zeros] * pairs, True)

    def body(step, carry):
        t, a = tile(i - 1 - step, list(carry[0]), list(carry[1]), False)
        return tuple(t), tuple(a)

    _, accs = lax.fori_loop(0, i, body, (tuple(tails), tuple(accs)))
    for pp in range(pairs):
        o_ref[:, pp * 128:(pp + 1) * 128] = accs[pp]


def _sb_attention(pb, batch, seq):
    nqb = seq // QB
    w = SB_GROUP * HEAD_DIM
    qc, kc, vc = PB_SB_Q // w, PB_SB_K // w, PB_SB_V // w
    return pl.pallas_call(
        _sb_kernel,
        grid=(batch, SB_HEADS // SB_GROUP, nqb),
        in_specs=[
            pl.BlockSpec((QB, w), lambda b, p, i: (b * nqb + i, qc + p)),
            pl.BlockSpec((seq, w), lambda b, p, i: (b, kc + p)),
            pl.BlockSpec((seq, w), lambda b, p, i: (b, vc + p)),
        ],
        out_specs=pl.BlockSpec((QB, w), lambda b, p, i: (b * nqb + i, p)),
        out_shape=jax.ShapeDtypeStruct((batch * seq, SB_WIDTH), F32),
        compiler_params=_params(("parallel", "parallel", "arbitrary")),
        name="sb_attention",
    )(pb, pb, pb)


def _nsa_cmp_kernel(q_ref, uk_ref, uv_ref, pos_ref, w_ref, bias_ref, ov_ref, ocmp_ref, sel_ref,
                    kc_ref, vc_ref, *, n_cmp, n_sel):
    i = pl.program_id(2)
    half = CMP_STRIDE * HEAD_DIM

    @pl.when(i == 0)
    def _():
        for kv, (u_ref, dst) in enumerate(((uk_ref, kc_ref), (uv_ref, vc_ref))):
            u = u_ref[0, 0]
            top = _dot((u + pos_ref[kv, 0:1, :]).astype(BF16), w_ref[kv, 0:half, :])
            bot = _dot((u + pos_ref[kv, 1:2, :]).astype(BF16), w_ref[kv, half:2 * half, :])
            dst[...] = (top + pltpu.roll(bot, QB - 1, 0)).astype(BF16)

    rows = i * QB + lax.broadcasted_iota(jnp.int32, (QB, QB), 0)
    cols = lax.broadcasted_iota(jnp.int32, (QB, QB), 1)
    valid = (rows >= cols * CMP_STRIDE + (CMP_BLOCK - 1)) & (cols < n_cmp)
    any_valid = rows >= CMP_BLOCK - 1
    kc = kc_ref[...]
    vc = vc_ref[...]
    p_sum = jnp.zeros((QB, QB), F32)
    for g in range(NSA_GQA):
        lanes = slice(g * HEAD_DIM, (g + 1) * HEAD_DIM)
        q = q_ref[:, g * 128:g * 128 + HEAD_DIM]
        s = jnp.where(valid, _dot_nt(q, kc) + bias_ref[g], NEG_INF)
        e = jnp.exp(s - jnp.max(s, axis=-1, keepdims=True))
        p = jnp.where(any_valid, e / jnp.sum(e, axis=-1, keepdims=True), 0.0)
        ocmp_ref[:, lanes] = _dot(p.astype(BF16), vc)
        p_sum = p_sum + p

    p_sel = _split2_dot(p_sum, ov_ref[...])
    cur = rows // SEL_BLOCK
    eligible = (cols * SEL_BLOCK <= rows) & (cols < n_sel)
    forced = (cols == 0) | (cols == cur) | (cols == cur - 1)
    score = jnp.where(eligible, p_sel + jnp.where(forced, FORCE_SCORE, 0.0), NEG_INF)
    rank = jnp.zeros((QB, QB), F32)
    for j in range(n_sel):
        other = score[:, j:j + 1]
        ahead = (other > score) | ((other == score) & (cols > j))
        rank = rank + jnp.where(ahead, 1.0, 0.0)
    sel_ref[0, 0] = jnp.where(eligible & (rank < SEL_TOPK), 1.0, 0.0).astype(BF16)


def _nsa_cmp(pb, ucmp, pos2, cmp_w, bias_c, overlap, batch, seq):
    nqb = seq // QB
    n_cmp = (seq - CMP_BLOCK) // CMP_STRIDE + 1
    n_sel = seq // SEL_BLOCK
    ng = seq // CMP_STRIDE
    assert ng == QB and n_sel <= QB
    wide = CMP_STRIDE * HEAD_DIM
    return pl.pallas_call(
        functools.partial(_nsa_cmp_kernel, n_cmp=n_cmp, n_sel=n_sel),
        grid=(batch, NSA_KV_HEADS, nqb),
        in_specs=[
            pl.BlockSpec((QB, NSA_GQA * 128), lambda b, h, i: (b * nqb + i, h)),
            pl.BlockSpec((1, 1, ng, wide), lambda b, h, i: (b, h, 0, 0)),
            pl.BlockSpec((1, 1, ng, wide), lambda b, h, i: (b, NSA_KV_HEADS + h, 0, 0)),
            pl.BlockSpec((2, 2, wide), lambda b, h, i: (0, 0, 0)),
            pl.BlockSpec((2, 2 * wide, HEAD_DIM), lambda b, h, i: (0, 0, 0)),
            pl.BlockSpec((NSA_GQA, QB, QB), lambda b, h, i: (h, i, 0)),
            pl.BlockSpec((QB, QB), lambda b, h, i: (0, 0)),
        ],
        out_specs=[
            pl.BlockSpec((QB, 256), lambda b, h, i: (b * nqb + i, h)),
            pl.BlockSpec((1, 1, QB, QB), lambda b, h, i: (b, h, i, 0)),
        ],
        out_shape=[
            jax.ShapeDtypeStruct((batch * seq, NSA_WIDTH), F32),
            jax.ShapeDtypeStruct((batch, NSA_KV_HEADS, seq, QB), BF16),
        ],
        scratch_shapes=[pltpu.VMEM((QB, HEAD_DIM), BF16), pltpu.VMEM((QB, HEAD_DIM), BF16)],
        compiler_params=_params(("parallel", "parallel", "arbitrary")),
        name="nsa_compressed",
    )(pb, ucmp, ucmp, pos2, cmp_w, bias_c, overlap)


SEL_CHUNK = 2 * QB
WIN_FAR = WINDOW - QB


def _nsa_sw_kernel(q_ref, ks_ref, kw_ref, ocmp_ref, misc_ref, sel_ref, exp_ref, tb_ref, o_ref, madd_ref):
    i = pl.program_id(2)
    G = NSA_GQA
    q4 = jnp.concatenate([q_ref[:, g * 128:(g + 1) * 128] for g in range(G)], axis=0)
    madd_ref[...] = (_dot(sel_ref[0, 0], exp_ref[...]) - 1.0) * (-NEG_INF)
    t_pos = i * QB + lax.broadcasted_iota(jnp.int32, (QB, 1), 0)

    def update(state, kv, add_head, add_all):
        s4 = _dot_nt(q4, kv)
        probs, scaled = [], []
        for g in range(G):
            m, l, acc = state[g]
            s = s4[g * QB:(g + 1) * QB]
            if add_head is not None:
                s = s + add_head[g * QB:(g + 1) * QB]
            if add_all is not None:
                s = s + add_all
            m_new = jnp.maximum(m, jnp.max(s, axis=-1, keepdims=True))
            p = jnp.exp(s - m_new)
            alpha = jnp.exp(m - m_new)
            probs.append(p.astype(BF16))
            scaled.append((m_new, alpha * l + jnp.sum(p, axis=-1, keepdims=True), alpha * acc))
        pv = _dot(jnp.concatenate(probs, axis=0), kv)
        return [(m, l, acc + pv[g * QB:(g + 1) * QB]) for g, (m, l, acc) in enumerate(scaled)]

    def near(kv_ref, with_sel):
        state = [(jnp.full((QB, 1), NEG_INF, F32), jnp.zeros((QB, 1), F32), jnp.zeros((QB, 128), F32))] * G
        rows0 = pl.ds(pl.multiple_of(i * QB, QB), QB)
        state = update(state, kv_ref[rows0, :], tb_ref[0, 0], madd_ref[:, rows0] if with_sel else None)
        j1 = jnp.maximum(i - 1, 0)
        rows1 = pl.ds(pl.multiple_of(j1 * QB, QB), QB)
        gone = jnp.where(i >= 1, 0.0, NEG_INF)
        extra = madd_ref[:, rows1] + gone if with_sel else jnp.full((QB, QB), gone, F32)
        return update(state, kv_ref[rows1, :], tb_ref[1, 0], extra)

    far_end = (i - 1) * QB

    def sel_far(cidx, flat):
        state = [tuple(flat[3 * g:3 * g + 3]) for g in range(G)]
        cols = pl.ds(pl.multiple_of(cidx * SEL_CHUNK, SEL_CHUNK), SEL_CHUNK)
        k_pos = cidx * SEL_CHUNK + lax.broadcasted_iota(jnp.int32, (QB, SEL_CHUNK), 1)
        add = jnp.where(k_pos < far_end, madd_ref[:, cols], NEG_INF)
        state = update(state, ks_ref[cols, :], None, add)
        return tuple(x for st in state for x in st)

    flat = lax.fori_loop(0, i // 2, sel_far, tuple(x for st in near(ks_ref, True) for x in st))
    sel_state = [tuple(flat[3 * g:3 * g + 3]) for g in range(G)]

    start = jnp.maximum(i - WINDOW // QB, 0) * QB
    rows = pl.ds(pl.multiple_of(start, QB), WIN_FAR)
    k_pos = start + lax.broadcasted_iota(jnp.int32, (QB, WIN_FAR), 1)
    live = (k_pos < far_end) & (k_pos > t_pos - WINDOW)
    win_state = update(near(kw_ref, False), kw_ref[rows, :], None, jnp.where(live, 0.0, NEG_INF))

    low = lax.broadcasted_iota(jnp.int32, (QB, 128), 1) < HEAD_DIM
    gate = _sigmoid(misc_ref[...])

    def pair(values, pp):
        a, b = values[2 * pp], values[2 * pp + 1]
        return jnp.where(low, pltpu.roll(a, HEAD_DIM, 1), b)

    def pair_gate(branch, pp):
        ca, cb = 3 * (2 * pp) + branch, 3 * (2 * pp + 1) + branch
        return jnp.where(low, gate[:, ca:ca + 1], gate[:, cb:cb + 1])

    o_sel = [acc / l for (_, l, acc) in sel_state]
    o_win = [acc / l for (_, l, acc) in win_state]
    for pp in range(G // 2):
        lanes = slice(pp * 128, (pp + 1) * 128)
        o_ref[:, lanes] = (pair_gate(0, pp) * ocmp_ref[:, lanes] + pair_gate(1, pp) * pair(o_sel, pp)
                           + pair_gate(2, pp) * pair(o_win, pp))


def _nsa_sel_win(pb, pf, o_cmp, sel, expand, tb, batch, seq):
    nqb = seq // QB
    G = NSA_GQA
    return pl.pallas_call(
        _nsa_sw_kernel,
        grid=(batch, NSA_KV_HEADS, nqb),
        in_specs=[
            pl.BlockSpec((QB, G * 128), lambda b, h, i: (b * nqb + i, h)),
            pl.BlockSpec((seq, 128), lambda b, h, i: (b, PB_SEL // 128 + h)),
            pl.BlockSpec((seq, 128), lambda b, h, i: (b, PB_WIN // 128 + h)),
            pl.BlockSpec((QB, 256), lambda b, h, i: (b * nqb + i, h)),
            pl.BlockSpec((QB, 128), lambda b, h, i: (b * nqb + i, PF_MISC // 128 + h)),
            pl.BlockSpec((1, 1, QB, QB), lambda b, h, i: (b, h, i, 0)),
            pl.BlockSpec((QB, seq), lambda b, h, i: (0, 0)),
            pl.BlockSpec((2, 1, G * QB, QB), lambda b, h, i: (0, h, 0, 0)),
        ],
        out_specs=pl.BlockSpec((QB, 256), lambda b, h, i: (b * nqb + i, h)),
        out_shape=jax.ShapeDtypeStruct((batch * seq, NSA_WIDTH), F32),
        scratch_shapes=[pltpu.VMEM((QB, seq), F32)],
        compiler_params=_params(("parallel", "parallel", "arbitrary")),
        name="nsa_selected_window",
    )(pb, pb, pb, o_cmp, pf, sel, expand, tb)


def _ssd_kernel(z_ref, xs_ref, bc_ref, misc_ref, cwx_ref, cwb_ref, cbx_ref, cbb_ref, hp_ref, ng_ref,
                o_ref, xbuf, bbuf, state, ybuf):
    ci = pl.program_id(1)
    L = QB
    P = HEAD_DIM
    GN = SSD_GROUPS * SSD_STATE

    @pl.when(ci == 0)
    def _():
        xbuf[0:8, :] = jnp.zeros((8, SSD_INNER), F32)
        bbuf[0:8, :] = jnp.zeros((8, 2 * GN), F32)
        state[...] = jnp.zeros_like(state)

    def conv_silu(buf, src_ref, w_ref, b_ref):
        buf[8:8 + L, :] = src_ref[...]
        out = b_ref[...]
        for k in range(SSD_CONV):
            lo = 8 - (SSD_CONV - 1) + k
            out = out + w_ref[k:k + 1, :] * buf[lo:lo + L, :]
        buf[0:8, :] = buf[L:L + 8, :]
        return _silu(out)

    xs = conv_silu(xbuf, xs_ref, cwx_ref, cbx_ref)
    bcs = conv_silu(bbuf, bc_ref, cwb_ref, cbb_ref)

    misc = misc_ref[...]
    pre = misc + hp_ref[0:1, :]
    dt = jnp.maximum(pre, 0.0) + jnp.log1p(jnp.exp(-jnp.abs(pre)))
    a_dt = dt * (-jnp.exp(hp_ref[1:2, :]))
    r = lax.broadcasted_iota(jnp.int32, (L, L), 0)
    c = lax.broadcasted_iota(jnp.int32, (L, L), 1)
    causal = r >= c
    a_cs = _split3_dot_left(causal.astype(BF16), a_dt)
    a_cs_t = a_cs.T
    d_skip = hp_ref[2:3, :]

    for g in range(SSD_GROUPS):
        bm = bcs[:, g * SSD_STATE:(g + 1) * SSD_STATE]
        cm = bcs[:, GN + g * SSD_STATE:GN + (g + 1) * SSD_STATE]
        cm16 = cm.astype(BF16)
        cb = _dot_nt(cm16, bm.astype(BF16))
        bm_t16 = bm.T.astype(BF16)
        for e in range(SSD_HPG):
            hd = g * SSD_HPG + e
            lanes = slice(hd * P, (hd + 1) * P)
            col = MISC_DT + hd
            x = xs[:, lanes]
            xdt = x * dt[:, col:col + 1]
            acs = a_cs[:, col:col + 1]
            seg = jnp.exp(jnp.where(causal, acs - a_cs_t[col:col + 1, :], -jnp.inf))
            y = _dot((cb * seg).astype(BF16), xdt.astype(BF16))
            h_in = state[hd]
            y = y + jnp.exp(acs) * _dot(cm16, h_in.astype(BF16))
            total = a_cs[L - 1:L, col:col + 1]
            to_end = jnp.exp(total - acs)
            state[hd] = h_in * jnp.exp(total) + _dot(bm_t16, (xdt * to_end).astype(BF16))
            y = y + d_skip[:, col:col + 1] * x
            ybuf[:, lanes] = y * _silu(z_ref[:, lanes])

    width = SSD_INNER // SSD_GROUPS
    for g in range(SSD_GROUPS):
        lanes = slice(g * width, (g + 1) * width)
        y = ybuf[:, lanes]
        o_ref[:, lanes] = y * lax.rsqrt(jnp.mean(y * y, axis=-1, keepdims=True) + EPS) * ng_ref[:, lanes]


def _ssd(pf, conv_w, conv_b, head_params, norm_g, batch, seq):
    nc = seq // QB
    GN = SSD_GROUPS * SSD_STATE
    cwx, cwb = conv_w[:, :SSD_INNER], conv_w[:, SSD_INNER:]
    cbx, cbb = conv_b[:SSD_INNER].reshape(1, -1), conv_b[SSD_INNER:].reshape(1, -1)
    full = lambda shape: pl.BlockSpec(shape, lambda b, ci: (0,) * len(shape))
    return pl.pallas_call(
        _ssd_kernel,
        grid=(batch, nc),
        in_specs=[
            pl.BlockSpec((QB, SSD_INNER), lambda b, ci: (b * nc + ci, PF_Z // SSD_INNER)),
            pl.BlockSpec((QB, SSD_INNER), lambda b, ci: (b * nc + ci, PF_XS // SSD_INNER)),
            pl.BlockSpec((QB, 2 * GN), lambda b, ci: (b * nc + ci, PF_BC // (2 * GN))),
            pl.BlockSpec((QB, 128), lambda b, ci: (b * nc + ci, PF_MISC // 128)),
            full((SSD_CONV, SSD_INNER)),
            full((SSD_CONV, 2 * GN)),
            full((1, SSD_INNER)),
            full((1, 2 * GN)),
            full((8, 128)),
            full((1, SSD_INNER)),
        ],
        out_specs=pl.BlockSpec((QB, SSD_INNER), lambda b, ci: (b * nc + ci, 0)),
        out_shape=jax.ShapeDtypeStruct((batch * seq, SSD_INNER), F32),
        scratch_shapes=[
            pltpu.VMEM((QB + 8, SSD_INNER), F32),
            pltpu.VMEM((QB + 8, 2 * GN), F32),
            pltpu.VMEM((SSD_HEADS, SSD_STATE, HEAD_DIM), F32),
            pltpu.VMEM((QB, SSD_INNER), F32),
        ],
        compiler_params=_params(("parallel", "arbitrary")),
        name="ssd",
    )(pf, pf, pf, pf, cwx, cwb, cbx, cbb, head_params, norm_g.reshape(1, -1))


def _out_proj_kernel(x_ref, nsa_ref, sb_ref, ssd_ref, gn_ref, gs_ref, w_ref, o_ref, mix_ref):
    @pl.when(pl.program_id(1) == 0)
    def _():
        mix_ref[:, 0:NSA_WIDTH] = _rms(nsa_ref[...], gn_ref[...]).astype(BF16)
        mix_ref[:, NSA_WIDTH:NSA_WIDTH + SB_WIDTH] = _rms(sb_ref[...], gs_ref[...]).astype(BF16)
        mix_ref[:, NSA_WIDTH + SB_WIDTH:] = ssd_ref[...].astype(BF16)

    o_ref[...] = x_ref[...] + _dot(mix_ref[...], w_ref[...])


def _out_proj(x, o_nsa, o_sb, o_ssd, g_nsa, g_sb, w, *, tm=1024, tn=512):
    n, d = x.shape
    dm = w.shape[0]
    return pl.pallas_call(
        _out_proj_kernel,
        grid=(n // tm, d // tn),
        in_specs=[
            pl.BlockSpec((tm, tn), lambda i, j: (i, j)),
            pl.BlockSpec((tm, NSA_WIDTH), lambda i, j: (i, 0)),
            pl.BlockSpec((tm, SB_WIDTH), lambda i, j: (i, 0)),
            pl.BlockSpec((tm, SSD_INNER), lambda i, j: (i, 0)),
            pl.BlockSpec((1, NSA_WIDTH), lambda i, j: (0, 0)),
            pl.BlockSpec((1, SB_WIDTH), lambda i, j: (0, 0)),
            pl.BlockSpec((dm, tn), lambda i, j: (0, j)),
        ],
        out_specs=pl.BlockSpec((tm, tn), lambda i, j: (i, j)),
        out_shape=jax.ShapeDtypeStruct((n, d), F32),
        scratch_shapes=[pltpu.VMEM((tm, dm), BF16)],
        compiler_params=_params(("parallel", "arbitrary")),
        name="out_proj",
    )(x, o_nsa, o_sb, o_ssd, g_nsa.reshape(1, -1), g_sb.reshape(1, -1), w)


def _rel_bucket(dist):
    dist = jnp.maximum(dist, 0)
    max_exact = REL_BUCKETS // 2
    log_ratio = jnp.log(jnp.maximum(dist, 1).astype(F32) / max_exact) / math.log(REL_MAX_DIST / max_exact)
    large = jnp.minimum(max_exact + (log_ratio * (REL_BUCKETS - max_exact)).astype(jnp.int32), REL_BUCKETS - 1)
    return jnp.where(dist < max_exact, dist, large)


def _bias_tables(rel_bias, seq):
    assert QB >= REL_MAX_DIST
    def lookup(dist):
        onehot = jax.nn.one_hot(_rel_bucket(dist), REL_BUCKETS, dtype=F32)
        return jnp.einsum('...k,kh->h...', onehot, rel_bias, precision=lax.Precision.HIGHEST)

    t = jnp.arange(seq)[:, None]
    cend = jnp.arange(QB)[None, :] * CMP_STRIDE + CMP_BLOCK - 1
    bias_c = lookup(t - cend)
    r = jnp.arange(QB)[:, None]
    m = jnp.arange(QB)[None, :]
    near = jnp.stack([lookup(r - m), lookup(QB + r - m)])
    near = near - rel_bias[REL_BUCKETS - 1][None, :, None, None]
    near = near.at[0].add(jnp.where(m > r, NEG_INF, 0.0))
    return bias_c, near.reshape(2, NSA_KV_HEADS, NSA_GQA * QB, QB)


def _expand_table(seq):
    j = jnp.arange(QB)[:, None]
    s = jnp.arange(seq)[None, :]
    return (s // SEL_BLOCK == j).astype(BF16)


def _overlap_table(seq):
    n_cmp = (seq - CMP_BLOCK) // CMP_STRIDE + 1
    cs = jnp.arange(QB)[:, None] * CMP_STRIDE
    ce = cs + CMP_BLOCK - 1
    ss = jnp.arange(QB)[None, :] * SEL_BLOCK
    ov = jnp.maximum(jnp.minimum(ce, ss + SEL_BLOCK - 1) - jnp.maximum(cs, ss) + 1, 0).astype(F32) / CMP_BLOCK
    keep = (jnp.arange(QB)[:, None] < n_cmp) & (jnp.arange(QB)[None, :] < seq // SEL_BLOCK)
    return jnp.where(keep, ov, 0.0).astype(BF16)


def _in_proj_weights(w):
    scale = HEAD_DIM ** -0.5
    o_q, o_kv, o_gate = 0, NSA_WIDTH, NSA_WIDTH + 768
    o_sb = o_gate + 3 * NSA_HEADS
    o_z = o_sb + 3 * SB_WIDTH
    o_xbc = o_z + SSD_INNER
    o_dt = o_xbc + SSD_INNER + 2 * SSD_GROUPS * SSD_STATE
    col = lambda a, n: w[:, a:a + n]
    kv = lambda br, which, h: col(o_kv + br * 256 + which * 128 + h * HEAD_DIM, HEAD_DIM)
    zeros = lambda n: jnp.zeros((w.shape[0], n), w.dtype)
    branch = lambda br: [kv(br, 0, 0), kv(br, 1, 0), kv(br, 0, 1), kv(br, 1, 1)]
    q_heads = []
    for h in range(NSA_HEADS):
        q_heads += [col(o_q + h * HEAD_DIM, HEAD_DIM) * scale, zeros(128 - HEAD_DIM)]
    wb = jnp.concatenate(
        q_heads + branch(1) + branch(2)
        + [col(o_sb, SB_WIDTH) * scale, col(o_sb + SB_WIDTH, 2 * SB_WIDTH)], axis=1)
    gates = 3 * NSA_GQA
    wf = jnp.concatenate(
        [col(o_z, SSD_INNER), col(o_xbc, SSD_INNER + 2 * SSD_GROUPS * SSD_STATE), col(o_kv, 256),
         col(o_gate, gates), zeros(MISC_DT - gates), col(o_dt, SSD_HEADS), zeros(128 - MISC_DT - SSD_HEADS),
         col(o_gate + gates, gates), zeros(128 - gates)], axis=1)
    assert wb.shape[1] == PB_COLS and wf.shape[1] == PF_COLS
    return wb.astype(BF16), wf.astype(BF16)


def _head_params(dt_bias, a_log, d_skip):
    rows = jnp.stack([dt_bias, a_log, d_skip]).astype(F32)
    return jnp.zeros((8, 128), F32).at[0:3, MISC_DT:MISC_DT + SSD_HEADS].set(rows)


def _mixer(x, l, tables, batch, seq, mix_norm, w_in, w_out, cmp_pos, cmp_w, nsa_norm, sb_norm,
           conv_w, conv_b, dt_bias, a_log, d_skip, ssd_norm):
    bias_c, near, overlap, expand = tables
    wb, wf = _in_proj_weights(w_in[l])
    pb = _rms_matmul(x, mix_norm[l], wb, BF16, name="in_proj_bf16")
    pf = _rms_matmul(x, mix_norm[l], wf, F32, name="in_proj_f32")

    ng = seq // CMP_STRIDE
    ucmp = pf[:, PF_CMP:PF_CMP + 256].reshape(batch, ng, CMP_STRIDE, 4, HEAD_DIM)
    ucmp = ucmp.transpose(0, 3, 1, 2, 4).reshape(batch, 4, ng, CMP_STRIDE * HEAD_DIM)
    pos2 = cmp_pos[l].reshape(2, 2, CMP_STRIDE * HEAD_DIM)
    o_cmp, sel = _nsa_cmp(pb, ucmp, pos2, cmp_w[l].astype(BF16), bias_c, overlap, batch, seq)
    o_nsa = _nsa_sel_win(pb, pf, o_cmp, sel, expand, near, batch, seq)
    o_sb = _sb_attention(pb, batch, seq)
    o_ssd = _ssd(pf, conv_w[l], conv_b[l], _head_params(dt_bias[l], a_log[l], d_skip[l]), ssd_norm[l],
                 batch, seq)
    return _out_proj(x, o_nsa, o_sb, o_ssd, nsa_norm[l], sb_norm[l], w_out[l].astype(BF16))


def kernel(x, rel_bias, ffn1_norm, ffn1_w_gate, ffn1_w_up, ffn1_w_down, mix_norm, w_in, w_out, nsa_cmp_pos, nsa_cmp_w, nsa_out_norm, sb_out_norm, ssd_conv_w, ssd_conv_b, ssd_dt_bias, ssd_a_log, ssd_d, ssd_out_norm, ffn2_norm, ffn2_w_gate, ffn2_w_up, ffn2_w_down, final_norm):
    batch, seq, d = x.shape
    depth = w_in.shape[0]
    tables = _bias_tables(rel_bias, seq) + (_overlap_table(seq), _expand_table(seq))
    h = x.reshape(batch * seq, d)
    for l in range(depth):
        h = _ffn(h, ffn1_norm[l], ffn1_w_gate[l].astype(BF16), ffn1_w_up[l].astype(BF16),
                 ffn1_w_down[l].astype(BF16), final_norm, final=False)
        h = _mixer(h, l, tables, batch, seq, mix_norm, w_in, w_out, nsa_cmp_pos, nsa_cmp_w, nsa_out_norm,
                   sb_out_norm, ssd_conv_w, ssd_conv_b, ssd_dt_bias, ssd_a_log, ssd_d, ssd_out_norm)
        h = _ffn(h, ffn2_norm[l], ffn2_w_gate[l].astype(BF16), ffn2_w_up[l].astype(BF16),
                 ffn2_w_down[l].astype(BF16), final_norm, final=(l == depth - 1))
    return h.reshape(batch, seq, d)
```

```python
import functools
import math

import jax
import jax.numpy as jnp
from jax import lax
from jax.experimental import pallas as pl
from jax.experimental.pallas import tpu as pltpu

D_MODEL = 2048
D_FF = 5632
HEAD_DIM = 64
QB = 128
NEG_INF = -1e30
EPS = 1e-6

NSA_HEADS = 8
NSA_KV_HEADS = 2
NSA_GQA = NSA_HEADS // NSA_KV_HEADS
NSA_WIDTH = NSA_HEADS * HEAD_DIM
CMP_BLOCK = 32
CMP_STRIDE = 16
SEL_BLOCK = 64
SEL_TOPK = 8
FORCE_SCORE = 1e3
WINDOW = 512
SB_HEADS = 8
SB_WIDTH = SB_HEADS * HEAD_DIM
SSD_HEADS = 16
SSD_INNER = SSD_HEADS * HEAD_DIM
SSD_GROUPS = 2
SSD_HPG = SSD_HEADS // SSD_GROUPS
SSD_STATE = 128
SSD_CONV = 4
REL_BUCKETS = 32
REL_MAX_DIST = 128

PB_NSA_Q, PB_SEL, PB_WIN, PB_SB_Q, PB_SB_K, PB_SB_V, PB_COLS = 0, 1024, 1280, 1536, 2048, 2560, 3072
PF_Z, PF_XS, PF_BC, PF_CMP, PF_MISC, PF_COLS = 0, 1024, 2048, 2560, 2816, 3072
MISC_DT = 16

VMEM_LIMIT = 52 * 1024 * 1024

BF16 = jnp.bfloat16
F32 = jnp.float32


def _dot(a, b):
    return jnp.dot(a, b, preferred_element_type=F32)


def _dot_nt(a, b):
    return lax.dot_general(a, b, (((1,), (1,)), ((), ())), preferred_element_type=F32)


def _split2_dot(x, m):
    hi = x.astype(BF16)
    lo = (x - hi.astype(F32)).astype(BF16)
    return _dot(hi, m) + _dot(lo, m)


def _split3_dot_left(m, x):
    hi = x.astype(BF16)
    r1 = x - hi.astype(F32)
    mid = r1.astype(BF16)
    lo = (r1 - mid.astype(F32)).astype(BF16)
    return _dot(m, hi) + _dot(m, mid) + _dot(m, lo)


def _rms(x, g):
    return x * lax.rsqrt(jnp.mean(x * x, axis=-1, keepdims=True) + EPS) * g


def _silu(x):
    return x / (1.0 + jnp.exp(-x))


def _sigmoid(x):
    return 1.0 / (1.0 + jnp.exp(-x))


def _params(sem):
    return pltpu.CompilerParams(dimension_semantics=sem, vmem_limit_bytes=VMEM_LIMIT)


def _ffn_kernel(x_ref, g_ref, wg_ref, wu_ref, wd_ref, fg_ref, o_ref, h_ref, *, final):
    f = pl.program_id(1)

    @pl.when(f == 0)
    def _():
        h_ref[...] = _rms(x_ref[...], g_ref[...]).astype(BF16)

    h = h_ref[...]
    gate = _dot(h, wg_ref[...].astype(BF16))
    up = _dot(h, wu_ref[...].astype(BF16))
    part = _dot((_silu(gate) * up).astype(BF16), wd_ref[...].astype(BF16))

    @pl.when(f == 0)
    def _():
        o_ref[...] = part

    @pl.when(f > 0)
    def _():
        o_ref[...] += part

    @pl.when(f == pl.num_programs(1) - 1)
    def _():
        y = x_ref[...] + 0.5 * o_ref[...]
        if final:
            y = _rms(y, fg_ref[...])
        o_ref[...] = y


def _ffn(x, g, wg, wu, wd, layer, final_g, *, final, tm=1024, tf=256):
    n, d = x.shape
    dff = wg.shape[2]
    once = pl.Buffered(1)
    return pl.pallas_call(
        functools.partial(_ffn_kernel, final=final),
        grid=(n // tm, dff // tf),
        in_specs=[
            pl.BlockSpec((tm, d), lambda i, f: (i, 0), pipeline_mode=once),
            pl.BlockSpec((1, d), lambda i, f: (0, 0)),
            pl.BlockSpec((None, d, tf), lambda i, f: (layer, 0, f)),
            pl.BlockSpec((None, d, tf), lambda i, f: (layer, 0, f)),
            pl.BlockSpec((None, tf, d), lambda i, f: (layer, f, 0)),
            pl.BlockSpec((1, d), lambda i, f: (0, 0)),
        ],
        out_specs=pl.BlockSpec((tm, d), lambda i, f: (i, 0), pipeline_mode=once),
        out_shape=jax.ShapeDtypeStruct((n, d), F32),
        scratch_shapes=[pltpu.VMEM((tm, d), BF16)],
        compiler_params=_params(("parallel", "arbitrary")),
        name="ffn",
    )(x, g.reshape(1, d), wg, wu, wd, final_g.reshape(1, d))


def _rms_matmul_kernel(x_ref, g_ref, w_ref, o_ref, h_ref):
    @pl.when(pl.program_id(1) == 0)
    def _():
        h_ref[...] = _rms(x_ref[...], g_ref[...]).astype(BF16)

    o_ref[...] = _dot(h_ref[...], w_ref[...]).astype(o_ref.dtype)


def _rms_matmul(x, g, w, out_dtype, *, tm=1024, tn=512, name):
    n, d = x.shape
    c = w.shape[1]
    return pl.pallas_call(
        _rms_matmul_kernel,
        grid=(n // tm, c // tn),
        in_specs=[
            pl.BlockSpec((tm, d), lambda i, j: (i, 0)),
            pl.BlockSpec((1, d), lambda i, j: (0, 0)),
            pl.BlockSpec((d, tn), lambda i, j: (0, j)),
        ],
        out_specs=pl.BlockSpec((tm, tn), lambda i, j: (i, j)),
        out_shape=jax.ShapeDtypeStruct((n, c), out_dtype),
        scratch_shapes=[pltpu.VMEM((tm, d), BF16)],
        compiler_params=_params(("parallel", "arbitrary")),
        name=name,
    )(x, g.reshape(1, d), w)


SB_GROUP = 8


def _sb_kernel(q_ref, k_ref, v_ref, o_ref):
    i = pl.program_id(2)
    pairs = SB_GROUP // 2
    r = lax.broadcasted_iota(jnp.int32, (QB, QB), 0)
    c = lax.broadcasted_iota(jnp.int32, (QB, QB), 1)
    strict = c < r
    low = c < HEAD_DIM
    later_ones = jnp.concatenate([(r > c).astype(BF16), jnp.ones((QB, QB), BF16)], axis=1)
    later_ones = jnp.concatenate([later_ones, later_ones], axis=0)
    zero16 = jnp.zeros((QB, QB), BF16)

    def block_diag(x):
        return jnp.concatenate([jnp.where(low, x, zero16), jnp.where(low, zero16, x)], axis=0)

    def tile(j, tails, accs, diagonal):
        rows = pl.ds(pl.multiple_of(j * QB, QB), QB)
        zs, softs, parts = [], [], []
        for h in range(SB_GROUP):
            pp = h // 2
            if h % 2 == 0:
                lanes = slice(pp * 128, (pp + 1) * 128)
                z_pair = _dot_nt(q_ref[:, lanes], block_diag(k_ref[rows, lanes]))
            z = z_pair[:, (h % 2) * QB:(h % 2 + 1) * QB]
            soft = jnp.maximum(z, 0.0) + jnp.log(1.0 + jnp.exp(-jnp.abs(z)))
            if diagonal:
                soft = jnp.where(strict, soft, 0.0)
            hi = soft.astype(BF16)
            parts.append(jnp.concatenate([hi, (soft - hi.astype(F32)).astype(BF16)], axis=1))
            zs.append(z)
            softs.append(soft)
        sums = _dot(jnp.concatenate(parts, axis=0), later_ones)
        new_tails, new_accs = [], []
        for pp in range(pairs):
            a_pair = []
            for hh in range(2):
                h = 2 * pp + hh
                sm = sums[h * QB:(h + 1) * QB]
                a = jnp.exp((zs[h] - softs[h]) - (tails[h] + sm[:, 0:QB]))
                if diagonal:
                    a = jnp.where(strict, a, 0.0)
                a_pair.append(a.astype(BF16))
                new_tails.append(tails[h] + sm[:, QB:2 * QB])
            v_bd = block_diag(v_ref[rows, pp * 128:(pp + 1) * 128])
            new_accs.append(accs[pp] + _dot(jnp.concatenate(a_pair, axis=1), v_bd))
        return new_tails, new_accs

    zeros = jnp.zeros((QB, QB), F32)
    tails, accs = tile(i, [zeros] * SB_GROUP, [zeros] * pairs, True)

    def body(step, carry):
        t, a = tile(i - 1 - step, list(carry[0]), list(carry[1]), False)
        return tuple(t), tuple(a)

    _, accs = lax.fori_loop(0, i, body, (tuple(tails), tuple(accs)))
    for pp in range(pairs):
        o_ref[:, pp * 128:(pp + 1) * 128] = accs[pp]


def _sb_attention(pb, batch, seq):
    nqb = seq // QB
    w = SB_GROUP * HEAD_DIM
    qc, kc, vc = PB_SB_Q // w, PB_SB_K // w, PB_SB_V // w
    return pl.pallas_call(
        _sb_kernel,
        grid=(batch, SB_HEADS // SB_GROUP, nqb),
        in_specs=[
            pl.BlockSpec((QB, w), lambda b, p, i: (b * nqb + i, qc + p)),
            pl.BlockSpec((seq, w), lambda b, p, i: (b, kc + p)),
            pl.BlockSpec((seq, w), lambda b, p, i: (b, vc + p)),
        ],
        out_specs=pl.BlockSpec((QB, w), lambda b, p, i: (b * nqb + i, p)),
        out_shape=jax.ShapeDtypeStruct((batch * seq, SB_WIDTH), F32),
        compiler_params=_params(("parallel", "parallel", "arbitrary")),
        name="sb_attention",
    )(pb, pb, pb)


def _nsa_cmp_kernel(q_ref, uk_ref, uv_ref, pos_ref, w_ref, bias_ref, ov_ref, ocmp_ref, sel_ref,
                    kc_ref, vc_ref, *, n_cmp, n_sel):
    i = pl.program_id(2)
    half = CMP_STRIDE * HEAD_DIM

    @pl.when(i == 0)
    def _():
        for kv, (u_ref, dst) in enumerate(((uk_ref, kc_ref), (uv_ref, vc_ref))):
            u = u_ref[0, 0]
            top = _dot((u + pos_ref[kv, 0:1, :]).astype(BF16), w_ref[kv, 0:half, :])
            bot = _dot((u + pos_ref[kv, 1:2, :]).astype(BF16), w_ref[kv, half:2 * half, :])
            dst[...] = (top + pltpu.roll(bot, QB - 1, 0)).astype(BF16)

    rows = i * QB + lax.broadcasted_iota(jnp.int32, (QB, QB), 0)
    cols = lax.broadcasted_iota(jnp.int32, (QB, QB), 1)
    valid = (rows >= cols * CMP_STRIDE + (CMP_BLOCK - 1)) & (cols < n_cmp)
    any_valid = rows >= CMP_BLOCK - 1
    kc = kc_ref[...]
    vc = vc_ref[...]
    p_sum = jnp.zeros((QB, QB), F32)
    for g in range(NSA_GQA):
        lanes = slice(g * HEAD_DIM, (g + 1) * HEAD_DIM)
        q = q_ref[:, g * 128:g * 128 + HEAD_DIM]
        s = jnp.where(valid, _dot_nt(q, kc) + bias_ref[g], NEG_INF)
        e = jnp.exp(s - jnp.max(s, axis=-1, keepdims=True))
        p = jnp.where(any_valid, e / jnp.sum(e, axis=-1, keepdims=True), 0.0)
        ocmp_ref[:, lanes] = _dot(p.astype(BF16), vc)
        p_sum = p_sum + p

    p_sel = _split2_dot(p_sum, ov_ref[...])
    cur = rows // SEL_BLOCK
    eligible = (cols * SEL_BLOCK <= rows) & (cols < n_sel)
    forced = (cols == 0) | (cols == cur) | (cols == cur - 1)
    score = jnp.where(eligible, p_sel + jnp.where(forced, FORCE_SCORE, 0.0), NEG_INF)
    rank = jnp.zeros((QB, QB), F32)
    for j in range(n_sel):
        other = score[:, j:j + 1]
        ahead = (other > score) | ((other == score) & (cols > j))
        rank = rank + jnp.where(ahead, 1.0, 0.0)
    sel_ref[0, 0] = jnp.where(eligible & (rank < SEL_TOPK), 1.0, 0.0).astype(BF16)


def _nsa_cmp(pb, ucmp, pos2, cmp_w, bias_c, overlap, batch, seq):
    nqb = seq // QB
    n_cmp = (seq - CMP_BLOCK) // CMP_STRIDE + 1
    n_sel = seq // SEL_BLOCK
    ng = seq // CMP_STRIDE
    assert ng == QB and n_sel <= QB
    wide = CMP_STRIDE * HEAD_DIM
    return pl.pallas_call(
        functools.partial(_nsa_cmp_kernel, n_cmp=n_cmp, n_sel=n_sel),
        grid=(batch, NSA_KV_HEADS, nqb),
        in_specs=[
            pl.BlockSpec((QB, NSA_GQA * 128), lambda b, h, i: (b * nqb + i, h)),
            pl.BlockSpec((1, 1, ng, wide), lambda b, h, i: (b, h, 0, 0)),
            pl.BlockSpec((1, 1, ng, wide), lambda b, h, i: (b, NSA_KV_HEADS + h, 0, 0)),
            pl.BlockSpec((2, 2, wide), lambda b, h, i: (0, 0, 0)),
            pl.BlockSpec((2, 2 * wide, HEAD_DIM), lambda b, h, i: (0, 0, 0)),
            pl.BlockSpec((NSA_GQA, QB, QB), lambda b, h, i: (h, i, 0)),
            pl.BlockSpec((QB, QB), lambda b, h, i: (0, 0)),
        ],
        out_specs=[
            pl.BlockSpec((QB, 256), lambda b, h, i: (b * nqb + i, h)),
            pl.BlockSpec((1, 1, QB, QB), lambda b, h, i: (b, h, i, 0)),
        ],
        out_shape=[
            jax.ShapeDtypeStruct((batch * seq, NSA_WIDTH), F32),
            jax.ShapeDtypeStruct((batch, NSA_KV_HEADS, seq, QB), BF16),
        ],
        scratch_shapes=[pltpu.VMEM((QB, HEAD_DIM), BF16), pltpu.VMEM((QB, HEAD_DIM), BF16)],
        compiler_params=_params(("parallel", "parallel", "arbitrary")),
        name="nsa_compressed",
    )(pb, ucmp, ucmp, pos2, cmp_w, bias_c, overlap)


SEL_CHUNK = 2 * QB
WIN_FAR = WINDOW - QB


def _nsa_sw_kernel(q_ref, ks_ref, kw_ref, ocmp_ref, misc_ref, sel_ref, exp_ref, tb_ref, o_ref, madd_ref):
    i = pl.program_id(2)
    G = NSA_GQA
    q4 = jnp.concatenate([q_ref[:, g * 128:(g + 1) * 128] for g in range(G)], axis=0)
    madd_ref[...] = (_dot(sel_ref[0, 0], exp_ref[...]) - 1.0) * (-NEG_INF)
    t_pos = i * QB + lax.broadcasted_iota(jnp.int32, (QB, 1), 0)

    def update(state, kv, add_head, add_all):
        s4 = _dot_nt(q4, kv)
        probs, scaled = [], []
        for g in range(G):
            m, l, acc = state[g]
            s = s4[g * QB:(g + 1) * QB]
            if add_head is not None:
                s = s + add_head[g * QB:(g + 1) * QB]
            if add_all is not None:
                s = s + add_all
            m_new = jnp.maximum(m, jnp.max(s, axis=-1, keepdims=True))
            p = jnp.exp(s - m_new)
            alpha = jnp.exp(m - m_new)
            probs.append(p.astype(BF16))
            scaled.append((m_new, alpha * l + jnp.sum(p, axis=-1, keepdims=True), alpha * acc))
        pv = _dot(jnp.concatenate(probs, axis=0), kv)
        return [(m, l, acc + pv[g * QB:(g + 1) * QB]) for g, (m, l, acc) in enumerate(scaled)]

    def near(kv_ref, with_sel):
        state = [(jnp.full((QB, 1), NEG_INF, F32), jnp.zeros((QB, 1), F32), jnp.zeros((QB, 128), F32))] * G
        rows0 = pl.ds(pl.multiple_of(i * QB, QB), QB)
        state = update(state, kv_ref[rows0, :], tb_ref[0, 0], madd_ref[:, rows0] if with_sel else None)
        j1 = jnp.maximum(i - 1, 0)
        rows1 = pl.ds(pl.multiple_of(j1 * QB, QB), QB)
        gone = jnp.where(i >= 1, 0.0, NEG_INF)
        extra = madd_ref[:, rows1] + gone if with_sel else jnp.full((QB, QB), gone, F32)
        return update(state, kv_ref[rows1, :], tb_ref[1, 0], extra)

    far_end = (i - 1) * QB

    def sel_far(cidx, flat):
        state = [tuple(flat[3 * g:3 * g + 3]) for g in range(G)]
        cols = pl.ds(pl.multiple_of(cidx * SEL_CHUNK, SEL_CHUNK), SEL_CHUNK)
        k_pos = cidx * SEL_CHUNK + lax.broadcasted_iota(jnp.int32, (QB, SEL_CHUNK), 1)
        add = jnp.where(k_pos < far_end, madd_ref[:, cols], NEG_INF)
        state = update(state, ks_ref[cols, :], None, add)
        return tuple(x for st in state for x in st)

    flat = lax.fori_loop(0, i // 2, sel_far, tuple(x for st in near(ks_ref, True) for x in st))
    sel_state = [tuple(flat[3 * g:3 * g + 3]) for g in range(G)]

    start = jnp.maximum(i - WINDOW // QB, 0) * QB
    rows = pl.ds(pl.multiple_of(start, QB), WIN_FAR)
    k_pos = start + lax.broadcasted_iota(jnp.int32, (QB, WIN_FAR), 1)
    live = (k_pos < far_end) & (k_pos > t_pos - WINDOW)
    win_state = update(near(kw_ref, False), kw_ref[rows, :], None, jnp.where(live, 0.0, NEG_INF))

    low = lax.broadcasted_iota(jnp.int32, (QB, 128), 1) < HEAD_DIM
    gate = _sigmoid(misc_ref[...])

    def pair(values, pp):
        a, b = values[2 * pp], values[2 * pp + 1]
        return jnp.where(low, pltpu.roll(a, HEAD_DIM, 1), b)

    def pair_gate(branch, pp):
        ca, cb = 3 * (2 * pp) + branch, 3 * (2 * pp + 1) + branch
        return jnp.where(low, gate[:, ca:ca + 1], gate[:, cb:cb + 1])

    o_sel = [acc / l for (_, l, acc) in sel_state]
    o_win = [acc / l for (_, l, acc) in win_state]
    for pp in range(G // 2):
        lanes = slice(pp * 128, (pp + 1) * 128)
        o_ref[:, lanes] = (pair_gate(0, pp) * ocmp_ref[:, lanes] + pair_gate(1, pp) * pair(o_sel, pp)
                           + pair_gate(2, pp) * pair(o_win, pp))


def _nsa_sel_win(pb, pf, o_cmp, sel, expand, tb, batch, seq):
    nqb = seq // QB
    G = NSA_GQA
    return pl.pallas_call(
        _nsa_sw_kernel,
        grid=(batch, NSA_KV_HEADS, nqb),
        in_specs=[
            pl.BlockSpec((QB, G * 128), lambda b, h, i: (b * nqb + i, h)),
            pl.BlockSpec((seq, 128), lambda b, h, i: (b, PB_SEL // 128 + h)),
            pl.BlockSpec((seq, 128), lambda b, h, i: (b, PB_WIN // 128 + h)),
            pl.BlockSpec((QB, 256), lambda b, h, i: (b * nqb + i, h)),
            pl.BlockSpec((QB, 128), lambda b, h, i: (b * nqb + i, PF_MISC // 128 + h)),
            pl.BlockSpec((1, 1, QB, QB), lambda b, h, i: (b, h, i, 0)),
            pl.BlockSpec((QB, seq), lambda b, h, i: (0, 0)),
            pl.BlockSpec((2, 1, G * QB, QB), lambda b, h, i: (0, h, 0, 0)),
        ],
        out_specs=pl.BlockSpec((QB, 256), lambda b, h, i: (b * nqb + i, h)),
        out_shape=jax.ShapeDtypeStruct((batch * seq, NSA_WIDTH), F32),
        scratch_shapes=[pltpu.VMEM((QB, seq), F32)],
        compiler_params=_params(("parallel", "parallel", "arbitrary")),
        name="nsa_selected_window",
    )(pb, pb, pb, o_cmp, pf, sel, expand, tb)


def _ssd_kernel(z_ref, xs_ref, bc_ref, misc_ref, cwx_ref, cwb_ref, cbx_ref, cbb_ref, hp_ref, ng_ref,
                o_ref, xbuf, bbuf, state, ybuf):
    ci = pl.program_id(1)
    L = QB
    P = HEAD_DIM
    GN = SSD_GROUPS * SSD_STATE

    @pl.when(ci == 0)
    def _():
        xbuf[0:8, :] = jnp.zeros((8, SSD_INNER), F32)
        bbuf[0:8, :] = jnp.zeros((8, 2 * GN), F32)
        state[...] = jnp.zeros_like(state)

    def conv_silu(buf, src_ref, w_ref, b_ref):
        buf[8:8 + L, :] = src_ref[...]
        out = b_ref[...]
        for k in range(SSD_CONV):
            lo = 8 - (SSD_CONV - 1) + k
            out = out + w_ref[k:k + 1, :] * buf[lo:lo + L, :]
        buf[0:8, :] = buf[L:L + 8, :]
        return _silu(out)

    xs = conv_silu(xbuf, xs_ref, cwx_ref, cbx_ref)
    bcs = conv_silu(bbuf, bc_ref, cwb_ref, cbb_ref)

    misc = misc_ref[...]
    pre = misc + hp_ref[0:1, :]
    dt = jnp.maximum(pre, 0.0) + jnp.log1p(jnp.exp(-jnp.abs(pre)))
    a_dt = dt * (-jnp.exp(hp_ref[1:2, :]))
    r = lax.broadcasted_iota(jnp.int32, (L, L), 0)
    c = lax.broadcasted_iota(jnp.int32, (L, L), 1)
    causal = r >= c
    a_cs = _split3_dot_left(causal.astype(BF16), a_dt)
    a_cs_t = a_cs.T
    d_skip = hp_ref[2:3, :]

    for g in range(SSD_GROUPS):
        bm = bcs[:, g * SSD_STATE:(g + 1) * SSD_STATE]
        cm = bcs[:, GN + g * SSD_STATE:GN + (g + 1) * SSD_STATE]
        cm16 = cm.astype(BF16)
        cb = _dot_nt(cm16, bm.astype(BF16))
        bm_t16 = bm.T.astype(BF16)
        for e in range(SSD_HPG):
            hd = g * SSD_HPG + e
            lanes = slice(hd * P, (hd + 1) * P)
            col = MISC_DT + hd
            x = xs[:, lanes]
            xdt = x * dt[:, col:col + 1]
            acs = a_cs[:, col:col + 1]
            seg = jnp.exp(jnp.where(causal, acs - a_cs_t[col:col + 1, :], -jnp.inf))
            y = _dot((cb * seg).astype(BF16), xdt.astype(BF16))
            h_in = state[hd]
            y = y + jnp.exp(acs) * _dot(cm16, h_in.astype(BF16))
            total = a_cs[L - 1:L, col:col + 1]
            to_end = jnp.exp(total - acs)
            state[hd] = h_in * jnp.exp(total) + _dot(bm_t16, (xdt * to_end).astype(BF16))
            y = y + d_skip[:, col:col + 1] * x
            ybuf[:, lanes] = y * _silu(z_ref[:, lanes])

    width = SSD_INNER // SSD_GROUPS
    for g in range(SSD_GROUPS):
        lanes = slice(g * width, (g + 1) * width)
        y = ybuf[:, lanes]
        o_ref[:, lanes] = y * lax.rsqrt(jnp.mean(y * y, axis=-1, keepdims=True) + EPS) * ng_ref[:, lanes]


def _ssd(pf, conv_w, conv_b, head_params, norm_g, batch, seq):
    nc = seq // QB
    GN = SSD_GROUPS * SSD_STATE
    cwx, cwb = conv_w[:, :SSD_INNER], conv_w[:, SSD_INNER:]
    cbx, cbb = conv_b[:SSD_INNER].reshape(1, -1), conv_b[SSD_INNER:].reshape(1, -1)
    full = lambda shape: pl.BlockSpec(shape, lambda b, ci: (0,) * len(shape))
    return pl.pallas_call(
        _ssd_kernel,
        grid=(batch, nc),
        in_specs=[
            pl.BlockSpec((QB, SSD_INNER), lambda b, ci: (b * nc + ci, PF_Z // SSD_INNER)),
            pl.BlockSpec((QB, SSD_INNER), lambda b, ci: (b * nc + ci, PF_XS // SSD_INNER)),
            pl.BlockSpec((QB, 2 * GN), lambda b, ci: (b * nc + ci, PF_BC // (2 * GN))),
            pl.BlockSpec((QB, 128), lambda b, ci: (b * nc + ci, PF_MISC // 128)),
            full((SSD_CONV, SSD_INNER)),
            full((SSD_CONV, 2 * GN)),
            full((1, SSD_INNER)),
            full((1, 2 * GN)),
            full((8, 128)),
            full((1, SSD_INNER)),
        ],
        out_specs=pl.BlockSpec((QB, SSD_INNER), lambda b, ci: (b * nc + ci, 0)),
        out_shape=jax.ShapeDtypeStruct((batch * seq, SSD_INNER), F32),
        scratch_shapes=[
            pltpu.VMEM((QB + 8, SSD_INNER), F32),
            pltpu.VMEM((QB + 8, 2 * GN), F32),
            pltpu.VMEM((SSD_HEADS, SSD_STATE, HEAD_DIM), F32),
            pltpu.VMEM((QB, SSD_INNER), F32),
        ],
        compiler_params=_params(("parallel", "arbitrary")),
        name="ssd",
    )(pf, pf, pf, pf, cwx, cwb, cbx, cbb, head_params, norm_g.reshape(1, -1))


def _out_proj_kernel(x_ref, nsa_ref, sb_ref, ssd_ref, gn_ref, gs_ref, w_ref, o_ref, mix_ref):
    @pl.when(pl.program_id(1) == 0)
    def _():
        mix_ref[:, 0:NSA_WIDTH] = _rms(nsa_ref[...], gn_ref[...]).astype(BF16)
        mix_ref[:, NSA_WIDTH:NSA_WIDTH + SB_WIDTH] = _rms(sb_ref[...], gs_ref[...]).astype(BF16)
        mix_ref[:, NSA_WIDTH + SB_WIDTH:] = ssd_ref[...].astype(BF16)

    o_ref[...] = x_ref[...] + _dot(mix_ref[...], w_ref[...])


def _out_proj(x, o_nsa, o_sb, o_ssd, g_nsa, g_sb, w, *, tm=1024, tn=512):
    n, d = x.shape
    dm = w.shape[0]
    return pl.pallas_call(
        _out_proj_kernel,
        grid=(n // tm, d // tn),
        in_specs=[
            pl.BlockSpec((tm, tn), lambda i, j: (i, j)),
            pl.BlockSpec((tm, NSA_WIDTH), lambda i, j: (i, 0)),
            pl.BlockSpec((tm, SB_WIDTH), lambda i, j: (i, 0)),
            pl.BlockSpec((tm, SSD_INNER), lambda i, j: (i, 0)),
            pl.BlockSpec((1, NSA_WIDTH), lambda i, j: (0, 0)),
            pl.BlockSpec((1, SB_WIDTH), lambda i, j: (0, 0)),
            pl.BlockSpec((dm, tn), lambda i, j: (0, j)),
        ],
        out_specs=pl.BlockSpec((tm, tn), lambda i, j: (i, j)),
        out_shape=jax.ShapeDtypeStruct((n, d), F32),
        scratch_shapes=[pltpu.VMEM((tm, dm), BF16)],
        compiler_params=_params(("parallel", "arbitrary")),
        name="out_proj",
    )(x, o_nsa, o_sb, o_ssd, g_nsa.reshape(1, -1), g_sb.reshape(1, -1), w)


def _rel_bucket(dist):
    dist = jnp.maximum(dist, 0)
    max_exact = REL_BUCKETS // 2
    log_ratio = jnp.log(jnp.maximum(dist, 1).astype(F32) / max_exact) / math.log(REL_MAX_DIST / max_exact)
    large = jnp.minimum(max_exact + (log_ratio * (REL_BUCKETS - max_exact)).astype(jnp.int32), REL_BUCKETS - 1)
    return jnp.where(dist < max_exact, dist, large)


def _bias_tables(rel_bias, seq):
    assert QB >= REL_MAX_DIST
    def lookup(dist):
        onehot = jax.nn.one_hot(_rel_bucket(dist), REL_BUCKETS, dtype=F32)
        return jnp.einsum('...k,kh->h...', onehot, rel_bias, precision=lax.Precision.HIGHEST)

    t = jnp.arange(seq)[:, None]
    cend = jnp.arange(QB)[None, :] * CMP_STRIDE + CMP_BLOCK - 1
    bias_c = lookup(t - cend)
    r = jnp.arange(QB)[:, None]
    m = jnp.arange(QB)[None, :]
    near = jnp.stack([lookup(r - m), lookup(QB + r - m)])
    near = near - rel_bias[REL_BUCKETS - 1][None, :, None, None]
    near = near.at[0].add(jnp.where(m > r, NEG_INF, 0.0))
    return bias_c, near.reshape(2, NSA_KV_HEADS, NSA_GQA * QB, QB)


def _expand_table(seq):
    j = jnp.arange(QB)[:, None]
    s = jnp.arange(seq)[None, :]
    return (s // SEL_BLOCK == j).astype(BF16)


def _overlap_table(seq):
    n_cmp = (seq - CMP_BLOCK) // CMP_STRIDE + 1
    cs = jnp.arange(QB)[:, None] * CMP_STRIDE
    ce = cs + CMP_BLOCK - 1
    ss = jnp.arange(QB)[None, :] * SEL_BLOCK
    ov = jnp.maximum(jnp.minimum(ce, ss + SEL_BLOCK - 1) - jnp.maximum(cs, ss) + 1, 0).astype(F32) / CMP_BLOCK
    keep = (jnp.arange(QB)[:, None] < n_cmp) & (jnp.arange(QB)[None, :] < seq // SEL_BLOCK)
    return jnp.where(keep, ov, 0.0).astype(BF16)


def _in_proj_weights(w):
    scale = HEAD_DIM ** -0.5
    o_q, o_kv, o_gate = 0, NSA_WIDTH, NSA_WIDTH + 768
    o_sb = o_gate + 3 * NSA_HEADS
    o_z = o_sb + 3 * SB_WIDTH
    o_xbc = o_z + SSD_INNER
    o_dt = o_xbc + SSD_INNER + 2 * SSD_GROUPS * SSD_STATE
    col = lambda a, n: w[:, a:a + n]
    kv = lambda br, which, h: col(o_kv + br * 256 + which * 128 + h * HEAD_DIM, HEAD_DIM)
    zeros = lambda n: jnp.zeros((w.shape[0], n), w.dtype)
    branch = lambda br: [kv(br, 0, 0), kv(br, 1, 0), kv(br, 0, 1), kv(br, 1, 1)]
    q_heads = []
    for h in range(NSA_HEADS):
        q_heads += [col(o_q + h * HEAD_DIM, HEAD_DIM) * scale, zeros(128 - HEAD_DIM)]
    wb = jnp.concatenate(
        q_heads + branch(1) + branch(2)
        + [col(o_sb, SB_WIDTH) * scale, col(o_sb + SB_WIDTH, 2 * SB_WIDTH)], axis=1)
    gates = 3 * NSA_GQA
    wf = jnp.concatenate(
        [col(o_z, SSD_INNER), col(o_xbc, SSD_INNER + 2 * SSD_GROUPS * SSD_STATE), col(o_kv, 256),
         col(o_gate, gates), zeros(MISC_DT - gates), col(o_dt, SSD_HEADS), zeros(128 - MISC_DT - SSD_HEADS),
         col(o_gate + gates, gates), zeros(128 - gates)], axis=1)
    assert wb.shape[1] == PB_COLS and wf.shape[1] == PF_COLS
    return wb.astype(BF16), wf.astype(BF16)


def _head_params(dt_bias, a_log, d_skip):
    rows = jnp.stack([dt_bias, a_log, d_skip]).astype(F32)
    return jnp.zeros((8, 128), F32).at[0:3, MISC_DT:MISC_DT + SSD_HEADS].set(rows)


def _mixer(x, l, tables, batch, seq, mix_norm, w_in, w_out, cmp_pos, cmp_w, nsa_norm, sb_norm,
           conv_w, conv_b, dt_bias, a_log, d_skip, ssd_norm):
    bias_c, near, overlap, expand = tables
    wb, wf = _in_proj_weights(w_in[l])
    pb = _rms_matmul(x, mix_norm[l], wb, BF16, name="in_proj_bf16")
    pf = _rms_matmul(x, mix_norm[l], wf, F32, name="in_proj_f32")

    ng = seq // CMP_STRIDE
    ucmp = pf[:, PF_CMP:PF_CMP + 256].reshape(batch, ng, CMP_STRIDE, 4, HEAD_DIM)
    ucmp = ucmp.transpose(0, 3, 1, 2, 4).reshape(batch, 4, ng, CMP_STRIDE * HEAD_DIM)
    pos2 = cmp_pos[l].reshape(2, 2, CMP_STRIDE * HEAD_DIM)
    o_cmp, sel = _nsa_cmp(pb, ucmp, pos2, cmp_w[l].astype(BF16), bias_c, overlap, batch, seq)
    o_nsa = _nsa_sel_win(pb, pf, o_cmp, sel, expand, near, batch, seq)
    o_sb = _sb_attention(pb, batch, seq)
    o_ssd = _ssd(pf, conv_w[l], conv_b[l], _head_params(dt_bias[l], a_log[l], d_skip[l]), ssd_norm[l],
                 batch, seq)
    return _out_proj(x, o_nsa, o_sb, o_ssd, nsa_norm[l], sb_norm[l], w_out[l].astype(BF16))


def kernel(x, rel_bias, ffn1_norm, ffn1_w_gate, ffn1_w_up, ffn1_w_down, mix_norm, w_in, w_out, nsa_cmp_pos, nsa_cmp_w, nsa_out_norm, sb_out_norm, ssd_conv_w, ssd_conv_b, ssd_dt_bias, ssd_a_log, ssd_d, ssd_out_norm, ffn2_norm, ffn2_w_gate, ffn2_w_up, ffn2_w_down, final_norm):
    batch, seq, d = x.shape
    depth = w_in.shape[0]
    tables = _bias_tables(rel_bias, seq) + (_overlap_table(seq), _expand_table(seq))
    h = x.reshape(batch * seq, d)
    for l in range(depth):
        h = _ffn(h, ffn1_norm[l], ffn1_w_gate, ffn1_w_up, ffn1_w_down, l, final_norm, final=False)
        h = _mixer(h, l, tables, batch, seq, mix_norm, w_in, w_out, nsa_cmp_pos, nsa_cmp_w, nsa_out_norm,
                   sb_out_norm, ssd_conv_w, ssd_conv_b, ssd_dt_bias, ssd_a_log, ssd_d, ssd_out_norm)
        h = _ffn(h, ffn2_norm[l], ffn2_w_gate, ffn2_w_up, ffn2_w_down, l, final_norm,
                 final=(l == depth - 1))
    return h.reshape(batch, seq, d)
```

```python
import functools
import math

import jax
import jax.numpy as jnp
from jax import lax
from jax.experimental import pallas as pl
from jax.experimental.pallas import tpu as pltpu

D_MODEL = 2048
D_FF = 5632
HEAD_DIM = 64
QB = 128
NEG_INF = -1e30
EPS = 1e-6

NSA_HEADS = 8
NSA_KV_HEADS = 2
NSA_GQA = NSA_HEADS // NSA_KV_HEADS
NSA_WIDTH = NSA_HEADS * HEAD_DIM
CMP_BLOCK = 32
CMP_STRIDE = 16
SEL_BLOCK = 64
SEL_TOPK = 8
FORCE_SCORE = 1e3
WINDOW = 512
SB_HEADS = 8
SB_WIDTH = SB_HEADS * HEAD_DIM
SSD_HEADS = 16
SSD_INNER = SSD_HEADS * HEAD_DIM
SSD_GROUPS = 2
SSD_HPG = SSD_HEADS // SSD_GROUPS
SSD_STATE = 128
SSD_CONV = 4
REL_BUCKETS = 32
REL_MAX_DIST = 128

PB_NSA_Q, PB_SEL, PB_WIN, PB_SB_Q, PB_SB_K, PB_SB_V, PB_COLS = 0, 1024, 1280, 1536, 2048, 2560, 3072
PF_Z, PF_XS, PF_BC, PF_CMP, PF_MISC, PF_COLS = 0, 1024, 2048, 2560, 2816, 3072
MISC_DT = 16

VMEM_LIMIT = 52 * 1024 * 1024

BF16 = jnp.bfloat16
F32 = jnp.float32


def _dot(a, b):
    return jnp.dot(a, b, preferred_element_type=F32)


def _dot_nt(a, b):
    return lax.dot_general(a, b, (((1,), (1,)), ((), ())), preferred_element_type=F32)


def _split2_dot(x, m):
    hi = x.astype(BF16)
    lo = (x - hi.astype(F32)).astype(BF16)
    return _dot(hi, m) + _dot(lo, m)


def _split3_dot_left(m, x):
    hi = x.astype(BF16)
    r1 = x - hi.astype(F32)
    mid = r1.astype(BF16)
    lo = (r1 - mid.astype(F32)).astype(BF16)
    return _dot(m, hi) + _dot(m, mid) + _dot(m, lo)


def _rms(x, g):
    return x * lax.rsqrt(jnp.mean(x * x, axis=-1, keepdims=True) + EPS) * g


def _silu(x):
    return x / (1.0 + jnp.exp(-x))


def _sigmoid(x):
    return 1.0 / (1.0 + jnp.exp(-x))


def _params(sem):
    return pltpu.CompilerParams(dimension_semantics=sem, vmem_limit_bytes=VMEM_LIMIT)


def _ffn_kernel(x_ref, g_ref, wg_ref, wu_ref, wd_ref, fg_ref, o_ref, h_ref, *, final):
    f = pl.program_id(1)

    @pl.when(f == 0)
    def _():
        h_ref[...] = _rms(x_ref[...], g_ref[...]).astype(BF16)

    h = h_ref[...]
    gate = _dot(h, wg_ref[...])
    up = _dot(h, wu_ref[...])
    part = _dot((_silu(gate) * up).astype(BF16), wd_ref[...])

    @pl.when(f == 0)
    def _():
        o_ref[...] = part

    @pl.when(f > 0)
    def _():
        o_ref[...] += part

    @pl.when(f == pl.num_programs(1) - 1)
    def _():
        y = x_ref[...] + 0.5 * o_ref[...]
        if final:
            y = _rms(y, fg_ref[...])
        o_ref[...] = y


def _ffn(x, g, wg, wu, wd, layer, final_g, *, final, tm=1024, tf=512):
    n, d = x.shape
    dff = wg.shape[2]
    once = pl.Buffered(1)
    return pl.pallas_call(
        functools.partial(_ffn_kernel, final=final),
        grid=(n // tm, dff // tf),
        in_specs=[
            pl.BlockSpec((tm, d), lambda i, f: (i, 0), pipeline_mode=once),
            pl.BlockSpec((1, d), lambda i, f: (0, 0)),
            pl.BlockSpec((None, d, tf), lambda i, f: (layer, 0, f)),
            pl.BlockSpec((None, d, tf), lambda i, f: (layer, 0, f)),
            pl.BlockSpec((None, tf, d), lambda i, f: (layer, f, 0)),
            pl.BlockSpec((1, d), lambda i, f: (0, 0)),
        ],
        out_specs=pl.BlockSpec((tm, d), lambda i, f: (i, 0), pipeline_mode=once),
        out_shape=jax.ShapeDtypeStruct((n, d), F32),
        scratch_shapes=[pltpu.VMEM((tm, d), BF16)],
        compiler_params=_params(("parallel", "arbitrary")),
        name="ffn",
    )(x, g.reshape(1, d), wg, wu, wd, final_g.reshape(1, d))


def _rms_matmul_kernel(x_ref, g_ref, w_ref, o_ref, h_ref):
    @pl.when(pl.program_id(1) == 0)
    def _():
        h_ref[...] = _rms(x_ref[...], g_ref[...]).astype(BF16)

    o_ref[...] = _dot(h_ref[...], w_ref[...]).astype(o_ref.dtype)


def _rms_matmul(x, g, w, out_dtype, *, tm=1024, tn=512, name):
    n, d = x.shape
    c = w.shape[1]
    return pl.pallas_call(
        _rms_matmul_kernel,
        grid=(n // tm, c // tn),
        in_specs=[
            pl.BlockSpec((tm, d), lambda i, j: (i, 0)),
            pl.BlockSpec((1, d), lambda i, j: (0, 0)),
            pl.BlockSpec((d, tn), lambda i, j: (0, j)),
        ],
        out_specs=pl.BlockSpec((tm, tn), lambda i, j: (i, j)),
        out_shape=jax.ShapeDtypeStruct((n, c), out_dtype),
        scratch_shapes=[pltpu.VMEM((tm, d), BF16)],
        compiler_params=_params(("parallel", "arbitrary")),
        name=name,
    )(x, g.reshape(1, d), w)


SB_GROUP = 8


def _sb_kernel(q_ref, k_ref, v_ref, o_ref):
    i = pl.program_id(2)
    pairs = SB_GROUP // 2
    r = lax.broadcasted_iota(jnp.int32, (QB, QB), 0)
    c = lax.broadcasted_iota(jnp.int32, (QB, QB), 1)
    strict = c < r
    low = c < HEAD_DIM
    later_ones = jnp.concatenate([(r > c).astype(BF16), jnp.ones((QB, QB), BF16)], axis=1)
    later_ones = jnp.concatenate([later_ones, later_ones], axis=0)
    zero16 = jnp.zeros((QB, QB), BF16)

    def block_diag(x):
        return jnp.concatenate([jnp.where(low, x, zero16), jnp.where(low, zero16, x)], axis=0)

    def tile(j, tails, accs, diagonal):
        rows = pl.ds(pl.multiple_of(j * QB, QB), QB)
        zs, softs, parts = [], [], []
        for h in range(SB_GROUP):
            pp = h // 2
            if h % 2 == 0:
                lanes = slice(pp * 128, (pp + 1) * 128)
                z_pair = _dot_nt(q_ref[:, lanes], block_diag(k_ref[rows, lanes]))
            z = z_pair[:, (h % 2) * QB:(h % 2 + 1) * QB]
            soft = jnp.maximum(z, 0.0) + jnp.log(1.0 + jnp.exp(-jnp.abs(z)))
            if diagonal:
                soft = jnp.where(strict, soft, 0.0)
            hi = soft.astype(BF16)
            parts.append(jnp.concatenate([hi, (soft - hi.astype(F32)).astype(BF16)], axis=1))
            zs.append(z)
            softs.append(soft)
        sums = _dot(jnp.concatenate(parts, axis=0), later_ones)
        new_tails, new_accs = [], []
        for pp in range(pairs):
            a_pair = []
            for hh in range(2):
                h = 2 * pp + hh
                sm = sums[h * QB:(h + 1) * QB]
                a = jnp.exp((zs[h] - softs[h]) - (tails[h] + sm[:, 0:QB]))
                if diagonal:
                    a = jnp.where(strict, a, 0.0)
                a_pair.append(a.astype(BF16))
                new_tails.append(tails[h] + sm[:, QB:2 * QB])
            v_bd = block_diag(v_ref[rows, pp * 128:(pp + 1) * 128])
            new_accs.append(accs[pp] + _dot(jnp.concatenate(a_pair, axis=1), v_bd))
        return new_tails, new_accs

    zeros = jnp.zeros((QB, QB), F32)
    tails, accs = tile(i, [zeros] * SB_GROUP, [zeros] * pairs, True)

    def body(step, carry):
        t, a = tile(i - 1 - step, list(carry[0]), list(carry[1]), False)
        return tuple(t), tuple(a)

    _, accs = lax.fori_loop(0, i, body, (tuple(tails), tuple(accs)))
    for pp in range(pairs):
        o_ref[:, pp * 128:(pp + 1) * 128] = accs[pp]


def _sb_attention(pb, batch, seq):
    nqb = seq // QB
    w = SB_GROUP * HEAD_DIM
    qc, kc, vc = PB_SB_Q // w, PB_SB_K // w, PB_SB_V // w
    return pl.pallas_call(
        _sb_kernel,
        grid=(batch, SB_HEADS // SB_GROUP, nqb),
        in_specs=[
            pl.BlockSpec((QB, w), lambda b, p, i: (b * nqb + i, qc + p)),
            pl.BlockSpec((seq, w), lambda b, p, i: (b, kc + p)),
            pl.BlockSpec((seq, w), lambda b, p, i: (b, vc + p)),
        ],
        out_specs=pl.BlockSpec((QB, w), lambda b, p, i: (b * nqb + i, p)),
        out_shape=jax.ShapeDtypeStruct((batch * seq, SB_WIDTH), F32),
        compiler_params=_params(("parallel", "parallel", "arbitrary")),
        name="sb_attention",
    )(pb, pb, pb)


def _nsa_cmp_kernel(q_ref, uk_ref, uv_ref, pos_ref, w_ref, bias_ref, ov_ref, ocmp_ref, sel_ref,
                    kc_ref, vc_ref, *, n_cmp, n_sel):
    i = pl.program_id(2)
    half = CMP_STRIDE * HEAD_DIM

    @pl.when(i == 0)
    def _():
        for kv, (u_ref, dst) in enumerate(((uk_ref, kc_ref), (uv_ref, vc_ref))):
            u = u_ref[0, 0]
            top = _dot((u + pos_ref[kv, 0:1, :]).astype(BF16), w_ref[kv, 0:half, :])
            bot = _dot((u + pos_ref[kv, 1:2, :]).astype(BF16), w_ref[kv, half:2 * half, :])
            dst[...] = (top + pltpu.roll(bot, QB - 1, 0)).astype(BF16)

    rows = i * QB + lax.broadcasted_iota(jnp.int32, (QB, QB), 0)
    cols = lax.broadcasted_iota(jnp.int32, (QB, QB), 1)
    valid = (rows >= cols * CMP_STRIDE + (CMP_BLOCK - 1)) & (cols < n_cmp)
    any_valid = rows >= CMP_BLOCK - 1
    kc = kc_ref[...]
    vc = vc_ref[...]
    p_sum = jnp.zeros((QB, QB), F32)
    for g in range(NSA_GQA):
        lanes = slice(g * HEAD_DIM, (g + 1) * HEAD_DIM)
        q = q_ref[:, g * 128:g * 128 + HEAD_DIM]
        s = jnp.where(valid, _dot_nt(q, kc) + bias_ref[g], NEG_INF)
        e = jnp.exp(s - jnp.max(s, axis=-1, keepdims=True))
        p = jnp.where(any_valid, e / jnp.sum(e, axis=-1, keepdims=True), 0.0)
        ocmp_ref[:, lanes] = _dot(p.astype(BF16), vc)
        p_sum = p_sum + p

    p_sel = _split2_dot(p_sum, ov_ref[...])
    cur = rows // SEL_BLOCK
    eligible = (cols * SEL_BLOCK <= rows) & (cols < n_sel)
    forced = (cols == 0) | (cols == cur) | (cols == cur - 1)
    score = jnp.where(eligible, p_sel + jnp.where(forced, FORCE_SCORE, 0.0), NEG_INF)
    rank = jnp.zeros((QB, QB), F32)
    for j in range(n_sel):
        other = score[:, j:j + 1]
        ahead = (other > score) | ((other == score) & (cols > j))
        rank = rank + jnp.where(ahead, 1.0, 0.0)
    sel_ref[0, 0] = jnp.where(eligible & (rank < SEL_TOPK), 1.0, 0.0).astype(BF16)


def _nsa_cmp(pb, ucmp, pos2, cmp_w, bias_c, overlap, batch, seq):
    nqb = seq // QB
    n_cmp = (seq - CMP_BLOCK) // CMP_STRIDE + 1
    n_sel = seq // SEL_BLOCK
    ng = seq // CMP_STRIDE
    assert ng == QB and n_sel <= QB
    wide = CMP_STRIDE * HEAD_DIM
    return pl.pallas_call(
        functools.partial(_nsa_cmp_kernel, n_cmp=n_cmp, n_sel=n_sel),
        grid=(batch, NSA_KV_HEADS, nqb),
        in_specs=[
            pl.BlockSpec((QB, NSA_GQA * 128), lambda b, h, i: (b * nqb + i, h)),
            pl.BlockSpec((1, 1, ng, wide), lambda b, h, i: (b, h, 0, 0)),
            pl.BlockSpec((1, 1, ng, wide), lambda b, h, i: (b, NSA_KV_HEADS + h, 0, 0)),
            pl.BlockSpec((2, 2, wide), lambda b, h, i: (0, 0, 0)),
            pl.BlockSpec((2, 2 * wide, HEAD_DIM), lambda b, h, i: (0, 0, 0)),
            pl.BlockSpec((NSA_GQA, QB, QB), lambda b, h, i: (h, i, 0)),
            pl.BlockSpec((QB, QB), lambda b, h, i: (0, 0)),
        ],
        out_specs=[
            pl.BlockSpec((QB, 256), lambda b, h, i: (b * nqb + i, h)),
            pl.BlockSpec((1, 1, QB, QB), lambda b, h, i: (b, h, i, 0)),
        ],
        out_shape=[
            jax.ShapeDtypeStruct((batch * seq, NSA_WIDTH), F32),
            jax.ShapeDtypeStruct((batch, NSA_KV_HEADS, seq, QB), BF16),
        ],
        scratch_shapes=[pltpu.VMEM((QB, HEAD_DIM), BF16), pltpu.VMEM((QB, HEAD_DIM), BF16)],
        compiler_params=_params(("parallel", "parallel", "arbitrary")),
        name="nsa_compressed",
    )(pb, ucmp, ucmp, pos2, cmp_w, bias_c, overlap)


SEL_CHUNK = 2 * QB
WIN_FAR = WINDOW - QB


def _nsa_sw_kernel(q_ref, ks_ref, kw_ref, ocmp_ref, misc_ref, sel_ref, exp_ref, tb_ref, o_ref, madd_ref):
    i = pl.program_id(2)
    G = NSA_GQA
    q4 = jnp.concatenate([q_ref[:, g * 128:(g + 1) * 128] for g in range(G)], axis=0)
    madd_ref[...] = (_dot(sel_ref[0, 0], exp_ref[...]) - 1.0) * (-NEG_INF)
    t_pos = i * QB + lax.broadcasted_iota(jnp.int32, (QB, 1), 0)

    def update(state, kv, add_head, add_all):
        s4 = _dot_nt(q4, kv)
        probs, scaled = [], []
        for g in range(G):
            m, l, acc = state[g]
            s = s4[g * QB:(g + 1) * QB]
            if add_head is not None:
                s = s + add_head[g * QB:(g + 1) * QB]
            if add_all is not None:
                s = s + add_all
            m_new = jnp.maximum(m, jnp.max(s, axis=-1, keepdims=True))
            p = jnp.exp(s - m_new)
            alpha = jnp.exp(m - m_new)
            probs.append(p.astype(BF16))
            scaled.append((m_new, alpha * l + jnp.sum(p, axis=-1, keepdims=True), alpha * acc))
        pv = _dot(jnp.concatenate(probs, axis=0), kv)
        return [(m, l, acc + pv[g * QB:(g + 1) * QB]) for g, (m, l, acc) in enumerate(scaled)]

    def near(kv_ref, with_sel):
        state = [(jnp.full((QB, 1), NEG_INF, F32), jnp.zeros((QB, 1), F32), jnp.zeros((QB, 128), F32))] * G
        rows0 = pl.ds(pl.multiple_of(i * QB, QB), QB)
        state = update(state, kv_ref[rows0, :], tb_ref[0, 0], madd_ref[:, rows0] if with_sel else None)
        j1 = jnp.maximum(i - 1, 0)
        rows1 = pl.ds(pl.multiple_of(j1 * QB, QB), QB)
        gone = jnp.where(i >= 1, 0.0, NEG_INF)
        extra = madd_ref[:, rows1] + gone if with_sel else jnp.full((QB, QB), gone, F32)
        return update(state, kv_ref[rows1, :], tb_ref[1, 0], extra)

    far_end = (i - 1) * QB

    def sel_far(cidx, flat):
        state = [tuple(flat[3 * g:3 * g + 3]) for g in range(G)]
        cols = pl.ds(pl.multiple_of(cidx * SEL_CHUNK, SEL_CHUNK), SEL_CHUNK)
        k_pos = cidx * SEL_CHUNK + lax.broadcasted_iota(jnp.int32, (QB, SEL_CHUNK), 1)
        add = jnp.where(k_pos < far_end, madd_ref[:, cols], NEG_INF)
        state = update(state, ks_ref[cols, :], None, add)
        return tuple(x for st in state for x in st)

    flat = lax.fori_loop(0, i // 2, sel_far, tuple(x for st in near(ks_ref, True) for x in st))
    sel_state = [tuple(flat[3 * g:3 * g + 3]) for g in range(G)]

    start = jnp.maximum(i - WINDOW // QB, 0) * QB
    rows = pl.ds(pl.multiple_of(start, QB), WIN_FAR)
    k_pos = start + lax.broadcasted_iota(jnp.int32, (QB, WIN_FAR), 1)
    live = (k_pos < far_end) & (k_pos > t_pos - WINDOW)
    win_state = update(near(kw_ref, False), kw_ref[rows, :], None, jnp.where(live, 0.0, NEG_INF))

    low = lax.broadcasted_iota(jnp.int32, (QB, 128), 1) < HEAD_DIM
    gate = _sigmoid(misc_ref[...])

    def pair(values, pp):
        a, b = values[2 * pp], values[2 * pp + 1]
        return jnp.where(low, pltpu.roll(a, HEAD_DIM, 1), b)

    def pair_gate(branch, pp):
        ca, cb = 3 * (2 * pp) + branch, 3 * (2 * pp + 1) + branch
        return jnp.where(low, gate[:, ca:ca + 1], gate[:, cb:cb + 1])

    o_sel = [acc / l for (_, l, acc) in sel_state]
    o_win = [acc / l for (_, l, acc) in win_state]
    for pp in range(G // 2):
        lanes = slice(pp * 128, (pp + 1) * 128)
        o_ref[:, lanes] = (pair_gate(0, pp) * ocmp_ref[:, lanes] + pair_gate(1, pp) * pair(o_sel, pp)
                           + pair_gate(2, pp) * pair(o_win, pp))


def _nsa_sel_win(pb, pf, o_cmp, sel, expand, tb, batch, seq):
    nqb = seq // QB
    G = NSA_GQA
    return pl.pallas_call(
        _nsa_sw_kernel,
        grid=(batch, NSA_KV_HEADS, nqb),
        in_specs=[
            pl.BlockSpec((QB, G * 128), lambda b, h, i: (b * nqb + i, h)),
            pl.BlockSpec((seq, 128), lambda b, h, i: (b, PB_SEL // 128 + h)),
            pl.BlockSpec((seq, 128), lambda b, h, i: (b, PB_WIN // 128 + h)),
            pl.BlockSpec((QB, 256), lambda b, h, i: (b * nqb + i, h)),
            pl.BlockSpec((QB, 128), lambda b, h, i: (b * nqb + i, PF_MISC // 128 + h)),
            pl.BlockSpec((1, 1, QB, QB), lambda b, h, i: (b, h, i, 0)),
            pl.BlockSpec((QB, seq), lambda b, h, i: (0, 0)),
            pl.BlockSpec((2, 1, G * QB, QB), lambda b, h, i: (0, h, 0, 0)),
        ],
        out_specs=pl.BlockSpec((QB, 256), lambda b, h, i: (b * nqb + i, h)),
        out_shape=jax.ShapeDtypeStruct((batch * seq, NSA_WIDTH), F32),
        scratch_shapes=[pltpu.VMEM((QB, seq), F32)],
        compiler_params=_params(("parallel", "parallel", "arbitrary")),
        name="nsa_selected_window",
    )(pb, pb, pb, o_cmp, pf, sel, expand, tb)


def _ssd_kernel(z_ref, xs_ref, bc_ref, misc_ref, cwx_ref, cwb_ref, cbx_ref, cbb_ref, hp_ref, ng_ref,
                o_ref, xbuf, bbuf, state, ybuf):
    ci = pl.program_id(1)
    L = QB
    P = HEAD_DIM
    GN = SSD_GROUPS * SSD_STATE

    @pl.when(ci == 0)
    def _():
        xbuf[0:8, :] = jnp.zeros((8, SSD_INNER), F32)
        bbuf[0:8, :] = jnp.zeros((8, 2 * GN), F32)
        state[...] = jnp.zeros_like(state)

    def conv_silu(buf, src_ref, w_ref, b_ref):
        buf[8:8 + L, :] = src_ref[...]
        out = b_ref[...]
        for k in range(SSD_CONV):
            lo = 8 - (SSD_CONV - 1) + k
            out = out + w_ref[k:k + 1, :] * buf[lo:lo + L, :]
        buf[0:8, :] = buf[L:L + 8, :]
        return _silu(out)

    xs = conv_silu(xbuf, xs_ref, cwx_ref, cbx_ref)
    bcs = conv_silu(bbuf, bc_ref, cwb_ref, cbb_ref)

    misc = misc_ref[...]
    pre = misc + hp_ref[0:1, :]
    dt = jnp.maximum(pre, 0.0) + jnp.log1p(jnp.exp(-jnp.abs(pre)))
    a_dt = dt * (-jnp.exp(hp_ref[1:2, :]))
    r = lax.broadcasted_iota(jnp.int32, (L, L), 0)
    c = lax.broadcasted_iota(jnp.int32, (L, L), 1)
    causal = r >= c
    a_cs = _split3_dot_left(causal.astype(BF16), a_dt)
    a_cs_t = a_cs.T
    d_skip = hp_ref[2:3, :]

    for g in range(SSD_GROUPS):
        bm = bcs[:, g * SSD_STATE:(g + 1) * SSD_STATE]
        cm = bcs[:, GN + g * SSD_STATE:GN + (g + 1) * SSD_STATE]
        cm16 = cm.astype(BF16)
        cb = _dot_nt(cm16, bm.astype(BF16))
        bm_t16 = bm.T.astype(BF16)
        for e in range(SSD_HPG):
            hd = g * SSD_HPG + e
            lanes = slice(hd * P, (hd + 1) * P)
            col = MISC_DT + hd
            x = xs[:, lanes]
            xdt = x * dt[:, col:col + 1]
            acs = a_cs[:, col:col + 1]
            seg = jnp.exp(jnp.where(causal, acs - a_cs_t[col:col + 1, :], -jnp.inf))
            y = _dot((cb * seg).astype(BF16), xdt.astype(BF16))
            h_in = state[hd]
            y = y + jnp.exp(acs) * _dot(cm16, h_in.astype(BF16))
            total = a_cs[L - 1:L, col:col + 1]
            to_end = jnp.exp(total - acs)
            state[hd] = h_in * jnp.exp(total) + _dot(bm_t16, (xdt * to_end).astype(BF16))
            y = y + d_skip[:, col:col + 1] * x
            ybuf[:, lanes] = y * _silu(z_ref[:, lanes])

    width = SSD_INNER // SSD_GROUPS
    for g in range(SSD_GROUPS):
        lanes = slice(g * width, (g + 1) * width)
        y = ybuf[:, lanes]
        o_ref[:, lanes] = y * lax.rsqrt(jnp.mean(y * y, axis=-1, keepdims=True) + EPS) * ng_ref[:, lanes]


def _ssd(pf, conv_w, conv_b, head_params, norm_g, batch, seq):
    nc = seq // QB
    GN = SSD_GROUPS * SSD_STATE
    cwx, cwb = conv_w[:, :SSD_INNER], conv_w[:, SSD_INNER:]
    cbx, cbb = conv_b[:SSD_INNER].reshape(1, -1), conv_b[SSD_INNER:].reshape(1, -1)
    full = lambda shape: pl.BlockSpec(shape, lambda b, ci: (0,) * len(shape))
    return pl.pallas_call(
        _ssd_kernel,
        grid=(batch, nc),
        in_specs=[
            pl.BlockSpec((QB, SSD_INNER), lambda b, ci: (b * nc + ci, PF_Z // SSD_INNER)),
            pl.BlockSpec((QB, SSD_INNER), lambda b, ci: (b * nc + ci, PF_XS // SSD_INNER)),
            pl.BlockSpec((QB, 2 * GN), lambda b, ci: (b * nc + ci, PF_BC // (2 * GN))),
            pl.BlockSpec((QB, 128), lambda b, ci: (b * nc + ci, PF_MISC // 128)),
            full((SSD_CONV, SSD_INNER)),
            full((SSD_CONV, 2 * GN)),
            full((1, SSD_INNER)),
            full((1, 2 * GN)),
            full((8, 128)),
            full((1, SSD_INNER)),
        ],
        out_specs=pl.BlockSpec((QB, SSD_INNER), lambda b, ci: (b * nc + ci, 0)),
        out_shape=jax.ShapeDtypeStruct((batch * seq, SSD_INNER), F32),
        scratch_shapes=[
            pltpu.VMEM((QB + 8, SSD_INNER), F32),
            pltpu.VMEM((QB + 8, 2 * GN), F32),
            pltpu.VMEM((SSD_HEADS, SSD_STATE, HEAD_DIM), F32),
            pltpu.VMEM((QB, SSD_INNER), F32),
        ],
        compiler_params=_params(("parallel", "arbitrary")),
        name="ssd",
    )(pf, pf, pf, pf, cwx, cwb, cbx, cbb, head_params, norm_g.reshape(1, -1))


def _out_proj_kernel(x_ref, nsa_ref, sb_ref, ssd_ref, gn_ref, gs_ref, w_ref, o_ref, mix_ref):
    @pl.when(pl.program_id(1) == 0)
    def _():
        mix_ref[:, 0:NSA_WIDTH] = _rms(nsa_ref[...], gn_ref[...]).astype(BF16)
        mix_ref[:, NSA_WIDTH:NSA_WIDTH + SB_WIDTH] = _rms(sb_ref[...], gs_ref[...]).astype(BF16)
        mix_ref[:, NSA_WIDTH + SB_WIDTH:] = ssd_ref[...].astype(BF16)

    o_ref[...] = x_ref[...] + _dot(mix_ref[...], w_ref[...])


def _out_proj(x, o_nsa, o_sb, o_ssd, g_nsa, g_sb, w, *, tm=1024, tn=512):
    n, d = x.shape
    dm = w.shape[0]
    return pl.pallas_call(
        _out_proj_kernel,
        grid=(n // tm, d // tn),
        in_specs=[
            pl.BlockSpec((tm, tn), lambda i, j: (i, j)),
            pl.BlockSpec((tm, NSA_WIDTH), lambda i, j: (i, 0)),
            pl.BlockSpec((tm, SB_WIDTH), lambda i, j: (i, 0)),
            pl.BlockSpec((tm, SSD_INNER), lambda i, j: (i, 0)),
            pl.BlockSpec((1, NSA_WIDTH), lambda i, j: (0, 0)),
            pl.BlockSpec((1, SB_WIDTH), lambda i, j: (0, 0)),
            pl.BlockSpec((dm, tn), lambda i, j: (0, j)),
        ],
        out_specs=pl.BlockSpec((tm, tn), lambda i, j: (i, j)),
        out_shape=jax.ShapeDtypeStruct((n, d), F32),
        scratch_shapes=[pltpu.VMEM((tm, dm), BF16)],
        compiler_params=_params(("parallel", "arbitrary")),
        name="out_proj",
    )(x, o_nsa, o_sb, o_ssd, g_nsa.reshape(1, -1), g_sb.reshape(1, -1), w)


def _rel_bucket(dist):
    dist = jnp.maximum(dist, 0)
    max_exact = REL_BUCKETS // 2
    log_ratio = jnp.log(jnp.maximum(dist, 1).astype(F32) / max_exact) / math.log(REL_MAX_DIST / max_exact)
    large = jnp.minimum(max_exact + (log_ratio * (REL_BUCKETS - max_exact)).astype(jnp.int32), REL_BUCKETS - 1)
    return jnp.where(dist < max_exact, dist, large)


def _bias_tables(rel_bias, seq):
    assert QB >= REL_MAX_DIST
    def lookup(dist):
        onehot = jax.nn.one_hot(_rel_bucket(dist), REL_BUCKETS, dtype=F32)
        return jnp.einsum('...k,kh->h...', onehot, rel_bias, precision=lax.Precision.HIGHEST)

    t = jnp.arange(seq)[:, None]
    cend = jnp.arange(QB)[None, :] * CMP_STRIDE + CMP_BLOCK - 1
    bias_c = lookup(t - cend)
    r = jnp.arange(QB)[:, None]
    m = jnp.arange(QB)[None, :]
    near = jnp.stack([lookup(r - m), lookup(QB + r - m)])
    near = near - rel_bias[REL_BUCKETS - 1][None, :, None, None]
    near = near.at[0].add(jnp.where(m > r, NEG_INF, 0.0))
    return bias_c, near.reshape(2, NSA_KV_HEADS, NSA_GQA * QB, QB)


def _expand_table(seq):
    j = jnp.arange(QB)[:, None]
    s = jnp.arange(seq)[None, :]
    return (s // SEL_BLOCK == j).astype(BF16)


def _overlap_table(seq):
    n_cmp = (seq - CMP_BLOCK) // CMP_STRIDE + 1
    cs = jnp.arange(QB)[:, None] * CMP_STRIDE
    ce = cs + CMP_BLOCK - 1
    ss = jnp.arange(QB)[None, :] * SEL_BLOCK
    ov = jnp.maximum(jnp.minimum(ce, ss + SEL_BLOCK - 1) - jnp.maximum(cs, ss) + 1, 0).astype(F32) / CMP_BLOCK
    keep = (jnp.arange(QB)[:, None] < n_cmp) & (jnp.arange(QB)[None, :] < seq // SEL_BLOCK)
    return jnp.where(keep, ov, 0.0).astype(BF16)


def _in_proj_weights(w):
    scale = HEAD_DIM ** -0.5
    o_q, o_kv, o_gate = 0, NSA_WIDTH, NSA_WIDTH + 768
    o_sb = o_gate + 3 * NSA_HEADS
    o_z = o_sb + 3 * SB_WIDTH
    o_xbc = o_z + SSD_INNER
    o_dt = o_xbc + SSD_INNER + 2 * SSD_GROUPS * SSD_STATE
    col = lambda a, n: w[:, a:a + n]
    kv = lambda br, which, h: col(o_kv + br * 256 + which * 128 + h * HEAD_DIM, HEAD_DIM)
    zeros = lambda n: jnp.zeros((w.shape[0], n), w.dtype)
    branch = lambda br: [kv(br, 0, 0), kv(br, 1, 0), kv(br, 0, 1), kv(br, 1, 1)]
    q_heads = []
    for h in range(NSA_HEADS):
        q_heads += [col(o_q + h * HEAD_DIM, HEAD_DIM) * scale, zeros(128 - HEAD_DIM)]
    wb = jnp.concatenate(
        q_heads + branch(1) + branch(2)
        + [col(o_sb, SB_WIDTH) * scale, col(o_sb + SB_WIDTH, 2 * SB_WIDTH)], axis=1)
    gates = 3 * NSA_GQA
    wf = jnp.concatenate(
        [col(o_z, SSD_INNER), col(o_xbc, SSD_INNER + 2 * SSD_GROUPS * SSD_STATE), col(o_kv, 256),
         col(o_gate, gates), zeros(MISC_DT - gates), col(o_dt, SSD_HEADS), zeros(128 - MISC_DT - SSD_HEADS),
         col(o_gate + gates, gates), zeros(128 - gates)], axis=1)
    assert wb.shape[1] == PB_COLS and wf.shape[1] == PF_COLS
    return wb.astype(BF16), wf.astype(BF16)


def _head_params(dt_bias, a_log, d_skip):
    rows = jnp.stack([dt_bias, a_log, d_skip]).astype(F32)
    return jnp.zeros((8, 128), F32).at[0:3, MISC_DT:MISC_DT + SSD_HEADS].set(rows)


def _mixer(x, l, tables, batch, seq, mix_norm, w_in, w_out, cmp_pos, cmp_w, nsa_norm, sb_norm,
           conv_w, conv_b, dt_bias, a_log, d_skip, ssd_norm):
    bias_c, near, overlap, expand = tables
    wb, wf = _in_proj_weights(w_in[l])
    pb = _rms_matmul(x, mix_norm[l], wb, BF16, name="in_proj_bf16")
    pf = _rms_matmul(x, mix_norm[l], wf, F32, name="in_proj_f32")

    ng = seq // CMP_STRIDE
    ucmp = pf[:, PF_CMP:PF_CMP + 256].reshape(batch, ng, CMP_STRIDE, 4, HEAD_DIM)
    ucmp = ucmp.transpose(0, 3, 1, 2, 4).reshape(batch, 4, ng, CMP_STRIDE * HEAD_DIM)
    pos2 = cmp_pos[l].reshape(2, 2, CMP_STRIDE * HEAD_DIM)
    o_cmp, sel = _nsa_cmp(pb, ucmp, pos2, cmp_w[l].astype(BF16), bias_c, overlap, batch, seq)
    o_nsa = _nsa_sel_win(pb, pf, o_cmp, sel, expand, near, batch, seq)
    o_sb = _sb_attention(pb, batch, seq)
    o_ssd = _ssd(pf, conv_w[l], conv_b[l], _head_params(dt_bias[l], a_log[l], d_skip[l]), ssd_norm[l],
                 batch, seq)
    return _out_proj(x, o_nsa, o_sb, o_ssd, nsa_norm[l], sb_norm[l], w_out[l].astype(BF16))


def kernel(x, rel_bias, ffn1_norm, ffn1_w_gate, ffn1_w_up, ffn1_w_down, mix_norm, w_in, w_out, nsa_cmp_pos, nsa_cmp_w, nsa_out_norm, sb_out_norm, ssd_conv_w, ssd_conv_b, ssd_dt_bias, ssd_a_log, ssd_d, ssd_out_norm, ffn2_norm, ffn2_w_gate, ffn2_w_up, ffn2_w_down, final_norm):
    batch, seq, d = x.shape
    depth = w_in.shape[0]
    tables = _bias_tables(rel_bias, seq) + (_overlap_table(seq), _expand_table(seq))
    h = x.reshape(batch * seq, d)
    ffn1_w_gate, ffn1_w_up, ffn1_w_down, ffn2_w_gate, ffn2_w_up, ffn2_w_down = (
        w.astype(BF16) for w in (ffn1_w_gate, ffn1_w_up, ffn1_w_down, ffn2_w_gate, ffn2_w_up, ffn2_w_down))
    for l in range(depth):
        h = _ffn(h, ffn1_norm[l], ffn1_w_gate, ffn1_w_up, ffn1_w_down, l, final_norm, final=False)
        h = _mixer(h, l, tables, batch, seq, mix_norm, w_in, w_out, nsa_cmp_pos, nsa_cmp_w, nsa_out_norm,
                   sb_out_norm, ssd_conv_w, ssd_conv_b, ssd_dt_bias, ssd_a_log, ssd_d, ssd_out_norm)
        h = _ffn(h, ffn2_norm[l], ffn2_w_gate, ffn2_w_up, ffn2_w_down, l, final_norm,
                 final=(l == depth - 1))
    return h.reshape(batch, seq, d)
```

```python
import functools
import math

import jax
import jax.numpy as jnp
from jax import lax
from jax.experimental import pallas as pl
from jax.experimental.pallas import tpu as pltpu

D_MODEL = 2048
D_FF = 5632
HEAD_DIM = 64
QB = 128
NEG_INF = -1e30
EPS = 1e-6

NSA_HEADS = 8
NSA_KV_HEADS = 2
NSA_GQA = NSA_HEADS // NSA_KV_HEADS
NSA_WIDTH = NSA_HEADS * HEAD_DIM
CMP_BLOCK = 32
CMP_STRIDE = 16
SEL_BLOCK = 64
SEL_TOPK = 8
FORCE_SCORE = 1e3
WINDOW = 512
SB_HEADS = 8
SB_WIDTH = SB_HEADS * HEAD_DIM
SSD_HEADS = 16
SSD_INNER = SSD_HEADS * HEAD_DIM
SSD_GROUPS = 2
SSD_HPG = SSD_HEADS // SSD_GROUPS
SSD_STATE = 128
SSD_CONV = 4
REL_BUCKETS = 32
REL_MAX_DIST = 128

PB_NSA_Q, PB_SEL, PB_WIN, PB_SB_Q, PB_SB_K, PB_SB_V, PB_COLS = 0, 1024, 1280, 1536, 2048, 2560, 3072
PF_Z, PF_XS, PF_BC, PF_CMP, PF_MISC, PF_COLS = 0, 1024, 2048, 2560, 2816, 3072
MISC_DT = 16

VMEM_LIMIT = 52 * 1024 * 1024

BF16 = jnp.bfloat16
F32 = jnp.float32


def _dot(a, b):
    return jnp.dot(a, b, preferred_element_type=F32)


def _dot_nt(a, b):
    return lax.dot_general(a, b, (((1,), (1,)), ((), ())), preferred_element_type=F32)


def _split2_dot(x, m):
    hi = x.astype(BF16)
    lo = (x - hi.astype(F32)).astype(BF16)
    return _dot(hi, m) + _dot(lo, m)


def _split3_dot(x, m):
    hi = x.astype(BF16)
    r1 = x - hi.astype(F32)
    mid = r1.astype(BF16)
    lo = (r1 - mid.astype(F32)).astype(BF16)
    return _dot(hi, m) + _dot(mid, m) + _dot(lo, m)


def _split3_dot_left(m, x):
    hi = x.astype(BF16)
    r1 = x - hi.astype(F32)
    mid = r1.astype(BF16)
    lo = (r1 - mid.astype(F32)).astype(BF16)
    return _dot(m, hi) + _dot(m, mid) + _dot(m, lo)


def _rms(x, g):
    return x * lax.rsqrt(jnp.mean(x * x, axis=-1, keepdims=True) + EPS) * g


def _silu(x):
    return x / (1.0 + jnp.exp(-x))


def _sigmoid(x):
    return 1.0 / (1.0 + jnp.exp(-x))


def _params(sem):
    return pltpu.CompilerParams(dimension_semantics=sem, vmem_limit_bytes=VMEM_LIMIT)


def _ffn_kernel(x_ref, g_ref, wg_ref, wu_ref, wd_ref, fg_ref, o_ref, h_ref, acc_ref, *, final):
    f = pl.program_id(1)

    @pl.when(f == 0)
    def _():
        h_ref[...] = _rms(x_ref[...], g_ref[...]).astype(BF16)
        acc_ref[...] = jnp.zeros_like(acc_ref)

    h = h_ref[...]
    gate = _dot(h, wg_ref[...])
    up = _dot(h, wu_ref[...])
    acc_ref[...] += _dot((_silu(gate) * up).astype(BF16), wd_ref[...])

    @pl.when(f == pl.num_programs(1) - 1)
    def _():
        y = x_ref[...] + 0.5 * acc_ref[...]
        if final:
            y = _rms(y, fg_ref[...])
        o_ref[...] = y


def _ffn(x, g, wg, wu, wd, layer, final_g, *, final, tm=512, tf=512):
    n, d = x.shape
    dff = wg.shape[2]
    return pl.pallas_call(
        functools.partial(_ffn_kernel, final=final),
        grid=(n // tm, dff // tf),
        in_specs=[
            pl.BlockSpec((tm, d), lambda i, f: (i, 0)),
            pl.BlockSpec((1, d), lambda i, f: (0, 0)),
            pl.BlockSpec((None, d, tf), lambda i, f: (layer, 0, f)),
            pl.BlockSpec((None, d, tf), lambda i, f: (layer, 0, f)),
            pl.BlockSpec((None, tf, d), lambda i, f: (layer, f, 0)),
            pl.BlockSpec((1, d), lambda i, f: (0, 0)),
        ],
        out_specs=pl.BlockSpec((tm, d), lambda i, f: (i, 0)),
        out_shape=jax.ShapeDtypeStruct((n, d), F32),
        scratch_shapes=[pltpu.VMEM((tm, d), BF16), pltpu.VMEM((tm, d), F32)],
        compiler_params=_params(("parallel", "arbitrary")),
        name="ffn",
    )(x, g.reshape(1, d), wg, wu, wd, final_g.reshape(1, d))


def _rms_matmul_kernel(x_ref, g_ref, w_ref, o_ref, h_ref):
    @pl.when(pl.program_id(1) == 0)
    def _():
        h_ref[...] = _rms(x_ref[...], g_ref[...]).astype(BF16)

    o_ref[...] = _dot(h_ref[...], w_ref[...]).astype(o_ref.dtype)


def _rms_matmul(x, g, w, out_dtype, *, tm=1024, tn=512, name):
    n, d = x.shape
    c = w.shape[1]
    return pl.pallas_call(
        _rms_matmul_kernel,
        grid=(n // tm, c // tn),
        in_specs=[
            pl.BlockSpec((tm, d), lambda i, j: (i, 0)),
            pl.BlockSpec((1, d), lambda i, j: (0, 0)),
            pl.BlockSpec((d, tn), lambda i, j: (0, j)),
        ],
        out_specs=pl.BlockSpec((tm, tn), lambda i, j: (i, j)),
        out_shape=jax.ShapeDtypeStruct((n, c), out_dtype),
        scratch_shapes=[pltpu.VMEM((tm, d), BF16)],
        compiler_params=_params(("parallel", "arbitrary")),
        name=name,
    )(x, g.reshape(1, d), w)


SB_GROUP = 8


def _sb_kernel(q_ref, k_ref, v_ref, o_ref):
    i = pl.program_id(2)
    pairs = SB_GROUP // 2
    r = lax.broadcasted_iota(jnp.int32, (QB, QB), 0)
    c = lax.broadcasted_iota(jnp.int32, (QB, QB), 1)
    strict = c < r
    low = c < HEAD_DIM
    later_ones = jnp.concatenate([(r > c).astype(BF16), jnp.ones((QB, QB), BF16)], axis=1)
    later_ones = jnp.concatenate([later_ones, later_ones], axis=0)
    zero16 = jnp.zeros((QB, QB), BF16)

    def block_diag(x):
        return jnp.concatenate([jnp.where(low, x, zero16), jnp.where(low, zero16, x)], axis=0)

    strict_all = jnp.concatenate([strict] * SB_GROUP, axis=0)

    def tile(j, tails, accs, diagonal):
        rows = pl.ds(pl.multiple_of(j * QB, QB), QB)
        zs = []
        for pp in range(pairs):
            lanes = slice(pp * 128, (pp + 1) * 128)
            z_pair = _dot_nt(q_ref[:, lanes], block_diag(k_ref[rows, lanes]))
            zs += [z_pair[:, 0:QB], z_pair[:, QB:2 * QB]]
        z = jnp.concatenate(zs, axis=0)
        soft = jnp.maximum(z, 0.0) + jnp.log(1.0 + jnp.exp(-jnp.abs(z)))
        if diagonal:
            soft = jnp.where(strict_all, soft, 0.0)
        hi = soft.astype(BF16)
        lo = (soft - hi.astype(F32)).astype(BF16)
        sums = _dot(jnp.concatenate([hi, lo], axis=1), later_ones)
        a = jnp.exp((z - soft) - (tails + sums[:, 0:QB]))
        if diagonal:
            a = jnp.where(strict_all, a, 0.0)
        a = a.astype(BF16)
        outs = []
        for pp in range(pairs):
            pair = jnp.concatenate([a[2 * pp * QB:(2 * pp + 1) * QB], a[(2 * pp + 1) * QB:(2 * pp + 2) * QB]],
                                   axis=1)
            outs.append(_dot(pair, block_diag(v_ref[rows, pp * 128:(pp + 1) * 128])))
        return tails + sums[:, QB:2 * QB], accs + jnp.concatenate(outs, axis=0)

    tails, accs = tile(i, jnp.zeros((SB_GROUP * QB, QB), F32), jnp.zeros((pairs * QB, 128), F32), True)
    _, accs = lax.fori_loop(0, i, lambda step, carry: tile(i - 1 - step, carry[0], carry[1], False),
                            (tails, accs))
    for pp in range(pairs):
        o_ref[:, pp * 128:(pp + 1) * 128] = accs[pp * QB:(pp + 1) * QB]


def _sb_attention(pb, batch, seq):
    nqb = seq // QB
    w = SB_GROUP * HEAD_DIM
    qc, kc, vc = PB_SB_Q // w, PB_SB_K // w, PB_SB_V // w
    return pl.pallas_call(
        _sb_kernel,
        grid=(batch, SB_HEADS // SB_GROUP, nqb),
        in_specs=[
            pl.BlockSpec((QB, w), lambda b, p, i: (b * nqb + i, qc + p)),
            pl.BlockSpec((seq, w), lambda b, p, i: (b, kc + p)),
            pl.BlockSpec((seq, w), lambda b, p, i: (b, vc + p)),
        ],
        out_specs=pl.BlockSpec((QB, w), lambda b, p, i: (b * nqb + i, p)),
        out_shape=jax.ShapeDtypeStruct((batch * seq, SB_WIDTH), F32),
        compiler_params=_params(("parallel", "parallel", "arbitrary")),
        name="sb_attention",
    )(pb, pb, pb)


CMP_TQ = 4 * QB


def _nsa_cmp_kernel(q_ref, uk_ref, uv_ref, pos_ref, w_ref, bias_ref, ovt_ref, ocmp_ref, sel_ref,
                    kc_ref, vc_ref, *, n_cmp, n_sel):
    i = pl.program_id(2)
    half = CMP_STRIDE * HEAD_DIM

    @pl.when(i == 0)
    def _():
        def compress(kv, u_ref):
            u = u_ref[0, 0]
            top = _dot((u + pos_ref[kv, 0:1, :]).astype(BF16), w_ref[kv, 0:half, :])
            bot = _dot((u + pos_ref[kv, 1:2, :]).astype(BF16), w_ref[kv, half:2 * half, :])
            return top + pltpu.roll(bot, QB - 1, 0)

        zeros = jnp.zeros((QB, HEAD_DIM), F32)
        kc_ref[...] = jnp.concatenate([compress(0, uk_ref), zeros], axis=1).astype(BF16)
        v = compress(1, uv_ref)
        vc_ref[...] = jnp.concatenate([jnp.concatenate([v, zeros], axis=1),
                                       jnp.concatenate([zeros, v], axis=1)], axis=0).astype(BF16)

    G = NSA_GQA
    TQ = CMP_TQ
    q4 = jnp.concatenate([q_ref[:, g * 128:(g + 1) * 128] for g in range(G)], axis=0)
    s = _dot_nt(q4, kc_ref[...]) + bias_ref[0]
    e = jnp.exp(s - jnp.max(s, axis=-1, keepdims=True))
    t_rows = i * TQ + lax.broadcasted_iota(jnp.int32, (TQ, 1), 0)
    any_valid = jnp.concatenate([t_rows >= CMP_BLOCK - 1] * G, axis=0)
    p = jnp.where(any_valid, e / jnp.sum(e, axis=-1, keepdims=True), 0.0)
    p16 = p.astype(BF16)
    for pp in range(G // 2):
        pair = jnp.concatenate([p16[2 * pp * TQ:(2 * pp + 1) * TQ], p16[(2 * pp + 1) * TQ:(2 * pp + 2) * TQ]],
                               axis=1)
        ocmp_ref[:, pp * 128:(pp + 1) * 128] = _dot(pair, vc_ref[...])
    p_all = p[0:TQ]
    for g in range(1, G):
        p_all = p_all + p[g * TQ:(g + 1) * TQ]

    blk = lax.broadcasted_iota(jnp.int32, (n_sel, QB), 0)
    for part in range(TQ // QB):
        rs = slice(part * QB, (part + 1) * QB)
        t0 = (i * (TQ // QB) + part) * QB
        p_sum = p_all[rs]
        hi = p_sum.astype(BF16)
        lo = (p_sum - hi.astype(F32)).astype(BF16)
        p_sel = (_dot_nt(ovt_ref[...], hi) + _dot_nt(ovt_ref[...], lo))[0:n_sel]
        t = t0 + lax.broadcasted_iota(jnp.int32, (n_sel, QB), 1)
        cur = t // SEL_BLOCK
        eligible = blk * SEL_BLOCK <= t
        forced = (blk == 0) | (blk == cur) | (blk == cur - 1)
        score = jnp.where(eligible, p_sel + jnp.where(forced, FORCE_SCORE, 0.0), NEG_INF)
        rank = jnp.zeros((n_sel, QB), F32)
        for j in range(n_sel):
            other = score[j:j + 1, :]
            ahead = (other > score) | ((other == score) & (blk > j))
            rank = rank + jnp.where(ahead, 1.0, 0.0)
        chosen = jnp.where(eligible & (rank < SEL_TOPK), 1.0, 0.0)
        chosen = jnp.concatenate([chosen, jnp.zeros((QB - n_sel, QB), F32)], axis=0)
        sel_ref[0, 0, rs, :] = chosen.T.astype(BF16)


def _nsa_cmp(pb, ucmp, pos2, cmp_w, bias_c, overlap_t, batch, seq):
    nqb = seq // CMP_TQ
    n_cmp = (seq - CMP_BLOCK) // CMP_STRIDE + 1
    n_sel = seq // SEL_BLOCK
    ng = seq // CMP_STRIDE
    assert ng == QB and n_sel <= QB and n_sel % 8 == 0
    wide = CMP_STRIDE * HEAD_DIM
    return pl.pallas_call(
        functools.partial(_nsa_cmp_kernel, n_cmp=n_cmp, n_sel=n_sel),
        grid=(batch, NSA_KV_HEADS, nqb),
        in_specs=[
            pl.BlockSpec((CMP_TQ, NSA_GQA * 128), lambda b, h, i: (b * nqb + i, h)),
            pl.BlockSpec((1, 1, ng, wide), lambda b, h, i: (b, h, 0, 0)),
            pl.BlockSpec((1, 1, ng, wide), lambda b, h, i: (b, NSA_KV_HEADS + h, 0, 0)),
            pl.BlockSpec((2, 2, wide), lambda b, h, i: (0, 0, 0)),
            pl.BlockSpec((2, 2 * wide, HEAD_DIM), lambda b, h, i: (0, 0, 0)),
            pl.BlockSpec((1, NSA_GQA * CMP_TQ, QB), lambda b, h, i: (h * nqb + i, 0, 0)),
            pl.BlockSpec((QB, QB), lambda b, h, i: (0, 0)),
        ],
        out_specs=[
            pl.BlockSpec((CMP_TQ, 256), lambda b, h, i: (b * nqb + i, h)),
            pl.BlockSpec((1, 1, CMP_TQ, QB), lambda b, h, i: (b, h, i, 0)),
        ],
        out_shape=[
            jax.ShapeDtypeStruct((batch * seq, NSA_WIDTH), F32),
            jax.ShapeDtypeStruct((batch, NSA_KV_HEADS, seq, QB), BF16),
        ],
        scratch_shapes=[pltpu.VMEM((QB, 128), BF16), pltpu.VMEM((2 * QB, 128), BF16)],
        compiler_params=_params(("parallel", "parallel", "arbitrary")),
        name="nsa_compressed",
    )(pb, ucmp, ucmp, pos2, cmp_w, bias_c, overlap_t)


SEL_CHUNK = 4 * QB
WIN_FAR = WINDOW - QB


def _nsa_sw_kernel(q_ref, ks_ref, kw_ref, ocmp_ref, misc_ref, sel_ref, exp_ref, tb_ref, o_ref, madd_ref):
    i = pl.program_id(2)
    G = NSA_GQA
    q4 = jnp.concatenate([q_ref[:, g * 128:(g + 1) * 128] for g in range(G)], axis=0)
    madd_ref[...] = (_dot(sel_ref[0, 0], exp_ref[...]) - 1.0) * (-NEG_INF)
    t_pos = i * QB + lax.broadcasted_iota(jnp.int32, (QB, 1), 0)

    low = lax.broadcasted_iota(jnp.int32, (QB, 128), 1) < HEAD_DIM

    def tile(x, n=G):
        return jnp.concatenate([x] * n, axis=0)

    def ones_v(kv):
        return jnp.where(tile(low, kv.shape[0] // QB), jnp.ones_like(kv), kv)

    rows0 = pl.ds(pl.multiple_of(i * QB, QB), QB)
    rows1 = pl.ds(pl.multiple_of(jnp.maximum(i - 1, 0) * QB, QB), QB)
    gone = jnp.where(i >= 1, 0.0, NEG_INF)
    far_end = (i - 1) * QB

    kv = jnp.concatenate([ks_ref[rows1, :], ks_ref[rows0, :]], axis=0)
    bias = jnp.concatenate([tb_ref[1, 0] + tile(madd_ref[:, rows1] + gone),
                            tb_ref[0, 0] + tile(madd_ref[:, rows0])], axis=1)
    s = _dot_nt(q4, kv) + bias
    m = jnp.max(s, axis=-1, keepdims=True)
    acc = _dot(jnp.exp(s - m).astype(BF16), ones_v(kv))

    def sel_far(cidx, state):
        m, acc = state
        cols = pl.ds(pl.multiple_of(cidx * SEL_CHUNK, SEL_CHUNK), SEL_CHUNK)
        k_pos = cidx * SEL_CHUNK + lax.broadcasted_iota(jnp.int32, (QB, SEL_CHUNK), 1)
        add = jnp.where(k_pos < far_end, madd_ref[:, cols], NEG_INF)
        kv = ks_ref[cols, :]
        s = _dot_nt(q4, kv) + tile(add)
        m_new = jnp.maximum(m, jnp.max(s, axis=-1, keepdims=True))
        return m_new, jnp.exp(m - m_new) * acc + _dot(jnp.exp(s - m_new).astype(BF16), ones_v(kv))

    _, sel_acc = lax.fori_loop(0, (i + 2) // (SEL_CHUNK // QB), sel_far, (m, acc))

    start = jnp.maximum(i - WINDOW // QB, 0) * QB
    rows_far = pl.ds(pl.multiple_of(start, QB), WIN_FAR)
    k_pos = start + lax.broadcasted_iota(jnp.int32, (QB, WIN_FAR), 1)
    live = (k_pos < far_end) & (k_pos > t_pos - WINDOW)
    kv = jnp.concatenate([kw_ref[rows_far, :], kw_ref[rows1, :], kw_ref[rows0, :]], axis=0)
    bias = jnp.concatenate([tile(jnp.where(live, 0.0, NEG_INF)), tb_ref[1, 0] + gone, tb_ref[0, 0]], axis=1)
    s = _dot_nt(q4, kv) + bias
    win_acc = _dot(jnp.exp(s - jnp.max(s, axis=-1, keepdims=True)).astype(BF16), ones_v(kv))

    gate = _sigmoid(misc_ref[...])

    def normalised(acc):
        return acc / jnp.where(tile(low), 1.0, pltpu.roll(acc, HEAD_DIM, 1))

    def pair(values, pp):
        a, b = values[2 * pp * QB:(2 * pp + 1) * QB], values[(2 * pp + 1) * QB:(2 * pp + 2) * QB]
        return jnp.where(low, pltpu.roll(a, HEAD_DIM, 1), b)

    def pair_gate(branch, pp):
        ca, cb = 3 * (2 * pp) + branch, 3 * (2 * pp + 1) + branch
        return jnp.where(low, gate[:, ca:ca + 1], gate[:, cb:cb + 1])

    o_sel = normalised(sel_acc)
    o_win = normalised(win_acc)
    for pp in range(G // 2):
        lanes = slice(pp * 128, (pp + 1) * 128)
        o_ref[:, lanes] = (pair_gate(0, pp) * ocmp_ref[:, lanes] + pair_gate(1, pp) * pair(o_sel, pp)
                           + pair_gate(2, pp) * pair(o_win, pp))


def _nsa_sel_win(pb, pf, o_cmp, sel, expand, tb, batch, seq):
    nqb = seq // QB
    G = NSA_GQA
    return pl.pallas_call(
        _nsa_sw_kernel,
        grid=(batch, NSA_KV_HEADS, nqb),
        in_specs=[
            pl.BlockSpec((QB, G * 128), lambda b, h, i: (b * nqb + i, h)),
            pl.BlockSpec((seq, 128), lambda b, h, i: (b, PB_SEL // 128 + h)),
            pl.BlockSpec((seq, 128), lambda b, h, i: (b, PB_WIN // 128 + h)),
            pl.BlockSpec((QB, 256), lambda b, h, i: (b * nqb + i, h)),
            pl.BlockSpec((QB, 128), lambda b, h, i: (b * nqb + i, PF_MISC // 128 + h)),
            pl.BlockSpec((1, 1, QB, QB), lambda b, h, i: (b, h, i, 0)),
            pl.BlockSpec((QB, seq), lambda b, h, i: (0, 0)),
            pl.BlockSpec((2, 1, G * QB, QB), lambda b, h, i: (0, h, 0, 0)),
        ],
        out_specs=pl.BlockSpec((QB, 256), lambda b, h, i: (b * nqb + i, h)),
        out_shape=jax.ShapeDtypeStruct((batch * seq, NSA_WIDTH), F32),
        scratch_shapes=[pltpu.VMEM((QB, seq), F32)],
        compiler_params=_params(("parallel", "parallel", "arbitrary")),
        name="nsa_selected_window",
    )(pb, pb, pb, o_cmp, pf, sel, expand, tb)


def _ssd_kernel(z_ref, xs_ref, bc_ref, misc_ref, cwx_ref, cwb_ref, cbx_ref, cbb_ref, hp_ref, spread_ref,
                dskip_ref, ng_ref, o_ref, xbuf, bbuf, state):
    ci = pl.program_id(1)
    L = QB
    P = HEAD_DIM
    GN = SSD_GROUPS * SSD_STATE

    @pl.when(ci == 0)
    def _():
        xbuf[0:8, :] = jnp.zeros((8, SSD_INNER), F32)
        bbuf[0:8, :] = jnp.zeros((8, 2 * GN), F32)
        state[...] = jnp.zeros_like(state)

    def conv_silu(buf, src_ref, w_ref, b_ref):
        buf[8:8 + L, :] = src_ref[...]
        out = b_ref[...]
        for k in range(SSD_CONV):
            lo = 8 - (SSD_CONV - 1) + k
            out = out + w_ref[k:k + 1, :] * buf[lo:lo + L, :]
        buf[0:8, :] = buf[L:L + 8, :]
        return _silu(out)

    xs = conv_silu(xbuf, xs_ref, cwx_ref, cbx_ref)
    bcs = conv_silu(bbuf, bc_ref, cwb_ref, cbb_ref)

    misc = misc_ref[...]
    pre = misc + hp_ref[0:1, :]
    dt = jnp.maximum(pre, 0.0) + jnp.log1p(jnp.exp(-jnp.abs(pre)))
    a_dt = dt * (-jnp.exp(hp_ref[1:2, :]))
    r = lax.broadcasted_iota(jnp.int32, (L, L), 0)
    c = lax.broadcasted_iota(jnp.int32, (L, L), 1)
    causal = r >= c
    a_cs = _split3_dot_left(causal.astype(BF16), a_dt)
    a_cs_t = a_cs.T

    dt_full = _split3_dot(dt, spread_ref[...])
    acs_full = _split3_dot(a_cs, spread_ref[...])
    total_full = acs_full[L - 1:L, :]
    xdt = xs * dt_full
    xdt16 = xdt.astype(BF16)
    to_end16 = (xdt * jnp.exp(total_full - acs_full)).astype(BF16)
    decay_in = jnp.exp(acs_full)
    chunk_decay = jnp.exp(total_full)
    low = c < P
    zero16 = jnp.zeros((L, 2 * P), BF16)
    width = SSD_HPG * P

    ys = []
    for g in range(SSD_GROUPS):
        bm = bcs[:, g * SSD_STATE:(g + 1) * SSD_STATE]
        cm16 = bcs[:, GN + g * SSD_STATE:GN + (g + 1) * SSD_STATE].astype(BF16)
        cb = _dot_nt(cm16, bm.astype(BF16))
        lanes = slice(g * width, (g + 1) * width)
        h_in = state[g]
        y_off = _dot(cm16, h_in.astype(BF16)) * decay_in[:, lanes]
        state[g] = h_in * chunk_decay[:, lanes] + _dot(bm.T.astype(BF16), to_end16[:, lanes])
        diag = []
        for pp in range(SSD_HPG // 2):
            hd = g * SSD_HPG + 2 * pp
            decayed = []
            for col in (MISC_DT + hd, MISC_DT + hd + 1):
                seg = jnp.exp(jnp.where(causal, a_cs[:, col:col + 1] - a_cs_t[col:col + 1, :], -jnp.inf))
                decayed.append((cb * seg).astype(BF16))
            xp = xdt16[:, hd * P:(hd + 2) * P]
            x_bd = jnp.concatenate([jnp.where(low, xp, zero16), jnp.where(low, zero16, xp)], axis=0)
            diag.append(_dot(jnp.concatenate(decayed, axis=1), x_bd))
        ys.append(jnp.concatenate(diag, axis=1) + y_off)

    y = (jnp.concatenate(ys, axis=1) + dskip_ref[...] * xs) * _silu(z_ref[...])
    for g in range(SSD_GROUPS):
        lanes = slice(g * width, (g + 1) * width)
        yg = y[:, lanes]
        o_ref[:, lanes] = yg * lax.rsqrt(jnp.mean(yg * yg, axis=-1, keepdims=True) + EPS) * ng_ref[:, lanes]


def _ssd(pf, conv_w, conv_b, head_params, d_skip, norm_g, batch, seq):
    nc = seq // QB
    GN = SSD_GROUPS * SSD_STATE
    lane_head = jnp.arange(SSD_INNER)[None, :] // HEAD_DIM
    spread = (jnp.arange(128)[:, None] == MISC_DT + lane_head).astype(BF16)
    cwx, cwb = conv_w[:, :SSD_INNER], conv_w[:, SSD_INNER:]
    cbx, cbb = conv_b[:SSD_INNER].reshape(1, -1), conv_b[SSD_INNER:].reshape(1, -1)
    full = lambda shape: pl.BlockSpec(shape, lambda b, ci: (0,) * len(shape))
    return pl.pallas_call(
        _ssd_kernel,
        grid=(batch, nc),
        in_specs=[
            pl.BlockSpec((QB, SSD_INNER), lambda b, ci: (b * nc + ci, PF_Z // SSD_INNER)),
            pl.BlockSpec((QB, SSD_INNER), lambda b, ci: (b * nc + ci, PF_XS // SSD_INNER)),
            pl.BlockSpec((QB, 2 * GN), lambda b, ci: (b * nc + ci, PF_BC // (2 * GN))),
            pl.BlockSpec((QB, 128), lambda b, ci: (b * nc + ci, PF_MISC // 128)),
            full((SSD_CONV, SSD_INNER)),
            full((SSD_CONV, 2 * GN)),
            full((1, SSD_INNER)),
            full((1, 2 * GN)),
            full((8, 128)),
            full((128, SSD_INNER)),
            full((1, SSD_INNER)),
            full((1, SSD_INNER)),
        ],
        out_specs=pl.BlockSpec((QB, SSD_INNER), lambda b, ci: (b * nc + ci, 0)),
        out_shape=jax.ShapeDtypeStruct((batch * seq, SSD_INNER), F32),
        scratch_shapes=[
            pltpu.VMEM((QB + 8, SSD_INNER), F32),
            pltpu.VMEM((QB + 8, 2 * GN), F32),
            pltpu.VMEM((SSD_GROUPS, SSD_STATE, SSD_HPG * HEAD_DIM), F32),
        ],
        compiler_params=_params(("parallel", "arbitrary")),
        name="ssd",
    )(pf, pf, pf, pf, cwx, cwb, cbx, cbb, head_params, spread,
      jnp.repeat(d_skip.astype(F32), HEAD_DIM).reshape(1, -1), norm_g.reshape(1, -1))


def _out_proj_kernel(x_ref, nsa_ref, sb_ref, ssd_ref, gn_ref, gs_ref, w_ref, o_ref, mix_ref):
    @pl.when(pl.program_id(1) == 0)
    def _():
        mix_ref[:, 0:NSA_WIDTH] = _rms(nsa_ref[...], gn_ref[...]).astype(BF16)
        mix_ref[:, NSA_WIDTH:NSA_WIDTH + SB_WIDTH] = _rms(sb_ref[...], gs_ref[...]).astype(BF16)
        mix_ref[:, NSA_WIDTH + SB_WIDTH:] = ssd_ref[...].astype(BF16)

    o_ref[...] = x_ref[...] + _dot(mix_ref[...], w_ref[...])


def _out_proj(x, o_nsa, o_sb, o_ssd, g_nsa, g_sb, w, *, tm=1024, tn=512):
    n, d = x.shape
    dm = w.shape[0]
    return pl.pallas_call(
        _out_proj_kernel,
        grid=(n // tm, d // tn),
        in_specs=[
            pl.BlockSpec((tm, tn), lambda i, j: (i, j)),
            pl.BlockSpec((tm, NSA_WIDTH), lambda i, j: (i, 0)),
            pl.BlockSpec((tm, SB_WIDTH), lambda i, j: (i, 0)),
            pl.BlockSpec((tm, SSD_INNER), lambda i, j: (i, 0)),
            pl.BlockSpec((1, NSA_WIDTH), lambda i, j: (0, 0)),
            pl.BlockSpec((1, SB_WIDTH), lambda i, j: (0, 0)),
            pl.BlockSpec((dm, tn), lambda i, j: (0, j)),
        ],
        out_specs=pl.BlockSpec((tm, tn), lambda i, j: (i, j)),
        out_shape=jax.ShapeDtypeStruct((n, d), F32),
        scratch_shapes=[pltpu.VMEM((tm, dm), BF16)],
        compiler_params=_params(("parallel", "arbitrary")),
        name="out_proj",
    )(x, o_nsa, o_sb, o_ssd, g_nsa.reshape(1, -1), g_sb.reshape(1, -1), w)


def _rel_bucket(dist):
    dist = jnp.maximum(dist, 0)
    max_exact = REL_BUCKETS // 2
    log_ratio = jnp.log(jnp.maximum(dist, 1).astype(F32) / max_exact) / math.log(REL_MAX_DIST / max_exact)
    large = jnp.minimum(max_exact + (log_ratio * (REL_BUCKETS - max_exact)).astype(jnp.int32), REL_BUCKETS - 1)
    return jnp.where(dist < max_exact, dist, large)


def _bias_tables(rel_bias, seq):
    assert QB >= REL_MAX_DIST
    def lookup(dist):
        onehot = jax.nn.one_hot(_rel_bucket(dist), REL_BUCKETS, dtype=F32)
        return jnp.einsum('...k,kh->h...', onehot, rel_bias, precision=lax.Precision.HIGHEST)

    t = jnp.arange(seq)[:, None]
    cend = jnp.arange(QB)[None, :] * CMP_STRIDE + CMP_BLOCK - 1
    n_cmp = (seq - CMP_BLOCK) // CMP_STRIDE + 1
    valid_c = (t >= cend) & (jnp.arange(QB)[None, :] < n_cmp)
    bias_c = jnp.where(valid_c, lookup(t - cend), NEG_INF)
    steps = seq // CMP_TQ
    bias_c = bias_c.reshape(NSA_KV_HEADS, NSA_GQA, steps, CMP_TQ, QB).transpose(0, 2, 1, 3, 4)
    bias_c = bias_c.reshape(NSA_KV_HEADS * steps, NSA_GQA * CMP_TQ, QB)
    r = jnp.arange(QB)[:, None]
    m = jnp.arange(QB)[None, :]
    near = jnp.stack([lookup(r - m), lookup(QB + r - m)])
    near = near - rel_bias[REL_BUCKETS - 1][None, :, None, None]
    near = near.at[0].add(jnp.where(m > r, NEG_INF, 0.0))
    return bias_c, near.reshape(2, NSA_KV_HEADS, NSA_GQA * QB, QB)


def _expand_table(seq):
    j = jnp.arange(QB)[:, None]
    s = jnp.arange(seq)[None, :]
    return (s // SEL_BLOCK == j).astype(BF16)


def _overlap_table(seq):
    n_cmp = (seq - CMP_BLOCK) // CMP_STRIDE + 1
    cs = jnp.arange(QB)[:, None] * CMP_STRIDE
    ce = cs + CMP_BLOCK - 1
    ss = jnp.arange(QB)[None, :] * SEL_BLOCK
    ov = jnp.maximum(jnp.minimum(ce, ss + SEL_BLOCK - 1) - jnp.maximum(cs, ss) + 1, 0).astype(F32) / CMP_BLOCK
    keep = (jnp.arange(QB)[:, None] < n_cmp) & (jnp.arange(QB)[None, :] < seq // SEL_BLOCK)
    return jnp.where(keep, ov, 0.0).T.astype(BF16)


def _in_proj_weights(w):
    scale = HEAD_DIM ** -0.5
    o_q, o_kv, o_gate = 0, NSA_WIDTH, NSA_WIDTH + 768
    o_sb = o_gate + 3 * NSA_HEADS
    o_z = o_sb + 3 * SB_WIDTH
    o_xbc = o_z + SSD_INNER
    o_dt = o_xbc + SSD_INNER + 2 * SSD_GROUPS * SSD_STATE
    col = lambda a, n: w[:, a:a + n]
    kv = lambda br, which, h: col(o_kv + br * 256 + which * 128 + h * HEAD_DIM, HEAD_DIM)
    zeros = lambda n: jnp.zeros((w.shape[0], n), w.dtype)
    branch = lambda br: [kv(br, 0, 0), kv(br, 1, 0), kv(br, 0, 1), kv(br, 1, 1)]
    q_heads = []
    for h in range(NSA_HEADS):
        q_heads += [col(o_q + h * HEAD_DIM, HEAD_DIM) * scale, zeros(128 - HEAD_DIM)]
    wb = jnp.concatenate(
        q_heads + branch(1) + branch(2)
        + [col(o_sb, SB_WIDTH) * scale, col(o_sb + SB_WIDTH, 2 * SB_WIDTH)], axis=1)
    gates = 3 * NSA_GQA
    wf = jnp.concatenate(
        [col(o_z, SSD_INNER), col(o_xbc, SSD_INNER + 2 * SSD_GROUPS * SSD_STATE), col(o_kv, 256),
         col(o_gate, gates), zeros(MISC_DT - gates), col(o_dt, SSD_HEADS), zeros(128 - MISC_DT - SSD_HEADS),
         col(o_gate + gates, gates), zeros(128 - gates)], axis=1)
    assert wb.shape[1] == PB_COLS and wf.shape[1] == PF_COLS
    return wb.astype(BF16), wf.astype(BF16)


def _head_params(dt_bias, a_log):
    rows = jnp.stack([dt_bias, a_log]).astype(F32)
    return jnp.zeros((8, 128), F32).at[0:2, MISC_DT:MISC_DT + SSD_HEADS].set(rows)


def _mixer(x, l, tables, batch, seq, mix_norm, w_in, w_out, cmp_pos, cmp_w, nsa_norm, sb_norm,
           conv_w, conv_b, dt_bias, a_log, d_skip, ssd_norm):
    bias_c, near, overlap, expand = tables
    wb, wf = _in_proj_weights(w_in[l])
    pb = _rms_matmul(x, mix_norm[l], wb, BF16, name="in_proj_bf16")
    pf = _rms_matmul(x, mix_norm[l], wf, F32, name="in_proj_f32")

    ng = seq // CMP_STRIDE
    ucmp = pf[:, PF_CMP:PF_CMP + 256].reshape(batch, ng, CMP_STRIDE, 4, HEAD_DIM)
    ucmp = ucmp.transpose(0, 3, 1, 2, 4).reshape(batch, 4, ng, CMP_STRIDE * HEAD_DIM)
    pos2 = cmp_pos[l].reshape(2, 2, CMP_STRIDE * HEAD_DIM)
    o_cmp, sel = _nsa_cmp(pb, ucmp, pos2, cmp_w[l].astype(BF16), bias_c, overlap, batch, seq)
    o_nsa = _nsa_sel_win(pb, pf, o_cmp, sel, expand, near, batch, seq)
    o_sb = _sb_attention(pb, batch, seq)
    o_ssd = _ssd(pf, conv_w[l], conv_b[l], _head_params(dt_bias[l], a_log[l]), d_skip[l], ssd_norm[l],
                 batch, seq)
    return _out_proj(x, o_nsa, o_sb, o_ssd, nsa_norm[l], sb_norm[l], w_out[l].astype(BF16))


def kernel(x, rel_bias, ffn1_norm, ffn1_w_gate, ffn1_w_up, ffn1_w_down, mix_norm, w_in, w_out, nsa_cmp_pos, nsa_cmp_w, nsa_out_norm, sb_out_norm, ssd_conv_w, ssd_conv_b, ssd_dt_bias, ssd_a_log, ssd_d, ssd_out_norm, ffn2_norm, ffn2_w_gate, ffn2_w_up, ffn2_w_down, final_norm):
    batch, seq, d = x.shape
    depth = w_in.shape[0]
    tables = _bias_tables(rel_bias, seq) + (_overlap_table(seq), _expand_table(seq))
    h = x.reshape(batch * seq, d)
    ffn1_w_gate, ffn1_w_up, ffn1_w_down, ffn2_w_gate, ffn2_w_up, ffn2_w_down = (
        w.astype(BF16) for w in (ffn1_w_gate, ffn1_w_up, ffn1_w_down, ffn2_w_gate, ffn2_w_up, ffn2_w_down))
    for l in range(depth):
        h = _ffn(h, ffn1_norm[l], ffn1_w_gate, ffn1_w_up, ffn1_w_down, l, final_norm, final=False)
        h = _mixer(h, l, tables, batch, seq, mix_norm, w_in, w_out, nsa_cmp_pos, nsa_cmp_w, nsa_out_norm,
                   sb_out_norm, ssd_conv_w, ssd_conv_b, ssd_dt_bias, ssd_a_log, ssd_d, ssd_out_norm)
        h = _ffn(h, ffn2_norm[l], ffn2_w_gate, ffn2_w_up, ffn2_w_down, l, final_norm,
                 final=(l == depth - 1))
    return h.reshape(batch, seq, d)
```

```python
import functools
import math

import jax
import jax.numpy as jnp
from jax import lax
from jax.experimental import pallas as pl
from jax.experimental.pallas import tpu as pltpu

D_MODEL = 2048
D_FF = 5632
HEAD_DIM = 64
QB = 128
NEG_INF = -1e30
EPS = 1e-6

NSA_HEADS = 8
NSA_KV_HEADS = 2
NSA_GQA = NSA_HEADS // NSA_KV_HEADS
NSA_WIDTH = NSA_HEADS * HEAD_DIM
CMP_BLOCK = 32
CMP_STRIDE = 16
SEL_BLOCK = 64
SEL_TOPK = 8
FORCE_SCORE = 1e3
WINDOW = 512
SB_HEADS = 8
SB_WIDTH = SB_HEADS * HEAD_DIM
SSD_HEADS = 16
SSD_INNER = SSD_HEADS * HEAD_DIM
SSD_GROUPS = 2
SSD_HPG = SSD_HEADS // SSD_GROUPS
SSD_STATE = 128
SSD_CONV = 4
REL_BUCKETS = 32
REL_MAX_DIST = 128

PB_NSA_Q, PB_SEL, PB_WIN, PB_SB_Q, PB_SB_K, PB_SB_V, PB_COLS = 0, 1024, 1280, 1536, 2048, 2560, 3072
PF_Z, PF_XS, PF_BC, PF_CMP, PF_MISC, PF_COLS = 0, 1024, 2048, 2560, 2816, 3072
MISC_DT = 16

VMEM_LIMIT = 52 * 1024 * 1024

BF16 = jnp.bfloat16
F32 = jnp.float32


def _dot(a, b):
    return jnp.dot(a, b, preferred_element_type=F32)


def _dot_nt(a, b):
    return lax.dot_general(a, b, (((1,), (1,)), ((), ())), preferred_element_type=F32)


def _split2_dot(x, m):
    hi = x.astype(BF16)
    lo = (x - hi.astype(F32)).astype(BF16)
    return _dot(hi, m) + _dot(lo, m)


def _split3_dot(x, m):
    hi = x.astype(BF16)
    r1 = x - hi.astype(F32)
    mid = r1.astype(BF16)
    lo = (r1 - mid.astype(F32)).astype(BF16)
    return _dot(hi, m) + _dot(mid, m) + _dot(lo, m)


def _split3_dot_left(m, x):
    hi = x.astype(BF16)
    r1 = x - hi.astype(F32)
    mid = r1.astype(BF16)
    lo = (r1 - mid.astype(F32)).astype(BF16)
    return _dot(m, hi) + _dot(m, mid) + _dot(m, lo)


def _rms(x, g):
    return x * lax.rsqrt(jnp.mean(x * x, axis=-1, keepdims=True) + EPS) * g


def _silu(x):
    return x / (1.0 + jnp.exp(-x))


def _sigmoid(x):
    return 1.0 / (1.0 + jnp.exp(-x))


def _params(sem):
    return pltpu.CompilerParams(dimension_semantics=sem, vmem_limit_bytes=VMEM_LIMIT)


def _ffn_kernel(x_ref, g_ref, wg_ref, wu_ref, wd_ref, fg_ref, o_ref, h_ref, acc_ref, *, final):
    f = pl.program_id(1)

    @pl.when(f == 0)
    def _():
        h_ref[...] = _rms(x_ref[...], g_ref[...]).astype(BF16)
        acc_ref[...] = jnp.zeros_like(acc_ref)

    h = h_ref[...]
    gate = _dot(h, wg_ref[...])
    up = _dot(h, wu_ref[...])
    acc_ref[...] += _dot((_silu(gate) * up).astype(BF16), wd_ref[...])

    @pl.when(f == pl.num_programs(1) - 1)
    def _():
        y = x_ref[...] + 0.5 * acc_ref[...]
        if final:
            y = _rms(y, fg_ref[...])
        o_ref[...] = y


def _ffn(x, g, wg, wu, wd, layer, final_g, *, final, tm=512, tf=512):
    n, d = x.shape
    dff = wg.shape[2]
    return pl.pallas_call(
        functools.partial(_ffn_kernel, final=final),
        grid=(n // tm, dff // tf),
        in_specs=[
            pl.BlockSpec((tm, d), lambda i, f: (i, 0)),
            pl.BlockSpec((1, d), lambda i, f: (0, 0)),
            pl.BlockSpec((None, d, tf), lambda i, f: (layer, 0, f)),
            pl.BlockSpec((None, d, tf), lambda i, f: (layer, 0, f)),
            pl.BlockSpec((None, tf, d), lambda i, f: (layer, f, 0)),
            pl.BlockSpec((1, d), lambda i, f: (0, 0)),
        ],
        out_specs=pl.BlockSpec((tm, d), lambda i, f: (i, 0)),
        out_shape=jax.ShapeDtypeStruct((n, d), F32),
        scratch_shapes=[pltpu.VMEM((tm, d), BF16), pltpu.VMEM((tm, d), F32)],
        compiler_params=_params(("parallel", "arbitrary")),
        name="ffn",
    )(x, g.reshape(1, d), wg, wu, wd, final_g.reshape(1, d))


def _rms_matmul_kernel(x_ref, g_ref, w_ref, o_ref, h_ref):
    @pl.when(pl.program_id(1) == 0)
    def _():
        h_ref[...] = _rms(x_ref[...], g_ref[...]).astype(BF16)

    o_ref[...] = _dot(h_ref[...], w_ref[...]).astype(o_ref.dtype)


def _rms_matmul(x, g, w, out_dtype, *, tm=1024, tn=512, name):
    n, d = x.shape
    c = w.shape[1]
    return pl.pallas_call(
        _rms_matmul_kernel,
        grid=(n // tm, c // tn),
        in_specs=[
            pl.BlockSpec((tm, d), lambda i, j: (i, 0)),
            pl.BlockSpec((1, d), lambda i, j: (0, 0)),
            pl.BlockSpec((d, tn), lambda i, j: (0, j)),
        ],
        out_specs=pl.BlockSpec((tm, tn), lambda i, j: (i, j)),
        out_shape=jax.ShapeDtypeStruct((n, c), out_dtype),
        scratch_shapes=[pltpu.VMEM((tm, d), BF16)],
        compiler_params=_params(("parallel", "arbitrary")),
        name=name,
    )(x, g.reshape(1, d), w)


SB_GROUP = 8
SB_DEAD = 104.0


def _sb_kernel(q_ref, k_ref, v_ref, o_ref, kbd_ref, vbd_ref):
    i = pl.program_id(2)
    pairs = SB_GROUP // 2
    nkb = k_ref.shape[0] // QB
    r = lax.broadcasted_iota(jnp.int32, (QB, QB), 0)
    c = lax.broadcasted_iota(jnp.int32, (QB, QB), 1)
    strict_all = jnp.concatenate([c < r] * SB_GROUP, axis=0)
    later_ones = jnp.concatenate([(r > c).astype(BF16), jnp.ones((QB, QB), BF16)], axis=1)
    later_ones = jnp.concatenate([later_ones, later_ones], axis=0)

    @pl.when(i == 0)
    def _():
        low = c < HEAD_DIM
        zero16 = jnp.zeros((QB, QB), BF16)

        def build(j, carry):
            src = pl.ds(pl.multiple_of(j * QB, QB), QB)
            dst = pl.ds(pl.multiple_of(j * 2 * QB, 2 * QB), 2 * QB)
            for pp in range(pairs):
                for ref, out in ((k_ref, kbd_ref), (v_ref, vbd_ref)):
                    x = ref[src, pp * 128:(pp + 1) * 128]
                    out[pp, dst, :] = jnp.concatenate([jnp.where(low, x, zero16), jnp.where(low, zero16, x)],
                                                      axis=0)
            return carry

        lax.fori_loop(0, nkb, build, 0)

    def scores(j, diagonal):
        rows = pl.ds(pl.multiple_of(j * 2 * QB, 2 * QB), 2 * QB)
        zs = []
        for pp in range(pairs):
            z_pair = _dot_nt(q_ref[:, pp * 128:(pp + 1) * 128], kbd_ref[pp, rows, :])
            zs += [z_pair[:, 0:QB], z_pair[:, QB:2 * QB]]
        z = jnp.concatenate(zs, axis=0)
        soft = jnp.maximum(z, 0.0) + jnp.log(1.0 + jnp.exp(-jnp.abs(z)))
        if diagonal:
            soft = jnp.where(strict_all, soft, 0.0)
        hi = soft.astype(BF16)
        lo = (soft - hi.astype(F32)).astype(BF16)
        sums = _dot(jnp.concatenate([hi, lo], axis=1), later_ones)
        return rows, z - soft, sums

    def absorb(tails, accs, rows, log_beta, sums, diagonal):
        a = jnp.exp(log_beta - (tails + sums[:, 0:QB]))
        if diagonal:
            a = jnp.where(strict_all, a, 0.0)
        a = a.astype(BF16)
        outs = []
        for pp in range(pairs):
            pair = jnp.concatenate([a[2 * pp * QB:(2 * pp + 1) * QB], a[(2 * pp + 1) * QB:(2 * pp + 2) * QB]],
                                   axis=1)
            outs.append(_dot(pair, vbd_ref[pp, rows, :]))
        return tails + sums[:, QB:2 * QB], accs + jnp.concatenate(outs, axis=0)

    def sweep(js, carry):
        staged = [scores(j, False) for j in js]
        for st in staged:
            carry = absorb(*carry, *st, False)
        return carry

    carry = (jnp.zeros((SB_GROUP * QB, QB), F32), jnp.zeros((pairs * QB, 128), F32))
    carry = absorb(*carry, *scores(i, True), True)
    odd = i % 2
    carry = lax.fori_loop(0, odd, lambda step, cr: sweep([i - 1], cr), carry)
    top = i - 1 - odd

    def live(state):
        step, smallest, _, _ = state
        return (step < i // 2) & (smallest < SB_DEAD)

    def pair_step(state):
        step, _, tails, accs = state
        tails, accs = sweep([top - 2 * step, top - 2 * step - 1], (tails, accs))
        return step + 1, jnp.min(tails), tails, accs

    accs = lax.while_loop(live, pair_step, (0, jnp.min(carry[0]), *carry))[3]
    for pp in range(pairs):
        o_ref[:, pp * 128:(pp + 1) * 128] = accs[pp * QB:(pp + 1) * QB]


def _sb_attention(pb, batch, seq):
    nqb = seq // QB
    w = SB_GROUP * HEAD_DIM
    qc, kc, vc = PB_SB_Q // w, PB_SB_K // w, PB_SB_V // w
    return pl.pallas_call(
        _sb_kernel,
        grid=(batch, SB_HEADS // SB_GROUP, nqb),
        in_specs=[
            pl.BlockSpec((QB, w), lambda b, p, i: (b * nqb + i, qc + p)),
            pl.BlockSpec((seq, w), lambda b, p, i: (b, kc + p)),
            pl.BlockSpec((seq, w), lambda b, p, i: (b, vc + p)),
        ],
        out_specs=pl.BlockSpec((QB, w), lambda b, p, i: (b * nqb + i, p)),
        out_shape=jax.ShapeDtypeStruct((batch * seq, SB_WIDTH), F32),
        scratch_shapes=[pltpu.VMEM((SB_GROUP // 2, 2 * seq, 128), BF16),
                        pltpu.VMEM((SB_GROUP // 2, 2 * seq, 128), BF16)],
        compiler_params=_params(("parallel", "parallel", "arbitrary")),
        name="sb_attention",
    )(pb, pb, pb)


CMP_TQ = 4 * QB


def _nsa_cmp_kernel(q_ref, uk_ref, uv_ref, pos_ref, w_ref, bias_ref, ovt_ref, ocmp_ref, sel_ref,
                    kc_ref, vc_ref, *, n_cmp, n_sel):
    i = pl.program_id(2)
    half = CMP_STRIDE * HEAD_DIM

    @pl.when(i == 0)
    def _():
        def compress(kv, u_ref):
            u = u_ref[0, 0]
            top = _dot((u + pos_ref[kv, 0:1, :]).astype(BF16), w_ref[kv, 0:half, :])
            bot = _dot((u + pos_ref[kv, 1:2, :]).astype(BF16), w_ref[kv, half:2 * half, :])
            return top + pltpu.roll(bot, QB - 1, 0)

        zeros = jnp.zeros((QB, HEAD_DIM), F32)
        kc_ref[...] = jnp.concatenate([compress(0, uk_ref), zeros], axis=1).astype(BF16)
        v = compress(1, uv_ref)
        vc_ref[...] = jnp.concatenate([jnp.concatenate([v, zeros], axis=1),
                                       jnp.concatenate([zeros, v], axis=1)], axis=0).astype(BF16)

    G = NSA_GQA
    TQ = CMP_TQ
    q4 = jnp.concatenate([q_ref[:, g * 128:(g + 1) * 128] for g in range(G)], axis=0)
    s = _dot_nt(q4, kc_ref[...]) + bias_ref[0]
    e = jnp.exp(s - jnp.max(s, axis=-1, keepdims=True))
    t_rows = i * TQ + lax.broadcasted_iota(jnp.int32, (TQ, 1), 0)
    any_valid = jnp.concatenate([t_rows >= CMP_BLOCK - 1] * G, axis=0)
    p = jnp.where(any_valid, e / jnp.sum(e, axis=-1, keepdims=True), 0.0)
    p16 = p.astype(BF16)
    for pp in range(G // 2):
        pair = jnp.concatenate([p16[2 * pp * TQ:(2 * pp + 1) * TQ], p16[(2 * pp + 1) * TQ:(2 * pp + 2) * TQ]],
                               axis=1)
        ocmp_ref[:, pp * 128:(pp + 1) * 128] = _dot(pair, vc_ref[...])
    p_all = p[0:TQ]
    for g in range(1, G):
        p_all = p_all + p[g * TQ:(g + 1) * TQ]

    blk = lax.broadcasted_iota(jnp.int32, (n_sel, QB), 0)
    for part in range(TQ // QB):
        rs = slice(part * QB, (part + 1) * QB)
        t0 = (i * (TQ // QB) + part) * QB
        p_sum = p_all[rs]
        hi = p_sum.astype(BF16)
        lo = (p_sum - hi.astype(F32)).astype(BF16)
        p_sel = (_dot_nt(ovt_ref[...], hi) + _dot_nt(ovt_ref[...], lo))[0:n_sel]
        t = t0 + lax.broadcasted_iota(jnp.int32, (n_sel, QB), 1)
        cur = t // SEL_BLOCK
        eligible = blk * SEL_BLOCK <= t
        forced = (blk == 0) | (blk == cur) | (blk == cur - 1)
        score = jnp.where(eligible, p_sel + jnp.where(forced, FORCE_SCORE, 0.0), NEG_INF)
        rank = jnp.zeros((n_sel, QB), F32)
        for j in range(n_sel):
            other = score[j:j + 1, :]
            ahead = (other > score) | ((other == score) & (blk > j))
            rank = rank + jnp.where(ahead, 1.0, 0.0)
        chosen = jnp.where(eligible & (rank < SEL_TOPK), 1.0, 0.0)
        chosen = jnp.concatenate([chosen, jnp.zeros((QB - n_sel, QB), F32)], axis=0)
        sel_ref[0, 0, rs, :] = chosen.T.astype(BF16)


def _nsa_cmp(pb, ucmp, pos2, cmp_w, bias_c, overlap_t, batch, seq):
    nqb = seq // CMP_TQ
    n_cmp = (seq - CMP_BLOCK) // CMP_STRIDE + 1
    n_sel = seq // SEL_BLOCK
    ng = seq // CMP_STRIDE
    assert ng == QB and n_sel <= QB and n_sel % 8 == 0
    wide = CMP_STRIDE * HEAD_DIM
    return pl.pallas_call(
        functools.partial(_nsa_cmp_kernel, n_cmp=n_cmp, n_sel=n_sel),
        grid=(batch, NSA_KV_HEADS, nqb),
        in_specs=[
            pl.BlockSpec((CMP_TQ, NSA_GQA * 128), lambda b, h, i: (b * nqb + i, h)),
            pl.BlockSpec((1, 1, ng, wide), lambda b, h, i: (b, h, 0, 0)),
            pl.BlockSpec((1, 1, ng, wide), lambda b, h, i: (b, NSA_KV_HEADS + h, 0, 0)),
            pl.BlockSpec((2, 2, wide), lambda b, h, i: (0, 0, 0)),
            pl.BlockSpec((2, 2 * wide, HEAD_DIM), lambda b, h, i: (0, 0, 0)),
            pl.BlockSpec((1, NSA_GQA * CMP_TQ, QB), lambda b, h, i: (h * nqb + i, 0, 0)),
            pl.BlockSpec((QB, QB), lambda b, h, i: (0, 0)),
        ],
        out_specs=[
            pl.BlockSpec((CMP_TQ, 256), lambda b, h, i: (b * nqb + i, h)),
            pl.BlockSpec((1, 1, CMP_TQ, QB), lambda b, h, i: (b, h, i, 0)),
        ],
        out_shape=[
            jax.ShapeDtypeStruct((batch * seq, NSA_WIDTH), F32),
            jax.ShapeDtypeStruct((batch, NSA_KV_HEADS, seq, QB), BF16),
        ],
        scratch_shapes=[pltpu.VMEM((QB, 128), BF16), pltpu.VMEM((2 * QB, 128), BF16)],
        compiler_params=_params(("parallel", "parallel", "arbitrary")),
        name="nsa_compressed",
    )(pb, ucmp, ucmp, pos2, cmp_w, bias_c, overlap_t)


SEL_CHUNK = 4 * QB
WIN_FAR = WINDOW - QB


def _nsa_sw_kernel(q_ref, ks_ref, kw_ref, ocmp_ref, misc_ref, sel_ref, exp_ref, tb_ref, o_ref, madd_ref):
    i = pl.program_id(2)
    G = NSA_GQA
    q4 = jnp.concatenate([q_ref[:, g * 128:(g + 1) * 128] for g in range(G)], axis=0)
    madd_ref[...] = (_dot(sel_ref[0, 0], exp_ref[...]) - 1.0) * (-NEG_INF)
    t_pos = i * QB + lax.broadcasted_iota(jnp.int32, (QB, 1), 0)

    low = lax.broadcasted_iota(jnp.int32, (QB, 128), 1) < HEAD_DIM

    def tile(x, n=G):
        return jnp.concatenate([x] * n, axis=0)

    def ones_v(kv):
        return jnp.where(tile(low, kv.shape[0] // QB), jnp.ones_like(kv), kv)

    rows0 = pl.ds(pl.multiple_of(i * QB, QB), QB)
    rows1 = pl.ds(pl.multiple_of(jnp.maximum(i - 1, 0) * QB, QB), QB)
    gone = jnp.where(i >= 1, 0.0, NEG_INF)
    far_end = (i - 1) * QB

    kv = jnp.concatenate([ks_ref[rows1, :], ks_ref[rows0, :]], axis=0)
    bias = jnp.concatenate([tb_ref[1, 0] + tile(madd_ref[:, rows1] + gone),
                            tb_ref[0, 0] + tile(madd_ref[:, rows0])], axis=1)
    s = _dot_nt(q4, kv) + bias
    m = jnp.max(s, axis=-1, keepdims=True)
    acc = _dot(jnp.exp(s - m).astype(BF16), ones_v(kv))

    def sel_far(cidx, state):
        m, acc = state
        cols = pl.ds(pl.multiple_of(cidx * SEL_CHUNK, SEL_CHUNK), SEL_CHUNK)
        k_pos = cidx * SEL_CHUNK + lax.broadcasted_iota(jnp.int32, (QB, SEL_CHUNK), 1)
        add = jnp.where(k_pos < far_end, madd_ref[:, cols], NEG_INF)
        kv = ks_ref[cols, :]
        s = _dot_nt(q4, kv) + tile(add)
        m_new = jnp.maximum(m, jnp.max(s, axis=-1, keepdims=True))
        return m_new, jnp.exp(m - m_new) * acc + _dot(jnp.exp(s - m_new).astype(BF16), ones_v(kv))

    _, sel_acc = lax.fori_loop(0, (i + 2) // (SEL_CHUNK // QB), sel_far, (m, acc))

    start = jnp.maximum(i - WINDOW // QB, 0) * QB
    rows_far = pl.ds(pl.multiple_of(start, QB), WIN_FAR)
    k_pos = start + lax.broadcasted_iota(jnp.int32, (QB, WIN_FAR), 1)
    live = (k_pos < far_end) & (k_pos > t_pos - WINDOW)
    kv = jnp.concatenate([kw_ref[rows_far, :], kw_ref[rows1, :], kw_ref[rows0, :]], axis=0)
    bias = jnp.concatenate([tile(jnp.where(live, 0.0, NEG_INF)), tb_ref[1, 0] + gone, tb_ref[0, 0]], axis=1)
    s = _dot_nt(q4, kv) + bias
    win_acc = _dot(jnp.exp(s - jnp.max(s, axis=-1, keepdims=True)).astype(BF16), ones_v(kv))

    gate = _sigmoid(misc_ref[...])

    def normalised(acc):
        return acc / jnp.where(tile(low), 1.0, pltpu.roll(acc, HEAD_DIM, 1))

    def pair(values, pp):
        a, b = values[2 * pp * QB:(2 * pp + 1) * QB], values[(2 * pp + 1) * QB:(2 * pp + 2) * QB]
        return jnp.where(low, pltpu.roll(a, HEAD_DIM, 1), b)

    def pair_gate(branch, pp):
        ca, cb = 3 * (2 * pp) + branch, 3 * (2 * pp + 1) + branch
        return jnp.where(low, gate[:, ca:ca + 1], gate[:, cb:cb + 1])

    o_sel = normalised(sel_acc)
    o_win = normalised(win_acc)
    for pp in range(G // 2):
        lanes = slice(pp * 128, (pp + 1) * 128)
        o_ref[:, lanes] = (pair_gate(0, pp) * ocmp_ref[:, lanes] + pair_gate(1, pp) * pair(o_sel, pp)
                           + pair_gate(2, pp) * pair(o_win, pp))


def _nsa_sel_win(pb, pf, o_cmp, sel, expand, tb, batch, seq):
    nqb = seq // QB
    G = NSA_GQA
    return pl.pallas_call(
        _nsa_sw_kernel,
        grid=(batch, NSA_KV_HEADS, nqb),
        in_specs=[
            pl.BlockSpec((QB, G * 128), lambda b, h, i: (b * nqb + i, h)),
            pl.BlockSpec((seq, 128), lambda b, h, i: (b, PB_SEL // 128 + h)),
            pl.BlockSpec((seq, 128), lambda b, h, i: (b, PB_WIN // 128 + h)),
            pl.BlockSpec((QB, 256), lambda b, h, i: (b * nqb + i, h)),
            pl.BlockSpec((QB, 128), lambda b, h, i: (b * nqb + i, PF_MISC // 128 + h)),
            pl.BlockSpec((1, 1, QB, QB), lambda b, h, i: (b, h, i, 0)),
            pl.BlockSpec((QB, seq), lambda b, h, i: (0, 0)),
            pl.BlockSpec((2, 1, G * QB, QB), lambda b, h, i: (0, h, 0, 0)),
        ],
        out_specs=pl.BlockSpec((QB, 256), lambda b, h, i: (b * nqb + i, h)),
        out_shape=jax.ShapeDtypeStruct((batch * seq, NSA_WIDTH), F32),
        scratch_shapes=[pltpu.VMEM((QB, seq), F32)],
        compiler_params=_params(("parallel", "parallel", "arbitrary")),
        name="nsa_selected_window",
    )(pb, pb, pb, o_cmp, pf, sel, expand, tb)


def _ssd_kernel(z_ref, xs_ref, bc_ref, misc_ref, cwx_ref, cwb_ref, cbx_ref, cbb_ref, hp_ref, spread_ref,
                dskip_ref, ng_ref, o_ref, xbuf, bbuf, state):
    ci = pl.program_id(1)
    L = QB
    P = HEAD_DIM
    GN = SSD_GROUPS * SSD_STATE

    @pl.when(ci == 0)
    def _():
        xbuf[0:8, :] = jnp.zeros((8, SSD_INNER), F32)
        bbuf[0:8, :] = jnp.zeros((8, 2 * GN), F32)
        state[...] = jnp.zeros_like(state)

    def conv_silu(buf, src_ref, w_ref, b_ref):
        buf[8:8 + L, :] = src_ref[...]
        out = b_ref[...]
        for k in range(SSD_CONV):
            lo = 8 - (SSD_CONV - 1) + k
            out = out + w_ref[k:k + 1, :] * buf[lo:lo + L, :]
        buf[0:8, :] = buf[L:L + 8, :]
        return _silu(out)

    xs = conv_silu(xbuf, xs_ref, cwx_ref, cbx_ref)
    bcs = conv_silu(bbuf, bc_ref, cwb_ref, cbb_ref)

    misc = misc_ref[...]
    pre = misc + hp_ref[0:1, :]
    dt = jnp.maximum(pre, 0.0) + jnp.log1p(jnp.exp(-jnp.abs(pre)))
    a_dt = dt * (-jnp.exp(hp_ref[1:2, :]))
    r = lax.broadcasted_iota(jnp.int32, (L, L), 0)
    c = lax.broadcasted_iota(jnp.int32, (L, L), 1)
    causal = r >= c
    a_cs = _split3_dot_left(causal.astype(BF16), a_dt)
    a_cs_t = a_cs.T

    dt_full = _split3_dot(dt, spread_ref[...])
    acs_full = _split3_dot(a_cs, spread_ref[...])
    total_full = acs_full[L - 1:L, :]
    xdt = xs * dt_full
    xdt16 = xdt.astype(BF16)
    to_end16 = (xdt * jnp.exp(total_full - acs_full)).astype(BF16)
    decay_in = jnp.exp(acs_full)
    chunk_decay = jnp.exp(total_full)
    low = c < P
    zero16 = jnp.zeros((L, 2 * P), BF16)
    width = SSD_HPG * P

    ys = []
    for g in range(SSD_GROUPS):
        bm = bcs[:, g * SSD_STATE:(g + 1) * SSD_STATE]
        cm16 = bcs[:, GN + g * SSD_STATE:GN + (g + 1) * SSD_STATE].astype(BF16)
        cb = _dot_nt(cm16, bm.astype(BF16))
        lanes = slice(g * width, (g + 1) * width)
        h_in = state[g]
        y_off = _dot(cm16, h_in.astype(BF16)) * decay_in[:, lanes]
        state[g] = h_in * chunk_decay[:, lanes] + _dot(bm.T.astype(BF16), to_end16[:, lanes])
        diag = []
        for pp in range(SSD_HPG // 2):
            hd = g * SSD_HPG + 2 * pp
            decayed = []
            for col in (MISC_DT + hd, MISC_DT + hd + 1):
                seg = jnp.exp(jnp.where(causal, a_cs[:, col:col + 1] - a_cs_t[col:col + 1, :], -jnp.inf))
                decayed.append((cb * seg).astype(BF16))
            xp = xdt16[:, hd * P:(hd + 2) * P]
            x_bd = jnp.concatenate([jnp.where(low, xp, zero16), jnp.where(low, zero16, xp)], axis=0)
            diag.append(_dot(jnp.concatenate(decayed, axis=1), x_bd))
        ys.append(jnp.concatenate(diag, axis=1) + y_off)

    y = (jnp.concatenate(ys, axis=1) + dskip_ref[...] * xs) * _silu(z_ref[...])
    for g in range(SSD_GROUPS):
        lanes = slice(g * width, (g + 1) * width)
        yg = y[:, lanes]
        o_ref[:, lanes] = yg * lax.rsqrt(jnp.mean(yg * yg, axis=-1, keepdims=True) + EPS) * ng_ref[:, lanes]


def _ssd(pf, conv_w, conv_b, head_params, d_skip, norm_g, batch, seq):
    nc = seq // QB
    GN = SSD_GROUPS * SSD_STATE
    lane_head = jnp.arange(SSD_INNER)[None, :] // HEAD_DIM
    spread = (jnp.arange(128)[:, None] == MISC_DT + lane_head).astype(BF16)
    cwx, cwb = conv_w[:, :SSD_INNER], conv_w[:, SSD_INNER:]
    cbx, cbb = conv_b[:SSD_INNER].reshape(1, -1), conv_b[SSD_INNER:].reshape(1, -1)
    full = lambda shape: pl.BlockSpec(shape, lambda b, ci: (0,) * len(shape))
    return pl.pallas_call(
        _ssd_kernel,
        grid=(batch, nc),
        in_specs=[
            pl.BlockSpec((QB, SSD_INNER), lambda b, ci: (b * nc + ci, PF_Z // SSD_INNER)),
            pl.BlockSpec((QB, SSD_INNER), lambda b, ci: (b * nc + ci, PF_XS // SSD_INNER)),
            pl.BlockSpec((QB, 2 * GN), lambda b, ci: (b * nc + ci, PF_BC // (2 * GN))),
            pl.BlockSpec((QB, 128), lambda b, ci: (b * nc + ci, PF_MISC // 128)),
            full((SSD_CONV, SSD_INNER)),
            full((SSD_CONV, 2 * GN)),
            full((1, SSD_INNER)),
            full((1, 2 * GN)),
            full((8, 128)),
            full((128, SSD_INNER)),
            full((1, SSD_INNER)),
            full((1, SSD_INNER)),
        ],
        out_specs=pl.BlockSpec((QB, SSD_INNER), lambda b, ci: (b * nc + ci, 0)),
        out_shape=jax.ShapeDtypeStruct((batch * seq, SSD_INNER), F32),
        scratch_shapes=[
            pltpu.VMEM((QB + 8, SSD_INNER), F32),
            pltpu.VMEM((QB + 8, 2 * GN), F32),
            pltpu.VMEM((SSD_GROUPS, SSD_STATE, SSD_HPG * HEAD_DIM), F32),
        ],
        compiler_params=_params(("parallel", "arbitrary")),
        name="ssd",
    )(pf, pf, pf, pf, cwx, cwb, cbx, cbb, head_params, spread,
      jnp.repeat(d_skip.astype(F32), HEAD_DIM).reshape(1, -1), norm_g.reshape(1, -1))


def _out_proj_kernel(x_ref, nsa_ref, sb_ref, ssd_ref, gn_ref, gs_ref, w_ref, o_ref, mix_ref):
    @pl.when(pl.program_id(1) == 0)
    def _():
        mix_ref[:, 0:NSA_WIDTH] = _rms(nsa_ref[...], gn_ref[...]).astype(BF16)
        mix_ref[:, NSA_WIDTH:NSA_WIDTH + SB_WIDTH] = _rms(sb_ref[...], gs_ref[...]).astype(BF16)
        mix_ref[:, NSA_WIDTH + SB_WIDTH:] = ssd_ref[...].astype(BF16)

    o_ref[...] = x_ref[...] + _dot(mix_ref[...], w_ref[...])


def _out_proj(x, o_nsa, o_sb, o_ssd, g_nsa, g_sb, w, *, tm=1024, tn=512):
    n, d = x.shape
    dm = w.shape[0]
    return pl.pallas_call(
        _out_proj_kernel,
        grid=(n // tm, d // tn),
        in_specs=[
            pl.BlockSpec((tm, tn), lambda i, j: (i, j)),
            pl.BlockSpec((tm, NSA_WIDTH), lambda i, j: (i, 0)),
            pl.BlockSpec((tm, SB_WIDTH), lambda i, j: (i, 0)),
            pl.BlockSpec((tm, SSD_INNER), lambda i, j: (i, 0)),
            pl.BlockSpec((1, NSA_WIDTH), lambda i, j: (0, 0)),
            pl.BlockSpec((1, SB_WIDTH), lambda i, j: (0, 0)),
            pl.BlockSpec((dm, tn), lambda i, j: (0, j)),
        ],
        out_specs=pl.BlockSpec((tm, tn), lambda i, j: (i, j)),
        out_shape=jax.ShapeDtypeStruct((n, d), F32),
        scratch_shapes=[pltpu.VMEM((tm, dm), BF16)],
        compiler_params=_params(("parallel", "arbitrary")),
        name="out_proj",
    )(x, o_nsa, o_sb, o_ssd, g_nsa.reshape(1, -1), g_sb.reshape(1, -1), w)


def _rel_bucket(dist):
    dist = jnp.maximum(dist, 0)
    max_exact = REL_BUCKETS // 2
    log_ratio = jnp.log(jnp.maximum(dist, 1).astype(F32) / max_exact) / math.log(REL_MAX_DIST / max_exact)
    large = jnp.minimum(max_exact + (log_ratio * (REL_BUCKETS - max_exact)).astype(jnp.int32), REL_BUCKETS - 1)
    return jnp.where(dist < max_exact, dist, large)


def _bias_tables(rel_bias, seq):
    assert QB >= REL_MAX_DIST
    def lookup(dist):
        onehot = jax.nn.one_hot(_rel_bucket(dist), REL_BUCKETS, dtype=F32)
        return jnp.einsum('...k,kh->h...', onehot, rel_bias, precision=lax.Precision.HIGHEST)

    t = jnp.arange(seq)[:, None]
    cend = jnp.arange(QB)[None, :] * CMP_STRIDE + CMP_BLOCK - 1
    n_cmp = (seq - CMP_BLOCK) // CMP_STRIDE + 1
    valid_c = (t >= cend) & (jnp.arange(QB)[None, :] < n_cmp)
    bias_c = jnp.where(valid_c, lookup(t - cend), NEG_INF)
    steps = seq // CMP_TQ
    bias_c = bias_c.reshape(NSA_KV_HEADS, NSA_GQA, steps, CMP_TQ, QB).transpose(0, 2, 1, 3, 4)
    bias_c = bias_c.reshape(NSA_KV_HEADS * steps, NSA_GQA * CMP_TQ, QB)
    r = jnp.arange(QB)[:, None]
    m = jnp.arange(QB)[None, :]
    near = jnp.stack([lookup(r - m), lookup(QB + r - m)])
    near = near - rel_bias[REL_BUCKETS - 1][None, :, None, None]
    near = near.at[0].add(jnp.where(m > r, NEG_INF, 0.0))
    return bias_c, near.reshape(2, NSA_KV_HEADS, NSA_GQA * QB, QB)


def _expand_table(seq):
    j = jnp.arange(QB)[:, None]
    s = jnp.arange(seq)[None, :]
    return (s // SEL_BLOCK == j).astype(BF16)


def _overlap_table(seq):
    n_cmp = (seq - CMP_BLOCK) // CMP_STRIDE + 1
    cs = jnp.arange(QB)[:, None] * CMP_STRIDE
    ce = cs + CMP_BLOCK - 1
    ss = jnp.arange(QB)[None, :] * SEL_BLOCK
    ov = jnp.maximum(jnp.minimum(ce, ss + SEL_BLOCK - 1) - jnp.maximum(cs, ss) + 1, 0).astype(F32) / CMP_BLOCK
    keep = (jnp.arange(QB)[:, None] < n_cmp) & (jnp.arange(QB)[None, :] < seq // SEL_BLOCK)
    return jnp.where(keep, ov, 0.0).T.astype(BF16)


def _in_proj_weights(w):
    scale = HEAD_DIM ** -0.5
    o_q, o_kv, o_gate = 0, NSA_WIDTH, NSA_WIDTH + 768
    o_sb = o_gate + 3 * NSA_HEADS
    o_z = o_sb + 3 * SB_WIDTH
    o_xbc = o_z + SSD_INNER
    o_dt = o_xbc + SSD_INNER + 2 * SSD_GROUPS * SSD_STATE
    col = lambda a, n: w[:, a:a + n]
    kv = lambda br, which, h: col(o_kv + br * 256 + which * 128 + h * HEAD_DIM, HEAD_DIM)
    zeros = lambda n: jnp.zeros((w.shape[0], n), w.dtype)
    branch = lambda br: [kv(br, 0, 0), kv(br, 1, 0), kv(br, 0, 1), kv(br, 1, 1)]
    q_heads = []
    for h in range(NSA_HEADS):
        q_heads += [col(o_q + h * HEAD_DIM, HEAD_DIM) * scale, zeros(128 - HEAD_DIM)]
    wb = jnp.concatenate(
        q_heads + branch(1) + branch(2)
        + [col(o_sb, SB_WIDTH) * scale, col(o_sb + SB_WIDTH, 2 * SB_WIDTH)], axis=1)
    gates = 3 * NSA_GQA
    wf = jnp.concatenate(
        [col(o_z, SSD_INNER), col(o_xbc, SSD_INNER + 2 * SSD_GROUPS * SSD_STATE), col(o_kv, 256),
         col(o_gate, gates), zeros(MISC_DT - gates), col(o_dt, SSD_HEADS), zeros(128 - MISC_DT - SSD_HEADS),
         col(o_gate + gates, gates), zeros(128 - gates)], axis=1)
    assert wb.shape[1] == PB_COLS and wf.shape[1] == PF_COLS
    return wb.astype(BF16), wf.astype(BF16)


def _head_params(dt_bias, a_log):
    rows = jnp.stack([dt_bias, a_log]).astype(F32)
    return jnp.zeros((8, 128), F32).at[0:2, MISC_DT:MISC_DT + SSD_HEADS].set(rows)


def _mixer(x, l, tables, batch, seq, mix_norm, w_in, w_out, cmp_pos, cmp_w, nsa_norm, sb_norm,
           conv_w, conv_b, dt_bias, a_log, d_skip, ssd_norm):
    bias_c, near, overlap, expand = tables
    wb, wf = _in_proj_weights(w_in[l])
    pb = _rms_matmul(x, mix_norm[l], wb, BF16, name="in_proj_bf16")
    pf = _rms_matmul(x, mix_norm[l], wf, F32, name="in_proj_f32")

    ng = seq // CMP_STRIDE
    ucmp = pf[:, PF_CMP:PF_CMP + 256].reshape(batch, ng, CMP_STRIDE, 4, HEAD_DIM)
    ucmp = ucmp.transpose(0, 3, 1, 2, 4).reshape(batch, 4, ng, CMP_STRIDE * HEAD_DIM)
    pos2 = cmp_pos[l].reshape(2, 2, CMP_STRIDE * HEAD_DIM)
    o_cmp, sel = _nsa_cmp(pb, ucmp, pos2, cmp_w[l].astype(BF16), bias_c, overlap, batch, seq)
    o_nsa = _nsa_sel_win(pb, pf, o_cmp, sel, expand, near, batch, seq)
    o_sb = _sb_attention(pb, batch, seq)
    o_ssd = _ssd(pf, conv_w[l], conv_b[l], _head_params(dt_bias[l], a_log[l]), d_skip[l], ssd_norm[l],
                 batch, seq)
    return _out_proj(x, o_nsa, o_sb, o_ssd, nsa_norm[l], sb_norm[l], w_out[l].astype(BF16))


def kernel(x, rel_bias, ffn1_norm, ffn1_w_gate, ffn1_w_up, ffn1_w_down, mix_norm, w_in, w_out, nsa_cmp_pos, nsa_cmp_w, nsa_out_norm, sb_out_norm, ssd_conv_w, ssd_conv_b, ssd_dt_bias, ssd_a_log, ssd_d, ssd_out_norm, ffn2_norm, ffn2_w_gate, ffn2_w_up, ffn2_w_down, final_norm):
    batch, seq, d = x.shape
    depth = w_in.shape[0]
    tables = _bias_tables(rel_bias, seq) + (_overlap_table(seq), _expand_table(seq))
    h = x.reshape(batch * seq, d)
    ffn1_w_gate, ffn1_w_up, ffn1_w_down, ffn2_w_gate, ffn2_w_up, ffn2_w_down = (
        w.astype(BF16) for w in (ffn1_w_gate, ffn1_w_up, ffn1_w_down, ffn2_w_gate, ffn2_w_up, ffn2_w_down))
    for l in range(depth):
        h = _ffn(h, ffn1_norm[l], ffn1_w_gate, ffn1_w_up, ffn1_w_down, l, final_norm, final=False)
        h = _mixer(h, l, tables, batch, seq, mix_norm, w_in, w_out, nsa_cmp_pos, nsa_cmp_w, nsa_out_norm,
                   sb_out_norm, ssd_conv_w, ssd_conv_b, ssd_dt_bias, ssd_a_log, ssd_d, ssd_out_norm)
        h = _ffn(h, ffn2_norm[l], ffn2_w_gate, ffn2_w_up, ffn2_w_down, l, final_norm,
                 final=(l == depth - 1))
    return h.reshape(batch, seq, d)
```

```python
import functools
import math

import jax
import jax.numpy as jnp
from jax import lax
from jax.experimental import pallas as pl
from jax.experimental.pallas import tpu as pltpu

D_MODEL = 2048
D_FF = 5632
HEAD_DIM = 64
QB = 128
NEG_INF = -1e30
EPS = 1e-6

NSA_HEADS = 8
NSA_KV_HEADS = 2
NSA_GQA = NSA_HEADS // NSA_KV_HEADS
NSA_WIDTH = NSA_HEADS * HEAD_DIM
CMP_BLOCK = 32
CMP_STRIDE = 16
SEL_BLOCK = 64
SEL_TOPK = 8
FORCE_SCORE = 1e3
WINDOW = 512
SB_HEADS = 8
SB_WIDTH = SB_HEADS * HEAD_DIM
SSD_HEADS = 16
SSD_INNER = SSD_HEADS * HEAD_DIM
SSD_GROUPS = 2
SSD_HPG = SSD_HEADS // SSD_GROUPS
SSD_STATE = 128
SSD_CONV = 4
REL_BUCKETS = 32
REL_MAX_DIST = 128

PB_NSA_Q, PB_SEL, PB_WIN, PB_SB_Q, PB_SB_K, PB_SB_V, PB_COLS = 0, 1024, 1280, 1536, 2048, 2560, 3072
PF_Z, PF_XS, PF_BC, PF_CMP, PF_MISC, PF_COLS = 0, 1024, 2048, 2560, 2816, 3072
MISC_DT = 16

VMEM_LIMIT = 52 * 1024 * 1024

BF16 = jnp.bfloat16
F32 = jnp.float32


def _dot(a, b):
    return jnp.dot(a, b, preferred_element_type=F32)


def _dot_nt(a, b):
    return lax.dot_general(a, b, (((1,), (1,)), ((), ())), preferred_element_type=F32)


def _split2_dot(x, m):
    hi = x.astype(BF16)
    lo = (x - hi.astype(F32)).astype(BF16)
    return _dot(hi, m) + _dot(lo, m)


def _split3_dot(x, m):
    hi = x.astype(BF16)
    r1 = x - hi.astype(F32)
    mid = r1.astype(BF16)
    lo = (r1 - mid.astype(F32)).astype(BF16)
    return _dot(hi, m) + _dot(mid, m) + _dot(lo, m)


def _split3_dot_left(m, x):
    hi = x.astype(BF16)
    r1 = x - hi.astype(F32)
    mid = r1.astype(BF16)
    lo = (r1 - mid.astype(F32)).astype(BF16)
    return _dot(m, hi) + _dot(m, mid) + _dot(m, lo)


def _rms(x, g):
    return x * lax.rsqrt(jnp.mean(x * x, axis=-1, keepdims=True) + EPS) * g


def _silu(x):
    return x / (1.0 + jnp.exp(-x))


def _sigmoid(x):
    return 1.0 / (1.0 + jnp.exp(-x))


def _params(sem):
    return pltpu.CompilerParams(dimension_semantics=sem, vmem_limit_bytes=VMEM_LIMIT)


def _ffn_kernel(x_ref, g_ref, wg_ref, wu_ref, wd_ref, fg_ref, o_ref, h_ref, acc_ref, *, final):
    f = pl.program_id(1)

    @pl.when(f == 0)
    def _():
        h_ref[...] = _rms(x_ref[...], g_ref[...]).astype(BF16)
        acc_ref[...] = jnp.zeros_like(acc_ref)

    h = h_ref[...]
    gate = _dot(h, wg_ref[...])
    up = _dot(h, wu_ref[...])
    acc_ref[...] += _dot((_silu(gate) * up).astype(BF16), wd_ref[...])

    @pl.when(f == pl.num_programs(1) - 1)
    def _():
        y = x_ref[...] + 0.5 * acc_ref[...]
        if final:
            y = _rms(y, fg_ref[...])
        o_ref[...] = y


FFN_TF = 512


def _tile_major(w, tn):
    *lead, d, c = w.shape
    return jnp.moveaxis(w.reshape(*lead, d, c // tn, tn), -2, -3)


def _ffn(x, g, wg, wu, wd, layer, final_g, *, final, tm=512):
    n, d = x.shape
    tf = wg.shape[3]
    dff = wd.shape[1]
    return pl.pallas_call(
        functools.partial(_ffn_kernel, final=final),
        grid=(n // tm, dff // tf),
        in_specs=[
            pl.BlockSpec((tm, d), lambda i, f: (i, 0)),
            pl.BlockSpec((1, d), lambda i, f: (0, 0)),
            pl.BlockSpec((None, None, d, tf), lambda i, f: (layer, f, 0, 0)),
            pl.BlockSpec((None, None, d, tf), lambda i, f: (layer, f, 0, 0)),
            pl.BlockSpec((None, tf, d), lambda i, f: (layer, f, 0)),
            pl.BlockSpec((1, d), lambda i, f: (0, 0)),
        ],
        out_specs=pl.BlockSpec((tm, d), lambda i, f: (i, 0)),
        out_shape=jax.ShapeDtypeStruct((n, d), F32),
        scratch_shapes=[pltpu.VMEM((tm, d), BF16), pltpu.VMEM((tm, d), F32)],
        compiler_params=_params(("parallel", "arbitrary")),
        name="ffn",
    )(x, g.reshape(1, d), wg, wu, wd, final_g.reshape(1, d))


def _rms_matmul_kernel(x_ref, g_ref, w_ref, o_ref, h_ref):
    @pl.when(pl.program_id(1) == 0)
    def _():
        h_ref[...] = _rms(x_ref[...], g_ref[...]).astype(BF16)

    o_ref[...] = _dot(h_ref[...], w_ref[...]).astype(o_ref.dtype)


PROJ_TN = 512


def _rms_matmul(x, g, w, out_dtype, *, tm=1024, name):
    n, d = x.shape
    nt, _, tn = w.shape
    c = nt * tn
    return pl.pallas_call(
        _rms_matmul_kernel,
        grid=(n // tm, nt),
        in_specs=[
            pl.BlockSpec((tm, d), lambda i, j: (i, 0)),
            pl.BlockSpec((1, d), lambda i, j: (0, 0)),
            pl.BlockSpec((None, d, tn), lambda i, j: (j, 0, 0)),
        ],
        out_specs=pl.BlockSpec((tm, tn), lambda i, j: (i, j)),
        out_shape=jax.ShapeDtypeStruct((n, c), out_dtype),
        scratch_shapes=[pltpu.VMEM((tm, d), BF16)],
        compiler_params=_params(("parallel", "arbitrary")),
        name=name,
    )(x, g.reshape(1, d), w)


SB_GROUP = 8
SB_DEAD = 104.0


def _sb_kernel(q_ref, k_ref, v_ref, o_ref, kbd_ref, vbd_ref):
    i = pl.program_id(2)
    pairs = SB_GROUP // 2
    nkb = k_ref.shape[0] // QB
    r = lax.broadcasted_iota(jnp.int32, (QB, QB), 0)
    c = lax.broadcasted_iota(jnp.int32, (QB, QB), 1)
    strict_all = jnp.concatenate([c < r] * SB_GROUP, axis=0)
    later_ones = jnp.concatenate([(r > c).astype(BF16), jnp.ones((QB, QB), BF16)], axis=1)
    later_ones = jnp.concatenate([later_ones, later_ones], axis=0)

    @pl.when(i == 0)
    def _():
        low = c < HEAD_DIM
        zero16 = jnp.zeros((QB, QB), BF16)

        def build(j, carry):
            src = pl.ds(pl.multiple_of(j * QB, QB), QB)
            dst = pl.ds(pl.multiple_of(j * 2 * QB, 2 * QB), 2 * QB)
            for pp in range(pairs):
                for ref, out in ((k_ref, kbd_ref), (v_ref, vbd_ref)):
                    x = ref[src, pp * 128:(pp + 1) * 128]
                    out[pp, dst, :] = jnp.concatenate([jnp.where(low, x, zero16), jnp.where(low, zero16, x)],
                                                      axis=0)
            return carry

        lax.fori_loop(0, nkb, build, 0)

    def scores(j, diagonal):
        rows = pl.ds(pl.multiple_of(j * 2 * QB, 2 * QB), 2 * QB)
        zs = []
        for pp in range(pairs):
            z_pair = _dot_nt(q_ref[:, pp * 128:(pp + 1) * 128], kbd_ref[pp, rows, :])
            zs += [z_pair[:, 0:QB], z_pair[:, QB:2 * QB]]
        z = jnp.concatenate(zs, axis=0)
        soft = jnp.maximum(z, 0.0) + jnp.log(1.0 + jnp.exp(-jnp.abs(z)))
        if diagonal:
            soft = jnp.where(strict_all, soft, 0.0)
        hi = soft.astype(BF16)
        lo = (soft - hi.astype(F32)).astype(BF16)
        sums = _dot(jnp.concatenate([hi, lo], axis=1), later_ones)
        return rows, z - soft, sums

    def absorb(tails, accs, rows, log_beta, sums, diagonal):
        a = jnp.exp(log_beta - (tails + sums[:, 0:QB]))
        if diagonal:
            a = jnp.where(strict_all, a, 0.0)
        a = a.astype(BF16)
        outs = []
        for pp in range(pairs):
            pair = jnp.concatenate([a[2 * pp * QB:(2 * pp + 1) * QB], a[(2 * pp + 1) * QB:(2 * pp + 2) * QB]],
                                   axis=1)
            outs.append(_dot(pair, vbd_ref[pp, rows, :]))
        return tails + sums[:, QB:2 * QB], accs + jnp.concatenate(outs, axis=0)

    def sweep(js, carry):
        staged = [scores(j, False) for j in js]
        for st in staged:
            carry = absorb(*carry, *st, False)
        return carry

    carry = (jnp.zeros((SB_GROUP * QB, QB), F32), jnp.zeros((pairs * QB, 128), F32))
    carry = absorb(*carry, *scores(i, True), True)
    odd = i % 2
    carry = lax.fori_loop(0, odd, lambda step, cr: sweep([i - 1], cr), carry)
    top = i - 1 - odd

    def live(state):
        step, smallest, _, _ = state
        return (step < i // 2) & (smallest < SB_DEAD)

    def pair_step(state):
        step, _, tails, accs = state
        tails, accs = sweep([top - 2 * step, top - 2 * step - 1], (tails, accs))
        return step + 1, jnp.min(tails), tails, accs

    accs = lax.while_loop(live, pair_step, (0, jnp.min(carry[0]), *carry))[3]
    for pp in range(pairs):
        o_ref[:, pp * 128:(pp + 1) * 128] = accs[pp * QB:(pp + 1) * QB]


def _sb_attention(pb, batch, seq):
    nqb = seq // QB
    w = SB_GROUP * HEAD_DIM
    qc, kc, vc = PB_SB_Q // w, PB_SB_K // w, PB_SB_V // w
    return pl.pallas_call(
        _sb_kernel,
        grid=(batch, SB_HEADS // SB_GROUP, nqb),
        in_specs=[
            pl.BlockSpec((QB, w), lambda b, p, i: (b * nqb + i, qc + p)),
            pl.BlockSpec((seq, w), lambda b, p, i: (b, kc + p)),
            pl.BlockSpec((seq, w), lambda b, p, i: (b, vc + p)),
        ],
        out_specs=pl.BlockSpec((QB, w), lambda b, p, i: (b * nqb + i, p)),
        out_shape=jax.ShapeDtypeStruct((batch * seq, SB_WIDTH), F32),
        scratch_shapes=[pltpu.VMEM((SB_GROUP // 2, 2 * seq, 128), BF16),
                        pltpu.VMEM((SB_GROUP // 2, 2 * seq, 128), BF16)],
        compiler_params=_params(("parallel", "parallel", "arbitrary")),
        name="sb_attention",
    )(pb, pb, pb)


CMP_TQ = 4 * QB


def _nsa_cmp_kernel(q_ref, uk_ref, uv_ref, pos_ref, w_ref, bias_ref, ovt_ref, ocmp_ref, sel_ref,
                    kc_ref, vc_ref, *, n_cmp, n_sel):
    i = pl.program_id(2)
    half = CMP_STRIDE * HEAD_DIM

    @pl.when(i == 0)
    def _():
        def compress(kv, u_ref):
            u = u_ref[0, 0]
            top = _dot((u + pos_ref[kv, 0:1, :]).astype(BF16), w_ref[kv, 0:half, :])
            bot = _dot((u + pos_ref[kv, 1:2, :]).astype(BF16), w_ref[kv, half:2 * half, :])
            return top + pltpu.roll(bot, QB - 1, 0)

        zeros = jnp.zeros((QB, HEAD_DIM), F32)
        kc_ref[...] = jnp.concatenate([compress(0, uk_ref), zeros], axis=1).astype(BF16)
        v = compress(1, uv_ref)
        vc_ref[...] = jnp.concatenate([jnp.concatenate([v, zeros], axis=1),
                                       jnp.concatenate([zeros, v], axis=1)], axis=0).astype(BF16)

    G = NSA_GQA
    TQ = CMP_TQ
    q4 = jnp.concatenate([q_ref[:, g * 128:(g + 1) * 128] for g in range(G)], axis=0)
    s = _dot_nt(q4, kc_ref[...]) + bias_ref[0]
    e = jnp.exp(s - jnp.max(s, axis=-1, keepdims=True))
    t_rows = i * TQ + lax.broadcasted_iota(jnp.int32, (TQ, 1), 0)
    any_valid = jnp.concatenate([t_rows >= CMP_BLOCK - 1] * G, axis=0)
    p = jnp.where(any_valid, e / jnp.sum(e, axis=-1, keepdims=True), 0.0)
    p16 = p.astype(BF16)
    for pp in range(G // 2):
        pair = jnp.concatenate([p16[2 * pp * TQ:(2 * pp + 1) * TQ], p16[(2 * pp + 1) * TQ:(2 * pp + 2) * TQ]],
                               axis=1)
        ocmp_ref[:, pp * 128:(pp + 1) * 128] = _dot(pair, vc_ref[...])
    p_all = p[0:TQ]
    for g in range(1, G):
        p_all = p_all + p[g * TQ:(g + 1) * TQ]

    blk = lax.broadcasted_iota(jnp.int32, (n_sel, QB), 0)
    for part in range(TQ // QB):
        rs = slice(part * QB, (part + 1) * QB)
        t0 = (i * (TQ // QB) + part) * QB
        p_sum = p_all[rs]
        hi = p_sum.astype(BF16)
        lo = (p_sum - hi.astype(F32)).astype(BF16)
        p_sel = (_dot_nt(ovt_ref[...], hi) + _dot_nt(ovt_ref[...], lo))[0:n_sel]
        t = t0 + lax.broadcasted_iota(jnp.int32, (n_sel, QB), 1)
        cur = t // SEL_BLOCK
        eligible = blk * SEL_BLOCK <= t
        forced = (blk == 0) | (blk == cur) | (blk == cur - 1)
        score = jnp.where(eligible, p_sel + jnp.where(forced, FORCE_SCORE, 0.0), NEG_INF)
        rank = jnp.zeros((n_sel, QB), F32)
        for j in range(n_sel):
            other = score[j:j + 1, :]
            ahead = (other > score) | ((other == score) & (blk > j))
            rank = rank + jnp.where(ahead, 1.0, 0.0)
        chosen = jnp.where(eligible & (rank < SEL_TOPK), 1.0, 0.0)
        chosen = jnp.concatenate([chosen, jnp.zeros((QB - n_sel, QB), F32)], axis=0)
        sel_ref[0, 0, rs, :] = chosen.T.astype(BF16)


def _nsa_cmp(pb, ucmp, pos2, cmp_w, bias_c, overlap_t, batch, seq):
    nqb = seq // CMP_TQ
    n_cmp = (seq - CMP_BLOCK) // CMP_STRIDE + 1
    n_sel = seq // SEL_BLOCK
    ng = seq // CMP_STRIDE
    assert ng == QB and n_sel <= QB and n_sel % 8 == 0
    wide = CMP_STRIDE * HEAD_DIM
    return pl.pallas_call(
        functools.partial(_nsa_cmp_kernel, n_cmp=n_cmp, n_sel=n_sel),
        grid=(batch, NSA_KV_HEADS, nqb),
        in_specs=[
            pl.BlockSpec((CMP_TQ, NSA_GQA * 128), lambda b, h, i: (b * nqb + i, h)),
            pl.BlockSpec((1, 1, ng, wide), lambda b, h, i: (b, h, 0, 0)),
            pl.BlockSpec((1, 1, ng, wide), lambda b, h, i: (b, NSA_KV_HEADS + h, 0, 0)),
            pl.BlockSpec((2, 2, wide), lambda b, h, i: (0, 0, 0)),
            pl.BlockSpec((2, 2 * wide, HEAD_DIM), lambda b, h, i: (0, 0, 0)),
            pl.BlockSpec((1, NSA_GQA * CMP_TQ, QB), lambda b, h, i: (h * nqb + i, 0, 0)),
            pl.BlockSpec((QB, QB), lambda b, h, i: (0, 0)),
        ],
        out_specs=[
            pl.BlockSpec((CMP_TQ, 256), lambda b, h, i: (b * nqb + i, h)),
            pl.BlockSpec((1, 1, CMP_TQ, QB), lambda b, h, i: (b, h, i, 0)),
        ],
        out_shape=[
            jax.ShapeDtypeStruct((batch * seq, NSA_WIDTH), F32),
            jax.ShapeDtypeStruct((batch, NSA_KV_HEADS, seq, QB), BF16),
        ],
        scratch_shapes=[pltpu.VMEM((QB, 128), BF16), pltpu.VMEM((2 * QB, 128), BF16)],
        compiler_params=_params(("parallel", "parallel", "arbitrary")),
        name="nsa_compressed",
    )(pb, ucmp, ucmp, pos2, cmp_w, bias_c, overlap_t)


SEL_CHUNK = 4 * QB
WIN_FAR = WINDOW - QB


def _nsa_sw_kernel(q_ref, ks_ref, kw_ref, ocmp_ref, misc_ref, sel_ref, exp_ref, tb_ref, o_ref, madd_ref):
    i = pl.program_id(2)
    G = NSA_GQA
    q4 = jnp.concatenate([q_ref[:, g * 128:(g + 1) * 128] for g in range(G)], axis=0)
    madd_ref[...] = (_dot(sel_ref[0, 0], exp_ref[...]) - 1.0) * (-NEG_INF)
    t_pos = i * QB + lax.broadcasted_iota(jnp.int32, (QB, 1), 0)

    low = lax.broadcasted_iota(jnp.int32, (QB, 128), 1) < HEAD_DIM

    def tile(x, n=G):
        return jnp.concatenate([x] * n, axis=0)

    def ones_v(kv):
        return jnp.where(tile(low, kv.shape[0] // QB), jnp.ones_like(kv), kv)

    rows0 = pl.ds(pl.multiple_of(i * QB, QB), QB)
    rows1 = pl.ds(pl.multiple_of(jnp.maximum(i - 1, 0) * QB, QB), QB)
    gone = jnp.where(i >= 1, 0.0, NEG_INF)
    far_end = (i - 1) * QB

    kv = jnp.concatenate([ks_ref[rows1, :], ks_ref[rows0, :]], axis=0)
    bias = jnp.concatenate([tb_ref[1, 0] + tile(madd_ref[:, rows1] + gone),
                            tb_ref[0, 0] + tile(madd_ref[:, rows0])], axis=1)
    s = _dot_nt(q4, kv) + bias
    m = jnp.max(s, axis=-1, keepdims=True)
    acc = _dot(jnp.exp(s - m).astype(BF16), ones_v(kv))

    def sel_far(cidx, state):
        m, acc = state
        cols = pl.ds(pl.multiple_of(cidx * SEL_CHUNK, SEL_CHUNK), SEL_CHUNK)
        k_pos = cidx * SEL_CHUNK + lax.broadcasted_iota(jnp.int32, (QB, SEL_CHUNK), 1)
        add = jnp.where(k_pos < far_end, madd_ref[:, cols], NEG_INF)
        kv = ks_ref[cols, :]
        s = _dot_nt(q4, kv) + tile(add)
        m_new = jnp.maximum(m, jnp.max(s, axis=-1, keepdims=True))
        return m_new, jnp.exp(m - m_new) * acc + _dot(jnp.exp(s - m_new).astype(BF16), ones_v(kv))

    _, sel_acc = lax.fori_loop(0, (i + 2) // (SEL_CHUNK // QB), sel_far, (m, acc))

    start = jnp.maximum(i - WINDOW // QB, 0) * QB
    rows_far = pl.ds(pl.multiple_of(start, QB), WIN_FAR)
    k_pos = start + lax.broadcasted_iota(jnp.int32, (QB, WIN_FAR), 1)
    live = (k_pos < far_end) & (k_pos > t_pos - WINDOW)
    kv = jnp.concatenate([kw_ref[rows_far, :], kw_ref[rows1, :], kw_ref[rows0, :]], axis=0)
    bias = jnp.concatenate([tile(jnp.where(live, 0.0, NEG_INF)), tb_ref[1, 0] + gone, tb_ref[0, 0]], axis=1)
    s = _dot_nt(q4, kv) + bias
    win_acc = _dot(jnp.exp(s - jnp.max(s, axis=-1, keepdims=True)).astype(BF16), ones_v(kv))

    gate = _sigmoid(misc_ref[...])

    def pair(acc, pp):
        a, b = acc[2 * pp * QB:(2 * pp + 1) * QB], acc[(2 * pp + 1) * QB:(2 * pp + 2) * QB]
        return (jnp.where(low, pltpu.roll(a, HEAD_DIM, 1), b)
                / jnp.where(low, a, pltpu.roll(b, HEAD_DIM, 1)))

    def pair_gate(branch, pp):
        ca, cb = 3 * (2 * pp) + branch, 3 * (2 * pp + 1) + branch
        return jnp.where(low, gate[:, ca:ca + 1], gate[:, cb:cb + 1])

    for pp in range(G // 2):
        lanes = slice(pp * 128, (pp + 1) * 128)
        o_ref[:, lanes] = (pair_gate(0, pp) * ocmp_ref[:, lanes] + pair_gate(1, pp) * pair(sel_acc, pp)
                           + pair_gate(2, pp) * pair(win_acc, pp))


def _nsa_sel_win(pb, pf, o_cmp, sel, expand, tb, batch, seq):
    nqb = seq // QB
    G = NSA_GQA
    return pl.pallas_call(
        _nsa_sw_kernel,
        grid=(batch, NSA_KV_HEADS, nqb),
        in_specs=[
            pl.BlockSpec((QB, G * 128), lambda b, h, i: (b * nqb + i, h)),
            pl.BlockSpec((seq, 128), lambda b, h, i: (b, PB_SEL // 128 + h)),
            pl.BlockSpec((seq, 128), lambda b, h, i: (b, PB_WIN // 128 + h)),
            pl.BlockSpec((QB, 256), lambda b, h, i: (b * nqb + i, h)),
            pl.BlockSpec((QB, 128), lambda b, h, i: (b * nqb + i, PF_MISC // 128 + h)),
            pl.BlockSpec((1, 1, QB, QB), lambda b, h, i: (b, h, i, 0)),
            pl.BlockSpec((QB, seq), lambda b, h, i: (0, 0)),
            pl.BlockSpec((2, 1, G * QB, QB), lambda b, h, i: (0, h, 0, 0)),
        ],
        out_specs=pl.BlockSpec((QB, 256), lambda b, h, i: (b * nqb + i, h)),
        out_shape=jax.ShapeDtypeStruct((batch * seq, NSA_WIDTH), F32),
        scratch_shapes=[pltpu.VMEM((QB, seq), F32)],
        compiler_params=_params(("parallel", "parallel", "arbitrary")),
        name="nsa_selected_window",
    )(pb, pb, pb, o_cmp, pf, sel, expand, tb)


def _ssd_kernel(z_ref, xs_ref, bc_ref, misc_ref, cwx_ref, cwb_ref, cbx_ref, cbb_ref, hp_ref, spread_ref,
                dskip_ref, ng_ref, o_ref, xbuf, bbuf, state):
    ci = pl.program_id(1)
    L = QB
    P = HEAD_DIM
    GN = SSD_GROUPS * SSD_STATE

    @pl.when(ci == 0)
    def _():
        xbuf[...] = jnp.zeros_like(xbuf)
        bbuf[...] = jnp.zeros_like(bbuf)
        state[...] = jnp.zeros_like(state)

    def conv_silu(buf, src_ref, w_ref, b_ref):
        cur = src_ref[...]
        ext = jnp.concatenate([buf[...], cur], axis=0)
        out = b_ref[...] + w_ref[SSD_CONV - 1:SSD_CONV, :] * cur
        for k in range(1, SSD_CONV):
            out = out + w_ref[SSD_CONV - 1 - k:SSD_CONV - k, :] * pltpu.roll(ext, k, 0)[8:8 + L]
        buf[...] = cur[L - 8:L]
        return _silu(out)

    xs = conv_silu(xbuf, xs_ref, cwx_ref, cbx_ref)
    bcs = conv_silu(bbuf, bc_ref, cwb_ref, cbb_ref)

    misc = misc_ref[...]
    pre = misc + hp_ref[0:1, :]
    dt = jnp.maximum(pre, 0.0) + jnp.log1p(jnp.exp(-jnp.abs(pre)))
    a_dt = dt * (-jnp.exp(hp_ref[1:2, :]))
    r = lax.broadcasted_iota(jnp.int32, (L, L), 0)
    c = lax.broadcasted_iota(jnp.int32, (L, L), 1)
    causal = r >= c
    a_cs = _split3_dot_left(causal.astype(BF16), a_dt)
    a_cs_t = a_cs.T

    dt_full = _split3_dot(dt, spread_ref[...])
    acs_full = _split3_dot(a_cs, spread_ref[...])
    total_full = acs_full[L - 1:L, :]
    xdt = xs * dt_full
    xdt16 = xdt.astype(BF16)
    to_end16 = (xdt * jnp.exp(total_full - acs_full)).astype(BF16)
    decay_in = jnp.exp(acs_full)
    chunk_decay = jnp.exp(total_full)
    low = c < P
    zero16 = jnp.zeros((L, 2 * P), BF16)
    width = SSD_HPG * P

    ys = []
    for g in range(SSD_GROUPS):
        bm = bcs[:, g * SSD_STATE:(g + 1) * SSD_STATE]
        cm16 = bcs[:, GN + g * SSD_STATE:GN + (g + 1) * SSD_STATE].astype(BF16)
        cb = _dot_nt(cm16, bm.astype(BF16))
        lanes = slice(g * width, (g + 1) * width)
        h_in = state[g]
        y_off = _dot(cm16, h_in.astype(BF16)) * decay_in[:, lanes]
        state[g] = h_in * chunk_decay[:, lanes] + _dot(bm.T.astype(BF16), to_end16[:, lanes])
        diag = []
        for pp in range(SSD_HPG // 2):
            hd = g * SSD_HPG + 2 * pp
            decayed = []
            for col in (MISC_DT + hd, MISC_DT + hd + 1):
                seg = jnp.exp(jnp.where(causal, a_cs[:, col:col + 1] - a_cs_t[col:col + 1, :], -jnp.inf))
                decayed.append((cb * seg).astype(BF16))
            xp = xdt16[:, hd * P:(hd + 2) * P]
            x_bd = jnp.concatenate([jnp.where(low, xp, zero16), jnp.where(low, zero16, xp)], axis=0)
            diag.append(_dot(jnp.concatenate(decayed, axis=1), x_bd))
        ys.append(jnp.concatenate(diag, axis=1) + y_off)

    y = (jnp.concatenate(ys, axis=1) + dskip_ref[...] * xs) * _silu(z_ref[...])
    for g in range(SSD_GROUPS):
        lanes = slice(g * width, (g + 1) * width)
        yg = y[:, lanes]
        o_ref[:, lanes] = yg * lax.rsqrt(jnp.mean(yg * yg, axis=-1, keepdims=True) + EPS) * ng_ref[:, lanes]


def _ssd(pf, conv_w, conv_b, head_params, d_skip, norm_g, batch, seq):
    nc = seq // QB
    GN = SSD_GROUPS * SSD_STATE
    lane_head = jnp.arange(SSD_INNER)[None, :] // HEAD_DIM
    spread = (jnp.arange(128)[:, None] == MISC_DT + lane_head).astype(BF16)
    cwx, cwb = conv_w[:, :SSD_INNER], conv_w[:, SSD_INNER:]
    cbx, cbb = conv_b[:SSD_INNER].reshape(1, -1), conv_b[SSD_INNER:].reshape(1, -1)
    full = lambda shape: pl.BlockSpec(shape, lambda b, ci: (0,) * len(shape))
    return pl.pallas_call(
        _ssd_kernel,
        grid=(batch, nc),
        in_specs=[
            pl.BlockSpec((QB, SSD_INNER), lambda b, ci: (b * nc + ci, PF_Z // SSD_INNER)),
            pl.BlockSpec((QB, SSD_INNER), lambda b, ci: (b * nc + ci, PF_XS // SSD_INNER)),
            pl.BlockSpec((QB, 2 * GN), lambda b, ci: (b * nc + ci, PF_BC // (2 * GN))),
            pl.BlockSpec((QB, 128), lambda b, ci: (b * nc + ci, PF_MISC // 128)),
            full((SSD_CONV, SSD_INNER)),
            full((SSD_CONV, 2 * GN)),
            full((1, SSD_INNER)),
            full((1, 2 * GN)),
            full((8, 128)),
            full((128, SSD_INNER)),
            full((1, SSD_INNER)),
            full((1, SSD_INNER)),
        ],
        out_specs=pl.BlockSpec((QB, SSD_INNER), lambda b, ci: (b * nc + ci, 0)),
        out_shape=jax.ShapeDtypeStruct((batch * seq, SSD_INNER), F32),
        scratch_shapes=[
            pltpu.VMEM((8, SSD_INNER), F32),
            pltpu.VMEM((8, 2 * GN), F32),
            pltpu.VMEM((SSD_GROUPS, SSD_STATE, SSD_HPG * HEAD_DIM), F32),
        ],
        compiler_params=_params(("parallel", "arbitrary")),
        name="ssd",
    )(pf, pf, pf, pf, cwx, cwb, cbx, cbb, head_params, spread,
      jnp.repeat(d_skip.astype(F32), HEAD_DIM).reshape(1, -1), norm_g.reshape(1, -1))


def _out_proj_kernel(x_ref, nsa_ref, sb_ref, ssd_ref, gn_ref, gs_ref, w_ref, o_ref, mix_ref):
    @pl.when(pl.program_id(1) == 0)
    def _():
        mix_ref[:, 0:NSA_WIDTH] = _rms(nsa_ref[...], gn_ref[...]).astype(BF16)
        mix_ref[:, NSA_WIDTH:NSA_WIDTH + SB_WIDTH] = _rms(sb_ref[...], gs_ref[...]).astype(BF16)
        mix_ref[:, NSA_WIDTH + SB_WIDTH:] = ssd_ref[...].astype(BF16)

    o_ref[...] = x_ref[...] + _dot(mix_ref[...], w_ref[...])


def _out_proj(x, o_nsa, o_sb, o_ssd, g_nsa, g_sb, w, *, tm=1024):
    n, d = x.shape
    nt, dm, tn = w.shape
    return pl.pallas_call(
        _out_proj_kernel,
        grid=(n // tm, nt),
        in_specs=[
            pl.BlockSpec((tm, tn), lambda i, j: (i, j)),
            pl.BlockSpec((tm, NSA_WIDTH), lambda i, j: (i, 0)),
            pl.BlockSpec((tm, SB_WIDTH), lambda i, j: (i, 0)),
            pl.BlockSpec((tm, SSD_INNER), lambda i, j: (i, 0)),
            pl.BlockSpec((1, NSA_WIDTH), lambda i, j: (0, 0)),
            pl.BlockSpec((1, SB_WIDTH), lambda i, j: (0, 0)),
            pl.BlockSpec((None, dm, tn), lambda i, j: (j, 0, 0)),
        ],
        out_specs=pl.BlockSpec((tm, tn), lambda i, j: (i, j)),
        out_shape=jax.ShapeDtypeStruct((n, d), F32),
        scratch_shapes=[pltpu.VMEM((tm, dm), BF16)],
        compiler_params=_params(("parallel", "arbitrary")),
        name="out_proj",
    )(x, o_nsa, o_sb, o_ssd, g_nsa.reshape(1, -1), g_sb.reshape(1, -1), w)


def _rel_bucket(dist):
    dist = jnp.maximum(dist, 0)
    max_exact = REL_BUCKETS // 2
    log_ratio = jnp.log(jnp.maximum(dist, 1).astype(F32) / max_exact) / math.log(REL_MAX_DIST / max_exact)
    large = jnp.minimum(max_exact + (log_ratio * (REL_BUCKETS - max_exact)).astype(jnp.int32), REL_BUCKETS - 1)
    return jnp.where(dist < max_exact, dist, large)


def _bias_tables(rel_bias, seq):
    assert QB >= REL_MAX_DIST
    def lookup(dist):
        onehot = jax.nn.one_hot(_rel_bucket(dist), REL_BUCKETS, dtype=F32)
        return jnp.einsum('...k,kh->h...', onehot, rel_bias, precision=lax.Precision.HIGHEST)

    t = jnp.arange(seq)[:, None]
    cend = jnp.arange(QB)[None, :] * CMP_STRIDE + CMP_BLOCK - 1
    n_cmp = (seq - CMP_BLOCK) // CMP_STRIDE + 1
    valid_c = (t >= cend) & (jnp.arange(QB)[None, :] < n_cmp)
    bias_c = jnp.where(valid_c, lookup(t - cend), NEG_INF)
    steps = seq // CMP_TQ
    bias_c = bias_c.reshape(NSA_KV_HEADS, NSA_GQA, steps, CMP_TQ, QB).transpose(0, 2, 1, 3, 4)
    bias_c = bias_c.reshape(NSA_KV_HEADS * steps, NSA_GQA * CMP_TQ, QB)
    r = jnp.arange(QB)[:, None]
    m = jnp.arange(QB)[None, :]
    near = jnp.stack([lookup(r - m), lookup(QB + r - m)])
    near = near - rel_bias[REL_BUCKETS - 1][None, :, None, None]
    near = near.at[0].add(jnp.where(m > r, NEG_INF, 0.0))
    return bias_c, near.reshape(2, NSA_KV_HEADS, NSA_GQA * QB, QB)


def _expand_table(seq):
    j = jnp.arange(QB)[:, None]
    s = jnp.arange(seq)[None, :]
    return (s // SEL_BLOCK == j).astype(BF16)


def _overlap_table(seq):
    n_cmp = (seq - CMP_BLOCK) // CMP_STRIDE + 1
    cs = jnp.arange(QB)[:, None] * CMP_STRIDE
    ce = cs + CMP_BLOCK - 1
    ss = jnp.arange(QB)[None, :] * SEL_BLOCK
    ov = jnp.maximum(jnp.minimum(ce, ss + SEL_BLOCK - 1) - jnp.maximum(cs, ss) + 1, 0).astype(F32) / CMP_BLOCK
    keep = (jnp.arange(QB)[:, None] < n_cmp) & (jnp.arange(QB)[None, :] < seq // SEL_BLOCK)
    return jnp.where(keep, ov, 0.0).T.astype(BF16)


def _in_proj_weights(w):
    w = w.astype(BF16)
    scale = HEAD_DIM ** -0.5
    o_q, o_kv, o_gate = 0, NSA_WIDTH, NSA_WIDTH + 768
    o_sb = o_gate + 3 * NSA_HEADS
    o_z = o_sb + 3 * SB_WIDTH
    o_xbc = o_z + SSD_INNER
    o_dt = o_xbc + SSD_INNER + 2 * SSD_GROUPS * SSD_STATE
    col = lambda a, n: w[:, a:a + n]
    kv = lambda br, which, h: col(o_kv + br * 256 + which * 128 + h * HEAD_DIM, HEAD_DIM)
    zeros = lambda n: jnp.zeros((w.shape[0], n), w.dtype)
    branch = lambda br: [kv(br, 0, 0), kv(br, 1, 0), kv(br, 0, 1), kv(br, 1, 1)]
    q_heads = []
    for h in range(NSA_HEADS):
        q_heads += [col(o_q + h * HEAD_DIM, HEAD_DIM) * scale, zeros(128 - HEAD_DIM)]
    wb = jnp.concatenate(
        q_heads + branch(1) + branch(2)
        + [col(o_sb, SB_WIDTH) * scale, col(o_sb + SB_WIDTH, 2 * SB_WIDTH)], axis=1)
    gates = 3 * NSA_GQA
    wf = jnp.concatenate(
        [col(o_z, SSD_INNER), col(o_xbc, SSD_INNER + 2 * SSD_GROUPS * SSD_STATE), col(o_kv, 256),
         col(o_gate, gates), zeros(MISC_DT - gates), col(o_dt, SSD_HEADS), zeros(128 - MISC_DT - SSD_HEADS),
         col(o_gate + gates, gates), zeros(128 - gates)], axis=1)
    assert wb.shape[1] == PB_COLS and wf.shape[1] == PF_COLS
    return _tile_major(wb, PROJ_TN), _tile_major(wf, PROJ_TN)


def _head_params(dt_bias, a_log):
    rows = jnp.stack([dt_bias, a_log]).astype(F32)
    return jnp.zeros((8, 128), F32).at[0:2, MISC_DT:MISC_DT + SSD_HEADS].set(rows)


def _mixer(x, l, tables, batch, seq, mix_norm, w_in, w_out, cmp_pos, cmp_w, nsa_norm, sb_norm,
           conv_w, conv_b, dt_bias, a_log, d_skip, ssd_norm):
    bias_c, near, overlap, expand = tables
    wb, wf = _in_proj_weights(w_in[l])
    pb = _rms_matmul(x, mix_norm[l], wb, BF16, name="in_proj_bf16")
    pf = _rms_matmul(x, mix_norm[l], wf, F32, name="in_proj_f32")

    ng = seq // CMP_STRIDE
    ucmp = pf[:, PF_CMP:PF_CMP + 256].reshape(batch, ng, CMP_STRIDE, 4, HEAD_DIM)
    ucmp = ucmp.transpose(0, 3, 1, 2, 4).reshape(batch, 4, ng, CMP_STRIDE * HEAD_DIM)
    pos2 = cmp_pos[l].reshape(2, 2, CMP_STRIDE * HEAD_DIM)
    o_cmp, sel = _nsa_cmp(pb, ucmp, pos2, cmp_w[l].astype(BF16), bias_c, overlap, batch, seq)
    o_nsa = _nsa_sel_win(pb, pf, o_cmp, sel, expand, near, batch, seq)
    o_sb = _sb_attention(pb, batch, seq)
    o_ssd = _ssd(pf, conv_w[l], conv_b[l], _head_params(dt_bias[l], a_log[l]), d_skip[l], ssd_norm[l],
                 batch, seq)
    return _out_proj(x, o_nsa, o_sb, o_ssd, nsa_norm[l], sb_norm[l],
                     _tile_major(w_out[l].astype(BF16), PROJ_TN))


def kernel(x, rel_bias, ffn1_norm, ffn1_w_gate, ffn1_w_up, ffn1_w_down, mix_norm, w_in, w_out, nsa_cmp_pos, nsa_cmp_w, nsa_out_norm, sb_out_norm, ssd_conv_w, ssd_conv_b, ssd_dt_bias, ssd_a_log, ssd_d, ssd_out_norm, ffn2_norm, ffn2_w_gate, ffn2_w_up, ffn2_w_down, final_norm):
    batch, seq, d = x.shape
    depth = w_in.shape[0]
    tables = _bias_tables(rel_bias, seq) + (_overlap_table(seq), _expand_table(seq))
    h = x.reshape(batch * seq, d)
    ffn1_w_gate, ffn1_w_up, ffn2_w_gate, ffn2_w_up = (
        _tile_major(w.astype(BF16), FFN_TF) for w in (ffn1_w_gate, ffn1_w_up, ffn2_w_gate, ffn2_w_up))
    ffn1_w_down, ffn2_w_down = ffn1_w_down.astype(BF16), ffn2_w_down.astype(BF16)
    for l in range(depth):
        h = _ffn(h, ffn1_norm[l], ffn1_w_gate, ffn1_w_up, ffn1_w_down, l, final_norm, final=False)
        h = _mixer(h, l, tables, batch, seq, mix_norm, w_in, w_out, nsa_cmp_pos, nsa_cmp_w, nsa_out_norm,
                   sb_out_norm, ssd_conv_w, ssd_conv_b, ssd_dt_bias, ssd_a_log, ssd_d, ssd_out_norm)
        h = _ffn(h, ffn2_norm[l], ffn2_w_gate, ffn2_w_up, ffn2_w_down, l, final_norm,
                 final=(l == depth - 1))
    return h.reshape(batch, seq, d)
```

```python
import functools
import math

import jax
import jax.numpy as jnp
from jax import lax
from jax.experimental import pallas as pl
from jax.experimental.pallas import tpu as pltpu

D_MODEL = 2048
D_FF = 5632
HEAD_DIM = 64
QB = 128
NEG_INF = -1e30
EPS = 1e-6

NSA_HEADS = 8
NSA_KV_HEADS = 2
NSA_GQA = NSA_HEADS // NSA_KV_HEADS
NSA_WIDTH = NSA_HEADS * HEAD_DIM
CMP_BLOCK = 32
CMP_STRIDE = 16
SEL_BLOCK = 64
SEL_TOPK = 8
FORCE_SCORE = 1e3
WINDOW = 512
SB_HEADS = 8
SB_WIDTH = SB_HEADS * HEAD_DIM
SSD_HEADS = 16
SSD_INNER = SSD_HEADS * HEAD_DIM
SSD_GROUPS = 2
SSD_HPG = SSD_HEADS // SSD_GROUPS
SSD_STATE = 128
SSD_CONV = 4
REL_BUCKETS = 32
REL_MAX_DIST = 128

PB_NSA_Q, PB_SEL, PB_WIN, PB_SB_Q, PB_SB_K, PB_SB_V, PB_COLS = 0, 1024, 1280, 1536, 2048, 2560, 3072
PF_Z, PF_XS, PF_BC, PF_CMP, PF_MISC, PF_COLS = 0, 1024, 2048, 2560, 2816, 3072
MISC_DT = 16

VMEM_LIMIT = 52 * 1024 * 1024

BF16 = jnp.bfloat16
F32 = jnp.float32


def _dot(a, b):
    return jnp.dot(a, b, preferred_element_type=F32)


def _dot_nt(a, b):
    return lax.dot_general(a, b, (((1,), (1,)), ((), ())), preferred_element_type=F32)


def _split2_dot(x, m):
    hi = x.astype(BF16)
    lo = (x - hi.astype(F32)).astype(BF16)
    return _dot(hi, m) + _dot(lo, m)


def _split3_dot(x, m):
    hi = x.astype(BF16)
    r1 = x - hi.astype(F32)
    mid = r1.astype(BF16)
    lo = (r1 - mid.astype(F32)).astype(BF16)
    return _dot(hi, m) + _dot(mid, m) + _dot(lo, m)


def _split3_dot_left(m, x):
    hi = x.astype(BF16)
    r1 = x - hi.astype(F32)
    mid = r1.astype(BF16)
    lo = (r1 - mid.astype(F32)).astype(BF16)
    return _dot(m, hi) + _dot(m, mid) + _dot(m, lo)


def _rms(x, g):
    return x * lax.rsqrt(jnp.mean(x * x, axis=-1, keepdims=True) + EPS) * g


def _silu(x):
    return x / (1.0 + jnp.exp(-x))


def _sigmoid(x):
    return 1.0 / (1.0 + jnp.exp(-x))


def _params(sem):
    return pltpu.CompilerParams(dimension_semantics=sem, vmem_limit_bytes=VMEM_LIMIT)


def _ffn_kernel(x_ref, g_ref, wg_ref, wu_ref, wd_ref, fg_ref, o_ref, h_ref, acc_ref, *, final):
    f = pl.program_id(1)

    @pl.when(f == 0)
    def _():
        h_ref[...] = _rms(x_ref[...], g_ref[...]).astype(BF16)
        acc_ref[...] = jnp.zeros_like(acc_ref)

    h = h_ref[...]
    gate = _dot(h, wg_ref[...])
    up = _dot(h, wu_ref[...])
    acc_ref[...] += _dot((_silu(gate) * up).astype(BF16), wd_ref[...])

    @pl.when(f == pl.num_programs(1) - 1)
    def _():
        y = x_ref[...] + 0.5 * acc_ref[...]
        if final:
            y = _rms(y, fg_ref[...])
        o_ref[...] = y


def _ffn(x, g, wg, wu, wd, layer, final_g, *, final, tm=512, tf=512):
    n, d = x.shape
    dff = wg.shape[2]
    return pl.pallas_call(
        functools.partial(_ffn_kernel, final=final),
        grid=(n // tm, dff // tf),
        in_specs=[
            pl.BlockSpec((tm, d), lambda i, f: (i, 0)),
            pl.BlockSpec((1, d), lambda i, f: (0, 0)),
            pl.BlockSpec((None, d, tf), lambda i, f: (layer, 0, f)),
            pl.BlockSpec((None, d, tf), lambda i, f: (layer, 0, f)),
            pl.BlockSpec((None, tf, d), lambda i, f: (layer, f, 0)),
            pl.BlockSpec((1, d), lambda i, f: (0, 0)),
        ],
        out_specs=pl.BlockSpec((tm, d), lambda i, f: (i, 0)),
        out_shape=jax.ShapeDtypeStruct((n, d), F32),
        scratch_shapes=[pltpu.VMEM((tm, d), BF16), pltpu.VMEM((tm, d), F32)],
        compiler_params=_params(("parallel", "arbitrary")),
        name="ffn",
    )(x, g.reshape(1, d), wg, wu, wd, final_g.reshape(1, d))


def _rms_matmul_kernel(x_ref, g_ref, w_ref, o_ref, h_ref):
    @pl.when(pl.program_id(1) == 0)
    def _():
        h_ref[...] = _rms(x_ref[...], g_ref[...]).astype(BF16)

    o_ref[...] = _dot(h_ref[...], w_ref[...]).astype(o_ref.dtype)


def _rms_matmul(x, g, w, out_dtype, *, tm=1024, tn=1024, name):
    n, d = x.shape
    c = w.shape[1]
    return pl.pallas_call(
        _rms_matmul_kernel,
        grid=(n // tm, c // tn),
        in_specs=[
            pl.BlockSpec((tm, d), lambda i, j: (i, 0)),
            pl.BlockSpec((1, d), lambda i, j: (0, 0)),
            pl.BlockSpec((d, tn), lambda i, j: (0, j)),
        ],
        out_specs=pl.BlockSpec((tm, tn), lambda i, j: (i, j)),
        out_shape=jax.ShapeDtypeStruct((n, c), out_dtype),
        scratch_shapes=[pltpu.VMEM((tm, d), BF16)],
        compiler_params=_params(("parallel", "arbitrary")),
        name=name,
    )(x, g.reshape(1, d), w)


SB_GROUP = 8
SB_DEAD = 104.0


def _sb_kernel(q_ref, k_ref, v_ref, o_ref, kbd_ref, vbd_ref):
    i = pl.program_id(2)
    pairs = SB_GROUP // 2
    nkb = k_ref.shape[0] // QB
    r = lax.broadcasted_iota(jnp.int32, (QB, QB), 0)
    c = lax.broadcasted_iota(jnp.int32, (QB, QB), 1)
    strict_all = jnp.concatenate([c < r] * SB_GROUP, axis=0)
    later_ones = jnp.concatenate([(r > c).astype(BF16), jnp.ones((QB, QB), BF16)], axis=1)
    later_ones = jnp.concatenate([later_ones, later_ones], axis=0)

    @pl.when(i == 0)
    def _():
        low = c < HEAD_DIM
        zero16 = jnp.zeros((QB, QB), BF16)

        def build(j, carry):
            src = pl.ds(pl.multiple_of(j * QB, QB), QB)
            dst = pl.ds(pl.multiple_of(j * 2 * QB, 2 * QB), 2 * QB)
            for pp in range(pairs):
                for ref, out in ((k_ref, kbd_ref), (v_ref, vbd_ref)):
                    x = ref[src, pp * 128:(pp + 1) * 128]
                    out[pp, dst, :] = jnp.concatenate([jnp.where(low, x, zero16), jnp.where(low, zero16, x)],
                                                      axis=0)
            return carry

        lax.fori_loop(0, nkb, build, 0)

    def scores(j, diagonal):
        rows = pl.ds(pl.multiple_of(j * 2 * QB, 2 * QB), 2 * QB)
        zs = []
        for pp in range(pairs):
            z_pair = _dot_nt(q_ref[:, pp * 128:(pp + 1) * 128], kbd_ref[pp, rows, :])
            zs += [z_pair[:, 0:QB], z_pair[:, QB:2 * QB]]
        z = jnp.concatenate(zs, axis=0)
        soft = jnp.maximum(z, 0.0) + jnp.log(1.0 + jnp.exp(-jnp.abs(z)))
        if diagonal:
            soft = jnp.where(strict_all, soft, 0.0)
        hi = soft.astype(BF16)
        lo = (soft - hi.astype(F32)).astype(BF16)
        sums = _dot(jnp.concatenate([hi, lo], axis=1), later_ones)
        return rows, z - soft, sums

    def absorb(tails, accs, rows, log_beta, sums, diagonal):
        a = jnp.exp(log_beta - (tails + sums[:, 0:QB]))
        if diagonal:
            a = jnp.where(strict_all, a, 0.0)
        a = a.astype(BF16)
        outs = []
        for pp in range(pairs):
            pair = jnp.concatenate([a[2 * pp * QB:(2 * pp + 1) * QB], a[(2 * pp + 1) * QB:(2 * pp + 2) * QB]],
                                   axis=1)
            outs.append(_dot(pair, vbd_ref[pp, rows, :]))
        return tails + sums[:, QB:2 * QB], accs + jnp.concatenate(outs, axis=0)

    def sweep(js, carry):
        staged = [scores(j, False) for j in js]
        for st in staged:
            carry = absorb(*carry, *st, False)
        return carry

    carry = (jnp.zeros((SB_GROUP * QB, QB), F32), jnp.zeros((pairs * QB, 128), F32))
    carry = absorb(*carry, *scores(i, True), True)
    odd = i % 2
    carry = lax.fori_loop(0, odd, lambda step, cr: sweep([i - 1], cr), carry)
    top = i - 1 - odd

    def live(state):
        step, smallest, _, _ = state
        return (step < i // 2) & (smallest < SB_DEAD)

    def pair_step(state):
        step, _, tails, accs = state
        tails, accs = sweep([top - 2 * step, top - 2 * step - 1], (tails, accs))
        return step + 1, jnp.min(tails), tails, accs

    accs = lax.while_loop(live, pair_step, (0, jnp.min(carry[0]), *carry))[3]
    for pp in range(pairs):
        o_ref[:, pp * 128:(pp + 1) * 128] = accs[pp * QB:(pp + 1) * QB]


def _sb_attention(pb, batch, seq):
    nqb = seq // QB
    w = SB_GROUP * HEAD_DIM
    qc, kc, vc = PB_SB_Q // w, PB_SB_K // w, PB_SB_V // w
    return pl.pallas_call(
        _sb_kernel,
        grid=(batch, SB_HEADS // SB_GROUP, nqb),
        in_specs=[
            pl.BlockSpec((QB, w), lambda b, p, i: (b * nqb + i, qc + p)),
            pl.BlockSpec((seq, w), lambda b, p, i: (b, kc + p)),
            pl.BlockSpec((seq, w), lambda b, p, i: (b, vc + p)),
        ],
        out_specs=pl.BlockSpec((QB, w), lambda b, p, i: (b * nqb + i, p)),
        out_shape=jax.ShapeDtypeStruct((batch * seq, SB_WIDTH), F32),
        scratch_shapes=[pltpu.VMEM((SB_GROUP // 2, 2 * seq, 128), BF16),
                        pltpu.VMEM((SB_GROUP // 2, 2 * seq, 128), BF16)],
        compiler_params=_params(("parallel", "parallel", "arbitrary")),
        name="sb_attention",
    )(pb, pb, pb)


CMP_TQ = 4 * QB


def _nsa_cmp_kernel(q_ref, uk_ref, uv_ref, pos_ref, w_ref, bias_ref, ovt_ref, ocmp_ref, sel_ref,
                    kc_ref, vc_ref, *, n_cmp, n_sel):
    i = pl.program_id(2)
    half = CMP_STRIDE * HEAD_DIM

    @pl.when(i == 0)
    def _():
        def compress(kv, u_ref):
            u = u_ref[0, 0]
            top = _dot((u + pos_ref[kv, 0:1, :]).astype(BF16), w_ref[kv, 0:half, :])
            bot = _dot((u + pos_ref[kv, 1:2, :]).astype(BF16), w_ref[kv, half:2 * half, :])
            return top + pltpu.roll(bot, QB - 1, 0)

        zeros = jnp.zeros((QB, HEAD_DIM), F32)
        kc_ref[...] = jnp.concatenate([compress(0, uk_ref), zeros], axis=1).astype(BF16)
        v = compress(1, uv_ref)
        vc_ref[...] = jnp.concatenate([jnp.concatenate([v, zeros], axis=1),
                                       jnp.concatenate([zeros, v], axis=1)], axis=0).astype(BF16)

    G = NSA_GQA
    TQ = CMP_TQ
    q4 = jnp.concatenate([q_ref[:, g * 128:(g + 1) * 128] for g in range(G)], axis=0)
    s = _dot_nt(q4, kc_ref[...]) + bias_ref[0]
    e = jnp.exp(s - jnp.max(s, axis=-1, keepdims=True))
    t_rows = i * TQ + lax.broadcasted_iota(jnp.int32, (TQ, 1), 0)
    any_valid = jnp.concatenate([t_rows >= CMP_BLOCK - 1] * G, axis=0)
    p = jnp.where(any_valid, e / jnp.sum(e, axis=-1, keepdims=True), 0.0)
    p16 = p.astype(BF16)
    for pp in range(G // 2):
        pair = jnp.concatenate([p16[2 * pp * TQ:(2 * pp + 1) * TQ], p16[(2 * pp + 1) * TQ:(2 * pp + 2) * TQ]],
                               axis=1)
        ocmp_ref[:, pp * 128:(pp + 1) * 128] = _dot(pair, vc_ref[...])
    p_all = p[0:TQ]
    for g in range(1, G):
        p_all = p_all + p[g * TQ:(g + 1) * TQ]

    blk = lax.broadcasted_iota(jnp.int32, (n_sel, QB), 0)
    for part in range(TQ // QB):
        rs = slice(part * QB, (part + 1) * QB)
        t0 = (i * (TQ // QB) + part) * QB
        p_sum = p_all[rs]
        hi = p_sum.astype(BF16)
        lo = (p_sum - hi.astype(F32)).astype(BF16)
        p_sel = (_dot_nt(ovt_ref[...], hi) + _dot_nt(ovt_ref[...], lo))[0:n_sel]
        t = t0 + lax.broadcasted_iota(jnp.int32, (n_sel, QB), 1)
        cur = t // SEL_BLOCK
        eligible = blk * SEL_BLOCK <= t
        forced = (blk == 0) | (blk == cur) | (blk == cur - 1)
        score = jnp.where(eligible, p_sel + jnp.where(forced, FORCE_SCORE, 0.0), NEG_INF)
        rank = jnp.zeros((n_sel, QB), F32)
        for j in range(n_sel):
            other = score[j:j + 1, :]
            ahead = (other > score) | ((other == score) & (blk > j))
            rank = rank + jnp.where(ahead, 1.0, 0.0)
        chosen = jnp.where(eligible & (rank < SEL_TOPK), 1.0, 0.0)
        chosen = jnp.concatenate([chosen, jnp.zeros((QB - n_sel, QB), F32)], axis=0)
        sel_ref[0, 0, rs, :] = chosen.T.astype(BF16)


def _nsa_cmp(pb, ucmp, pos2, cmp_w, bias_c, overlap_t, batch, seq):
    nqb = seq // CMP_TQ
    n_cmp = (seq - CMP_BLOCK) // CMP_STRIDE + 1
    n_sel = seq // SEL_BLOCK
    ng = seq // CMP_STRIDE
    assert ng == QB and n_sel <= QB and n_sel % 8 == 0
    wide = CMP_STRIDE * HEAD_DIM
    return pl.pallas_call(
        functools.partial(_nsa_cmp_kernel, n_cmp=n_cmp, n_sel=n_sel),
        grid=(batch, NSA_KV_HEADS, nqb),
        in_specs=[
            pl.BlockSpec((CMP_TQ, NSA_GQA * 128), lambda b, h, i: (b * nqb + i, h)),
            pl.BlockSpec((1, 1, ng, wide), lambda b, h, i: (b, h, 0, 0)),
            pl.BlockSpec((1, 1, ng, wide), lambda b, h, i: (b, NSA_KV_HEADS + h, 0, 0)),
            pl.BlockSpec((2, 2, wide), lambda b, h, i: (0, 0, 0)),
            pl.BlockSpec((2, 2 * wide, HEAD_DIM), lambda b, h, i: (0, 0, 0)),
            pl.BlockSpec((1, NSA_GQA * CMP_TQ, QB), lambda b, h, i: (h * nqb + i, 0, 0)),
            pl.BlockSpec((QB, QB), lambda b, h, i: (0, 0)),
        ],
        out_specs=[
            pl.BlockSpec((CMP_TQ, 256), lambda b, h, i: (b * nqb + i, h)),
            pl.BlockSpec((1, 1, CMP_TQ, QB), lambda b, h, i: (b, h, i, 0)),
        ],
        out_shape=[
            jax.ShapeDtypeStruct((batch * seq, NSA_WIDTH), F32),
            jax.ShapeDtypeStruct((batch, NSA_KV_HEADS, seq, QB), BF16),
        ],
        scratch_shapes=[pltpu.VMEM((QB, 128), BF16), pltpu.VMEM((2 * QB, 128), BF16)],
        compiler_params=_params(("parallel", "parallel", "arbitrary")),
        name="nsa_compressed",
    )(pb, ucmp, ucmp, pos2, cmp_w, bias_c, overlap_t)


SEL_CHUNK = 4 * QB
WIN_FAR = WINDOW - QB


def _nsa_sw_kernel(q_ref, ks_ref, kw_ref, ocmp_ref, misc_ref, sel_ref, exp_ref, tb_ref, o_ref, madd_ref):
    i = pl.program_id(2)
    G = NSA_GQA
    q4 = jnp.concatenate([q_ref[:, g * 128:(g + 1) * 128] for g in range(G)], axis=0)
    madd_ref[...] = (_dot(sel_ref[0, 0], exp_ref[...]) - 1.0) * (-NEG_INF)
    t_pos = i * QB + lax.broadcasted_iota(jnp.int32, (QB, 1), 0)

    low = lax.broadcasted_iota(jnp.int32, (QB, 128), 1) < HEAD_DIM

    def tile(x, n=G):
        return jnp.concatenate([x] * n, axis=0)

    def ones_v(kv):
        return jnp.where(tile(low, kv.shape[0] // QB), jnp.ones_like(kv), kv)

    rows0 = pl.ds(pl.multiple_of(i * QB, QB), QB)
    rows1 = pl.ds(pl.multiple_of(jnp.maximum(i - 1, 0) * QB, QB), QB)
    gone = jnp.where(i >= 1, 0.0, NEG_INF)
    far_end = (i - 1) * QB

    kv = jnp.concatenate([ks_ref[rows1, :], ks_ref[rows0, :]], axis=0)
    bias = jnp.concatenate([tb_ref[1, 0] + tile(madd_ref[:, rows1] + gone),
                            tb_ref[0, 0] + tile(madd_ref[:, rows0])], axis=1)
    s = _dot_nt(q4, kv) + bias
    m = jnp.max(s, axis=-1, keepdims=True)
    acc = _dot(jnp.exp(s - m).astype(BF16), ones_v(kv))

    def sel_far(cidx, state):
        m, acc = state
        cols = pl.ds(pl.multiple_of(cidx * SEL_CHUNK, SEL_CHUNK), SEL_CHUNK)
        k_pos = cidx * SEL_CHUNK + lax.broadcasted_iota(jnp.int32, (QB, SEL_CHUNK), 1)
        add = jnp.where(k_pos < far_end, madd_ref[:, cols], NEG_INF)
        kv = ks_ref[cols, :]
        s = _dot_nt(q4, kv) + tile(add)
        m_new = jnp.maximum(m, jnp.max(s, axis=-1, keepdims=True))
        return m_new, jnp.exp(m - m_new) * acc + _dot(jnp.exp(s - m_new).astype(BF16), ones_v(kv))

    _, sel_acc = lax.fori_loop(0, (i + 2) // (SEL_CHUNK // QB), sel_far, (m, acc))

    start = jnp.maximum(i - WINDOW // QB, 0) * QB
    rows_far = pl.ds(pl.multiple_of(start, QB), WIN_FAR)
    k_pos = start + lax.broadcasted_iota(jnp.int32, (QB, WIN_FAR), 1)
    live = (k_pos < far_end) & (k_pos > t_pos - WINDOW)
    kv = jnp.concatenate([kw_ref[rows_far, :], kw_ref[rows1, :], kw_ref[rows0, :]], axis=0)
    bias = jnp.concatenate([tile(jnp.where(live, 0.0, NEG_INF)), tb_ref[1, 0] + gone, tb_ref[0, 0]], axis=1)
    s = _dot_nt(q4, kv) + bias
    win_acc = _dot(jnp.exp(s - jnp.max(s, axis=-1, keepdims=True)).astype(BF16), ones_v(kv))

    gate = _sigmoid(misc_ref[...])

    def pair(acc, pp):
        a, b = acc[2 * pp * QB:(2 * pp + 1) * QB], acc[(2 * pp + 1) * QB:(2 * pp + 2) * QB]
        return (jnp.where(low, pltpu.roll(a, HEAD_DIM, 1), b)
                / jnp.where(low, a, pltpu.roll(b, HEAD_DIM, 1)))

    def pair_gate(branch, pp):
        ca, cb = 3 * (2 * pp) + branch, 3 * (2 * pp + 1) + branch
        return jnp.where(low, gate[:, ca:ca + 1], gate[:, cb:cb + 1])

    for pp in range(G // 2):
        lanes = slice(pp * 128, (pp + 1) * 128)
        o_ref[:, lanes] = (pair_gate(0, pp) * ocmp_ref[:, lanes] + pair_gate(1, pp) * pair(sel_acc, pp)
                           + pair_gate(2, pp) * pair(win_acc, pp))


def _nsa_sel_win(pb, pf, o_cmp, sel, expand, tb, batch, seq):
    nqb = seq // QB
    G = NSA_GQA
    return pl.pallas_call(
        _nsa_sw_kernel,
        grid=(batch, NSA_KV_HEADS, nqb),
        in_specs=[
            pl.BlockSpec((QB, G * 128), lambda b, h, i: (b * nqb + i, h)),
            pl.BlockSpec((seq, 128), lambda b, h, i: (b, PB_SEL // 128 + h)),
            pl.BlockSpec((seq, 128), lambda b, h, i: (b, PB_WIN // 128 + h)),
            pl.BlockSpec((QB, 256), lambda b, h, i: (b * nqb + i, h)),
            pl.BlockSpec((QB, 128), lambda b, h, i: (b * nqb + i, PF_MISC // 128 + h)),
            pl.BlockSpec((1, 1, QB, QB), lambda b, h, i: (b, h, i, 0)),
            pl.BlockSpec((QB, seq), lambda b, h, i: (0, 0)),
            pl.BlockSpec((2, 1, G * QB, QB), lambda b, h, i: (0, h, 0, 0)),
        ],
        out_specs=pl.BlockSpec((QB, 256), lambda b, h, i: (b * nqb + i, h)),
        out_shape=jax.ShapeDtypeStruct((batch * seq, NSA_WIDTH), F32),
        scratch_shapes=[pltpu.VMEM((QB, seq), F32)],
        compiler_params=_params(("parallel", "parallel", "arbitrary")),
        name="nsa_selected_window",
    )(pb, pb, pb, o_cmp, pf, sel, expand, tb)


def _ssd_kernel(z_ref, xs_ref, bc_ref, misc_ref, cwx_ref, cwb_ref, cbx_ref, cbb_ref, hp_ref, spread_ref,
                dskip_ref, ng_ref, o_ref, xbuf, bbuf, state):
    ci = pl.program_id(1)
    L = QB
    P = HEAD_DIM
    GN = SSD_GROUPS * SSD_STATE

    @pl.when(ci == 0)
    def _():
        xbuf[...] = jnp.zeros_like(xbuf)
        bbuf[...] = jnp.zeros_like(bbuf)
        state[...] = jnp.zeros_like(state)

    def conv_silu(buf, src_ref, w_ref, b_ref):
        cur = src_ref[...]
        ext = jnp.concatenate([buf[...], cur], axis=0)
        out = b_ref[...] + w_ref[SSD_CONV - 1:SSD_CONV, :] * cur
        for k in range(1, SSD_CONV):
            out = out + w_ref[SSD_CONV - 1 - k:SSD_CONV - k, :] * pltpu.roll(ext, k, 0)[8:8 + L]
        buf[...] = cur[L - 8:L]
        return _silu(out)

    xs = conv_silu(xbuf, xs_ref, cwx_ref, cbx_ref)
    bcs = conv_silu(bbuf, bc_ref, cwb_ref, cbb_ref)

    misc = misc_ref[...]
    pre = misc + hp_ref[0:1, :]
    dt = jnp.maximum(pre, 0.0) + jnp.log1p(jnp.exp(-jnp.abs(pre)))
    a_dt = dt * (-jnp.exp(hp_ref[1:2, :]))
    r = lax.broadcasted_iota(jnp.int32, (L, L), 0)
    c = lax.broadcasted_iota(jnp.int32, (L, L), 1)
    causal = r >= c
    a_cs = _split3_dot_left(causal.astype(BF16), a_dt)
    a_cs_t = a_cs.T

    dt_full = _split3_dot(dt, spread_ref[...])
    acs_full = _split3_dot(a_cs, spread_ref[...])
    total_full = acs_full[L - 1:L, :]
    xdt = xs * dt_full
    xdt16 = xdt.astype(BF16)
    to_end16 = (xdt * jnp.exp(total_full - acs_full)).astype(BF16)
    decay_in = jnp.exp(acs_full)
    chunk_decay = jnp.exp(total_full)
    low = c < P
    zero16 = jnp.zeros((L, 2 * P), BF16)
    width = SSD_HPG * P

    ys = []
    for g in range(SSD_GROUPS):
        bm = bcs[:, g * SSD_STATE:(g + 1) * SSD_STATE]
        cm16 = bcs[:, GN + g * SSD_STATE:GN + (g + 1) * SSD_STATE].astype(BF16)
        cb = _dot_nt(cm16, bm.astype(BF16))
        lanes = slice(g * width, (g + 1) * width)
        h_in = state[g]
        y_off = _dot(cm16, h_in.astype(BF16)) * decay_in[:, lanes]
        state[g] = h_in * chunk_decay[:, lanes] + _dot(bm.T.astype(BF16), to_end16[:, lanes])
        diag = []
        for pp in range(SSD_HPG // 2):
            hd = g * SSD_HPG + 2 * pp
            decayed = []
            for col in (MISC_DT + hd, MISC_DT + hd + 1):
                seg = jnp.exp(jnp.where(causal, a_cs[:, col:col + 1] - a_cs_t[col:col + 1, :], -jnp.inf))
                decayed.append((cb * seg).astype(BF16))
            xp = xdt16[:, hd * P:(hd + 2) * P]
            x_bd = jnp.concatenate([jnp.where(low, xp, zero16), jnp.where(low, zero16, xp)], axis=0)
            diag.append(_dot(jnp.concatenate(decayed, axis=1), x_bd))
        ys.append(jnp.concatenate(diag, axis=1) + y_off)

    y = (jnp.concatenate(ys, axis=1) + dskip_ref[...] * xs) * _silu(z_ref[...])
    for g in range(SSD_GROUPS):
        lanes = slice(g * width, (g + 1) * width)
        yg = y[:, lanes]
        o_ref[:, lanes] = yg * lax.rsqrt(jnp.mean(yg * yg, axis=-1, keepdims=True) + EPS) * ng_ref[:, lanes]


def _ssd(pf, conv_w, conv_b, head_params, d_skip, norm_g, batch, seq):
    nc = seq // QB
    GN = SSD_GROUPS * SSD_STATE
    lane_head = jnp.arange(SSD_INNER)[None, :] // HEAD_DIM
    spread = (jnp.arange(128)[:, None] == MISC_DT + lane_head).astype(BF16)
    cwx, cwb = conv_w[:, :SSD_INNER], conv_w[:, SSD_INNER:]
    cbx, cbb = conv_b[:SSD_INNER].reshape(1, -1), conv_b[SSD_INNER:].reshape(1, -1)
    full = lambda shape: pl.BlockSpec(shape, lambda b, ci: (0,) * len(shape))
    return pl.pallas_call(
        _ssd_kernel,
        grid=(batch, nc),
        in_specs=[
            pl.BlockSpec((QB, SSD_INNER), lambda b, ci: (b * nc + ci, PF_Z // SSD_INNER)),
            pl.BlockSpec((QB, SSD_INNER), lambda b, ci: (b * nc + ci, PF_XS // SSD_INNER)),
            pl.BlockSpec((QB, 2 * GN), lambda b, ci: (b * nc + ci, PF_BC // (2 * GN))),
            pl.BlockSpec((QB, 128), lambda b, ci: (b * nc + ci, PF_MISC // 128)),
            full((SSD_CONV, SSD_INNER)),
            full((SSD_CONV, 2 * GN)),
            full((1, SSD_INNER)),
            full((1, 2 * GN)),
            full((8, 128)),
            full((128, SSD_INNER)),
            full((1, SSD_INNER)),
            full((1, SSD_INNER)),
        ],
        out_specs=pl.BlockSpec((QB, SSD_INNER), lambda b, ci: (b * nc + ci, 0)),
        out_shape=jax.ShapeDtypeStruct((batch * seq, SSD_INNER), F32),
        scratch_shapes=[
            pltpu.VMEM((8, SSD_INNER), F32),
            pltpu.VMEM((8, 2 * GN), F32),
            pltpu.VMEM((SSD_GROUPS, SSD_STATE, SSD_HPG * HEAD_DIM), F32),
        ],
        compiler_params=_params(("parallel", "arbitrary")),
        name="ssd",
    )(pf, pf, pf, pf, cwx, cwb, cbx, cbb, head_params, spread,
      jnp.repeat(d_skip.astype(F32), HEAD_DIM).reshape(1, -1), norm_g.reshape(1, -1))


def _out_proj_kernel(x_ref, nsa_ref, sb_ref, ssd_ref, gn_ref, gs_ref, w_ref, o_ref, mix_ref):
    @pl.when(pl.program_id(1) == 0)
    def _():
        mix_ref[:, 0:NSA_WIDTH] = _rms(nsa_ref[...], gn_ref[...]).astype(BF16)
        mix_ref[:, NSA_WIDTH:NSA_WIDTH + SB_WIDTH] = _rms(sb_ref[...], gs_ref[...]).astype(BF16)
        mix_ref[:, NSA_WIDTH + SB_WIDTH:] = ssd_ref[...].astype(BF16)

    o_ref[...] = x_ref[...] + _dot(mix_ref[...], w_ref[...])


def _out_proj(x, o_nsa, o_sb, o_ssd, g_nsa, g_sb, w, *, tm=1024, tn=1024):
    n, d = x.shape
    dm = w.shape[0]
    return pl.pallas_call(
        _out_proj_kernel,
        grid=(n // tm, d // tn),
        in_specs=[
            pl.BlockSpec((tm, tn), lambda i, j: (i, j)),
            pl.BlockSpec((tm, NSA_WIDTH), lambda i, j: (i, 0)),
            pl.BlockSpec((tm, SB_WIDTH), lambda i, j: (i, 0)),
            pl.BlockSpec((tm, SSD_INNER), lambda i, j: (i, 0)),
            pl.BlockSpec((1, NSA_WIDTH), lambda i, j: (0, 0)),
            pl.BlockSpec((1, SB_WIDTH), lambda i, j: (0, 0)),
            pl.BlockSpec((dm, tn), lambda i, j: (0, j)),
        ],
        out_specs=pl.BlockSpec((tm, tn), lambda i, j: (i, j)),
        out_shape=jax.ShapeDtypeStruct((n, d), F32),
        scratch_shapes=[pltpu.VMEM((tm, dm), BF16)],
        compiler_params=_params(("parallel", "arbitrary")),
        name="out_proj",
    )(x, o_nsa, o_sb, o_ssd, g_nsa.reshape(1, -1), g_sb.reshape(1, -1), w)


def _rel_bucket(dist):
    dist = jnp.maximum(dist, 0)
    max_exact = REL_BUCKETS // 2
    log_ratio = jnp.log(jnp.maximum(dist, 1).astype(F32) / max_exact) / math.log(REL_MAX_DIST / max_exact)
    large = jnp.minimum(max_exact + (log_ratio * (REL_BUCKETS - max_exact)).astype(jnp.int32), REL_BUCKETS - 1)
    return jnp.where(dist < max_exact, dist, large)


def _bias_tables(rel_bias, seq):
    assert QB >= REL_MAX_DIST
    def lookup(dist):
        onehot = jax.nn.one_hot(_rel_bucket(dist), REL_BUCKETS, dtype=F32)
        return jnp.einsum('...k,kh->h...', onehot, rel_bias, precision=lax.Precision.HIGHEST)

    t = jnp.arange(seq)[:, None]
    cend = jnp.arange(QB)[None, :] * CMP_STRIDE + CMP_BLOCK - 1
    n_cmp = (seq - CMP_BLOCK) // CMP_STRIDE + 1
    valid_c = (t >= cend) & (jnp.arange(QB)[None, :] < n_cmp)
    bias_c = jnp.where(valid_c, lookup(t - cend), NEG_INF)
    steps = seq // CMP_TQ
    bias_c = bias_c.reshape(NSA_KV_HEADS, NSA_GQA, steps, CMP_TQ, QB).transpose(0, 2, 1, 3, 4)
    bias_c = bias_c.reshape(NSA_KV_HEADS * steps, NSA_GQA * CMP_TQ, QB)
    r = jnp.arange(QB)[:, None]
    m = jnp.arange(QB)[None, :]
    near = jnp.stack([lookup(r - m), lookup(QB + r - m)])
    near = near - rel_bias[REL_BUCKETS - 1][None, :, None, None]
    near = near.at[0].add(jnp.where(m > r, NEG_INF, 0.0))
    return bias_c, near.reshape(2, NSA_KV_HEADS, NSA_GQA * QB, QB)


def _expand_table(seq):
    j = jnp.arange(QB)[:, None]
    s = jnp.arange(seq)[None, :]
    return (s // SEL_BLOCK == j).astype(BF16)


def _overlap_table(seq):
    n_cmp = (seq - CMP_BLOCK) // CMP_STRIDE + 1
    cs = jnp.arange(QB)[:, None] * CMP_STRIDE
    ce = cs + CMP_BLOCK - 1
    ss = jnp.arange(QB)[None, :] * SEL_BLOCK
    ov = jnp.maximum(jnp.minimum(ce, ss + SEL_BLOCK - 1) - jnp.maximum(cs, ss) + 1, 0).astype(F32) / CMP_BLOCK
    keep = (jnp.arange(QB)[:, None] < n_cmp) & (jnp.arange(QB)[None, :] < seq // SEL_BLOCK)
    return jnp.where(keep, ov, 0.0).T.astype(BF16)


def _in_proj_weights(w):
    scale = HEAD_DIM ** -0.5
    o_q, o_kv, o_gate = 0, NSA_WIDTH, NSA_WIDTH + 768
    o_sb = o_gate + 3 * NSA_HEADS
    o_z = o_sb + 3 * SB_WIDTH
    o_xbc = o_z + SSD_INNER
    o_dt = o_xbc + SSD_INNER + 2 * SSD_GROUPS * SSD_STATE
    col = lambda a, n: w[:, a:a + n]
    kv = lambda br, which, h: col(o_kv + br * 256 + which * 128 + h * HEAD_DIM, HEAD_DIM)
    zeros = lambda n: jnp.zeros((w.shape[0], n), w.dtype)
    branch = lambda br: [kv(br, 0, 0), kv(br, 1, 0), kv(br, 0, 1), kv(br, 1, 1)]
    q_heads = []
    for h in range(NSA_HEADS):
        q_heads += [col(o_q + h * HEAD_DIM, HEAD_DIM) * scale, zeros(128 - HEAD_DIM)]
    wb = jnp.concatenate(
        q_heads + branch(1) + branch(2)
        + [col(o_sb, SB_WIDTH) * scale, col(o_sb + SB_WIDTH, 2 * SB_WIDTH)], axis=1)
    gates = 3 * NSA_GQA
    wf = jnp.concatenate(
        [col(o_z, SSD_INNER), col(o_xbc, SSD_INNER + 2 * SSD_GROUPS * SSD_STATE), col(o_kv, 256),
         col(o_gate, gates), zeros(MISC_DT - gates), col(o_dt, SSD_HEADS), zeros(128 - MISC_DT - SSD_HEADS),
         col(o_gate + gates, gates), zeros(128 - gates)], axis=1)
    assert wb.shape[1] == PB_COLS and wf.shape[1] == PF_COLS
    return wb.astype(BF16), wf.astype(BF16)


def _head_params(dt_bias, a_log):
    rows = jnp.stack([dt_bias, a_log]).astype(F32)
    return jnp.zeros((8, 128), F32).at[0:2, MISC_DT:MISC_DT + SSD_HEADS].set(rows)


def _mixer(x, l, tables, batch, seq, mix_norm, w_in, w_out, cmp_pos, cmp_w, nsa_norm, sb_norm,
           conv_w, conv_b, dt_bias, a_log, d_skip, ssd_norm):
    bias_c, near, overlap, expand = tables
    wb, wf = _in_proj_weights(w_in[l])
    pb = _rms_matmul(x, mix_norm[l], wb, BF16, name="in_proj_bf16")
    pf = _rms_matmul(x, mix_norm[l], wf, F32, name="in_proj_f32")

    ng = seq // CMP_STRIDE
    ucmp = pf[:, PF_CMP:PF_CMP + 256].reshape(batch, ng, CMP_STRIDE, 4, HEAD_DIM)
    ucmp = ucmp.transpose(0, 3, 1, 2, 4).reshape(batch, 4, ng, CMP_STRIDE * HEAD_DIM)
    pos2 = cmp_pos[l].reshape(2, 2, CMP_STRIDE * HEAD_DIM)
    o_cmp, sel = _nsa_cmp(pb, ucmp, pos2, cmp_w[l].astype(BF16), bias_c, overlap, batch, seq)
    o_nsa = _nsa_sel_win(pb, pf, o_cmp, sel, expand, near, batch, seq)
    o_sb = _sb_attention(pb, batch, seq)
    o_ssd = _ssd(pf, conv_w[l], conv_b[l], _head_params(dt_bias[l], a_log[l]), d_skip[l], ssd_norm[l],
                 batch, seq)
    return _out_proj(x, o_nsa, o_sb, o_ssd, nsa_norm[l], sb_norm[l], w_out[l].astype(BF16))


def kernel(x, rel_bias, ffn1_norm, ffn1_w_gate, ffn1_w_up, ffn1_w_down, mix_norm, w_in, w_out, nsa_cmp_pos, nsa_cmp_w, nsa_out_norm, sb_out_norm, ssd_conv_w, ssd_conv_b, ssd_dt_bias, ssd_a_log, ssd_d, ssd_out_norm, ffn2_norm, ffn2_w_gate, ffn2_w_up, ffn2_w_down, final_norm):
    batch, seq, d = x.shape
    depth = w_in.shape[0]
    tables = _bias_tables(rel_bias, seq) + (_overlap_table(seq), _expand_table(seq))
    h = x.reshape(batch * seq, d)
    ffn1_w_gate, ffn1_w_up, ffn1_w_down, ffn2_w_gate, ffn2_w_up, ffn2_w_down = (
        w.astype(BF16) for w in (ffn1_w_gate, ffn1_w_up, ffn1_w_down, ffn2_w_gate, ffn2_w_up, ffn2_w_down))
    for l in range(depth):
        h = _ffn(h, ffn1_norm[l], ffn1_w_gate, ffn1_w_up, ffn1_w_down, l, final_norm, final=False)
        h = _mixer(h, l, tables, batch, seq, mix_norm, w_in, w_out, nsa_cmp_pos, nsa_cmp_w, nsa_out_norm,
                   sb_out_norm, ssd_conv_w, ssd_conv_b, ssd_dt_bias, ssd_a_log, ssd_d, ssd_out_norm)
        h = _ffn(h, ffn2_norm[l], ffn2_w_gate, ffn2_w_up, ffn2_w_down, l, final_norm,
                 final=(l == depth - 1))
    return h.reshape(batch, seq, d)
```

```python
import functools
import math

import jax
import jax.numpy as jnp
from jax import lax
from jax.experimental import pallas as pl
from jax.experimental.pallas import tpu as pltpu

D_MODEL = 2048
D_FF = 5632
HEAD_DIM = 64
QB = 128
NEG_INF = -1e30
EPS = 1e-6

NSA_HEADS = 8
NSA_KV_HEADS = 2
NSA_GQA = NSA_HEADS // NSA_KV_HEADS
NSA_WIDTH = NSA_HEADS * HEAD_DIM
CMP_BLOCK = 32
CMP_STRIDE = 16
SEL_BLOCK = 64
SEL_TOPK = 8
FORCE_SCORE = 1e3
WINDOW = 512
SB_HEADS = 8
SB_WIDTH = SB_HEADS * HEAD_DIM
SSD_HEADS = 16
SSD_INNER = SSD_HEADS * HEAD_DIM
SSD_GROUPS = 2
SSD_HPG = SSD_HEADS // SSD_GROUPS
SSD_STATE = 128
SSD_CONV = 4
REL_BUCKETS = 32
REL_MAX_DIST = 128

PB_NSA_Q, PB_SEL, PB_WIN, PB_SB_Q, PB_SB_K, PB_SB_V, PB_COLS = 0, 1024, 1280, 1536, 2048, 2560, 3072
PF_Z, PF_XS, PF_BC, PF_CMP, PF_MISC, PF_COLS = 0, 1024, 2048, 2560, 2816, 3072
MISC_DT = 16

VMEM_LIMIT = 52 * 1024 * 1024

BF16 = jnp.bfloat16
F32 = jnp.float32


def _dot(a, b):
    return jnp.dot(a, b, preferred_element_type=F32)


def _dot_nt(a, b):
    return lax.dot_general(a, b, (((1,), (1,)), ((), ())), preferred_element_type=F32)


def _split2_dot(x, m):
    hi = x.astype(BF16)
    lo = (x - hi.astype(F32)).astype(BF16)
    return _dot(hi, m) + _dot(lo, m)


def _split3_dot(x, m):
    hi = x.astype(BF16)
    r1 = x - hi.astype(F32)
    mid = r1.astype(BF16)
    lo = (r1 - mid.astype(F32)).astype(BF16)
    return _dot(hi, m) + _dot(mid, m) + _dot(lo, m)


def _split3_dot_left(m, x):
    hi = x.astype(BF16)
    r1 = x - hi.astype(F32)
    mid = r1.astype(BF16)
    lo = (r1 - mid.astype(F32)).astype(BF16)
    return _dot(m, hi) + _dot(m, mid) + _dot(m, lo)


def _rms(x, g):
    return x * lax.rsqrt(jnp.mean(x * x, axis=-1, keepdims=True) + EPS) * g


def _silu(x):
    return x / (1.0 + jnp.exp(-x))


def _sigmoid(x):
    return 1.0 / (1.0 + jnp.exp(-x))


def _params(sem):
    return pltpu.CompilerParams(dimension_semantics=sem, vmem_limit_bytes=VMEM_LIMIT)


def _ffn_kernel(x_ref, g_ref, wg_ref, wu_ref, wd_ref, fg_ref, o_ref, h_ref, acc_ref, *, final):
    f = pl.program_id(1)

    @pl.when(f == 0)
    def _():
        h_ref[...] = _rms(x_ref[...], g_ref[...]).astype(BF16)
        acc_ref[...] = jnp.zeros_like(acc_ref)

    h = h_ref[...]
    gate = _dot(h, wg_ref[...])
    up = _dot(h, wu_ref[...])
    acc_ref[...] += _dot((_silu(gate) * up).astype(BF16), wd_ref[...])

    @pl.when(f == pl.num_programs(1) - 1)
    def _():
        y = x_ref[...] + 0.5 * acc_ref[...]
        if final:
            y = _rms(y, fg_ref[...])
        o_ref[...] = y


def _ffn(x, g, wg, wu, wd, layer, final_g, *, final, tm=512, tf=512):
    n, d = x.shape
    dff = wg.shape[2]
    return pl.pallas_call(
        functools.partial(_ffn_kernel, final=final),
        grid=(n // tm, dff // tf),
        in_specs=[
            pl.BlockSpec((tm, d), lambda i, f: (i, 0)),
            pl.BlockSpec((1, d), lambda i, f: (0, 0)),
            pl.BlockSpec((None, d, tf), lambda i, f: (layer, 0, f)),
            pl.BlockSpec((None, d, tf), lambda i, f: (layer, 0, f)),
            pl.BlockSpec((None, tf, d), lambda i, f: (layer, f, 0)),
            pl.BlockSpec((1, d), lambda i, f: (0, 0)),
        ],
        out_specs=pl.BlockSpec((tm, d), lambda i, f: (i, 0)),
        out_shape=jax.ShapeDtypeStruct((n, d), F32),
        scratch_shapes=[pltpu.VMEM((tm, d), BF16), pltpu.VMEM((tm, d), F32)],
        compiler_params=_params(("parallel", "arbitrary")),
        name="ffn",
    )(x, g.reshape(1, d), wg, wu, wd, final_g.reshape(1, d))


def _rms_matmul_kernel(x_ref, g_ref, w_ref, o_ref, h_ref):
    @pl.when(pl.program_id(1) == 0)
    def _():
        h_ref[...] = _rms(x_ref[...], g_ref[...]).astype(BF16)

    o_ref[...] = _dot(h_ref[...], w_ref[...]).astype(o_ref.dtype)


def _rms_matmul(x, g, w, out_dtype, *, tm=1024, tn=1024, name):
    n, d = x.shape
    c = w.shape[1]
    return pl.pallas_call(
        _rms_matmul_kernel,
        grid=(n // tm, c // tn),
        in_specs=[
            pl.BlockSpec((tm, d), lambda i, j: (i, 0)),
            pl.BlockSpec((1, d), lambda i, j: (0, 0)),
            pl.BlockSpec((d, tn), lambda i, j: (0, j)),
        ],
        out_specs=pl.BlockSpec((tm, tn), lambda i, j: (i, j)),
        out_shape=jax.ShapeDtypeStruct((n, c), out_dtype),
        scratch_shapes=[pltpu.VMEM((tm, d), BF16)],
        compiler_params=_params(("parallel", "arbitrary")),
        name=name,
    )(x, g.reshape(1, d), w)


SB_GROUP = 8
SB_DEAD = 104.0


def _sb_kernel(q_ref, k_ref, v_ref, o_ref, kbd_ref, vbd_ref):
    i = pl.program_id(2)
    pairs = SB_GROUP // 2
    nkb = k_ref.shape[0] // QB
    r = lax.broadcasted_iota(jnp.int32, (QB, QB), 0)
    c = lax.broadcasted_iota(jnp.int32, (QB, QB), 1)
    strict_all = jnp.concatenate([c < r] * SB_GROUP, axis=0)
    later_ones = jnp.concatenate([(r > c).astype(BF16), jnp.ones((QB, QB), BF16)], axis=1)
    later_ones = jnp.concatenate([later_ones, later_ones], axis=0)

    @pl.when(i == 0)
    def _():
        low = c < HEAD_DIM
        zero16 = jnp.zeros((QB, QB), BF16)

        def build(j, carry):
            src = pl.ds(pl.multiple_of(j * QB, QB), QB)
            dst = pl.ds(pl.multiple_of(j * 2 * QB, 2 * QB), 2 * QB)
            for pp in range(pairs):
                for ref, out in ((k_ref, kbd_ref), (v_ref, vbd_ref)):
                    x = ref[src, pp * 128:(pp + 1) * 128]
                    out[pp, dst, :] = jnp.concatenate([jnp.where(low, x, zero16), jnp.where(low, zero16, x)],
                                                      axis=0)
            return carry

        lax.fori_loop(0, nkb, build, 0)

    def scores(j, diagonal):
        rows = pl.ds(pl.multiple_of(j * 2 * QB, 2 * QB), 2 * QB)
        zs = []
        for pp in range(pairs):
            z_pair = _dot_nt(q_ref[:, pp * 128:(pp + 1) * 128], kbd_ref[pp, rows, :])
            zs += [z_pair[:, 0:QB], z_pair[:, QB:2 * QB]]
        z = jnp.concatenate(zs, axis=0)
        soft = jnp.maximum(z, 0.0) + jnp.log(1.0 + jnp.exp(-jnp.abs(z)))
        if diagonal:
            soft = jnp.where(strict_all, soft, 0.0)
        hi = soft.astype(BF16)
        lo = (soft - hi.astype(F32)).astype(BF16)
        sums = _dot(jnp.concatenate([hi, lo], axis=1), later_ones)
        return rows, z - soft, sums

    def absorb(tails, accs, rows, log_beta, sums, diagonal):
        a = jnp.exp(log_beta - (tails + sums[:, 0:QB]))
        if diagonal:
            a = jnp.where(strict_all, a, 0.0)
        a = a.astype(BF16)
        outs = []
        for pp in range(pairs):
            pair = jnp.concatenate([a[2 * pp * QB:(2 * pp + 1) * QB], a[(2 * pp + 1) * QB:(2 * pp + 2) * QB]],
                                   axis=1)
            outs.append(_dot(pair, vbd_ref[pp, rows, :]))
        return tails + sums[:, QB:2 * QB], accs + jnp.concatenate(outs, axis=0)

    def sweep(js, carry):
        staged = [scores(j, False) for j in js]
        for st in staged:
            carry = absorb(*carry, *st, False)
        return carry

    carry = (jnp.zeros((SB_GROUP * QB, QB), F32), jnp.zeros((pairs * QB, 128), F32))
    carry = absorb(*carry, *scores(i, True), True)
    odd = i % 2
    carry = lax.fori_loop(0, odd, lambda step, cr: sweep([i - 1], cr), carry)
    top = i - 1 - odd

    def live(state):
        step, smallest, _, _ = state
        return (step < i // 2) & (smallest < SB_DEAD)

    def pair_step(state):
        step, _, tails, accs = state
        tails, accs = sweep([top - 2 * step, top - 2 * step - 1], (tails, accs))
        return step + 1, jnp.min(tails), tails, accs

    accs = lax.while_loop(live, pair_step, (0, jnp.min(carry[0]), *carry))[3]
    for pp in range(pairs):
        o_ref[:, pp * 128:(pp + 1) * 128] = accs[pp * QB:(pp + 1) * QB]


def _sb_attention(pb, batch, seq):
    nqb = seq // QB
    w = SB_GROUP * HEAD_DIM
    qc, kc, vc = PB_SB_Q // w, PB_SB_K // w, PB_SB_V // w
    return pl.pallas_call(
        _sb_kernel,
        grid=(batch, SB_HEADS // SB_GROUP, nqb),
        in_specs=[
            pl.BlockSpec((QB, w), lambda b, p, i: (b * nqb + i, qc + p)),
            pl.BlockSpec((seq, w), lambda b, p, i: (b, kc + p)),
            pl.BlockSpec((seq, w), lambda b, p, i: (b, vc + p)),
        ],
        out_specs=pl.BlockSpec((QB, w), lambda b, p, i: (b * nqb + i, p)),
        out_shape=jax.ShapeDtypeStruct((batch * seq, SB_WIDTH), F32),
        scratch_shapes=[pltpu.VMEM((SB_GROUP // 2, 2 * seq, 128), BF16),
                        pltpu.VMEM((SB_GROUP // 2, 2 * seq, 128), BF16)],
        compiler_params=_params(("parallel", "parallel", "arbitrary")),
        name="sb_attention",
    )(pb, pb, pb)


CMP_TQ = 4 * QB


def _nsa_cmp_kernel(q_ref, uk_ref, uv_ref, pos_ref, w_ref, bias_ref, ovt_ref, ocmp_ref, sel_ref,
                    kc_ref, vc_ref, *, n_cmp, n_sel):
    i = pl.program_id(2)
    half = CMP_STRIDE * HEAD_DIM

    @pl.when(i == 0)
    def _():
        def compress(kv, u_ref):
            u = u_ref[0, 0]
            top = _dot((u + pos_ref[kv, 0:1, :]).astype(BF16), w_ref[kv, 0:half, :])
            bot = _dot((u + pos_ref[kv, 1:2, :]).astype(BF16), w_ref[kv, half:2 * half, :])
            return top + pltpu.roll(bot, QB - 1, 0)

        zeros = jnp.zeros((QB, HEAD_DIM), F32)
        kc_ref[...] = jnp.concatenate([compress(0, uk_ref), zeros], axis=1).astype(BF16)
        v = compress(1, uv_ref)
        vc_ref[...] = jnp.concatenate([jnp.concatenate([v, zeros], axis=1),
                                       jnp.concatenate([zeros, v], axis=1)], axis=0).astype(BF16)

    G = NSA_GQA
    TQ = CMP_TQ
    q4 = jnp.concatenate([q_ref[:, g * 128:(g + 1) * 128] for g in range(G)], axis=0)
    s = _dot_nt(q4, kc_ref[...]) + bias_ref[0]
    e = jnp.exp(s - jnp.max(s, axis=-1, keepdims=True))
    t_rows = i * TQ + lax.broadcasted_iota(jnp.int32, (TQ, 1), 0)
    any_valid = jnp.concatenate([t_rows >= CMP_BLOCK - 1] * G, axis=0)
    p = jnp.where(any_valid, e / jnp.sum(e, axis=-1, keepdims=True), 0.0)
    p16 = p.astype(BF16)
    for pp in range(G // 2):
        pair = jnp.concatenate([p16[2 * pp * TQ:(2 * pp + 1) * TQ], p16[(2 * pp + 1) * TQ:(2 * pp + 2) * TQ]],
                               axis=1)
        ocmp_ref[:, pp * 128:(pp + 1) * 128] = _dot(pair, vc_ref[...])
    p_all = p[0:TQ]
    for g in range(1, G):
        p_all = p_all + p[g * TQ:(g + 1) * TQ]

    blk = lax.broadcasted_iota(jnp.int32, (n_sel, QB), 0)
    for part in range(TQ // QB):
        rs = slice(part * QB, (part + 1) * QB)
        t0 = (i * (TQ // QB) + part) * QB
        p_sum = p_all[rs]
        hi = p_sum.astype(BF16)
        lo = (p_sum - hi.astype(F32)).astype(BF16)
        p_sel = (_dot_nt(ovt_ref[...], hi) + _dot_nt(ovt_ref[...], lo))[0:n_sel]
        t = t0 + lax.broadcasted_iota(jnp.int32, (n_sel, QB), 1)
        cur = t // SEL_BLOCK
        eligible = blk * SEL_BLOCK <= t
        forced = (blk == 0) | (blk == cur) | (blk == cur - 1)
        score = jnp.where(eligible, p_sel + jnp.where(forced, FORCE_SCORE, 0.0), NEG_INF)
        rank = jnp.zeros((n_sel, QB), F32)
        for j in range(n_sel):
            other = score[j:j + 1, :]
            ahead = (other > score) | ((other == score) & (blk > j))
            rank = rank + jnp.where(ahead, 1.0, 0.0)
        chosen = jnp.where(eligible & (rank < SEL_TOPK), 1.0, 0.0)
        chosen = jnp.concatenate([chosen, jnp.zeros((QB - n_sel, QB), F32)], axis=0)
        sel_ref[0, 0, rs, :] = chosen.T.astype(BF16)


def _nsa_cmp(pb, ucmp, pos2, cmp_w, bias_c, overlap_t, batch, seq):
    nqb = seq // CMP_TQ
    n_cmp = (seq - CMP_BLOCK) // CMP_STRIDE + 1
    n_sel = seq // SEL_BLOCK
    ng = seq // CMP_STRIDE
    assert ng == QB and n_sel <= QB and n_sel % 8 == 0
    wide = CMP_STRIDE * HEAD_DIM
    return pl.pallas_call(
        functools.partial(_nsa_cmp_kernel, n_cmp=n_cmp, n_sel=n_sel),
        grid=(batch, NSA_KV_HEADS, nqb),
        in_specs=[
            pl.BlockSpec((CMP_TQ, NSA_GQA * 128), lambda b, h, i: (b * nqb + i, h)),
            pl.BlockSpec((1, 1, ng, wide), lambda b, h, i: (b, h, 0, 0)),
            pl.BlockSpec((1, 1, ng, wide), lambda b, h, i: (b, NSA_KV_HEADS + h, 0, 0)),
            pl.BlockSpec((2, 2, wide), lambda b, h, i: (0, 0, 0)),
            pl.BlockSpec((2, 2 * wide, HEAD_DIM), lambda b, h, i: (0, 0, 0)),
            pl.BlockSpec((1, NSA_GQA * CMP_TQ, QB), lambda b, h, i: (h * nqb + i, 0, 0)),
            pl.BlockSpec((QB, QB), lambda b, h, i: (0, 0)),
        ],
        out_specs=[
            pl.BlockSpec((CMP_TQ, 256), lambda b, h, i: (b * nqb + i, h)),
            pl.BlockSpec((1, 1, CMP_TQ, QB), lambda b, h, i: (b, h, i, 0)),
        ],
        out_shape=[
            jax.ShapeDtypeStruct((batch * seq, NSA_WIDTH), F32),
            jax.ShapeDtypeStruct((batch, NSA_KV_HEADS, seq, QB), BF16),
        ],
        scratch_shapes=[pltpu.VMEM((QB, 128), BF16), pltpu.VMEM((2 * QB, 128), BF16)],
        compiler_params=_params(("parallel", "parallel", "arbitrary")),
        name="nsa_compressed",
    )(pb, ucmp, ucmp, pos2, cmp_w, bias_c, overlap_t)


SW_TQ = 2 * QB
SEL_CHUNK = 4 * QB
WIN_FAR = WINDOW - QB


def _nsa_sw_kernel(q_ref, ks_ref, kw_ref, ocmp_ref, misc_ref, sel_ref, exp_ref, tb_ref, o_ref, madd_ref):
    i = pl.program_id(2)
    G = NSA_GQA
    TQ = SW_TQ
    q4 = jnp.concatenate([q_ref[:, g * 128:(g + 1) * 128] for g in range(G)], axis=0)
    madd_ref[...] = (_dot(sel_ref[0, 0], exp_ref[...]) - 1.0) * (-NEG_INF)
    t_pos = i * TQ + lax.broadcasted_iota(jnp.int32, (TQ, 1), 0)

    low = lax.broadcasted_iota(jnp.int32, (QB, 128), 1) < HEAD_DIM

    def tile(x, n=G):
        return jnp.concatenate([x] * n, axis=0)

    def ones_v(kv):
        return jnp.where(tile(low, kv.shape[0] // QB), jnp.ones_like(kv), kv)

    first = 2 * i - 1
    rows_m = pl.ds(pl.multiple_of(jnp.maximum(first, 0) * QB, QB), QB)
    rows_a = pl.ds(pl.multiple_of(2 * i * QB, QB), QB)
    rows_b = pl.ds(pl.multiple_of((2 * i + 1) * QB, QB), QB)
    gone = jnp.where(i >= 1, 0.0, NEG_INF)
    gone_m = jnp.concatenate([jnp.full((TQ, QB), gone, F32), jnp.zeros((TQ, 2 * QB), F32)], axis=1)
    far_end = first * QB

    kv = jnp.concatenate([ks_ref[rows_m, :], ks_ref[rows_a, :], ks_ref[rows_b, :]], axis=0)
    near_sel = jnp.concatenate([madd_ref[:, rows_m], madd_ref[:, rows_a], madd_ref[:, rows_b]], axis=1)
    s = _dot_nt(q4, kv) + (tb_ref[0] + tile(near_sel + gone_m))
    m = jnp.max(s, axis=-1, keepdims=True)
    acc = _dot(jnp.exp(s - m).astype(BF16), ones_v(kv))

    def sel_far(cidx, state):
        m, acc = state
        cols = pl.ds(pl.multiple_of(cidx * SEL_CHUNK, SEL_CHUNK), SEL_CHUNK)
        k_pos = cidx * SEL_CHUNK + lax.broadcasted_iota(jnp.int32, (TQ, SEL_CHUNK), 1)
        add = jnp.where(k_pos < far_end, madd_ref[:, cols], NEG_INF)
        kv = ks_ref[cols, :]
        s = _dot_nt(q4, kv) + tile(add)
        m_new = jnp.maximum(m, jnp.max(s, axis=-1, keepdims=True))
        return m_new, jnp.exp(m - m_new) * acc + _dot(jnp.exp(s - m_new).astype(BF16), ones_v(kv))

    _, sel_acc = lax.fori_loop(0, (2 * i + 2) // (SEL_CHUNK // QB), sel_far, (m, acc))

    start = jnp.maximum(first - WIN_FAR // QB, 0) * QB
    rows_far = pl.ds(pl.multiple_of(start, QB), WIN_FAR)
    k_pos = start + lax.broadcasted_iota(jnp.int32, (TQ, WIN_FAR), 1)
    live = (k_pos < far_end) & (k_pos > t_pos - WINDOW)
    kv = jnp.concatenate([kw_ref[rows_far, :], kw_ref[rows_m, :], kw_ref[rows_a, :], kw_ref[rows_b, :]], axis=0)
    bias = jnp.concatenate([tile(jnp.where(live, 0.0, NEG_INF)), tb_ref[0] + tile(gone_m)], axis=1)
    s = _dot_nt(q4, kv) + bias
    win_acc = _dot(jnp.exp(s - jnp.max(s, axis=-1, keepdims=True)).astype(BF16), ones_v(kv))

    gate = _sigmoid(misc_ref[...])
    low = tile(low, TQ // QB)

    def pair(acc, pp):
        a, b = acc[2 * pp * TQ:(2 * pp + 1) * TQ], acc[(2 * pp + 1) * TQ:(2 * pp + 2) * TQ]
        return (jnp.where(low, pltpu.roll(a, HEAD_DIM, 1), b)
                / jnp.where(low, a, pltpu.roll(b, HEAD_DIM, 1)))

    def pair_gate(branch, pp):
        ca, cb = 3 * (2 * pp) + branch, 3 * (2 * pp + 1) + branch
        return jnp.where(low, gate[:, ca:ca + 1], gate[:, cb:cb + 1])

    for pp in range(G // 2):
        lanes = slice(pp * 128, (pp + 1) * 128)
        o_ref[:, lanes] = (pair_gate(0, pp) * ocmp_ref[:, lanes] + pair_gate(1, pp) * pair(sel_acc, pp)
                           + pair_gate(2, pp) * pair(win_acc, pp))


def _nsa_sel_win(pb, pf, o_cmp, sel, expand, tb, batch, seq):
    nqb = seq // SW_TQ
    G = NSA_GQA
    return pl.pallas_call(
        _nsa_sw_kernel,
        grid=(batch, NSA_KV_HEADS, nqb),
        in_specs=[
            pl.BlockSpec((SW_TQ, G * 128), lambda b, h, i: (b * nqb + i, h)),
            pl.BlockSpec((seq, 128), lambda b, h, i: (b, PB_SEL // 128 + h)),
            pl.BlockSpec((seq, 128), lambda b, h, i: (b, PB_WIN // 128 + h)),
            pl.BlockSpec((SW_TQ, 256), lambda b, h, i: (b * nqb + i, h)),
            pl.BlockSpec((SW_TQ, 128), lambda b, h, i: (b * nqb + i, PF_MISC // 128 + h)),
            pl.BlockSpec((1, 1, SW_TQ, QB), lambda b, h, i: (b, h, i, 0)),
            pl.BlockSpec((QB, seq), lambda b, h, i: (0, 0)),
            pl.BlockSpec((1, G * SW_TQ, 3 * QB), lambda b, h, i: (h, 0, 0)),
        ],
        out_specs=pl.BlockSpec((SW_TQ, 256), lambda b, h, i: (b * nqb + i, h)),
        out_shape=jax.ShapeDtypeStruct((batch * seq, NSA_WIDTH), F32),
        scratch_shapes=[pltpu.VMEM((SW_TQ, seq), F32)],
        compiler_params=_params(("parallel", "parallel", "arbitrary")),
        name="nsa_selected_window",
    )(pb, pb, pb, o_cmp, pf, sel, expand, tb)


def _ssd_kernel(z_ref, xs_ref, bc_ref, misc_ref, cwx_ref, cwb_ref, cbx_ref, cbb_ref, hp_ref, spread_ref,
                dskip_ref, ng_ref, o_ref, xbuf, bbuf, state):
    ci = pl.program_id(1)
    L = QB
    P = HEAD_DIM
    GN = SSD_GROUPS * SSD_STATE

    @pl.when(ci == 0)
    def _():
        xbuf[...] = jnp.zeros_like(xbuf)
        bbuf[...] = jnp.zeros_like(bbuf)
        state[...] = jnp.zeros_like(state)

    def conv_silu(buf, src_ref, w_ref, b_ref):
        cur = src_ref[...]
        ext = jnp.concatenate([buf[...], cur], axis=0)
        out = b_ref[...] + w_ref[SSD_CONV - 1:SSD_CONV, :] * cur
        for k in range(1, SSD_CONV):
            out = out + w_ref[SSD_CONV - 1 - k:SSD_CONV - k, :] * pltpu.roll(ext, k, 0)[8:8 + L]
        buf[...] = cur[L - 8:L]
        return _silu(out)

    xs = conv_silu(xbuf, xs_ref, cwx_ref, cbx_ref)
    bcs = conv_silu(bbuf, bc_ref, cwb_ref, cbb_ref)

    misc = misc_ref[...]
    pre = misc + hp_ref[0:1, :]
    dt = jnp.maximum(pre, 0.0) + jnp.log1p(jnp.exp(-jnp.abs(pre)))
    a_dt = dt * (-jnp.exp(hp_ref[1:2, :]))
    r = lax.broadcasted_iota(jnp.int32, (L, L), 0)
    c = lax.broadcasted_iota(jnp.int32, (L, L), 1)
    causal = r >= c
    a_cs = _split3_dot_left(causal.astype(BF16), a_dt)
    a_cs_t = a_cs.T

    dt_full = _split3_dot(dt, spread_ref[...])
    acs_full = _split3_dot(a_cs, spread_ref[...])
    total_full = acs_full[L - 1:L, :]
    xdt = xs * dt_full
    xdt16 = xdt.astype(BF16)
    to_end16 = (xdt * jnp.exp(total_full - acs_full)).astype(BF16)
    decay_in = jnp.exp(acs_full)
    chunk_decay = jnp.exp(total_full)
    low = c < P
    zero16 = jnp.zeros((L, 2 * P), BF16)
    width = SSD_HPG * P

    ys = []
    for g in range(SSD_GROUPS):
        bm = bcs[:, g * SSD_STATE:(g + 1) * SSD_STATE]
        cm16 = bcs[:, GN + g * SSD_STATE:GN + (g + 1) * SSD_STATE].astype(BF16)
        cb = _dot_nt(cm16, bm.astype(BF16))
        lanes = slice(g * width, (g + 1) * width)
        h_in = state[g]
        y_off = _dot(cm16, h_in.astype(BF16)) * decay_in[:, lanes]
        state[g] = h_in * chunk_decay[:, lanes] + _dot(bm.T.astype(BF16), to_end16[:, lanes])
        diag = []
        for pp in range(SSD_HPG // 2):
            hd = g * SSD_HPG + 2 * pp
            decayed = []
            for col in (MISC_DT + hd, MISC_DT + hd + 1):
                seg = jnp.exp(jnp.where(causal, a_cs[:, col:col + 1] - a_cs_t[col:col + 1, :], -jnp.inf))
                decayed.append((cb * seg).astype(BF16))
            xp = xdt16[:, hd * P:(hd + 2) * P]
            x_bd = jnp.concatenate([jnp.where(low, xp, zero16), jnp.where(low, zero16, xp)], axis=0)
            diag.append(_dot(jnp.concatenate(decayed, axis=1), x_bd))
        ys.append(jnp.concatenate(diag, axis=1) + y_off)

    y = (jnp.concatenate(ys, axis=1) + dskip_ref[...] * xs) * _silu(z_ref[...])
    for g in range(SSD_GROUPS):
        lanes = slice(g * width, (g + 1) * width)
        yg = y[:, lanes]
        o_ref[:, lanes] = yg * lax.rsqrt(jnp.mean(yg * yg, axis=-1, keepdims=True) + EPS) * ng_ref[:, lanes]


def _ssd(pf, conv_w, conv_b, head_params, d_skip, norm_g, batch, seq):
    nc = seq // QB
    GN = SSD_GROUPS * SSD_STATE
    lane_head = jnp.arange(SSD_INNER)[None, :] // HEAD_DIM
    spread = (jnp.arange(128)[:, None] == MISC_DT + lane_head).astype(BF16)
    cwx, cwb = conv_w[:, :SSD_INNER], conv_w[:, SSD_INNER:]
    cbx, cbb = conv_b[:SSD_INNER].reshape(1, -1), conv_b[SSD_INNER:].reshape(1, -1)
    full = lambda shape: pl.BlockSpec(shape, lambda b, ci: (0,) * len(shape))
    return pl.pallas_call(
        _ssd_kernel,
        grid=(batch, nc),
        in_specs=[
            pl.BlockSpec((QB, SSD_INNER), lambda b, ci: (b * nc + ci, PF_Z // SSD_INNER)),
            pl.BlockSpec((QB, SSD_INNER), lambda b, ci: (b * nc + ci, PF_XS // SSD_INNER)),
            pl.BlockSpec((QB, 2 * GN), lambda b, ci: (b * nc + ci, PF_BC // (2 * GN))),
            pl.BlockSpec((QB, 128), lambda b, ci: (b * nc + ci, PF_MISC // 128)),
            full((SSD_CONV, SSD_INNER)),
            full((SSD_CONV, 2 * GN)),
            full((1, SSD_INNER)),
            full((1, 2 * GN)),
            full((8, 128)),
            full((128, SSD_INNER)),
            full((1, SSD_INNER)),
            full((1, SSD_INNER)),
        ],
        out_specs=pl.BlockSpec((QB, SSD_INNER), lambda b, ci: (b * nc + ci, 0)),
        out_shape=jax.ShapeDtypeStruct((batch * seq, SSD_INNER), F32),
        scratch_shapes=[
            pltpu.VMEM((8, SSD_INNER), F32),
            pltpu.VMEM((8, 2 * GN), F32),
            pltpu.VMEM((SSD_GROUPS, SSD_STATE, SSD_HPG * HEAD_DIM), F32),
        ],
        compiler_params=_params(("parallel", "arbitrary")),
        name="ssd",
    )(pf, pf, pf, pf, cwx, cwb, cbx, cbb, head_params, spread,
      jnp.repeat(d_skip.astype(F32), HEAD_DIM).reshape(1, -1), norm_g.reshape(1, -1))


def _out_proj_kernel(x_ref, nsa_ref, sb_ref, ssd_ref, gn_ref, gs_ref, w_ref, o_ref, mix_ref):
    @pl.when(pl.program_id(1) == 0)
    def _():
        mix_ref[:, 0:NSA_WIDTH] = _rms(nsa_ref[...], gn_ref[...]).astype(BF16)
        mix_ref[:, NSA_WIDTH:NSA_WIDTH + SB_WIDTH] = _rms(sb_ref[...], gs_ref[...]).astype(BF16)
        mix_ref[:, NSA_WIDTH + SB_WIDTH:] = ssd_ref[...].astype(BF16)

    o_ref[...] = x_ref[...] + _dot(mix_ref[...], w_ref[...])


def _out_proj(x, o_nsa, o_sb, o_ssd, g_nsa, g_sb, w, *, tm=1024, tn=1024):
    n, d = x.shape
    dm = w.shape[0]
    return pl.pallas_call(
        _out_proj_kernel,
        grid=(n // tm, d // tn),
        in_specs=[
            pl.BlockSpec((tm, tn), lambda i, j: (i, j)),
            pl.BlockSpec((tm, NSA_WIDTH), lambda i, j: (i, 0)),
            pl.BlockSpec((tm, SB_WIDTH), lambda i, j: (i, 0)),
            pl.BlockSpec((tm, SSD_INNER), lambda i, j: (i, 0)),
            pl.BlockSpec((1, NSA_WIDTH), lambda i, j: (0, 0)),
            pl.BlockSpec((1, SB_WIDTH), lambda i, j: (0, 0)),
            pl.BlockSpec((dm, tn), lambda i, j: (0, j)),
        ],
        out_specs=pl.BlockSpec((tm, tn), lambda i, j: (i, j)),
        out_shape=jax.ShapeDtypeStruct((n, d), F32),
        scratch_shapes=[pltpu.VMEM((tm, dm), BF16)],
        compiler_params=_params(("parallel", "arbitrary")),
        name="out_proj",
    )(x, o_nsa, o_sb, o_ssd, g_nsa.reshape(1, -1), g_sb.reshape(1, -1), w)


def _rel_bucket(dist):
    dist = jnp.maximum(dist, 0)
    max_exact = REL_BUCKETS // 2
    log_ratio = jnp.log(jnp.maximum(dist, 1).astype(F32) / max_exact) / math.log(REL_MAX_DIST / max_exact)
    large = jnp.minimum(max_exact + (log_ratio * (REL_BUCKETS - max_exact)).astype(jnp.int32), REL_BUCKETS - 1)
    return jnp.where(dist < max_exact, dist, large)


def _bias_tables(rel_bias, seq):
    assert QB >= REL_MAX_DIST
    def lookup(dist):
        bucket = _rel_bucket(dist)
        out = jnp.zeros((rel_bias.shape[1],) + dist.shape, F32)
        for k in range(REL_BUCKETS):
            out = out + jnp.where(bucket == k, rel_bias[k].reshape((-1,) + (1,) * dist.ndim), 0.0)
        return out

    t = jnp.arange(seq)[:, None]
    cend = jnp.arange(QB)[None, :] * CMP_STRIDE + CMP_BLOCK - 1
    n_cmp = (seq - CMP_BLOCK) // CMP_STRIDE + 1
    valid_c = (t >= cend) & (jnp.arange(QB)[None, :] < n_cmp)
    bias_c = jnp.where(valid_c, lookup(t - cend), NEG_INF)
    steps = seq // CMP_TQ
    bias_c = bias_c.reshape(NSA_KV_HEADS, NSA_GQA, steps, CMP_TQ, QB).transpose(0, 2, 1, 3, 4)
    bias_c = bias_c.reshape(NSA_KV_HEADS * steps, NSA_GQA * CMP_TQ, QB)
    r = jnp.arange(QB)[:, None]
    m = jnp.arange(QB)[None, :]
    near = jnp.stack([lookup(r - m), lookup(QB + r - m)])
    near = near - rel_bias[REL_BUCKETS - 1][None, :, None, None]
    diag = near[0] + jnp.where(m > r, NEG_INF, 0.0)
    prev = near[1]
    rows_a = jnp.concatenate([prev, diag, jnp.full_like(diag, NEG_INF)], axis=2)
    rows_b = jnp.concatenate([jnp.zeros_like(diag), prev, diag], axis=2)
    near2 = jnp.concatenate([rows_a, rows_b], axis=1)
    return bias_c, near2.reshape(NSA_KV_HEADS, NSA_GQA * SW_TQ, 3 * QB)


def _expand_table(seq):
    j = jnp.arange(QB)[:, None]
    s = jnp.arange(seq)[None, :]
    return (s // SEL_BLOCK == j).astype(BF16)


def _overlap_table(seq):
    n_cmp = (seq - CMP_BLOCK) // CMP_STRIDE + 1
    cs = jnp.arange(QB)[:, None] * CMP_STRIDE
    ce = cs + CMP_BLOCK - 1
    ss = jnp.arange(QB)[None, :] * SEL_BLOCK
    ov = jnp.maximum(jnp.minimum(ce, ss + SEL_BLOCK - 1) - jnp.maximum(cs, ss) + 1, 0).astype(F32) / CMP_BLOCK
    keep = (jnp.arange(QB)[:, None] < n_cmp) & (jnp.arange(QB)[None, :] < seq // SEL_BLOCK)
    return jnp.where(keep, ov, 0.0).T.astype(BF16)


def _in_proj_weights(w):
    scale = HEAD_DIM ** -0.5
    o_q, o_kv, o_gate = 0, NSA_WIDTH, NSA_WIDTH + 768
    o_sb = o_gate + 3 * NSA_HEADS
    o_z = o_sb + 3 * SB_WIDTH
    o_xbc = o_z + SSD_INNER
    o_dt = o_xbc + SSD_INNER + 2 * SSD_GROUPS * SSD_STATE
    col = lambda a, n: w[:, a:a + n]
    kv = lambda br, which, h: col(o_kv + br * 256 + which * 128 + h * HEAD_DIM, HEAD_DIM)
    zeros = lambda n: jnp.zeros((w.shape[0], n), w.dtype)
    branch = lambda br: [kv(br, 0, 0), kv(br, 1, 0), kv(br, 0, 1), kv(br, 1, 1)]
    q_heads = []
    for h in range(NSA_HEADS):
        q_heads += [col(o_q + h * HEAD_DIM, HEAD_DIM) * scale, zeros(128 - HEAD_DIM)]
    wb = jnp.concatenate(
        q_heads + branch(1) + branch(2)
        + [col(o_sb, SB_WIDTH) * scale, col(o_sb + SB_WIDTH, 2 * SB_WIDTH)], axis=1)
    gates = 3 * NSA_GQA
    wf = jnp.concatenate(
        [col(o_z, SSD_INNER), col(o_xbc, SSD_INNER + 2 * SSD_GROUPS * SSD_STATE), col(o_kv, 256),
         col(o_gate, gates), zeros(MISC_DT - gates), col(o_dt, SSD_HEADS), zeros(128 - MISC_DT - SSD_HEADS),
         col(o_gate + gates, gates), zeros(128 - gates)], axis=1)
    assert wb.shape[1] == PB_COLS and wf.shape[1] == PF_COLS
    return wb.astype(BF16), wf.astype(BF16)


def _head_params(dt_bias, a_log):
    rows = jnp.stack([dt_bias, a_log]).astype(F32)
    return jnp.zeros((8, 128), F32).at[0:2, MISC_DT:MISC_DT + SSD_HEADS].set(rows)


def _mixer(x, l, tables, batch, seq, mix_norm, w_in, w_out, cmp_pos, cmp_w, nsa_norm, sb_norm,
           conv_w, conv_b, dt_bias, a_log, d_skip, ssd_norm):
    bias_c, near, overlap, expand = tables
    wb, wf = _in_proj_weights(w_in[l])
    pb = _rms_matmul(x, mix_norm[l], wb, BF16, name="in_proj_bf16")
    pf = _rms_matmul(x, mix_norm[l], wf, F32, name="in_proj_f32")

    ng = seq // CMP_STRIDE
    ucmp = pf[:, PF_CMP:PF_CMP + 256].reshape(batch, ng, CMP_STRIDE, 4, HEAD_DIM)
    ucmp = ucmp.transpose(0, 3, 1, 2, 4).reshape(batch, 4, ng, CMP_STRIDE * HEAD_DIM)
    pos2 = cmp_pos[l].reshape(2, 2, CMP_STRIDE * HEAD_DIM)
    o_cmp, sel = _nsa_cmp(pb, ucmp, pos2, cmp_w[l].astype(BF16), bias_c, overlap, batch, seq)
    o_nsa = _nsa_sel_win(pb, pf, o_cmp, sel, expand, near, batch, seq)
    o_sb = _sb_attention(pb, batch, seq)
    o_ssd = _ssd(pf, conv_w[l], conv_b[l], _head_params(dt_bias[l], a_log[l]), d_skip[l], ssd_norm[l],
                 batch, seq)
    return _out_proj(x, o_nsa, o_sb, o_ssd, nsa_norm[l], sb_norm[l], w_out[l].astype(BF16))


def kernel(x, rel_bias, ffn1_norm, ffn1_w_gate, ffn1_w_up, ffn1_w_down, mix_norm, w_in, w_out, nsa_cmp_pos, nsa_cmp_w, nsa_out_norm, sb_out_norm, ssd_conv_w, ssd_conv_b, ssd_dt_bias, ssd_a_log, ssd_d, ssd_out_norm, ffn2_norm, ffn2_w_gate, ffn2_w_up, ffn2_w_down, final_norm):
    batch, seq, d = x.shape
    depth = w_in.shape[0]
    tables = _bias_tables(rel_bias, seq) + (_overlap_table(seq), _expand_table(seq))
    h = x.reshape(batch * seq, d)
    ffn1_w_gate, ffn1_w_up, ffn1_w_down, ffn2_w_gate, ffn2_w_up, ffn2_w_down = (
        w.astype(BF16) for w in (ffn1_w_gate, ffn1_w_up, ffn1_w_down, ffn2_w_gate, ffn2_w_up, ffn2_w_down))
    for l in range(depth):
        h = _ffn(h, ffn1_norm[l], ffn1_w_gate, ffn1_w_up, ffn1_w_down, l, final_norm, final=False)
        h = _mixer(h, l, tables, batch, seq, mix_norm, w_in, w_out, nsa_cmp_pos, nsa_cmp_w, nsa_out_norm,
                   sb_out_norm, ssd_conv_w, ssd_conv_b, ssd_dt_bias, ssd_a_log, ssd_d, ssd_out_norm)
        h = _ffn(h, ffn2_norm[l], ffn2_w_gate, ffn2_w_up, ffn2_w_down, l, final_norm,
                 final=(l == depth - 1))
    return h.reshape(batch, seq, d)
```

```python
import functools
import math

import jax
import jax.numpy as jnp
from jax import lax
from jax.experimental import pallas as pl
from jax.experimental.pallas import tpu as pltpu

D_MODEL = 2048
D_FF = 5632
HEAD_DIM = 64
QB = 128
NEG_INF = -1e30
EPS = 1e-6

NSA_HEADS = 8
NSA_KV_HEADS = 2
NSA_GQA = NSA_HEADS // NSA_KV_HEADS
NSA_WIDTH = NSA_HEADS * HEAD_DIM
CMP_BLOCK = 32
CMP_STRIDE = 16
SEL_BLOCK = 64
SEL_TOPK = 8
FORCE_SCORE = 1e3
WINDOW = 512
SB_HEADS = 8
SB_WIDTH = SB_HEADS * HEAD_DIM
SSD_HEADS = 16
SSD_INNER = SSD_HEADS * HEAD_DIM
SSD_GROUPS = 2
SSD_HPG = SSD_HEADS // SSD_GROUPS
SSD_STATE = 128
SSD_CONV = 4
REL_BUCKETS = 32
REL_MAX_DIST = 128

PB_NSA_Q, PB_SEL, PB_WIN, PB_SB_Q, PB_SB_K, PB_SB_V, PB_COLS = 0, 1024, 1280, 1536, 2048, 2560, 3072
PF_Z, PF_XS, PF_BC, PF_CMP, PF_MISC, PF_COLS = 0, 1024, 2048, 2560, 2816, 3072
MISC_DT = 16

VMEM_LIMIT = 52 * 1024 * 1024

BF16 = jnp.bfloat16
F32 = jnp.float32


def _dot(a, b):
    return jnp.dot(a, b, preferred_element_type=F32)


def _dot_nt(a, b):
    return lax.dot_general(a, b, (((1,), (1,)), ((), ())), preferred_element_type=F32)


def _split2_dot(x, m):
    hi = x.astype(BF16)
    lo = (x - hi.astype(F32)).astype(BF16)
    return _dot(hi, m) + _dot(lo, m)


def _split3_dot(x, m):
    hi = x.astype(BF16)
    r1 = x - hi.astype(F32)
    mid = r1.astype(BF16)
    lo = (r1 - mid.astype(F32)).astype(BF16)
    return _dot(hi, m) + _dot(mid, m) + _dot(lo, m)


def _split3_dot_left(m, x):
    hi = x.astype(BF16)
    r1 = x - hi.astype(F32)
    mid = r1.astype(BF16)
    lo = (r1 - mid.astype(F32)).astype(BF16)
    return _dot(m, hi) + _dot(m, mid) + _dot(m, lo)


def _rms(x, g):
    return x * lax.rsqrt(jnp.mean(x * x, axis=-1, keepdims=True) + EPS) * g


def _silu(x):
    return x / (1.0 + jnp.exp(-x))


def _sigmoid(x):
    return 1.0 / (1.0 + jnp.exp(-x))


def _params(sem):
    return pltpu.CompilerParams(dimension_semantics=sem, vmem_limit_bytes=VMEM_LIMIT)


def _ffn_kernel(x_ref, g_ref, wg_ref, wu_ref, wd_ref, fg_ref, o_ref, h_ref, acc_ref, *, final):
    f = pl.program_id(1)

    @pl.when(f == 0)
    def _():
        h_ref[...] = _rms(x_ref[...], g_ref[...]).astype(BF16)
        acc_ref[...] = jnp.zeros_like(acc_ref)

    h = h_ref[...]
    gate = _dot(h, wg_ref[...])
    up = _dot(h, wu_ref[...])
    acc_ref[...] += _dot((_silu(gate) * up).astype(BF16), wd_ref[...])

    @pl.when(f == pl.num_programs(1) - 1)
    def _():
        y = x_ref[...] + 0.5 * acc_ref[...]
        if final:
            y = _rms(y, fg_ref[...])
        o_ref[...] = y


def _ffn(x, g, wg, wu, wd, layer, final_g, *, final, tm=512, tf=512):
    n, d = x.shape
    dff = wg.shape[2]
    return pl.pallas_call(
        functools.partial(_ffn_kernel, final=final),
        grid=(n // tm, dff // tf),
        in_specs=[
            pl.BlockSpec((tm, d), lambda i, f: (i, 0)),
            pl.BlockSpec((1, d), lambda i, f: (0, 0)),
            pl.BlockSpec((None, d, tf), lambda i, f: (layer, 0, f)),
            pl.BlockSpec((None, d, tf), lambda i, f: (layer, 0, f)),
            pl.BlockSpec((None, tf, d), lambda i, f: (layer, f, 0)),
            pl.BlockSpec((1, d), lambda i, f: (0, 0)),
        ],
        out_specs=pl.BlockSpec((tm, d), lambda i, f: (i, 0)),
        out_shape=jax.ShapeDtypeStruct((n, d), F32),
        scratch_shapes=[pltpu.VMEM((tm, d), BF16), pltpu.VMEM((tm, d), F32)],
        compiler_params=_params(("parallel", "arbitrary")),
        name="ffn",
    )(x, g.reshape(1, d), wg, wu, wd, final_g.reshape(1, d))


def _rms_matmul_kernel(x_ref, g_ref, w_ref, o_ref, h_ref):
    @pl.when(pl.program_id(1) == 0)
    def _():
        h_ref[...] = _rms(x_ref[...], g_ref[...]).astype(BF16)

    o_ref[...] = _dot(h_ref[...], w_ref[...]).astype(o_ref.dtype)


def _rms_matmul(x, g, w, out_dtype, *, tm=1024, tn=1024, name):
    n, d = x.shape
    c = w.shape[1]
    return pl.pallas_call(
        _rms_matmul_kernel,
        grid=(n // tm, c // tn),
        in_specs=[
            pl.BlockSpec((tm, d), lambda i, j: (i, 0)),
            pl.BlockSpec((1, d), lambda i, j: (0, 0)),
            pl.BlockSpec((d, tn), lambda i, j: (0, j)),
        ],
        out_specs=pl.BlockSpec((tm, tn), lambda i, j: (i, j)),
        out_shape=jax.ShapeDtypeStruct((n, c), out_dtype),
        scratch_shapes=[pltpu.VMEM((tm, d), BF16)],
        compiler_params=_params(("parallel", "arbitrary")),
        name=name,
    )(x, g.reshape(1, d), w)


SB_GROUP = 8
SB_DEAD = 104.0


def _sb_kernel(q_ref, k_ref, v_ref, o_ref, kbd_ref, vbd_ref):
    i = pl.program_id(2)
    pairs = SB_GROUP // 2
    nkb = k_ref.shape[0] // QB
    r = lax.broadcasted_iota(jnp.int32, (QB, QB), 0)
    c = lax.broadcasted_iota(jnp.int32, (QB, QB), 1)
    strict_all = jnp.concatenate([c < r] * SB_GROUP, axis=0)
    later_ones = jnp.concatenate([(r > c).astype(BF16), jnp.ones((QB, QB), BF16)], axis=1)
    later_ones = jnp.concatenate([later_ones, later_ones], axis=0)

    @pl.when(i == 0)
    def _():
        low = c < HEAD_DIM
        zero16 = jnp.zeros((QB, QB), BF16)

        def build(j, carry):
            src = pl.ds(pl.multiple_of(j * QB, QB), QB)
            dst = pl.ds(pl.multiple_of(j * 2 * QB, 2 * QB), 2 * QB)
            for pp in range(pairs):
                for ref, out in ((k_ref, kbd_ref), (v_ref, vbd_ref)):
                    x = ref[src, pp * 128:(pp + 1) * 128]
                    out[pp, dst, :] = jnp.concatenate([jnp.where(low, x, zero16), jnp.where(low, zero16, x)],
                                                      axis=0)
            return carry

        lax.fori_loop(0, nkb, build, 0)

    def scores(j, diagonal):
        rows = pl.ds(pl.multiple_of(j * 2 * QB, 2 * QB), 2 * QB)
        zs = []
        for pp in range(pairs):
            z_pair = _dot_nt(q_ref[:, pp * 128:(pp + 1) * 128], kbd_ref[pp, rows, :])
            zs += [z_pair[:, 0:QB], z_pair[:, QB:2 * QB]]
        z = jnp.concatenate(zs, axis=0)
        soft = jnp.maximum(z, 0.0) + jnp.log(1.0 + jnp.exp(-jnp.abs(z)))
        if diagonal:
            soft = jnp.where(strict_all, soft, 0.0)
        hi = soft.astype(BF16)
        lo = (soft - hi.astype(F32)).astype(BF16)
        sums = _dot(jnp.concatenate([hi, lo], axis=1), later_ones)
        return rows, z - soft, sums

    def absorb(tails, accs, rows, log_beta, sums, diagonal):
        a = jnp.exp(log_beta - (tails + sums[:, 0:QB]))
        if diagonal:
            a = jnp.where(strict_all, a, 0.0)
        a = a.astype(BF16)
        outs = []
        for pp in range(pairs):
            pair = jnp.concatenate([a[2 * pp * QB:(2 * pp + 1) * QB], a[(2 * pp + 1) * QB:(2 * pp + 2) * QB]],
                                   axis=1)
            outs.append(_dot(pair, vbd_ref[pp, rows, :]))
        return tails + sums[:, QB:2 * QB], accs + jnp.concatenate(outs, axis=0)

    def sweep(js, carry):
        staged = [scores(j, False) for j in js]
        for st in staged:
            carry = absorb(*carry, *st, False)
        return carry

    carry = (jnp.zeros((SB_GROUP * QB, QB), F32), jnp.zeros((pairs * QB, 128), F32))
    carry = absorb(*carry, *scores(i, True), True)
    odd = i % 2
    carry = lax.fori_loop(0, odd, lambda step, cr: sweep([i - 1], cr), carry)
    top = i - 1 - odd

    def live(state):
        step, smallest, _, _ = state
        return (step < i // 2) & (smallest < SB_DEAD)

    def pair_step(state):
        step, _, tails, accs = state
        tails, accs = sweep([top - 2 * step, top - 2 * step - 1], (tails, accs))
        return step + 1, jnp.min(tails), tails, accs

    accs = lax.while_loop(live, pair_step, (0, jnp.min(carry[0]), *carry))[3]
    for pp in range(pairs):
        o_ref[:, pp * 128:(pp + 1) * 128] = accs[pp * QB:(pp + 1) * QB]


def _sb_attention(pb, batch, seq):
    nqb = seq // QB
    w = SB_GROUP * HEAD_DIM
    qc, kc, vc = PB_SB_Q // w, PB_SB_K // w, PB_SB_V // w
    return pl.pallas_call(
        _sb_kernel,
        grid=(batch, SB_HEADS // SB_GROUP, nqb),
        in_specs=[
            pl.BlockSpec((QB, w), lambda b, p, i: (b * nqb + i, qc + p)),
            pl.BlockSpec((seq, w), lambda b, p, i: (b, kc + p)),
            pl.BlockSpec((seq, w), lambda b, p, i: (b, vc + p)),
        ],
        out_specs=pl.BlockSpec((QB, w), lambda b, p, i: (b * nqb + i, p)),
        out_shape=jax.ShapeDtypeStruct((batch * seq, SB_WIDTH), F32),
        scratch_shapes=[pltpu.VMEM((SB_GROUP // 2, 2 * seq, 128), BF16),
                        pltpu.VMEM((SB_GROUP // 2, 2 * seq, 128), BF16)],
        compiler_params=_params(("parallel", "parallel", "arbitrary")),
        name="sb_attention",
    )(pb, pb, pb)


CMP_TQ = 4 * QB


def _nsa_cmp_kernel(q_ref, uk_ref, uv_ref, pos_ref, w_ref, bias_ref, ovt_ref, ocmp_ref, sel_ref,
                    kc_ref, vc_ref, *, n_cmp, n_sel):
    i = pl.program_id(2)
    half = CMP_STRIDE * HEAD_DIM

    @pl.when(i == 0)
    def _():
        def compress(kv, u_ref):
            u = u_ref[0, 0]
            top = _dot((u + pos_ref[kv, 0:1, :]).astype(BF16), w_ref[kv, 0:half, :])
            bot = _dot((u + pos_ref[kv, 1:2, :]).astype(BF16), w_ref[kv, half:2 * half, :])
            return top + pltpu.roll(bot, QB - 1, 0)

        zeros = jnp.zeros((QB, HEAD_DIM), F32)
        kc_ref[...] = jnp.concatenate([compress(0, uk_ref), zeros], axis=1).astype(BF16)
        v = compress(1, uv_ref)
        vc_ref[...] = jnp.concatenate([jnp.concatenate([v, zeros], axis=1),
                                       jnp.concatenate([zeros, v], axis=1)], axis=0).astype(BF16)

    G = NSA_GQA
    TQ = CMP_TQ
    q4 = jnp.concatenate([q_ref[:, g * 128:(g + 1) * 128] for g in range(G)], axis=0)
    s = _dot_nt(q4, kc_ref[...]) + bias_ref[0]
    e = jnp.exp(s - jnp.max(s, axis=-1, keepdims=True))
    t_rows = i * TQ + lax.broadcasted_iota(jnp.int32, (TQ, 1), 0)
    any_valid = jnp.concatenate([t_rows >= CMP_BLOCK - 1] * G, axis=0)
    p = jnp.where(any_valid, e / jnp.sum(e, axis=-1, keepdims=True), 0.0)
    p16 = p.astype(BF16)
    for pp in range(G // 2):
        pair = jnp.concatenate([p16[2 * pp * TQ:(2 * pp + 1) * TQ], p16[(2 * pp + 1) * TQ:(2 * pp + 2) * TQ]],
                               axis=1)
        ocmp_ref[:, pp * 128:(pp + 1) * 128] = _dot(pair, vc_ref[...])
    p_all = p[0:TQ]
    for g in range(1, G):
        p_all = p_all + p[g * TQ:(g + 1) * TQ]

    blk = lax.broadcasted_iota(jnp.int32, (n_sel, QB), 0)
    for part in range(TQ // QB):
        rs = slice(part * QB, (part + 1) * QB)
        t0 = (i * (TQ // QB) + part) * QB
        p_sum = p_all[rs]
        hi = p_sum.astype(BF16)
        lo = (p_sum - hi.astype(F32)).astype(BF16)
        p_sel = (_dot_nt(ovt_ref[...], hi) + _dot_nt(ovt_ref[...], lo))[0:n_sel]
        t = t0 + lax.broadcasted_iota(jnp.int32, (n_sel, QB), 1)
        cur = t // SEL_BLOCK
        eligible = blk * SEL_BLOCK <= t
        forced = (blk == 0) | (blk == cur) | (blk == cur - 1)
        score = jnp.where(eligible, p_sel + jnp.where(forced, FORCE_SCORE, 0.0), NEG_INF)
        rank = jnp.zeros((n_sel, QB), F32)
        for j in range(n_sel):
            other = score[j:j + 1, :]
            ahead = (other > score) | ((other == score) & (blk > j))
            rank = rank + jnp.where(ahead, 1.0, 0.0)
        chosen = jnp.where(eligible & (rank < SEL_TOPK), 1.0, 0.0)
        chosen = jnp.concatenate([chosen, jnp.zeros((QB - n_sel, QB), F32)], axis=0)
        sel_ref[0, 0, rs, :] = chosen.T.astype(BF16)


def _nsa_cmp(pb, ucmp, pos2, cmp_w, bias_c, overlap_t, batch, seq):
    nqb = seq // CMP_TQ
    n_cmp = (seq - CMP_BLOCK) // CMP_STRIDE + 1
    n_sel = seq // SEL_BLOCK
    ng = seq // CMP_STRIDE
    assert ng == QB and n_sel <= QB and n_sel % 8 == 0
    wide = CMP_STRIDE * HEAD_DIM
    return pl.pallas_call(
        functools.partial(_nsa_cmp_kernel, n_cmp=n_cmp, n_sel=n_sel),
        grid=(batch, NSA_KV_HEADS, nqb),
        in_specs=[
            pl.BlockSpec((CMP_TQ, NSA_GQA * 128), lambda b, h, i: (b * nqb + i, h)),
            pl.BlockSpec((1, 1, ng, wide), lambda b, h, i: (b, h, 0, 0)),
            pl.BlockSpec((1, 1, ng, wide), lambda b, h, i: (b, NSA_KV_HEADS + h, 0, 0)),
            pl.BlockSpec((2, 2, wide), lambda b, h, i: (0, 0, 0)),
            pl.BlockSpec((2, 2 * wide, HEAD_DIM), lambda b, h, i: (0, 0, 0)),
            pl.BlockSpec((1, NSA_GQA * CMP_TQ, QB), lambda b, h, i: (h * nqb + i, 0, 0)),
            pl.BlockSpec((QB, QB), lambda b, h, i: (0, 0)),
        ],
        out_specs=[
            pl.BlockSpec((CMP_TQ, 256), lambda b, h, i: (b * nqb + i, h)),
            pl.BlockSpec((1, 1, CMP_TQ, QB), lambda b, h, i: (b, h, i, 0)),
        ],
        out_shape=[
            jax.ShapeDtypeStruct((batch * seq, NSA_WIDTH), F32),
            jax.ShapeDtypeStruct((batch, NSA_KV_HEADS, seq, QB), BF16),
        ],
        scratch_shapes=[pltpu.VMEM((QB, 128), BF16), pltpu.VMEM((2 * QB, 128), BF16)],
        compiler_params=_params(("parallel", "parallel", "arbitrary")),
        name="nsa_compressed",
    )(pb, ucmp, ucmp, pos2, cmp_w, bias_c, overlap_t)


SW_TQ = 2 * QB
SEL_CHUNK = 4 * QB
WIN_FAR = WINDOW - QB


def _nsa_sw_kernel(q_ref, ks_ref, kw_ref, ocmp_ref, misc_ref, sel_ref, exp_ref, tb_ref, o_ref, madd_ref):
    i = pl.program_id(2)
    G = NSA_GQA
    TQ = SW_TQ
    q4 = jnp.concatenate([q_ref[:, g * 128:(g + 1) * 128] for g in range(G)], axis=0)
    madd_ref[...] = (_dot(sel_ref[0, 0], exp_ref[...]) - 1.0) * (-NEG_INF)
    t_pos = i * TQ + lax.broadcasted_iota(jnp.int32, (TQ, 1), 0)

    low = lax.broadcasted_iota(jnp.int32, (QB, 128), 1) < HEAD_DIM

    def tile(x, n=G):
        return jnp.concatenate([x] * n, axis=0)

    def ones_v(kv):
        return jnp.where(tile(low, kv.shape[0] // QB), jnp.ones_like(kv), kv)

    first = 2 * i - 1
    rows_m = pl.ds(pl.multiple_of(jnp.maximum(first, 0) * QB, QB), QB)
    rows_a = pl.ds(pl.multiple_of(2 * i * QB, QB), QB)
    rows_b = pl.ds(pl.multiple_of((2 * i + 1) * QB, QB), QB)
    gone = jnp.where(i >= 1, 0.0, NEG_INF)
    gone_m = jnp.concatenate([jnp.full((TQ, QB), gone, F32), jnp.zeros((TQ, 2 * QB), F32)], axis=1)
    far_end = first * QB

    kv = jnp.concatenate([ks_ref[rows_m, :], ks_ref[rows_a, :], ks_ref[rows_b, :]], axis=0)
    near_sel = jnp.concatenate([madd_ref[:, rows_m], madd_ref[:, rows_a], madd_ref[:, rows_b]], axis=1)
    s = _dot_nt(q4, kv) + (tb_ref[0] + tile(near_sel + gone_m))
    m = jnp.max(s, axis=-1, keepdims=True)
    acc = _dot(jnp.exp(s - m).astype(BF16), ones_v(kv))

    def sel_far(cidx, state):
        m, acc = state
        cols = pl.ds(pl.multiple_of(cidx * SEL_CHUNK, SEL_CHUNK), SEL_CHUNK)
        k_pos = cidx * SEL_CHUNK + lax.broadcasted_iota(jnp.int32, (TQ, SEL_CHUNK), 1)
        add = jnp.where(k_pos < far_end, madd_ref[:, cols], NEG_INF)
        kv = ks_ref[cols, :]
        s = _dot_nt(q4, kv) + tile(add)
        m_new = jnp.maximum(m, jnp.max(s, axis=-1, keepdims=True))
        return m_new, jnp.exp(m - m_new) * acc + _dot(jnp.exp(s - m_new).astype(BF16), ones_v(kv))

    _, sel_acc = lax.fori_loop(0, (2 * i + 2) // (SEL_CHUNK // QB), sel_far, (m, acc))

    start = jnp.maximum(first - WIN_FAR // QB, 0) * QB
    rows_far = pl.ds(pl.multiple_of(start, QB), WIN_FAR)
    k_pos = start + lax.broadcasted_iota(jnp.int32, (TQ, WIN_FAR), 1)
    live = (k_pos < far_end) & (k_pos > t_pos - WINDOW)
    kv = jnp.concatenate([kw_ref[rows_far, :], kw_ref[rows_m, :], kw_ref[rows_a, :], kw_ref[rows_b, :]], axis=0)
    bias = jnp.concatenate([tile(jnp.where(live, 0.0, NEG_INF)), tb_ref[0] + tile(gone_m)], axis=1)
    s = _dot_nt(q4, kv) + bias
    win_acc = _dot(jnp.exp(s - jnp.max(s, axis=-1, keepdims=True)).astype(BF16), ones_v(kv))

    gate = _sigmoid(misc_ref[...])
    low = tile(low, TQ // QB)

    def pair(acc, pp):
        a, b = acc[2 * pp * TQ:(2 * pp + 1) * TQ], acc[(2 * pp + 1) * TQ:(2 * pp + 2) * TQ]
        return (jnp.where(low, pltpu.roll(a, HEAD_DIM, 1), b)
                / jnp.where(low, a, pltpu.roll(b, HEAD_DIM, 1)))

    def pair_gate(branch, pp):
        ca, cb = 3 * (2 * pp) + branch, 3 * (2 * pp + 1) + branch
        return jnp.where(low, gate[:, ca:ca + 1], gate[:, cb:cb + 1])

    for pp in range(G // 2):
        lanes = slice(pp * 128, (pp + 1) * 128)
        o_ref[:, lanes] = (pair_gate(0, pp) * ocmp_ref[:, lanes] + pair_gate(1, pp) * pair(sel_acc, pp)
                           + pair_gate(2, pp) * pair(win_acc, pp))


def _nsa_sel_win(pb, pf, o_cmp, sel, expand, tb, batch, seq):
    nqb = seq // SW_TQ
    G = NSA_GQA
    return pl.pallas_call(
        _nsa_sw_kernel,
        grid=(batch, NSA_KV_HEADS, nqb),
        in_specs=[
            pl.BlockSpec((SW_TQ, G * 128), lambda b, h, i: (b * nqb + i, h)),
            pl.BlockSpec((seq, 128), lambda b, h, i: (b, PB_SEL // 128 + h)),
            pl.BlockSpec((seq, 128), lambda b, h, i: (b, PB_WIN // 128 + h)),
            pl.BlockSpec((SW_TQ, 256), lambda b, h, i: (b * nqb + i, h)),
            pl.BlockSpec((SW_TQ, 128), lambda b, h, i: (b * nqb + i, PF_MISC // 128 + h)),
            pl.BlockSpec((1, 1, SW_TQ, QB), lambda b, h, i: (b, h, i, 0)),
            pl.BlockSpec((QB, seq), lambda b, h, i: (0, 0)),
            pl.BlockSpec((1, G * SW_TQ, 3 * QB), lambda b, h, i: (h, 0, 0)),
        ],
        out_specs=pl.BlockSpec((SW_TQ, 256), lambda b, h, i: (b * nqb + i, h)),
        out_shape=jax.ShapeDtypeStruct((batch * seq, NSA_WIDTH), F32),
        scratch_shapes=[pltpu.VMEM((SW_TQ, seq), F32)],
        compiler_params=_params(("parallel", "parallel", "arbitrary")),
        name="nsa_selected_window",
    )(pb, pb, pb, o_cmp, pf, sel, expand, tb)


def _ssd_kernel(z_ref, xs_ref, bc_ref, misc_ref, cwx_ref, cwb_ref, cbx_ref, cbb_ref, hp_ref, spread_ref,
                dskip_ref, ng_ref, o_ref, xbuf, bbuf, state):
    ci = pl.program_id(1)
    L = QB
    P = HEAD_DIM
    GN = SSD_GROUPS * SSD_STATE

    @pl.when(ci == 0)
    def _():
        xbuf[...] = jnp.zeros_like(xbuf)
        bbuf[...] = jnp.zeros_like(bbuf)
        state[...] = jnp.zeros_like(state)

    def conv_silu(buf, src_ref, w_ref, b_ref):
        cur = src_ref[...]
        ext = jnp.concatenate([buf[...], cur], axis=0)
        out = b_ref[...] + w_ref[SSD_CONV - 1:SSD_CONV, :] * cur
        for k in range(1, SSD_CONV):
            out = out + w_ref[SSD_CONV - 1 - k:SSD_CONV - k, :] * pltpu.roll(ext, k, 0)[8:8 + L]
        buf[...] = cur[L - 8:L]
        return _silu(out)

    xs = conv_silu(xbuf, xs_ref, cwx_ref, cbx_ref)
    bcs = conv_silu(bbuf, bc_ref, cwb_ref, cbb_ref)

    misc = misc_ref[...]
    pre = misc + hp_ref[0:1, :]
    dt = jnp.maximum(pre, 0.0) + jnp.log1p(jnp.exp(-jnp.abs(pre)))
    a_dt = dt * (-jnp.exp(hp_ref[1:2, :]))
    r = lax.broadcasted_iota(jnp.int32, (L, L), 0)
    c = lax.broadcasted_iota(jnp.int32, (L, L), 1)
    causal = r >= c
    a_cs = _split3_dot_left(causal.astype(BF16), a_dt)
    a_cs_t = a_cs.T

    dt_full = _split3_dot(dt, spread_ref[...])
    acs_full = _split3_dot(a_cs, spread_ref[...])
    total_full = acs_full[L - 1:L, :]
    xdt = xs * dt_full
    xdt16 = xdt.astype(BF16)
    to_end16 = (xdt * jnp.exp(total_full - acs_full)).astype(BF16)
    decay_in = jnp.exp(acs_full)
    chunk_decay = jnp.exp(total_full)
    low = c < P
    zero16 = jnp.zeros((L, 2 * P), BF16)
    width = SSD_HPG * P

    ys = []
    for g in range(SSD_GROUPS):
        bm = bcs[:, g * SSD_STATE:(g + 1) * SSD_STATE]
        cm16 = bcs[:, GN + g * SSD_STATE:GN + (g + 1) * SSD_STATE].astype(BF16)
        cb = _dot_nt(cm16, bm.astype(BF16))
        lanes = slice(g * width, (g + 1) * width)
        h_in = state[g]
        y_off = _dot(cm16, h_in.astype(BF16)) * decay_in[:, lanes]
        state[g] = h_in * chunk_decay[:, lanes] + _dot(bm.T.astype(BF16), to_end16[:, lanes])
        diag = []
        for pp in range(SSD_HPG // 2):
            hd = g * SSD_HPG + 2 * pp
            decayed = []
            for col in (MISC_DT + hd, MISC_DT + hd + 1):
                seg = jnp.exp(jnp.where(causal, a_cs[:, col:col + 1] - a_cs_t[col:col + 1, :], -jnp.inf))
                decayed.append((cb * seg).astype(BF16))
            xp = xdt16[:, hd * P:(hd + 2) * P]
            x_bd = jnp.concatenate([jnp.where(low, xp, zero16), jnp.where(low, zero16, xp)], axis=0)
            diag.append(_dot(jnp.concatenate(decayed, axis=1), x_bd))
        ys.append(jnp.concatenate(diag, axis=1) + y_off)

    y = (jnp.concatenate(ys, axis=1) + dskip_ref[...] * xs) * _silu(z_ref[...])
    for g in range(SSD_GROUPS):
        lanes = slice(g * width, (g + 1) * width)
        yg = y[:, lanes]
        o_ref[:, lanes] = yg * lax.rsqrt(jnp.mean(yg * yg, axis=-1, keepdims=True) + EPS) * ng_ref[:, lanes]


def _ssd(pf, conv_w, conv_b, head_params, d_skip, norm_g, batch, seq):
    nc = seq // QB
    GN = SSD_GROUPS * SSD_STATE
    lane_head = jnp.arange(SSD_INNER)[None, :] // HEAD_DIM
    spread = (jnp.arange(128)[:, None] == MISC_DT + lane_head).astype(BF16)
    cwx, cwb = conv_w[:, :SSD_INNER], conv_w[:, SSD_INNER:]
    cbx, cbb = conv_b[:SSD_INNER].reshape(1, -1), conv_b[SSD_INNER:].reshape(1, -1)
    full = lambda shape: pl.BlockSpec(shape, lambda b, ci: (0,) * len(shape))
    return pl.pallas_call(
        _ssd_kernel,
        grid=(batch, nc),
        in_specs=[
            pl.BlockSpec((QB, SSD_INNER), lambda b, ci: (b * nc + ci, PF_Z // SSD_INNER)),
            pl.BlockSpec((QB, SSD_INNER), lambda b, ci: (b * nc + ci, PF_XS // SSD_INNER)),
            pl.BlockSpec((QB, 2 * GN), lambda b, ci: (b * nc + ci, PF_BC // (2 * GN))),
            pl.BlockSpec((QB, 128), lambda b, ci: (b * nc + ci, PF_MISC // 128)),
            full((SSD_CONV, SSD_INNER)),
            full((SSD_CONV, 2 * GN)),
            full((1, SSD_INNER)),
            full((1, 2 * GN)),
            full((8, 128)),
            full((128, SSD_INNER)),
            full((1, SSD_INNER)),
            full((1, SSD_INNER)),
        ],
        out_specs=pl.BlockSpec((QB, SSD_INNER), lambda b, ci: (b * nc + ci, 0)),
        out_shape=jax.ShapeDtypeStruct((batch * seq, SSD_INNER), F32),
        scratch_shapes=[
            pltpu.VMEM((8, SSD_INNER), F32),
            pltpu.VMEM((8, 2 * GN), F32),
            pltpu.VMEM((SSD_GROUPS, SSD_STATE, SSD_HPG * HEAD_DIM), F32),
        ],
        compiler_params=_params(("parallel", "arbitrary")),
        name="ssd",
    )(pf, pf, pf, pf, cwx, cwb, cbx, cbb, head_params, spread,
      jnp.repeat(d_skip.astype(F32), HEAD_DIM).reshape(1, -1), norm_g.reshape(1, -1))


def _out_proj_kernel(x_ref, nsa_ref, sb_ref, ssd_ref, gn_ref, gs_ref, w_ref, o_ref, mix_ref):
    @pl.when(pl.program_id(1) == 0)
    def _():
        mix_ref[:, 0:NSA_WIDTH] = _rms(nsa_ref[...], gn_ref[...]).astype(BF16)
        mix_ref[:, NSA_WIDTH:NSA_WIDTH + SB_WIDTH] = _rms(sb_ref[...], gs_ref[...]).astype(BF16)
        mix_ref[:, NSA_WIDTH + SB_WIDTH:] = ssd_ref[...].astype(BF16)

    o_ref[...] = x_ref[...] + _dot(mix_ref[...], w_ref[...])


def _out_proj(x, o_nsa, o_sb, o_ssd, g_nsa, g_sb, w, *, tm=1024, tn=1024):
    n, d = x.shape
    dm = w.shape[0]
    return pl.pallas_call(
        _out_proj_kernel,
        grid=(n // tm, d // tn),
        in_specs=[
            pl.BlockSpec((tm, tn), lambda i, j: (i, j)),
            pl.BlockSpec((tm, NSA_WIDTH), lambda i, j: (i, 0)),
            pl.BlockSpec((tm, SB_WIDTH), lambda i, j: (i, 0)),
            pl.BlockSpec((tm, SSD_INNER), lambda i, j: (i, 0)),
            pl.BlockSpec((1, NSA_WIDTH), lambda i, j: (0, 0)),
            pl.BlockSpec((1, SB_WIDTH), lambda i, j: (0, 0)),
            pl.BlockSpec((dm, tn), lambda i, j: (0, j)),
        ],
        out_specs=pl.BlockSpec((tm, tn), lambda i, j: (i, j)),
        out_shape=jax.ShapeDtypeStruct((n, d), F32),
        scratch_shapes=[pltpu.VMEM((tm, dm), BF16)],
        compiler_params=_params(("parallel", "arbitrary")),
        name="out_proj",
    )(x, o_nsa, o_sb, o_ssd, g_nsa.reshape(1, -1), g_sb.reshape(1, -1), w)


def _rel_bucket(dist):
    dist = jnp.maximum(dist, 0)
    max_exact = REL_BUCKETS // 2
    log_ratio = jnp.log(jnp.maximum(dist, 1).astype(F32) / max_exact) / math.log(REL_MAX_DIST / max_exact)
    large = jnp.minimum(max_exact + (log_ratio * (REL_BUCKETS - max_exact)).astype(jnp.int32), REL_BUCKETS - 1)
    return jnp.where(dist < max_exact, dist, large)


def _bias_tables(rel_bias, seq):
    assert QB >= REL_MAX_DIST
    def lookup(dist):
        onehot = jax.nn.one_hot(_rel_bucket(dist), REL_BUCKETS, dtype=F32)
        return jnp.einsum('...k,kh->h...', onehot, rel_bias, precision=lax.Precision.HIGHEST)

    t = jnp.arange(seq)[:, None]
    cend = jnp.arange(QB)[None, :] * CMP_STRIDE + CMP_BLOCK - 1
    n_cmp = (seq - CMP_BLOCK) // CMP_STRIDE + 1
    valid_c = (t >= cend) & (jnp.arange(QB)[None, :] < n_cmp)
    bias_c = jnp.where(valid_c, lookup(t - cend), NEG_INF)
    steps = seq // CMP_TQ
    bias_c = bias_c.reshape(NSA_KV_HEADS, NSA_GQA, steps, CMP_TQ, QB).transpose(0, 2, 1, 3, 4)
    bias_c = bias_c.reshape(NSA_KV_HEADS * steps, NSA_GQA * CMP_TQ, QB)
    r = jnp.arange(QB)[:, None]
    m = jnp.arange(QB)[None, :]
    near = jnp.stack([lookup(r - m), lookup(QB + r - m)])
    near = near - rel_bias[REL_BUCKETS - 1][None, :, None, None]
    diag = near[0] + jnp.where(m > r, NEG_INF, 0.0)
    prev = near[1]
    rows_a = jnp.concatenate([prev, diag, jnp.full_like(diag, NEG_INF)], axis=2)
    rows_b = jnp.concatenate([jnp.zeros_like(diag), prev, diag], axis=2)
    near2 = jnp.concatenate([rows_a, rows_b], axis=1)
    return bias_c, near2.reshape(NSA_KV_HEADS, NSA_GQA * SW_TQ, 3 * QB)


def _expand_table(seq):
    j = jnp.arange(QB)[:, None]
    s = jnp.arange(seq)[None, :]
    return (s // SEL_BLOCK == j).astype(BF16)


def _overlap_table(seq):
    n_cmp = (seq - CMP_BLOCK) // CMP_STRIDE + 1
    cs = jnp.arange(QB)[:, None] * CMP_STRIDE
    ce = cs + CMP_BLOCK - 1
    ss = jnp.arange(QB)[None, :] * SEL_BLOCK
    ov = jnp.maximum(jnp.minimum(ce, ss + SEL_BLOCK - 1) - jnp.maximum(cs, ss) + 1, 0).astype(F32) / CMP_BLOCK
    keep = (jnp.arange(QB)[:, None] < n_cmp) & (jnp.arange(QB)[None, :] < seq // SEL_BLOCK)
    return jnp.where(keep, ov, 0.0).T.astype(BF16)


def _in_proj_weights(w):
    scale = HEAD_DIM ** -0.5
    o_q, o_kv, o_gate = 0, NSA_WIDTH, NSA_WIDTH + 768
    o_sb = o_gate + 3 * NSA_HEADS
    o_z = o_sb + 3 * SB_WIDTH
    o_xbc = o_z + SSD_INNER
    o_dt = o_xbc + SSD_INNER + 2 * SSD_GROUPS * SSD_STATE
    col = lambda a, n: w[:, a:a + n]
    kv = lambda br, which, h: col(o_kv + br * 256 + which * 128 + h * HEAD_DIM, HEAD_DIM)
    zeros = lambda n: jnp.zeros((w.shape[0], n), w.dtype)
    branch = lambda br: [kv(br, 0, 0), kv(br, 1, 0), kv(br, 0, 1), kv(br, 1, 1)]
    q_heads = []
    for h in range(NSA_HEADS):
        q_heads += [col(o_q + h * HEAD_DIM, HEAD_DIM) * scale, zeros(128 - HEAD_DIM)]
    wb = jnp.concatenate(
        q_heads + branch(1) + branch(2)
        + [col(o_sb, SB_WIDTH) * scale, col(o_sb + SB_WIDTH, 2 * SB_WIDTH)], axis=1)
    gates = 3 * NSA_GQA
    wf = jnp.concatenate(
        [col(o_z, SSD_INNER), col(o_xbc, SSD_INNER + 2 * SSD_GROUPS * SSD_STATE), col(o_kv, 256),
         col(o_gate, gates), zeros(MISC_DT - gates), col(o_dt, SSD_HEADS), zeros(128 - MISC_DT - SSD_HEADS),
         col(o_gate + gates, gates), zeros(128 - gates)], axis=1)
    assert wb.shape[1] == PB_COLS and wf.shape[1] == PF_COLS
    return wb.astype(BF16), wf.astype(BF16)


def _head_params(dt_bias, a_log):
    rows = jnp.stack([dt_bias, a_log]).astype(F32)
    return jnp.zeros((8, 128), F32).at[0:2, MISC_DT:MISC_DT + SSD_HEADS].set(rows)


def _mixer(x, l, tables, batch, seq, mix_norm, w_in, w_out, cmp_pos, cmp_w, nsa_norm, sb_norm,
           conv_w, conv_b, dt_bias, a_log, d_skip, ssd_norm):
    bias_c, near, overlap, expand = tables
    wb, wf = _in_proj_weights(w_in[l])
    pb = _rms_matmul(x, mix_norm[l], wb, BF16, name="in_proj_bf16")
    pf = _rms_matmul(x, mix_norm[l], wf, F32, name="in_proj_f32")

    ng = seq // CMP_STRIDE
    ucmp = pf[:, PF_CMP:PF_CMP + 256].reshape(batch, ng, CMP_STRIDE, 4, HEAD_DIM)
    ucmp = ucmp.transpose(0, 3, 1, 2, 4).reshape(batch, 4, ng, CMP_STRIDE * HEAD_DIM)
    pos2 = cmp_pos[l].reshape(2, 2, CMP_STRIDE * HEAD_DIM)
    o_cmp, sel = _nsa_cmp(pb, ucmp, pos2, cmp_w[l].astype(BF16), bias_c, overlap, batch, seq)
    o_nsa = _nsa_sel_win(pb, pf, o_cmp, sel, expand, near, batch, seq)
    o_sb = _sb_attention(pb, batch, seq)
    o_ssd = _ssd(pf, conv_w[l], conv_b[l], _head_params(dt_bias[l], a_log[l]), d_skip[l], ssd_norm[l],
                 batch, seq)
    return _out_proj(x, o_nsa, o_sb, o_ssd, nsa_norm[l], sb_norm[l], w_out[l].astype(BF16))


def kernel(x, rel_bias, ffn1_norm, ffn1_w_gate, ffn1_w_up, ffn1_w_down, mix_norm, w_in, w_out, nsa_cmp_pos, nsa_cmp_w, nsa_out_norm, sb_out_norm, ssd_conv_w, ssd_conv_b, ssd_dt_bias, ssd_a_log, ssd_d, ssd_out_norm, ffn2_norm, ffn2_w_gate, ffn2_w_up, ffn2_w_down, final_norm):
    batch, seq, d = x.shape
    depth = w_in.shape[0]
    tables = _bias_tables(rel_bias, seq) + (_overlap_table(seq), _expand_table(seq))
    h = x.reshape(batch * seq, d)
    ffn1_w_gate, ffn1_w_up, ffn1_w_down, ffn2_w_gate, ffn2_w_up, ffn2_w_down = (
        w.astype(BF16) for w in (ffn1_w_gate, ffn1_w_up, ffn1_w_down, ffn2_w_gate, ffn2_w_up, ffn2_w_down))
    for l in range(depth):
        h = _ffn(h, ffn1_norm[l], ffn1_w_gate, ffn1_w_up, ffn1_w_down, l, final_norm, final=False)
        h = _mixer(h, l, tables, batch, seq, mix_norm, w_in, w_out, nsa_cmp_pos, nsa_cmp_w, nsa_out_norm,
                   sb_out_norm, ssd_conv_w, ssd_conv_b, ssd_dt_bias, ssd_a_log, ssd_d, ssd_out_norm)
        h = _ffn(h, ffn2_norm[l], ffn2_w_gate, ffn2_w_up, ffn2_w_down, l, final_norm,
                 final=(l == depth - 1))
    return h.reshape(batch, seq, d)
```

```python
import functools
import math

import jax
import jax.numpy as jnp
from jax import lax
from jax.experimental import pallas as pl
from jax.experimental.pallas import tpu as pltpu

D_MODEL = 2048
D_FF = 5632
HEAD_DIM = 64
QB = 128
NEG_INF = -1e30
EPS = 1e-6

NSA_HEADS = 8
NSA_KV_HEADS = 2
NSA_GQA = NSA_HEADS // NSA_KV_HEADS
NSA_WIDTH = NSA_HEADS * HEAD_DIM
CMP_BLOCK = 32
CMP_STRIDE = 16
SEL_BLOCK = 64
SEL_TOPK = 8
FORCE_SCORE = 1e3
WINDOW = 512
SB_HEADS = 8
SB_WIDTH = SB_HEADS * HEAD_DIM
SSD_HEADS = 16
SSD_INNER = SSD_HEADS * HEAD_DIM
SSD_GROUPS = 2
SSD_HPG = SSD_HEADS // SSD_GROUPS
SSD_STATE = 128
SSD_CONV = 4
REL_BUCKETS = 32
REL_MAX_DIST = 128

PB_NSA_Q, PB_SEL, PB_WIN, PB_SB_Q, PB_SB_K, PB_SB_V, PB_COLS = 0, 1024, 1280, 1536, 2048, 2560, 3072
PF_Z, PF_XS, PF_BC, PF_CMP, PF_MISC, PF_COLS = 0, 1024, 2048, 2560, 2816, 3072
MISC_DT = 16

VMEM_LIMIT = 52 * 1024 * 1024

BF16 = jnp.bfloat16
F32 = jnp.float32


def _dot(a, b):
    return jnp.dot(a, b, preferred_element_type=F32)


def _dot_nt(a, b):
    return lax.dot_general(a, b, (((1,), (1,)), ((), ())), preferred_element_type=F32)


def _split2_dot(x, m):
    hi = x.astype(BF16)
    lo = (x - hi.astype(F32)).astype(BF16)
    return _dot(hi, m) + _dot(lo, m)


def _split3_dot(x, m):
    hi = x.astype(BF16)
    r1 = x - hi.astype(F32)
    mid = r1.astype(BF16)
    lo = (r1 - mid.astype(F32)).astype(BF16)
    return _dot(hi, m) + _dot(mid, m) + _dot(lo, m)


def _split3_dot_left(m, x):
    hi = x.astype(BF16)
    r1 = x - hi.astype(F32)
    mid = r1.astype(BF16)
    lo = (r1 - mid.astype(F32)).astype(BF16)
    return _dot(m, hi) + _dot(m, mid) + _dot(m, lo)


def _rms(x, g):
    return x * lax.rsqrt(jnp.mean(x * x, axis=-1, keepdims=True) + EPS) * g


def _silu(x):
    return x / (1.0 + jnp.exp(-x))


def _sigmoid(x):
    return 1.0 / (1.0 + jnp.exp(-x))


def _params(sem):
    return pltpu.CompilerParams(dimension_semantics=sem, vmem_limit_bytes=VMEM_LIMIT)


def _cast_specs(weights, layer, steps, step_of):
    in_specs, out_specs, out_shapes = [], [], []
    for w in weights:
        _, rows, cols = w.shape
        slab = rows // steps
        assert slab * steps == rows and slab % 16 == 0
        in_specs.append(pl.BlockSpec((None, slab, cols), lambda *ids: (layer, step_of(*ids), 0)))
        out_specs.append(pl.BlockSpec((slab, cols), lambda *ids: (step_of(*ids), 0)))
        out_shapes.append(jax.ShapeDtypeStruct((rows, cols), BF16))
    return in_specs, out_specs, out_shapes


def _cast_slabs(srcs, dsts):
    for src, dst in zip(srcs, dsts):
        dst[...] = src[...].astype(BF16)


def _ffn_kernel(x_ref, g_ref, wg_ref, wu_ref, wd_ref, fg_ref, o_ref, h_ref, acc_ref, *, final):
    f = pl.program_id(1)

    @pl.when(f == 0)
    def _():
        h_ref[...] = _rms(x_ref[...], g_ref[...]).astype(BF16)
        acc_ref[...] = jnp.zeros_like(acc_ref)

    h = h_ref[...]
    gate = _dot(h, wg_ref[...])
    up = _dot(h, wu_ref[...])
    acc_ref[...] += _dot((_silu(gate) * up).astype(BF16), wd_ref[...])

    @pl.when(f == pl.num_programs(1) - 1)
    def _():
        y = x_ref[...] + 0.5 * acc_ref[...]
        if final:
            y = _rms(y, fg_ref[...])
        o_ref[...] = y


def _ffn(x, g, wg, wu, wd, layer, final_g, *, final, tm=512, tf=512):
    n, d = x.shape
    dff = wg.shape[2]
    return pl.pallas_call(
        functools.partial(_ffn_kernel, final=final),
        grid=(n // tm, dff // tf),
        in_specs=[
            pl.BlockSpec((tm, d), lambda i, f: (i, 0)),
            pl.BlockSpec((1, d), lambda i, f: (0, 0)),
            pl.BlockSpec((None, d, tf), lambda i, f: (layer, 0, f)),
            pl.BlockSpec((None, d, tf), lambda i, f: (layer, 0, f)),
            pl.BlockSpec((None, tf, d), lambda i, f: (layer, f, 0)),
            pl.BlockSpec((1, d), lambda i, f: (0, 0)),
        ],
        out_specs=pl.BlockSpec((tm, d), lambda i, f: (i, 0)),
        out_shape=jax.ShapeDtypeStruct((n, d), F32),
        scratch_shapes=[pltpu.VMEM((tm, d), BF16), pltpu.VMEM((tm, d), F32)],
        compiler_params=_params(("parallel", "arbitrary")),
        name="ffn",
    )(x, g.reshape(1, d), wg, wu, wd, final_g.reshape(1, d))


def _rms_matmul_kernel(x_ref, g_ref, w_ref, o_ref, h_ref):
    @pl.when(pl.program_id(1) == 0)
    def _():
        h_ref[...] = _rms(x_ref[...], g_ref[...]).astype(BF16)

    o_ref[...] = _dot(h_ref[...], w_ref[...]).astype(o_ref.dtype)


def _rms_matmul(x, g, w, out_dtype, *, tm=1024, tn=1024, name):
    n, d = x.shape
    c = w.shape[1]
    return pl.pallas_call(
        _rms_matmul_kernel,
        grid=(n // tm, c // tn),
        in_specs=[
            pl.BlockSpec((tm, d), lambda i, j: (i, 0)),
            pl.BlockSpec((1, d), lambda i, j: (0, 0)),
            pl.BlockSpec((d, tn), lambda i, j: (0, j)),
        ],
        out_specs=pl.BlockSpec((tm, tn), lambda i, j: (i, j)),
        out_shape=jax.ShapeDtypeStruct((n, c), out_dtype),
        scratch_shapes=[pltpu.VMEM((tm, d), BF16)],
        compiler_params=_params(("parallel", "arbitrary")),
        name=name,
    )(x, g.reshape(1, d), w)


SB_GROUP = 8
SB_DEAD = 104.0


def _sb_kernel(*refs, n_cast):
    q_ref, k_ref, v_ref = refs[:3]
    o_ref = refs[3 + n_cast]
    kbd_ref, vbd_ref = refs[4 + 2 * n_cast:]
    _cast_slabs(refs[3:3 + n_cast], refs[4 + n_cast:4 + 2 * n_cast])
    _sb_body(q_ref, k_ref, v_ref, o_ref, kbd_ref, vbd_ref)


def _sb_body(q_ref, k_ref, v_ref, o_ref, kbd_ref, vbd_ref):
    i = pl.program_id(2)
    pairs = SB_GROUP // 2
    nkb = k_ref.shape[0] // QB
    r = lax.broadcasted_iota(jnp.int32, (QB, QB), 0)
    c = lax.broadcasted_iota(jnp.int32, (QB, QB), 1)
    strict_all = jnp.concatenate([c < r] * SB_GROUP, axis=0)
    later_ones = jnp.concatenate([(r > c).astype(BF16), jnp.ones((QB, QB), BF16)], axis=1)
    later_ones = jnp.concatenate([later_ones, later_ones], axis=0)

    @pl.when(i == 0)
    def _():
        low = c < HEAD_DIM
        zero16 = jnp.zeros((QB, QB), BF16)

        def build(j, carry):
            src = pl.ds(pl.multiple_of(j * QB, QB), QB)
            dst = pl.ds(pl.multiple_of(j * 2 * QB, 2 * QB), 2 * QB)
            for pp in range(pairs):
                for ref, out in ((k_ref, kbd_ref), (v_ref, vbd_ref)):
                    x = ref[src, pp * 128:(pp + 1) * 128]
                    out[pp, dst, :] = jnp.concatenate([jnp.where(low, x, zero16), jnp.where(low, zero16, x)],
                                                      axis=0)
            return carry

        lax.fori_loop(0, nkb, build, 0)

    def scores(j, diagonal):
        rows = pl.ds(pl.multiple_of(j * 2 * QB, 2 * QB), 2 * QB)
        zs = []
        for pp in range(pairs):
            z_pair = _dot_nt(q_ref[:, pp * 128:(pp + 1) * 128], kbd_ref[pp, rows, :])
            zs += [z_pair[:, 0:QB], z_pair[:, QB:2 * QB]]
        z = jnp.concatenate(zs, axis=0)
        soft = jnp.maximum(z, 0.0) + jnp.log(1.0 + jnp.exp(-jnp.abs(z)))
        if diagonal:
            soft = jnp.where(strict_all, soft, 0.0)
        hi = soft.astype(BF16)
        lo = (soft - hi.astype(F32)).astype(BF16)
        sums = _dot(jnp.concatenate([hi, lo], axis=1), later_ones)
        return rows, z - soft, sums

    def absorb(tails, accs, rows, log_beta, sums, diagonal):
        a = jnp.exp(log_beta - (tails + sums[:, 0:QB]))
        if diagonal:
            a = jnp.where(strict_all, a, 0.0)
        a = a.astype(BF16)
        outs = []
        for pp in range(pairs):
            pair = jnp.concatenate([a[2 * pp * QB:(2 * pp + 1) * QB], a[(2 * pp + 1) * QB:(2 * pp + 2) * QB]],
                                   axis=1)
            outs.append(_dot(pair, vbd_ref[pp, rows, :]))
        return tails + sums[:, QB:2 * QB], accs + jnp.concatenate(outs, axis=0)

    def sweep(js, carry):
        staged = [scores(j, False) for j in js]
        for st in staged:
            carry = absorb(*carry, *st, False)
        return carry

    carry = (jnp.zeros((SB_GROUP * QB, QB), F32), jnp.zeros((pairs * QB, 128), F32))
    carry = absorb(*carry, *scores(i, True), True)
    odd = i % 2
    carry = lax.fori_loop(0, odd, lambda step, cr: sweep([i - 1], cr), carry)
    top = i - 1 - odd

    def live(state):
        step, smallest, _, _ = state
        return (step < i // 2) & (smallest < SB_DEAD)

    def pair_step(state):
        step, _, tails, accs = state
        tails, accs = sweep([top - 2 * step, top - 2 * step - 1], (tails, accs))
        return step + 1, jnp.min(tails), tails, accs

    accs = lax.while_loop(live, pair_step, (0, jnp.min(carry[0]), *carry))[3]
    for pp in range(pairs):
        o_ref[:, pp * 128:(pp + 1) * 128] = accs[pp * QB:(pp + 1) * QB]


def _sb_attention(pb, batch, seq, cast=(), cast_layer=0):
    nqb = seq // QB
    w = SB_GROUP * HEAD_DIM
    assert SB_HEADS == SB_GROUP
    qc, kc, vc = PB_SB_Q // w, PB_SB_K // w, PB_SB_V // w
    c_in, c_out, c_shapes = _cast_specs(cast, cast_layer, batch * nqb, lambda b, p, i: b * nqb + i)
    out = pl.pallas_call(
        functools.partial(_sb_kernel, n_cast=len(cast)),
        grid=(batch, SB_HEADS // SB_GROUP, nqb),
        in_specs=[
            pl.BlockSpec((QB, w), lambda b, p, i: (b * nqb + i, qc + p)),
            pl.BlockSpec((seq, w), lambda b, p, i: (b, kc + p)),
            pl.BlockSpec((seq, w), lambda b, p, i: (b, vc + p)),
        ] + c_in,
        out_specs=[pl.BlockSpec((QB, w), lambda b, p, i: (b * nqb + i, p))] + c_out,
        out_shape=[jax.ShapeDtypeStruct((batch * seq, SB_WIDTH), F32)] + c_shapes,
        scratch_shapes=[pltpu.VMEM((SB_GROUP // 2, 2 * seq, 128), BF16),
                        pltpu.VMEM((SB_GROUP // 2, 2 * seq, 128), BF16)],
        compiler_params=_params(("parallel", "parallel", "arbitrary")),
        name="sb_attention",
    )(pb, pb, pb, *cast)
    return out[0], out[1:]


CMP_TQ = 4 * QB


def _nsa_cmp_kernel(q_ref, uk_ref, uv_ref, pos_ref, w_ref, bias_ref, ovt_ref, ocmp_ref, sel_ref,
                    kc_ref, vc_ref, *, n_cmp, n_sel):
    i = pl.program_id(2)
    half = CMP_STRIDE * HEAD_DIM

    @pl.when(i == 0)
    def _():
        def compress(kv, u_ref):
            u = u_ref[0, 0]
            top = _dot((u + pos_ref[kv, 0:1, :]).astype(BF16), w_ref[kv, 0:half, :])
            bot = _dot((u + pos_ref[kv, 1:2, :]).astype(BF16), w_ref[kv, half:2 * half, :])
            return top + pltpu.roll(bot, QB - 1, 0)

        zeros = jnp.zeros((QB, HEAD_DIM), F32)
        kc_ref[...] = jnp.concatenate([compress(0, uk_ref), zeros], axis=1).astype(BF16)
        v = compress(1, uv_ref)
        vc_ref[...] = jnp.concatenate([jnp.concatenate([v, zeros], axis=1),
                                       jnp.concatenate([zeros, v], axis=1)], axis=0).astype(BF16)

    G = NSA_GQA
    TQ = CMP_TQ
    q4 = jnp.concatenate([q_ref[:, g * 128:(g + 1) * 128] for g in range(G)], axis=0)
    s = _dot_nt(q4, kc_ref[...]) + bias_ref[0]
    e = jnp.exp(s - jnp.max(s, axis=-1, keepdims=True))
    t_rows = i * TQ + lax.broadcasted_iota(jnp.int32, (TQ, 1), 0)
    any_valid = jnp.concatenate([t_rows >= CMP_BLOCK - 1] * G, axis=0)
    p = jnp.where(any_valid, e / jnp.sum(e, axis=-1, keepdims=True), 0.0)
    p16 = p.astype(BF16)
    for pp in range(G // 2):
        pair = jnp.concatenate([p16[2 * pp * TQ:(2 * pp + 1) * TQ], p16[(2 * pp + 1) * TQ:(2 * pp + 2) * TQ]],
                               axis=1)
        ocmp_ref[:, pp * 128:(pp + 1) * 128] = _dot(pair, vc_ref[...])
    p_all = p[0:TQ]
    for g in range(1, G):
        p_all = p_all + p[g * TQ:(g + 1) * TQ]

    blk = lax.broadcasted_iota(jnp.int32, (n_sel, QB), 0)
    for part in range(TQ // QB):
        rs = slice(part * QB, (part + 1) * QB)
        t0 = (i * (TQ // QB) + part) * QB
        p_sum = p_all[rs]
        hi = p_sum.astype(BF16)
        lo = (p_sum - hi.astype(F32)).astype(BF16)
        p_sel = (_dot_nt(ovt_ref[...], hi) + _dot_nt(ovt_ref[...], lo))[0:n_sel]
        t = t0 + lax.broadcasted_iota(jnp.int32, (n_sel, QB), 1)
        cur = t // SEL_BLOCK
        eligible = blk * SEL_BLOCK <= t
        forced = (blk == 0) | (blk == cur) | (blk == cur - 1)
        score = jnp.where(eligible, p_sel + jnp.where(forced, FORCE_SCORE, 0.0), NEG_INF)
        rank = jnp.zeros((n_sel, QB), F32)
        for j in range(n_sel):
            other = score[j:j + 1, :]
            ahead = (other > score) | ((other == score) & (blk > j))
            rank = rank + jnp.where(ahead, 1.0, 0.0)
        chosen = jnp.where(eligible & (rank < SEL_TOPK), 1.0, 0.0)
        chosen = jnp.concatenate([chosen, jnp.zeros((QB - n_sel, QB), F32)], axis=0)
        sel_ref[0, 0, rs, :] = chosen.T.astype(BF16)


def _nsa_cmp(pb, ucmp, pos2, cmp_w, bias_c, overlap_t, batch, seq):
    nqb = seq // CMP_TQ
    n_cmp = (seq - CMP_BLOCK) // CMP_STRIDE + 1
    n_sel = seq // SEL_BLOCK
    ng = seq // CMP_STRIDE
    assert ng == QB and n_sel <= QB and n_sel % 8 == 0
    wide = CMP_STRIDE * HEAD_DIM
    return pl.pallas_call(
        functools.partial(_nsa_cmp_kernel, n_cmp=n_cmp, n_sel=n_sel),
        grid=(batch, NSA_KV_HEADS, nqb),
        in_specs=[
            pl.BlockSpec((CMP_TQ, NSA_GQA * 128), lambda b, h, i: (b * nqb + i, h)),
            pl.BlockSpec((1, 1, ng, wide), lambda b, h, i: (b, h, 0, 0)),
            pl.BlockSpec((1, 1, ng, wide), lambda b, h, i: (b, NSA_KV_HEADS + h, 0, 0)),
            pl.BlockSpec((2, 2, wide), lambda b, h, i: (0, 0, 0)),
            pl.BlockSpec((2, 2 * wide, HEAD_DIM), lambda b, h, i: (0, 0, 0)),
            pl.BlockSpec((1, NSA_GQA * CMP_TQ, QB), lambda b, h, i: (h * nqb + i, 0, 0)),
            pl.BlockSpec((QB, QB), lambda b, h, i: (0, 0)),
        ],
        out_specs=[
            pl.BlockSpec((CMP_TQ, 256), lambda b, h, i: (b * nqb + i, h)),
            pl.BlockSpec((1, 1, CMP_TQ, QB), lambda b, h, i: (b, h, i, 0)),
        ],
        out_shape=[
            jax.ShapeDtypeStruct((batch * seq, NSA_WIDTH), F32),
            jax.ShapeDtypeStruct((batch, NSA_KV_HEADS, seq, QB), BF16),
        ],
        scratch_shapes=[pltpu.VMEM((QB, 128), BF16), pltpu.VMEM((2 * QB, 128), BF16)],
        compiler_params=_params(("parallel", "parallel", "arbitrary")),
        name="nsa_compressed",
    )(pb, ucmp, ucmp, pos2, cmp_w, bias_c, overlap_t)


SW_TQ = 2 * QB
SEL_CHUNK = 4 * QB
WIN_FAR = WINDOW - QB


def _nsa_sw_kernel(q_ref, ks_ref, kw_ref, ocmp_ref, misc_ref, sel_ref, exp_ref, tb_ref, o_ref, madd_ref):
    i = pl.program_id(2)
    G = NSA_GQA
    TQ = SW_TQ
    q4 = jnp.concatenate([q_ref[:, g * 128:(g + 1) * 128] for g in range(G)], axis=0)
    madd_ref[...] = (_dot(sel_ref[0, 0], exp_ref[...]) - 1.0) * (-NEG_INF)
    t_pos = i * TQ + lax.broadcasted_iota(jnp.int32, (TQ, 1), 0)

    low = lax.broadcasted_iota(jnp.int32, (QB, 128), 1) < HEAD_DIM

    def tile(x, n=G):
        return jnp.concatenate([x] * n, axis=0)

    def ones_v(kv):
        return jnp.where(tile(low, kv.shape[0] // QB), jnp.ones_like(kv), kv)

    first = 2 * i - 1
    rows_m = pl.ds(pl.multiple_of(jnp.maximum(first, 0) * QB, QB), QB)
    rows_a = pl.ds(pl.multiple_of(2 * i * QB, QB), QB)
    rows_b = pl.ds(pl.multiple_of((2 * i + 1) * QB, QB), QB)
    gone = jnp.where(i >= 1, 0.0, NEG_INF)
    gone_m = jnp.concatenate([jnp.full((TQ, QB), gone, F32), jnp.zeros((TQ, 2 * QB), F32)], axis=1)
    far_end = first * QB

    kv = jnp.concatenate([ks_ref[rows_m, :], ks_ref[rows_a, :], ks_ref[rows_b, :]], axis=0)
    near_sel = jnp.concatenate([madd_ref[:, rows_m], madd_ref[:, rows_a], madd_ref[:, rows_b]], axis=1)
    s = _dot_nt(q4, kv) + (tb_ref[0] + tile(near_sel + gone_m))
    m = jnp.max(s, axis=-1, keepdims=True)
    acc = _dot(jnp.exp(s - m).astype(BF16), ones_v(kv))

    def sel_far(cidx, state):
        m, acc = state
        cols = pl.ds(pl.multiple_of(cidx * SEL_CHUNK, SEL_CHUNK), SEL_CHUNK)
        k_pos = cidx * SEL_CHUNK + lax.broadcasted_iota(jnp.int32, (TQ, SEL_CHUNK), 1)
        add = jnp.where(k_pos < far_end, madd_ref[:, cols], NEG_INF)
        kv = ks_ref[cols, :]
        s = _dot_nt(q4, kv) + tile(add)
        m_new = jnp.maximum(m, jnp.max(s, axis=-1, keepdims=True))
        return m_new, jnp.exp(m - m_new) * acc + _dot(jnp.exp(s - m_new).astype(BF16), ones_v(kv))

    _, sel_acc = lax.fori_loop(0, (2 * i + 2) // (SEL_CHUNK // QB), sel_far, (m, acc))

    start = jnp.maximum(first - WIN_FAR // QB, 0) * QB
    rows_far = pl.ds(pl.multiple_of(start, QB), WIN_FAR)
    k_pos = start + lax.broadcasted_iota(jnp.int32, (TQ, WIN_FAR), 1)
    live = (k_pos < far_end) & (k_pos > t_pos - WINDOW)
    kv = jnp.concatenate([kw_ref[rows_far, :], kw_ref[rows_m, :], kw_ref[rows_a, :], kw_ref[rows_b, :]], axis=0)
    bias = jnp.concatenate([tile(jnp.where(live, 0.0, NEG_INF)), tb_ref[0] + tile(gone_m)], axis=1)
    s = _dot_nt(q4, kv) + bias
    win_acc = _dot(jnp.exp(s - jnp.max(s, axis=-1, keepdims=True)).astype(BF16), ones_v(kv))

    gate = _sigmoid(misc_ref[...])
    low = tile(low, TQ // QB)

    def pair(acc, pp):
        a, b = acc[2 * pp * TQ:(2 * pp + 1) * TQ], acc[(2 * pp + 1) * TQ:(2 * pp + 2) * TQ]
        return (jnp.where(low, pltpu.roll(a, HEAD_DIM, 1), b)
                / jnp.where(low, a, pltpu.roll(b, HEAD_DIM, 1)))

    def pair_gate(branch, pp):
        ca, cb = 3 * (2 * pp) + branch, 3 * (2 * pp + 1) + branch
        return jnp.where(low, gate[:, ca:ca + 1], gate[:, cb:cb + 1])

    for pp in range(G // 2):
        lanes = slice(pp * 128, (pp + 1) * 128)
        o_ref[:, lanes] = (pair_gate(0, pp) * ocmp_ref[:, lanes] + pair_gate(1, pp) * pair(sel_acc, pp)
                           + pair_gate(2, pp) * pair(win_acc, pp))


def _nsa_sel_win(pb, pf, o_cmp, sel, expand, tb, batch, seq):
    nqb = seq // SW_TQ
    G = NSA_GQA
    return pl.pallas_call(
        _nsa_sw_kernel,
        grid=(batch, NSA_KV_HEADS, nqb),
        in_specs=[
            pl.BlockSpec((SW_TQ, G * 128), lambda b, h, i: (b * nqb + i, h)),
            pl.BlockSpec((seq, 128), lambda b, h, i: (b, PB_SEL // 128 + h)),
            pl.BlockSpec((seq, 128), lambda b, h, i: (b, PB_WIN // 128 + h)),
            pl.BlockSpec((SW_TQ, 256), lambda b, h, i: (b * nqb + i, h)),
            pl.BlockSpec((SW_TQ, 128), lambda b, h, i: (b * nqb + i, PF_MISC // 128 + h)),
            pl.BlockSpec((1, 1, SW_TQ, QB), lambda b, h, i: (b, h, i, 0)),
            pl.BlockSpec((QB, seq), lambda b, h, i: (0, 0)),
            pl.BlockSpec((1, G * SW_TQ, 3 * QB), lambda b, h, i: (h, 0, 0)),
        ],
        out_specs=pl.BlockSpec((SW_TQ, 256), lambda b, h, i: (b * nqb + i, h)),
        out_shape=jax.ShapeDtypeStruct((batch * seq, NSA_WIDTH), F32),
        scratch_shapes=[pltpu.VMEM((SW_TQ, seq), F32)],
        compiler_params=_params(("parallel", "parallel", "arbitrary")),
        name="nsa_selected_window",
    )(pb, pb, pb, o_cmp, pf, sel, expand, tb)


def _ssd_kernel(*refs, n_cast):
    n_in = 12
    _cast_slabs(refs[n_in:n_in + n_cast], refs[n_in + n_cast + 1:n_in + 2 * n_cast + 1])
    _ssd_body(*refs[:n_in], refs[n_in + n_cast], *refs[n_in + 2 * n_cast + 1:])


def _ssd_body(z_ref, xs_ref, bc_ref, misc_ref, cwx_ref, cwb_ref, cbx_ref, cbb_ref, hp_ref, spread_ref,
              dskip_ref, ng_ref, o_ref, xbuf, bbuf, state):
    ci = pl.program_id(1)
    L = QB
    P = HEAD_DIM
    GN = SSD_GROUPS * SSD_STATE

    @pl.when(ci == 0)
    def _():
        xbuf[...] = jnp.zeros_like(xbuf)
        bbuf[...] = jnp.zeros_like(bbuf)
        state[...] = jnp.zeros_like(state)

    def conv_silu(buf, src_ref, w_ref, b_ref):
        cur = src_ref[...]
        ext = jnp.concatenate([buf[...], cur], axis=0)
        out = b_ref[...] + w_ref[SSD_CONV - 1:SSD_CONV, :] * cur
        for k in range(1, SSD_CONV):
            out = out + w_ref[SSD_CONV - 1 - k:SSD_CONV - k, :] * pltpu.roll(ext, k, 0)[8:8 + L]
        buf[...] = cur[L - 8:L]
        return _silu(out)

    xs = conv_silu(xbuf, xs_ref, cwx_ref, cbx_ref)
    bcs = conv_silu(bbuf, bc_ref, cwb_ref, cbb_ref)

    misc = misc_ref[...]
    pre = misc + hp_ref[0:1, :]
    dt = jnp.maximum(pre, 0.0) + jnp.log1p(jnp.exp(-jnp.abs(pre)))
    a_dt = dt * (-jnp.exp(hp_ref[1:2, :]))
    r = lax.broadcasted_iota(jnp.int32, (L, L), 0)
    c = lax.broadcasted_iota(jnp.int32, (L, L), 1)
    causal = r >= c
    a_cs = _split3_dot_left(causal.astype(BF16), a_dt)
    a_cs_t = a_cs.T

    dt_full = _split3_dot(dt, spread_ref[...])
    acs_full = _split3_dot(a_cs, spread_ref[...])
    total_full = acs_full[L - 1:L, :]
    xdt = xs * dt_full
    xdt16 = xdt.astype(BF16)
    to_end16 = (xdt * jnp.exp(total_full - acs_full)).astype(BF16)
    decay_in = jnp.exp(acs_full)
    chunk_decay = jnp.exp(total_full)
    low = c < P
    zero16 = jnp.zeros((L, 2 * P), BF16)
    width = SSD_HPG * P

    ys = []
    for g in range(SSD_GROUPS):
        bm = bcs[:, g * SSD_STATE:(g + 1) * SSD_STATE]
        cm16 = bcs[:, GN + g * SSD_STATE:GN + (g + 1) * SSD_STATE].astype(BF16)
        cb = _dot_nt(cm16, bm.astype(BF16))
        lanes = slice(g * width, (g + 1) * width)
        h_in = state[g]
        y_off = _dot(cm16, h_in.astype(BF16)) * decay_in[:, lanes]
        state[g] = h_in * chunk_decay[:, lanes] + _dot(bm.T.astype(BF16), to_end16[:, lanes])
        diag = []
        for pp in range(SSD_HPG // 2):
            hd = g * SSD_HPG + 2 * pp
            decayed = []
            for col in (MISC_DT + hd, MISC_DT + hd + 1):
                seg = jnp.exp(jnp.where(causal, a_cs[:, col:col + 1] - a_cs_t[col:col + 1, :], -jnp.inf))
                decayed.append((cb * seg).astype(BF16))
            xp = xdt16[:, hd * P:(hd + 2) * P]
            x_bd = jnp.concatenate([jnp.where(low, xp, zero16), jnp.where(low, zero16, xp)], axis=0)
            diag.append(_dot(jnp.concatenate(decayed, axis=1), x_bd))
        ys.append(jnp.concatenate(diag, axis=1) + y_off)

    y = (jnp.concatenate(ys, axis=1) + dskip_ref[...] * xs) * _silu(z_ref[...])
    for g in range(SSD_GROUPS):
        lanes = slice(g * width, (g + 1) * width)
        yg = y[:, lanes]
        o_ref[:, lanes] = yg * lax.rsqrt(jnp.mean(yg * yg, axis=-1, keepdims=True) + EPS) * ng_ref[:, lanes]


def _ssd(pf, conv_w, conv_b, head_params, d_skip, norm_g, batch, seq, cast=(), cast_layer=0):
    nc = seq // QB
    c_in, c_out, c_shapes = _cast_specs(cast, cast_layer, batch * nc, lambda b, ci: b * nc + ci)
    GN = SSD_GROUPS * SSD_STATE
    lane_head = jnp.arange(SSD_INNER)[None, :] // HEAD_DIM
    spread = (jnp.arange(128)[:, None] == MISC_DT + lane_head).astype(BF16)
    cwx, cwb = conv_w[:, :SSD_INNER], conv_w[:, SSD_INNER:]
    cbx, cbb = conv_b[:SSD_INNER].reshape(1, -1), conv_b[SSD_INNER:].reshape(1, -1)
    full = lambda shape: pl.BlockSpec(shape, lambda b, ci: (0,) * len(shape))
    out = pl.pallas_call(
        functools.partial(_ssd_kernel, n_cast=len(cast)),
        grid=(batch, nc),
        in_specs=[
            pl.BlockSpec((QB, SSD_INNER), lambda b, ci: (b * nc + ci, PF_Z // SSD_INNER)),
            pl.BlockSpec((QB, SSD_INNER), lambda b, ci: (b * nc + ci, PF_XS // SSD_INNER)),
            pl.BlockSpec((QB, 2 * GN), lambda b, ci: (b * nc + ci, PF_BC // (2 * GN))),
            pl.BlockSpec((QB, 128), lambda b, ci: (b * nc + ci, PF_MISC // 128)),
            full((SSD_CONV, SSD_INNER)),
            full((SSD_CONV, 2 * GN)),
            full((1, SSD_INNER)),
            full((1, 2 * GN)),
            full((8, 128)),
            full((128, SSD_INNER)),
            full((1, SSD_INNER)),
            full((1, SSD_INNER)),
        ] + c_in,
        out_specs=[pl.BlockSpec((QB, SSD_INNER), lambda b, ci: (b * nc + ci, 0))] + c_out,
        out_shape=[jax.ShapeDtypeStruct((batch * seq, SSD_INNER), F32)] + c_shapes,
        scratch_shapes=[
            pltpu.VMEM((8, SSD_INNER), F32),
            pltpu.VMEM((8, 2 * GN), F32),
            pltpu.VMEM((SSD_GROUPS, SSD_STATE, SSD_HPG * HEAD_DIM), F32),
        ],
        compiler_params=_params(("parallel", "arbitrary")),
        name="ssd",
    )(pf, pf, pf, pf, cwx, cwb, cbx, cbb, head_params, spread,
      jnp.repeat(d_skip.astype(F32), HEAD_DIM).reshape(1, -1), norm_g.reshape(1, -1), *cast)
    return out[0], out[1:]


def _out_proj_kernel(x_ref, nsa_ref, sb_ref, ssd_ref, gn_ref, gs_ref, w_ref, o_ref, mix_ref):
    @pl.when(pl.program_id(1) == 0)
    def _():
        mix_ref[:, 0:NSA_WIDTH] = _rms(nsa_ref[...], gn_ref[...]).astype(BF16)
        mix_ref[:, NSA_WIDTH:NSA_WIDTH + SB_WIDTH] = _rms(sb_ref[...], gs_ref[...]).astype(BF16)
        mix_ref[:, NSA_WIDTH + SB_WIDTH:] = ssd_ref[...].astype(BF16)

    o_ref[...] = x_ref[...] + _dot(mix_ref[...], w_ref[...])


def _out_proj(x, o_nsa, o_sb, o_ssd, g_nsa, g_sb, w, *, tm=1024, tn=1024):
    n, d = x.shape
    dm = w.shape[0]
    return pl.pallas_call(
        _out_proj_kernel,
        grid=(n // tm, d // tn),
        in_specs=[
            pl.BlockSpec((tm, tn), lambda i, j: (i, j)),
            pl.BlockSpec((tm, NSA_WIDTH), lambda i, j: (i, 0)),
            pl.BlockSpec((tm, SB_WIDTH), lambda i, j: (i, 0)),
            pl.BlockSpec((tm, SSD_INNER), lambda i, j: (i, 0)),
            pl.BlockSpec((1, NSA_WIDTH), lambda i, j: (0, 0)),
            pl.BlockSpec((1, SB_WIDTH), lambda i, j: (0, 0)),
            pl.BlockSpec((dm, tn), lambda i, j: (0, j)),
        ],
        out_specs=pl.BlockSpec((tm, tn), lambda i, j: (i, j)),
        out_shape=jax.ShapeDtypeStruct((n, d), F32),
        scratch_shapes=[pltpu.VMEM((tm, dm), BF16)],
        compiler_params=_params(("parallel", "arbitrary")),
        name="out_proj",
    )(x, o_nsa, o_sb, o_ssd, g_nsa.reshape(1, -1), g_sb.reshape(1, -1), w)


def _rel_bucket(dist):
    dist = jnp.maximum(dist, 0)
    max_exact = REL_BUCKETS // 2
    log_ratio = jnp.log(jnp.maximum(dist, 1).astype(F32) / max_exact) / math.log(REL_MAX_DIST / max_exact)
    large = jnp.minimum(max_exact + (log_ratio * (REL_BUCKETS - max_exact)).astype(jnp.int32), REL_BUCKETS - 1)
    return jnp.where(dist < max_exact, dist, large)


def _bias_tables(rel_bias, seq):
    assert QB >= REL_MAX_DIST
    def lookup(dist):
        onehot = jax.nn.one_hot(_rel_bucket(dist), REL_BUCKETS, dtype=F32)
        return jnp.einsum('...k,kh->h...', onehot, rel_bias, precision=lax.Precision.HIGHEST)

    t = jnp.arange(seq)[:, None]
    cend = jnp.arange(QB)[None, :] * CMP_STRIDE + CMP_BLOCK - 1
    n_cmp = (seq - CMP_BLOCK) // CMP_STRIDE + 1
    valid_c = (t >= cend) & (jnp.arange(QB)[None, :] < n_cmp)
    bias_c = jnp.where(valid_c, lookup(t - cend), NEG_INF)
    steps = seq // CMP_TQ
    bias_c = bias_c.reshape(NSA_KV_HEADS, NSA_GQA, steps, CMP_TQ, QB).transpose(0, 2, 1, 3, 4)
    bias_c = bias_c.reshape(NSA_KV_HEADS * steps, NSA_GQA * CMP_TQ, QB)
    r = jnp.arange(QB)[:, None]
    m = jnp.arange(QB)[None, :]
    near = jnp.stack([lookup(r - m), lookup(QB + r - m)])
    near = near - rel_bias[REL_BUCKETS - 1][None, :, None, None]
    diag = near[0] + jnp.where(m > r, NEG_INF, 0.0)
    prev = near[1]
    rows_a = jnp.concatenate([prev, diag, jnp.full_like(diag, NEG_INF)], axis=2)
    rows_b = jnp.concatenate([jnp.zeros_like(diag), prev, diag], axis=2)
    near2 = jnp.concatenate([rows_a, rows_b], axis=1)
    return bias_c, near2.reshape(NSA_KV_HEADS, NSA_GQA * SW_TQ, 3 * QB)


def _expand_table(seq):
    j = jnp.arange(QB)[:, None]
    s = jnp.arange(seq)[None, :]
    return (s // SEL_BLOCK == j).astype(BF16)


def _overlap_table(seq):
    n_cmp = (seq - CMP_BLOCK) // CMP_STRIDE + 1
    cs = jnp.arange(QB)[:, None] * CMP_STRIDE
    ce = cs + CMP_BLOCK - 1
    ss = jnp.arange(QB)[None, :] * SEL_BLOCK
    ov = jnp.maximum(jnp.minimum(ce, ss + SEL_BLOCK - 1) - jnp.maximum(cs, ss) + 1, 0).astype(F32) / CMP_BLOCK
    keep = (jnp.arange(QB)[:, None] < n_cmp) & (jnp.arange(QB)[None, :] < seq // SEL_BLOCK)
    return jnp.where(keep, ov, 0.0).T.astype(BF16)


def _in_proj_weights(w):
    scale = HEAD_DIM ** -0.5
    o_q, o_kv, o_gate = 0, NSA_WIDTH, NSA_WIDTH + 768
    o_sb = o_gate + 3 * NSA_HEADS
    o_z = o_sb + 3 * SB_WIDTH
    o_xbc = o_z + SSD_INNER
    o_dt = o_xbc + SSD_INNER + 2 * SSD_GROUPS * SSD_STATE
    col = lambda a, n: w[:, a:a + n]
    kv = lambda br, which, h: col(o_kv + br * 256 + which * 128 + h * HEAD_DIM, HEAD_DIM)
    zeros = lambda n: jnp.zeros((w.shape[0], n), w.dtype)
    branch = lambda br: [kv(br, 0, 0), kv(br, 1, 0), kv(br, 0, 1), kv(br, 1, 1)]
    q_heads = []
    for h in range(NSA_HEADS):
        q_heads += [col(o_q + h * HEAD_DIM, HEAD_DIM) * scale, zeros(128 - HEAD_DIM)]
    wb = jnp.concatenate(
        q_heads + branch(1) + branch(2)
        + [col(o_sb, SB_WIDTH) * scale, col(o_sb + SB_WIDTH, 2 * SB_WIDTH)], axis=1)
    gates = 3 * NSA_GQA
    wf = jnp.concatenate(
        [col(o_z, SSD_INNER), col(o_xbc, SSD_INNER + 2 * SSD_GROUPS * SSD_STATE), col(o_kv, 256),
         col(o_gate, gates), zeros(MISC_DT - gates), col(o_dt, SSD_HEADS), zeros(128 - MISC_DT - SSD_HEADS),
         col(o_gate + gates, gates), zeros(128 - gates)], axis=1)
    assert wb.shape[1] == PB_COLS and wf.shape[1] == PF_COLS
    return wb.astype(BF16), wf.astype(BF16)


def _head_params(dt_bias, a_log):
    rows = jnp.stack([dt_bias, a_log]).astype(F32)
    return jnp.zeros((8, 128), F32).at[0:2, MISC_DT:MISC_DT + SSD_HEADS].set(rows)


def _mixer(x, l, tables, batch, seq, mix_norm, w_in, w_out, cmp_pos, cmp_w, nsa_norm, sb_norm,
           conv_w, conv_b, dt_bias, a_log, d_skip, ssd_norm, cast_sb, cast_ssd):
    bias_c, near, overlap, expand = tables
    wb, wf = _in_proj_weights(w_in[l])
    pb = _rms_matmul(x, mix_norm[l], wb, BF16, name="in_proj_bf16")
    pf = _rms_matmul(x, mix_norm[l], wf, F32, name="in_proj_f32")

    ng = seq // CMP_STRIDE
    ucmp = pf[:, PF_CMP:PF_CMP + 256].reshape(batch, ng, CMP_STRIDE, 4, HEAD_DIM)
    ucmp = ucmp.transpose(0, 3, 1, 2, 4).reshape(batch, 4, ng, CMP_STRIDE * HEAD_DIM)
    pos2 = cmp_pos[l].reshape(2, 2, CMP_STRIDE * HEAD_DIM)
    o_cmp, sel = _nsa_cmp(pb, ucmp, pos2, cmp_w[l].astype(BF16), bias_c, overlap, batch, seq)
    o_nsa = _nsa_sel_win(pb, pf, o_cmp, sel, expand, near, batch, seq)
    o_sb, w_sb = _sb_attention(pb, batch, seq, *cast_sb)
    o_ssd, w_ssd = _ssd(pf, conv_w[l], conv_b[l], _head_params(dt_bias[l], a_log[l]), d_skip[l], ssd_norm[l],
                        batch, seq, *cast_ssd)
    out = _out_proj(x, o_nsa, o_sb, o_ssd, nsa_norm[l], sb_norm[l], w_out[l].astype(BF16))
    return out, w_sb, w_ssd


def kernel(x, rel_bias, ffn1_norm, ffn1_w_gate, ffn1_w_up, ffn1_w_down, mix_norm, w_in, w_out, nsa_cmp_pos, nsa_cmp_w, nsa_out_norm, sb_out_norm, ssd_conv_w, ssd_conv_b, ssd_dt_bias, ssd_a_log, ssd_d, ssd_out_norm, ffn2_norm, ffn2_w_gate, ffn2_w_up, ffn2_w_down, final_norm):
    batch, seq, d = x.shape
    depth = w_in.shape[0]
    dff = ffn1_w_gate.shape[2]
    tables = _bias_tables(rel_bias, seq) + (_overlap_table(seq), _expand_table(seq))
    h = x.reshape(batch * seq, d)

    narrow = lambda w: w.reshape(depth, 2 * dff, d // 2)
    ffn1 = (ffn1_w_gate, ffn1_w_up, narrow(ffn1_w_down))
    ffn2 = (ffn2_w_gate, ffn2_w_up, narrow(ffn2_w_down))
    widen = lambda ws: (ws[0][None], ws[1][None], ws[2].reshape(1, dff, d))
    w1 = tuple(w[0:1].astype(BF16) for w in (ffn1_w_gate, ffn1_w_up, ffn1_w_down))
    for l in range(depth):
        h = _ffn(h, ffn1_norm[l], *w1, 0, final_norm, final=False)
        cast_sb = (ffn1, l + 1) if l + 1 < depth else ((), 0)
        h, w_sb, w_ssd = _mixer(h, l, tables, batch, seq, mix_norm, w_in, w_out, nsa_cmp_pos, nsa_cmp_w,
                                nsa_out_norm, sb_out_norm, ssd_conv_w, ssd_conv_b, ssd_dt_bias, ssd_a_log,
                                ssd_d, ssd_out_norm, cast_sb, (ffn2, l))
        h = _ffn(h, ffn2_norm[l], *widen(w_ssd), 0, final_norm, final=(l == depth - 1))
        if l + 1 < depth:
            w1 = widen(w_sb)
    return h.reshape(batch, seq, d)
```

```python
import functools
import math

import jax
import jax.numpy as jnp
from jax import lax
from jax.experimental import pallas as pl
from jax.experimental.pallas import tpu as pltpu

D_MODEL = 2048
D_FF = 5632
HEAD_DIM = 64
QB = 128
NEG_INF = -1e30
EPS = 1e-6

NSA_HEADS = 8
NSA_KV_HEADS = 2
NSA_GQA = NSA_HEADS // NSA_KV_HEADS
NSA_WIDTH = NSA_HEADS * HEAD_DIM
CMP_BLOCK = 32
CMP_STRIDE = 16
SEL_BLOCK = 64
SEL_TOPK = 8
FORCE_SCORE = 1e3
WINDOW = 512
SB_HEADS = 8
SB_WIDTH = SB_HEADS * HEAD_DIM
SSD_HEADS = 16
SSD_INNER = SSD_HEADS * HEAD_DIM
SSD_GROUPS = 2
SSD_HPG = SSD_HEADS // SSD_GROUPS
SSD_STATE = 128
SSD_CONV = 4
REL_BUCKETS = 32
REL_MAX_DIST = 128

PB_NSA_Q, PB_SEL, PB_WIN, PB_SB_Q, PB_SB_K, PB_SB_V, PB_COLS = 0, 1024, 1280, 1536, 2048, 2560, 3072
PF_Z, PF_XS, PF_BC, PF_CMP, PF_MISC, PF_COLS = 0, 1024, 2048, 2560, 2816, 3072
MISC_DT = 16

VMEM_LIMIT = 52 * 1024 * 1024

BF16 = jnp.bfloat16
F32 = jnp.float32


def _dot(a, b):
    return jnp.dot(a, b, preferred_element_type=F32)


def _dot_nt(a, b):
    return lax.dot_general(a, b, (((1,), (1,)), ((), ())), preferred_element_type=F32)


def _split2_dot(x, m):
    hi = x.astype(BF16)
    lo = (x - hi.astype(F32)).astype(BF16)
    return _dot(hi, m) + _dot(lo, m)


def _split3_dot(x, m):
    hi = x.astype(BF16)
    r1 = x - hi.astype(F32)
    mid = r1.astype(BF16)
    lo = (r1 - mid.astype(F32)).astype(BF16)
    return _dot(hi, m) + _dot(mid, m) + _dot(lo, m)


def _split3_dot_left(m, x):
    hi = x.astype(BF16)
    r1 = x - hi.astype(F32)
    mid = r1.astype(BF16)
    lo = (r1 - mid.astype(F32)).astype(BF16)
    return _dot(m, hi) + _dot(m, mid) + _dot(m, lo)


def _rms(x, g):
    return x * lax.rsqrt(jnp.mean(x * x, axis=-1, keepdims=True) + EPS) * g


def _silu(x):
    return x / (1.0 + jnp.exp(-x))


def _sigmoid(x):
    return 1.0 / (1.0 + jnp.exp(-x))


def _params(sem):
    return pltpu.CompilerParams(dimension_semantics=sem, vmem_limit_bytes=VMEM_LIMIT)


def _cast_specs(weights, layer, steps, step_of):
    in_specs, out_specs, out_shapes = [], [], []
    for w, at in zip(weights, layer):
        _, rows, cols = w.shape
        share = 1
        while (rows * share) % (16 * steps):
            share *= 2
        slab = rows * share // steps
        in_specs.append(pl.BlockSpec((None, slab, cols),
                                     lambda *ids, k=share, at=at: (at, step_of(*ids) // k, 0)))
        out_specs.append(pl.BlockSpec((slab, cols), lambda *ids, k=share: (step_of(*ids) // k, 0)))
        out_shapes.append(jax.ShapeDtypeStruct((rows, cols), BF16))
    return in_specs, out_specs, out_shapes


def _cast_slabs(srcs, dsts):
    for src, dst in zip(srcs, dsts):
        dst[...] = src[...].astype(BF16)


def _ffn_kernel(x_ref, g_ref, wg_ref, wu_ref, wd_ref, fg_ref, o_ref, h_ref, acc_ref, *, final):
    f = pl.program_id(1)

    @pl.when(f == 0)
    def _():
        h_ref[...] = _rms(x_ref[...], g_ref[...]).astype(BF16)
        acc_ref[...] = jnp.zeros_like(acc_ref)

    h = h_ref[...]
    gate = _dot(h, wg_ref[...])
    up = _dot(h, wu_ref[...])
    acc_ref[...] += _dot((_silu(gate) * up).astype(BF16), wd_ref[...])

    @pl.when(f == pl.num_programs(1) - 1)
    def _():
        y = x_ref[...] + 0.5 * acc_ref[...]
        if final:
            y = _rms(y, fg_ref[...])
        o_ref[...] = y


def _ffn(x, g, wg, wu, wd, layer, final_g, *, final, tm=512, tf=512):
    n, d = x.shape
    dff = wg.shape[2]
    return pl.pallas_call(
        functools.partial(_ffn_kernel, final=final),
        grid=(n // tm, dff // tf),
        in_specs=[
            pl.BlockSpec((tm, d), lambda i, f: (i, 0)),
            pl.BlockSpec((1, d), lambda i, f: (0, 0)),
            pl.BlockSpec((None, d, tf), lambda i, f: (layer, 0, f)),
            pl.BlockSpec((None, d, tf), lambda i, f: (layer, 0, f)),
            pl.BlockSpec((None, tf, d), lambda i, f: (layer, f, 0)),
            pl.BlockSpec((1, d), lambda i, f: (0, 0)),
        ],
        out_specs=pl.BlockSpec((tm, d), lambda i, f: (i, 0)),
        out_shape=jax.ShapeDtypeStruct((n, d), F32),
        scratch_shapes=[pltpu.VMEM((tm, d), BF16), pltpu.VMEM((tm, d), F32)],
        compiler_params=_params(("parallel", "arbitrary")),
        name="ffn",
    )(x, g.reshape(1, d), wg, wu, wd, final_g.reshape(1, d))


def _rms_matmul_kernel(x_ref, g_ref, w_ref, o_ref, h_ref):
    @pl.when(pl.program_id(1) == 0)
    def _():
        h_ref[...] = _rms(x_ref[...], g_ref[...]).astype(BF16)

    o_ref[...] = _dot(h_ref[...], w_ref[...]).astype(o_ref.dtype)


def _rms_matmul(x, g, w, out_dtype, *, tm=1024, tn=1024, name):
    n, d = x.shape
    c = w.shape[1]
    return pl.pallas_call(
        _rms_matmul_kernel,
        grid=(n // tm, c // tn),
        in_specs=[
            pl.BlockSpec((tm, d), lambda i, j: (i, 0)),
            pl.BlockSpec((1, d), lambda i, j: (0, 0)),
            pl.BlockSpec((d, tn), lambda i, j: (0, j)),
        ],
        out_specs=pl.BlockSpec((tm, tn), lambda i, j: (i, j)),
        out_shape=jax.ShapeDtypeStruct((n, c), out_dtype),
        scratch_shapes=[pltpu.VMEM((tm, d), BF16)],
        compiler_params=_params(("parallel", "arbitrary")),
        name=name,
    )(x, g.reshape(1, d), w)


SB_GROUP = 8
SB_DEAD = 104.0


def _sb_kernel(*refs, n_cast):
    q_ref, k_ref, v_ref = refs[:3]
    o_ref = refs[3 + n_cast]
    kbd_ref, vbd_ref = refs[4 + 2 * n_cast:]
    _cast_slabs(refs[3:3 + n_cast], refs[4 + n_cast:4 + 2 * n_cast])
    _sb_body(q_ref, k_ref, v_ref, o_ref, kbd_ref, vbd_ref)


def _sb_body(q_ref, k_ref, v_ref, o_ref, kbd_ref, vbd_ref):
    i = pl.program_id(2)
    pairs = SB_GROUP // 2
    nkb = k_ref.shape[0] // QB
    r = lax.broadcasted_iota(jnp.int32, (QB, QB), 0)
    c = lax.broadcasted_iota(jnp.int32, (QB, QB), 1)
    strict_all = jnp.concatenate([c < r] * SB_GROUP, axis=0)
    later_ones = jnp.concatenate([(r > c).astype(BF16), jnp.ones((QB, QB), BF16)], axis=1)
    later_ones = jnp.concatenate([later_ones, later_ones], axis=0)

    @pl.when(i == 0)
    def _():
        low = c < HEAD_DIM
        zero16 = jnp.zeros((QB, QB), BF16)

        def build(j, carry):
            src = pl.ds(pl.multiple_of(j * QB, QB), QB)
            dst = pl.ds(pl.multiple_of(j * 2 * QB, 2 * QB), 2 * QB)
            for pp in range(pairs):
                for ref, out in ((k_ref, kbd_ref), (v_ref, vbd_ref)):
                    x = ref[src, pp * 128:(pp + 1) * 128]
                    out[pp, dst, :] = jnp.concatenate([jnp.where(low, x, zero16), jnp.where(low, zero16, x)],
                                                      axis=0)
            return carry

        lax.fori_loop(0, nkb, build, 0)

    def scores(j, diagonal):
        rows = pl.ds(pl.multiple_of(j * 2 * QB, 2 * QB), 2 * QB)
        zs = []
        for pp in range(pairs):
            z_pair = _dot_nt(q_ref[:, pp * 128:(pp + 1) * 128], kbd_ref[pp, rows, :])
            zs += [z_pair[:, 0:QB], z_pair[:, QB:2 * QB]]
        z = jnp.concatenate(zs, axis=0)
        soft = jnp.maximum(z, 0.0) + jnp.log(1.0 + jnp.exp(-jnp.abs(z)))
        if diagonal:
            soft = jnp.where(strict_all, soft, 0.0)
        hi = soft.astype(BF16)
        lo = (soft - hi.astype(F32)).astype(BF16)
        sums = _dot(jnp.concatenate([hi, lo], axis=1), later_ones)
        return rows, z - soft, sums

    def absorb(tails, accs, rows, log_beta, sums, diagonal):
        a = jnp.exp(log_beta - (tails + sums[:, 0:QB]))
        if diagonal:
            a = jnp.where(strict_all, a, 0.0)
        a = a.astype(BF16)
        outs = []
        for pp in range(pairs):
            pair = jnp.concatenate([a[2 * pp * QB:(2 * pp + 1) * QB], a[(2 * pp + 1) * QB:(2 * pp + 2) * QB]],
                                   axis=1)
            outs.append(_dot(pair, vbd_ref[pp, rows, :]))
        return tails + sums[:, QB:2 * QB], accs + jnp.concatenate(outs, axis=0)

    def sweep(js, carry):
        staged = [scores(j, False) for j in js]
        for st in staged:
            carry = absorb(*carry, *st, False)
        return carry

    carry = (jnp.zeros((SB_GROUP * QB, QB), F32), jnp.zeros((pairs * QB, 128), F32))
    carry = absorb(*carry, *scores(i, True), True)
    odd = i % 2
    carry = lax.fori_loop(0, odd, lambda step, cr: sweep([i - 1], cr), carry)
    top = i - 1 - odd

    def live(state):
        step, smallest, _, _ = state
        return (step < i // 2) & (smallest < SB_DEAD)

    def pair_step(state):
        step, _, tails, accs = state
        tails, accs = sweep([top - 2 * step, top - 2 * step - 1], (tails, accs))
        return step + 1, jnp.min(tails), tails, accs

    accs = lax.while_loop(live, pair_step, (0, jnp.min(carry[0]), *carry))[3]
    for pp in range(pairs):
        o_ref[:, pp * 128:(pp + 1) * 128] = accs[pp * QB:(pp + 1) * QB]


def _sb_attention(pb, batch, seq, cast=(), cast_layer=()):
    nqb = seq // QB
    w = SB_GROUP * HEAD_DIM
    assert SB_HEADS == SB_GROUP
    qc, kc, vc = PB_SB_Q // w, PB_SB_K // w, PB_SB_V // w
    c_in, c_out, c_shapes = _cast_specs(cast, cast_layer, batch * nqb, lambda b, p, i: b * nqb + i)
    out = pl.pallas_call(
        functools.partial(_sb_kernel, n_cast=len(cast)),
        grid=(batch, SB_HEADS // SB_GROUP, nqb),
        in_specs=[
            pl.BlockSpec((QB, w), lambda b, p, i: (b * nqb + i, qc + p)),
            pl.BlockSpec((seq, w), lambda b, p, i: (b, kc + p)),
            pl.BlockSpec((seq, w), lambda b, p, i: (b, vc + p)),
        ] + c_in,
        out_specs=[pl.BlockSpec((QB, w), lambda b, p, i: (b * nqb + i, p))] + c_out,
        out_shape=[jax.ShapeDtypeStruct((batch * seq, SB_WIDTH), F32)] + c_shapes,
        scratch_shapes=[pltpu.VMEM((SB_GROUP // 2, 2 * seq, 128), BF16),
                        pltpu.VMEM((SB_GROUP // 2, 2 * seq, 128), BF16)],
        compiler_params=_params(("parallel", "parallel", "arbitrary")),
        name="sb_attention",
    )(pb, pb, pb, *cast)
    return out[0], out[1:]


CMP_TQ = 4 * QB


def _nsa_cmp_kernel(q_ref, uk_ref, uv_ref, pos_ref, w_ref, bias_ref, ovt_ref, ocmp_ref, sel_ref,
                    kc_ref, vc_ref, *, n_cmp, n_sel):
    i = pl.program_id(2)
    half = CMP_STRIDE * HEAD_DIM

    @pl.when(i == 0)
    def _():
        def compress(kv, u_ref):
            u = u_ref[0, 0]
            top = _dot((u + pos_ref[kv, 0:1, :]).astype(BF16), w_ref[kv, 0:half, :])
            bot = _dot((u + pos_ref[kv, 1:2, :]).astype(BF16), w_ref[kv, half:2 * half, :])
            return top + pltpu.roll(bot, QB - 1, 0)

        zeros = jnp.zeros((QB, HEAD_DIM), F32)
        kc_ref[...] = jnp.concatenate([compress(0, uk_ref), zeros], axis=1).astype(BF16)
        v = compress(1, uv_ref)
        vc_ref[...] = jnp.concatenate([jnp.concatenate([v, zeros], axis=1),
                                       jnp.concatenate([zeros, v], axis=1)], axis=0).astype(BF16)

    G = NSA_GQA
    TQ = CMP_TQ
    q4 = jnp.concatenate([q_ref[:, g * 128:(g + 1) * 128] for g in range(G)], axis=0)
    s = _dot_nt(q4, kc_ref[...]) + bias_ref[0]
    e = jnp.exp(s - jnp.max(s, axis=-1, keepdims=True))
    t_rows = i * TQ + lax.broadcasted_iota(jnp.int32, (TQ, 1), 0)
    any_valid = jnp.concatenate([t_rows >= CMP_BLOCK - 1] * G, axis=0)
    p = jnp.where(any_valid, e / jnp.sum(e, axis=-1, keepdims=True), 0.0)
    p16 = p.astype(BF16)
    for pp in range(G // 2):
        pair = jnp.concatenate([p16[2 * pp * TQ:(2 * pp + 1) * TQ], p16[(2 * pp + 1) * TQ:(2 * pp + 2) * TQ]],
                               axis=1)
        ocmp_ref[:, pp * 128:(pp + 1) * 128] = _dot(pair, vc_ref[...])
    p_all = p[0:TQ]
    for g in range(1, G):
        p_all = p_all + p[g * TQ:(g + 1) * TQ]

    blk = lax.broadcasted_iota(jnp.int32, (n_sel, QB), 0)
    for part in range(TQ // QB):
        rs = slice(part * QB, (part + 1) * QB)
        t0 = (i * (TQ // QB) + part) * QB
        p_sum = p_all[rs]
        hi = p_sum.astype(BF16)
        lo = (p_sum - hi.astype(F32)).astype(BF16)
        p_sel = (_dot_nt(ovt_ref[...], hi) + _dot_nt(ovt_ref[...], lo))[0:n_sel]
        t = t0 + lax.broadcasted_iota(jnp.int32, (n_sel, QB), 1)
        cur = t // SEL_BLOCK
        eligible = blk * SEL_BLOCK <= t
        forced = (blk == 0) | (blk == cur) | (blk == cur - 1)
        score = jnp.where(eligible, p_sel + jnp.where(forced, FORCE_SCORE, 0.0), NEG_INF)
        rank = jnp.zeros((n_sel, QB), F32)
        for j in range(n_sel):
            other = score[j:j + 1, :]
            ahead = (other > score) | ((other == score) & (blk > j))
            rank = rank + jnp.where(ahead, 1.0, 0.0)
        chosen = jnp.where(eligible & (rank < SEL_TOPK), 1.0, 0.0)
        chosen = jnp.concatenate([chosen, jnp.zeros((QB - n_sel, QB), F32)], axis=0)
        sel_ref[0, 0, rs, :] = chosen.T.astype(BF16)


def _nsa_cmp(pb, ucmp, pos2, cmp_w, bias_c, overlap_t, batch, seq):
    nqb = seq // CMP_TQ
    n_cmp = (seq - CMP_BLOCK) // CMP_STRIDE + 1
    n_sel = seq // SEL_BLOCK
    ng = seq // CMP_STRIDE
    assert ng == QB and n_sel <= QB and n_sel % 8 == 0
    wide = CMP_STRIDE * HEAD_DIM
    return pl.pallas_call(
        functools.partial(_nsa_cmp_kernel, n_cmp=n_cmp, n_sel=n_sel),
        grid=(batch, NSA_KV_HEADS, nqb),
        in_specs=[
            pl.BlockSpec((CMP_TQ, NSA_GQA * 128), lambda b, h, i: (b * nqb + i, h)),
            pl.BlockSpec((1, 1, ng, wide), lambda b, h, i: (b, h, 0, 0)),
            pl.BlockSpec((1, 1, ng, wide), lambda b, h, i: (b, NSA_KV_HEADS + h, 0, 0)),
            pl.BlockSpec((2, 2, wide), lambda b, h, i: (0, 0, 0)),
            pl.BlockSpec((2, 2 * wide, HEAD_DIM), lambda b, h, i: (0, 0, 0)),
            pl.BlockSpec((1, NSA_GQA * CMP_TQ, QB), lambda b, h, i: (h * nqb + i, 0, 0)),
            pl.BlockSpec((QB, QB), lambda b, h, i: (0, 0)),
        ],
        out_specs=[
            pl.BlockSpec((CMP_TQ, 256), lambda b, h, i: (b * nqb + i, h)),
            pl.BlockSpec((1, 1, CMP_TQ, QB), lambda b, h, i: (b, h, i, 0)),
        ],
        out_shape=[
            jax.ShapeDtypeStruct((batch * seq, NSA_WIDTH), F32),
            jax.ShapeDtypeStruct((batch, NSA_KV_HEADS, seq, QB), BF16),
        ],
        scratch_shapes=[pltpu.VMEM((QB, 128), BF16), pltpu.VMEM((2 * QB, 128), BF16)],
        compiler_params=_params(("parallel", "parallel", "arbitrary")),
        name="nsa_compressed",
    )(pb, ucmp, ucmp, pos2, cmp_w, bias_c, overlap_t)


SW_TQ = 2 * QB
SEL_CHUNK = 4 * QB
WIN_FAR = WINDOW - QB


def _nsa_sw_kernel(*refs, n_cast):
    n_in = 8
    _cast_slabs(refs[n_in:n_in + n_cast], refs[n_in + n_cast + 1:n_in + 2 * n_cast + 1])
    _nsa_sw_body(*refs[:n_in], refs[n_in + n_cast], *refs[n_in + 2 * n_cast + 1:])


def _nsa_sw_body(q_ref, ks_ref, kw_ref, ocmp_ref, misc_ref, sel_ref, exp_ref, tb_ref, o_ref, madd_ref):
    i = pl.program_id(2)
    G = NSA_GQA
    TQ = SW_TQ
    q4 = jnp.concatenate([q_ref[:, g * 128:(g + 1) * 128] for g in range(G)], axis=0)
    madd_ref[...] = (_dot(sel_ref[0, 0], exp_ref[...]) - 1.0) * (-NEG_INF)
    t_pos = i * TQ + lax.broadcasted_iota(jnp.int32, (TQ, 1), 0)

    low = lax.broadcasted_iota(jnp.int32, (QB, 128), 1) < HEAD_DIM

    def tile(x, n=G):
        return jnp.concatenate([x] * n, axis=0)

    def ones_v(kv):
        return jnp.where(tile(low, kv.shape[0] // QB), jnp.ones_like(kv), kv)

    first = 2 * i - 1
    rows_m = pl.ds(pl.multiple_of(jnp.maximum(first, 0) * QB, QB), QB)
    rows_a = pl.ds(pl.multiple_of(2 * i * QB, QB), QB)
    rows_b = pl.ds(pl.multiple_of((2 * i + 1) * QB, QB), QB)
    gone = jnp.where(i >= 1, 0.0, NEG_INF)
    gone_m = jnp.concatenate([jnp.full((TQ, QB), gone, F32), jnp.zeros((TQ, 2 * QB), F32)], axis=1)
    far_end = first * QB

    kv = jnp.concatenate([ks_ref[rows_m, :], ks_ref[rows_a, :], ks_ref[rows_b, :]], axis=0)
    near_sel = jnp.concatenate([madd_ref[:, rows_m], madd_ref[:, rows_a], madd_ref[:, rows_b]], axis=1)
    s = _dot_nt(q4, kv) + (tb_ref[0] + tile(near_sel + gone_m))
    m = jnp.max(s, axis=-1, keepdims=True)
    acc = _dot(jnp.exp(s - m).astype(BF16), ones_v(kv))

    def sel_far(cidx, state):
        m, acc = state
        cols = pl.ds(pl.multiple_of(cidx * SEL_CHUNK, SEL_CHUNK), SEL_CHUNK)
        k_pos = cidx * SEL_CHUNK + lax.broadcasted_iota(jnp.int32, (TQ, SEL_CHUNK), 1)
        add = jnp.where(k_pos < far_end, madd_ref[:, cols], NEG_INF)
        kv = ks_ref[cols, :]
        s = _dot_nt(q4, kv) + tile(add)
        m_new = jnp.maximum(m, jnp.max(s, axis=-1, keepdims=True))
        return m_new, jnp.exp(m - m_new) * acc + _dot(jnp.exp(s - m_new).astype(BF16), ones_v(kv))

    _, sel_acc = lax.fori_loop(0, (2 * i + 2) // (SEL_CHUNK // QB), sel_far, (m, acc))

    start = jnp.maximum(first - WIN_FAR // QB, 0) * QB
    rows_far = pl.ds(pl.multiple_of(start, QB), WIN_FAR)
    k_pos = start + lax.broadcasted_iota(jnp.int32, (TQ, WIN_FAR), 1)
    live = (k_pos < far_end) & (k_pos > t_pos - WINDOW)
    kv = jnp.concatenate([kw_ref[rows_far, :], kw_ref[rows_m, :], kw_ref[rows_a, :], kw_ref[rows_b, :]], axis=0)
    bias = jnp.concatenate([tile(jnp.where(live, 0.0, NEG_INF)), tb_ref[0] + tile(gone_m)], axis=1)
    s = _dot_nt(q4, kv) + bias
    win_acc = _dot(jnp.exp(s - jnp.max(s, axis=-1, keepdims=True)).astype(BF16), ones_v(kv))

    gate = _sigmoid(misc_ref[...])
    low = tile(low, TQ // QB)

    def pair(acc, pp):
        a, b = acc[2 * pp * TQ:(2 * pp + 1) * TQ], acc[(2 * pp + 1) * TQ:(2 * pp + 2) * TQ]
        return (jnp.where(low, pltpu.roll(a, HEAD_DIM, 1), b)
                / jnp.where(low, a, pltpu.roll(b, HEAD_DIM, 1)))

    def pair_gate(branch, pp):
        ca, cb = 3 * (2 * pp) + branch, 3 * (2 * pp + 1) + branch
        return jnp.where(low, gate[:, ca:ca + 1], gate[:, cb:cb + 1])

    for pp in range(G // 2):
        lanes = slice(pp * 128, (pp + 1) * 128)
        o_ref[:, lanes] = (pair_gate(0, pp) * ocmp_ref[:, lanes] + pair_gate(1, pp) * pair(sel_acc, pp)
                           + pair_gate(2, pp) * pair(win_acc, pp))


def _nsa_sel_win(pb, pf, o_cmp, sel, expand, tb, batch, seq, cast=(), cast_layer=()):
    nqb = seq // SW_TQ
    G = NSA_GQA
    c_in, c_out, c_shapes = _cast_specs(cast, cast_layer, batch * NSA_KV_HEADS * nqb,
                                        lambda b, h, i: (b * NSA_KV_HEADS + h) * nqb + i)
    out = pl.pallas_call(
        functools.partial(_nsa_sw_kernel, n_cast=len(cast)),
        grid=(batch, NSA_KV_HEADS, nqb),
        in_specs=[
            pl.BlockSpec((SW_TQ, G * 128), lambda b, h, i: (b * nqb + i, h)),
            pl.BlockSpec((seq, 128), lambda b, h, i: (b, PB_SEL // 128 + h)),
            pl.BlockSpec((seq, 128), lambda b, h, i: (b, PB_WIN // 128 + h)),
            pl.BlockSpec((SW_TQ, 256), lambda b, h, i: (b * nqb + i, h)),
            pl.BlockSpec((SW_TQ, 128), lambda b, h, i: (b * nqb + i, PF_MISC // 128 + h)),
            pl.BlockSpec((1, 1, SW_TQ, QB), lambda b, h, i: (b, h, i, 0)),
            pl.BlockSpec((QB, seq), lambda b, h, i: (0, 0)),
            pl.BlockSpec((1, G * SW_TQ, 3 * QB), lambda b, h, i: (h, 0, 0)),
        ] + c_in,
        out_specs=[pl.BlockSpec((SW_TQ, 256), lambda b, h, i: (b * nqb + i, h))] + c_out,
        out_shape=[jax.ShapeDtypeStruct((batch * seq, NSA_WIDTH), F32)] + c_shapes,
        scratch_shapes=[pltpu.VMEM((SW_TQ, seq), F32)],
        compiler_params=_params(("parallel", "parallel", "arbitrary")),
        name="nsa_selected_window",
    )(pb, pb, pb, o_cmp, pf, sel, expand, tb, *cast)
    return out[0], out[1:]


def _ssd_kernel(*refs, n_cast):
    n_in = 12
    _cast_slabs(refs[n_in:n_in + n_cast], refs[n_in + n_cast + 1:n_in + 2 * n_cast + 1])
    _ssd_body(*refs[:n_in], refs[n_in + n_cast], *refs[n_in + 2 * n_cast + 1:])


def _ssd_body(z_ref, xs_ref, bc_ref, misc_ref, cwx_ref, cwb_ref, cbx_ref, cbb_ref, hp_ref, spread_ref,
              dskip_ref, ng_ref, o_ref, xbuf, bbuf, state):
    ci = pl.program_id(1)
    L = QB
    P = HEAD_DIM
    GN = SSD_GROUPS * SSD_STATE

    @pl.when(ci == 0)
    def _():
        xbuf[...] = jnp.zeros_like(xbuf)
        bbuf[...] = jnp.zeros_like(bbuf)
        state[...] = jnp.zeros_like(state)

    def conv_silu(buf, src_ref, w_ref, b_ref):
        cur = src_ref[...]
        ext = jnp.concatenate([buf[...], cur], axis=0)
        out = b_ref[...] + w_ref[SSD_CONV - 1:SSD_CONV, :] * cur
        for k in range(1, SSD_CONV):
            out = out + w_ref[SSD_CONV - 1 - k:SSD_CONV - k, :] * pltpu.roll(ext, k, 0)[8:8 + L]
        buf[...] = cur[L - 8:L]
        return _silu(out)

    xs = conv_silu(xbuf, xs_ref, cwx_ref, cbx_ref)
    bcs = conv_silu(bbuf, bc_ref, cwb_ref, cbb_ref)

    misc = misc_ref[...]
    pre = misc + hp_ref[0:1, :]
    dt = jnp.maximum(pre, 0.0) + jnp.log1p(jnp.exp(-jnp.abs(pre)))
    a_dt = dt * (-jnp.exp(hp_ref[1:2, :]))
    r = lax.broadcasted_iota(jnp.int32, (L, L), 0)
    c = lax.broadcasted_iota(jnp.int32, (L, L), 1)
    causal = r >= c
    a_cs = _split3_dot_left(causal.astype(BF16), a_dt)
    a_cs_t = a_cs.T

    dt_full = _split3_dot(dt, spread_ref[...])
    acs_full = _split3_dot(a_cs, spread_ref[...])
    total_full = acs_full[L - 1:L, :]
    xdt = xs * dt_full
    xdt16 = xdt.astype(BF16)
    to_end16 = (xdt * jnp.exp(total_full - acs_full)).astype(BF16)
    decay_in = jnp.exp(acs_full)
    chunk_decay = jnp.exp(total_full)
    low = c < P
    zero16 = jnp.zeros((L, 2 * P), BF16)
    width = SSD_HPG * P

    ys = []
    for g in range(SSD_GROUPS):
        bm = bcs[:, g * SSD_STATE:(g + 1) * SSD_STATE]
        cm16 = bcs[:, GN + g * SSD_STATE:GN + (g + 1) * SSD_STATE].astype(BF16)
        cb = _dot_nt(cm16, bm.astype(BF16))
        lanes = slice(g * width, (g + 1) * width)
        h_in = state[g]
        y_off = _dot(cm16, h_in.astype(BF16)) * decay_in[:, lanes]
        state[g] = h_in * chunk_decay[:, lanes] + _dot(bm.T.astype(BF16), to_end16[:, lanes])
        diag = []
        for pp in range(SSD_HPG // 2):
            hd = g * SSD_HPG + 2 * pp
            decayed = []
            for col in (MISC_DT + hd, MISC_DT + hd + 1):
                seg = jnp.exp(jnp.where(causal, a_cs[:, col:col + 1] - a_cs_t[col:col + 1, :], -jnp.inf))
                decayed.append((cb * seg).astype(BF16))
            xp = xdt16[:, hd * P:(hd + 2) * P]
            x_bd = jnp.concatenate([jnp.where(low, xp, zero16), jnp.where(low, zero16, xp)], axis=0)
            diag.append(_dot(jnp.concatenate(decayed, axis=1), x_bd))
        ys.append(jnp.concatenate(diag, axis=1) + y_off)

    y = (jnp.concatenate(ys, axis=1) + dskip_ref[...] * xs) * _silu(z_ref[...])
    for g in range(SSD_GROUPS):
        lanes = slice(g * width, (g + 1) * width)
        yg = y[:, lanes]
        o_ref[:, lanes] = yg * lax.rsqrt(jnp.mean(yg * yg, axis=-1, keepdims=True) + EPS) * ng_ref[:, lanes]


def _ssd(pf, conv_w, conv_b, head_params, d_skip, norm_g, batch, seq, cast=(), cast_layer=()):
    nc = seq // QB
    c_in, c_out, c_shapes = _cast_specs(cast, cast_layer, batch * nc, lambda b, ci: b * nc + ci)
    GN = SSD_GROUPS * SSD_STATE
    lane_head = jnp.arange(SSD_INNER)[None, :] // HEAD_DIM
    spread = (jnp.arange(128)[:, None] == MISC_DT + lane_head).astype(BF16)
    cwx, cwb = conv_w[:, :SSD_INNER], conv_w[:, SSD_INNER:]
    cbx, cbb = conv_b[:SSD_INNER].reshape(1, -1), conv_b[SSD_INNER:].reshape(1, -1)
    full = lambda shape: pl.BlockSpec(shape, lambda b, ci: (0,) * len(shape))
    out = pl.pallas_call(
        functools.partial(_ssd_kernel, n_cast=len(cast)),
        grid=(batch, nc),
        in_specs=[
            pl.BlockSpec((QB, SSD_INNER), lambda b, ci: (b * nc + ci, PF_Z // SSD_INNER)),
            pl.BlockSpec((QB, SSD_INNER), lambda b, ci: (b * nc + ci, PF_XS // SSD_INNER)),
            pl.BlockSpec((QB, 2 * GN), lambda b, ci: (b * nc + ci, PF_BC // (2 * GN))),
            pl.BlockSpec((QB, 128), lambda b, ci: (b * nc + ci, PF_MISC // 128)),
            full((SSD_CONV, SSD_INNER)),
            full((SSD_CONV, 2 * GN)),
            full((1, SSD_INNER)),
            full((1, 2 * GN)),
            full((8, 128)),
            full((128, SSD_INNER)),
            full((1, SSD_INNER)),
            full((1, SSD_INNER)),
        ] + c_in,
        out_specs=[pl.BlockSpec((QB, SSD_INNER), lambda b, ci: (b * nc + ci, 0))] + c_out,
        out_shape=[jax.ShapeDtypeStruct((batch * seq, SSD_INNER), F32)] + c_shapes,
        scratch_shapes=[
            pltpu.VMEM((8, SSD_INNER), F32),
            pltpu.VMEM((8, 2 * GN), F32),
            pltpu.VMEM((SSD_GROUPS, SSD_STATE, SSD_HPG * HEAD_DIM), F32),
        ],
        compiler_params=_params(("parallel", "arbitrary")),
        name="ssd",
    )(pf, pf, pf, pf, cwx, cwb, cbx, cbb, head_params, spread,
      jnp.repeat(d_skip.astype(F32), HEAD_DIM).reshape(1, -1), norm_g.reshape(1, -1), *cast)
    return out[0], out[1:]


def _out_proj_kernel(x_ref, nsa_ref, sb_ref, ssd_ref, gn_ref, gs_ref, w_ref, o_ref, mix_ref):
    @pl.when(pl.program_id(1) == 0)
    def _():
        mix_ref[:, 0:NSA_WIDTH] = _rms(nsa_ref[...], gn_ref[...]).astype(BF16)
        mix_ref[:, NSA_WIDTH:NSA_WIDTH + SB_WIDTH] = _rms(sb_ref[...], gs_ref[...]).astype(BF16)
        mix_ref[:, NSA_WIDTH + SB_WIDTH:] = ssd_ref[...].astype(BF16)

    o_ref[...] = x_ref[...] + _dot(mix_ref[...], w_ref[...])


def _out_proj(x, o_nsa, o_sb, o_ssd, g_nsa, g_sb, w, *, tm=1024, tn=1024):
    n, d = x.shape
    dm = w.shape[0]
    return pl.pallas_call(
        _out_proj_kernel,
        grid=(n // tm, d // tn),
        in_specs=[
            pl.BlockSpec((tm, tn), lambda i, j: (i, j)),
            pl.BlockSpec((tm, NSA_WIDTH), lambda i, j: (i, 0)),
            pl.BlockSpec((tm, SB_WIDTH), lambda i, j: (i, 0)),
            pl.BlockSpec((tm, SSD_INNER), lambda i, j: (i, 0)),
            pl.BlockSpec((1, NSA_WIDTH), lambda i, j: (0, 0)),
            pl.BlockSpec((1, SB_WIDTH), lambda i, j: (0, 0)),
            pl.BlockSpec((dm, tn), lambda i, j: (0, j)),
        ],
        out_specs=pl.BlockSpec((tm, tn), lambda i, j: (i, j)),
        out_shape=jax.ShapeDtypeStruct((n, d), F32),
        scratch_shapes=[pltpu.VMEM((tm, dm), BF16)],
        compiler_params=_params(("parallel", "arbitrary")),
        name="out_proj",
    )(x, o_nsa, o_sb, o_ssd, g_nsa.reshape(1, -1), g_sb.reshape(1, -1), w)


def _rel_bucket(dist):
    dist = jnp.maximum(dist, 0)
    max_exact = REL_BUCKETS // 2
    log_ratio = jnp.log(jnp.maximum(dist, 1).astype(F32) / max_exact) / math.log(REL_MAX_DIST / max_exact)
    large = jnp.minimum(max_exact + (log_ratio * (REL_BUCKETS - max_exact)).astype(jnp.int32), REL_BUCKETS - 1)
    return jnp.where(dist < max_exact, dist, large)


def _bias_tables(rel_bias, seq):
    assert QB >= REL_MAX_DIST
    def lookup(dist):
        onehot = jax.nn.one_hot(_rel_bucket(dist), REL_BUCKETS, dtype=F32)
        return jnp.einsum('...k,kh->h...', onehot, rel_bias, precision=lax.Precision.HIGHEST)

    t = jnp.arange(seq)[:, None]
    cend = jnp.arange(QB)[None, :] * CMP_STRIDE + CMP_BLOCK - 1
    n_cmp = (seq - CMP_BLOCK) // CMP_STRIDE + 1
    valid_c = (t >= cend) & (jnp.arange(QB)[None, :] < n_cmp)
    bias_c = jnp.where(valid_c, lookup(t - cend), NEG_INF)
    steps = seq // CMP_TQ
    bias_c = bias_c.reshape(NSA_KV_HEADS, NSA_GQA, steps, CMP_TQ, QB).transpose(0, 2, 1, 3, 4)
    bias_c = bias_c.reshape(NSA_KV_HEADS * steps, NSA_GQA * CMP_TQ, QB)
    r = jnp.arange(QB)[:, None]
    m = jnp.arange(QB)[None, :]
    near = jnp.stack([lookup(r - m), lookup(QB + r - m)])
    near = near - rel_bias[REL_BUCKETS - 1][None, :, None, None]
    diag = near[0] + jnp.where(m > r, NEG_INF, 0.0)
    prev = near[1]
    rows_a = jnp.concatenate([prev, diag, jnp.full_like(diag, NEG_INF)], axis=2)
    rows_b = jnp.concatenate([jnp.zeros_like(diag), prev, diag], axis=2)
    near2 = jnp.concatenate([rows_a, rows_b], axis=1)
    return bias_c, near2.reshape(NSA_KV_HEADS, NSA_GQA * SW_TQ, 3 * QB)


def _expand_table(seq):
    j = jnp.arange(QB)[:, None]
    s = jnp.arange(seq)[None, :]
    return (s // SEL_BLOCK == j).astype(BF16)


def _overlap_table(seq):
    n_cmp = (seq - CMP_BLOCK) // CMP_STRIDE + 1
    cs = jnp.arange(QB)[:, None] * CMP_STRIDE
    ce = cs + CMP_BLOCK - 1
    ss = jnp.arange(QB)[None, :] * SEL_BLOCK
    ov = jnp.maximum(jnp.minimum(ce, ss + SEL_BLOCK - 1) - jnp.maximum(cs, ss) + 1, 0).astype(F32) / CMP_BLOCK
    keep = (jnp.arange(QB)[:, None] < n_cmp) & (jnp.arange(QB)[None, :] < seq // SEL_BLOCK)
    return jnp.where(keep, ov, 0.0).T.astype(BF16)


def _in_proj_weights(w):
    scale = HEAD_DIM ** -0.5
    o_q, o_kv, o_gate = 0, NSA_WIDTH, NSA_WIDTH + 768
    o_sb = o_gate + 3 * NSA_HEADS
    o_z = o_sb + 3 * SB_WIDTH
    o_xbc = o_z + SSD_INNER
    o_dt = o_xbc + SSD_INNER + 2 * SSD_GROUPS * SSD_STATE
    col = lambda a, n: w[:, a:a + n]
    kv = lambda br, which, h: col(o_kv + br * 256 + which * 128 + h * HEAD_DIM, HEAD_DIM)
    zeros = lambda n: jnp.zeros((w.shape[0], n), w.dtype)
    branch = lambda br: [kv(br, 0, 0), kv(br, 1, 0), kv(br, 0, 1), kv(br, 1, 1)]
    q_heads = []
    for h in range(NSA_HEADS):
        q_heads += [col(o_q + h * HEAD_DIM, HEAD_DIM) * scale, zeros(128 - HEAD_DIM)]
    wb = jnp.concatenate(
        q_heads + branch(1) + branch(2)
        + [col(o_sb, SB_WIDTH) * scale, col(o_sb + SB_WIDTH, 2 * SB_WIDTH)], axis=1)
    gates = 3 * NSA_GQA
    wf = jnp.concatenate(
        [col(o_z, SSD_INNER), col(o_xbc, SSD_INNER + 2 * SSD_GROUPS * SSD_STATE), col(o_kv, 256),
         col(o_gate, gates), zeros(MISC_DT - gates), col(o_dt, SSD_HEADS), zeros(128 - MISC_DT - SSD_HEADS),
         col(o_gate + gates, gates), zeros(128 - gates)], axis=1)
    assert wb.shape[1] == PB_COLS and wf.shape[1] == PF_COLS
    return wb.astype(BF16), wf.astype(BF16)


def _head_params(dt_bias, a_log):
    rows = jnp.stack([dt_bias, a_log]).astype(F32)
    return jnp.zeros((8, 128), F32).at[0:2, MISC_DT:MISC_DT + SSD_HEADS].set(rows)


def _mixer(x, l, tables, batch, seq, mix_norm, w_in, w_out, cmp_pos, cmp_w, nsa_norm, sb_norm,
           conv_w, conv_b, dt_bias, a_log, d_skip, ssd_norm, casts):
    hosts = [([w for w, _ in casts[k::3]], tuple(at for _, at in casts[k::3])) for k in range(3)]
    bias_c, near, overlap, expand = tables
    wb, wf = _in_proj_weights(w_in[l])
    pb = _rms_matmul(x, mix_norm[l], wb, BF16, name="in_proj_bf16")
    pf = _rms_matmul(x, mix_norm[l], wf, F32, name="in_proj_f32")

    ng = seq // CMP_STRIDE
    ucmp = pf[:, PF_CMP:PF_CMP + 256].reshape(batch, ng, CMP_STRIDE, 4, HEAD_DIM)
    ucmp = ucmp.transpose(0, 3, 1, 2, 4).reshape(batch, 4, ng, CMP_STRIDE * HEAD_DIM)
    pos2 = cmp_pos[l].reshape(2, 2, CMP_STRIDE * HEAD_DIM)
    o_cmp, sel = _nsa_cmp(pb, ucmp, pos2, cmp_w[l].astype(BF16), bias_c, overlap, batch, seq)
    o_nsa, w0 = _nsa_sel_win(pb, pf, o_cmp, sel, expand, near, batch, seq, *hosts[0])
    o_sb, w1 = _sb_attention(pb, batch, seq, *hosts[1])
    o_ssd, w2 = _ssd(pf, conv_w[l], conv_b[l], _head_params(dt_bias[l], a_log[l]), d_skip[l], ssd_norm[l],
                     batch, seq, *hosts[2])
    out = _out_proj(x, o_nsa, o_sb, o_ssd, nsa_norm[l], sb_norm[l], w_out[l].astype(BF16))
    done = [None] * len(casts)
    for k, ws in enumerate((w0, w1, w2)):
        done[k::3] = ws
    return out, done


def kernel(x, rel_bias, ffn1_norm, ffn1_w_gate, ffn1_w_up, ffn1_w_down, mix_norm, w_in, w_out, nsa_cmp_pos, nsa_cmp_w, nsa_out_norm, sb_out_norm, ssd_conv_w, ssd_conv_b, ssd_dt_bias, ssd_a_log, ssd_d, ssd_out_norm, ffn2_norm, ffn2_w_gate, ffn2_w_up, ffn2_w_down, final_norm):
    batch, seq, d = x.shape
    depth = w_in.shape[0]
    tables = _bias_tables(rel_bias, seq) + (_overlap_table(seq), _expand_table(seq))
    h = x.reshape(batch * seq, d)

    ffn1 = (ffn1_w_gate, ffn1_w_up, ffn1_w_down)
    ffn2 = (ffn2_w_gate, ffn2_w_up, ffn2_w_down)
    widen = lambda ws: tuple(w[None] for w in ws)
    w1 = tuple(w[0:1].astype(BF16) for w in ffn1)
    for l in range(depth):
        h = _ffn(h, ffn1_norm[l], *w1, 0, final_norm, final=False)
        casts = [(w, l) for w in ffn2] + ([(w, l + 1) for w in ffn1] if l + 1 < depth else [])
        h, done = _mixer(h, l, tables, batch, seq, mix_norm, w_in, w_out, nsa_cmp_pos, nsa_cmp_w,
                         nsa_out_norm, sb_out_norm, ssd_conv_w, ssd_conv_b, ssd_dt_bias, ssd_a_log,
                         ssd_d, ssd_out_norm, casts)
        h = _ffn(h, ffn2_norm[l], *widen(done[0:3]), 0, final_norm, final=(l == depth - 1))
        w1 = widen(done[3:6])
    return h.reshape(batch, seq, d)
```

```python
import functools
import math

import jax
import jax.numpy as jnp
from jax import lax
from jax.experimental import pallas as pl
from jax.experimental.pallas import tpu as pltpu

D_MODEL = 2048
D_FF = 5632
HEAD_DIM = 64
QB = 128
NEG_INF = -1e30
EPS = 1e-6

NSA_HEADS = 8
NSA_KV_HEADS = 2
NSA_GQA = NSA_HEADS // NSA_KV_HEADS
NSA_WIDTH = NSA_HEADS * HEAD_DIM
CMP_BLOCK = 32
CMP_STRIDE = 16
SEL_BLOCK = 64
SEL_TOPK = 8
FORCE_SCORE = 1e3
WINDOW = 512
SB_HEADS = 8
SB_WIDTH = SB_HEADS * HEAD_DIM
SSD_HEADS = 16
SSD_INNER = SSD_HEADS * HEAD_DIM
SSD_GROUPS = 2
SSD_HPG = SSD_HEADS // SSD_GROUPS
SSD_STATE = 128
SSD_CONV = 4
REL_BUCKETS = 32
REL_MAX_DIST = 128

PB_NSA_Q, PB_SEL, PB_WIN, PB_SB_Q, PB_SB_K, PB_SB_V, PB_COLS = 0, 1024, 1280, 1536, 2048, 2560, 3072
PF_Z, PF_XS, PF_BC, PF_CMP, PF_MISC, PF_COLS = 0, 1024, 2048, 2560, 2816, 3072
MISC_DT = 16

VMEM_LIMIT = 52 * 1024 * 1024

BF16 = jnp.bfloat16
F32 = jnp.float32


def _dot(a, b):
    return jnp.dot(a, b, preferred_element_type=F32)


def _dot_nt(a, b):
    return lax.dot_general(a, b, (((1,), (1,)), ((), ())), preferred_element_type=F32)


def _split2_dot(x, m):
    hi = x.astype(BF16)
    lo = (x - hi.astype(F32)).astype(BF16)
    return _dot(hi, m) + _dot(lo, m)


def _split3_dot(x, m):
    hi = x.astype(BF16)
    r1 = x - hi.astype(F32)
    mid = r1.astype(BF16)
    lo = (r1 - mid.astype(F32)).astype(BF16)
    return _dot(hi, m) + _dot(mid, m) + _dot(lo, m)


def _split3_dot_left(m, x):
    hi = x.astype(BF16)
    r1 = x - hi.astype(F32)
    mid = r1.astype(BF16)
    lo = (r1 - mid.astype(F32)).astype(BF16)
    return _dot(m, hi) + _dot(m, mid) + _dot(m, lo)


def _rms(x, g):
    return x * lax.rsqrt(jnp.mean(x * x, axis=-1, keepdims=True) + EPS) * g


def _silu(x):
    return x / (1.0 + jnp.exp(-x))


def _sigmoid(x):
    return 1.0 / (1.0 + jnp.exp(-x))


def _params(sem):
    return pltpu.CompilerParams(dimension_semantics=sem, vmem_limit_bytes=VMEM_LIMIT)


def _cast_specs(weights, layer, steps, step_of):
    in_specs, out_specs, out_shapes = [], [], []
    for w, at in zip(weights, layer):
        _, rows, cols = w.shape
        share = 1
        while (rows * share) % (16 * steps):
            share *= 2
        slab = rows * share // steps
        in_specs.append(pl.BlockSpec((None, slab, cols),
                                     lambda *ids, k=share, at=at: (at, step_of(*ids) // k, 0)))
        out_specs.append(pl.BlockSpec((slab, cols), lambda *ids, k=share: (step_of(*ids) // k, 0)))
        out_shapes.append(jax.ShapeDtypeStruct((rows, cols), BF16))
    return in_specs, out_specs, out_shapes


def _cast_slabs(srcs, dsts):
    for src, dst in zip(srcs, dsts):
        dst[...] = src[...].astype(BF16)


def _cast_kernel(*refs):
    _cast_slabs(refs[:len(refs) // 2], refs[len(refs) // 2:])


def _cast_layer(weights, layer, steps=16):
    c_in, c_out, c_shapes = _cast_specs(weights, (layer,) * len(weights), steps, lambda s: s)
    return pl.pallas_call(_cast_kernel, grid=(steps,), in_specs=c_in, out_specs=c_out, out_shape=c_shapes,
                          compiler_params=_params(("parallel",)), name="cast_weights")(*weights)


def _ffn_kernel(x_ref, g_ref, wg_ref, wu_ref, wd_ref, fg_ref, o_ref, h_ref, acc_ref, *, final):
    f = pl.program_id(1)

    @pl.when(f == 0)
    def _():
        h_ref[...] = _rms(x_ref[...], g_ref[...]).astype(BF16)
        acc_ref[...] = jnp.zeros_like(acc_ref)

    h = h_ref[...]
    gate = _dot(h, wg_ref[...])
    up = _dot(h, wu_ref[...])
    acc_ref[...] += _dot((_silu(gate) * up).astype(BF16), wd_ref[...])

    @pl.when(f == pl.num_programs(1) - 1)
    def _():
        y = x_ref[...] + 0.5 * acc_ref[...]
        if final:
            y = _rms(y, fg_ref[...])
        o_ref[...] = y


def _ffn(x, g, wg, wu, wd, layer, final_g, *, final, tm=512, tf=512):
    n, d = x.shape
    dff = wg.shape[2]
    return pl.pallas_call(
        functools.partial(_ffn_kernel, final=final),
        grid=(n // tm, dff // tf),
        in_specs=[
            pl.BlockSpec((tm, d), lambda i, f: (i, 0)),
            pl.BlockSpec((1, d), lambda i, f: (0, 0)),
            pl.BlockSpec((None, d, tf), lambda i, f: (layer, 0, f)),
            pl.BlockSpec((None, d, tf), lambda i, f: (layer, 0, f)),
            pl.BlockSpec((None, tf, d), lambda i, f: (layer, f, 0)),
            pl.BlockSpec((1, d), lambda i, f: (0, 0)),
        ],
        out_specs=pl.BlockSpec((tm, d), lambda i, f: (i, 0)),
        out_shape=jax.ShapeDtypeStruct((n, d), F32),
        scratch_shapes=[pltpu.VMEM((tm, d), BF16), pltpu.VMEM((tm, d), F32)],
        compiler_params=_params(("parallel", "arbitrary")),
        name="ffn",
    )(x, g.reshape(1, d), wg, wu, wd, final_g.reshape(1, d))


def _rms_matmul_kernel(x_ref, g_ref, w_ref, o_ref, h_ref):
    @pl.when(pl.program_id(1) == 0)
    def _():
        h_ref[...] = _rms(x_ref[...], g_ref[...]).astype(BF16)

    o_ref[...] = _dot(h_ref[...], w_ref[...]).astype(o_ref.dtype)


def _rms_matmul(x, g, w, out_dtype, *, tm=1024, tn=1024, name):
    n, d = x.shape
    c = w.shape[1]
    return pl.pallas_call(
        _rms_matmul_kernel,
        grid=(n // tm, c // tn),
        in_specs=[
            pl.BlockSpec((tm, d), lambda i, j: (i, 0)),
            pl.BlockSpec((1, d), lambda i, j: (0, 0)),
            pl.BlockSpec((d, tn), lambda i, j: (0, j)),
        ],
        out_specs=pl.BlockSpec((tm, tn), lambda i, j: (i, j)),
        out_shape=jax.ShapeDtypeStruct((n, c), out_dtype),
        scratch_shapes=[pltpu.VMEM((tm, d), BF16)],
        compiler_params=_params(("parallel", "arbitrary")),
        name=name,
    )(x, g.reshape(1, d), w)


SB_GROUP = 8
SB_DEAD = 104.0


def _sb_kernel(*refs, n_cast):
    q_ref, k_ref, v_ref = refs[:3]
    o_ref = refs[3 + n_cast]
    kbd_ref, vbd_ref = refs[4 + 2 * n_cast:]
    _cast_slabs(refs[3:3 + n_cast], refs[4 + n_cast:4 + 2 * n_cast])
    _sb_body(q_ref, k_ref, v_ref, o_ref, kbd_ref, vbd_ref)


def _sb_body(q_ref, k_ref, v_ref, o_ref, kbd_ref, vbd_ref):
    i = pl.program_id(2)
    pairs = SB_GROUP // 2
    nkb = k_ref.shape[0] // QB
    r = lax.broadcasted_iota(jnp.int32, (QB, QB), 0)
    c = lax.broadcasted_iota(jnp.int32, (QB, QB), 1)
    strict_all = jnp.concatenate([c < r] * SB_GROUP, axis=0)
    later_ones = jnp.concatenate([(r > c).astype(BF16), jnp.ones((QB, QB), BF16)], axis=1)
    later_ones = jnp.concatenate([later_ones, later_ones], axis=0)

    @pl.when(i == 0)
    def _():
        low = c < HEAD_DIM
        zero16 = jnp.zeros((QB, QB), BF16)

        def build(j, carry):
            src = pl.ds(pl.multiple_of(j * QB, QB), QB)
            dst = pl.ds(pl.multiple_of(j * 2 * QB, 2 * QB), 2 * QB)
            for pp in range(pairs):
                for ref, out in ((k_ref, kbd_ref), (v_ref, vbd_ref)):
                    x = ref[src, pp * 128:(pp + 1) * 128]
                    out[pp, dst, :] = jnp.concatenate([jnp.where(low, x, zero16), jnp.where(low, zero16, x)],
                                                      axis=0)
            return carry

        lax.fori_loop(0, nkb, build, 0)

    def scores(j, diagonal):
        rows = pl.ds(pl.multiple_of(j * 2 * QB, 2 * QB), 2 * QB)
        zs = []
        for pp in range(pairs):
            z_pair = _dot_nt(q_ref[:, pp * 128:(pp + 1) * 128], kbd_ref[pp, rows, :])
            zs += [z_pair[:, 0:QB], z_pair[:, QB:2 * QB]]
        z = jnp.concatenate(zs, axis=0)
        soft = jnp.maximum(z, 0.0) + jnp.log(1.0 + jnp.exp(-jnp.abs(z)))
        if diagonal:
            soft = jnp.where(strict_all, soft, 0.0)
        hi = soft.astype(BF16)
        lo = (soft - hi.astype(F32)).astype(BF16)
        sums = _dot(jnp.concatenate([hi, lo], axis=1), later_ones)
        return rows, z - soft, sums

    def absorb(tails, accs, rows, log_beta, sums, diagonal):
        a = jnp.exp(log_beta - (tails + sums[:, 0:QB]))
        if diagonal:
            a = jnp.where(strict_all, a, 0.0)
        a = a.astype(BF16)
        outs = []
        for pp in range(pairs):
            pair = jnp.concatenate([a[2 * pp * QB:(2 * pp + 1) * QB], a[(2 * pp + 1) * QB:(2 * pp + 2) * QB]],
                                   axis=1)
            outs.append(_dot(pair, vbd_ref[pp, rows, :]))
        return tails + sums[:, QB:2 * QB], accs + jnp.concatenate(outs, axis=0)

    def sweep(js, carry):
        staged = [scores(j, False) for j in js]
        for st in staged:
            carry = absorb(*carry, *st, False)
        return carry

    carry = (jnp.zeros((SB_GROUP * QB, QB), F32), jnp.zeros((pairs * QB, 128), F32))
    carry = absorb(*carry, *scores(i, True), True)
    odd = i % 2
    carry = lax.fori_loop(0, odd, lambda step, cr: sweep([i - 1], cr), carry)
    top = i - 1 - odd

    def live(state):
        step, smallest, _, _ = state
        return (step < i // 2) & (smallest < SB_DEAD)

    def pair_step(state):
        step, _, tails, accs = state
        tails, accs = sweep([top - 2 * step, top - 2 * step - 1], (tails, accs))
        return step + 1, jnp.min(tails), tails, accs

    accs = lax.while_loop(live, pair_step, (0, jnp.min(carry[0]), *carry))[3]
    for pp in range(pairs):
        o_ref[:, pp * 128:(pp + 1) * 128] = accs[pp * QB:(pp + 1) * QB]


def _sb_attention(pb, batch, seq, cast=(), cast_layer=()):
    nqb = seq // QB
    w = SB_GROUP * HEAD_DIM
    assert SB_HEADS == SB_GROUP
    qc, kc, vc = PB_SB_Q // w, PB_SB_K // w, PB_SB_V // w
    c_in, c_out, c_shapes = _cast_specs(cast, cast_layer, batch * nqb, lambda b, p, i: b * nqb + i)
    out = pl.pallas_call(
        functools.partial(_sb_kernel, n_cast=len(cast)),
        grid=(batch, SB_HEADS // SB_GROUP, nqb),
        in_specs=[
            pl.BlockSpec((QB, w), lambda b, p, i: (b * nqb + i, qc + p)),
            pl.BlockSpec((seq, w), lambda b, p, i: (b, kc + p)),
            pl.BlockSpec((seq, w), lambda b, p, i: (b, vc + p)),
        ] + c_in,
        out_specs=[pl.BlockSpec((QB, w), lambda b, p, i: (b * nqb + i, p))] + c_out,
        out_shape=[jax.ShapeDtypeStruct((batch * seq, SB_WIDTH), F32)] + c_shapes,
        scratch_shapes=[pltpu.VMEM((SB_GROUP // 2, 2 * seq, 128), BF16),
                        pltpu.VMEM((SB_GROUP // 2, 2 * seq, 128), BF16)],
        compiler_params=_params(("parallel", "parallel", "arbitrary")),
        name="sb_attention",
    )(pb, pb, pb, *cast)
    return out[0], out[1:]


CMP_TQ = 4 * QB


def _nsa_cmp_kernel(q_ref, uk_ref, uv_ref, pos_ref, w_ref, bias_ref, ovt_ref, ocmp_ref, sel_ref,
                    kc_ref, vc_ref, *, n_cmp, n_sel):
    i = pl.program_id(2)
    half = CMP_STRIDE * HEAD_DIM

    @pl.when(i == 0)
    def _():
        def compress(kv, u_ref):
            u = u_ref[0, 0]
            top = _dot((u + pos_ref[kv, 0:1, :]).astype(BF16), w_ref[kv, 0:half, :])
            bot = _dot((u + pos_ref[kv, 1:2, :]).astype(BF16), w_ref[kv, half:2 * half, :])
            return top + pltpu.roll(bot, QB - 1, 0)

        zeros = jnp.zeros((QB, HEAD_DIM), F32)
        kc_ref[...] = jnp.concatenate([compress(0, uk_ref), zeros], axis=1).astype(BF16)
        v = compress(1, uv_ref)
        vc_ref[...] = jnp.concatenate([jnp.concatenate([v, zeros], axis=1),
                                       jnp.concatenate([zeros, v], axis=1)], axis=0).astype(BF16)

    G = NSA_GQA
    TQ = CMP_TQ
    q4 = jnp.concatenate([q_ref[:, g * 128:(g + 1) * 128] for g in range(G)], axis=0)
    s = _dot_nt(q4, kc_ref[...]) + bias_ref[0]
    e = jnp.exp(s - jnp.max(s, axis=-1, keepdims=True))
    t_rows = i * TQ + lax.broadcasted_iota(jnp.int32, (TQ, 1), 0)
    any_valid = jnp.concatenate([t_rows >= CMP_BLOCK - 1] * G, axis=0)
    p = jnp.where(any_valid, e / jnp.sum(e, axis=-1, keepdims=True), 0.0)
    p16 = p.astype(BF16)
    for pp in range(G // 2):
        pair = jnp.concatenate([p16[2 * pp * TQ:(2 * pp + 1) * TQ], p16[(2 * pp + 1) * TQ:(2 * pp + 2) * TQ]],
                               axis=1)
        ocmp_ref[:, pp * 128:(pp + 1) * 128] = _dot(pair, vc_ref[...])
    p_all = p[0:TQ]
    for g in range(1, G):
        p_all = p_all + p[g * TQ:(g + 1) * TQ]

    blk = lax.broadcasted_iota(jnp.int32, (n_sel, QB), 0)
    for part in range(TQ // QB):
        rs = slice(part * QB, (part + 1) * QB)
        t0 = (i * (TQ // QB) + part) * QB
        p_sum = p_all[rs]
        hi = p_sum.astype(BF16)
        lo = (p_sum - hi.astype(F32)).astype(BF16)
        p_sel = (_dot_nt(ovt_ref[...], hi) + _dot_nt(ovt_ref[...], lo))[0:n_sel]
        t = t0 + lax.broadcasted_iota(jnp.int32, (n_sel, QB), 1)
        cur = t // SEL_BLOCK
        eligible = blk * SEL_BLOCK <= t
        forced = (blk == 0) | (blk == cur) | (blk == cur - 1)
        score = jnp.where(eligible, p_sel + jnp.where(forced, FORCE_SCORE, 0.0), NEG_INF)
        rank = jnp.zeros((n_sel, QB), F32)
        for j in range(n_sel):
            other = score[j:j + 1, :]
            ahead = (other > score) | ((other == score) & (blk > j))
            rank = rank + jnp.where(ahead, 1.0, 0.0)
        chosen = jnp.where(eligible & (rank < SEL_TOPK), 1.0, 0.0)
        chosen = jnp.concatenate([chosen, jnp.zeros((QB - n_sel, QB), F32)], axis=0)
        sel_ref[0, 0, rs, :] = chosen.T.astype(BF16)


def _nsa_cmp(pb, ucmp, pos2, cmp_w, bias_c, overlap_t, batch, seq):
    nqb = seq // CMP_TQ
    n_cmp = (seq - CMP_BLOCK) // CMP_STRIDE + 1
    n_sel = seq // SEL_BLOCK
    ng = seq // CMP_STRIDE
    assert ng == QB and n_sel <= QB and n_sel % 8 == 0
    wide = CMP_STRIDE * HEAD_DIM
    return pl.pallas_call(
        functools.partial(_nsa_cmp_kernel, n_cmp=n_cmp, n_sel=n_sel),
        grid=(batch, NSA_KV_HEADS, nqb),
        in_specs=[
            pl.BlockSpec((CMP_TQ, NSA_GQA * 128), lambda b, h, i: (b * nqb + i, h)),
            pl.BlockSpec((1, 1, ng, wide), lambda b, h, i: (b, h, 0, 0)),
            pl.BlockSpec((1, 1, ng, wide), lambda b, h, i: (b, NSA_KV_HEADS + h, 0, 0)),
            pl.BlockSpec((2, 2, wide), lambda b, h, i: (0, 0, 0)),
            pl.BlockSpec((2, 2 * wide, HEAD_DIM), lambda b, h, i: (0, 0, 0)),
            pl.BlockSpec((1, NSA_GQA * CMP_TQ, QB), lambda b, h, i: (h * nqb + i, 0, 0)),
            pl.BlockSpec((QB, QB), lambda b, h, i: (0, 0)),
        ],
        out_specs=[
            pl.BlockSpec((CMP_TQ, 256), lambda b, h, i: (b * nqb + i, h)),
            pl.BlockSpec((1, 1, CMP_TQ, QB), lambda b, h, i: (b, h, i, 0)),
        ],
        out_shape=[
            jax.ShapeDtypeStruct((batch * seq, NSA_WIDTH), F32),
            jax.ShapeDtypeStruct((batch, NSA_KV_HEADS, seq, QB), BF16),
        ],
        scratch_shapes=[pltpu.VMEM((QB, 128), BF16), pltpu.VMEM((2 * QB, 128), BF16)],
        compiler_params=_params(("parallel", "parallel", "arbitrary")),
        name="nsa_compressed",
    )(pb, ucmp, ucmp, pos2, cmp_w, bias_c, overlap_t)


SW_TQ = 2 * QB
SEL_CHUNK = 4 * QB
WIN_FAR = WINDOW - QB


def _nsa_sw_kernel(*refs, n_cast):
    n_in = 8
    _cast_slabs(refs[n_in:n_in + n_cast], refs[n_in + n_cast + 1:n_in + 2 * n_cast + 1])
    _nsa_sw_body(*refs[:n_in], refs[n_in + n_cast], *refs[n_in + 2 * n_cast + 1:])


def _nsa_sw_body(q_ref, ks_ref, kw_ref, ocmp_ref, misc_ref, sel_ref, exp_ref, tb_ref, o_ref, madd_ref):
    i = pl.program_id(2)
    G = NSA_GQA
    TQ = SW_TQ
    q4 = jnp.concatenate([q_ref[:, g * 128:(g + 1) * 128] for g in range(G)], axis=0)
    madd_ref[...] = (_dot(sel_ref[0, 0], exp_ref[...]) - 1.0) * (-NEG_INF)
    t_pos = i * TQ + lax.broadcasted_iota(jnp.int32, (TQ, 1), 0)

    low = lax.broadcasted_iota(jnp.int32, (QB, 128), 1) < HEAD_DIM

    def tile(x, n=G):
        return jnp.concatenate([x] * n, axis=0)

    def ones_v(kv):
        return jnp.where(tile(low, kv.shape[0] // QB), jnp.ones_like(kv), kv)

    first = 2 * i - 1
    rows_m = pl.ds(pl.multiple_of(jnp.maximum(first, 0) * QB, QB), QB)
    rows_a = pl.ds(pl.multiple_of(2 * i * QB, QB), QB)
    rows_b = pl.ds(pl.multiple_of((2 * i + 1) * QB, QB), QB)
    gone = jnp.where(i >= 1, 0.0, NEG_INF)
    gone_m = jnp.concatenate([jnp.full((TQ, QB), gone, F32), jnp.zeros((TQ, 2 * QB), F32)], axis=1)
    far_end = first * QB

    kv = jnp.concatenate([ks_ref[rows_m, :], ks_ref[rows_a, :], ks_ref[rows_b, :]], axis=0)
    near_sel = jnp.concatenate([madd_ref[:, rows_m], madd_ref[:, rows_a], madd_ref[:, rows_b]], axis=1)
    s = _dot_nt(q4, kv) + (tb_ref[0] + tile(near_sel + gone_m))
    m = jnp.max(s, axis=-1, keepdims=True)
    acc = _dot(jnp.exp(s - m).astype(BF16), ones_v(kv))

    def sel_far(cidx, state):
        m, acc = state
        cols = pl.ds(pl.multiple_of(cidx * SEL_CHUNK, SEL_CHUNK), SEL_CHUNK)
        k_pos = cidx * SEL_CHUNK + lax.broadcasted_iota(jnp.int32, (TQ, SEL_CHUNK), 1)
        add = jnp.where(k_pos < far_end, madd_ref[:, cols], NEG_INF)
        kv = ks_ref[cols, :]
        s = _dot_nt(q4, kv) + tile(add)
        m_new = jnp.maximum(m, jnp.max(s, axis=-1, keepdims=True))
        return m_new, jnp.exp(m - m_new) * acc + _dot(jnp.exp(s - m_new).astype(BF16), ones_v(kv))

    _, sel_acc = lax.fori_loop(0, (2 * i + 2) // (SEL_CHUNK // QB), sel_far, (m, acc))

    start = jnp.maximum(first - WIN_FAR // QB, 0) * QB
    rows_far = pl.ds(pl.multiple_of(start, QB), WIN_FAR)
    k_pos = start + lax.broadcasted_iota(jnp.int32, (TQ, WIN_FAR), 1)
    live = (k_pos < far_end) & (k_pos > t_pos - WINDOW)
    kv = jnp.concatenate([kw_ref[rows_far, :], kw_ref[rows_m, :], kw_ref[rows_a, :], kw_ref[rows_b, :]], axis=0)
    bias = jnp.concatenate([tile(jnp.where(live, 0.0, NEG_INF)), tb_ref[0] + tile(gone_m)], axis=1)
    s = _dot_nt(q4, kv) + bias
    win_acc = _dot(jnp.exp(s - jnp.max(s, axis=-1, keepdims=True)).astype(BF16), ones_v(kv))

    gate = _sigmoid(misc_ref[...])
    low = tile(low, TQ // QB)

    def pair(acc, pp):
        a, b = acc[2 * pp * TQ:(2 * pp + 1) * TQ], acc[(2 * pp + 1) * TQ:(2 * pp + 2) * TQ]
        return (jnp.where(low, pltpu.roll(a, HEAD_DIM, 1), b)
                / jnp.where(low, a, pltpu.roll(b, HEAD_DIM, 1)))

    def pair_gate(branch, pp):
        ca, cb = 3 * (2 * pp) + branch, 3 * (2 * pp + 1) + branch
        return jnp.where(low, gate[:, ca:ca + 1], gate[:, cb:cb + 1])

    for pp in range(G // 2):
        lanes = slice(pp * 128, (pp + 1) * 128)
        o_ref[:, lanes] = (pair_gate(0, pp) * ocmp_ref[:, lanes] + pair_gate(1, pp) * pair(sel_acc, pp)
                           + pair_gate(2, pp) * pair(win_acc, pp))


def _nsa_sel_win(pb, pf, o_cmp, sel, expand, tb, batch, seq, cast=(), cast_layer=()):
    nqb = seq // SW_TQ
    G = NSA_GQA
    c_in, c_out, c_shapes = _cast_specs(cast, cast_layer, batch * NSA_KV_HEADS * nqb,
                                        lambda b, h, i: (b * NSA_KV_HEADS + h) * nqb + i)
    out = pl.pallas_call(
        functools.partial(_nsa_sw_kernel, n_cast=len(cast)),
        grid=(batch, NSA_KV_HEADS, nqb),
        in_specs=[
            pl.BlockSpec((SW_TQ, G * 128), lambda b, h, i: (b * nqb + i, h)),
            pl.BlockSpec((seq, 128), lambda b, h, i: (b, PB_SEL // 128 + h)),
            pl.BlockSpec((seq, 128), lambda b, h, i: (b, PB_WIN // 128 + h)),
            pl.BlockSpec((SW_TQ, 256), lambda b, h, i: (b * nqb + i, h)),
            pl.BlockSpec((SW_TQ, 128), lambda b, h, i: (b * nqb + i, PF_MISC // 128 + h)),
            pl.BlockSpec((1, 1, SW_TQ, QB), lambda b, h, i: (b, h, i, 0)),
            pl.BlockSpec((QB, seq), lambda b, h, i: (0, 0)),
            pl.BlockSpec((1, G * SW_TQ, 3 * QB), lambda b, h, i: (h, 0, 0)),
        ] + c_in,
        out_specs=[pl.BlockSpec((SW_TQ, 256), lambda b, h, i: (b * nqb + i, h))] + c_out,
        out_shape=[jax.ShapeDtypeStruct((batch * seq, NSA_WIDTH), F32)] + c_shapes,
        scratch_shapes=[pltpu.VMEM((SW_TQ, seq), F32)],
        compiler_params=_params(("parallel", "parallel", "arbitrary")),
        name="nsa_selected_window",
    )(pb, pb, pb, o_cmp, pf, sel, expand, tb, *cast)
    return out[0], out[1:]


def _ssd_kernel(*refs, n_cast):
    n_in = 12
    _cast_slabs(refs[n_in:n_in + n_cast], refs[n_in + n_cast + 1:n_in + 2 * n_cast + 1])
    _ssd_body(*refs[:n_in], refs[n_in + n_cast], *refs[n_in + 2 * n_cast + 1:])


def _ssd_body(z_ref, xs_ref, bc_ref, misc_ref, cwx_ref, cwb_ref, cbx_ref, cbb_ref, hp_ref, spread_ref,
              dskip_ref, ng_ref, o_ref, xbuf, bbuf, state):
    ci = pl.program_id(1)
    L = QB
    P = HEAD_DIM
    GN = SSD_GROUPS * SSD_STATE

    @pl.when(ci == 0)
    def _():
        xbuf[...] = jnp.zeros_like(xbuf)
        bbuf[...] = jnp.zeros_like(bbuf)
        state[...] = jnp.zeros_like(state)

    def conv_silu(buf, src_ref, w_ref, b_ref):
        cur = src_ref[...]
        ext = jnp.concatenate([buf[...], cur], axis=0)
        out = b_ref[...] + w_ref[SSD_CONV - 1:SSD_CONV, :] * cur
        for k in range(1, SSD_CONV):
            out = out + w_ref[SSD_CONV - 1 - k:SSD_CONV - k, :] * pltpu.roll(ext, k, 0)[8:8 + L]
        buf[...] = cur[L - 8:L]
        return _silu(out)

    xs = conv_silu(xbuf, xs_ref, cwx_ref, cbx_ref)
    bcs = conv_silu(bbuf, bc_ref, cwb_ref, cbb_ref)

    misc = misc_ref[...]
    pre = misc + hp_ref[0:1, :]
    dt = jnp.maximum(pre, 0.0) + jnp.log1p(jnp.exp(-jnp.abs(pre)))
    a_dt = dt * (-jnp.exp(hp_ref[1:2, :]))
    r = lax.broadcasted_iota(jnp.int32, (L, L), 0)
    c = lax.broadcasted_iota(jnp.int32, (L, L), 1)
    causal = r >= c
    a_cs = _split3_dot_left(causal.astype(BF16), a_dt)
    a_cs_t = a_cs.T

    dt_full = _split3_dot(dt, spread_ref[...])
    acs_full = _split3_dot(a_cs, spread_ref[...])
    total_full = acs_full[L - 1:L, :]
    xdt = xs * dt_full
    xdt16 = xdt.astype(BF16)
    to_end16 = (xdt * jnp.exp(total_full - acs_full)).astype(BF16)
    decay_in = jnp.exp(acs_full)
    chunk_decay = jnp.exp(total_full)
    low = c < P
    zero16 = jnp.zeros((L, 2 * P), BF16)
    width = SSD_HPG * P

    ys = []
    for g in range(SSD_GROUPS):
        bm = bcs[:, g * SSD_STATE:(g + 1) * SSD_STATE]
        cm16 = bcs[:, GN + g * SSD_STATE:GN + (g + 1) * SSD_STATE].astype(BF16)
        cb = _dot_nt(cm16, bm.astype(BF16))
        lanes = slice(g * width, (g + 1) * width)
        h_in = state[g]
        y_off = _dot(cm16, h_in.astype(BF16)) * decay_in[:, lanes]
        state[g] = h_in * chunk_decay[:, lanes] + _dot(bm.T.astype(BF16), to_end16[:, lanes])
        diag = []
        for pp in range(SSD_HPG // 2):
            hd = g * SSD_HPG + 2 * pp
            decayed = []
            for col in (MISC_DT + hd, MISC_DT + hd + 1):
                seg = jnp.exp(jnp.where(causal, a_cs[:, col:col + 1] - a_cs_t[col:col + 1, :], -jnp.inf))
                decayed.append((cb * seg).astype(BF16))
            xp = xdt16[:, hd * P:(hd + 2) * P]
            x_bd = jnp.concatenate([jnp.where(low, xp, zero16), jnp.where(low, zero16, xp)], axis=0)
            diag.append(_dot(jnp.concatenate(decayed, axis=1), x_bd))
        ys.append(jnp.concatenate(diag, axis=1) + y_off)

    y = (jnp.concatenate(ys, axis=1) + dskip_ref[...] * xs) * _silu(z_ref[...])
    for g in range(SSD_GROUPS):
        lanes = slice(g * width, (g + 1) * width)
        yg = y[:, lanes]
        o_ref[:, lanes] = yg * lax.rsqrt(jnp.mean(yg * yg, axis=-1, keepdims=True) + EPS) * ng_ref[:, lanes]


def _ssd(pf, conv_w, conv_b, head_params, d_skip, norm_g, batch, seq, cast=(), cast_layer=()):
    nc = seq // QB
    c_in, c_out, c_shapes = _cast_specs(cast, cast_layer, batch * nc, lambda b, ci: b * nc + ci)
    GN = SSD_GROUPS * SSD_STATE
    lane_head = jnp.arange(SSD_INNER)[None, :] // HEAD_DIM
    spread = (jnp.arange(128)[:, None] == MISC_DT + lane_head).astype(BF16)
    cwx, cwb = conv_w[:, :SSD_INNER], conv_w[:, SSD_INNER:]
    cbx, cbb = conv_b[:SSD_INNER].reshape(1, -1), conv_b[SSD_INNER:].reshape(1, -1)
    full = lambda shape: pl.BlockSpec(shape, lambda b, ci: (0,) * len(shape))
    out = pl.pallas_call(
        functools.partial(_ssd_kernel, n_cast=len(cast)),
        grid=(batch, nc),
        in_specs=[
            pl.BlockSpec((QB, SSD_INNER), lambda b, ci: (b * nc + ci, PF_Z // SSD_INNER)),
            pl.BlockSpec((QB, SSD_INNER), lambda b, ci: (b * nc + ci, PF_XS // SSD_INNER)),
            pl.BlockSpec((QB, 2 * GN), lambda b, ci: (b * nc + ci, PF_BC // (2 * GN))),
            pl.BlockSpec((QB, 128), lambda b, ci: (b * nc + ci, PF_MISC // 128)),
            full((SSD_CONV, SSD_INNER)),
            full((SSD_CONV, 2 * GN)),
            full((1, SSD_INNER)),
            full((1, 2 * GN)),
            full((8, 128)),
            full((128, SSD_INNER)),
            full((1, SSD_INNER)),
            full((1, SSD_INNER)),
        ] + c_in,
        out_specs=[pl.BlockSpec((QB, SSD_INNER), lambda b, ci: (b * nc + ci, 0))] + c_out,
        out_shape=[jax.ShapeDtypeStruct((batch * seq, SSD_INNER), F32)] + c_shapes,
        scratch_shapes=[
            pltpu.VMEM((8, SSD_INNER), F32),
            pltpu.VMEM((8, 2 * GN), F32),
            pltpu.VMEM((SSD_GROUPS, SSD_STATE, SSD_HPG * HEAD_DIM), F32),
        ],
        compiler_params=_params(("parallel", "arbitrary")),
        name="ssd",
    )(pf, pf, pf, pf, cwx, cwb, cbx, cbb, head_params, spread,
      jnp.repeat(d_skip.astype(F32), HEAD_DIM).reshape(1, -1), norm_g.reshape(1, -1), *cast)
    return out[0], out[1:]


def _out_proj_kernel(x_ref, nsa_ref, sb_ref, ssd_ref, gn_ref, gs_ref, w_ref, o_ref, mix_ref):
    @pl.when(pl.program_id(1) == 0)
    def _():
        mix_ref[:, 0:NSA_WIDTH] = _rms(nsa_ref[...], gn_ref[...]).astype(BF16)
        mix_ref[:, NSA_WIDTH:NSA_WIDTH + SB_WIDTH] = _rms(sb_ref[...], gs_ref[...]).astype(BF16)
        mix_ref[:, NSA_WIDTH + SB_WIDTH:] = ssd_ref[...].astype(BF16)

    o_ref[...] = x_ref[...] + _dot(mix_ref[...], w_ref[...])


def _out_proj(x, o_nsa, o_sb, o_ssd, g_nsa, g_sb, w, *, tm=1024, tn=1024):
    n, d = x.shape
    dm = w.shape[0]
    return pl.pallas_call(
        _out_proj_kernel,
        grid=(n // tm, d // tn),
        in_specs=[
            pl.BlockSpec((tm, tn), lambda i, j: (i, j)),
            pl.BlockSpec((tm, NSA_WIDTH), lambda i, j: (i, 0)),
            pl.BlockSpec((tm, SB_WIDTH), lambda i, j: (i, 0)),
            pl.BlockSpec((tm, SSD_INNER), lambda i, j: (i, 0)),
            pl.BlockSpec((1, NSA_WIDTH), lambda i, j: (0, 0)),
            pl.BlockSpec((1, SB_WIDTH), lambda i, j: (0, 0)),
            pl.BlockSpec((dm, tn), lambda i, j: (0, j)),
        ],
        out_specs=pl.BlockSpec((tm, tn), lambda i, j: (i, j)),
        out_shape=jax.ShapeDtypeStruct((n, d), F32),
        scratch_shapes=[pltpu.VMEM((tm, dm), BF16)],
        compiler_params=_params(("parallel", "arbitrary")),
        name="out_proj",
    )(x, o_nsa, o_sb, o_ssd, g_nsa.reshape(1, -1), g_sb.reshape(1, -1), w)


def _rel_bucket(dist):
    dist = jnp.maximum(dist, 0)
    max_exact = REL_BUCKETS // 2
    log_ratio = jnp.log(jnp.maximum(dist, 1).astype(F32) / max_exact) / math.log(REL_MAX_DIST / max_exact)
    large = jnp.minimum(max_exact + (log_ratio * (REL_BUCKETS - max_exact)).astype(jnp.int32), REL_BUCKETS - 1)
    return jnp.where(dist < max_exact, dist, large)


def _bias_tables(rel_bias, seq):
    assert QB >= REL_MAX_DIST
    def lookup(dist):
        onehot = jax.nn.one_hot(_rel_bucket(dist), REL_BUCKETS, dtype=F32)
        return jnp.einsum('...k,kh->h...', onehot, rel_bias, precision=lax.Precision.HIGHEST)

    t = jnp.arange(seq)[:, None]
    cend = jnp.arange(QB)[None, :] * CMP_STRIDE + CMP_BLOCK - 1
    n_cmp = (seq - CMP_BLOCK) // CMP_STRIDE + 1
    valid_c = (t >= cend) & (jnp.arange(QB)[None, :] < n_cmp)
    bias_c = jnp.where(valid_c, lookup(t - cend), NEG_INF)
    steps = seq // CMP_TQ
    bias_c = bias_c.reshape(NSA_KV_HEADS, NSA_GQA, steps, CMP_TQ, QB).transpose(0, 2, 1, 3, 4)
    bias_c = bias_c.reshape(NSA_KV_HEADS * steps, NSA_GQA * CMP_TQ, QB)
    r = jnp.arange(QB)[:, None]
    m = jnp.arange(QB)[None, :]
    near = jnp.stack([lookup(r - m), lookup(QB + r - m)])
    near = near - rel_bias[REL_BUCKETS - 1][None, :, None, None]
    diag = near[0] + jnp.where(m > r, NEG_INF, 0.0)
    prev = near[1]
    rows_a = jnp.concatenate([prev, diag, jnp.full_like(diag, NEG_INF)], axis=2)
    rows_b = jnp.concatenate([jnp.zeros_like(diag), prev, diag], axis=2)
    near2 = jnp.concatenate([rows_a, rows_b], axis=1)
    return bias_c, near2.reshape(NSA_KV_HEADS, NSA_GQA * SW_TQ, 3 * QB)


def _expand_table(seq):
    j = jnp.arange(QB)[:, None]
    s = jnp.arange(seq)[None, :]
    return (s // SEL_BLOCK == j).astype(BF16)


def _overlap_table(seq):
    n_cmp = (seq - CMP_BLOCK) // CMP_STRIDE + 1
    cs = jnp.arange(QB)[:, None] * CMP_STRIDE
    ce = cs + CMP_BLOCK - 1
    ss = jnp.arange(QB)[None, :] * SEL_BLOCK
    ov = jnp.maximum(jnp.minimum(ce, ss + SEL_BLOCK - 1) - jnp.maximum(cs, ss) + 1, 0).astype(F32) / CMP_BLOCK
    keep = (jnp.arange(QB)[:, None] < n_cmp) & (jnp.arange(QB)[None, :] < seq // SEL_BLOCK)
    return jnp.where(keep, ov, 0.0).T.astype(BF16)


def _in_proj_weights(w):
    w = w.astype(BF16)
    scale = HEAD_DIM ** -0.5
    o_q, o_kv, o_gate = 0, NSA_WIDTH, NSA_WIDTH + 768
    o_sb = o_gate + 3 * NSA_HEADS
    o_z = o_sb + 3 * SB_WIDTH
    o_xbc = o_z + SSD_INNER
    o_dt = o_xbc + SSD_INNER + 2 * SSD_GROUPS * SSD_STATE
    col = lambda a, n: w[:, a:a + n]
    kv = lambda br, which, h: col(o_kv + br * 256 + which * 128 + h * HEAD_DIM, HEAD_DIM)
    zeros = lambda n: jnp.zeros((w.shape[0], n), w.dtype)
    branch = lambda br: [kv(br, 0, 0), kv(br, 1, 0), kv(br, 0, 1), kv(br, 1, 1)]
    q_heads = []
    for h in range(NSA_HEADS):
        q_heads += [col(o_q + h * HEAD_DIM, HEAD_DIM) * scale, zeros(128 - HEAD_DIM)]
    wb = jnp.concatenate(
        q_heads + branch(1) + branch(2)
        + [col(o_sb, SB_WIDTH) * scale, col(o_sb + SB_WIDTH, 2 * SB_WIDTH)], axis=1)
    gates = 3 * NSA_GQA
    wf = jnp.concatenate(
        [col(o_z, SSD_INNER), col(o_xbc, SSD_INNER + 2 * SSD_GROUPS * SSD_STATE), col(o_kv, 256),
         col(o_gate, gates), zeros(MISC_DT - gates), col(o_dt, SSD_HEADS), zeros(128 - MISC_DT - SSD_HEADS),
         col(o_gate + gates, gates), zeros(128 - gates)], axis=1)
    assert wb.shape[1] == PB_COLS and wf.shape[1] == PF_COLS
    return wb.astype(BF16), wf.astype(BF16)


def _head_params(dt_bias, a_log):
    rows = jnp.stack([dt_bias, a_log]).astype(F32)
    return jnp.zeros((8, 128), F32).at[0:2, MISC_DT:MISC_DT + SSD_HEADS].set(rows)


def _mixer(x, l, tables, batch, seq, mix_norm, w_in, w_out, cmp_pos, cmp_w, nsa_norm, sb_norm,
           conv_w, conv_b, dt_bias, a_log, d_skip, ssd_norm, casts):
    hosts = [([w for w, _ in casts[k::3]], tuple(at for _, at in casts[k::3])) for k in range(3)]
    bias_c, near, overlap, expand = tables
    wb, wf = _in_proj_weights(w_in[l])
    pb = _rms_matmul(x, mix_norm[l], wb, BF16, name="in_proj_bf16")
    pf = _rms_matmul(x, mix_norm[l], wf, F32, name="in_proj_f32")

    ng = seq // CMP_STRIDE
    ucmp = pf[:, PF_CMP:PF_CMP + 256].reshape(batch, ng, CMP_STRIDE, 4, HEAD_DIM)
    ucmp = ucmp.transpose(0, 3, 1, 2, 4).reshape(batch, 4, ng, CMP_STRIDE * HEAD_DIM)
    pos2 = cmp_pos[l].reshape(2, 2, CMP_STRIDE * HEAD_DIM)
    o_cmp, sel = _nsa_cmp(pb, ucmp, pos2, cmp_w[l].astype(BF16), bias_c, overlap, batch, seq)
    o_nsa, w0 = _nsa_sel_win(pb, pf, o_cmp, sel, expand, near, batch, seq, *hosts[0])
    o_sb, w1 = _sb_attention(pb, batch, seq, *hosts[1])
    o_ssd, w2 = _ssd(pf, conv_w[l], conv_b[l], _head_params(dt_bias[l], a_log[l]), d_skip[l], ssd_norm[l],
                     batch, seq, *hosts[2])
    out = _out_proj(x, o_nsa, o_sb, o_ssd, nsa_norm[l], sb_norm[l], w_out[l].astype(BF16))
    done = [None] * len(casts)
    for k, ws in enumerate((w0, w1, w2)):
        done[k::3] = ws
    return out, done


def kernel(x, rel_bias, ffn1_norm, ffn1_w_gate, ffn1_w_up, ffn1_w_down, mix_norm, w_in, w_out, nsa_cmp_pos, nsa_cmp_w, nsa_out_norm, sb_out_norm, ssd_conv_w, ssd_conv_b, ssd_dt_bias, ssd_a_log, ssd_d, ssd_out_norm, ffn2_norm, ffn2_w_gate, ffn2_w_up, ffn2_w_down, final_norm):
    batch, seq, d = x.shape
    depth = w_in.shape[0]
    tables = _bias_tables(rel_bias, seq) + (_overlap_table(seq), _expand_table(seq))
    h = x.reshape(batch * seq, d)

    ffn1 = (ffn1_w_gate, ffn1_w_up, ffn1_w_down)
    ffn2 = (ffn2_w_gate, ffn2_w_up, ffn2_w_down)
    widen = lambda ws: tuple(w[None] for w in ws)
    w1 = widen(_cast_layer(ffn1, 0))
    for l in range(depth):
        h = _ffn(h, ffn1_norm[l], *w1, 0, final_norm, final=False)
        casts = [(w, l) for w in ffn2] + ([(w, l + 1) for w in ffn1] if l + 1 < depth else [])
        h, done = _mixer(h, l, tables, batch, seq, mix_norm, w_in, w_out, nsa_cmp_pos, nsa_cmp_w,
                         nsa_out_norm, sb_out_norm, ssd_conv_w, ssd_conv_b, ssd_dt_bias, ssd_a_log,
                         ssd_d, ssd_out_norm, casts)
        h = _ffn(h, ffn2_norm[l], *widen(done[0:3]), 0, final_norm, final=(l == depth - 1))
        w1 = widen(done[3:6])
    return h.reshape(batch, seq, d)
```

```python
import functools
import math

import jax
import jax.numpy as jnp
from jax import lax
from jax.experimental import pallas as pl
from jax.experimental.pallas import tpu as pltpu

D_MODEL = 2048
D_FF = 5632
HEAD_DIM = 64
QB = 128
NEG_INF = -1e30
EPS = 1e-6

NSA_HEADS = 8
NSA_KV_HEADS = 2
NSA_GQA = NSA_HEADS // NSA_KV_HEADS
NSA_WIDTH = NSA_HEADS * HEAD_DIM
CMP_BLOCK = 32
CMP_STRIDE = 16
SEL_BLOCK = 64
SEL_TOPK = 8
FORCE_SCORE = 1e3
WINDOW = 512
SB_HEADS = 8
SB_WIDTH = SB_HEADS * HEAD_DIM
SSD_HEADS = 16
SSD_INNER = SSD_HEADS * HEAD_DIM
SSD_GROUPS = 2
SSD_HPG = SSD_HEADS // SSD_GROUPS
SSD_STATE = 128
SSD_CONV = 4
REL_BUCKETS = 32
REL_MAX_DIST = 128

PB_NSA_Q, PB_SEL, PB_WIN, PB_SB_Q, PB_SB_K, PB_SB_V, PB_COLS = 0, 1024, 1280, 1536, 2048, 2560, 3072
PF_Z, PF_XS, PF_BC, PF_CMP, PF_MISC, PF_COLS = 0, 1024, 2048, 2560, 2816, 3072
MISC_DT = 16

VMEM_LIMIT = 52 * 1024 * 1024

BF16 = jnp.bfloat16
F32 = jnp.float32


def _dot(a, b):
    return jnp.dot(a, b, preferred_element_type=F32)


def _dot_nt(a, b):
    return lax.dot_general(a, b, (((1,), (1,)), ((), ())), preferred_element_type=F32)


def _split2_dot(x, m):
    hi = x.astype(BF16)
    lo = (x - hi.astype(F32)).astype(BF16)
    return _dot(hi, m) + _dot(lo, m)


def _split3_dot(x, m):
    hi = x.astype(BF16)
    r1 = x - hi.astype(F32)
    mid = r1.astype(BF16)
    lo = (r1 - mid.astype(F32)).astype(BF16)
    return _dot(hi, m) + _dot(mid, m) + _dot(lo, m)


def _split3_dot_left(m, x):
    hi = x.astype(BF16)
    r1 = x - hi.astype(F32)
    mid = r1.astype(BF16)
    lo = (r1 - mid.astype(F32)).astype(BF16)
    return _dot(m, hi) + _dot(m, mid) + _dot(m, lo)


def _rms(x, g):
    return x * lax.rsqrt(jnp.mean(x * x, axis=-1, keepdims=True) + EPS) * g


def _silu(x):
    return x / (1.0 + jnp.exp(-x))


def _sigmoid(x):
    return 1.0 / (1.0 + jnp.exp(-x))


def _params(sem):
    return pltpu.CompilerParams(dimension_semantics=sem, vmem_limit_bytes=VMEM_LIMIT)


def _cast_specs(weights, layer, steps, step_of):
    in_specs, out_specs, out_shapes = [], [], []
    for w, at in zip(weights, layer):
        _, rows, cols = w.shape
        share = 1
        while (rows * share) % (16 * steps):
            share *= 2
        slab = rows * share // steps
        in_specs.append(pl.BlockSpec((None, slab, cols),
                                     lambda *ids, k=share, at=at: (at, step_of(*ids) // k, 0)))
        out_specs.append(pl.BlockSpec((slab, cols), lambda *ids, k=share: (step_of(*ids) // k, 0)))
        out_shapes.append(jax.ShapeDtypeStruct((rows, cols), BF16))
    return in_specs, out_specs, out_shapes


def _cast_slabs(srcs, dsts):
    for src, dst in zip(srcs, dsts):
        dst[...] = src[...].astype(BF16)


def _cast_kernel(*refs):
    _cast_slabs(refs[:len(refs) // 2], refs[len(refs) // 2:])


def _cast_layer(weights, layer, steps=16):
    c_in, c_out, c_shapes = _cast_specs(weights, (layer,) * len(weights), steps, lambda s: s)
    return pl.pallas_call(_cast_kernel, grid=(steps,), in_specs=c_in, out_specs=c_out, out_shape=c_shapes,
                          compiler_params=_params(("parallel",)), name="cast_weights")(*weights)


def _ffn_kernel(x_ref, g_ref, wg_ref, wu_ref, wd_ref, fg_ref, o_ref, h_ref, acc_ref, *, final):
    f = pl.program_id(1)

    @pl.when(f == 0)
    def _():
        h_ref[...] = _rms(x_ref[...], g_ref[...]).astype(BF16)
        acc_ref[...] = jnp.zeros_like(acc_ref)

    h = h_ref[...]
    gate = _dot(h, wg_ref[...])
    up = _dot(h, wu_ref[...])
    acc_ref[...] += _dot((_silu(gate) * up).astype(BF16), wd_ref[...])

    @pl.when(f == pl.num_programs(1) - 1)
    def _():
        y = x_ref[...] + 0.5 * acc_ref[...]
        if final:
            y = _rms(y, fg_ref[...])
        o_ref[...] = y


def _ffn(x, g, wg, wu, wd, layer, final_g, *, final, tm=512, tf=512):
    n, d = x.shape
    dff = wg.shape[2]
    return pl.pallas_call(
        functools.partial(_ffn_kernel, final=final),
        grid=(n // tm, dff // tf),
        in_specs=[
            pl.BlockSpec((tm, d), lambda i, f: (i, 0)),
            pl.BlockSpec((1, d), lambda i, f: (0, 0)),
            pl.BlockSpec((None, d, tf), lambda i, f: (layer, 0, f)),
            pl.BlockSpec((None, d, tf), lambda i, f: (layer, 0, f)),
            pl.BlockSpec((None, tf, d), lambda i, f: (layer, f, 0)),
            pl.BlockSpec((1, d), lambda i, f: (0, 0)),
        ],
        out_specs=pl.BlockSpec((tm, d), lambda i, f: (i, 0)),
        out_shape=jax.ShapeDtypeStruct((n, d), F32),
        scratch_shapes=[pltpu.VMEM((tm, d), BF16), pltpu.VMEM((tm, d), F32)],
        compiler_params=_params(("parallel", "arbitrary")),
        name="ffn",
    )(x, g.reshape(1, d), wg, wu, wd, final_g.reshape(1, d))


def _rms_matmul_kernel(x_ref, g_ref, w_ref, o_ref, h_ref):
    @pl.when(pl.program_id(1) == 0)
    def _():
        h_ref[...] = _rms(x_ref[...], g_ref[...]).astype(BF16)

    o_ref[...] = _dot(h_ref[...], w_ref[...]).astype(o_ref.dtype)


def _rms_matmul(x, g, w, out_dtype, *, tm=1024, tn=1024, name):
    n, d = x.shape
    c = w.shape[1]
    return pl.pallas_call(
        _rms_matmul_kernel,
        grid=(n // tm, c // tn),
        in_specs=[
            pl.BlockSpec((tm, d), lambda i, j: (i, 0)),
            pl.BlockSpec((1, d), lambda i, j: (0, 0)),
            pl.BlockSpec((d, tn), lambda i, j: (0, j)),
        ],
        out_specs=pl.BlockSpec((tm, tn), lambda i, j: (i, j)),
        out_shape=jax.ShapeDtypeStruct((n, c), out_dtype),
        scratch_shapes=[pltpu.VMEM((tm, d), BF16)],
        compiler_params=_params(("parallel", "arbitrary")),
        name=name,
    )(x, g.reshape(1, d), w)


SB_GROUP = 8
SB_DEAD = 104.0


def _sb_kernel(*refs, n_cast):
    q_ref, k_ref, v_ref = refs[:3]
    o_ref = refs[3 + n_cast]
    kbd_ref, vbd_ref = refs[4 + 2 * n_cast:]
    _cast_slabs(refs[3:3 + n_cast], refs[4 + n_cast:4 + 2 * n_cast])
    _sb_body(q_ref, k_ref, v_ref, o_ref, kbd_ref, vbd_ref)


def _sb_body(q_ref, k_ref, v_ref, o_ref, kbd_ref, vbd_ref):
    i = pl.program_id(2)
    pairs = SB_GROUP // 2
    nkb = k_ref.shape[0] // QB
    r = lax.broadcasted_iota(jnp.int32, (QB, QB), 0)
    c = lax.broadcasted_iota(jnp.int32, (QB, QB), 1)
    strict_all = jnp.concatenate([c < r] * SB_GROUP, axis=0)
    later_ones = jnp.concatenate([(r > c).astype(BF16), jnp.ones((QB, QB), BF16)], axis=1)
    later_ones = jnp.concatenate([later_ones, later_ones], axis=0)

    @pl.when(i == 0)
    def _():
        low = c < HEAD_DIM
        zero16 = jnp.zeros((QB, QB), BF16)

        def build(j, carry):
            src = pl.ds(pl.multiple_of(j * QB, QB), QB)
            dst = pl.ds(pl.multiple_of(j * 2 * QB, 2 * QB), 2 * QB)
            for pp in range(pairs):
                for ref, out in ((k_ref, kbd_ref), (v_ref, vbd_ref)):
                    x = ref[src, pp * 128:(pp + 1) * 128]
                    out[pp, dst, :] = jnp.concatenate([jnp.where(low, x, zero16), jnp.where(low, zero16, x)],
                                                      axis=0)
            return carry

        lax.fori_loop(0, nkb, build, 0)

    def scores(j, diagonal):
        rows = pl.ds(pl.multiple_of(j * 2 * QB, 2 * QB), 2 * QB)
        zs = []
        for pp in range(pairs):
            z_pair = _dot_nt(q_ref[:, pp * 128:(pp + 1) * 128], kbd_ref[pp, rows, :])
            zs += [z_pair[:, 0:QB], z_pair[:, QB:2 * QB]]
        z = jnp.concatenate(zs, axis=0)
        soft = jnp.maximum(z, 0.0) + jnp.log(1.0 + jnp.exp(-jnp.abs(z)))
        if diagonal:
            soft = jnp.where(strict_all, soft, 0.0)
        hi = soft.astype(BF16)
        lo = (soft - hi.astype(F32)).astype(BF16)
        sums = _dot(jnp.concatenate([hi, lo], axis=1), later_ones)
        return rows, z - soft, sums

    def absorb(tails, accs, rows, log_beta, sums, diagonal):
        a = jnp.exp(log_beta - (tails + sums[:, 0:QB]))
        if diagonal:
            a = jnp.where(strict_all, a, 0.0)
        a = a.astype(BF16)
        outs = []
        for pp in range(pairs):
            pair = jnp.concatenate([a[2 * pp * QB:(2 * pp + 1) * QB], a[(2 * pp + 1) * QB:(2 * pp + 2) * QB]],
                                   axis=1)
            outs.append(_dot(pair, vbd_ref[pp, rows, :]))
        return tails + sums[:, QB:2 * QB], accs + jnp.concatenate(outs, axis=0)

    def sweep(js, carry):
        staged = [scores(j, False) for j in js]
        for st in staged:
            carry = absorb(*carry, *st, False)
        return carry

    carry = (jnp.zeros((SB_GROUP * QB, QB), F32), jnp.zeros((pairs * QB, 128), F32))
    carry = absorb(*carry, *scores(i, True), True)
    odd = i % 2
    carry = lax.fori_loop(0, odd, lambda step, cr: sweep([i - 1], cr), carry)
    top = i - 1 - odd

    def live(state):
        step, smallest, _, _ = state
        return (step < i // 2) & (smallest < SB_DEAD)

    def pair_step(state):
        step, _, tails, accs = state
        tails, accs = sweep([top - 2 * step, top - 2 * step - 1], (tails, accs))
        return step + 1, jnp.min(tails), tails, accs

    accs = lax.while_loop(live, pair_step, (0, jnp.min(carry[0]), *carry))[3]
    for pp in range(pairs):
        o_ref[:, pp * 128:(pp + 1) * 128] = accs[pp * QB:(pp + 1) * QB]


def _sb_attention(pb, batch, seq, cast=(), cast_layer=()):
    nqb = seq // QB
    w = SB_GROUP * HEAD_DIM
    assert SB_HEADS == SB_GROUP
    qc, kc, vc = PB_SB_Q // w, PB_SB_K // w, PB_SB_V // w
    c_in, c_out, c_shapes = _cast_specs(cast, cast_layer, batch * nqb, lambda b, p, i: b * nqb + i)
    out = pl.pallas_call(
        functools.partial(_sb_kernel, n_cast=len(cast)),
        grid=(batch, SB_HEADS // SB_GROUP, nqb),
        in_specs=[
            pl.BlockSpec((QB, w), lambda b, p, i: (b * nqb + i, qc + p)),
            pl.BlockSpec((seq, w), lambda b, p, i: (b, kc + p)),
            pl.BlockSpec((seq, w), lambda b, p, i: (b, vc + p)),
        ] + c_in,
        out_specs=[pl.BlockSpec((QB, w), lambda b, p, i: (b * nqb + i, p))] + c_out,
        out_shape=[jax.ShapeDtypeStruct((batch * seq, SB_WIDTH), F32)] + c_shapes,
        scratch_shapes=[pltpu.VMEM((SB_GROUP // 2, 2 * seq, 128), BF16),
                        pltpu.VMEM((SB_GROUP // 2, 2 * seq, 128), BF16)],
        compiler_params=_params(("parallel", "parallel", "arbitrary")),
        name="sb_attention",
    )(pb, pb, pb, *cast)
    return out[0], out[1:]


CMP_TQ = 4 * QB


def _nsa_cmp_kernel(q_ref, uk_ref, uv_ref, pos_ref, w_ref, b0_ref, b1_ref, b2_ref, b3_ref, ovt_ref,
                    ocmp_ref, sel_ref, kc_ref, vc_ref, *, n_cmp, n_sel):
    i = pl.program_id(2)
    half = CMP_STRIDE * HEAD_DIM

    @pl.when(i == 0)
    def _():
        def compress(kv, u_ref):
            u = u_ref[0, 0]
            top = _dot((u + pos_ref[kv, 0:1, :]).astype(BF16), w_ref[kv, 0:half, :])
            bot = _dot((u + pos_ref[kv, 1:2, :]).astype(BF16), w_ref[kv, half:2 * half, :])
            return top + pltpu.roll(bot, QB - 1, 0)

        zeros = jnp.zeros((QB, HEAD_DIM), F32)
        kc_ref[...] = jnp.concatenate([compress(0, uk_ref), zeros], axis=1).astype(BF16)
        v = compress(1, uv_ref)
        vc_ref[...] = jnp.concatenate([jnp.concatenate([v, zeros], axis=1),
                                       jnp.concatenate([zeros, v], axis=1)], axis=0).astype(BF16)

    G = NSA_GQA
    TQ = CMP_TQ
    q4 = jnp.concatenate([q_ref[:, g * 128:(g + 1) * 128] for g in range(G)], axis=0)
    bias = jnp.concatenate([b[...] for b in (b0_ref, b1_ref, b2_ref, b3_ref)], axis=0)
    s = _dot_nt(q4, kc_ref[...]) + bias
    e = jnp.exp(s - jnp.max(s, axis=-1, keepdims=True))
    t_rows = i * TQ + lax.broadcasted_iota(jnp.int32, (TQ, 1), 0)
    any_valid = jnp.concatenate([t_rows >= CMP_BLOCK - 1] * G, axis=0)
    p = jnp.where(any_valid, e / jnp.sum(e, axis=-1, keepdims=True), 0.0)
    p16 = p.astype(BF16)
    for pp in range(G // 2):
        pair = jnp.concatenate([p16[2 * pp * TQ:(2 * pp + 1) * TQ], p16[(2 * pp + 1) * TQ:(2 * pp + 2) * TQ]],
                               axis=1)
        ocmp_ref[:, pp * 128:(pp + 1) * 128] = _dot(pair, vc_ref[...])
    p_all = p[0:TQ]
    for g in range(1, G):
        p_all = p_all + p[g * TQ:(g + 1) * TQ]

    blk = lax.broadcasted_iota(jnp.int32, (n_sel, QB), 0)
    for part in range(TQ // QB):
        rs = slice(part * QB, (part + 1) * QB)
        t0 = (i * (TQ // QB) + part) * QB
        p_sum = p_all[rs]
        hi = p_sum.astype(BF16)
        lo = (p_sum - hi.astype(F32)).astype(BF16)
        p_sel = (_dot_nt(ovt_ref[...], hi) + _dot_nt(ovt_ref[...], lo))[0:n_sel]
        t = t0 + lax.broadcasted_iota(jnp.int32, (n_sel, QB), 1)
        cur = t // SEL_BLOCK
        eligible = blk * SEL_BLOCK <= t
        forced = (blk == 0) | (blk == cur) | (blk == cur - 1)
        score = jnp.where(eligible, p_sel + jnp.where(forced, FORCE_SCORE, 0.0), NEG_INF)
        rank = jnp.zeros((n_sel, QB), F32)
        for j in range(n_sel):
            other = score[j:j + 1, :]
            ahead = (other > score) | ((other == score) & (blk > j))
            rank = rank + jnp.where(ahead, 1.0, 0.0)
        chosen = jnp.where(eligible & (rank < SEL_TOPK), 1.0, 0.0)
        chosen = jnp.concatenate([chosen, jnp.zeros((QB - n_sel, QB), F32)], axis=0)
        sel_ref[0, 0, rs, :] = chosen.T.astype(BF16)


def _nsa_cmp(pb, ucmp, pos2, cmp_w, bias_c, overlap_t, batch, seq):
    nqb = seq // CMP_TQ
    n_cmp = (seq - CMP_BLOCK) // CMP_STRIDE + 1
    n_sel = seq // SEL_BLOCK
    ng = seq // CMP_STRIDE
    assert ng == QB and n_sel <= QB and n_sel % 8 == 0
    wide = CMP_STRIDE * HEAD_DIM
    return pl.pallas_call(
        functools.partial(_nsa_cmp_kernel, n_cmp=n_cmp, n_sel=n_sel),
        grid=(batch, NSA_KV_HEADS, nqb),
        in_specs=[
            pl.BlockSpec((CMP_TQ, NSA_GQA * 128), lambda b, h, i: (b * nqb + i, h)),
            pl.BlockSpec((1, 1, ng, wide), lambda b, h, i: (b, h, 0, 0)),
            pl.BlockSpec((1, 1, ng, wide), lambda b, h, i: (b, NSA_KV_HEADS + h, 0, 0)),
            pl.BlockSpec((2, 2, wide), lambda b, h, i: (0, 0, 0)),
            pl.BlockSpec((2, 2 * wide, HEAD_DIM), lambda b, h, i: (0, 0, 0)),
            *[pl.BlockSpec((None, CMP_TQ, QB), lambda b, h, i, g=g: (h * NSA_GQA + g, i, 0))
              for g in range(NSA_GQA)],
            pl.BlockSpec((QB, QB), lambda b, h, i: (0, 0)),
        ],
        out_specs=[
            pl.BlockSpec((CMP_TQ, 256), lambda b, h, i: (b * nqb + i, h)),
            pl.BlockSpec((1, 1, CMP_TQ, QB), lambda b, h, i: (b, h, i, 0)),
        ],
        out_shape=[
            jax.ShapeDtypeStruct((batch * seq, NSA_WIDTH), F32),
            jax.ShapeDtypeStruct((batch, NSA_KV_HEADS, seq, QB), BF16),
        ],
        scratch_shapes=[pltpu.VMEM((QB, 128), BF16), pltpu.VMEM((2 * QB, 128), BF16)],
        compiler_params=_params(("parallel", "parallel", "arbitrary")),
        name="nsa_compressed",
    )(pb, ucmp, ucmp, pos2, cmp_w, *[bias_c] * NSA_GQA, overlap_t)


SW_TQ = 2 * QB
SEL_CHUNK = 4 * QB
WIN_FAR = WINDOW - QB


def _nsa_sw_kernel(*refs, n_cast):
    n_in = 8
    _cast_slabs(refs[n_in:n_in + n_cast], refs[n_in + n_cast + 1:n_in + 2 * n_cast + 1])
    _nsa_sw_body(*refs[:n_in], refs[n_in + n_cast], *refs[n_in + 2 * n_cast + 1:])


def _nsa_sw_body(q_ref, ks_ref, kw_ref, ocmp_ref, misc_ref, sel_ref, exp_ref, tb_ref, o_ref, madd_ref):
    i = pl.program_id(2)
    G = NSA_GQA
    TQ = SW_TQ
    q4 = jnp.concatenate([q_ref[:, g * 128:(g + 1) * 128] for g in range(G)], axis=0)
    madd_ref[...] = (_dot(sel_ref[0, 0], exp_ref[...]) - 1.0) * (-NEG_INF)
    t_pos = i * TQ + lax.broadcasted_iota(jnp.int32, (TQ, 1), 0)

    low = lax.broadcasted_iota(jnp.int32, (QB, 128), 1) < HEAD_DIM

    def tile(x, n=G):
        return jnp.concatenate([x] * n, axis=0)

    def ones_v(kv):
        return jnp.where(tile(low, kv.shape[0] // QB), jnp.ones_like(kv), kv)

    first = 2 * i - 1
    rows_m = pl.ds(pl.multiple_of(jnp.maximum(first, 0) * QB, QB), QB)
    rows_a = pl.ds(pl.multiple_of(2 * i * QB, QB), QB)
    rows_b = pl.ds(pl.multiple_of((2 * i + 1) * QB, QB), QB)
    gone = jnp.where(i >= 1, 0.0, NEG_INF)
    gone_m = jnp.concatenate([jnp.full((TQ, QB), gone, F32), jnp.zeros((TQ, 2 * QB), F32)], axis=1)
    far_end = first * QB

    kv = jnp.concatenate([ks_ref[rows_m, :], ks_ref[rows_a, :], ks_ref[rows_b, :]], axis=0)
    near_sel = jnp.concatenate([madd_ref[:, rows_m], madd_ref[:, rows_a], madd_ref[:, rows_b]], axis=1)
    s = _dot_nt(q4, kv) + (tb_ref[0] + tile(near_sel + gone_m))
    m = jnp.max(s, axis=-1, keepdims=True)
    acc = _dot(jnp.exp(s - m).astype(BF16), ones_v(kv))

    def sel_far(cidx, state):
        m, acc = state
        cols = pl.ds(pl.multiple_of(cidx * SEL_CHUNK, SEL_CHUNK), SEL_CHUNK)
        k_pos = cidx * SEL_CHUNK + lax.broadcasted_iota(jnp.int32, (TQ, SEL_CHUNK), 1)
        add = jnp.where(k_pos < far_end, madd_ref[:, cols], NEG_INF)
        kv = ks_ref[cols, :]
        s = _dot_nt(q4, kv) + tile(add)
        m_new = jnp.maximum(m, jnp.max(s, axis=-1, keepdims=True))
        return m_new, jnp.exp(m - m_new) * acc + _dot(jnp.exp(s - m_new).astype(BF16), ones_v(kv))

    _, sel_acc = lax.fori_loop(0, (2 * i + 2) // (SEL_CHUNK // QB), sel_far, (m, acc))

    start = jnp.maximum(first - WIN_FAR // QB, 0) * QB
    rows_far = pl.ds(pl.multiple_of(start, QB), WIN_FAR)
    k_pos = start + lax.broadcasted_iota(jnp.int32, (TQ, WIN_FAR), 1)
    live = (k_pos < far_end) & (k_pos > t_pos - WINDOW)
    kv = jnp.concatenate([kw_ref[rows_far, :], kw_ref[rows_m, :], kw_ref[rows_a, :], kw_ref[rows_b, :]], axis=0)
    bias = jnp.concatenate([tile(jnp.where(live, 0.0, NEG_INF)), tb_ref[0] + tile(gone_m)], axis=1)
    s = _dot_nt(q4, kv) + bias
    win_acc = _dot(jnp.exp(s - jnp.max(s, axis=-1, keepdims=True)).astype(BF16), ones_v(kv))

    gate = _sigmoid(misc_ref[...])
    low = tile(low, TQ // QB)

    def pair(acc, pp):
        a, b = acc[2 * pp * TQ:(2 * pp + 1) * TQ], acc[(2 * pp + 1) * TQ:(2 * pp + 2) * TQ]
        return (jnp.where(low, pltpu.roll(a, HEAD_DIM, 1), b)
                / jnp.where(low, a, pltpu.roll(b, HEAD_DIM, 1)))

    def pair_gate(branch, pp):
        ca, cb = 3 * (2 * pp) + branch, 3 * (2 * pp + 1) + branch
        return jnp.where(low, gate[:, ca:ca + 1], gate[:, cb:cb + 1])

    for pp in range(G // 2):
        lanes = slice(pp * 128, (pp + 1) * 128)
        o_ref[:, lanes] = (pair_gate(0, pp) * ocmp_ref[:, lanes] + pair_gate(1, pp) * pair(sel_acc, pp)
                           + pair_gate(2, pp) * pair(win_acc, pp))


def _nsa_sel_win(pb, pf, o_cmp, sel, expand, tb, batch, seq, cast=(), cast_layer=()):
    nqb = seq // SW_TQ
    G = NSA_GQA
    c_in, c_out, c_shapes = _cast_specs(cast, cast_layer, batch * NSA_KV_HEADS * nqb,
                                        lambda b, h, i: (b * NSA_KV_HEADS + h) * nqb + i)
    out = pl.pallas_call(
        functools.partial(_nsa_sw_kernel, n_cast=len(cast)),
        grid=(batch, NSA_KV_HEADS, nqb),
        in_specs=[
            pl.BlockSpec((SW_TQ, G * 128), lambda b, h, i: (b * nqb + i, h)),
            pl.BlockSpec((seq, 128), lambda b, h, i: (b, PB_SEL // 128 + h)),
            pl.BlockSpec((seq, 128), lambda b, h, i: (b, PB_WIN // 128 + h)),
            pl.BlockSpec((SW_TQ, 256), lambda b, h, i: (b * nqb + i, h)),
            pl.BlockSpec((SW_TQ, 128), lambda b, h, i: (b * nqb + i, PF_MISC // 128 + h)),
            pl.BlockSpec((1, 1, SW_TQ, QB), lambda b, h, i: (b, h, i, 0)),
            pl.BlockSpec((QB, seq), lambda b, h, i: (0, 0)),
            pl.BlockSpec((1, G * SW_TQ, 3 * QB), lambda b, h, i: (h, 0, 0)),
        ] + c_in,
        out_specs=[pl.BlockSpec((SW_TQ, 256), lambda b, h, i: (b * nqb + i, h))] + c_out,
        out_shape=[jax.ShapeDtypeStruct((batch * seq, NSA_WIDTH), F32)] + c_shapes,
        scratch_shapes=[pltpu.VMEM((SW_TQ, seq), F32)],
        compiler_params=_params(("parallel", "parallel", "arbitrary")),
        name="nsa_selected_window",
    )(pb, pb, pb, o_cmp, pf, sel, expand, tb, *cast)
    return out[0], out[1:]


def _ssd_kernel(*refs, n_cast):
    n_in = 12
    _cast_slabs(refs[n_in:n_in + n_cast], refs[n_in + n_cast + 1:n_in + 2 * n_cast + 1])
    _ssd_body(*refs[:n_in], refs[n_in + n_cast], *refs[n_in + 2 * n_cast + 1:])


def _ssd_body(z_ref, xs_ref, bc_ref, misc_ref, cwx_ref, cwb_ref, cbx_ref, cbb_ref, hp_ref, spread_ref,
              dskip_ref, ng_ref, o_ref, xbuf, bbuf, state):
    ci = pl.program_id(1)
    L = QB
    P = HEAD_DIM
    GN = SSD_GROUPS * SSD_STATE

    @pl.when(ci == 0)
    def _():
        xbuf[...] = jnp.zeros_like(xbuf)
        bbuf[...] = jnp.zeros_like(bbuf)
        state[...] = jnp.zeros_like(state)

    def conv_silu(buf, src_ref, w_ref, b_ref):
        cur = src_ref[...]
        ext = jnp.concatenate([buf[...], cur], axis=0)
        out = b_ref[...] + w_ref[SSD_CONV - 1:SSD_CONV, :] * cur
        for k in range(1, SSD_CONV):
            out = out + w_ref[SSD_CONV - 1 - k:SSD_CONV - k, :] * pltpu.roll(ext, k, 0)[8:8 + L]
        buf[...] = cur[L - 8:L]
        return _silu(out)

    xs = conv_silu(xbuf, xs_ref, cwx_ref, cbx_ref)
    bcs = conv_silu(bbuf, bc_ref, cwb_ref, cbb_ref)

    misc = misc_ref[...]
    pre = misc + hp_ref[0:1, :]
    dt = jnp.maximum(pre, 0.0) + jnp.log1p(jnp.exp(-jnp.abs(pre)))
    a_dt = dt * (-jnp.exp(hp_ref[1:2, :]))
    r = lax.broadcasted_iota(jnp.int32, (L, L), 0)
    c = lax.broadcasted_iota(jnp.int32, (L, L), 1)
    causal = r >= c
    a_cs = _split3_dot_left(causal.astype(BF16), a_dt)
    a_cs_t = a_cs.T

    dt_full = _split3_dot(dt, spread_ref[...])
    acs_full = _split3_dot(a_cs, spread_ref[...])
    total_full = acs_full[L - 1:L, :]
    xdt = xs * dt_full
    xdt16 = xdt.astype(BF16)
    to_end16 = (xdt * jnp.exp(total_full - acs_full)).astype(BF16)
    decay_in = jnp.exp(acs_full)
    chunk_decay = jnp.exp(total_full)
    low = c < P
    zero16 = jnp.zeros((L, 2 * P), BF16)
    width = SSD_HPG * P

    ys = []
    for g in range(SSD_GROUPS):
        bm = bcs[:, g * SSD_STATE:(g + 1) * SSD_STATE]
        cm16 = bcs[:, GN + g * SSD_STATE:GN + (g + 1) * SSD_STATE].astype(BF16)
        cb = _dot_nt(cm16, bm.astype(BF16))
        lanes = slice(g * width, (g + 1) * width)
        h_in = state[g]
        y_off = _dot(cm16, h_in.astype(BF16)) * decay_in[:, lanes]
        state[g] = h_in * chunk_decay[:, lanes] + _dot(bm.T.astype(BF16), to_end16[:, lanes])
        diag = []
        for pp in range(SSD_HPG // 2):
            hd = g * SSD_HPG + 2 * pp
            decayed = []
            for col in (MISC_DT + hd, MISC_DT + hd + 1):
                seg = jnp.exp(jnp.where(causal, a_cs[:, col:col + 1] - a_cs_t[col:col + 1, :], -jnp.inf))
                decayed.append((cb * seg).astype(BF16))
            xp = xdt16[:, hd * P:(hd + 2) * P]
            x_bd = jnp.concatenate([jnp.where(low, xp, zero16), jnp.where(low, zero16, xp)], axis=0)
            diag.append(_dot(jnp.concatenate(decayed, axis=1), x_bd))
        ys.append(jnp.concatenate(diag, axis=1) + y_off)

    y = (jnp.concatenate(ys, axis=1) + dskip_ref[...] * xs) * _silu(z_ref[...])
    for g in range(SSD_GROUPS):
        lanes = slice(g * width, (g + 1) * width)
        yg = y[:, lanes]
        o_ref[:, lanes] = yg * lax.rsqrt(jnp.mean(yg * yg, axis=-1, keepdims=True) + EPS) * ng_ref[:, lanes]


def _ssd(pf, conv_w, conv_b, head_params, d_skip, norm_g, batch, seq, cast=(), cast_layer=()):
    nc = seq // QB
    c_in, c_out, c_shapes = _cast_specs(cast, cast_layer, batch * nc, lambda b, ci: b * nc + ci)
    GN = SSD_GROUPS * SSD_STATE
    lane_head = jnp.arange(SSD_INNER)[None, :] // HEAD_DIM
    spread = (jnp.arange(128)[:, None] == MISC_DT + lane_head).astype(BF16)
    cwx, cwb = conv_w[:, :SSD_INNER], conv_w[:, SSD_INNER:]
    cbx, cbb = conv_b[:SSD_INNER].reshape(1, -1), conv_b[SSD_INNER:].reshape(1, -1)
    full = lambda shape: pl.BlockSpec(shape, lambda b, ci: (0,) * len(shape))
    out = pl.pallas_call(
        functools.partial(_ssd_kernel, n_cast=len(cast)),
        grid=(batch, nc),
        in_specs=[
            pl.BlockSpec((QB, SSD_INNER), lambda b, ci: (b * nc + ci, PF_Z // SSD_INNER)),
            pl.BlockSpec((QB, SSD_INNER), lambda b, ci: (b * nc + ci, PF_XS // SSD_INNER)),
            pl.BlockSpec((QB, 2 * GN), lambda b, ci: (b * nc + ci, PF_BC // (2 * GN))),
            pl.BlockSpec((QB, 128), lambda b, ci: (b * nc + ci, PF_MISC // 128)),
            full((SSD_CONV, SSD_INNER)),
            full((SSD_CONV, 2 * GN)),
            full((1, SSD_INNER)),
            full((1, 2 * GN)),
            full((8, 128)),
            full((128, SSD_INNER)),
            full((1, SSD_INNER)),
            full((1, SSD_INNER)),
        ] + c_in,
        out_specs=[pl.BlockSpec((QB, SSD_INNER), lambda b, ci: (b * nc + ci, 0))] + c_out,
        out_shape=[jax.ShapeDtypeStruct((batch * seq, SSD_INNER), F32)] + c_shapes,
        scratch_shapes=[
            pltpu.VMEM((8, SSD_INNER), F32),
            pltpu.VMEM((8, 2 * GN), F32),
            pltpu.VMEM((SSD_GROUPS, SSD_STATE, SSD_HPG * HEAD_DIM), F32),
        ],
        compiler_params=_params(("parallel", "arbitrary")),
        name="ssd",
    )(pf, pf, pf, pf, cwx, cwb, cbx, cbb, head_params, spread,
      jnp.repeat(d_skip.astype(F32), HEAD_DIM).reshape(1, -1), norm_g.reshape(1, -1), *cast)
    return out[0], out[1:]


def _out_proj_kernel(x_ref, nsa_ref, sb_ref, ssd_ref, gn_ref, gs_ref, w_ref, o_ref, mix_ref):
    @pl.when(pl.program_id(1) == 0)
    def _():
        mix_ref[:, 0:NSA_WIDTH] = _rms(nsa_ref[...], gn_ref[...]).astype(BF16)
        mix_ref[:, NSA_WIDTH:NSA_WIDTH + SB_WIDTH] = _rms(sb_ref[...], gs_ref[...]).astype(BF16)
        mix_ref[:, NSA_WIDTH + SB_WIDTH:] = ssd_ref[...].astype(BF16)

    o_ref[...] = x_ref[...] + _dot(mix_ref[...], w_ref[...])


def _out_proj(x, o_nsa, o_sb, o_ssd, g_nsa, g_sb, w, *, tm=1024, tn=1024):
    n, d = x.shape
    dm = w.shape[0]
    return pl.pallas_call(
        _out_proj_kernel,
        grid=(n // tm, d // tn),
        in_specs=[
            pl.BlockSpec((tm, tn), lambda i, j: (i, j)),
            pl.BlockSpec((tm, NSA_WIDTH), lambda i, j: (i, 0)),
            pl.BlockSpec((tm, SB_WIDTH), lambda i, j: (i, 0)),
            pl.BlockSpec((tm, SSD_INNER), lambda i, j: (i, 0)),
            pl.BlockSpec((1, NSA_WIDTH), lambda i, j: (0, 0)),
            pl.BlockSpec((1, SB_WIDTH), lambda i, j: (0, 0)),
            pl.BlockSpec((dm, tn), lambda i, j: (0, j)),
        ],
        out_specs=pl.BlockSpec((tm, tn), lambda i, j: (i, j)),
        out_shape=jax.ShapeDtypeStruct((n, d), F32),
        scratch_shapes=[pltpu.VMEM((tm, dm), BF16)],
        compiler_params=_params(("parallel", "arbitrary")),
        name="out_proj",
    )(x, o_nsa, o_sb, o_ssd, g_nsa.reshape(1, -1), g_sb.reshape(1, -1), w)


def _rel_bucket(dist):
    dist = jnp.maximum(dist, 0)
    max_exact = REL_BUCKETS // 2
    log_ratio = jnp.log(jnp.maximum(dist, 1).astype(F32) / max_exact) / math.log(REL_MAX_DIST / max_exact)
    large = jnp.minimum(max_exact + (log_ratio * (REL_BUCKETS - max_exact)).astype(jnp.int32), REL_BUCKETS - 1)
    return jnp.where(dist < max_exact, dist, large)


def _bias_tables(rel_bias, seq):
    assert QB >= REL_MAX_DIST
    def lookup(dist):
        onehot = jax.nn.one_hot(_rel_bucket(dist), REL_BUCKETS, dtype=F32)
        return jnp.einsum('...k,kh->h...', onehot, rel_bias, precision=lax.Precision.HIGHEST)

    t = jnp.arange(seq)[:, None]
    cend = jnp.arange(QB)[None, :] * CMP_STRIDE + CMP_BLOCK - 1
    n_cmp = (seq - CMP_BLOCK) // CMP_STRIDE + 1
    valid_c = (t >= cend) & (jnp.arange(QB)[None, :] < n_cmp)
    bias_c = jnp.where(valid_c, lookup(t - cend), NEG_INF)
    r = jnp.arange(QB)[:, None]
    m = jnp.arange(QB)[None, :]
    near = jnp.stack([lookup(r - m), lookup(QB + r - m)])
    near = near - rel_bias[REL_BUCKETS - 1][None, :, None, None]
    diag = near[0] + jnp.where(m > r, NEG_INF, 0.0)
    prev = near[1]
    rows_a = jnp.concatenate([prev, diag, jnp.full_like(diag, NEG_INF)], axis=2)
    rows_b = jnp.concatenate([jnp.zeros_like(diag), prev, diag], axis=2)
    near2 = jnp.concatenate([rows_a, rows_b], axis=1)
    return bias_c, near2.reshape(NSA_KV_HEADS, NSA_GQA * SW_TQ, 3 * QB)


def _expand_table(seq):
    j = jnp.arange(QB)[:, None]
    s = jnp.arange(seq)[None, :]
    return (s // SEL_BLOCK == j).astype(BF16)


def _overlap_table(seq):
    n_cmp = (seq - CMP_BLOCK) // CMP_STRIDE + 1
    cs = jnp.arange(QB)[:, None] * CMP_STRIDE
    ce = cs + CMP_BLOCK - 1
    ss = jnp.arange(QB)[None, :] * SEL_BLOCK
    ov = jnp.maximum(jnp.minimum(ce, ss + SEL_BLOCK - 1) - jnp.maximum(cs, ss) + 1, 0).astype(F32) / CMP_BLOCK
    keep = (jnp.arange(QB)[:, None] < n_cmp) & (jnp.arange(QB)[None, :] < seq // SEL_BLOCK)
    return jnp.where(keep, ov, 0.0).T.astype(BF16)


def _in_proj_weights(w):
    w = w.astype(BF16)
    scale = HEAD_DIM ** -0.5
    o_q, o_kv, o_gate = 0, NSA_WIDTH, NSA_WIDTH + 768
    o_sb = o_gate + 3 * NSA_HEADS
    o_z = o_sb + 3 * SB_WIDTH
    o_xbc = o_z + SSD_INNER
    o_dt = o_xbc + SSD_INNER + 2 * SSD_GROUPS * SSD_STATE
    col = lambda a, n: w[:, a:a + n]
    kv = lambda br, which, h: col(o_kv + br * 256 + which * 128 + h * HEAD_DIM, HEAD_DIM)
    zeros = lambda n: jnp.zeros((w.shape[0], n), w.dtype)
    branch = lambda br: [kv(br, 0, 0), kv(br, 1, 0), kv(br, 0, 1), kv(br, 1, 1)]
    q_heads = []
    for h in range(NSA_HEADS):
        q_heads += [col(o_q + h * HEAD_DIM, HEAD_DIM) * scale, zeros(128 - HEAD_DIM)]
    wb = jnp.concatenate(
        q_heads + branch(1) + branch(2)
        + [col(o_sb, SB_WIDTH) * scale, col(o_sb + SB_WIDTH, 2 * SB_WIDTH)], axis=1)
    gates = 3 * NSA_GQA
    wf = jnp.concatenate(
        [col(o_z, SSD_INNER), col(o_xbc, SSD_INNER + 2 * SSD_GROUPS * SSD_STATE), col(o_kv, 256),
         col(o_gate, gates), zeros(MISC_DT - gates), col(o_dt, SSD_HEADS), zeros(128 - MISC_DT - SSD_HEADS),
         col(o_gate + gates, gates), zeros(128 - gates)], axis=1)
    assert wb.shape[1] == PB_COLS and wf.shape[1] == PF_COLS
    return wb.astype(BF16), wf.astype(BF16)


def _head_params(dt_bias, a_log):
    rows = jnp.stack([dt_bias, a_log]).astype(F32)
    return jnp.zeros((8, 128), F32).at[0:2, MISC_DT:MISC_DT + SSD_HEADS].set(rows)


def _mixer(x, l, tables, batch, seq, mix_norm, w_in, w_out, cmp_pos, cmp_w, nsa_norm, sb_norm,
           conv_w, conv_b, dt_bias, a_log, d_skip, ssd_norm, casts):
    hosts = [([w for w, _ in casts[k::3]], tuple(at for _, at in casts[k::3])) for k in range(3)]
    bias_c, near, overlap, expand = tables
    wb, wf = _in_proj_weights(w_in[l])
    pb = _rms_matmul(x, mix_norm[l], wb, BF16, name="in_proj_bf16")
    pf = _rms_matmul(x, mix_norm[l], wf, F32, name="in_proj_f32")

    ng = seq // CMP_STRIDE
    ucmp = pf[:, PF_CMP:PF_CMP + 256].reshape(batch, ng, CMP_STRIDE, 4, HEAD_DIM)
    ucmp = ucmp.transpose(0, 3, 1, 2, 4).reshape(batch, 4, ng, CMP_STRIDE * HEAD_DIM)
    pos2 = cmp_pos[l].reshape(2, 2, CMP_STRIDE * HEAD_DIM)
    o_cmp, sel = _nsa_cmp(pb, ucmp, pos2, cmp_w[l].astype(BF16), bias_c, overlap, batch, seq)
    o_nsa, w0 = _nsa_sel_win(pb, pf, o_cmp, sel, expand, near, batch, seq, *hosts[0])
    o_sb, w1 = _sb_attention(pb, batch, seq, *hosts[1])
    o_ssd, w2 = _ssd(pf, conv_w[l], conv_b[l], _head_params(dt_bias[l], a_log[l]), d_skip[l], ssd_norm[l],
                     batch, seq, *hosts[2])
    out = _out_proj(x, o_nsa, o_sb, o_ssd, nsa_norm[l], sb_norm[l], w_out[l].astype(BF16))
    done = [None] * len(casts)
    for k, ws in enumerate((w0, w1, w2)):
        done[k::3] = ws
    return out, done


def kernel(x, rel_bias, ffn1_norm, ffn1_w_gate, ffn1_w_up, ffn1_w_down, mix_norm, w_in, w_out, nsa_cmp_pos, nsa_cmp_w, nsa_out_norm, sb_out_norm, ssd_conv_w, ssd_conv_b, ssd_dt_bias, ssd_a_log, ssd_d, ssd_out_norm, ffn2_norm, ffn2_w_gate, ffn2_w_up, ffn2_w_down, final_norm):
    batch, seq, d = x.shape
    depth = w_in.shape[0]
    tables = _bias_tables(rel_bias, seq) + (_overlap_table(seq), _expand_table(seq))
    h = x.reshape(batch * seq, d)

    ffn1 = (ffn1_w_gate, ffn1_w_up, ffn1_w_down)
    ffn2 = (ffn2_w_gate, ffn2_w_up, ffn2_w_down)
    widen = lambda ws: tuple(w[None] for w in ws)
    w1 = widen(_cast_layer(ffn1, 0))
    for l in range(depth):
        h = _ffn(h, ffn1_norm[l], *w1, 0, final_norm, final=False)
        casts = [(w, l) for w in ffn2] + ([(w, l + 1) for w in ffn1] if l + 1 < depth else [])
        h, done = _mixer(h, l, tables, batch, seq, mix_norm, w_in, w_out, nsa_cmp_pos, nsa_cmp_w,
                         nsa_out_norm, sb_out_norm, ssd_conv_w, ssd_conv_b, ssd_dt_bias, ssd_a_log,
                         ssd_d, ssd_out_norm, casts)
        h = _ffn(h, ffn2_norm[l], *widen(done[0:3]), 0, final_norm, final=(l == depth - 1))
        w1 = widen(done[3:6])
    return h.reshape(batch, seq, d)
```

```python
import functools
import math

import jax
import jax.numpy as jnp
from jax import lax
from jax.experimental import pallas as pl
from jax.experimental.pallas import tpu as pltpu

D_MODEL = 2048
D_FF = 5632
HEAD_DIM = 64
QB = 128
NEG_INF = -1e30
EPS = 1e-6

NSA_HEADS = 8
NSA_KV_HEADS = 2
NSA_GQA = NSA_HEADS // NSA_KV_HEADS
NSA_WIDTH = NSA_HEADS * HEAD_DIM
CMP_BLOCK = 32
CMP_STRIDE = 16
SEL_BLOCK = 64
SEL_TOPK = 8
FORCE_SCORE = 1e3
WINDOW = 512
SB_HEADS = 8
SB_WIDTH = SB_HEADS * HEAD_DIM
SSD_HEADS = 16
SSD_INNER = SSD_HEADS * HEAD_DIM
SSD_GROUPS = 2
SSD_HPG = SSD_HEADS // SSD_GROUPS
SSD_STATE = 128
SSD_CONV = 4
REL_BUCKETS = 32
REL_MAX_DIST = 128

PB_NSA_Q, PB_SEL, PB_WIN, PB_SB_Q, PB_SB_K, PB_SB_V, PB_COLS = 0, 1024, 1280, 1536, 2048, 2560, 3072
PF_Z, PF_XS, PF_BC, PF_CMP, PF_MISC, PF_COLS = 0, 1024, 2048, 2560, 2816, 3072
MISC_DT = 16

VMEM_LIMIT = 52 * 1024 * 1024

BF16 = jnp.bfloat16
F32 = jnp.float32


def _dot(a, b):
    return jnp.dot(a, b, preferred_element_type=F32)


def _dot_nt(a, b):
    return lax.dot_general(a, b, (((1,), (1,)), ((), ())), preferred_element_type=F32)


def _split2_dot(x, m):
    hi = x.astype(BF16)
    lo = (x - hi.astype(F32)).astype(BF16)
    return _dot(hi, m) + _dot(lo, m)


def _split3_dot(x, m):
    hi = x.astype(BF16)
    r1 = x - hi.astype(F32)
    mid = r1.astype(BF16)
    lo = (r1 - mid.astype(F32)).astype(BF16)
    return _dot(hi, m) + _dot(mid, m) + _dot(lo, m)


def _split3_dot_left(m, x):
    hi = x.astype(BF16)
    r1 = x - hi.astype(F32)
    mid = r1.astype(BF16)
    lo = (r1 - mid.astype(F32)).astype(BF16)
    return _dot(m, hi) + _dot(m, mid) + _dot(m, lo)


def _rms(x, g):
    return x * lax.rsqrt(jnp.mean(x * x, axis=-1, keepdims=True) + EPS) * g


def _silu(x):
    return x / (1.0 + jnp.exp(-x))


def _sigmoid(x):
    return 1.0 / (1.0 + jnp.exp(-x))


def _params(sem):
    return pltpu.CompilerParams(dimension_semantics=sem, vmem_limit_bytes=VMEM_LIMIT)


def _cast_specs(weights, layer, steps, step_of):
    in_specs, out_specs, out_shapes = [], [], []
    for w, at in zip(weights, layer):
        _, rows, cols = w.shape
        share = 1
        while (rows * share) % (16 * steps):
            share *= 2
        slab = rows * share // steps
        in_specs.append(pl.BlockSpec((None, slab, cols),
                                     lambda *ids, k=share, at=at: (at, step_of(*ids) // k, 0)))
        out_specs.append(pl.BlockSpec((slab, cols), lambda *ids, k=share: (step_of(*ids) // k, 0)))
        out_shapes.append(jax.ShapeDtypeStruct((rows, cols), BF16))
    return in_specs, out_specs, out_shapes


def _cast_slabs(srcs, dsts):
    for src, dst in zip(srcs, dsts):
        dst[...] = src[...].astype(BF16)


def _cast_kernel(*refs):
    _cast_slabs(refs[:len(refs) // 2], refs[len(refs) // 2:])


def _cast_layer(weights, layer, steps=16):
    c_in, c_out, c_shapes = _cast_specs(weights, (layer,) * len(weights), steps, lambda s: s)
    return pl.pallas_call(_cast_kernel, grid=(steps,), in_specs=c_in, out_specs=c_out, out_shape=c_shapes,
                          compiler_params=_params(("parallel",)), name="cast_weights")(*weights)


def _ffn_kernel(x_ref, g_ref, wg_ref, wu_ref, wd_ref, fg_ref, o_ref, h_ref, acc_ref, *, final):
    f = pl.program_id(1)

    @pl.when(f == 0)
    def _():
        h_ref[...] = _rms(x_ref[...], g_ref[...]).astype(BF16)
        acc_ref[...] = jnp.zeros_like(acc_ref)

    h = h_ref[...]
    gate = _dot(h, wg_ref[...])
    up = _dot(h, wu_ref[...])
    acc_ref[...] += _dot((_silu(gate) * up).astype(BF16), wd_ref[...])

    @pl.when(f == pl.num_programs(1) - 1)
    def _():
        y = x_ref[...] + 0.5 * acc_ref[...]
        if final:
            y = _rms(y, fg_ref[...])
        o_ref[...] = y


def _ffn(x, g, wg, wu, wd, layer, final_g, *, final, tm=512, tf=512):
    n, d = x.shape
    dff = wg.shape[2]
    return pl.pallas_call(
        functools.partial(_ffn_kernel, final=final),
        grid=(n // tm, dff // tf),
        in_specs=[
            pl.BlockSpec((tm, d), lambda i, f: (i, 0)),
            pl.BlockSpec((1, d), lambda i, f: (0, 0)),
            pl.BlockSpec((None, d, tf), lambda i, f: (layer, 0, f)),
            pl.BlockSpec((None, d, tf), lambda i, f: (layer, 0, f)),
            pl.BlockSpec((None, tf, d), lambda i, f: (layer, f, 0)),
            pl.BlockSpec((1, d), lambda i, f: (0, 0)),
        ],
        out_specs=pl.BlockSpec((tm, d), lambda i, f: (i, 0)),
        out_shape=jax.ShapeDtypeStruct((n, d), F32),
        scratch_shapes=[pltpu.VMEM((tm, d), BF16), pltpu.VMEM((tm, d), F32)],
        compiler_params=_params(("parallel", "arbitrary")),
        name="ffn",
    )(x, g.reshape(1, d), wg, wu, wd, final_g.reshape(1, d))


def _rms_matmul_kernel(x_ref, g_ref, w_ref, o_ref, h_ref):
    @pl.when(pl.program_id(1) == 0)
    def _():
        h_ref[...] = _rms(x_ref[...], g_ref[...]).astype(BF16)

    o_ref[...] = _dot(h_ref[...], w_ref[...]).astype(o_ref.dtype)


def _rms_matmul(x, g, w, out_dtype, *, tm=1024, tn=1024, name):
    n, d = x.shape
    c = w.shape[1]
    return pl.pallas_call(
        _rms_matmul_kernel,
        grid=(n // tm, c // tn),
        in_specs=[
            pl.BlockSpec((tm, d), lambda i, j: (i, 0)),
            pl.BlockSpec((1, d), lambda i, j: (0, 0)),
            pl.BlockSpec((d, tn), lambda i, j: (0, j)),
        ],
        out_specs=pl.BlockSpec((tm, tn), lambda i, j: (i, j)),
        out_shape=jax.ShapeDtypeStruct((n, c), out_dtype),
        scratch_shapes=[pltpu.VMEM((tm, d), BF16)],
        compiler_params=_params(("parallel", "arbitrary")),
        name=name,
    )(x, g.reshape(1, d), w)


SB_GROUP = 8
SB_DEAD = 104.0


def _sb_kernel(*refs, n_cast):
    q_ref, k_ref, v_ref = refs[:3]
    o_ref = refs[3 + n_cast]
    kbd_ref, vbd_ref = refs[4 + 2 * n_cast:]
    _cast_slabs(refs[3:3 + n_cast], refs[4 + n_cast:4 + 2 * n_cast])
    _sb_body(q_ref, k_ref, v_ref, o_ref, kbd_ref, vbd_ref)


def _sb_body(q_ref, k_ref, v_ref, o_ref, kbd_ref, vbd_ref):
    i = pl.program_id(2)
    pairs = SB_GROUP // 2
    nkb = k_ref.shape[0] // QB
    r = lax.broadcasted_iota(jnp.int32, (QB, QB), 0)
    c = lax.broadcasted_iota(jnp.int32, (QB, QB), 1)
    strict_all = jnp.concatenate([c < r] * SB_GROUP, axis=0)
    later_ones = jnp.concatenate([(r > c).astype(BF16), jnp.ones((QB, QB), BF16)], axis=1)
    later_ones = jnp.concatenate([later_ones, later_ones], axis=0)

    @pl.when(i == 0)
    def _():
        low = c < HEAD_DIM
        zero16 = jnp.zeros((QB, QB), BF16)

        def build(j, carry):
            src = pl.ds(pl.multiple_of(j * QB, QB), QB)
            dst = pl.ds(pl.multiple_of(j * 2 * QB, 2 * QB), 2 * QB)
            for pp in range(pairs):
                for ref, out in ((k_ref, kbd_ref), (v_ref, vbd_ref)):
                    x = ref[src, pp * 128:(pp + 1) * 128]
                    out[pp, dst, :] = jnp.concatenate([jnp.where(low, x, zero16), jnp.where(low, zero16, x)],
                                                      axis=0)
            return carry

        lax.fori_loop(0, nkb, build, 0)

    def scores(j, diagonal):
        rows = pl.ds(pl.multiple_of(j * 2 * QB, 2 * QB), 2 * QB)
        zs = []
        for pp in range(pairs):
            z_pair = _dot_nt(q_ref[:, pp * 128:(pp + 1) * 128], kbd_ref[pp, rows, :])
            zs += [z_pair[:, 0:QB], z_pair[:, QB:2 * QB]]
        z = jnp.concatenate(zs, axis=0)
        soft = jnp.maximum(z, 0.0) + jnp.log(1.0 + jnp.exp(-jnp.abs(z)))
        if diagonal:
            soft = jnp.where(strict_all, soft, 0.0)
        hi = soft.astype(BF16)
        lo = (soft - hi.astype(F32)).astype(BF16)
        sums = _dot(jnp.concatenate([hi, lo], axis=1), later_ones)
        return rows, z - soft, sums

    def absorb(tails, accs, rows, log_beta, sums, diagonal):
        a = jnp.exp(log_beta - (tails + sums[:, 0:QB]))
        if diagonal:
            a = jnp.where(strict_all, a, 0.0)
        a = a.astype(BF16)
        outs = []
        for pp in range(pairs):
            pair = jnp.concatenate([a[2 * pp * QB:(2 * pp + 1) * QB], a[(2 * pp + 1) * QB:(2 * pp + 2) * QB]],
                                   axis=1)
            outs.append(_dot(pair, vbd_ref[pp, rows, :]))
        return tails + sums[:, QB:2 * QB], accs + jnp.concatenate(outs, axis=0)

    def sweep(js, carry):
        staged = [scores(j, False) for j in js]
        for st in staged:
            carry = absorb(*carry, *st, False)
        return carry

    carry = (jnp.zeros((SB_GROUP * QB, QB), F32), jnp.zeros((pairs * QB, 128), F32))
    carry = absorb(*carry, *scores(i, True), True)
    odd = i % 2
    carry = lax.fori_loop(0, odd, lambda step, cr: sweep([i - 1], cr), carry)
    top = i - 1 - odd

    def live(state):
        step, smallest, _, _ = state
        return (step < i // 2) & (smallest < SB_DEAD)

    def pair_step(state):
        step, _, tails, accs = state
        tails, accs = sweep([top - 2 * step, top - 2 * step - 1], (tails, accs))
        return step + 1, jnp.min(tails), tails, accs

    accs = lax.while_loop(live, pair_step, (0, jnp.min(carry[0]), *carry))[3]
    for pp in range(pairs):
        o_ref[:, pp * 128:(pp + 1) * 128] = accs[pp * QB:(pp + 1) * QB]


def _sb_attention(pb, batch, seq, cast=(), cast_layer=()):
    nqb = seq // QB
    w = SB_GROUP * HEAD_DIM
    assert SB_HEADS == SB_GROUP
    qc, kc, vc = PB_SB_Q // w, PB_SB_K // w, PB_SB_V // w
    c_in, c_out, c_shapes = _cast_specs(cast, cast_layer, batch * nqb, lambda b, p, i: b * nqb + i)
    out = pl.pallas_call(
        functools.partial(_sb_kernel, n_cast=len(cast)),
        grid=(batch, SB_HEADS // SB_GROUP, nqb),
        in_specs=[
            pl.BlockSpec((QB, w), lambda b, p, i: (b * nqb + i, qc + p)),
            pl.BlockSpec((seq, w), lambda b, p, i: (b, kc + p)),
            pl.BlockSpec((seq, w), lambda b, p, i: (b, vc + p)),
        ] + c_in,
        out_specs=[pl.BlockSpec((QB, w), lambda b, p, i: (b * nqb + i, p))] + c_out,
        out_shape=[jax.ShapeDtypeStruct((batch * seq, SB_WIDTH), F32)] + c_shapes,
        scratch_shapes=[pltpu.VMEM((SB_GROUP // 2, 2 * seq, 128), BF16),
                        pltpu.VMEM((SB_GROUP // 2, 2 * seq, 128), BF16)],
        compiler_params=_params(("parallel", "parallel", "arbitrary")),
        name="sb_attention",
    )(pb, pb, pb, *cast)
    return out[0], out[1:]


CMP_TQ = 4 * QB


def _nsa_cmp_kernel(q_ref, uk_ref, uv_ref, pos_ref, w_ref, b0_ref, b1_ref, b2_ref, b3_ref, ovt_ref,
                    ocmp_ref, sel_ref, kc_ref, vc_ref, *, n_cmp, n_sel):
    i = pl.program_id(2)
    half = CMP_STRIDE * HEAD_DIM

    @pl.when(i == 0)
    def _():
        def compress(kv, u_ref):
            u = u_ref[0, 0]
            top = _dot((u + pos_ref[kv, 0:1, :]).astype(BF16), w_ref[kv, 0:half, :])
            bot = _dot((u + pos_ref[kv, 1:2, :]).astype(BF16), w_ref[kv, half:2 * half, :])
            return top + pltpu.roll(bot, QB - 1, 0)

        zeros = jnp.zeros((QB, HEAD_DIM), F32)
        kc_ref[...] = jnp.concatenate([compress(0, uk_ref), zeros], axis=1).astype(BF16)
        v = compress(1, uv_ref)
        vc_ref[...] = jnp.concatenate([jnp.concatenate([v, zeros], axis=1),
                                       jnp.concatenate([zeros, v], axis=1)], axis=0).astype(BF16)

    G = NSA_GQA
    TQ = CMP_TQ
    q4 = jnp.concatenate([q_ref[:, g * 128:(g + 1) * 128] for g in range(G)], axis=0)
    bias = jnp.concatenate([b[...] for b in (b0_ref, b1_ref, b2_ref, b3_ref)], axis=0)
    s = _dot_nt(q4, kc_ref[...]) + bias
    e = jnp.exp(s - jnp.max(s, axis=-1, keepdims=True))
    t_rows = i * TQ + lax.broadcasted_iota(jnp.int32, (TQ, 1), 0)
    any_valid = jnp.concatenate([t_rows >= CMP_BLOCK - 1] * G, axis=0)
    p = jnp.where(any_valid, e / jnp.sum(e, axis=-1, keepdims=True), 0.0)
    p16 = p.astype(BF16)
    for pp in range(G // 2):
        pair = jnp.concatenate([p16[2 * pp * TQ:(2 * pp + 1) * TQ], p16[(2 * pp + 1) * TQ:(2 * pp + 2) * TQ]],
                               axis=1)
        ocmp_ref[:, pp * 128:(pp + 1) * 128] = _dot(pair, vc_ref[...])
    p_all = p[0:TQ]
    for g in range(1, G):
        p_all = p_all + p[g * TQ:(g + 1) * TQ]

    blk = lax.broadcasted_iota(jnp.int32, (n_sel, QB), 0)
    for part in range(TQ // QB):
        rs = slice(part * QB, (part + 1) * QB)
        t0 = (i * (TQ // QB) + part) * QB
        p_sum = p_all[rs]
        hi = p_sum.astype(BF16)
        lo = (p_sum - hi.astype(F32)).astype(BF16)
        p_sel = (_dot_nt(ovt_ref[...], hi) + _dot_nt(ovt_ref[...], lo))[0:n_sel]
        t = t0 + lax.broadcasted_iota(jnp.int32, (n_sel, QB), 1)
        cur = t // SEL_BLOCK
        eligible = blk * SEL_BLOCK <= t
        forced = (blk == 0) | (blk == cur) | (blk == cur - 1)
        score = jnp.where(eligible, p_sel + jnp.where(forced, FORCE_SCORE, 0.0), NEG_INF)
        rank = jnp.zeros((n_sel, QB), F32)
        for j in range(n_sel):
            other = score[j:j + 1, :]
            ahead = (other > score) | ((other == score) & (blk > j))
            rank = rank + jnp.where(ahead, 1.0, 0.0)
        chosen = jnp.where(eligible & (rank < SEL_TOPK), 1.0, 0.0)
        chosen = jnp.concatenate([chosen, jnp.zeros((QB - n_sel, QB), F32)], axis=0)
        sel_ref[0, 0, rs, :] = chosen.T.astype(BF16)


def _nsa_cmp(pb, ucmp, pos2, cmp_w, bias_c, overlap_t, batch, seq):
    nqb = seq // CMP_TQ
    n_cmp = (seq - CMP_BLOCK) // CMP_STRIDE + 1
    n_sel = seq // SEL_BLOCK
    ng = seq // CMP_STRIDE
    assert ng == QB and n_sel <= QB and n_sel % 8 == 0
    wide = CMP_STRIDE * HEAD_DIM
    return pl.pallas_call(
        functools.partial(_nsa_cmp_kernel, n_cmp=n_cmp, n_sel=n_sel),
        grid=(batch, NSA_KV_HEADS, nqb),
        in_specs=[
            pl.BlockSpec((CMP_TQ, NSA_GQA * 128), lambda b, h, i: (b * nqb + i, h)),
            pl.BlockSpec((1, 1, ng, wide), lambda b, h, i: (b, h, 0, 0)),
            pl.BlockSpec((1, 1, ng, wide), lambda b, h, i: (b, NSA_KV_HEADS + h, 0, 0)),
            pl.BlockSpec((2, 2, wide), lambda b, h, i: (0, 0, 0)),
            pl.BlockSpec((2, 2 * wide, HEAD_DIM), lambda b, h, i: (0, 0, 0)),
            *[pl.BlockSpec((None, CMP_TQ, QB), lambda b, h, i, g=g: (h * NSA_GQA + g, i, 0))
              for g in range(NSA_GQA)],
            pl.BlockSpec((QB, QB), lambda b, h, i: (0, 0)),
        ],
        out_specs=[
            pl.BlockSpec((CMP_TQ, 256), lambda b, h, i: (b * nqb + i, h)),
            pl.BlockSpec((1, 1, CMP_TQ, QB), lambda b, h, i: (b, h, i, 0)),
        ],
        out_shape=[
            jax.ShapeDtypeStruct((batch * seq, NSA_WIDTH), F32),
            jax.ShapeDtypeStruct((batch, NSA_KV_HEADS, seq, QB), BF16),
        ],
        scratch_shapes=[pltpu.VMEM((QB, 128), BF16), pltpu.VMEM((2 * QB, 128), BF16)],
        compiler_params=_params(("parallel", "parallel", "arbitrary")),
        name="nsa_compressed",
    )(pb, ucmp, ucmp, pos2, cmp_w, *[bias_c] * NSA_GQA, overlap_t)


SW_TQ = 2 * QB
SEL_CHUNK = 4 * QB
WIN_FAR = WINDOW - QB


def _nsa_sw_kernel(*refs, n_cast):
    n_in = 8
    _cast_slabs(refs[n_in:n_in + n_cast], refs[n_in + n_cast + 1:n_in + 2 * n_cast + 1])
    _nsa_sw_body(*refs[:n_in], refs[n_in + n_cast], *refs[n_in + 2 * n_cast + 1:])


def _nsa_sw_body(q_ref, ks_ref, kw_ref, ocmp_ref, misc_ref, sel_ref, exp_ref, tb_ref, o_ref, madd_ref):
    i = pl.program_id(2)
    G = NSA_GQA
    TQ = SW_TQ
    q4 = jnp.concatenate([q_ref[:, g * 128:(g + 1) * 128] for g in range(G)], axis=0)
    madd_ref[...] = (_dot(sel_ref[0, 0], exp_ref[...]) - 1.0) * (-NEG_INF)
    t_pos = i * TQ + lax.broadcasted_iota(jnp.int32, (TQ, 1), 0)

    low = lax.broadcasted_iota(jnp.int32, (QB, 128), 1) < HEAD_DIM

    def tile(x, n=G):
        return jnp.concatenate([x] * n, axis=0)

    def ones_v(kv):
        return jnp.where(tile(low, kv.shape[0] // QB), jnp.ones_like(kv), kv)

    first = 2 * i - 1
    rows_m = pl.ds(pl.multiple_of(jnp.maximum(first, 0) * QB, QB), QB)
    rows_a = pl.ds(pl.multiple_of(2 * i * QB, QB), QB)
    rows_b = pl.ds(pl.multiple_of((2 * i + 1) * QB, QB), QB)
    gone = jnp.where(i >= 1, 0.0, NEG_INF)
    gone_m = jnp.concatenate([jnp.full((TQ, QB), gone, F32), jnp.zeros((TQ, 2 * QB), F32)], axis=1)
    far_end = first * QB

    kv = jnp.concatenate([ks_ref[rows_m, :], ks_ref[rows_a, :], ks_ref[rows_b, :]], axis=0)
    near_sel = jnp.concatenate([madd_ref[:, rows_m], madd_ref[:, rows_a], madd_ref[:, rows_b]], axis=1)
    s = _dot_nt(q4, kv) + (tb_ref[0] + tile(near_sel + gone_m))
    m = jnp.max(s, axis=-1, keepdims=True)
    acc = _dot(jnp.exp(s - m).astype(BF16), ones_v(kv))

    def sel_far(cidx, state):
        m, acc = state
        cols = pl.ds(pl.multiple_of(cidx * SEL_CHUNK, SEL_CHUNK), SEL_CHUNK)
        k_pos = cidx * SEL_CHUNK + lax.broadcasted_iota(jnp.int32, (TQ, SEL_CHUNK), 1)
        add = jnp.where(k_pos < far_end, madd_ref[:, cols], NEG_INF)
        kv = ks_ref[cols, :]
        s = _dot_nt(q4, kv) + tile(add)
        m_new = jnp.maximum(m, jnp.max(s, axis=-1, keepdims=True))
        return m_new, jnp.exp(m - m_new) * acc + _dot(jnp.exp(s - m_new).astype(BF16), ones_v(kv))

    _, sel_acc = lax.fori_loop(0, (2 * i + 2) // (SEL_CHUNK // QB), sel_far, (m, acc))

    start = jnp.maximum(first - WIN_FAR // QB, 0) * QB
    rows_far = pl.ds(pl.multiple_of(start, QB), WIN_FAR)
    k_pos = start + lax.broadcasted_iota(jnp.int32, (TQ, WIN_FAR), 1)
    live = (k_pos < far_end) & (k_pos > t_pos - WINDOW)
    kv = jnp.concatenate([kw_ref[rows_far, :], kw_ref[rows_m, :], kw_ref[rows_a, :], kw_ref[rows_b, :]], axis=0)
    bias = jnp.concatenate([tile(jnp.where(live, 0.0, NEG_INF)), tb_ref[0] + tile(gone_m)], axis=1)
    s = _dot_nt(q4, kv) + bias
    win_acc = _dot(jnp.exp(s - jnp.max(s, axis=-1, keepdims=True)).astype(BF16), ones_v(kv))

    gate = _sigmoid(misc_ref[...])
    low = tile(low, TQ // QB)

    def pair(acc, pp):
        a, b = acc[2 * pp * TQ:(2 * pp + 1) * TQ], acc[(2 * pp + 1) * TQ:(2 * pp + 2) * TQ]
        return (jnp.where(low, pltpu.roll(a, HEAD_DIM, 1), b)
                / jnp.where(low, a, pltpu.roll(b, HEAD_DIM, 1)))

    def pair_gate(branch, pp):
        ca, cb = 3 * (2 * pp) + branch, 3 * (2 * pp + 1) + branch
        return jnp.where(low, gate[:, ca:ca + 1], gate[:, cb:cb + 1])

    for pp in range(G // 2):
        lanes = slice(pp * 128, (pp + 1) * 128)
        o_ref[:, lanes] = (pair_gate(0, pp) * ocmp_ref[:, lanes] + pair_gate(1, pp) * pair(sel_acc, pp)
                           + pair_gate(2, pp) * pair(win_acc, pp))


def _nsa_sel_win(pb, pf, o_cmp, sel, expand, tb, batch, seq, cast=(), cast_layer=()):
    nqb = seq // SW_TQ
    G = NSA_GQA
    c_in, c_out, c_shapes = _cast_specs(cast, cast_layer, batch * NSA_KV_HEADS * nqb,
                                        lambda b, h, i: (b * NSA_KV_HEADS + h) * nqb + i)
    out = pl.pallas_call(
        functools.partial(_nsa_sw_kernel, n_cast=len(cast)),
        grid=(batch, NSA_KV_HEADS, nqb),
        in_specs=[
            pl.BlockSpec((SW_TQ, G * 128), lambda b, h, i: (b * nqb + i, h)),
            pl.BlockSpec((seq, 128), lambda b, h, i: (b, PB_SEL // 128 + h)),
            pl.BlockSpec((seq, 128), lambda b, h, i: (b, PB_WIN // 128 + h)),
            pl.BlockSpec((SW_TQ, 256), lambda b, h, i: (b * nqb + i, h)),
            pl.BlockSpec((SW_TQ, 128), lambda b, h, i: (b * nqb + i, PF_MISC // 128 + h)),
            pl.BlockSpec((1, 1, SW_TQ, QB), lambda b, h, i: (b, h, i, 0)),
            pl.BlockSpec((QB, seq), lambda b, h, i: (0, 0)),
            pl.BlockSpec((1, G * SW_TQ, 3 * QB), lambda b, h, i: (h, 0, 0)),
        ] + c_in,
        out_specs=[pl.BlockSpec((SW_TQ, 256), lambda b, h, i: (b * nqb + i, h))] + c_out,
        out_shape=[jax.ShapeDtypeStruct((batch * seq, NSA_WIDTH), F32)] + c_shapes,
        scratch_shapes=[pltpu.VMEM((SW_TQ, seq), F32)],
        compiler_params=_params(("parallel", "parallel", "arbitrary")),
        name="nsa_selected_window",
    )(pb, pb, pb, o_cmp, pf, sel, expand, tb, *cast)
    return out[0], out[1:]


def _ssd_kernel(*refs, n_cast):
    n_in = 12
    _cast_slabs(refs[n_in:n_in + n_cast], refs[n_in + n_cast + 1:n_in + 2 * n_cast + 1])
    _ssd_body(*refs[:n_in], refs[n_in + n_cast], *refs[n_in + 2 * n_cast + 1:])


def _ssd_body(z_ref, xs_ref, bc_ref, misc_ref, cwx_ref, cwb_ref, cbx_ref, cbb_ref, hp_ref, spread_ref,
              dskip_ref, ng_ref, o_ref, xbuf, bbuf, state):
    ci = pl.program_id(1)
    L = QB
    P = HEAD_DIM
    GN = SSD_GROUPS * SSD_STATE

    @pl.when(ci == 0)
    def _():
        xbuf[...] = jnp.zeros_like(xbuf)
        bbuf[...] = jnp.zeros_like(bbuf)
        state[...] = jnp.zeros_like(state)

    def conv_silu(buf, src_ref, w_ref, b_ref):
        cur = src_ref[...]
        ext = jnp.concatenate([buf[...], cur], axis=0)
        out = b_ref[...] + w_ref[SSD_CONV - 1:SSD_CONV, :] * cur
        for k in range(1, SSD_CONV):
            out = out + w_ref[SSD_CONV - 1 - k:SSD_CONV - k, :] * pltpu.roll(ext, k, 0)[8:8 + L]
        buf[...] = cur[L - 8:L]
        return _silu(out)

    xs = conv_silu(xbuf, xs_ref, cwx_ref, cbx_ref)
    bcs = conv_silu(bbuf, bc_ref, cwb_ref, cbb_ref)

    misc = misc_ref[...]
    pre = misc + hp_ref[0:1, :]
    dt = jnp.maximum(pre, 0.0) + jnp.log1p(jnp.exp(-jnp.abs(pre)))
    a_dt = dt * (-jnp.exp(hp_ref[1:2, :]))
    r = lax.broadcasted_iota(jnp.int32, (L, L), 0)
    c = lax.broadcasted_iota(jnp.int32, (L, L), 1)
    causal = r >= c
    a_cs = _split3_dot_left(causal.astype(BF16), a_dt)
    a_cs_t = a_cs.T

    dt_full = _split3_dot(dt, spread_ref[...])
    acs_full = _split3_dot(a_cs, spread_ref[...])
    total_full = acs_full[L - 1:L, :]
    xdt = xs * dt_full
    xdt16 = xdt.astype(BF16)
    to_end16 = (xdt * jnp.exp(total_full - acs_full)).astype(BF16)
    decay_in = jnp.exp(acs_full)
    chunk_decay = jnp.exp(total_full)
    low = c < P
    zero16 = jnp.zeros((L, 2 * P), BF16)
    width = SSD_HPG * P

    ys = []
    for g in range(SSD_GROUPS):
        bm = bcs[:, g * SSD_STATE:(g + 1) * SSD_STATE]
        cm16 = bcs[:, GN + g * SSD_STATE:GN + (g + 1) * SSD_STATE].astype(BF16)
        cb = _dot_nt(cm16, bm.astype(BF16))
        lanes = slice(g * width, (g + 1) * width)
        h_in = state[g]
        y_off = _dot(cm16, h_in.astype(BF16)) * decay_in[:, lanes]
        state[g] = h_in * chunk_decay[:, lanes] + _dot(bm.T.astype(BF16), to_end16[:, lanes])
        diag = []
        for pp in range(SSD_HPG // 2):
            hd = g * SSD_HPG + 2 * pp
            decayed = []
            for col in (MISC_DT + hd, MISC_DT + hd + 1):
                seg = jnp.exp(jnp.where(causal, a_cs[:, col:col + 1] - a_cs_t[col:col + 1, :], -jnp.inf))
                decayed.append((cb * seg).astype(BF16))
            xp = xdt16[:, hd * P:(hd + 2) * P]
            x_bd = jnp.concatenate([jnp.where(low, xp, zero16), jnp.where(low, zero16, xp)], axis=0)
            diag.append(_dot(jnp.concatenate(decayed, axis=1), x_bd))
        ys.append(jnp.concatenate(diag, axis=1) + y_off)

    y = (jnp.concatenate(ys, axis=1) + dskip_ref[...] * xs) * _silu(z_ref[...])
    for g in range(SSD_GROUPS):
        lanes = slice(g * width, (g + 1) * width)
        yg = y[:, lanes]
        o_ref[:, lanes] = yg * lax.rsqrt(jnp.mean(yg * yg, axis=-1, keepdims=True) + EPS) * ng_ref[:, lanes]


def _ssd(pf, conv_w, conv_b, head_params, d_skip, norm_g, batch, seq, cast=(), cast_layer=()):
    nc = seq // QB
    c_in, c_out, c_shapes = _cast_specs(cast, cast_layer, batch * nc, lambda b, ci: b * nc + ci)
    GN = SSD_GROUPS * SSD_STATE
    lane_head = jnp.arange(SSD_INNER)[None, :] // HEAD_DIM
    spread = (jnp.arange(128)[:, None] == MISC_DT + lane_head).astype(BF16)
    cwx, cwb = conv_w[:, :SSD_INNER], conv_w[:, SSD_INNER:]
    cbx, cbb = conv_b[:SSD_INNER].reshape(1, -1), conv_b[SSD_INNER:].reshape(1, -1)
    full = lambda shape: pl.BlockSpec(shape, lambda b, ci: (0,) * len(shape))
    out = pl.pallas_call(
        functools.partial(_ssd_kernel, n_cast=len(cast)),
        grid=(batch, nc),
        in_specs=[
            pl.BlockSpec((QB, SSD_INNER), lambda b, ci: (b * nc + ci, PF_Z // SSD_INNER)),
            pl.BlockSpec((QB, SSD_INNER), lambda b, ci: (b * nc + ci, PF_XS // SSD_INNER)),
            pl.BlockSpec((QB, 2 * GN), lambda b, ci: (b * nc + ci, PF_BC // (2 * GN))),
            pl.BlockSpec((QB, 128), lambda b, ci: (b * nc + ci, PF_MISC // 128)),
            full((SSD_CONV, SSD_INNER)),
            full((SSD_CONV, 2 * GN)),
            full((1, SSD_INNER)),
            full((1, 2 * GN)),
            full((8, 128)),
            full((128, SSD_INNER)),
            full((1, SSD_INNER)),
            full((1, SSD_INNER)),
        ] + c_in,
        out_specs=[pl.BlockSpec((QB, SSD_INNER), lambda b, ci: (b * nc + ci, 0))] + c_out,
        out_shape=[jax.ShapeDtypeStruct((batch * seq, SSD_INNER), F32)] + c_shapes,
        scratch_shapes=[
            pltpu.VMEM((8, SSD_INNER), F32),
            pltpu.VMEM((8, 2 * GN), F32),
            pltpu.VMEM((SSD_GROUPS, SSD_STATE, SSD_HPG * HEAD_DIM), F32),
        ],
        compiler_params=_params(("parallel", "arbitrary")),
        name="ssd",
    )(pf, pf, pf, pf, cwx, cwb, cbx, cbb, head_params, spread,
      jnp.repeat(d_skip.astype(F32), HEAD_DIM).reshape(1, -1), norm_g.reshape(1, -1), *cast)
    return out[0], out[1:]


def _out_proj_kernel(x_ref, nsa_ref, sb_ref, ssd_ref, gn_ref, gs_ref, w_ref, o_ref, mix_ref):
    @pl.when(pl.program_id(1) == 0)
    def _():
        mix_ref[:, 0:NSA_WIDTH] = _rms(nsa_ref[...], gn_ref[...]).astype(BF16)
        mix_ref[:, NSA_WIDTH:NSA_WIDTH + SB_WIDTH] = _rms(sb_ref[...], gs_ref[...]).astype(BF16)
        mix_ref[:, NSA_WIDTH + SB_WIDTH:] = ssd_ref[...].astype(BF16)

    o_ref[...] = x_ref[...] + _dot(mix_ref[...], w_ref[...])


def _out_proj(x, o_nsa, o_sb, o_ssd, g_nsa, g_sb, w, *, tm=1024, tn=1024):
    n, d = x.shape
    dm = w.shape[0]
    return pl.pallas_call(
        _out_proj_kernel,
        grid=(n // tm, d // tn),
        in_specs=[
            pl.BlockSpec((tm, tn), lambda i, j: (i, j)),
            pl.BlockSpec((tm, NSA_WIDTH), lambda i, j: (i, 0)),
            pl.BlockSpec((tm, SB_WIDTH), lambda i, j: (i, 0)),
            pl.BlockSpec((tm, SSD_INNER), lambda i, j: (i, 0)),
            pl.BlockSpec((1, NSA_WIDTH), lambda i, j: (0, 0)),
            pl.BlockSpec((1, SB_WIDTH), lambda i, j: (0, 0)),
            pl.BlockSpec((dm, tn), lambda i, j: (0, j)),
        ],
        out_specs=pl.BlockSpec((tm, tn), lambda i, j: (i, j)),
        out_shape=jax.ShapeDtypeStruct((n, d), F32),
        scratch_shapes=[pltpu.VMEM((tm, dm), BF16)],
        compiler_params=_params(("parallel", "arbitrary")),
        name="out_proj",
    )(x, o_nsa, o_sb, o_ssd, g_nsa.reshape(1, -1), g_sb.reshape(1, -1), w)


def _rel_bucket(dist):
    dist = jnp.maximum(dist, 0)
    max_exact = REL_BUCKETS // 2
    log_ratio = jnp.log(jnp.maximum(dist, 1).astype(F32) / max_exact) / math.log(REL_MAX_DIST / max_exact)
    large = jnp.minimum(max_exact + (log_ratio * (REL_BUCKETS - max_exact)).astype(jnp.int32), REL_BUCKETS - 1)
    return jnp.where(dist < max_exact, dist, large)


def _bias_tables(rel_bias, seq):
    assert QB >= REL_MAX_DIST
    def lookup(dist):
        buckets = jnp.arange(REL_BUCKETS).reshape((-1,) + (1,) * dist.ndim)
        onehot = (_rel_bucket(dist)[None] == buckets).astype(F32)
        return jnp.einsum('kh,k...->h...', rel_bias, onehot, precision=lax.Precision.HIGHEST)

    t = jnp.arange(seq)[:, None]
    cend = jnp.arange(QB)[None, :] * CMP_STRIDE + CMP_BLOCK - 1
    n_cmp = (seq - CMP_BLOCK) // CMP_STRIDE + 1
    valid_c = (t >= cend) & (jnp.arange(QB)[None, :] < n_cmp)
    bias_c = jnp.where(valid_c, lookup(t - cend), NEG_INF)
    r = jnp.arange(QB)[:, None]
    m = jnp.arange(QB)[None, :]
    near = jnp.stack([lookup(r - m), lookup(QB + r - m)])
    near = near - rel_bias[REL_BUCKETS - 1][None, :, None, None]
    diag = near[0] + jnp.where(m > r, NEG_INF, 0.0)
    prev = near[1]
    rows_a = jnp.concatenate([prev, diag, jnp.full_like(diag, NEG_INF)], axis=2)
    rows_b = jnp.concatenate([jnp.zeros_like(diag), prev, diag], axis=2)
    near2 = jnp.concatenate([rows_a, rows_b], axis=1)
    return bias_c, near2.reshape(NSA_KV_HEADS, NSA_GQA * SW_TQ, 3 * QB)


def _expand_table(seq):
    j = jnp.arange(QB)[:, None]
    s = jnp.arange(seq)[None, :]
    return (s // SEL_BLOCK == j).astype(BF16)


def _overlap_table(seq):
    n_cmp = (seq - CMP_BLOCK) // CMP_STRIDE + 1
    cs = jnp.arange(QB)[:, None] * CMP_STRIDE
    ce = cs + CMP_BLOCK - 1
    ss = jnp.arange(QB)[None, :] * SEL_BLOCK
    ov = jnp.maximum(jnp.minimum(ce, ss + SEL_BLOCK - 1) - jnp.maximum(cs, ss) + 1, 0).astype(F32) / CMP_BLOCK
    keep = (jnp.arange(QB)[:, None] < n_cmp) & (jnp.arange(QB)[None, :] < seq // SEL_BLOCK)
    return jnp.where(keep, ov, 0.0).T.astype(BF16)


def _in_proj_weights(w):
    w = w.astype(BF16)
    scale = HEAD_DIM ** -0.5
    o_q, o_kv, o_gate = 0, NSA_WIDTH, NSA_WIDTH + 768
    o_sb = o_gate + 3 * NSA_HEADS
    o_z = o_sb + 3 * SB_WIDTH
    o_xbc = o_z + SSD_INNER
    o_dt = o_xbc + SSD_INNER + 2 * SSD_GROUPS * SSD_STATE
    col = lambda a, n: w[:, a:a + n]
    kv = lambda br, which, h: col(o_kv + br * 256 + which * 128 + h * HEAD_DIM, HEAD_DIM)
    zeros = lambda n: jnp.zeros((w.shape[0], n), w.dtype)
    branch = lambda br: [kv(br, 0, 0), kv(br, 1, 0), kv(br, 0, 1), kv(br, 1, 1)]
    q_heads = []
    for h in range(NSA_HEADS):
        q_heads += [col(o_q + h * HEAD_DIM, HEAD_DIM) * scale, zeros(128 - HEAD_DIM)]
    wb = jnp.concatenate(
        q_heads + branch(1) + branch(2)
        + [col(o_sb, SB_WIDTH) * scale, col(o_sb + SB_WIDTH, 2 * SB_WIDTH)], axis=1)
    gates = 3 * NSA_GQA
    wf = jnp.concatenate(
        [col(o_z, SSD_INNER), col(o_xbc, SSD_INNER + 2 * SSD_GROUPS * SSD_STATE), col(o_kv, 256),
         col(o_gate, gates), zeros(MISC_DT - gates), col(o_dt, SSD_HEADS), zeros(128 - MISC_DT - SSD_HEADS),
         col(o_gate + gates, gates), zeros(128 - gates)], axis=1)
    assert wb.shape[1] == PB_COLS and wf.shape[1] == PF_COLS
    return wb.astype(BF16), wf.astype(BF16)


def _head_params(dt_bias, a_log):
    rows = jnp.stack([dt_bias, a_log]).astype(F32)
    return jnp.zeros((8, 128), F32).at[0:2, MISC_DT:MISC_DT + SSD_HEADS].set(rows)


def _mixer(x, l, tables, batch, seq, mix_norm, w_in, w_out, cmp_pos, cmp_w, nsa_norm, sb_norm,
           conv_w, conv_b, dt_bias, a_log, d_skip, ssd_norm, casts):
    hosts = [([w for w, _ in casts[k::3]], tuple(at for _, at in casts[k::3])) for k in range(3)]
    bias_c, near, overlap, expand = tables
    wb, wf = _in_proj_weights(w_in[l])
    pb = _rms_matmul(x, mix_norm[l], wb, BF16, name="in_proj_bf16")
    pf = _rms_matmul(x, mix_norm[l], wf, F32, name="in_proj_f32")

    ng = seq // CMP_STRIDE
    ucmp = pf[:, PF_CMP:PF_CMP + 256].reshape(batch, ng, CMP_STRIDE, 4, HEAD_DIM)
    ucmp = ucmp.transpose(0, 3, 1, 2, 4).reshape(batch, 4, ng, CMP_STRIDE * HEAD_DIM)
    pos2 = cmp_pos[l].reshape(2, 2, CMP_STRIDE * HEAD_DIM)
    o_cmp, sel = _nsa_cmp(pb, ucmp, pos2, cmp_w[l].astype(BF16), bias_c, overlap, batch, seq)
    o_nsa, w0 = _nsa_sel_win(pb, pf, o_cmp, sel, expand, near, batch, seq, *hosts[0])
    o_sb, w1 = _sb_attention(pb, batch, seq, *hosts[1])
    o_ssd, w2 = _ssd(pf, conv_w[l], conv_b[l], _head_params(dt_bias[l], a_log[l]), d_skip[l], ssd_norm[l],
                     batch, seq, *hosts[2])
    out = _out_proj(x, o_nsa, o_sb, o_ssd, nsa_norm[l], sb_norm[l], w_out[l].astype(BF16))
    done = [None] * len(casts)
    for k, ws in enumerate((w0, w1, w2)):
        done[k::3] = ws
    return out, done


def kernel(x, rel_bias, ffn1_norm, ffn1_w_gate, ffn1_w_up, ffn1_w_down, mix_norm, w_in, w_out, nsa_cmp_pos, nsa_cmp_w, nsa_out_norm, sb_out_norm, ssd_conv_w, ssd_conv_b, ssd_dt_bias, ssd_a_log, ssd_d, ssd_out_norm, ffn2_norm, ffn2_w_gate, ffn2_w_up, ffn2_w_down, final_norm):
    batch, seq, d = x.shape
    depth = w_in.shape[0]
    tables = _bias_tables(rel_bias, seq) + (_overlap_table(seq), _expand_table(seq))
    h = x.reshape(batch * seq, d)

    ffn1 = (ffn1_w_gate, ffn1_w_up, ffn1_w_down)
    ffn2 = (ffn2_w_gate, ffn2_w_up, ffn2_w_down)
    widen = lambda ws: tuple(w[None] for w in ws)
    w1 = widen(_cast_layer(ffn1, 0))
    for l in range(depth):
        h = _ffn(h, ffn1_norm[l], *w1, 0, final_norm, final=False)
        casts = [(w, l) for w in ffn2] + ([(w, l + 1) for w in ffn1] if l + 1 < depth else [])
        h, done = _mixer(h, l, tables, batch, seq, mix_norm, w_in, w_out, nsa_cmp_pos, nsa_cmp_w,
                         nsa_out_norm, sb_out_norm, ssd_conv_w, ssd_conv_b, ssd_dt_bias, ssd_a_log,
                         ssd_d, ssd_out_norm, casts)
        h = _ffn(h, ffn2_norm[l], *widen(done[0:3]), 0, final_norm, final=(l == depth - 1))
        w1 = widen(done[3:6])
    return h.reshape(batch, seq, d)
```

```python
import functools
import math

import jax
import jax.numpy as jnp
from jax import lax
from jax.experimental import pallas as pl
from jax.experimental.pallas import tpu as pltpu

LANES = 128
HEAD_DIM = 64
QB = 128
NEG_INF = -1e30
EPS = 1e-6

NSA_HEADS = 8
NSA_KV_HEADS = 2
NSA_GQA = NSA_HEADS // NSA_KV_HEADS
NSA_WIDTH = NSA_HEADS * HEAD_DIM
CMP_BLOCK = 32
CMP_STRIDE = 16
SEL_BLOCK = 64
SEL_TOPK = 8
FORCE_SCORE = 1e3
WINDOW = 512
SB_HEADS = 8
SB_WIDTH = SB_HEADS * HEAD_DIM
SSD_HEADS = 16
SSD_INNER = SSD_HEADS * HEAD_DIM
SSD_GROUPS = 2
SSD_HPG = SSD_HEADS // SSD_GROUPS
SSD_STATE = 128
SSD_CONV = 4
REL_BUCKETS = 32
REL_MAX_DIST = 128

PB_NSA_Q, PB_SEL, PB_WIN, PB_SB_Q, PB_SB_K, PB_SB_V, PB_COLS = 0, 1024, 1280, 1536, 2048, 2560, 3072
PF_Z, PF_XS, PF_BC, PF_CMP, PF_MISC, PF_COLS = 0, 1024, 2048, 2560, 2816, 3072
MISC_DT = 16

VMEM_LIMIT = 52 * 1024 * 1024

BF16 = jnp.bfloat16
F32 = jnp.float32


def _dot(a, b):
    return jnp.dot(a, b, preferred_element_type=F32)


def _dot_nt(a, b):
    return lax.dot_general(a, b, (((1,), (1,)), ((), ())), preferred_element_type=F32)


def _split3_dot(x, m):
    hi = x.astype(BF16)
    r1 = x - hi.astype(F32)
    mid = r1.astype(BF16)
    lo = (r1 - mid.astype(F32)).astype(BF16)
    return _dot(hi, m) + _dot(mid, m) + _dot(lo, m)


def _split3_dot_left(m, x):
    hi = x.astype(BF16)
    r1 = x - hi.astype(F32)
    mid = r1.astype(BF16)
    lo = (r1 - mid.astype(F32)).astype(BF16)
    return _dot(m, hi) + _dot(m, mid) + _dot(m, lo)


def _rms(x, g):
    return x * lax.rsqrt(jnp.mean(x * x, axis=-1, keepdims=True) + EPS) * g


def _silu(x):
    return x / (1.0 + jnp.exp(-x))


def _sigmoid(x):
    return 1.0 / (1.0 + jnp.exp(-x))


def _params(sem):
    return pltpu.CompilerParams(dimension_semantics=sem, vmem_limit_bytes=VMEM_LIMIT)


def _cast_specs(weights, layer, steps, step_of):
    in_specs, out_specs, out_shapes = [], [], []
    for w, at in zip(weights, layer):
        _, rows, cols = w.shape
        share = 1
        while (rows * share) % (16 * steps):
            share *= 2
        slab = rows * share // steps
        in_specs.append(pl.BlockSpec((None, slab, cols),
                                     lambda *ids, k=share, at=at: (at, step_of(*ids) // k, 0)))
        out_specs.append(pl.BlockSpec((slab, cols), lambda *ids, k=share: (step_of(*ids) // k, 0)))
        out_shapes.append(jax.ShapeDtypeStruct((rows, cols), BF16))
    return in_specs, out_specs, out_shapes


def _cast_slabs(srcs, dsts):
    for src, dst in zip(srcs, dsts):
        dst[...] = src[...].astype(BF16)


def _cast_kernel(*refs):
    _cast_slabs(refs[:len(refs) // 2], refs[len(refs) // 2:])


def _cast_layer(weights, layer, steps=16):
    c_in, c_out, c_shapes = _cast_specs(weights, (layer,) * len(weights), steps, lambda s: s)
    return pl.pallas_call(_cast_kernel, grid=(steps,), in_specs=c_in, out_specs=c_out, out_shape=c_shapes,
                          compiler_params=_params(("parallel",)), name="cast_weights")(*weights)


def _ffn_kernel(x_ref, g_ref, wg_ref, wu_ref, wd_ref, fg_ref, o_ref, h_ref, acc_ref, *, final):
    f = pl.program_id(1)

    @pl.when(f == 0)
    def _():
        h_ref[...] = _rms(x_ref[...], g_ref[...]).astype(BF16)
        acc_ref[...] = jnp.zeros_like(acc_ref)

    h = h_ref[...]
    gate = _dot(h, wg_ref[...])
    up = _dot(h, wu_ref[...])
    acc_ref[...] += _dot((_silu(gate) * up).astype(BF16), wd_ref[...])

    @pl.when(f == pl.num_programs(1) - 1)
    def _():
        y = x_ref[...] + 0.5 * acc_ref[...]
        if final:
            y = _rms(y, fg_ref[...])
        o_ref[...] = y


def _ffn(x, g, wg, wu, wd, layer, final_g, *, final, tm=512, tf=512):
    n, d = x.shape
    dff = wg.shape[2]
    return pl.pallas_call(
        functools.partial(_ffn_kernel, final=final),
        grid=(n // tm, dff // tf),
        in_specs=[
            pl.BlockSpec((tm, d), lambda i, f: (i, 0)),
            pl.BlockSpec((1, d), lambda i, f: (0, 0)),
            pl.BlockSpec((None, d, tf), lambda i, f: (layer, 0, f)),
            pl.BlockSpec((None, d, tf), lambda i, f: (layer, 0, f)),
            pl.BlockSpec((None, tf, d), lambda i, f: (layer, f, 0)),
            pl.BlockSpec((1, d), lambda i, f: (0, 0)),
        ],
        out_specs=pl.BlockSpec((tm, d), lambda i, f: (i, 0)),
        out_shape=jax.ShapeDtypeStruct((n, d), F32),
        scratch_shapes=[pltpu.VMEM((tm, d), BF16), pltpu.VMEM((tm, d), F32)],
        compiler_params=_params(("parallel", "arbitrary")),
        name="ffn",
    )(x, g.reshape(1, d), wg, wu, wd, final_g.reshape(1, d))


def _rms_matmul_kernel(x_ref, g_ref, w_ref, o_ref, h_ref):
    @pl.when(pl.program_id(1) == 0)
    def _():
        h_ref[...] = _rms(x_ref[...], g_ref[...]).astype(BF16)

    o_ref[...] = _dot(h_ref[...], w_ref[...]).astype(o_ref.dtype)


def _rms_matmul(x, g, w, out_dtype, *, tm=1024, tn=1024, name):
    n, d = x.shape
    c = w.shape[1]
    return pl.pallas_call(
        _rms_matmul_kernel,
        grid=(n // tm, c // tn),
        in_specs=[
            pl.BlockSpec((tm, d), lambda i, j: (i, 0)),
            pl.BlockSpec((1, d), lambda i, j: (0, 0)),
            pl.BlockSpec((d, tn), lambda i, j: (0, j)),
        ],
        out_specs=pl.BlockSpec((tm, tn), lambda i, j: (i, j)),
        out_shape=jax.ShapeDtypeStruct((n, c), out_dtype),
        scratch_shapes=[pltpu.VMEM((tm, d), BF16)],
        compiler_params=_params(("parallel", "arbitrary")),
        name=name,
    )(x, g.reshape(1, d), w)


SB_GROUP = 8
SB_DEAD = 104.0


def _sb_kernel(*refs, n_cast):
    q_ref, k_ref, v_ref = refs[:3]
    o_ref = refs[3 + n_cast]
    kbd_ref, vbd_ref = refs[4 + 2 * n_cast:]
    _cast_slabs(refs[3:3 + n_cast], refs[4 + n_cast:4 + 2 * n_cast])
    _sb_body(q_ref, k_ref, v_ref, o_ref, kbd_ref, vbd_ref)


def _sb_body(q_ref, k_ref, v_ref, o_ref, kbd_ref, vbd_ref):
    i = pl.program_id(2)
    pairs = SB_GROUP // 2
    nkb = k_ref.shape[0] // QB
    r = lax.broadcasted_iota(jnp.int32, (QB, QB), 0)
    c = lax.broadcasted_iota(jnp.int32, (QB, QB), 1)
    strict_all = jnp.concatenate([c < r] * SB_GROUP, axis=0)
    later_ones = jnp.concatenate([(r > c).astype(BF16), jnp.ones((QB, QB), BF16)], axis=1)
    later_ones = jnp.concatenate([later_ones, later_ones], axis=0)

    @pl.when(i == 0)
    def _():
        low = c < HEAD_DIM
        zero16 = jnp.zeros((QB, QB), BF16)

        def build(j, carry):
            src = pl.ds(pl.multiple_of(j * QB, QB), QB)
            dst = pl.ds(pl.multiple_of(j * 2 * QB, 2 * QB), 2 * QB)
            for pp in range(pairs):
                for ref, out in ((k_ref, kbd_ref), (v_ref, vbd_ref)):
                    x = ref[src, pp * LANES:(pp + 1) * LANES]
                    out[pp, dst, :] = jnp.concatenate([jnp.where(low, x, zero16), jnp.where(low, zero16, x)],
                                                      axis=0)
            return carry

        lax.fori_loop(0, nkb, build, 0)

    def scores(j, diagonal):
        rows = pl.ds(pl.multiple_of(j * 2 * QB, 2 * QB), 2 * QB)
        zs = []
        for pp in range(pairs):
            z_pair = _dot_nt(q_ref[:, pp * LANES:(pp + 1) * LANES], kbd_ref[pp, rows, :])
            zs += [z_pair[:, 0:QB], z_pair[:, QB:2 * QB]]
        z = jnp.concatenate(zs, axis=0)
        soft = jnp.maximum(z, 0.0) + jnp.log(1.0 + jnp.exp(-jnp.abs(z)))
        if diagonal:
            soft = jnp.where(strict_all, soft, 0.0)
        hi = soft.astype(BF16)
        lo = (soft - hi.astype(F32)).astype(BF16)
        sums = _dot(jnp.concatenate([hi, lo], axis=1), later_ones)
        return rows, z - soft, sums

    def absorb(tails, accs, rows, log_beta, sums, diagonal):
        a = jnp.exp(log_beta - (tails + sums[:, 0:QB]))
        if diagonal:
            a = jnp.where(strict_all, a, 0.0)
        a = a.astype(BF16)
        outs = []
        for pp in range(pairs):
            pair = jnp.concatenate([a[2 * pp * QB:(2 * pp + 1) * QB], a[(2 * pp + 1) * QB:(2 * pp + 2) * QB]],
                                   axis=1)
            outs.append(_dot(pair, vbd_ref[pp, rows, :]))
        return tails + sums[:, QB:2 * QB], accs + jnp.concatenate(outs, axis=0)

    def sweep(js, carry):
        staged = [scores(j, False) for j in js]
        for st in staged:
            carry = absorb(*carry, *st, False)
        return carry

    carry = (jnp.zeros((SB_GROUP * QB, QB), F32), jnp.zeros((pairs * QB, LANES), F32))
    carry = absorb(*carry, *scores(i, True), True)
    odd = i % 2
    carry = lax.fori_loop(0, odd, lambda step, cr: sweep([i - 1], cr), carry)
    top = i - 1 - odd

    def live(state):
        step, smallest, _, _ = state
        return (step < i // 2) & (smallest < SB_DEAD)

    def pair_step(state):
        step, _, tails, accs = state
        tails, accs = sweep([top - 2 * step, top - 2 * step - 1], (tails, accs))
        return step + 1, jnp.min(tails), tails, accs

    accs = lax.while_loop(live, pair_step, (0, jnp.min(carry[0]), *carry))[3]
    for pp in range(pairs):
        o_ref[:, pp * LANES:(pp + 1) * LANES] = accs[pp * QB:(pp + 1) * QB]


def _sb_attention(pb, batch, seq, cast=(), cast_layer=()):
    nqb = seq // QB
    w = SB_GROUP * HEAD_DIM
    assert SB_HEADS == SB_GROUP
    qc, kc, vc = PB_SB_Q // w, PB_SB_K // w, PB_SB_V // w
    c_in, c_out, c_shapes = _cast_specs(cast, cast_layer, batch * nqb, lambda b, p, i: b * nqb + i)
    out = pl.pallas_call(
        functools.partial(_sb_kernel, n_cast=len(cast)),
        grid=(batch, SB_HEADS // SB_GROUP, nqb),
        in_specs=[
            pl.BlockSpec((QB, w), lambda b, p, i: (b * nqb + i, qc + p)),
            pl.BlockSpec((seq, w), lambda b, p, i: (b, kc + p)),
            pl.BlockSpec((seq, w), lambda b, p, i: (b, vc + p)),
        ] + c_in,
        out_specs=[pl.BlockSpec((QB, w), lambda b, p, i: (b * nqb + i, p))] + c_out,
        out_shape=[jax.ShapeDtypeStruct((batch * seq, SB_WIDTH), F32)] + c_shapes,
        scratch_shapes=[pltpu.VMEM((SB_GROUP // 2, 2 * seq, LANES), BF16),
                        pltpu.VMEM((SB_GROUP // 2, 2 * seq, LANES), BF16)],
        compiler_params=_params(("parallel", "parallel", "arbitrary")),
        name="sb_attention",
    )(pb, pb, pb, *cast)
    return out[0], out[1:]


CMP_TQ = 4 * QB


def _nsa_cmp_kernel(q_ref, uk_ref, uv_ref, pos_ref, w_ref, b0_ref, b1_ref, b2_ref, b3_ref, ovt_ref,
                    ocmp_ref, sel_ref, kc_ref, vc_ref, *, n_cmp, n_sel):
    i = pl.program_id(2)
    half = CMP_STRIDE * HEAD_DIM

    @pl.when(i == 0)
    def _():
        def compress(kv, u_ref):
            u = u_ref[0, 0]
            top = _dot((u + pos_ref[kv, 0:1, :]).astype(BF16), w_ref[kv, 0:half, :])
            bot = _dot((u + pos_ref[kv, 1:2, :]).astype(BF16), w_ref[kv, half:2 * half, :])
            return top + pltpu.roll(bot, QB - 1, 0)

        zeros = jnp.zeros((QB, HEAD_DIM), F32)
        kc_ref[...] = jnp.concatenate([compress(0, uk_ref), zeros], axis=1).astype(BF16)
        v = compress(1, uv_ref)
        vc_ref[...] = jnp.concatenate([jnp.concatenate([v, zeros], axis=1),
                                       jnp.concatenate([zeros, v], axis=1)], axis=0).astype(BF16)

    G = NSA_GQA
    TQ = CMP_TQ
    q4 = jnp.concatenate([q_ref[:, g * LANES:(g + 1) * LANES] for g in range(G)], axis=0)
    bias = jnp.concatenate([b[...] for b in (b0_ref, b1_ref, b2_ref, b3_ref)], axis=0)
    s = _dot_nt(q4, kc_ref[...]) + bias
    e = jnp.exp(s - jnp.max(s, axis=-1, keepdims=True))
    t_rows = i * TQ + lax.broadcasted_iota(jnp.int32, (TQ, 1), 0)
    any_valid = jnp.concatenate([t_rows >= CMP_BLOCK - 1] * G, axis=0)
    p = jnp.where(any_valid, e / jnp.sum(e, axis=-1, keepdims=True), 0.0)
    p16 = p.astype(BF16)
    for pp in range(G // 2):
        pair = jnp.concatenate([p16[2 * pp * TQ:(2 * pp + 1) * TQ], p16[(2 * pp + 1) * TQ:(2 * pp + 2) * TQ]],
                               axis=1)
        ocmp_ref[:, pp * LANES:(pp + 1) * LANES] = _dot(pair, vc_ref[...])
    p_all = p[0:TQ]
    for g in range(1, G):
        p_all = p_all + p[g * TQ:(g + 1) * TQ]

    blk = lax.broadcasted_iota(jnp.int32, (n_sel, QB), 0)
    for part in range(TQ // QB):
        rs = slice(part * QB, (part + 1) * QB)
        t0 = (i * (TQ // QB) + part) * QB
        p_sum = p_all[rs]
        hi = p_sum.astype(BF16)
        lo = (p_sum - hi.astype(F32)).astype(BF16)
        p_sel = (_dot_nt(ovt_ref[...], hi) + _dot_nt(ovt_ref[...], lo))[0:n_sel]
        t = t0 + lax.broadcasted_iota(jnp.int32, (n_sel, QB), 1)
        cur = t // SEL_BLOCK
        eligible = blk * SEL_BLOCK <= t
        forced = (blk == 0) | (blk == cur) | (blk == cur - 1)
        score = jnp.where(eligible, p_sel + jnp.where(forced, FORCE_SCORE, 0.0), NEG_INF)
        rank = jnp.zeros((n_sel, QB), F32)
        for j in range(n_sel):
            other = score[j:j + 1, :]
            ahead = (other > score) | ((other == score) & (blk > j))
            rank = rank + jnp.where(ahead, 1.0, 0.0)
        chosen = jnp.where(eligible & (rank < SEL_TOPK), 1.0, 0.0)
        chosen = jnp.concatenate([chosen, jnp.zeros((QB - n_sel, QB), F32)], axis=0)
        sel_ref[0, 0, rs, :] = chosen.T.astype(BF16)


def _nsa_cmp(pb, ucmp, pos2, cmp_w, bias_c, overlap_t, batch, seq):
    nqb = seq // CMP_TQ
    n_cmp = (seq - CMP_BLOCK) // CMP_STRIDE + 1
    n_sel = seq // SEL_BLOCK
    ng = seq // CMP_STRIDE
    assert ng == QB and n_sel <= QB and n_sel % 8 == 0
    wide = CMP_STRIDE * HEAD_DIM
    return pl.pallas_call(
        functools.partial(_nsa_cmp_kernel, n_cmp=n_cmp, n_sel=n_sel),
        grid=(batch, NSA_KV_HEADS, nqb),
        in_specs=[
            pl.BlockSpec((CMP_TQ, NSA_GQA * LANES), lambda b, h, i: (b * nqb + i, h)),
            pl.BlockSpec((1, 1, ng, wide), lambda b, h, i: (b, h, 0, 0)),
            pl.BlockSpec((1, 1, ng, wide), lambda b, h, i: (b, NSA_KV_HEADS + h, 0, 0)),
            pl.BlockSpec((2, 2, wide), lambda b, h, i: (0, 0, 0)),
            pl.BlockSpec((2, 2 * wide, HEAD_DIM), lambda b, h, i: (0, 0, 0)),
            *[pl.BlockSpec((None, CMP_TQ, QB), lambda b, h, i, g=g: (h * NSA_GQA + g, i, 0))
              for g in range(NSA_GQA)],
            pl.BlockSpec((QB, QB), lambda b, h, i: (0, 0)),
        ],
        out_specs=[
            pl.BlockSpec((CMP_TQ, 256), lambda b, h, i: (b * nqb + i, h)),
            pl.BlockSpec((1, 1, CMP_TQ, QB), lambda b, h, i: (b, h, i, 0)),
        ],
        out_shape=[
            jax.ShapeDtypeStruct((batch * seq, NSA_WIDTH), F32),
            jax.ShapeDtypeStruct((batch, NSA_KV_HEADS, seq, QB), BF16),
        ],
        scratch_shapes=[pltpu.VMEM((QB, LANES), BF16), pltpu.VMEM((2 * QB, LANES), BF16)],
        compiler_params=_params(("parallel", "parallel", "arbitrary")),
        name="nsa_compressed",
    )(pb, ucmp, ucmp, pos2, cmp_w, *[bias_c] * NSA_GQA, overlap_t)


SW_TQ = 2 * QB
SEL_CHUNK = 4 * QB
WIN_FAR = WINDOW - QB


def _nsa_sw_kernel(*refs, n_cast):
    n_in = 8
    _cast_slabs(refs[n_in:n_in + n_cast], refs[n_in + n_cast + 1:n_in + 2 * n_cast + 1])
    _nsa_sw_body(*refs[:n_in], refs[n_in + n_cast], *refs[n_in + 2 * n_cast + 1:])


def _nsa_sw_body(q_ref, ks_ref, kw_ref, ocmp_ref, misc_ref, sel_ref, exp_ref, tb_ref, o_ref, madd_ref):
    i = pl.program_id(2)
    G = NSA_GQA
    TQ = SW_TQ
    q4 = jnp.concatenate([q_ref[:, g * LANES:(g + 1) * LANES] for g in range(G)], axis=0)
    madd_ref[...] = (_dot(sel_ref[0, 0], exp_ref[...]) - 1.0) * (-NEG_INF)
    t_pos = i * TQ + lax.broadcasted_iota(jnp.int32, (TQ, 1), 0)

    low = lax.broadcasted_iota(jnp.int32, (QB, LANES), 1) < HEAD_DIM

    def tile(x, n=G):
        return jnp.concatenate([x] * n, axis=0)

    def ones_v(kv):
        return jnp.where(tile(low, kv.shape[0] // QB), jnp.ones_like(kv), kv)

    first = 2 * i - 1
    rows_m = pl.ds(pl.multiple_of(jnp.maximum(first, 0) * QB, QB), QB)
    rows_a = pl.ds(pl.multiple_of(2 * i * QB, QB), QB)
    rows_b = pl.ds(pl.multiple_of((2 * i + 1) * QB, QB), QB)
    gone = jnp.where(i >= 1, 0.0, NEG_INF)
    gone_m = jnp.concatenate([jnp.full((TQ, QB), gone, F32), jnp.zeros((TQ, 2 * QB), F32)], axis=1)
    far_end = first * QB

    kv = jnp.concatenate([ks_ref[rows_m, :], ks_ref[rows_a, :], ks_ref[rows_b, :]], axis=0)
    near_sel = jnp.concatenate([madd_ref[:, rows_m], madd_ref[:, rows_a], madd_ref[:, rows_b]], axis=1)
    s = _dot_nt(q4, kv) + (tb_ref[0] + tile(near_sel + gone_m))
    m = jnp.max(s, axis=-1, keepdims=True)
    acc = _dot(jnp.exp(s - m).astype(BF16), ones_v(kv))

    def sel_far(cidx, state):
        m, acc = state
        cols = pl.ds(pl.multiple_of(cidx * SEL_CHUNK, SEL_CHUNK), SEL_CHUNK)
        k_pos = cidx * SEL_CHUNK + lax.broadcasted_iota(jnp.int32, (TQ, SEL_CHUNK), 1)
        add = jnp.where(k_pos < far_end, madd_ref[:, cols], NEG_INF)
        kv = ks_ref[cols, :]
        s = _dot_nt(q4, kv) + tile(add)
        m_new = jnp.maximum(m, jnp.max(s, axis=-1, keepdims=True))
        return m_new, jnp.exp(m - m_new) * acc + _dot(jnp.exp(s - m_new).astype(BF16), ones_v(kv))

    _, sel_acc = lax.fori_loop(0, (2 * i + 2) // (SEL_CHUNK // QB), sel_far, (m, acc))

    start = jnp.maximum(first - WIN_FAR // QB, 0) * QB
    rows_far = pl.ds(pl.multiple_of(start, QB), WIN_FAR)
    k_pos = start + lax.broadcasted_iota(jnp.int32, (TQ, WIN_FAR), 1)
    live = (k_pos < far_end) & (k_pos > t_pos - WINDOW)
    kv = jnp.concatenate([kw_ref[rows_far, :], kw_ref[rows_m, :], kw_ref[rows_a, :], kw_ref[rows_b, :]], axis=0)
    bias = jnp.concatenate([tile(jnp.where(live, 0.0, NEG_INF)), tb_ref[0] + tile(gone_m)], axis=1)
    s = _dot_nt(q4, kv) + bias
    win_acc = _dot(jnp.exp(s - jnp.max(s, axis=-1, keepdims=True)).astype(BF16), ones_v(kv))

    gate = _sigmoid(misc_ref[...])
    low = tile(low, TQ // QB)

    def pair(acc, pp):
        a, b = acc[2 * pp * TQ:(2 * pp + 1) * TQ], acc[(2 * pp + 1) * TQ:(2 * pp + 2) * TQ]
        return (jnp.where(low, pltpu.roll(a, HEAD_DIM, 1), b)
                / jnp.where(low, a, pltpu.roll(b, HEAD_DIM, 1)))

    def pair_gate(branch, pp):
        ca, cb = 3 * (2 * pp) + branch, 3 * (2 * pp + 1) + branch
        return jnp.where(low, gate[:, ca:ca + 1], gate[:, cb:cb + 1])

    for pp in range(G // 2):
        lanes = slice(pp * LANES, (pp + 1) * LANES)
        o_ref[:, lanes] = (pair_gate(0, pp) * ocmp_ref[:, lanes] + pair_gate(1, pp) * pair(sel_acc, pp)
                           + pair_gate(2, pp) * pair(win_acc, pp))


def _nsa_sel_win(pb, pf, o_cmp, sel, expand, tb, batch, seq, cast=(), cast_layer=()):
    nqb = seq // SW_TQ
    G = NSA_GQA
    c_in, c_out, c_shapes = _cast_specs(cast, cast_layer, batch * NSA_KV_HEADS * nqb,
                                        lambda b, h, i: (b * NSA_KV_HEADS + h) * nqb + i)
    out = pl.pallas_call(
        functools.partial(_nsa_sw_kernel, n_cast=len(cast)),
        grid=(batch, NSA_KV_HEADS, nqb),
        in_specs=[
            pl.BlockSpec((SW_TQ, G * LANES), lambda b, h, i: (b * nqb + i, h)),
            pl.BlockSpec((seq, LANES), lambda b, h, i: (b, PB_SEL // LANES + h)),
            pl.BlockSpec((seq, LANES), lambda b, h, i: (b, PB_WIN // LANES + h)),
            pl.BlockSpec((SW_TQ, 256), lambda b, h, i: (b * nqb + i, h)),
            pl.BlockSpec((SW_TQ, LANES), lambda b, h, i: (b * nqb + i, PF_MISC // LANES + h)),
            pl.BlockSpec((1, 1, SW_TQ, QB), lambda b, h, i: (b, h, i, 0)),
            pl.BlockSpec((QB, seq), lambda b, h, i: (0, 0)),
            pl.BlockSpec((1, G * SW_TQ, 3 * QB), lambda b, h, i: (h, 0, 0)),
        ] + c_in,
        out_specs=[pl.BlockSpec((SW_TQ, 256), lambda b, h, i: (b * nqb + i, h))] + c_out,
        out_shape=[jax.ShapeDtypeStruct((batch * seq, NSA_WIDTH), F32)] + c_shapes,
        scratch_shapes=[pltpu.VMEM((SW_TQ, seq), F32)],
        compiler_params=_params(("parallel", "parallel", "arbitrary")),
        name="nsa_selected_window",
    )(pb, pb, pb, o_cmp, pf, sel, expand, tb, *cast)
    return out[0], out[1:]


def _ssd_kernel(*refs, n_cast):
    n_in = 12
    _cast_slabs(refs[n_in:n_in + n_cast], refs[n_in + n_cast + 1:n_in + 2 * n_cast + 1])
    _ssd_body(*refs[:n_in], refs[n_in + n_cast], *refs[n_in + 2 * n_cast + 1:])


def _ssd_body(z_ref, xs_ref, bc_ref, misc_ref, cwx_ref, cwb_ref, cbx_ref, cbb_ref, hp_ref, spread_ref,
              dskip_ref, ng_ref, o_ref, xbuf, bbuf, state):
    ci = pl.program_id(1)
    L = QB
    P = HEAD_DIM
    GN = SSD_GROUPS * SSD_STATE

    @pl.when(ci == 0)
    def _():
        xbuf[...] = jnp.zeros_like(xbuf)
        bbuf[...] = jnp.zeros_like(bbuf)
        state[...] = jnp.zeros_like(state)

    def conv_silu(buf, src_ref, w_ref, b_ref):
        cur = src_ref[...]
        ext = jnp.concatenate([buf[...], cur], axis=0)
        out = b_ref[...] + w_ref[SSD_CONV - 1:SSD_CONV, :] * cur
        for k in range(1, SSD_CONV):
            out = out + w_ref[SSD_CONV - 1 - k:SSD_CONV - k, :] * pltpu.roll(ext, k, 0)[8:8 + L]
        buf[...] = cur[L - 8:L]
        return _silu(out)

    xs = conv_silu(xbuf, xs_ref, cwx_ref, cbx_ref)
    bcs = conv_silu(bbuf, bc_ref, cwb_ref, cbb_ref)

    misc = misc_ref[...]
    pre = misc + hp_ref[0:1, :]
    dt = jnp.maximum(pre, 0.0) + jnp.log1p(jnp.exp(-jnp.abs(pre)))
    a_dt = dt * (-jnp.exp(hp_ref[1:2, :]))
    r = lax.broadcasted_iota(jnp.int32, (L, L), 0)
    c = lax.broadcasted_iota(jnp.int32, (L, L), 1)
    causal = r >= c
    a_cs = _split3_dot_left(causal.astype(BF16), a_dt)
    a_cs_t = a_cs.T

    dt_full = _split3_dot(dt, spread_ref[...])
    acs_full = _split3_dot(a_cs, spread_ref[...])
    total_full = acs_full[L - 1:L, :]
    xdt = xs * dt_full
    xdt16 = xdt.astype(BF16)
    to_end16 = (xdt * jnp.exp(total_full - acs_full)).astype(BF16)
    decay_in = jnp.exp(acs_full)
    chunk_decay = jnp.exp(total_full)
    low = c < P
    zero16 = jnp.zeros((L, 2 * P), BF16)
    width = SSD_HPG * P

    ys = []
    for g in range(SSD_GROUPS):
        bm = bcs[:, g * SSD_STATE:(g + 1) * SSD_STATE]
        cm16 = bcs[:, GN + g * SSD_STATE:GN + (g + 1) * SSD_STATE].astype(BF16)
        cb = _dot_nt(cm16, bm.astype(BF16))
        lanes = slice(g * width, (g + 1) * width)
        h_in = state[g]
        y_off = _dot(cm16, h_in.astype(BF16)) * decay_in[:, lanes]
        state[g] = h_in * chunk_decay[:, lanes] + _dot(bm.T.astype(BF16), to_end16[:, lanes])
        diag = []
        for pp in range(SSD_HPG // 2):
            hd = g * SSD_HPG + 2 * pp
            decayed = []
            for col in (MISC_DT + hd, MISC_DT + hd + 1):
                seg = jnp.exp(jnp.where(causal, a_cs[:, col:col + 1] - a_cs_t[col:col + 1, :], -jnp.inf))
                decayed.append((cb * seg).astype(BF16))
            xp = xdt16[:, hd * P:(hd + 2) * P]
            x_bd = jnp.concatenate([jnp.where(low, xp, zero16), jnp.where(low, zero16, xp)], axis=0)
            diag.append(_dot(jnp.concatenate(decayed, axis=1), x_bd))
        ys.append(jnp.concatenate(diag, axis=1) + y_off)

    y = (jnp.concatenate(ys, axis=1) + dskip_ref[...] * xs) * _silu(z_ref[...])
    for g in range(SSD_GROUPS):
        lanes = slice(g * width, (g + 1) * width)
        yg = y[:, lanes]
        o_ref[:, lanes] = yg * lax.rsqrt(jnp.mean(yg * yg, axis=-1, keepdims=True) + EPS) * ng_ref[:, lanes]


def _ssd(pf, conv_w, conv_b, head_params, d_skip, norm_g, batch, seq, cast=(), cast_layer=()):
    nc = seq // QB
    c_in, c_out, c_shapes = _cast_specs(cast, cast_layer, batch * nc, lambda b, ci: b * nc + ci)
    GN = SSD_GROUPS * SSD_STATE
    lane_head = jnp.arange(SSD_INNER)[None, :] // HEAD_DIM
    spread = (jnp.arange(LANES)[:, None] == MISC_DT + lane_head).astype(BF16)
    cwx, cwb = conv_w[:, :SSD_INNER], conv_w[:, SSD_INNER:]
    cbx, cbb = conv_b[:SSD_INNER].reshape(1, -1), conv_b[SSD_INNER:].reshape(1, -1)
    full = lambda shape: pl.BlockSpec(shape, lambda b, ci: (0,) * len(shape))
    out = pl.pallas_call(
        functools.partial(_ssd_kernel, n_cast=len(cast)),
        grid=(batch, nc),
        in_specs=[
            pl.BlockSpec((QB, SSD_INNER), lambda b, ci: (b * nc + ci, PF_Z // SSD_INNER)),
            pl.BlockSpec((QB, SSD_INNER), lambda b, ci: (b * nc + ci, PF_XS // SSD_INNER)),
            pl.BlockSpec((QB, 2 * GN), lambda b, ci: (b * nc + ci, PF_BC // (2 * GN))),
            pl.BlockSpec((QB, LANES), lambda b, ci: (b * nc + ci, PF_MISC // LANES)),
            full((SSD_CONV, SSD_INNER)),
            full((SSD_CONV, 2 * GN)),
            full((1, SSD_INNER)),
            full((1, 2 * GN)),
            full((8, LANES)),
            full((LANES, SSD_INNER)),
            full((1, SSD_INNER)),
            full((1, SSD_INNER)),
        ] + c_in,
        out_specs=[pl.BlockSpec((QB, SSD_INNER), lambda b, ci: (b * nc + ci, 0))] + c_out,
        out_shape=[jax.ShapeDtypeStruct((batch * seq, SSD_INNER), F32)] + c_shapes,
        scratch_shapes=[
            pltpu.VMEM((8, SSD_INNER), F32),
            pltpu.VMEM((8, 2 * GN), F32),
            pltpu.VMEM((SSD_GROUPS, SSD_STATE, SSD_HPG * HEAD_DIM), F32),
        ],
        compiler_params=_params(("parallel", "arbitrary")),
        name="ssd",
    )(pf, pf, pf, pf, cwx, cwb, cbx, cbb, head_params, spread,
      jnp.repeat(d_skip.astype(F32), HEAD_DIM).reshape(1, -1), norm_g.reshape(1, -1), *cast)
    return out[0], out[1:]


def _out_proj_kernel(x_ref, nsa_ref, sb_ref, ssd_ref, gn_ref, gs_ref, w_ref, o_ref, mix_ref):
    @pl.when(pl.program_id(1) == 0)
    def _():
        mix_ref[:, 0:NSA_WIDTH] = _rms(nsa_ref[...], gn_ref[...]).astype(BF16)
        mix_ref[:, NSA_WIDTH:NSA_WIDTH + SB_WIDTH] = _rms(sb_ref[...], gs_ref[...]).astype(BF16)
        mix_ref[:, NSA_WIDTH + SB_WIDTH:] = ssd_ref[...].astype(BF16)

    o_ref[...] = x_ref[...] + _dot(mix_ref[...], w_ref[...])


def _out_proj(x, o_nsa, o_sb, o_ssd, g_nsa, g_sb, w, *, tm=1024, tn=1024):
    n, d = x.shape
    dm = w.shape[0]
    return pl.pallas_call(
        _out_proj_kernel,
        grid=(n // tm, d // tn),
        in_specs=[
            pl.BlockSpec((tm, tn), lambda i, j: (i, j)),
            pl.BlockSpec((tm, NSA_WIDTH), lambda i, j: (i, 0)),
            pl.BlockSpec((tm, SB_WIDTH), lambda i, j: (i, 0)),
            pl.BlockSpec((tm, SSD_INNER), lambda i, j: (i, 0)),
            pl.BlockSpec((1, NSA_WIDTH), lambda i, j: (0, 0)),
            pl.BlockSpec((1, SB_WIDTH), lambda i, j: (0, 0)),
            pl.BlockSpec((dm, tn), lambda i, j: (0, j)),
        ],
        out_specs=pl.BlockSpec((tm, tn), lambda i, j: (i, j)),
        out_shape=jax.ShapeDtypeStruct((n, d), F32),
        scratch_shapes=[pltpu.VMEM((tm, dm), BF16)],
        compiler_params=_params(("parallel", "arbitrary")),
        name="out_proj",
    )(x, o_nsa, o_sb, o_ssd, g_nsa.reshape(1, -1), g_sb.reshape(1, -1), w)


def _rel_bucket(dist):
    dist = jnp.maximum(dist, 0)
    max_exact = REL_BUCKETS // 2
    log_ratio = jnp.log(jnp.maximum(dist, 1).astype(F32) / max_exact) / math.log(REL_MAX_DIST / max_exact)
    large = jnp.minimum(max_exact + (log_ratio * (REL_BUCKETS - max_exact)).astype(jnp.int32), REL_BUCKETS - 1)
    return jnp.where(dist < max_exact, dist, large)


def _bias_tables(rel_bias, seq):
    assert QB >= REL_MAX_DIST
    def lookup(dist):
        buckets = jnp.arange(REL_BUCKETS).reshape((-1,) + (1,) * dist.ndim)
        onehot = (_rel_bucket(dist)[None] == buckets).astype(F32)
        return jnp.einsum('kh,k...->h...', rel_bias, onehot, precision=lax.Precision.HIGHEST)

    t = jnp.arange(seq)[:, None]
    cend = jnp.arange(QB)[None, :] * CMP_STRIDE + CMP_BLOCK - 1
    n_cmp = (seq - CMP_BLOCK) // CMP_STRIDE + 1
    valid_c = (t >= cend) & (jnp.arange(QB)[None, :] < n_cmp)
    bias_c = jnp.where(valid_c, lookup(t - cend), NEG_INF)
    r = jnp.arange(QB)[:, None]
    m = jnp.arange(QB)[None, :]
    near = jnp.stack([lookup(r - m), lookup(QB + r - m)])
    near = near - rel_bias[REL_BUCKETS - 1][None, :, None, None]
    diag = near[0] + jnp.where(m > r, NEG_INF, 0.0)
    prev = near[1]
    rows_a = jnp.concatenate([prev, diag, jnp.full_like(diag, NEG_INF)], axis=2)
    rows_b = jnp.concatenate([jnp.zeros_like(diag), prev, diag], axis=2)
    near2 = jnp.concatenate([rows_a, rows_b], axis=1)
    return bias_c, near2.reshape(NSA_KV_HEADS, NSA_GQA * SW_TQ, 3 * QB)


def _expand_table(seq):
    j = jnp.arange(QB)[:, None]
    s = jnp.arange(seq)[None, :]
    return (s // SEL_BLOCK == j).astype(BF16)


def _overlap_table(seq):
    n_cmp = (seq - CMP_BLOCK) // CMP_STRIDE + 1
    cs = jnp.arange(QB)[:, None] * CMP_STRIDE
    ce = cs + CMP_BLOCK - 1
    ss = jnp.arange(QB)[None, :] * SEL_BLOCK
    ov = jnp.maximum(jnp.minimum(ce, ss + SEL_BLOCK - 1) - jnp.maximum(cs, ss) + 1, 0).astype(F32) / CMP_BLOCK
    keep = (jnp.arange(QB)[:, None] < n_cmp) & (jnp.arange(QB)[None, :] < seq // SEL_BLOCK)
    return jnp.where(keep, ov, 0.0).T.astype(BF16)


def _in_proj_weights(w):
    w = w.astype(BF16)
    scale = HEAD_DIM ** -0.5
    o_q, o_kv, o_gate = 0, NSA_WIDTH, NSA_WIDTH + 768
    o_sb = o_gate + 3 * NSA_HEADS
    o_z = o_sb + 3 * SB_WIDTH
    o_xbc = o_z + SSD_INNER
    o_dt = o_xbc + SSD_INNER + 2 * SSD_GROUPS * SSD_STATE
    col = lambda a, n: w[:, a:a + n]
    kv = lambda br, which, h: col(o_kv + br * 256 + which * 128 + h * HEAD_DIM, HEAD_DIM)
    zeros = lambda n: jnp.zeros((w.shape[0], n), w.dtype)
    branch = lambda br: [kv(br, 0, 0), kv(br, 1, 0), kv(br, 0, 1), kv(br, 1, 1)]
    q_heads = []
    for h in range(NSA_HEADS):
        q_heads += [col(o_q + h * HEAD_DIM, HEAD_DIM) * scale, zeros(LANES -HEAD_DIM)]
    wb = jnp.concatenate(
        q_heads + branch(1) + branch(2)
        + [col(o_sb, SB_WIDTH) * scale, col(o_sb + SB_WIDTH, 2 * SB_WIDTH)], axis=1)
    gates = 3 * NSA_GQA
    wf = jnp.concatenate(
        [col(o_z, SSD_INNER), col(o_xbc, SSD_INNER + 2 * SSD_GROUPS * SSD_STATE), col(o_kv, 256),
         col(o_gate, gates), zeros(MISC_DT - gates), col(o_dt, SSD_HEADS), zeros(LANES -MISC_DT - SSD_HEADS),
         col(o_gate + gates, gates), zeros(LANES -gates)], axis=1)
    assert wb.shape[1] == PB_COLS and wf.shape[1] == PF_COLS
    return wb.astype(BF16), wf.astype(BF16)


def _head_params(dt_bias, a_log):
    rows = jnp.stack([dt_bias, a_log]).astype(F32)
    return jnp.zeros((8, LANES), F32).at[0:2, MISC_DT:MISC_DT + SSD_HEADS].set(rows)


def _mixer(x, l, tables, batch, seq, mix_norm, w_in, w_out, cmp_pos, cmp_w, nsa_norm, sb_norm,
           conv_w, conv_b, dt_bias, a_log, d_skip, ssd_norm, casts):
    hosts = [([w for w, _ in casts[k::3]], tuple(at for _, at in casts[k::3])) for k in range(3)]
    bias_c, near, overlap, expand = tables
    wb, wf = _in_proj_weights(w_in[l])
    pb = _rms_matmul(x, mix_norm[l], wb, BF16, name="in_proj_bf16")
    pf = _rms_matmul(x, mix_norm[l], wf, F32, name="in_proj_f32")

    ng = seq // CMP_STRIDE
    ucmp = pf[:, PF_CMP:PF_CMP + 256].reshape(batch, ng, CMP_STRIDE, 4, HEAD_DIM)
    ucmp = ucmp.transpose(0, 3, 1, 2, 4).reshape(batch, 4, ng, CMP_STRIDE * HEAD_DIM)
    pos2 = cmp_pos[l].reshape(2, 2, CMP_STRIDE * HEAD_DIM)
    o_cmp, sel = _nsa_cmp(pb, ucmp, pos2, cmp_w[l].astype(BF16), bias_c, overlap, batch, seq)
    o_nsa, w0 = _nsa_sel_win(pb, pf, o_cmp, sel, expand, near, batch, seq, *hosts[0])
    o_sb, w1 = _sb_attention(pb, batch, seq, *hosts[1])
    o_ssd, w2 = _ssd(pf, conv_w[l], conv_b[l], _head_params(dt_bias[l], a_log[l]), d_skip[l], ssd_norm[l],
                     batch, seq, *hosts[2])
    out = _out_proj(x, o_nsa, o_sb, o_ssd, nsa_norm[l], sb_norm[l], w_out[l].astype(BF16))
    done = [None] * len(casts)
    for k, ws in enumerate((w0, w1, w2)):
        done[k::3] = ws
    return out, done


def kernel(x, rel_bias, ffn1_norm, ffn1_w_gate, ffn1_w_up, ffn1_w_down, mix_norm, w_in, w_out, nsa_cmp_pos, nsa_cmp_w, nsa_out_norm, sb_out_norm, ssd_conv_w, ssd_conv_b, ssd_dt_bias, ssd_a_log, ssd_d, ssd_out_norm, ffn2_norm, ffn2_w_gate, ffn2_w_up, ffn2_w_down, final_norm):
    batch, seq, d = x.shape
    depth = w_in.shape[0]
    tables = _bias_tables(rel_bias, seq) + (_overlap_table(seq), _expand_table(seq))
    h = x.reshape(batch * seq, d)

    ffn1 = (ffn1_w_gate, ffn1_w_up, ffn1_w_down)
    ffn2 = (ffn2_w_gate, ffn2_w_up, ffn2_w_down)
    widen = lambda ws: tuple(w[None] for w in ws)
    w1 = widen(_cast_layer(ffn1, 0))
    for l in range(depth):
        h = _ffn(h, ffn1_norm[l], *w1, 0, final_norm, final=False)
        casts = [(w, l) for w in ffn2] + ([(w, l + 1) for w in ffn1] if l + 1 < depth else [])
        h, done = _mixer(h, l, tables, batch, seq, mix_norm, w_in, w_out, nsa_cmp_pos, nsa_cmp_w,
                         nsa_out_norm, sb_out_norm, ssd_conv_w, ssd_conv_b, ssd_dt_bias, ssd_a_log,
                         ssd_d, ssd_out_norm, casts)
        h = _ffn(h, ffn2_norm[l], *widen(done[0:3]), 0, final_norm, final=(l == depth - 1))
        w1 = widen(done[3:6])
    return h.reshape(batch, seq, d)
```

```python
import functools
import math

import jax
import jax.numpy as jnp
from jax import lax
from jax.experimental import pallas as pl
from jax.experimental.pallas import tpu as pltpu

LANES = 128
HEAD_DIM = 64
QB = 128
NEG_INF = -1e30
EPS = 1e-6

NSA_HEADS = 8
NSA_KV_HEADS = 2
NSA_GQA = NSA_HEADS // NSA_KV_HEADS
NSA_WIDTH = NSA_HEADS * HEAD_DIM
CMP_BLOCK = 32
CMP_STRIDE = 16
SEL_BLOCK = 64
SEL_TOPK = 8
FORCE_SCORE = 1e3
WINDOW = 512
SB_HEADS = 8
SB_WIDTH = SB_HEADS * HEAD_DIM
SSD_HEADS = 16
SSD_INNER = SSD_HEADS * HEAD_DIM
SSD_GROUPS = 2
SSD_HPG = SSD_HEADS // SSD_GROUPS
SSD_STATE = 128
SSD_CONV = 4
REL_BUCKETS = 32
REL_MAX_DIST = 128

PB_NSA_Q, PB_SEL, PB_WIN, PB_SB_Q, PB_SB_K, PB_SB_V, PB_COLS = 0, 1024, 1280, 1536, 2048, 2560, 3072
PF_Z, PF_XS, PF_BC, PF_CMP, PF_MISC, PF_COLS = 0, 1024, 2048, 2560, 2816, 3072
MISC_DT = 16

VMEM_LIMIT = 52 * 1024 * 1024

BF16 = jnp.bfloat16
F32 = jnp.float32


def _dot(a, b):
    return jnp.dot(a, b, preferred_element_type=F32)


def _dot_nt(a, b):
    return lax.dot_general(a, b, (((1,), (1,)), ((), ())), preferred_element_type=F32)


def _split3_dot(x, m):
    hi = x.astype(BF16)
    r1 = x - hi.astype(F32)
    mid = r1.astype(BF16)
    lo = (r1 - mid.astype(F32)).astype(BF16)
    return _dot(hi, m) + _dot(mid, m) + _dot(lo, m)


def _split3_dot_left(m, x):
    hi = x.astype(BF16)
    r1 = x - hi.astype(F32)
    mid = r1.astype(BF16)
    lo = (r1 - mid.astype(F32)).astype(BF16)
    return _dot(m, hi) + _dot(m, mid) + _dot(m, lo)


def _rms(x, g):
    return x * lax.rsqrt(jnp.mean(x * x, axis=-1, keepdims=True) + EPS) * g


def _silu(x):
    return x / (1.0 + jnp.exp(-x))


def _sigmoid(x):
    return 1.0 / (1.0 + jnp.exp(-x))


def _params(sem):
    return pltpu.CompilerParams(dimension_semantics=sem, vmem_limit_bytes=VMEM_LIMIT)


def _cast_specs(weights, layer, steps, step_of):
    in_specs, out_specs, out_shapes = [], [], []
    for w, at in zip(weights, layer):
        _, rows, cols = w.shape
        share = 1
        while (rows * share) % (16 * steps):
            share *= 2
        slab = rows * share // steps
        in_specs.append(pl.BlockSpec((None, slab, cols),
                                     lambda *ids, k=share, at=at: (at, step_of(*ids) // k, 0)))
        out_specs.append(pl.BlockSpec((slab, cols), lambda *ids, k=share: (step_of(*ids) // k, 0)))
        out_shapes.append(jax.ShapeDtypeStruct((rows, cols), BF16))
    return in_specs, out_specs, out_shapes


def _cast_slabs(srcs, dsts):
    for src, dst in zip(srcs, dsts):
        dst[...] = src[...].astype(BF16)


def _cast_kernel(*refs):
    _cast_slabs(refs[:len(refs) // 2], refs[len(refs) // 2:])


def _cast_layer(weights, layer, steps=16):
    c_in, c_out, c_shapes = _cast_specs(weights, (layer,) * len(weights), steps, lambda s: s)
    return pl.pallas_call(_cast_kernel, grid=(steps,), in_specs=c_in, out_specs=c_out, out_shape=c_shapes,
                          compiler_params=_params(("parallel",)), name="cast_weights")(*weights)


def _ffn_kernel(x_ref, g_ref, wg_ref, wu_ref, wd_ref, fg_ref, o_ref, h_ref, acc_ref, *, final):
    f = pl.program_id(1)

    @pl.when(f == 0)
    def _():
        h_ref[...] = _rms(x_ref[...], g_ref[...]).astype(BF16)
        acc_ref[...] = jnp.zeros_like(acc_ref)

    h = h_ref[...]
    gate = _dot(h, wg_ref[...])
    up = _dot(h, wu_ref[...])
    acc_ref[...] += _dot((_silu(gate) * up).astype(BF16), wd_ref[...])

    @pl.when(f == pl.num_programs(1) - 1)
    def _():
        y = x_ref[...] + 0.5 * acc_ref[...]
        if final:
            y = _rms(y, fg_ref[...])
        o_ref[...] = y


def _ffn(x, g, wg, wu, wd, layer, final_g, *, final, tm=512, tf=512):
    n, d = x.shape
    dff = wg.shape[2]
    return pl.pallas_call(
        functools.partial(_ffn_kernel, final=final),
        grid=(n // tm, dff // tf),
        in_specs=[
            pl.BlockSpec((tm, d), lambda i, f: (i, 0)),
            pl.BlockSpec((1, d), lambda i, f: (0, 0)),
            pl.BlockSpec((None, d, tf), lambda i, f: (layer, 0, f)),
            pl.BlockSpec((None, d, tf), lambda i, f: (layer, 0, f)),
            pl.BlockSpec((None, tf, d), lambda i, f: (layer, f, 0)),
            pl.BlockSpec((1, d), lambda i, f: (0, 0)),
        ],
        out_specs=pl.BlockSpec((tm, d), lambda i, f: (i, 0)),
        out_shape=jax.ShapeDtypeStruct((n, d), F32),
        scratch_shapes=[pltpu.VMEM((tm, d), BF16), pltpu.VMEM((tm, d), F32)],
        compiler_params=_params(("parallel", "arbitrary")),
        name="ffn",
    )(x, g.reshape(1, d), wg, wu, wd, final_g.reshape(1, d))


def _in_proj_kernel(x_ref, g_ref, w_ref, ob_ref, of_ref, h_ref, *, nb):
    j = pl.program_id(1)

    @pl.when(j == 0)
    def _():
        h_ref[...] = _rms(x_ref[...], g_ref[...]).astype(BF16)

    @pl.when(j < nb)
    def _():
        ob_ref[...] = _dot(h_ref[...], w_ref[...]).astype(BF16)

    @pl.when(j >= nb)
    def _():
        of_ref[...] = _dot(h_ref[...], w_ref[...])


def _in_proj(x, g, w, cols_bf16, *, tm=1024, tn=1024):
    n, d = x.shape
    c = w.shape[1]
    nb = cols_bf16 // tn
    return pl.pallas_call(
        functools.partial(_in_proj_kernel, nb=nb),
        grid=(n // tm, c // tn),
        in_specs=[
            pl.BlockSpec((tm, d), lambda i, j: (i, 0)),
            pl.BlockSpec((1, d), lambda i, j: (0, 0)),
            pl.BlockSpec((d, tn), lambda i, j: (0, j)),
        ],
        out_specs=[pl.BlockSpec((tm, tn), lambda i, j: (i, jnp.minimum(j, nb - 1))),
                   pl.BlockSpec((tm, tn), lambda i, j: (i, jnp.maximum(j - nb, 0)))],
        out_shape=[jax.ShapeDtypeStruct((n, cols_bf16), BF16),
                   jax.ShapeDtypeStruct((n, c - cols_bf16), F32)],
        scratch_shapes=[pltpu.VMEM((tm, d), BF16)],
        compiler_params=_params(("parallel", "arbitrary")),
        name="in_proj",
    )(x, g.reshape(1, d), w)


SB_GROUP = 8
SB_DEAD = 104.0


def _sb_kernel(*refs, n_cast):
    q_ref, k_ref, v_ref = refs[:3]
    o_ref = refs[3 + n_cast]
    kbd_ref, vbd_ref = refs[4 + 2 * n_cast:]
    _cast_slabs(refs[3:3 + n_cast], refs[4 + n_cast:4 + 2 * n_cast])
    _sb_body(q_ref, k_ref, v_ref, o_ref, kbd_ref, vbd_ref)


def _sb_body(q_ref, k_ref, v_ref, o_ref, kbd_ref, vbd_ref):
    i = pl.program_id(2)
    pairs = SB_GROUP // 2
    nkb = k_ref.shape[0] // QB
    r = lax.broadcasted_iota(jnp.int32, (QB, QB), 0)
    c = lax.broadcasted_iota(jnp.int32, (QB, QB), 1)
    strict_all = jnp.concatenate([c < r] * SB_GROUP, axis=0)
    later_ones = jnp.concatenate([(r > c).astype(BF16), jnp.ones((QB, QB), BF16)], axis=1)
    later_ones = jnp.concatenate([later_ones, later_ones], axis=0)

    @pl.when(i == 0)
    def _():
        low = c < HEAD_DIM
        zero16 = jnp.zeros((QB, QB), BF16)

        def build(j, carry):
            src = pl.ds(pl.multiple_of(j * QB, QB), QB)
            dst = pl.ds(pl.multiple_of(j * 2 * QB, 2 * QB), 2 * QB)
            for pp in range(pairs):
                for ref, out in ((k_ref, kbd_ref), (v_ref, vbd_ref)):
                    x = ref[src, pp * LANES:(pp + 1) * LANES]
                    out[pp, dst, :] = jnp.concatenate([jnp.where(low, x, zero16), jnp.where(low, zero16, x)],
                                                      axis=0)
            return carry

        lax.fori_loop(0, nkb, build, 0)

    def scores(j, diagonal):
        rows = pl.ds(pl.multiple_of(j * 2 * QB, 2 * QB), 2 * QB)
        zs = []
        for pp in range(pairs):
            z_pair = _dot_nt(q_ref[:, pp * LANES:(pp + 1) * LANES], kbd_ref[pp, rows, :])
            zs += [z_pair[:, 0:QB], z_pair[:, QB:2 * QB]]
        z = jnp.concatenate(zs, axis=0)
        soft = jnp.maximum(z, 0.0) + jnp.log(1.0 + jnp.exp(-jnp.abs(z)))
        if diagonal:
            soft = jnp.where(strict_all, soft, 0.0)
        hi = soft.astype(BF16)
        lo = (soft - hi.astype(F32)).astype(BF16)
        sums = _dot(jnp.concatenate([hi, lo], axis=1), later_ones)
        return rows, z - soft, sums

    def absorb(tails, accs, rows, log_beta, sums, diagonal):
        a = jnp.exp(log_beta - (tails + sums[:, 0:QB]))
        if diagonal:
            a = jnp.where(strict_all, a, 0.0)
        a = a.astype(BF16)
        outs = []
        for pp in range(pairs):
            pair = jnp.concatenate([a[2 * pp * QB:(2 * pp + 1) * QB], a[(2 * pp + 1) * QB:(2 * pp + 2) * QB]],
                                   axis=1)
            outs.append(_dot(pair, vbd_ref[pp, rows, :]))
        return tails + sums[:, QB:2 * QB], accs + jnp.concatenate(outs, axis=0)

    def sweep(js, carry):
        staged = [scores(j, False) for j in js]
        for st in staged:
            carry = absorb(*carry, *st, False)
        return carry

    carry = (jnp.zeros((SB_GROUP * QB, QB), F32), jnp.zeros((pairs * QB, LANES), F32))
    carry = absorb(*carry, *scores(i, True), True)
    odd = i % 2
    carry = lax.fori_loop(0, odd, lambda step, cr: sweep([i - 1], cr), carry)
    top = i - 1 - odd

    def live(state):
        step, smallest, _, _ = state
        return (step < i // 2) & (smallest < SB_DEAD)

    def pair_step(state):
        step, _, tails, accs = state
        tails, accs = sweep([top - 2 * step, top - 2 * step - 1], (tails, accs))
        return step + 1, jnp.min(tails), tails, accs

    accs = lax.while_loop(live, pair_step, (0, jnp.min(carry[0]), *carry))[3]
    for pp in range(pairs):
        o_ref[:, pp * LANES:(pp + 1) * LANES] = accs[pp * QB:(pp + 1) * QB]


def _sb_attention(pb, batch, seq, cast=(), cast_layer=()):
    nqb = seq // QB
    w = SB_GROUP * HEAD_DIM
    assert SB_HEADS == SB_GROUP
    qc, kc, vc = PB_SB_Q // w, PB_SB_K // w, PB_SB_V // w
    c_in, c_out, c_shapes = _cast_specs(cast, cast_layer, batch * nqb, lambda b, p, i: b * nqb + i)
    out = pl.pallas_call(
        functools.partial(_sb_kernel, n_cast=len(cast)),
        grid=(batch, SB_HEADS // SB_GROUP, nqb),
        in_specs=[
            pl.BlockSpec((QB, w), lambda b, p, i: (b * nqb + i, qc + p)),
            pl.BlockSpec((seq, w), lambda b, p, i: (b, kc + p)),
            pl.BlockSpec((seq, w), lambda b, p, i: (b, vc + p)),
        ] + c_in,
        out_specs=[pl.BlockSpec((QB, w), lambda b, p, i: (b * nqb + i, p))] + c_out,
        out_shape=[jax.ShapeDtypeStruct((batch * seq, SB_WIDTH), F32)] + c_shapes,
        scratch_shapes=[pltpu.VMEM((SB_GROUP // 2, 2 * seq, LANES), BF16),
                        pltpu.VMEM((SB_GROUP // 2, 2 * seq, LANES), BF16)],
        compiler_params=_params(("parallel", "parallel", "arbitrary")),
        name="sb_attention",
    )(pb, pb, pb, *cast)
    return out[0], out[1:]


CMP_TQ = 4 * QB


def _nsa_cmp_kernel(q_ref, uk_ref, uv_ref, pos_ref, w_ref, b0_ref, b1_ref, b2_ref, b3_ref, ovt_ref,
                    ocmp_ref, sel_ref, kc_ref, vc_ref, *, n_cmp, n_sel):
    i = pl.program_id(2)
    half = CMP_STRIDE * HEAD_DIM

    @pl.when(i == 0)
    def _():
        def compress(kv, u_ref):
            u = u_ref[0, 0]
            top = _dot((u + pos_ref[kv, 0:1, :]).astype(BF16), w_ref[kv, 0:half, :])
            bot = _dot((u + pos_ref[kv, 1:2, :]).astype(BF16), w_ref[kv, half:2 * half, :])
            return top + pltpu.roll(bot, QB - 1, 0)

        zeros = jnp.zeros((QB, HEAD_DIM), F32)
        kc_ref[...] = jnp.concatenate([compress(0, uk_ref), zeros], axis=1).astype(BF16)
        v = compress(1, uv_ref)
        vc_ref[...] = jnp.concatenate([jnp.concatenate([v, zeros], axis=1),
                                       jnp.concatenate([zeros, v], axis=1)], axis=0).astype(BF16)

    G = NSA_GQA
    TQ = CMP_TQ
    q4 = jnp.concatenate([q_ref[:, g * LANES:(g + 1) * LANES] for g in range(G)], axis=0)
    bias = jnp.concatenate([b[...] for b in (b0_ref, b1_ref, b2_ref, b3_ref)], axis=0)
    s = _dot_nt(q4, kc_ref[...]) + bias
    e = jnp.exp(s - jnp.max(s, axis=-1, keepdims=True))
    t_rows = i * TQ + lax.broadcasted_iota(jnp.int32, (TQ, 1), 0)
    any_valid = jnp.concatenate([t_rows >= CMP_BLOCK - 1] * G, axis=0)
    p = jnp.where(any_valid, e / jnp.sum(e, axis=-1, keepdims=True), 0.0)
    p16 = p.astype(BF16)
    for pp in range(G // 2):
        pair = jnp.concatenate([p16[2 * pp * TQ:(2 * pp + 1) * TQ], p16[(2 * pp + 1) * TQ:(2 * pp + 2) * TQ]],
                               axis=1)
        ocmp_ref[:, pp * LANES:(pp + 1) * LANES] = _dot(pair, vc_ref[...])
    p_all = p[0:TQ]
    for g in range(1, G):
        p_all = p_all + p[g * TQ:(g + 1) * TQ]

    blk = lax.broadcasted_iota(jnp.int32, (n_sel, QB), 0)
    for part in range(TQ // QB):
        rs = slice(part * QB, (part + 1) * QB)
        t0 = (i * (TQ // QB) + part) * QB
        p_sum = p_all[rs]
        hi = p_sum.astype(BF16)
        lo = (p_sum - hi.astype(F32)).astype(BF16)
        p_sel = (_dot_nt(ovt_ref[...], hi) + _dot_nt(ovt_ref[...], lo))[0:n_sel]
        t = t0 + lax.broadcasted_iota(jnp.int32, (n_sel, QB), 1)
        cur = t // SEL_BLOCK
        eligible = blk * SEL_BLOCK <= t
        forced = (blk == 0) | (blk == cur) | (blk == cur - 1)
        score = jnp.where(eligible, p_sel + jnp.where(forced, FORCE_SCORE, 0.0), NEG_INF)
        rank = jnp.zeros((n_sel, QB), F32)
        for j in range(n_sel):
            other = score[j:j + 1, :]
            ahead = (other > score) | ((other == score) & (blk > j))
            rank = rank + jnp.where(ahead, 1.0, 0.0)
        chosen = jnp.where(eligible & (rank < SEL_TOPK), 1.0, 0.0)
        chosen = jnp.concatenate([chosen, jnp.zeros((QB - n_sel, QB), F32)], axis=0)
        sel_ref[0, 0, rs, :] = chosen.T.astype(BF16)


def _nsa_cmp(pb, ucmp, pos2, cmp_w, bias_c, overlap_t, batch, seq):
    nqb = seq // CMP_TQ
    n_cmp = (seq - CMP_BLOCK) // CMP_STRIDE + 1
    n_sel = seq // SEL_BLOCK
    ng = seq // CMP_STRIDE
    assert ng == QB and n_sel <= QB and n_sel % 8 == 0
    wide = CMP_STRIDE * HEAD_DIM
    return pl.pallas_call(
        functools.partial(_nsa_cmp_kernel, n_cmp=n_cmp, n_sel=n_sel),
        grid=(batch, NSA_KV_HEADS, nqb),
        in_specs=[
            pl.BlockSpec((CMP_TQ, NSA_GQA * LANES), lambda b, h, i: (b * nqb + i, h)),
            pl.BlockSpec((1, 1, ng, wide), lambda b, h, i: (b, h, 0, 0)),
            pl.BlockSpec((1, 1, ng, wide), lambda b, h, i: (b, NSA_KV_HEADS + h, 0, 0)),
            pl.BlockSpec((2, 2, wide), lambda b, h, i: (0, 0, 0)),
            pl.BlockSpec((2, 2 * wide, HEAD_DIM), lambda b, h, i: (0, 0, 0)),
            *[pl.BlockSpec((None, CMP_TQ, QB), lambda b, h, i, g=g: (h * NSA_GQA + g, i, 0))
              for g in range(NSA_GQA)],
            pl.BlockSpec((QB, QB), lambda b, h, i: (0, 0)),
        ],
        out_specs=[
            pl.BlockSpec((CMP_TQ, 256), lambda b, h, i: (b * nqb + i, h)),
            pl.BlockSpec((1, 1, CMP_TQ, QB), lambda b, h, i: (b, h, i, 0)),
        ],
        out_shape=[
            jax.ShapeDtypeStruct((batch * seq, NSA_WIDTH), F32),
            jax.ShapeDtypeStruct((batch, NSA_KV_HEADS, seq, QB), BF16),
        ],
        scratch_shapes=[pltpu.VMEM((QB, LANES), BF16), pltpu.VMEM((2 * QB, LANES), BF16)],
        compiler_params=_params(("parallel", "parallel", "arbitrary")),
        name="nsa_compressed",
    )(pb, ucmp, ucmp, pos2, cmp_w, *[bias_c] * NSA_GQA, overlap_t)


SW_TQ = 2 * QB
SEL_CHUNK = 4 * QB
WIN_FAR = WINDOW - QB


def _nsa_sw_kernel(*refs, n_cast):
    n_in = 8
    _cast_slabs(refs[n_in:n_in + n_cast], refs[n_in + n_cast + 1:n_in + 2 * n_cast + 1])
    _nsa_sw_body(*refs[:n_in], refs[n_in + n_cast], *refs[n_in + 2 * n_cast + 1:])


def _nsa_sw_body(q_ref, ks_ref, kw_ref, ocmp_ref, misc_ref, sel_ref, exp_ref, tb_ref, o_ref, madd_ref):
    i = pl.program_id(2)
    G = NSA_GQA
    TQ = SW_TQ
    q4 = jnp.concatenate([q_ref[:, g * LANES:(g + 1) * LANES] for g in range(G)], axis=0)
    madd_ref[...] = (_dot(sel_ref[0, 0], exp_ref[...]) - 1.0) * (-NEG_INF)
    t_pos = i * TQ + lax.broadcasted_iota(jnp.int32, (TQ, 1), 0)

    low = lax.broadcasted_iota(jnp.int32, (QB, LANES), 1) < HEAD_DIM

    def tile(x, n=G):
        return jnp.concatenate([x] * n, axis=0)

    def ones_v(kv):
        return jnp.where(tile(low, kv.shape[0] // QB), jnp.ones_like(kv), kv)

    first = 2 * i - 1
    rows_m = pl.ds(pl.multiple_of(jnp.maximum(first, 0) * QB, QB), QB)
    rows_a = pl.ds(pl.multiple_of(2 * i * QB, QB), QB)
    rows_b = pl.ds(pl.multiple_of((2 * i + 1) * QB, QB), QB)
    gone = jnp.where(i >= 1, 0.0, NEG_INF)
    gone_m = jnp.concatenate([jnp.full((TQ, QB), gone, F32), jnp.zeros((TQ, 2 * QB), F32)], axis=1)
    far_end = first * QB

    kv = jnp.concatenate([ks_ref[rows_m, :], ks_ref[rows_a, :], ks_ref[rows_b, :]], axis=0)
    near_sel = jnp.concatenate([madd_ref[:, rows_m], madd_ref[:, rows_a], madd_ref[:, rows_b]], axis=1)
    s = _dot_nt(q4, kv) + (tb_ref[0] + tile(near_sel + gone_m))
    m = jnp.max(s, axis=-1, keepdims=True)
    acc = _dot(jnp.exp(s - m).astype(BF16), ones_v(kv))

    def sel_far(cidx, state):
        m, acc = state
        cols = pl.ds(pl.multiple_of(cidx * SEL_CHUNK, SEL_CHUNK), SEL_CHUNK)
        k_pos = cidx * SEL_CHUNK + lax.broadcasted_iota(jnp.int32, (TQ, SEL_CHUNK), 1)
        add = jnp.where(k_pos < far_end, madd_ref[:, cols], NEG_INF)
        kv = ks_ref[cols, :]
        s = _dot_nt(q4, kv) + tile(add)
        m_new = jnp.maximum(m, jnp.max(s, axis=-1, keepdims=True))
        return m_new, jnp.exp(m - m_new) * acc + _dot(jnp.exp(s - m_new).astype(BF16), ones_v(kv))

    _, sel_acc = lax.fori_loop(0, (2 * i + 2) // (SEL_CHUNK // QB), sel_far, (m, acc))

    start = jnp.maximum(first - WIN_FAR // QB, 0) * QB
    rows_far = pl.ds(pl.multiple_of(start, QB), WIN_FAR)
    k_pos = start + lax.broadcasted_iota(jnp.int32, (TQ, WIN_FAR), 1)
    live = (k_pos < far_end) & (k_pos > t_pos - WINDOW)
    kv = jnp.concatenate([kw_ref[rows_far, :], kw_ref[rows_m, :], kw_ref[rows_a, :], kw_ref[rows_b, :]], axis=0)
    bias = jnp.concatenate([tile(jnp.where(live, 0.0, NEG_INF)), tb_ref[0] + tile(gone_m)], axis=1)
    s = _dot_nt(q4, kv) + bias
    win_acc = _dot(jnp.exp(s - jnp.max(s, axis=-1, keepdims=True)).astype(BF16), ones_v(kv))

    gate = _sigmoid(misc_ref[...])
    low = tile(low, TQ // QB)

    def pair(acc, pp):
        a, b = acc[2 * pp * TQ:(2 * pp + 1) * TQ], acc[(2 * pp + 1) * TQ:(2 * pp + 2) * TQ]
        return (jnp.where(low, pltpu.roll(a, HEAD_DIM, 1), b)
                / jnp.where(low, a, pltpu.roll(b, HEAD_DIM, 1)))

    def pair_gate(branch, pp):
        ca, cb = 3 * (2 * pp) + branch, 3 * (2 * pp + 1) + branch
        return jnp.where(low, gate[:, ca:ca + 1], gate[:, cb:cb + 1])

    for pp in range(G // 2):
        lanes = slice(pp * LANES, (pp + 1) * LANES)
        o_ref[:, lanes] = (pair_gate(0, pp) * ocmp_ref[:, lanes] + pair_gate(1, pp) * pair(sel_acc, pp)
                           + pair_gate(2, pp) * pair(win_acc, pp))


def _nsa_sel_win(pb, pf, o_cmp, sel, expand, tb, batch, seq, cast=(), cast_layer=()):
    nqb = seq // SW_TQ
    G = NSA_GQA
    c_in, c_out, c_shapes = _cast_specs(cast, cast_layer, batch * NSA_KV_HEADS * nqb,
                                        lambda b, h, i: (b * NSA_KV_HEADS + h) * nqb + i)
    out = pl.pallas_call(
        functools.partial(_nsa_sw_kernel, n_cast=len(cast)),
        grid=(batch, NSA_KV_HEADS, nqb),
        in_specs=[
            pl.BlockSpec((SW_TQ, G * LANES), lambda b, h, i: (b * nqb + i, h)),
            pl.BlockSpec((seq, LANES), lambda b, h, i: (b, PB_SEL // LANES + h)),
            pl.BlockSpec((seq, LANES), lambda b, h, i: (b, PB_WIN // LANES + h)),
            pl.BlockSpec((SW_TQ, 256), lambda b, h, i: (b * nqb + i, h)),
            pl.BlockSpec((SW_TQ, LANES), lambda b, h, i: (b * nqb + i, PF_MISC // LANES + h)),
            pl.BlockSpec((1, 1, SW_TQ, QB), lambda b, h, i: (b, h, i, 0)),
            pl.BlockSpec((QB, seq), lambda b, h, i: (0, 0)),
            pl.BlockSpec((1, G * SW_TQ, 3 * QB), lambda b, h, i: (h, 0, 0)),
        ] + c_in,
        out_specs=[pl.BlockSpec((SW_TQ, 256), lambda b, h, i: (b * nqb + i, h))] + c_out,
        out_shape=[jax.ShapeDtypeStruct((batch * seq, NSA_WIDTH), F32)] + c_shapes,
        scratch_shapes=[pltpu.VMEM((SW_TQ, seq), F32)],
        compiler_params=_params(("parallel", "parallel", "arbitrary")),
        name="nsa_selected_window",
    )(pb, pb, pb, o_cmp, pf, sel, expand, tb, *cast)
    return out[0], out[1:]


def _ssd_kernel(*refs, n_cast):
    n_in = 12
    _cast_slabs(refs[n_in:n_in + n_cast], refs[n_in + n_cast + 1:n_in + 2 * n_cast + 1])
    _ssd_body(*refs[:n_in], refs[n_in + n_cast], *refs[n_in + 2 * n_cast + 1:])


def _ssd_body(z_ref, xs_ref, bc_ref, misc_ref, cwx_ref, cwb_ref, cbx_ref, cbb_ref, hp_ref, spread_ref,
              dskip_ref, ng_ref, o_ref, xbuf, bbuf, state):
    ci = pl.program_id(1)
    L = QB
    P = HEAD_DIM
    GN = SSD_GROUPS * SSD_STATE

    @pl.when(ci == 0)
    def _():
        xbuf[...] = jnp.zeros_like(xbuf)
        bbuf[...] = jnp.zeros_like(bbuf)
        state[...] = jnp.zeros_like(state)

    def conv_silu(buf, src_ref, w_ref, b_ref):
        cur = src_ref[...]
        ext = jnp.concatenate([buf[...], cur], axis=0)
        out = b_ref[...] + w_ref[SSD_CONV - 1:SSD_CONV, :] * cur
        for k in range(1, SSD_CONV):
            out = out + w_ref[SSD_CONV - 1 - k:SSD_CONV - k, :] * pltpu.roll(ext, k, 0)[8:8 + L]
        buf[...] = cur[L - 8:L]
        return _silu(out)

    xs = conv_silu(xbuf, xs_ref, cwx_ref, cbx_ref)
    bcs = conv_silu(bbuf, bc_ref, cwb_ref, cbb_ref)

    misc = misc_ref[...]
    pre = misc + hp_ref[0:1, :]
    dt = jnp.maximum(pre, 0.0) + jnp.log1p(jnp.exp(-jnp.abs(pre)))
    a_dt = dt * (-jnp.exp(hp_ref[1:2, :]))
    r = lax.broadcasted_iota(jnp.int32, (L, L), 0)
    c = lax.broadcasted_iota(jnp.int32, (L, L), 1)
    causal = r >= c
    a_cs = _split3_dot_left(causal.astype(BF16), a_dt)
    a_cs_t = a_cs.T

    dt_full = _split3_dot(dt, spread_ref[...])
    acs_full = _split3_dot(a_cs, spread_ref[...])
    total_full = acs_full[L - 1:L, :]
    xdt = xs * dt_full
    xdt16 = xdt.astype(BF16)
    to_end16 = (xdt * jnp.exp(total_full - acs_full)).astype(BF16)
    decay_in = jnp.exp(acs_full)
    chunk_decay = jnp.exp(total_full)
    low = c < P
    zero16 = jnp.zeros((L, 2 * P), BF16)
    width = SSD_HPG * P

    ys = []
    for g in range(SSD_GROUPS):
        bm = bcs[:, g * SSD_STATE:(g + 1) * SSD_STATE]
        cm16 = bcs[:, GN + g * SSD_STATE:GN + (g + 1) * SSD_STATE].astype(BF16)
        cb = _dot_nt(cm16, bm.astype(BF16))
        lanes = slice(g * width, (g + 1) * width)
        h_in = state[g]
        y_off = _dot(cm16, h_in.astype(BF16)) * decay_in[:, lanes]
        state[g] = h_in * chunk_decay[:, lanes] + _dot(bm.T.astype(BF16), to_end16[:, lanes])
        diag = []
        for pp in range(SSD_HPG // 2):
            hd = g * SSD_HPG + 2 * pp
            decayed = []
            for col in (MISC_DT + hd, MISC_DT + hd + 1):
                seg = jnp.exp(jnp.where(causal, a_cs[:, col:col + 1] - a_cs_t[col:col + 1, :], -jnp.inf))
                decayed.append((cb * seg).astype(BF16))
            xp = xdt16[:, hd * P:(hd + 2) * P]
            x_bd = jnp.concatenate([jnp.where(low, xp, zero16), jnp.where(low, zero16, xp)], axis=0)
            diag.append(_dot(jnp.concatenate(decayed, axis=1), x_bd))
        ys.append(jnp.concatenate(diag, axis=1) + y_off)

    y = (jnp.concatenate(ys, axis=1) + dskip_ref[...] * xs) * _silu(z_ref[...])
    for g in range(SSD_GROUPS):
        lanes = slice(g * width, (g + 1) * width)
        yg = y[:, lanes]
        o_ref[:, lanes] = yg * lax.rsqrt(jnp.mean(yg * yg, axis=-1, keepdims=True) + EPS) * ng_ref[:, lanes]


def _ssd(pf, conv_w, conv_b, head_params, d_skip, norm_g, batch, seq, cast=(), cast_layer=()):
    nc = seq // QB
    c_in, c_out, c_shapes = _cast_specs(cast, cast_layer, batch * nc, lambda b, ci: b * nc + ci)
    GN = SSD_GROUPS * SSD_STATE
    lane_head = jnp.arange(SSD_INNER)[None, :] // HEAD_DIM
    spread = (jnp.arange(LANES)[:, None] == MISC_DT + lane_head).astype(BF16)
    cwx, cwb = conv_w[:, :SSD_INNER], conv_w[:, SSD_INNER:]
    cbx, cbb = conv_b[:SSD_INNER].reshape(1, -1), conv_b[SSD_INNER:].reshape(1, -1)
    full = lambda shape: pl.BlockSpec(shape, lambda b, ci: (0,) * len(shape))
    out = pl.pallas_call(
        functools.partial(_ssd_kernel, n_cast=len(cast)),
        grid=(batch, nc),
        in_specs=[
            pl.BlockSpec((QB, SSD_INNER), lambda b, ci: (b * nc + ci, PF_Z // SSD_INNER)),
            pl.BlockSpec((QB, SSD_INNER), lambda b, ci: (b * nc + ci, PF_XS // SSD_INNER)),
            pl.BlockSpec((QB, 2 * GN), lambda b, ci: (b * nc + ci, PF_BC // (2 * GN))),
            pl.BlockSpec((QB, LANES), lambda b, ci: (b * nc + ci, PF_MISC // LANES)),
            full((SSD_CONV, SSD_INNER)),
            full((SSD_CONV, 2 * GN)),
            full((1, SSD_INNER)),
            full((1, 2 * GN)),
            full((8, LANES)),
            full((LANES, SSD_INNER)),
            full((1, SSD_INNER)),
            full((1, SSD_INNER)),
        ] + c_in,
        out_specs=[pl.BlockSpec((QB, SSD_INNER), lambda b, ci: (b * nc + ci, 0))] + c_out,
        out_shape=[jax.ShapeDtypeStruct((batch * seq, SSD_INNER), F32)] + c_shapes,
        scratch_shapes=[
            pltpu.VMEM((8, SSD_INNER), F32),
            pltpu.VMEM((8, 2 * GN), F32),
            pltpu.VMEM((SSD_GROUPS, SSD_STATE, SSD_HPG * HEAD_DIM), F32),
        ],
        compiler_params=_params(("parallel", "arbitrary")),
        name="ssd",
    )(pf, pf, pf, pf, cwx, cwb, cbx, cbb, head_params, spread,
      jnp.repeat(d_skip.astype(F32), HEAD_DIM).reshape(1, -1), norm_g.reshape(1, -1), *cast)
    return out[0], out[1:]


def _out_proj_kernel(x_ref, nsa_ref, sb_ref, ssd_ref, gn_ref, gs_ref, w_ref, o_ref, mix_ref):
    @pl.when(pl.program_id(1) == 0)
    def _():
        mix_ref[:, 0:NSA_WIDTH] = _rms(nsa_ref[...], gn_ref[...]).astype(BF16)
        mix_ref[:, NSA_WIDTH:NSA_WIDTH + SB_WIDTH] = _rms(sb_ref[...], gs_ref[...]).astype(BF16)
        mix_ref[:, NSA_WIDTH + SB_WIDTH:] = ssd_ref[...].astype(BF16)

    o_ref[...] = x_ref[...] + _dot(mix_ref[...], w_ref[...])


def _out_proj(x, o_nsa, o_sb, o_ssd, g_nsa, g_sb, w, *, tm=1024, tn=1024):
    n, d = x.shape
    dm = w.shape[0]
    return pl.pallas_call(
        _out_proj_kernel,
        grid=(n // tm, d // tn),
        in_specs=[
            pl.BlockSpec((tm, tn), lambda i, j: (i, j)),
            pl.BlockSpec((tm, NSA_WIDTH), lambda i, j: (i, 0)),
            pl.BlockSpec((tm, SB_WIDTH), lambda i, j: (i, 0)),
            pl.BlockSpec((tm, SSD_INNER), lambda i, j: (i, 0)),
            pl.BlockSpec((1, NSA_WIDTH), lambda i, j: (0, 0)),
            pl.BlockSpec((1, SB_WIDTH), lambda i, j: (0, 0)),
            pl.BlockSpec((dm, tn), lambda i, j: (0, j)),
        ],
        out_specs=pl.BlockSpec((tm, tn), lambda i, j: (i, j)),
        out_shape=jax.ShapeDtypeStruct((n, d), F32),
        scratch_shapes=[pltpu.VMEM((tm, dm), BF16)],
        compiler_params=_params(("parallel", "arbitrary")),
        name="out_proj",
    )(x, o_nsa, o_sb, o_ssd, g_nsa.reshape(1, -1), g_sb.reshape(1, -1), w)


def _rel_bucket(dist):
    dist = jnp.maximum(dist, 0)
    max_exact = REL_BUCKETS // 2
    log_ratio = jnp.log(jnp.maximum(dist, 1).astype(F32) / max_exact) / math.log(REL_MAX_DIST / max_exact)
    large = jnp.minimum(max_exact + (log_ratio * (REL_BUCKETS - max_exact)).astype(jnp.int32), REL_BUCKETS - 1)
    return jnp.where(dist < max_exact, dist, large)


def _bias_tables(rel_bias, seq):
    assert QB >= REL_MAX_DIST
    def lookup(dist):
        buckets = jnp.arange(REL_BUCKETS).reshape((-1,) + (1,) * dist.ndim)
        onehot = (_rel_bucket(dist)[None] == buckets).astype(F32)
        return jnp.einsum('kh,k...->h...', rel_bias, onehot, precision=lax.Precision.HIGHEST)

    t = jnp.arange(seq)[:, None]
    cend = jnp.arange(QB)[None, :] * CMP_STRIDE + CMP_BLOCK - 1
    n_cmp = (seq - CMP_BLOCK) // CMP_STRIDE + 1
    valid_c = (t >= cend) & (jnp.arange(QB)[None, :] < n_cmp)
    bias_c = jnp.where(valid_c, lookup(t - cend), NEG_INF)
    r = jnp.arange(QB)[:, None]
    m = jnp.arange(QB)[None, :]
    near = jnp.stack([lookup(r - m), lookup(QB + r - m)])
    near = near - rel_bias[REL_BUCKETS - 1][None, :, None, None]
    diag = near[0] + jnp.where(m > r, NEG_INF, 0.0)
    prev = near[1]
    rows_a = jnp.concatenate([prev, diag, jnp.full_like(diag, NEG_INF)], axis=2)
    rows_b = jnp.concatenate([jnp.zeros_like(diag), prev, diag], axis=2)
    near2 = jnp.concatenate([rows_a, rows_b], axis=1)
    return bias_c, near2.reshape(NSA_KV_HEADS, NSA_GQA * SW_TQ, 3 * QB)


def _expand_table(seq):
    j = jnp.arange(QB)[:, None]
    s = jnp.arange(seq)[None, :]
    return (s // SEL_BLOCK == j).astype(BF16)


def _overlap_table(seq):
    n_cmp = (seq - CMP_BLOCK) // CMP_STRIDE + 1
    cs = jnp.arange(QB)[:, None] * CMP_STRIDE
    ce = cs + CMP_BLOCK - 1
    ss = jnp.arange(QB)[None, :] * SEL_BLOCK
    ov = jnp.maximum(jnp.minimum(ce, ss + SEL_BLOCK - 1) - jnp.maximum(cs, ss) + 1, 0).astype(F32) / CMP_BLOCK
    keep = (jnp.arange(QB)[:, None] < n_cmp) & (jnp.arange(QB)[None, :] < seq // SEL_BLOCK)
    return jnp.where(keep, ov, 0.0).T.astype(BF16)


def _in_proj_weights(w):
    w = w.astype(BF16)
    scale = HEAD_DIM ** -0.5
    o_q, o_kv, o_gate = 0, NSA_WIDTH, NSA_WIDTH + 768
    o_sb = o_gate + 3 * NSA_HEADS
    o_z = o_sb + 3 * SB_WIDTH
    o_xbc = o_z + SSD_INNER
    o_dt = o_xbc + SSD_INNER + 2 * SSD_GROUPS * SSD_STATE
    col = lambda a, n: w[:, a:a + n]
    kv = lambda br, which, h: col(o_kv + br * 256 + which * 128 + h * HEAD_DIM, HEAD_DIM)
    zeros = lambda n: jnp.zeros((w.shape[0], n), w.dtype)
    branch = lambda br: [kv(br, 0, 0), kv(br, 1, 0), kv(br, 0, 1), kv(br, 1, 1)]
    q_heads = []
    for h in range(NSA_HEADS):
        q_heads += [col(o_q + h * HEAD_DIM, HEAD_DIM) * scale, zeros(LANES -HEAD_DIM)]
    wb = jnp.concatenate(
        q_heads + branch(1) + branch(2)
        + [col(o_sb, SB_WIDTH) * scale, col(o_sb + SB_WIDTH, 2 * SB_WIDTH)], axis=1)
    gates = 3 * NSA_GQA
    wf = jnp.concatenate(
        [col(o_z, SSD_INNER), col(o_xbc, SSD_INNER + 2 * SSD_GROUPS * SSD_STATE), col(o_kv, 256),
         col(o_gate, gates), zeros(MISC_DT - gates), col(o_dt, SSD_HEADS), zeros(LANES -MISC_DT - SSD_HEADS),
         col(o_gate + gates, gates), zeros(LANES -gates)], axis=1)
    assert wb.shape[1] == PB_COLS and wf.shape[1] == PF_COLS
    return wb.astype(BF16), wf.astype(BF16)


def _head_params(dt_bias, a_log):
    rows = jnp.stack([dt_bias, a_log]).astype(F32)
    return jnp.zeros((8, LANES), F32).at[0:2, MISC_DT:MISC_DT + SSD_HEADS].set(rows)


def _mixer(x, l, tables, batch, seq, mix_norm, w_in, w_out, cmp_pos, cmp_w, nsa_norm, sb_norm,
           conv_w, conv_b, dt_bias, a_log, d_skip, ssd_norm, casts):
    hosts = [([w for w, _ in casts[k::3]], tuple(at for _, at in casts[k::3])) for k in range(3)]
    bias_c, near, overlap, expand = tables
    wb, wf = _in_proj_weights(w_in[l])
    pb, pf = _in_proj(x, mix_norm[l], jnp.concatenate([wb, wf], axis=1), PB_COLS)

    ng = seq // CMP_STRIDE
    ucmp = pf[:, PF_CMP:PF_CMP + 256].reshape(batch, ng, CMP_STRIDE, 4, HEAD_DIM)
    ucmp = ucmp.transpose(0, 3, 1, 2, 4).reshape(batch, 4, ng, CMP_STRIDE * HEAD_DIM)
    pos2 = cmp_pos[l].reshape(2, 2, CMP_STRIDE * HEAD_DIM)
    o_cmp, sel = _nsa_cmp(pb, ucmp, pos2, cmp_w[l].astype(BF16), bias_c, overlap, batch, seq)
    o_nsa, w0 = _nsa_sel_win(pb, pf, o_cmp, sel, expand, near, batch, seq, *hosts[0])
    o_sb, w1 = _sb_attention(pb, batch, seq, *hosts[1])
    o_ssd, w2 = _ssd(pf, conv_w[l], conv_b[l], _head_params(dt_bias[l], a_log[l]), d_skip[l], ssd_norm[l],
                     batch, seq, *hosts[2])
    out = _out_proj(x, o_nsa, o_sb, o_ssd, nsa_norm[l], sb_norm[l], w_out[l].astype(BF16))
    done = [None] * len(casts)
    for k, ws in enumerate((w0, w1, w2)):
        done[k::3] = ws
    return out, done


def kernel(x, rel_bias, ffn1_norm, ffn1_w_gate, ffn1_w_up, ffn1_w_down, mix_norm, w_in, w_out, nsa_cmp_pos, nsa_cmp_w, nsa_out_norm, sb_out_norm, ssd_conv_w, ssd_conv_b, ssd_dt_bias, ssd_a_log, ssd_d, ssd_out_norm, ffn2_norm, ffn2_w_gate, ffn2_w_up, ffn2_w_down, final_norm):
    batch, seq, d = x.shape
    depth = w_in.shape[0]
    tables = _bias_tables(rel_bias, seq) + (_overlap_table(seq), _expand_table(seq))
    h = x.reshape(batch * seq, d)

    ffn1 = (ffn1_w_gate, ffn1_w_up, ffn1_w_down)
    ffn2 = (ffn2_w_gate, ffn2_w_up, ffn2_w_down)
    widen = lambda ws: tuple(w[None] for w in ws)
    w1 = widen(_cast_layer(ffn1, 0))
    for l in range(depth):
        h = _ffn(h, ffn1_norm[l], *w1, 0, final_norm, final=False)
        casts = [(w, l) for w in ffn2] + ([(w, l + 1) for w in ffn1] if l + 1 < depth else [])
        h, done = _mixer(h, l, tables, batch, seq, mix_norm, w_in, w_out, nsa_cmp_pos, nsa_cmp_w,
                         nsa_out_norm, sb_out_norm, ssd_conv_w, ssd_conv_b, ssd_dt_bias, ssd_a_log,
                         ssd_d, ssd_out_norm, casts)
        h = _ffn(h, ffn2_norm[l], *widen(done[0:3]), 0, final_norm, final=(l == depth - 1))
        w1 = widen(done[3:6])
    return h.reshape(batch, seq, d)
```

```python
import functools
import math

import jax
import jax.numpy as jnp
from jax import lax
from jax.experimental import pallas as pl
from jax.experimental.pallas import tpu as pltpu

LANES = 128
HEAD_DIM = 64
QB = 128
NEG_INF = -1e30
EPS = 1e-6

NSA_HEADS = 8
NSA_KV_HEADS = 2
NSA_GQA = NSA_HEADS // NSA_KV_HEADS
NSA_WIDTH = NSA_HEADS * HEAD_DIM
CMP_BLOCK = 32
CMP_STRIDE = 16
SEL_BLOCK = 64
SEL_TOPK = 8
FORCE_SCORE = 1e3
WINDOW = 512
SB_HEADS = 8
SB_WIDTH = SB_HEADS * HEAD_DIM
SSD_HEADS = 16
SSD_INNER = SSD_HEADS * HEAD_DIM
SSD_GROUPS = 2
SSD_HPG = SSD_HEADS // SSD_GROUPS
SSD_STATE = 128
SSD_CONV = 4
REL_BUCKETS = 32
REL_MAX_DIST = 128

PB_NSA_Q, PB_SEL, PB_WIN, PB_SB_Q, PB_SB_K, PB_SB_V, PB_COLS = 0, 1024, 1280, 1536, 2048, 2560, 3072
PF_Z, PF_XS, PF_BC, PF_CMP, PF_MISC, PF_COLS = 0, 1024, 2048, 2560, 2816, 3072
MISC_DT = 16

VMEM_LIMIT = 52 * 1024 * 1024

BF16 = jnp.bfloat16
F32 = jnp.float32


def _dot(a, b):
    return jnp.dot(a, b, preferred_element_type=F32)


def _dot_nt(a, b):
    return lax.dot_general(a, b, (((1,), (1,)), ((), ())), preferred_element_type=F32)


def _split3_dot(x, m):
    hi = x.astype(BF16)
    r1 = x - hi.astype(F32)
    mid = r1.astype(BF16)
    lo = (r1 - mid.astype(F32)).astype(BF16)
    return _dot(hi, m) + _dot(mid, m) + _dot(lo, m)


def _split3_dot_left(m, x):
    hi = x.astype(BF16)
    r1 = x - hi.astype(F32)
    mid = r1.astype(BF16)
    lo = (r1 - mid.astype(F32)).astype(BF16)
    return _dot(m, hi) + _dot(m, mid) + _dot(m, lo)


def _rms(x, g):
    return x * lax.rsqrt(jnp.mean(x * x, axis=-1, keepdims=True) + EPS) * g


def _silu(x):
    return x / (1.0 + jnp.exp(-x))


def _sigmoid(x):
    return 1.0 / (1.0 + jnp.exp(-x))


def _params(sem):
    return pltpu.CompilerParams(dimension_semantics=sem, vmem_limit_bytes=VMEM_LIMIT)


def _cast_specs(weights, layer, steps, step_of):
    in_specs, out_specs, out_shapes = [], [], []
    for w, at in zip(weights, layer):
        _, rows, cols = w.shape
        share = 1
        while (rows * share) % (16 * steps):
            share *= 2
        slab = rows * share // steps
        in_specs.append(pl.BlockSpec((None, slab, cols),
                                     lambda *ids, k=share, at=at: (at, step_of(*ids) // k, 0)))
        out_specs.append(pl.BlockSpec((slab, cols), lambda *ids, k=share: (step_of(*ids) // k, 0)))
        out_shapes.append(jax.ShapeDtypeStruct((rows, cols), BF16))
    return in_specs, out_specs, out_shapes


def _cast_slabs(srcs, dsts):
    for src, dst in zip(srcs, dsts):
        dst[...] = src[...].astype(BF16)


def _cast_kernel(*refs):
    _cast_slabs(refs[:len(refs) // 2], refs[len(refs) // 2:])


def _cast_layer(weights, layer, steps=16):
    c_in, c_out, c_shapes = _cast_specs(weights, (layer,) * len(weights), steps, lambda s: s)
    return pl.pallas_call(_cast_kernel, grid=(steps,), in_specs=c_in, out_specs=c_out, out_shape=c_shapes,
                          compiler_params=_params(("parallel",)), name="cast_weights")(*weights)


def _ffn_kernel(x_ref, g_ref, wg_ref, wu_ref, wd_ref, fg_ref, o_ref, h_ref, acc_ref, *, final):
    f = pl.program_id(1)

    @pl.when(f == 0)
    def _():
        h_ref[...] = _rms(x_ref[...], g_ref[...]).astype(BF16)
        acc_ref[...] = jnp.zeros_like(acc_ref)

    h = h_ref[...]
    gate = _dot(h, wg_ref[...])
    up = _dot(h, wu_ref[...])
    acc_ref[...] += _dot((_silu(gate) * up).astype(BF16), wd_ref[...])

    @pl.when(f == pl.num_programs(1) - 1)
    def _():
        y = x_ref[...] + 0.5 * acc_ref[...]
        if final:
            y = _rms(y, fg_ref[...])
        o_ref[...] = y


def _ffn(x, g, wg, wu, wd, layer, final_g, *, final, tm=512, tf=512):
    n, d = x.shape
    dff = wg.shape[2]
    return pl.pallas_call(
        functools.partial(_ffn_kernel, final=final),
        grid=(n // tm, dff // tf),
        in_specs=[
            pl.BlockSpec((tm, d), lambda i, f: (i, 0)),
            pl.BlockSpec((1, d), lambda i, f: (0, 0)),
            pl.BlockSpec((None, d, tf), lambda i, f: (layer, 0, f)),
            pl.BlockSpec((None, d, tf), lambda i, f: (layer, 0, f)),
            pl.BlockSpec((None, tf, d), lambda i, f: (layer, f, 0)),
            pl.BlockSpec((1, d), lambda i, f: (0, 0)),
        ],
        out_specs=pl.BlockSpec((tm, d), lambda i, f: (i, 0)),
        out_shape=jax.ShapeDtypeStruct((n, d), F32),
        scratch_shapes=[pltpu.VMEM((tm, d), BF16), pltpu.VMEM((tm, d), F32)],
        compiler_params=_params(("parallel", "arbitrary")),
        name="ffn",
    )(x, g.reshape(1, d), wg, wu, wd, final_g.reshape(1, d))


def _in_proj_kernel(x_ref, g_ref, w_ref, ob_ref, of_ref, h_ref, *, nb):
    j = pl.program_id(1)

    @pl.when(j == 0)
    def _():
        h_ref[...] = _rms(x_ref[...], g_ref[...]).astype(BF16)

    @pl.when(j < nb)
    def _():
        ob_ref[...] = _dot(h_ref[...], w_ref[...]).astype(BF16)

    @pl.when(j >= nb)
    def _():
        of_ref[...] = _dot(h_ref[...], w_ref[...])


def _in_proj(x, g, w, cols_bf16, *, tm=1024, tn=1024):
    n, d = x.shape
    c = w.shape[1]
    nb = cols_bf16 // tn
    return pl.pallas_call(
        functools.partial(_in_proj_kernel, nb=nb),
        grid=(n // tm, c // tn),
        in_specs=[
            pl.BlockSpec((tm, d), lambda i, j: (i, 0)),
            pl.BlockSpec((1, d), lambda i, j: (0, 0)),
            pl.BlockSpec((d, tn), lambda i, j: (0, j)),
        ],
        out_specs=[pl.BlockSpec((tm, tn), lambda i, j: (i, jnp.minimum(j, nb - 1))),
                   pl.BlockSpec((tm, tn), lambda i, j: (i, jnp.maximum(j - nb, 0)))],
        out_shape=[jax.ShapeDtypeStruct((n, cols_bf16), BF16),
                   jax.ShapeDtypeStruct((n, c - cols_bf16), F32)],
        scratch_shapes=[pltpu.VMEM((tm, d), BF16)],
        compiler_params=_params(("parallel", "arbitrary")),
        name="in_proj",
    )(x, g.reshape(1, d), w)


SB_GROUP = 8
SB_DEAD = 104.0


def _sb_kernel(*refs, n_cast):
    q_ref, k_ref, v_ref = refs[:3]
    o_ref = refs[3 + n_cast]
    kbd_ref, vbd_ref = refs[4 + 2 * n_cast:]
    _cast_slabs(refs[3:3 + n_cast], refs[4 + n_cast:4 + 2 * n_cast])
    _sb_body(q_ref, k_ref, v_ref, o_ref, kbd_ref, vbd_ref)


def _sb_body(q_ref, k_ref, v_ref, o_ref, kbd_ref, vbd_ref):
    i = pl.program_id(2)
    pairs = SB_GROUP // 2
    nkb = k_ref.shape[0] // QB
    r = lax.broadcasted_iota(jnp.int32, (QB, QB), 0)
    c = lax.broadcasted_iota(jnp.int32, (QB, QB), 1)
    strict_all = jnp.concatenate([c < r] * SB_GROUP, axis=0)
    later_ones = jnp.concatenate([(r > c).astype(BF16), jnp.ones((QB, QB), BF16)], axis=1)
    later_ones = jnp.concatenate([later_ones, later_ones], axis=0)

    @pl.when(i == 0)
    def _():
        low = c < HEAD_DIM
        zero16 = jnp.zeros((QB, QB), BF16)

        def build(j, carry):
            src = pl.ds(pl.multiple_of(j * QB, QB), QB)
            dst = pl.ds(pl.multiple_of(j * 2 * QB, 2 * QB), 2 * QB)
            for pp in range(pairs):
                for ref, out in ((k_ref, kbd_ref), (v_ref, vbd_ref)):
                    x = ref[src, pp * LANES:(pp + 1) * LANES]
                    out[pp, dst, :] = jnp.concatenate([jnp.where(low, x, zero16), jnp.where(low, zero16, x)],
                                                      axis=0)
            return carry

        lax.fori_loop(0, nkb, build, 0)

    def scores(j, diagonal):
        rows = pl.ds(pl.multiple_of(j * 2 * QB, 2 * QB), 2 * QB)
        zs = []
        for pp in range(pairs):
            z_pair = _dot_nt(q_ref[:, pp * LANES:(pp + 1) * LANES], kbd_ref[pp, rows, :])
            zs += [z_pair[:, 0:QB], z_pair[:, QB:2 * QB]]
        z = jnp.concatenate(zs, axis=0)
        soft = jnp.maximum(z, 0.0) + jnp.log(1.0 + jnp.exp(-jnp.abs(z)))
        if diagonal:
            soft = jnp.where(strict_all, soft, 0.0)
        hi = soft.astype(BF16)
        lo = (soft - hi.astype(F32)).astype(BF16)
        sums = _dot(jnp.concatenate([hi, lo], axis=1), later_ones)
        return rows, z - soft, sums

    def absorb(tails, accs, rows, log_beta, sums, diagonal):
        a = jnp.exp(log_beta - (tails + sums[:, 0:QB]))
        if diagonal:
            a = jnp.where(strict_all, a, 0.0)
        a = a.astype(BF16)
        outs = []
        for pp in range(pairs):
            pair = jnp.concatenate([a[2 * pp * QB:(2 * pp + 1) * QB], a[(2 * pp + 1) * QB:(2 * pp + 2) * QB]],
                                   axis=1)
            outs.append(_dot(pair, vbd_ref[pp, rows, :]))
        return tails + sums[:, QB:2 * QB], accs + jnp.concatenate(outs, axis=0)

    def sweep(js, carry):
        staged = [scores(j, False) for j in js]
        for st in staged:
            carry = absorb(*carry, *st, False)
        return carry

    carry = (jnp.zeros((SB_GROUP * QB, QB), F32), jnp.zeros((pairs * QB, LANES), F32))
    carry = absorb(*carry, *scores(i, True), True)
    odd = i % 2
    carry = lax.fori_loop(0, odd, lambda step, cr: sweep([i - 1], cr), carry)
    top = i - 1 - odd

    def live(state):
        step, smallest, _, _ = state
        return (step < i // 2) & (smallest < SB_DEAD)

    def pair_step(state):
        step, _, tails, accs = state
        tails, accs = sweep([top - 2 * step, top - 2 * step - 1], (tails, accs))
        return step + 1, jnp.min(tails), tails, accs

    accs = lax.while_loop(live, pair_step, (0, jnp.min(carry[0]), *carry))[3]
    for pp in range(pairs):
        o_ref[:, pp * LANES:(pp + 1) * LANES] = accs[pp * QB:(pp + 1) * QB]


def _sb_attention(pb, batch, seq, cast=(), cast_layer=()):
    nqb = seq // QB
    w = SB_GROUP * HEAD_DIM
    assert SB_HEADS == SB_GROUP
    qc, kc, vc = PB_SB_Q // w, PB_SB_K // w, PB_SB_V // w
    c_in, c_out, c_shapes = _cast_specs(cast, cast_layer, batch * nqb, lambda b, p, i: b * nqb + i)
    out = pl.pallas_call(
        functools.partial(_sb_kernel, n_cast=len(cast)),
        grid=(batch, SB_HEADS // SB_GROUP, nqb),
        in_specs=[
            pl.BlockSpec((QB, w), lambda b, p, i: (b * nqb + i, qc + p)),
            pl.BlockSpec((seq, w), lambda b, p, i: (b, kc + p)),
            pl.BlockSpec((seq, w), lambda b, p, i: (b, vc + p)),
        ] + c_in,
        out_specs=[pl.BlockSpec((QB, w), lambda b, p, i: (b * nqb + i, p))] + c_out,
        out_shape=[jax.ShapeDtypeStruct((batch * seq, SB_WIDTH), F32)] + c_shapes,
        scratch_shapes=[pltpu.VMEM((SB_GROUP // 2, 2 * seq, LANES), BF16),
                        pltpu.VMEM((SB_GROUP // 2, 2 * seq, LANES), BF16)],
        compiler_params=_params(("parallel", "parallel", "arbitrary")),
        name="sb_attention",
    )(pb, pb, pb, *cast)
    return out[0], out[1:]


CMP_TQ = 4 * QB


def _nsa_cmp_kernel(q_ref, uk_ref, uv_ref, pos_ref, w_ref, b0_ref, b1_ref, b2_ref, b3_ref, ovt_ref,
                    ocmp_ref, sel_ref, kc_ref, vc_ref, *, n_cmp, n_sel):
    i = pl.program_id(2)
    half = CMP_STRIDE * HEAD_DIM

    @pl.when(i == 0)
    def _():
        def compress(kv, u_ref):
            u = u_ref[0, 0]
            top = _dot((u + pos_ref[kv, 0:1, :]).astype(BF16), w_ref[kv, 0:half, :])
            bot = _dot((u + pos_ref[kv, 1:2, :]).astype(BF16), w_ref[kv, half:2 * half, :])
            return top + pltpu.roll(bot, QB - 1, 0)

        zeros = jnp.zeros((QB, HEAD_DIM), F32)
        kc_ref[...] = jnp.concatenate([compress(0, uk_ref), zeros], axis=1).astype(BF16)
        v = compress(1, uv_ref)
        vc_ref[...] = jnp.concatenate([jnp.concatenate([v, zeros], axis=1),
                                       jnp.concatenate([zeros, v], axis=1)], axis=0).astype(BF16)

    G = NSA_GQA
    TQ = CMP_TQ
    q4 = jnp.concatenate([q_ref[:, g * LANES:(g + 1) * LANES] for g in range(G)], axis=0)
    bias = jnp.concatenate([b[...] for b in (b0_ref, b1_ref, b2_ref, b3_ref)], axis=0)
    s = _dot_nt(q4, kc_ref[...]) + bias
    e = jnp.exp(s - jnp.max(s, axis=-1, keepdims=True))
    t_rows = i * TQ + lax.broadcasted_iota(jnp.int32, (TQ, 1), 0)
    any_valid = jnp.concatenate([t_rows >= CMP_BLOCK - 1] * G, axis=0)
    p = jnp.where(any_valid, e / jnp.sum(e, axis=-1, keepdims=True), 0.0)
    p16 = p.astype(BF16)
    for pp in range(G // 2):
        pair = jnp.concatenate([p16[2 * pp * TQ:(2 * pp + 1) * TQ], p16[(2 * pp + 1) * TQ:(2 * pp + 2) * TQ]],
                               axis=1)
        ocmp_ref[:, pp * LANES:(pp + 1) * LANES] = _dot(pair, vc_ref[...])
    p_all = p[0:TQ]
    for g in range(1, G):
        p_all = p_all + p[g * TQ:(g + 1) * TQ]

    blk = lax.broadcasted_iota(jnp.int32, (n_sel, QB), 0)
    for part in range(TQ // QB):
        rs = slice(part * QB, (part + 1) * QB)
        t0 = (i * (TQ // QB) + part) * QB
        p_sum = p_all[rs]
        hi = p_sum.astype(BF16)
        lo = (p_sum - hi.astype(F32)).astype(BF16)
        p_sel = (_dot_nt(ovt_ref[...], hi) + _dot_nt(ovt_ref[...], lo))[0:n_sel]
        t = t0 + lax.broadcasted_iota(jnp.int32, (n_sel, QB), 1)
        cur = t // SEL_BLOCK
        eligible = blk * SEL_BLOCK <= t
        forced = (blk == 0) | (blk == cur) | (blk == cur - 1)
        score = jnp.where(eligible, p_sel + jnp.where(forced, FORCE_SCORE, 0.0), NEG_INF)
        rank = jnp.zeros((n_sel, QB), F32)
        for j in range(n_sel):
            other = score[j:j + 1, :]
            ahead = (other > score) | ((other == score) & (blk > j))
            rank = rank + jnp.where(ahead, 1.0, 0.0)
        chosen = jnp.where(eligible & (rank < SEL_TOPK), 1.0, 0.0)
        chosen = jnp.concatenate([chosen, jnp.zeros((QB - n_sel, QB), F32)], axis=0)
        sel_ref[0, 0, rs, :] = chosen.T.astype(BF16)


def _nsa_cmp(pb, ucmp, pos2, cmp_w, bias_c, overlap_t, batch, seq):
    nqb = seq // CMP_TQ
    n_cmp = (seq - CMP_BLOCK) // CMP_STRIDE + 1
    n_sel = seq // SEL_BLOCK
    ng = seq // CMP_STRIDE
    assert ng == QB and n_sel <= QB and n_sel % 8 == 0
    wide = CMP_STRIDE * HEAD_DIM
    return pl.pallas_call(
        functools.partial(_nsa_cmp_kernel, n_cmp=n_cmp, n_sel=n_sel),
        grid=(batch, NSA_KV_HEADS, nqb),
        in_specs=[
            pl.BlockSpec((CMP_TQ, NSA_GQA * LANES), lambda b, h, i: (b * nqb + i, h)),
            pl.BlockSpec((1, 1, ng, wide), lambda b, h, i: (b, h, 0, 0)),
            pl.BlockSpec((1, 1, ng, wide), lambda b, h, i: (b, NSA_KV_HEADS + h, 0, 0)),
            pl.BlockSpec((2, 2, wide), lambda b, h, i: (0, 0, 0)),
            pl.BlockSpec((2, 2 * wide, HEAD_DIM), lambda b, h, i: (0, 0, 0)),
            *[pl.BlockSpec((None, CMP_TQ, QB), lambda b, h, i, g=g: (h * NSA_GQA + g, i, 0))
              for g in range(NSA_GQA)],
            pl.BlockSpec((QB, QB), lambda b, h, i: (0, 0)),
        ],
        out_specs=[
            pl.BlockSpec((CMP_TQ, 256), lambda b, h, i: (b * nqb + i, h)),
            pl.BlockSpec((1, 1, CMP_TQ, QB), lambda b, h, i: (b, h, i, 0)),
        ],
        out_shape=[
            jax.ShapeDtypeStruct((batch * seq, NSA_WIDTH), F32),
            jax.ShapeDtypeStruct((batch, NSA_KV_HEADS, seq, QB), BF16),
        ],
        scratch_shapes=[pltpu.VMEM((QB, LANES), BF16), pltpu.VMEM((2 * QB, LANES), BF16)],
        compiler_params=_params(("parallel", "parallel", "arbitrary")),
        name="nsa_compressed",
    )(pb, ucmp, ucmp, pos2, cmp_w, *[bias_c] * NSA_GQA, overlap_t)


SW_TQ = 2 * QB
SEL_CHUNK = 4 * QB
WIN_FAR = WINDOW - QB


def _nsa_sw_kernel(*refs, n_cast):
    n_in = 8
    _cast_slabs(refs[n_in:n_in + n_cast], refs[n_in + n_cast + 1:n_in + 2 * n_cast + 1])
    _nsa_sw_body(*refs[:n_in], refs[n_in + n_cast], *refs[n_in + 2 * n_cast + 1:])


def _nsa_sw_body(q_ref, ks_ref, kw_ref, ocmp_ref, misc_ref, sel_ref, exp_ref, tb_ref, o_ref, madd_ref):
    i = pl.program_id(2)
    G = NSA_GQA
    TQ = SW_TQ
    q4 = jnp.concatenate([q_ref[:, g * LANES:(g + 1) * LANES] for g in range(G)], axis=0)
    madd_ref[...] = (_dot(sel_ref[0, 0], exp_ref[...]) - 1.0) * (-NEG_INF)
    t_pos = i * TQ + lax.broadcasted_iota(jnp.int32, (TQ, 1), 0)

    low = lax.broadcasted_iota(jnp.int32, (QB, LANES), 1) < HEAD_DIM

    def tile(x, n=G):
        return jnp.concatenate([x] * n, axis=0)

    def ones_v(kv):
        return jnp.where(tile(low, kv.shape[0] // QB), jnp.ones_like(kv), kv)

    first = 2 * i - 1
    rows_m = pl.ds(pl.multiple_of(jnp.maximum(first, 0) * QB, QB), QB)
    rows_a = pl.ds(pl.multiple_of(2 * i * QB, QB), QB)
    rows_b = pl.ds(pl.multiple_of((2 * i + 1) * QB, QB), QB)
    gone = jnp.where(i >= 1, 0.0, NEG_INF)
    gone_m = jnp.concatenate([jnp.full((TQ, QB), gone, F32), jnp.zeros((TQ, 2 * QB), F32)], axis=1)
    far_end = first * QB

    kv = jnp.concatenate([ks_ref[rows_m, :], ks_ref[rows_a, :], ks_ref[rows_b, :]], axis=0)
    near_sel = jnp.concatenate([madd_ref[:, rows_m], madd_ref[:, rows_a], madd_ref[:, rows_b]], axis=1)
    s = _dot_nt(q4, kv) + (tb_ref[0] + tile(near_sel + gone_m))
    m = jnp.max(s, axis=-1, keepdims=True)
    acc = _dot(jnp.exp(s - m).astype(BF16), ones_v(kv))

    def sel_far(cidx, state):
        m, acc = state
        cols = pl.ds(pl.multiple_of(cidx * SEL_CHUNK, SEL_CHUNK), SEL_CHUNK)
        k_pos = cidx * SEL_CHUNK + lax.broadcasted_iota(jnp.int32, (TQ, SEL_CHUNK), 1)
        add = jnp.where(k_pos < far_end, madd_ref[:, cols], NEG_INF)
        kv = ks_ref[cols, :]
        s = _dot_nt(q4, kv) + tile(add)
        m_new = jnp.maximum(m, jnp.max(s, axis=-1, keepdims=True))
        return m_new, jnp.exp(m - m_new) * acc + _dot(jnp.exp(s - m_new).astype(BF16), ones_v(kv))

    _, sel_acc = lax.fori_loop(0, (2 * i + 2) // (SEL_CHUNK // QB), sel_far, (m, acc))

    start = jnp.maximum(first - WIN_FAR // QB, 0) * QB
    rows_far = pl.ds(pl.multiple_of(start, QB), WIN_FAR)
    k_pos = start + lax.broadcasted_iota(jnp.int32, (TQ, WIN_FAR), 1)
    live = (k_pos < far_end) & (k_pos > t_pos - WINDOW)
    kv = jnp.concatenate([kw_ref[rows_far, :], kw_ref[rows_m, :], kw_ref[rows_a, :], kw_ref[rows_b, :]], axis=0)
    bias = jnp.concatenate([tile(jnp.where(live, 0.0, NEG_INF)), tb_ref[0] + tile(gone_m)], axis=1)
    s = _dot_nt(q4, kv) + bias
    win_acc = _dot(jnp.exp(s - jnp.max(s, axis=-1, keepdims=True)).astype(BF16), ones_v(kv))

    gate = _sigmoid(misc_ref[...])
    low = tile(low, TQ // QB)

    def pair(acc, pp):
        a, b = acc[2 * pp * TQ:(2 * pp + 1) * TQ], acc[(2 * pp + 1) * TQ:(2 * pp + 2) * TQ]
        return (jnp.where(low, pltpu.roll(a, HEAD_DIM, 1), b)
                / jnp.where(low, a, pltpu.roll(b, HEAD_DIM, 1)))

    def pair_gate(branch, pp):
        ca, cb = 3 * (2 * pp) + branch, 3 * (2 * pp + 1) + branch
        return jnp.where(low, gate[:, ca:ca + 1], gate[:, cb:cb + 1])

    for pp in range(G // 2):
        lanes = slice(pp * LANES, (pp + 1) * LANES)
        o_ref[:, lanes] = (pair_gate(0, pp) * ocmp_ref[:, lanes] + pair_gate(1, pp) * pair(sel_acc, pp)
                           + pair_gate(2, pp) * pair(win_acc, pp))


def _nsa_sel_win(pb, pf, o_cmp, sel, expand, tb, batch, seq, cast=(), cast_layer=()):
    nqb = seq // SW_TQ
    G = NSA_GQA
    c_in, c_out, c_shapes = _cast_specs(cast, cast_layer, batch * NSA_KV_HEADS * nqb,
                                        lambda b, h, i: (b * NSA_KV_HEADS + h) * nqb + i)
    out = pl.pallas_call(
        functools.partial(_nsa_sw_kernel, n_cast=len(cast)),
        grid=(batch, NSA_KV_HEADS, nqb),
        in_specs=[
            pl.BlockSpec((SW_TQ, G * LANES), lambda b, h, i: (b * nqb + i, h)),
            pl.BlockSpec((seq, LANES), lambda b, h, i: (b, PB_SEL // LANES + h)),
            pl.BlockSpec((seq, LANES), lambda b, h, i: (b, PB_WIN // LANES + h)),
            pl.BlockSpec((SW_TQ, 256), lambda b, h, i: (b * nqb + i, h)),
            pl.BlockSpec((SW_TQ, LANES), lambda b, h, i: (b * nqb + i, PF_MISC // LANES + h)),
            pl.BlockSpec((1, 1, SW_TQ, QB), lambda b, h, i: (b, h, i, 0)),
            pl.BlockSpec((QB, seq), lambda b, h, i: (0, 0)),
            pl.BlockSpec((1, G * SW_TQ, 3 * QB), lambda b, h, i: (h, 0, 0)),
        ] + c_in,
        out_specs=[pl.BlockSpec((SW_TQ, 256), lambda b, h, i: (b * nqb + i, h))] + c_out,
        out_shape=[jax.ShapeDtypeStruct((batch * seq, NSA_WIDTH), F32)] + c_shapes,
        scratch_shapes=[pltpu.VMEM((SW_TQ, seq), F32)],
        compiler_params=_params(("parallel", "parallel", "arbitrary")),
        name="nsa_selected_window",
    )(pb, pb, pb, o_cmp, pf, sel, expand, tb, *cast)
    return out[0], out[1:]


def _ssd_kernel(*refs, n_cast):
    n_in = 12
    _cast_slabs(refs[n_in:n_in + n_cast], refs[n_in + n_cast + 1:n_in + 2 * n_cast + 1])
    _ssd_body(*refs[:n_in], refs[n_in + n_cast], *refs[n_in + 2 * n_cast + 1:])


def _ssd_body(z_ref, xs_ref, bc_ref, misc_ref, cwx_ref, cwb_ref, cbx_ref, cbb_ref, hp_ref, spread_ref,
              dskip_ref, ng_ref, o_ref, xbuf, bbuf, state):
    ci = pl.program_id(1)
    L = QB
    P = HEAD_DIM
    GN = SSD_GROUPS * SSD_STATE

    @pl.when(ci == 0)
    def _():
        xbuf[...] = jnp.zeros_like(xbuf)
        bbuf[...] = jnp.zeros_like(bbuf)
        state[...] = jnp.zeros_like(state)

    def conv_silu(buf, src_ref, w_ref, b_ref):
        cur = src_ref[...]
        ext = jnp.concatenate([buf[...], cur], axis=0)
        out = b_ref[...] + w_ref[SSD_CONV - 1:SSD_CONV, :] * cur
        for k in range(1, SSD_CONV):
            out = out + w_ref[SSD_CONV - 1 - k:SSD_CONV - k, :] * pltpu.roll(ext, k, 0)[8:8 + L]
        buf[...] = cur[L - 8:L]
        return _silu(out)

    xs = conv_silu(xbuf, xs_ref, cwx_ref, cbx_ref)
    bcs = conv_silu(bbuf, bc_ref, cwb_ref, cbb_ref)

    misc = misc_ref[...]
    pre = misc + hp_ref[0:1, :]
    dt = jnp.maximum(pre, 0.0) + jnp.log1p(jnp.exp(-jnp.abs(pre)))
    a_dt = dt * (-jnp.exp(hp_ref[1:2, :]))
    r = lax.broadcasted_iota(jnp.int32, (L, L), 0)
    c = lax.broadcasted_iota(jnp.int32, (L, L), 1)
    causal = r >= c
    a_cs = _split3_dot_left(causal.astype(BF16), a_dt)
    a_cs_t = a_cs.T

    dt_full = _split3_dot(dt, spread_ref[...])
    acs_full = _split3_dot(a_cs, spread_ref[...])
    total_full = acs_full[L - 1:L, :]
    xdt = xs * dt_full
    xdt16 = xdt.astype(BF16)
    to_end16 = (xdt * jnp.exp(total_full - acs_full)).astype(BF16)
    decay_in = jnp.exp(acs_full)
    chunk_decay = jnp.exp(total_full)
    low = c < P
    zero16 = jnp.zeros((L, 2 * P), BF16)
    width = SSD_HPG * P

    ys = []
    for g in range(SSD_GROUPS):
        bm = bcs[:, g * SSD_STATE:(g + 1) * SSD_STATE]
        cm16 = bcs[:, GN + g * SSD_STATE:GN + (g + 1) * SSD_STATE].astype(BF16)
        cb = _dot_nt(cm16, bm.astype(BF16))
        lanes = slice(g * width, (g + 1) * width)
        h_in = state[g]
        y_off = _dot(cm16, h_in.astype(BF16)) * decay_in[:, lanes]
        state[g] = h_in * chunk_decay[:, lanes] + _dot(bm.T.astype(BF16), to_end16[:, lanes])
        diag = []
        for pp in range(SSD_HPG // 2):
            hd = g * SSD_HPG + 2 * pp
            decayed = []
            for col in (MISC_DT + hd, MISC_DT + hd + 1):
                seg = jnp.exp(jnp.where(causal, a_cs[:, col:col + 1] - a_cs_t[col:col + 1, :], -jnp.inf))
                decayed.append((cb * seg).astype(BF16))
            xp = xdt16[:, hd * P:(hd + 2) * P]
            x_bd = jnp.concatenate([jnp.where(low, xp, zero16), jnp.where(low, zero16, xp)], axis=0)
            diag.append(_dot(jnp.concatenate(decayed, axis=1), x_bd))
        ys.append(jnp.concatenate(diag, axis=1) + y_off)

    y = (jnp.concatenate(ys, axis=1) + dskip_ref[...] * xs) * _silu(z_ref[...])
    for g in range(SSD_GROUPS):
        lanes = slice(g * width, (g + 1) * width)
        yg = y[:, lanes]
        o_ref[:, lanes] = yg * lax.rsqrt(jnp.mean(yg * yg, axis=-1, keepdims=True) + EPS) * ng_ref[:, lanes]


def _ssd(pf, conv_w, conv_b, head_params, d_skip, norm_g, batch, seq, cast=(), cast_layer=()):
    nc = seq // QB
    c_in, c_out, c_shapes = _cast_specs(cast, cast_layer, batch * nc, lambda b, ci: b * nc + ci)
    GN = SSD_GROUPS * SSD_STATE
    lane_head = jnp.arange(SSD_INNER)[None, :] // HEAD_DIM
    spread = (jnp.arange(LANES)[:, None] == MISC_DT + lane_head).astype(BF16)
    cwx, cwb = conv_w[:, :SSD_INNER], conv_w[:, SSD_INNER:]
    cbx, cbb = conv_b[:SSD_INNER].reshape(1, -1), conv_b[SSD_INNER:].reshape(1, -1)
    full = lambda shape: pl.BlockSpec(shape, lambda b, ci: (0,) * len(shape))
    out = pl.pallas_call(
        functools.partial(_ssd_kernel, n_cast=len(cast)),
        grid=(batch, nc),
        in_specs=[
            pl.BlockSpec((QB, SSD_INNER), lambda b, ci: (b * nc + ci, PF_Z // SSD_INNER)),
            pl.BlockSpec((QB, SSD_INNER), lambda b, ci: (b * nc + ci, PF_XS // SSD_INNER)),
            pl.BlockSpec((QB, 2 * GN), lambda b, ci: (b * nc + ci, PF_BC // (2 * GN))),
            pl.BlockSpec((QB, LANES), lambda b, ci: (b * nc + ci, PF_MISC // LANES)),
            full((SSD_CONV, SSD_INNER)),
            full((SSD_CONV, 2 * GN)),
            full((1, SSD_INNER)),
            full((1, 2 * GN)),
            full((8, LANES)),
            full((LANES, SSD_INNER)),
            full((1, SSD_INNER)),
            full((1, SSD_INNER)),
        ] + c_in,
        out_specs=[pl.BlockSpec((QB, SSD_INNER), lambda b, ci: (b * nc + ci, 0))] + c_out,
        out_shape=[jax.ShapeDtypeStruct((batch * seq, SSD_INNER), F32)] + c_shapes,
        scratch_shapes=[
            pltpu.VMEM((8, SSD_INNER), F32),
            pltpu.VMEM((8, 2 * GN), F32),
            pltpu.VMEM((SSD_GROUPS, SSD_STATE, SSD_HPG * HEAD_DIM), F32),
        ],
        compiler_params=_params(("parallel", "arbitrary")),
        name="ssd",
    )(pf, pf, pf, pf, cwx, cwb, cbx, cbb, head_params, spread,
      jnp.repeat(d_skip.astype(F32), HEAD_DIM).reshape(1, -1), norm_g.reshape(1, -1), *cast)
    return out[0], out[1:]


def _out_proj_kernel(x_ref, nsa_ref, sb_ref, ssd_ref, gn_ref, gs_ref, w_ref, o_ref, mix_ref):
    @pl.when(pl.program_id(1) == 0)
    def _():
        mix_ref[:, 0:NSA_WIDTH] = _rms(nsa_ref[...], gn_ref[...]).astype(BF16)
        mix_ref[:, NSA_WIDTH:NSA_WIDTH + SB_WIDTH] = _rms(sb_ref[...], gs_ref[...]).astype(BF16)
        mix_ref[:, NSA_WIDTH + SB_WIDTH:] = ssd_ref[...].astype(BF16)

    o_ref[...] = x_ref[...] + _dot(mix_ref[...], w_ref[...])


def _out_proj(x, o_nsa, o_sb, o_ssd, g_nsa, g_sb, w, *, tm=1024, tn=1024):
    n, d = x.shape
    dm = w.shape[0]
    return pl.pallas_call(
        _out_proj_kernel,
        grid=(n // tm, d // tn),
        in_specs=[
            pl.BlockSpec((tm, tn), lambda i, j: (i, j)),
            pl.BlockSpec((tm, NSA_WIDTH), lambda i, j: (i, 0)),
            pl.BlockSpec((tm, SB_WIDTH), lambda i, j: (i, 0)),
            pl.BlockSpec((tm, SSD_INNER), lambda i, j: (i, 0)),
            pl.BlockSpec((1, NSA_WIDTH), lambda i, j: (0, 0)),
            pl.BlockSpec((1, SB_WIDTH), lambda i, j: (0, 0)),
            pl.BlockSpec((dm, tn), lambda i, j: (0, j)),
        ],
        out_specs=pl.BlockSpec((tm, tn), lambda i, j: (i, j)),
        out_shape=jax.ShapeDtypeStruct((n, d), F32),
        scratch_shapes=[pltpu.VMEM((tm, dm), BF16)],
        compiler_params=_params(("parallel", "arbitrary")),
        name="out_proj",
    )(x, o_nsa, o_sb, o_ssd, g_nsa.reshape(1, -1), g_sb.reshape(1, -1), w)


def _rel_bucket(dist):
    dist = jnp.maximum(dist, 0)
    max_exact = REL_BUCKETS // 2
    log_ratio = jnp.log(jnp.maximum(dist, 1).astype(F32) / max_exact) / math.log(REL_MAX_DIST / max_exact)
    large = jnp.minimum(max_exact + (log_ratio * (REL_BUCKETS - max_exact)).astype(jnp.int32), REL_BUCKETS - 1)
    return jnp.where(dist < max_exact, dist, large)


def _bias_tables(rel_bias, seq):
    assert QB >= REL_MAX_DIST
    def lookup(dist):
        buckets = jnp.arange(REL_BUCKETS).reshape((-1,) + (1,) * dist.ndim)
        onehot = (_rel_bucket(dist)[None] == buckets).astype(F32)
        return jnp.einsum('kh,k...->h...', rel_bias, onehot, precision=lax.Precision.HIGHEST)

    t = jnp.arange(seq)[:, None]
    cend = jnp.arange(QB)[None, :] * CMP_STRIDE + CMP_BLOCK - 1
    n_cmp = (seq - CMP_BLOCK) // CMP_STRIDE + 1
    valid_c = (t >= cend) & (jnp.arange(QB)[None, :] < n_cmp)
    bias_c = jnp.where(valid_c, lookup(t - cend), NEG_INF)
    r = jnp.arange(QB)[:, None]
    m = jnp.arange(QB)[None, :]
    near = jnp.stack([lookup(r - m), lookup(QB + r - m)])
    near = near - rel_bias[REL_BUCKETS - 1][None, :, None, None]
    diag = near[0] + jnp.where(m > r, NEG_INF, 0.0)
    prev = near[1]
    rows_a = jnp.concatenate([prev, diag, jnp.full_like(diag, NEG_INF)], axis=2)
    rows_b = jnp.concatenate([jnp.zeros_like(diag), prev, diag], axis=2)
    near2 = jnp.concatenate([rows_a, rows_b], axis=1)
    return bias_c, near2.reshape(NSA_KV_HEADS, NSA_GQA * SW_TQ, 3 * QB)


def _expand_table(seq):
    j = jnp.arange(QB)[:, None]
    s = jnp.arange(seq)[None, :]
    return (s // SEL_BLOCK == j).astype(BF16)


def _overlap_table(seq):
    n_cmp = (seq - CMP_BLOCK) // CMP_STRIDE + 1
    cs = jnp.arange(QB)[:, None] * CMP_STRIDE
    ce = cs + CMP_BLOCK - 1
    ss = jnp.arange(QB)[None, :] * SEL_BLOCK
    ov = jnp.maximum(jnp.minimum(ce, ss + SEL_BLOCK - 1) - jnp.maximum(cs, ss) + 1, 0).astype(F32) / CMP_BLOCK
    keep = (jnp.arange(QB)[:, None] < n_cmp) & (jnp.arange(QB)[None, :] < seq // SEL_BLOCK)
    return jnp.where(keep, ov, 0.0).T.astype(BF16)


def _in_proj_weights(w):
    w = w.astype(BF16)
    scale = HEAD_DIM ** -0.5
    o_q, o_kv, o_gate = 0, NSA_WIDTH, NSA_WIDTH + 768
    o_sb = o_gate + 3 * NSA_HEADS
    o_z = o_sb + 3 * SB_WIDTH
    o_xbc = o_z + SSD_INNER
    o_dt = o_xbc + SSD_INNER + 2 * SSD_GROUPS * SSD_STATE
    col = lambda a, n: w[:, a:a + n]
    kv = lambda br, which, h: col(o_kv + br * 256 + which * 128 + h * HEAD_DIM, HEAD_DIM)
    zeros = lambda n: jnp.zeros((w.shape[0], n), w.dtype)
    branch = lambda br: [kv(br, 0, 0), kv(br, 1, 0), kv(br, 0, 1), kv(br, 1, 1)]
    q_heads = []
    for h in range(NSA_HEADS):
        q_heads += [col(o_q + h * HEAD_DIM, HEAD_DIM) * scale, zeros(LANES - HEAD_DIM)]
    part_bf16 = (q_heads + branch(1) + branch(2)
                 + [col(o_sb, SB_WIDTH) * scale, col(o_sb + SB_WIDTH, 2 * SB_WIDTH)])
    gates = 3 * NSA_GQA
    part_f32 = [col(o_z, 2 * SSD_INNER + 2 * SSD_GROUPS * SSD_STATE), col(o_kv, 256),
                col(o_gate, gates), zeros(MISC_DT - gates), col(o_dt, SSD_HEADS),
                zeros(LANES - MISC_DT - SSD_HEADS), col(o_gate + gates, gates), zeros(LANES - gates)]
    out = jnp.concatenate(part_bf16 + part_f32, axis=1)
    assert out.shape[1] == PB_COLS + PF_COLS and sum(p.shape[1] for p in part_bf16) == PB_COLS
    return out


def _head_params(dt_bias, a_log):
    rows = jnp.stack([dt_bias, a_log]).astype(F32)
    return jnp.zeros((8, LANES), F32).at[0:2, MISC_DT:MISC_DT + SSD_HEADS].set(rows)


def _mixer(x, l, tables, batch, seq, mix_norm, w_in, w_out, cmp_pos, cmp_w, nsa_norm, sb_norm,
           conv_w, conv_b, dt_bias, a_log, d_skip, ssd_norm, casts):
    hosts = [([w for w, _ in casts[k::3]], tuple(at for _, at in casts[k::3])) for k in range(3)]
    bias_c, near, overlap, expand = tables
    pb, pf = _in_proj(x, mix_norm[l], _in_proj_weights(w_in[l]), PB_COLS)

    ng = seq // CMP_STRIDE
    ucmp = pf[:, PF_CMP:PF_CMP + 256].reshape(batch, ng, CMP_STRIDE, 4, HEAD_DIM)
    ucmp = ucmp.transpose(0, 3, 1, 2, 4).reshape(batch, 4, ng, CMP_STRIDE * HEAD_DIM)
    pos2 = cmp_pos[l].reshape(2, 2, CMP_STRIDE * HEAD_DIM)
    o_cmp, sel = _nsa_cmp(pb, ucmp, pos2, cmp_w[l].astype(BF16), bias_c, overlap, batch, seq)
    o_nsa, w0 = _nsa_sel_win(pb, pf, o_cmp, sel, expand, near, batch, seq, *hosts[0])
    o_sb, w1 = _sb_attention(pb, batch, seq, *hosts[1])
    o_ssd, w2 = _ssd(pf, conv_w[l], conv_b[l], _head_params(dt_bias[l], a_log[l]), d_skip[l], ssd_norm[l],
                     batch, seq, *hosts[2])
    out = _out_proj(x, o_nsa, o_sb, o_ssd, nsa_norm[l], sb_norm[l], w_out[l].astype(BF16))
    done = [None] * len(casts)
    for k, ws in enumerate((w0, w1, w2)):
        done[k::3] = ws
    return out, done


def kernel(x, rel_bias, ffn1_norm, ffn1_w_gate, ffn1_w_up, ffn1_w_down, mix_norm, w_in, w_out, nsa_cmp_pos, nsa_cmp_w, nsa_out_norm, sb_out_norm, ssd_conv_w, ssd_conv_b, ssd_dt_bias, ssd_a_log, ssd_d, ssd_out_norm, ffn2_norm, ffn2_w_gate, ffn2_w_up, ffn2_w_down, final_norm):
    batch, seq, d = x.shape
    depth = w_in.shape[0]
    tables = _bias_tables(rel_bias, seq) + (_overlap_table(seq), _expand_table(seq))
    h = x.reshape(batch * seq, d)

    ffn1 = (ffn1_w_gate, ffn1_w_up, ffn1_w_down)
    ffn2 = (ffn2_w_gate, ffn2_w_up, ffn2_w_down)
    widen = lambda ws: tuple(w[None] for w in ws)
    w1 = widen(_cast_layer(ffn1, 0))
    for l in range(depth):
        h = _ffn(h, ffn1_norm[l], *w1, 0, final_norm, final=False)
        casts = [(w, l) for w in ffn2] + ([(w, l + 1) for w in ffn1] if l + 1 < depth else [])
        h, done = _mixer(h, l, tables, batch, seq, mix_norm, w_in, w_out, nsa_cmp_pos, nsa_cmp_w,
                         nsa_out_norm, sb_out_norm, ssd_conv_w, ssd_conv_b, ssd_dt_bias, ssd_a_log,
                         ssd_d, ssd_out_norm, casts)
        h = _ffn(h, ffn2_norm[l], *widen(done[0:3]), 0, final_norm, final=(l == depth - 1))
        w1 = widen(done[3:6])
    return h.reshape(batch, seq, d)
```

```python
import functools
import math

import jax
import jax.numpy as jnp
from jax import lax
from jax.experimental import pallas as pl
from jax.experimental.pallas import tpu as pltpu

LANES = 128
HEAD_DIM = 64
QB = 128
NEG_INF = -1e30
EPS = 1e-6

NSA_HEADS = 8
NSA_KV_HEADS = 2
NSA_GQA = NSA_HEADS // NSA_KV_HEADS
NSA_WIDTH = NSA_HEADS * HEAD_DIM
CMP_BLOCK = 32
CMP_STRIDE = 16
SEL_BLOCK = 64
SEL_TOPK = 8
FORCE_SCORE = 1e3
WINDOW = 512
SB_HEADS = 8
SB_WIDTH = SB_HEADS * HEAD_DIM
SSD_HEADS = 16
SSD_INNER = SSD_HEADS * HEAD_DIM
SSD_GROUPS = 2
SSD_HPG = SSD_HEADS // SSD_GROUPS
SSD_STATE = 128
SSD_CONV = 4
REL_BUCKETS = 32
REL_MAX_DIST = 128

PB_NSA_Q, PB_SEL, PB_WIN, PB_SB_Q, PB_SB_K, PB_SB_V, PB_COLS = 0, 1024, 1280, 1536, 2048, 2560, 3072
PF_Z, PF_XS, PF_BC, PF_CMP, PF_MISC, PF_COLS = 0, 1024, 2048, 2560, 2816, 3072
MISC_DT = 16

VMEM_LIMIT = 52 * 1024 * 1024

BF16 = jnp.bfloat16
F32 = jnp.float32


def _dot(a, b):
    return jnp.dot(a, b, preferred_element_type=F32)


def _dot_nt(a, b):
    return lax.dot_general(a, b, (((1,), (1,)), ((), ())), preferred_element_type=F32)


def _split3_dot(x, m):
    hi = x.astype(BF16)
    r1 = x - hi.astype(F32)
    mid = r1.astype(BF16)
    lo = (r1 - mid.astype(F32)).astype(BF16)
    return _dot(hi, m) + _dot(mid, m) + _dot(lo, m)


def _split3_dot_left(m, x):
    hi = x.astype(BF16)
    r1 = x - hi.astype(F32)
    mid = r1.astype(BF16)
    lo = (r1 - mid.astype(F32)).astype(BF16)
    return _dot(m, hi) + _dot(m, mid) + _dot(m, lo)


def _rms(x, g):
    return x * lax.rsqrt(jnp.mean(x * x, axis=-1, keepdims=True) + EPS) * g


def _silu(x):
    return x / (1.0 + jnp.exp(-x))


def _sigmoid(x):
    return 1.0 / (1.0 + jnp.exp(-x))


def _params(sem):
    return pltpu.CompilerParams(dimension_semantics=sem, vmem_limit_bytes=VMEM_LIMIT)


def _cast_specs(weights, layer, steps, step_of):
    in_specs, out_specs, out_shapes = [], [], []
    for w, at in zip(weights, layer):
        _, rows, cols = w.shape
        share = 1
        while (rows * share) % (16 * steps):
            share *= 2
        slab = rows * share // steps
        in_specs.append(pl.BlockSpec((None, slab, cols),
                                     lambda *ids, k=share, at=at: (at, step_of(*ids) // k, 0)))
        out_specs.append(pl.BlockSpec((slab, cols), lambda *ids, k=share: (step_of(*ids) // k, 0)))
        out_shapes.append(jax.ShapeDtypeStruct((rows, cols), BF16))
    return in_specs, out_specs, out_shapes


def _cast_slabs(srcs, dsts):
    for src, dst in zip(srcs, dsts):
        dst[...] = src[...].astype(BF16)


def _cast_kernel(*refs):
    _cast_slabs(refs[:len(refs) // 2], refs[len(refs) // 2:])


def _cast_layer(weights, layer, steps=16):
    c_in, c_out, c_shapes = _cast_specs(weights, (layer,) * len(weights), steps, lambda s: s)
    return pl.pallas_call(_cast_kernel, grid=(steps,), in_specs=c_in, out_specs=c_out, out_shape=c_shapes,
                          compiler_params=_params(("parallel",)), name="cast_weights")(*weights)


def _ffn_kernel(x_ref, g_ref, wg_ref, wu_ref, wd_ref, fg_ref, o_ref, h_ref, acc_ref, *, final):
    f = pl.program_id(1)

    @pl.when(f == 0)
    def _():
        h_ref[...] = _rms(x_ref[...], g_ref[...]).astype(BF16)
        acc_ref[...] = jnp.zeros_like(acc_ref)

    h = h_ref[...]
    gate = _dot(h, wg_ref[...])
    up = _dot(h, wu_ref[...])
    acc_ref[...] += _dot((_silu(gate) * up).astype(BF16), wd_ref[...])

    @pl.when(f == pl.num_programs(1) - 1)
    def _():
        y = x_ref[...] + 0.5 * acc_ref[...]
        if final:
            y = _rms(y, fg_ref[...])
        o_ref[...] = y


def _ffn(x, g, wg, wu, wd, layer, final_g, *, final, tm=512, tf=512):
    n, d = x.shape
    dff = wg.shape[2]
    return pl.pallas_call(
        functools.partial(_ffn_kernel, final=final),
        grid=(n // tm, dff // tf),
        in_specs=[
            pl.BlockSpec((tm, d), lambda i, f: (i, 0)),
            pl.BlockSpec((1, d), lambda i, f: (0, 0)),
            pl.BlockSpec((None, d, tf), lambda i, f: (layer, 0, f)),
            pl.BlockSpec((None, d, tf), lambda i, f: (layer, 0, f)),
            pl.BlockSpec((None, tf, d), lambda i, f: (layer, f, 0)),
            pl.BlockSpec((1, d), lambda i, f: (0, 0)),
        ],
        out_specs=pl.BlockSpec((tm, d), lambda i, f: (i, 0)),
        out_shape=jax.ShapeDtypeStruct((n, d), F32),
        scratch_shapes=[pltpu.VMEM((tm, d), BF16), pltpu.VMEM((tm, d), F32)],
        compiler_params=_params(("parallel", "arbitrary")),
        name="ffn",
    )(x, g.reshape(1, d), wg, wu, wd, final_g.reshape(1, d))


def _in_proj_kernel(x_ref, g_ref, w_ref, ob_ref, of_ref, h_ref, *, nb):
    j = pl.program_id(1)

    @pl.when(j == 0)
    def _():
        h_ref[...] = _rms(x_ref[...], g_ref[...]).astype(BF16)

    @pl.when(j < nb)
    def _():
        ob_ref[...] = _dot(h_ref[...], w_ref[...]).astype(BF16)

    @pl.when(j >= nb)
    def _():
        of_ref[...] = _dot(h_ref[...], w_ref[...])


def _in_proj(x, g, w, cols_bf16, *, tm=1024, tn=1024):
    n, d = x.shape
    c = w.shape[1]
    nb = cols_bf16 // tn
    return pl.pallas_call(
        functools.partial(_in_proj_kernel, nb=nb),
        grid=(n // tm, c // tn),
        in_specs=[
            pl.BlockSpec((tm, d), lambda i, j: (i, 0)),
            pl.BlockSpec((1, d), lambda i, j: (0, 0)),
            pl.BlockSpec((d, tn), lambda i, j: (0, j)),
        ],
        out_specs=[pl.BlockSpec((tm, tn), lambda i, j: (i, jnp.minimum(j, nb - 1))),
                   pl.BlockSpec((tm, tn), lambda i, j: (i, jnp.maximum(j - nb, 0)))],
        out_shape=[jax.ShapeDtypeStruct((n, cols_bf16), BF16),
                   jax.ShapeDtypeStruct((n, c - cols_bf16), F32)],
        scratch_shapes=[pltpu.VMEM((tm, d), BF16)],
        compiler_params=_params(("parallel", "arbitrary")),
        name="in_proj",
    )(x, g.reshape(1, d), w)


SB_GROUP = 8
SB_DEAD = 104.0


def _sb_kernel(*refs, n_cast):
    q_ref, k_ref, v_ref = refs[:3]
    o_ref = refs[3 + n_cast]
    kbd_ref, vbd_ref = refs[4 + 2 * n_cast:]
    _cast_slabs(refs[3:3 + n_cast], refs[4 + n_cast:4 + 2 * n_cast])
    _sb_body(q_ref, k_ref, v_ref, o_ref, kbd_ref, vbd_ref)


def _sb_body(q_ref, k_ref, v_ref, o_ref, kbd_ref, vbd_ref):
    i = pl.program_id(2)
    pairs = SB_GROUP // 2
    nkb = k_ref.shape[0] // QB
    r = lax.broadcasted_iota(jnp.int32, (QB, QB), 0)
    c = lax.broadcasted_iota(jnp.int32, (QB, QB), 1)
    strict_all = jnp.concatenate([c < r] * SB_GROUP, axis=0)
    later_ones = jnp.concatenate([(r > c).astype(BF16), jnp.ones((QB, QB), BF16)], axis=1)
    later_ones = jnp.concatenate([later_ones, later_ones], axis=0)

    @pl.when(i == 0)
    def _():
        low = c < HEAD_DIM
        zero16 = jnp.zeros((QB, QB), BF16)

        def build(j, carry):
            src = pl.ds(pl.multiple_of(j * QB, QB), QB)
            dst = pl.ds(pl.multiple_of(j * 2 * QB, 2 * QB), 2 * QB)
            for pp in range(pairs):
                for ref, out in ((k_ref, kbd_ref), (v_ref, vbd_ref)):
                    x = ref[src, pp * LANES:(pp + 1) * LANES]
                    out[pp, dst, :] = jnp.concatenate([jnp.where(low, x, zero16), jnp.where(low, zero16, x)],
                                                      axis=0)
            return carry

        lax.fori_loop(0, nkb, build, 0)

    def scores(j, diagonal):
        rows = pl.ds(pl.multiple_of(j * 2 * QB, 2 * QB), 2 * QB)
        zs = []
        for pp in range(pairs):
            z_pair = _dot_nt(q_ref[:, pp * LANES:(pp + 1) * LANES], kbd_ref[pp, rows, :])
            zs += [z_pair[:, 0:QB], z_pair[:, QB:2 * QB]]
        z = jnp.concatenate(zs, axis=0)
        soft = jnp.maximum(z, 0.0) + jnp.log(1.0 + jnp.exp(-jnp.abs(z)))
        if diagonal:
            soft = jnp.where(strict_all, soft, 0.0)
        hi = soft.astype(BF16)
        lo = (soft - hi.astype(F32)).astype(BF16)
        sums = _dot(jnp.concatenate([hi, lo], axis=1), later_ones)
        return rows, z - soft, sums

    def absorb(tails, accs, rows, log_beta, sums, diagonal):
        a = jnp.exp(log_beta - (tails + sums[:, 0:QB]))
        if diagonal:
            a = jnp.where(strict_all, a, 0.0)
        a = a.astype(BF16)
        outs = []
        for pp in range(pairs):
            pair = jnp.concatenate([a[2 * pp * QB:(2 * pp + 1) * QB], a[(2 * pp + 1) * QB:(2 * pp + 2) * QB]],
                                   axis=1)
            outs.append(_dot(pair, vbd_ref[pp, rows, :]))
        return tails + sums[:, QB:2 * QB], accs + jnp.concatenate(outs, axis=0)

    def sweep(js, carry):
        staged = [scores(j, False) for j in js]
        for st in staged:
            carry = absorb(*carry, *st, False)
        return carry

    carry = (jnp.zeros((SB_GROUP * QB, QB), F32), jnp.zeros((pairs * QB, LANES), F32))
    carry = absorb(*carry, *scores(i, True), True)
    odd = i % 2
    carry = lax.fori_loop(0, odd, lambda step, cr: sweep([i - 1], cr), carry)
    top = i - 1 - odd

    def live(state):
        step, smallest, _, _ = state
        return (step < i // 2) & (smallest < SB_DEAD)

    def pair_step(state):
        step, _, tails, accs = state
        tails, accs = sweep([top - 2 * step, top - 2 * step - 1], (tails, accs))
        return step + 1, jnp.min(tails), tails, accs

    accs = lax.while_loop(live, pair_step, (0, jnp.min(carry[0]), *carry))[3]
    for pp in range(pairs):
        o_ref[:, pp * LANES:(pp + 1) * LANES] = accs[pp * QB:(pp + 1) * QB]


def _sb_attention(pb, batch, seq, cast=(), cast_layer=()):
    nqb = seq // QB
    w = SB_GROUP * HEAD_DIM
    assert SB_HEADS == SB_GROUP
    qc, kc, vc = PB_SB_Q // w, PB_SB_K // w, PB_SB_V // w
    c_in, c_out, c_shapes = _cast_specs(cast, cast_layer, batch * nqb, lambda b, p, i: b * nqb + i)
    out = pl.pallas_call(
        functools.partial(_sb_kernel, n_cast=len(cast)),
        grid=(batch, SB_HEADS // SB_GROUP, nqb),
        in_specs=[
            pl.BlockSpec((QB, w), lambda b, p, i: (b * nqb + i, qc + p)),
            pl.BlockSpec((seq, w), lambda b, p, i: (b, kc + p)),
            pl.BlockSpec((seq, w), lambda b, p, i: (b, vc + p)),
        ] + c_in,
        out_specs=[pl.BlockSpec((QB, w), lambda b, p, i: (b * nqb + i, p))] + c_out,
        out_shape=[jax.ShapeDtypeStruct((batch * seq, SB_WIDTH), F32)] + c_shapes,
        scratch_shapes=[pltpu.VMEM((SB_GROUP // 2, 2 * seq, LANES), BF16),
                        pltpu.VMEM((SB_GROUP // 2, 2 * seq, LANES), BF16)],
        compiler_params=_params(("parallel", "parallel", "arbitrary")),
        name="sb_attention",
    )(pb, pb, pb, *cast)
    return out[0], out[1:]


CMP_TQ = 4 * QB


def _nsa_cmp_kernel(q_ref, uk_ref, uv_ref, pos_ref, w_ref, b0_ref, b1_ref, b2_ref, b3_ref, ovt_ref,
                    ocmp_ref, sel_ref, kc_ref, vc_ref, *, n_cmp, n_sel):
    i = pl.program_id(2)
    half = CMP_STRIDE * HEAD_DIM

    @pl.when(i == 0)
    def _():
        def compress(kv, u_ref):
            u = u_ref[0, 0]
            top = _dot((u + pos_ref[kv, 0:1, :]).astype(BF16), w_ref[kv, 0:half, :])
            bot = _dot((u + pos_ref[kv, 1:2, :]).astype(BF16), w_ref[kv, half:2 * half, :])
            return top + pltpu.roll(bot, QB - 1, 0)

        zeros = jnp.zeros((QB, HEAD_DIM), F32)
        kc_ref[...] = jnp.concatenate([compress(0, uk_ref), zeros], axis=1).astype(BF16)
        v = compress(1, uv_ref)
        vc_ref[...] = jnp.concatenate([jnp.concatenate([v, zeros], axis=1),
                                       jnp.concatenate([zeros, v], axis=1)], axis=0).astype(BF16)

    G = NSA_GQA
    TQ = CMP_TQ
    q4 = jnp.concatenate([q_ref[:, g * LANES:(g + 1) * LANES] for g in range(G)], axis=0)
    bias = jnp.concatenate([b[...] for b in (b0_ref, b1_ref, b2_ref, b3_ref)], axis=0)
    s = _dot_nt(q4, kc_ref[...]) + bias
    e = jnp.exp(s - jnp.max(s, axis=-1, keepdims=True))
    t_rows = i * TQ + lax.broadcasted_iota(jnp.int32, (TQ, 1), 0)
    any_valid = jnp.concatenate([t_rows >= CMP_BLOCK - 1] * G, axis=0)
    p = jnp.where(any_valid, e / jnp.sum(e, axis=-1, keepdims=True), 0.0)
    p16 = p.astype(BF16)
    for pp in range(G // 2):
        pair = jnp.concatenate([p16[2 * pp * TQ:(2 * pp + 1) * TQ], p16[(2 * pp + 1) * TQ:(2 * pp + 2) * TQ]],
                               axis=1)
        ocmp_ref[:, pp * LANES:(pp + 1) * LANES] = _dot(pair, vc_ref[...])
    p_all = p[0:TQ]
    for g in range(1, G):
        p_all = p_all + p[g * TQ:(g + 1) * TQ]

    blk = lax.broadcasted_iota(jnp.int32, (n_sel, QB), 0)
    for part in range(TQ // QB):
        rs = slice(part * QB, (part + 1) * QB)
        t0 = (i * (TQ // QB) + part) * QB
        p_sum = p_all[rs]
        hi = p_sum.astype(BF16)
        lo = (p_sum - hi.astype(F32)).astype(BF16)
        p_sel = (_dot_nt(ovt_ref[...], hi) + _dot_nt(ovt_ref[...], lo))[0:n_sel]
        t = t0 + lax.broadcasted_iota(jnp.int32, (n_sel, QB), 1)
        cur = t // SEL_BLOCK
        eligible = blk * SEL_BLOCK <= t
        forced = (blk == 0) | (blk == cur) | (blk == cur - 1)
        score = jnp.where(eligible, p_sel + jnp.where(forced, FORCE_SCORE, 0.0), NEG_INF)
        rank = jnp.zeros((n_sel, QB), F32)
        for j in range(n_sel):
            other = score[j:j + 1, :]
            ahead = (other > score) | ((other == score) & (blk > j))
            rank = rank + jnp.where(ahead, 1.0, 0.0)
        chosen = jnp.where(eligible & (rank < SEL_TOPK), 1.0, 0.0)
        chosen = jnp.concatenate([chosen, jnp.zeros((QB - n_sel, QB), F32)], axis=0)
        sel_ref[0, 0, rs, :] = chosen.T.astype(BF16)


def _nsa_cmp(pb, ucmp, pos2, cmp_w, bias_c, overlap_t, batch, seq):
    nqb = seq // CMP_TQ
    n_cmp = (seq - CMP_BLOCK) // CMP_STRIDE + 1
    n_sel = seq // SEL_BLOCK
    ng = seq // CMP_STRIDE
    assert ng == QB and n_sel <= QB and n_sel % 8 == 0
    wide = CMP_STRIDE * HEAD_DIM
    return pl.pallas_call(
        functools.partial(_nsa_cmp_kernel, n_cmp=n_cmp, n_sel=n_sel),
        grid=(batch, NSA_KV_HEADS, nqb),
        in_specs=[
            pl.BlockSpec((CMP_TQ, NSA_GQA * LANES), lambda b, h, i: (b * nqb + i, h)),
            pl.BlockSpec((1, 1, ng, wide), lambda b, h, i: (b, h, 0, 0)),
            pl.BlockSpec((1, 1, ng, wide), lambda b, h, i: (b, NSA_KV_HEADS + h, 0, 0)),
            pl.BlockSpec((2, 2, wide), lambda b, h, i: (0, 0, 0)),
            pl.BlockSpec((2, 2 * wide, HEAD_DIM), lambda b, h, i: (0, 0, 0)),
            *[pl.BlockSpec((None, CMP_TQ, QB), lambda b, h, i, g=g: (h * NSA_GQA + g, i, 0))
              for g in range(NSA_GQA)],
            pl.BlockSpec((QB, QB), lambda b, h, i: (0, 0)),
        ],
        out_specs=[
            pl.BlockSpec((CMP_TQ, 256), lambda b, h, i: (b * nqb + i, h)),
            pl.BlockSpec((1, 1, CMP_TQ, QB), lambda b, h, i: (b, h, i, 0)),
        ],
        out_shape=[
            jax.ShapeDtypeStruct((batch * seq, NSA_WIDTH), F32),
            jax.ShapeDtypeStruct((batch, NSA_KV_HEADS, seq, QB), BF16),
        ],
        scratch_shapes=[pltpu.VMEM((QB, LANES), BF16), pltpu.VMEM((2 * QB, LANES), BF16)],
        compiler_params=_params(("parallel", "parallel", "arbitrary")),
        name="nsa_compressed",
    )(pb, ucmp, ucmp, pos2, cmp_w, *[bias_c] * NSA_GQA, overlap_t)


SW_TQ = 2 * QB
SEL_CHUNK = 4 * QB
WIN_FAR = WINDOW - QB


def _nsa_sw_kernel(*refs, n_cast):
    n_in = 8
    _cast_slabs(refs[n_in:n_in + n_cast], refs[n_in + n_cast + 1:n_in + 2 * n_cast + 1])
    _nsa_sw_body(*refs[:n_in], refs[n_in + n_cast], *refs[n_in + 2 * n_cast + 1:])


def _nsa_sw_body(q_ref, ks_ref, kw_ref, ocmp_ref, misc_ref, sel_ref, exp_ref, tb_ref, o_ref, madd_ref):
    i = pl.program_id(2)
    G = NSA_GQA
    TQ = SW_TQ
    q4 = jnp.concatenate([q_ref[:, g * LANES:(g + 1) * LANES] for g in range(G)], axis=0)
    madd_ref[...] = (_dot(sel_ref[0, 0], exp_ref[...]) - 1.0) * (-NEG_INF)
    t_pos = i * TQ + lax.broadcasted_iota(jnp.int32, (TQ, 1), 0)

    low = lax.broadcasted_iota(jnp.int32, (QB, LANES), 1) < HEAD_DIM

    def tile(x, n=G):
        return jnp.concatenate([x] * n, axis=0)

    def ones_v(kv):
        return jnp.where(tile(low, kv.shape[0] // QB), jnp.ones_like(kv), kv)

    first = 2 * i - 1
    rows_m = pl.ds(pl.multiple_of(jnp.maximum(first, 0) * QB, QB), QB)
    rows_a = pl.ds(pl.multiple_of(2 * i * QB, QB), QB)
    rows_b = pl.ds(pl.multiple_of((2 * i + 1) * QB, QB), QB)
    gone = jnp.where(i >= 1, 0.0, NEG_INF)
    gone_m = jnp.concatenate([jnp.full((TQ, QB), gone, F32), jnp.zeros((TQ, 2 * QB), F32)], axis=1)
    far_end = first * QB

    kv = jnp.concatenate([ks_ref[rows_m, :], ks_ref[rows_a, :], ks_ref[rows_b, :]], axis=0)
    near_sel = jnp.concatenate([madd_ref[:, rows_m], madd_ref[:, rows_a], madd_ref[:, rows_b]], axis=1)
    s = _dot_nt(q4, kv) + (tb_ref[0] + tile(near_sel + gone_m))
    m = jnp.max(s, axis=-1, keepdims=True)
    acc = _dot(jnp.exp(s - m).astype(BF16), ones_v(kv))

    def sel_far(cidx, state):
        m, acc = state
        cols = pl.ds(pl.multiple_of(cidx * SEL_CHUNK, SEL_CHUNK), SEL_CHUNK)
        k_pos = cidx * SEL_CHUNK + lax.broadcasted_iota(jnp.int32, (TQ, SEL_CHUNK), 1)
        add = jnp.where(k_pos < far_end, madd_ref[:, cols], NEG_INF)
        kv = ks_ref[cols, :]
        s = _dot_nt(q4, kv) + tile(add)
        m_new = jnp.maximum(m, jnp.max(s, axis=-1, keepdims=True))
        return m_new, jnp.exp(m - m_new) * acc + _dot(jnp.exp(s - m_new).astype(BF16), ones_v(kv))

    _, sel_acc = lax.fori_loop(0, (2 * i + 2) // (SEL_CHUNK // QB), sel_far, (m, acc))

    start = jnp.maximum(first - WIN_FAR // QB, 0) * QB
    rows_far = pl.ds(pl.multiple_of(start, QB), WIN_FAR)
    k_pos = start + lax.broadcasted_iota(jnp.int32, (TQ, WIN_FAR), 1)
    live = (k_pos < far_end) & (k_pos > t_pos - WINDOW)
    kv = jnp.concatenate([kw_ref[rows_far, :], kw_ref[rows_m, :], kw_ref[rows_a, :], kw_ref[rows_b, :]], axis=0)
    bias = jnp.concatenate([tile(jnp.where(live, 0.0, NEG_INF)), tb_ref[0] + tile(gone_m)], axis=1)
    s = _dot_nt(q4, kv) + bias
    win_acc = _dot(jnp.exp(s - jnp.max(s, axis=-1, keepdims=True)).astype(BF16), ones_v(kv))

    gate = _sigmoid(misc_ref[...])
    low = tile(low, TQ // QB)

    def pair(acc, pp):
        a, b = acc[2 * pp * TQ:(2 * pp + 1) * TQ], acc[(2 * pp + 1) * TQ:(2 * pp + 2) * TQ]
        return (jnp.where(low, pltpu.roll(a, HEAD_DIM, 1), b)
                / jnp.where(low, a, pltpu.roll(b, HEAD_DIM, 1)))

    def pair_gate(branch, pp):
        ca, cb = 3 * (2 * pp) + branch, 3 * (2 * pp + 1) + branch
        return jnp.where(low, gate[:, ca:ca + 1], gate[:, cb:cb + 1])

    for pp in range(G // 2):
        lanes = slice(pp * LANES, (pp + 1) * LANES)
        o_ref[:, lanes] = (pair_gate(0, pp) * ocmp_ref[:, lanes] + pair_gate(1, pp) * pair(sel_acc, pp)
                           + pair_gate(2, pp) * pair(win_acc, pp))


def _nsa_sel_win(pb, pf, o_cmp, sel, expand, tb, batch, seq, cast=(), cast_layer=()):
    nqb = seq // SW_TQ
    G = NSA_GQA
    c_in, c_out, c_shapes = _cast_specs(cast, cast_layer, batch * NSA_KV_HEADS * nqb,
                                        lambda b, h, i: (b * NSA_KV_HEADS + h) * nqb + i)
    out = pl.pallas_call(
        functools.partial(_nsa_sw_kernel, n_cast=len(cast)),
        grid=(batch, NSA_KV_HEADS, nqb),
        in_specs=[
            pl.BlockSpec((SW_TQ, G * LANES), lambda b, h, i: (b * nqb + i, h)),
            pl.BlockSpec((seq, LANES), lambda b, h, i: (b, PB_SEL // LANES + h)),
            pl.BlockSpec((seq, LANES), lambda b, h, i: (b, PB_WIN // LANES + h)),
            pl.BlockSpec((SW_TQ, 256), lambda b, h, i: (b * nqb + i, h)),
            pl.BlockSpec((SW_TQ, LANES), lambda b, h, i: (b * nqb + i, PF_MISC // LANES + h)),
            pl.BlockSpec((1, 1, SW_TQ, QB), lambda b, h, i: (b, h, i, 0)),
            pl.BlockSpec((QB, seq), lambda b, h, i: (0, 0)),
            pl.BlockSpec((1, G * SW_TQ, 3 * QB), lambda b, h, i: (h, 0, 0)),
        ] + c_in,
        out_specs=[pl.BlockSpec((SW_TQ, 256), lambda b, h, i: (b * nqb + i, h))] + c_out,
        out_shape=[jax.ShapeDtypeStruct((batch * seq, NSA_WIDTH), F32)] + c_shapes,
        scratch_shapes=[pltpu.VMEM((SW_TQ, seq), F32)],
        compiler_params=_params(("parallel", "parallel", "arbitrary")),
        name="nsa_selected_window",
    )(pb, pb, pb, o_cmp, pf, sel, expand, tb, *cast)
    return out[0], out[1:]


def _ssd_kernel(*refs, n_cast):
    n_in = 12
    _cast_slabs(refs[n_in:n_in + n_cast], refs[n_in + n_cast + 1:n_in + 2 * n_cast + 1])
    _ssd_body(*refs[:n_in], refs[n_in + n_cast], *refs[n_in + 2 * n_cast + 1:])


def _ssd_body(z_ref, xs_ref, bc_ref, misc_ref, cwx_ref, cwb_ref, cbx_ref, cbb_ref, hp_ref, spread_ref,
              dskip_ref, ng_ref, o_ref, xbuf, bbuf, state):
    ci = pl.program_id(1)
    L = QB
    P = HEAD_DIM
    GN = SSD_GROUPS * SSD_STATE

    @pl.when(ci == 0)
    def _():
        xbuf[...] = jnp.zeros_like(xbuf)
        bbuf[...] = jnp.zeros_like(bbuf)
        state[...] = jnp.zeros_like(state)

    def conv_silu(buf, src_ref, w_ref, b_ref):
        cur = src_ref[...]
        ext = jnp.concatenate([buf[...], cur], axis=0)
        out = b_ref[...] + w_ref[SSD_CONV - 1:SSD_CONV, :] * cur
        for k in range(1, SSD_CONV):
            out = out + w_ref[SSD_CONV - 1 - k:SSD_CONV - k, :] * pltpu.roll(ext, k, 0)[8:8 + L]
        buf[...] = cur[L - 8:L]
        return _silu(out)

    xs = conv_silu(xbuf, xs_ref, cwx_ref, cbx_ref)
    bcs = conv_silu(bbuf, bc_ref, cwb_ref, cbb_ref)

    misc = misc_ref[...]
    pre = misc + hp_ref[0:1, :]
    dt = jnp.maximum(pre, 0.0) + jnp.log1p(jnp.exp(-jnp.abs(pre)))
    a_dt = dt * (-jnp.exp(hp_ref[1:2, :]))
    r = lax.broadcasted_iota(jnp.int32, (L, L), 0)
    c = lax.broadcasted_iota(jnp.int32, (L, L), 1)
    causal = r >= c
    a_cs = _split3_dot_left(causal.astype(BF16), a_dt)
    a_cs_t = a_cs.T

    dt_full = _split3_dot(dt, spread_ref[...])
    acs_full = _split3_dot(a_cs, spread_ref[...])
    total_full = acs_full[L - 1:L, :]
    xdt = xs * dt_full
    xdt16 = xdt.astype(BF16)
    to_end16 = (xdt * jnp.exp(total_full - acs_full)).astype(BF16)
    decay_in = jnp.exp(acs_full)
    chunk_decay = jnp.exp(total_full)
    low = c < P
    zero16 = jnp.zeros((L, 2 * P), BF16)
    width = SSD_HPG * P

    ys = []
    for g in range(SSD_GROUPS):
        bm = bcs[:, g * SSD_STATE:(g + 1) * SSD_STATE]
        cm16 = bcs[:, GN + g * SSD_STATE:GN + (g + 1) * SSD_STATE].astype(BF16)
        cb = _dot_nt(cm16, bm.astype(BF16))
        lanes = slice(g * width, (g + 1) * width)
        h_in = state[g]
        y_off = _dot(cm16, h_in.astype(BF16)) * decay_in[:, lanes]
        state[g] = h_in * chunk_decay[:, lanes] + _dot(bm.T.astype(BF16), to_end16[:, lanes])
        diag = []
        for pp in range(SSD_HPG // 2):
            hd = g * SSD_HPG + 2 * pp
            decayed = []
            for col in (MISC_DT + hd, MISC_DT + hd + 1):
                seg = jnp.exp(jnp.where(causal, a_cs[:, col:col + 1] - a_cs_t[col:col + 1, :], -jnp.inf))
                decayed.append((cb * seg).astype(BF16))
            xp = xdt16[:, hd * P:(hd + 2) * P]
            x_bd = jnp.concatenate([jnp.where(low, xp, zero16), jnp.where(low, zero16, xp)], axis=0)
            diag.append(_dot(jnp.concatenate(decayed, axis=1), x_bd))
        ys.append(jnp.concatenate(diag, axis=1) + y_off)

    y = (jnp.concatenate(ys, axis=1) + dskip_ref[...] * xs) * _silu(z_ref[...])
    for g in range(SSD_GROUPS):
        lanes = slice(g * width, (g + 1) * width)
        yg = y[:, lanes]
        o_ref[:, lanes] = yg * lax.rsqrt(jnp.mean(yg * yg, axis=-1, keepdims=True) + EPS) * ng_ref[:, lanes]


def _ssd(pf, conv_w, conv_b, head_params, d_skip, norm_g, batch, seq, cast=(), cast_layer=()):
    nc = seq // QB
    c_in, c_out, c_shapes = _cast_specs(cast, cast_layer, batch * nc, lambda b, ci: b * nc + ci)
    GN = SSD_GROUPS * SSD_STATE
    lane_head = jnp.arange(SSD_INNER)[None, :] // HEAD_DIM
    spread = (jnp.arange(LANES)[:, None] == MISC_DT + lane_head).astype(BF16)
    cwx, cwb = conv_w[:, :SSD_INNER], conv_w[:, SSD_INNER:]
    cbx, cbb = conv_b[:SSD_INNER].reshape(1, -1), conv_b[SSD_INNER:].reshape(1, -1)
    full = lambda shape: pl.BlockSpec(shape, lambda b, ci: (0,) * len(shape))
    out = pl.pallas_call(
        functools.partial(_ssd_kernel, n_cast=len(cast)),
        grid=(batch, nc),
        in_specs=[
            pl.BlockSpec((QB, SSD_INNER), lambda b, ci: (b * nc + ci, PF_Z // SSD_INNER)),
            pl.BlockSpec((QB, SSD_INNER), lambda b, ci: (b * nc + ci, PF_XS // SSD_INNER)),
            pl.BlockSpec((QB, 2 * GN), lambda b, ci: (b * nc + ci, PF_BC // (2 * GN))),
            pl.BlockSpec((QB, LANES), lambda b, ci: (b * nc + ci, PF_MISC // LANES)),
            full((SSD_CONV, SSD_INNER)),
            full((SSD_CONV, 2 * GN)),
            full((1, SSD_INNER)),
            full((1, 2 * GN)),
            full((8, LANES)),
            full((LANES, SSD_INNER)),
            full((1, SSD_INNER)),
            full((1, SSD_INNER)),
        ] + c_in,
        out_specs=[pl.BlockSpec((QB, SSD_INNER), lambda b, ci: (b * nc + ci, 0))] + c_out,
        out_shape=[jax.ShapeDtypeStruct((batch * seq, SSD_INNER), F32)] + c_shapes,
        scratch_shapes=[
            pltpu.VMEM((8, SSD_INNER), F32),
            pltpu.VMEM((8, 2 * GN), F32),
            pltpu.VMEM((SSD_GROUPS, SSD_STATE, SSD_HPG * HEAD_DIM), F32),
        ],
        compiler_params=_params(("parallel", "arbitrary")),
        name="ssd",
    )(pf, pf, pf, pf, cwx, cwb, cbx, cbb, head_params, spread,
      jnp.repeat(d_skip.astype(F32), HEAD_DIM).reshape(1, -1), norm_g.reshape(1, -1), *cast)
    return out[0], out[1:]


def _out_proj_kernel(x_ref, nsa_ref, sb_ref, ssd_ref, gn_ref, gs_ref, w_ref, o_ref, mix_ref):
    @pl.when(pl.program_id(1) == 0)
    def _():
        mix_ref[:, 0:NSA_WIDTH] = _rms(nsa_ref[...], gn_ref[...]).astype(BF16)
        mix_ref[:, NSA_WIDTH:NSA_WIDTH + SB_WIDTH] = _rms(sb_ref[...], gs_ref[...]).astype(BF16)
        mix_ref[:, NSA_WIDTH + SB_WIDTH:] = ssd_ref[...].astype(BF16)

    o_ref[...] = x_ref[...] + _dot(mix_ref[...], w_ref[...])


def _out_proj(x, o_nsa, o_sb, o_ssd, g_nsa, g_sb, w, *, tm=1024, tn=1024):
    n, d = x.shape
    dm = w.shape[0]
    return pl.pallas_call(
        _out_proj_kernel,
        grid=(n // tm, d // tn),
        in_specs=[
            pl.BlockSpec((tm, tn), lambda i, j: (i, j)),
            pl.BlockSpec((tm, NSA_WIDTH), lambda i, j: (i, 0)),
            pl.BlockSpec((tm, SB_WIDTH), lambda i, j: (i, 0)),
            pl.BlockSpec((tm, SSD_INNER), lambda i, j: (i, 0)),
            pl.BlockSpec((1, NSA_WIDTH), lambda i, j: (0, 0)),
            pl.BlockSpec((1, SB_WIDTH), lambda i, j: (0, 0)),
            pl.BlockSpec((dm, tn), lambda i, j: (0, j)),
        ],
        out_specs=pl.BlockSpec((tm, tn), lambda i, j: (i, j)),
        out_shape=jax.ShapeDtypeStruct((n, d), F32),
        scratch_shapes=[pltpu.VMEM((tm, dm), BF16)],
        compiler_params=_params(("parallel", "arbitrary")),
        name="out_proj",
    )(x, o_nsa, o_sb, o_ssd, g_nsa.reshape(1, -1), g_sb.reshape(1, -1), w)


def _rel_bucket(dist):
    dist = jnp.maximum(dist, 0)
    max_exact = REL_BUCKETS // 2
    log_ratio = jnp.log(jnp.maximum(dist, 1).astype(F32) / max_exact) / math.log(REL_MAX_DIST / max_exact)
    large = jnp.minimum(max_exact + (log_ratio * (REL_BUCKETS - max_exact)).astype(jnp.int32), REL_BUCKETS - 1)
    return jnp.where(dist < max_exact, dist, large)


def _bias_tables(rel_bias, seq):
    assert QB >= REL_MAX_DIST
    def lookup(dist):
        buckets = jnp.arange(REL_BUCKETS).reshape((-1,) + (1,) * dist.ndim)
        onehot = (_rel_bucket(dist)[None] == buckets).astype(F32)
        return jnp.einsum('kh,k...->h...', rel_bias, onehot, precision=lax.Precision.HIGHEST)

    t = jnp.arange(seq)[:, None]
    cend = jnp.arange(QB)[None, :] * CMP_STRIDE + CMP_BLOCK - 1
    n_cmp = (seq - CMP_BLOCK) // CMP_STRIDE + 1
    valid_c = (t >= cend) & (jnp.arange(QB)[None, :] < n_cmp)
    bias_c = jnp.where(valid_c, lookup(t - cend), NEG_INF)
    r = jnp.arange(QB)[:, None]
    m = jnp.arange(QB)[None, :]
    near = jnp.stack([lookup(r - m), lookup(QB + r - m)])
    near = near - rel_bias[REL_BUCKETS - 1][None, :, None, None]
    diag = near[0] + jnp.where(m > r, NEG_INF, 0.0)
    prev = near[1]
    rows_a = jnp.concatenate([prev, diag, jnp.full_like(diag, NEG_INF)], axis=2)
    rows_b = jnp.concatenate([jnp.zeros_like(diag), prev, diag], axis=2)
    near2 = jnp.concatenate([rows_a, rows_b], axis=1)
    return bias_c, near2.reshape(NSA_KV_HEADS, NSA_GQA * SW_TQ, 3 * QB)


def _expand_table(seq):
    j = jnp.arange(QB)[:, None]
    s = jnp.arange(seq)[None, :]
    return (s // SEL_BLOCK == j).astype(BF16)


def _overlap_table(seq):
    n_cmp = (seq - CMP_BLOCK) // CMP_STRIDE + 1
    cs = jnp.arange(QB)[:, None] * CMP_STRIDE
    ce = cs + CMP_BLOCK - 1
    ss = jnp.arange(QB)[None, :] * SEL_BLOCK
    ov = jnp.maximum(jnp.minimum(ce, ss + SEL_BLOCK - 1) - jnp.maximum(cs, ss) + 1, 0).astype(F32) / CMP_BLOCK
    keep = (jnp.arange(QB)[:, None] < n_cmp) & (jnp.arange(QB)[None, :] < seq // SEL_BLOCK)
    return jnp.where(keep, ov, 0.0).T.astype(BF16)


def _in_proj_weights(w):
    w = w.astype(BF16)
    scale = HEAD_DIM ** -0.5
    o_q, o_kv, o_gate = 0, NSA_WIDTH, NSA_WIDTH + 768
    o_sb = o_gate + 3 * NSA_HEADS
    o_z = o_sb + 3 * SB_WIDTH
    o_xbc = o_z + SSD_INNER
    o_dt = o_xbc + SSD_INNER + 2 * SSD_GROUPS * SSD_STATE
    d = w.shape[0]
    col = lambda a, n: w[:, a:a + n]
    zeros = lambda n: jnp.zeros((d, n), w.dtype)
    q_pad = jnp.pad(col(o_q, NSA_WIDTH).reshape(d, NSA_HEADS, HEAD_DIM) * scale,
                    ((0, 0), (0, 0), (0, LANES - HEAD_DIM))).reshape(d, NSA_HEADS * LANES)
    kv = col(o_kv + 256, 512).reshape(d, 2, 2, NSA_KV_HEADS, HEAD_DIM).transpose(0, 1, 3, 2, 4).reshape(d, 512)
    part_bf16 = [q_pad, kv, col(o_sb, SB_WIDTH) * scale, col(o_sb + SB_WIDTH, 2 * SB_WIDTH)]
    gates = 3 * NSA_GQA
    part_f32 = [col(o_z, 2 * SSD_INNER + 2 * SSD_GROUPS * SSD_STATE), col(o_kv, 256),
                col(o_gate, gates), zeros(MISC_DT - gates), col(o_dt, SSD_HEADS),
                zeros(LANES - MISC_DT - SSD_HEADS), col(o_gate + gates, gates), zeros(LANES - gates)]
    out = jnp.concatenate(part_bf16 + part_f32, axis=1)
    assert out.shape[1] == PB_COLS + PF_COLS and sum(p.shape[1] for p in part_bf16) == PB_COLS
    return out


def _head_params(dt_bias, a_log):
    rows = jnp.stack([dt_bias, a_log]).astype(F32)
    return jnp.zeros((8, LANES), F32).at[0:2, MISC_DT:MISC_DT + SSD_HEADS].set(rows)


def _mixer(x, l, tables, batch, seq, mix_norm, w_in, w_out, cmp_pos, cmp_w, nsa_norm, sb_norm,
           conv_w, conv_b, dt_bias, a_log, d_skip, ssd_norm, casts):
    hosts = [([w for w, _ in casts[k::3]], tuple(at for _, at in casts[k::3])) for k in range(3)]
    bias_c, near, overlap, expand = tables
    pb, pf = _in_proj(x, mix_norm[l], _in_proj_weights(w_in[l]), PB_COLS)

    ng = seq // CMP_STRIDE
    ucmp = pf[:, PF_CMP:PF_CMP + 256].reshape(batch, ng, CMP_STRIDE, 4, HEAD_DIM)
    ucmp = ucmp.transpose(0, 3, 1, 2, 4).reshape(batch, 4, ng, CMP_STRIDE * HEAD_DIM)
    pos2 = cmp_pos[l].reshape(2, 2, CMP_STRIDE * HEAD_DIM)
    o_cmp, sel = _nsa_cmp(pb, ucmp, pos2, cmp_w[l].astype(BF16), bias_c, overlap, batch, seq)
    o_nsa, w0 = _nsa_sel_win(pb, pf, o_cmp, sel, expand, near, batch, seq, *hosts[0])
    o_sb, w1 = _sb_attention(pb, batch, seq, *hosts[1])
    o_ssd, w2 = _ssd(pf, conv_w[l], conv_b[l], _head_params(dt_bias[l], a_log[l]), d_skip[l], ssd_norm[l],
                     batch, seq, *hosts[2])
    out = _out_proj(x, o_nsa, o_sb, o_ssd, nsa_norm[l], sb_norm[l], w_out[l].astype(BF16))
    done = [None] * len(casts)
    for k, ws in enumerate((w0, w1, w2)):
        done[k::3] = ws
    return out, done


def kernel(x, rel_bias, ffn1_norm, ffn1_w_gate, ffn1_w_up, ffn1_w_down, mix_norm, w_in, w_out, nsa_cmp_pos, nsa_cmp_w, nsa_out_norm, sb_out_norm, ssd_conv_w, ssd_conv_b, ssd_dt_bias, ssd_a_log, ssd_d, ssd_out_norm, ffn2_norm, ffn2_w_gate, ffn2_w_up, ffn2_w_down, final_norm):
    batch, seq, d = x.shape
    depth = w_in.shape[0]
    tables = _bias_tables(rel_bias, seq) + (_overlap_table(seq), _expand_table(seq))
    h = x.reshape(batch * seq, d)

    ffn1 = (ffn1_w_gate, ffn1_w_up, ffn1_w_down)
    ffn2 = (ffn2_w_gate, ffn2_w_up, ffn2_w_down)
    widen = lambda ws: tuple(w[None] for w in ws)
    w1 = widen(_cast_layer(ffn1, 0))
    for l in range(depth):
        h = _ffn(h, ffn1_norm[l], *w1, 0, final_norm, final=False)
        casts = [(w, l) for w in ffn2] + ([(w, l + 1) for w in ffn1] if l + 1 < depth else [])
        h, done = _mixer(h, l, tables, batch, seq, mix_norm, w_in, w_out, nsa_cmp_pos, nsa_cmp_w,
                         nsa_out_norm, sb_out_norm, ssd_conv_w, ssd_conv_b, ssd_dt_bias, ssd_a_log,
                         ssd_d, ssd_out_norm, casts)
        h = _ffn(h, ffn2_norm[l], *widen(done[0:3]), 0, final_norm, final=(l == depth - 1))
        w1 = widen(done[3:6])
    return h.reshape(batch, seq, d)
```

```python
import functools
import math

import jax
import jax.numpy as jnp
from jax import lax
from jax.experimental import pallas as pl
from jax.experimental.pallas import tpu as pltpu

LANES = 128
HEAD_DIM = 64
QB = 128
NEG_INF = -1e30
EPS = 1e-6

NSA_HEADS = 8
NSA_KV_HEADS = 2
NSA_GQA = NSA_HEADS // NSA_KV_HEADS
NSA_WIDTH = NSA_HEADS * HEAD_DIM
CMP_BLOCK = 32
CMP_STRIDE = 16
SEL_BLOCK = 64
SEL_TOPK = 8
FORCE_SCORE = 1e3
WINDOW = 512
SB_HEADS = 8
SB_WIDTH = SB_HEADS * HEAD_DIM
SSD_HEADS = 16
SSD_INNER = SSD_HEADS * HEAD_DIM
SSD_GROUPS = 2
SSD_HPG = SSD_HEADS // SSD_GROUPS
SSD_STATE = 128
SSD_CONV = 4
REL_BUCKETS = 32
REL_MAX_DIST = 128

PB_NSA_Q, PB_SEL, PB_WIN, PB_SB_Q, PB_SB_K, PB_SB_V, PB_COLS = 0, 1024, 1280, 1536, 2048, 2560, 3072
PF_Z, PF_XS, PF_BC, PF_CMP, PF_MISC, PF_COLS = 0, 1024, 2048, 2560, 2816, 3072
MISC_DT = 16

VMEM_LIMIT = 52 * 1024 * 1024

BF16 = jnp.bfloat16
F32 = jnp.float32


def _dot(a, b):
    return jnp.dot(a, b, preferred_element_type=F32)


def _dot_nt(a, b):
    return lax.dot_general(a, b, (((1,), (1,)), ((), ())), preferred_element_type=F32)


def _split3_dot(x, m):
    hi = x.astype(BF16)
    r1 = x - hi.astype(F32)
    mid = r1.astype(BF16)
    lo = (r1 - mid.astype(F32)).astype(BF16)
    return _dot(hi, m) + _dot(mid, m) + _dot(lo, m)


def _split3_dot_left(m, x):
    hi = x.astype(BF16)
    r1 = x - hi.astype(F32)
    mid = r1.astype(BF16)
    lo = (r1 - mid.astype(F32)).astype(BF16)
    return _dot(m, hi) + _dot(m, mid) + _dot(m, lo)


def _rms(x, g):
    return x * lax.rsqrt(jnp.mean(x * x, axis=-1, keepdims=True) + EPS) * g


def _silu(x):
    return x / (1.0 + jnp.exp(-x))


def _sigmoid(x):
    return 1.0 / (1.0 + jnp.exp(-x))


def _params(sem):
    return pltpu.CompilerParams(dimension_semantics=sem, vmem_limit_bytes=VMEM_LIMIT)


def _cast_specs(weights, layer, steps, step_of):
    in_specs, out_specs, out_shapes = [], [], []
    for w, at in zip(weights, layer):
        _, rows, cols = w.shape
        share = 1
        while (rows * share) % (16 * steps):
            share *= 2
        slab = rows * share // steps
        in_specs.append(pl.BlockSpec((None, slab, cols),
                                     lambda *ids, k=share, at=at: (at, step_of(*ids) // k, 0)))
        out_specs.append(pl.BlockSpec((slab, cols), lambda *ids, k=share: (step_of(*ids) // k, 0)))
        out_shapes.append(jax.ShapeDtypeStruct((rows, cols), BF16))
    return in_specs, out_specs, out_shapes


def _cast_slabs(srcs, dsts):
    for src, dst in zip(srcs, dsts):
        dst[...] = src[...].astype(BF16)


def _cast_kernel(*refs):
    _cast_slabs(refs[:len(refs) // 2], refs[len(refs) // 2:])


def _cast_layer(weights, layer, steps=16):
    c_in, c_out, c_shapes = _cast_specs(weights, (layer,) * len(weights), steps, lambda s: s)
    return pl.pallas_call(_cast_kernel, grid=(steps,), in_specs=c_in, out_specs=c_out, out_shape=c_shapes,
                          compiler_params=_params(("parallel",)), name="cast_weights")(*weights)


def _ffn_kernel(x_ref, g_ref, wg_ref, wu_ref, wd_ref, fg_ref, o_ref, h_ref, acc_ref, *, final):
    f = pl.program_id(1)

    @pl.when(f == 0)
    def _():
        h_ref[...] = _rms(x_ref[...], g_ref[...]).astype(BF16)
        acc_ref[...] = jnp.zeros_like(acc_ref)

    h = h_ref[...]
    gate = _dot(h, wg_ref[...])
    up = _dot(h, wu_ref[...])
    acc_ref[...] += _dot((_silu(gate) * up).astype(BF16), wd_ref[...])

    @pl.when(f == pl.num_programs(1) - 1)
    def _():
        y = x_ref[...] + 0.5 * acc_ref[...]
        if final:
            y = _rms(y, fg_ref[...])
        o_ref[...] = y


def _ffn(x, g, wg, wu, wd, layer, final_g, *, final, tm=512, tf=512):
    n, d = x.shape
    dff = wg.shape[2]
    return pl.pallas_call(
        functools.partial(_ffn_kernel, final=final),
        grid=(n // tm, dff // tf),
        in_specs=[
            pl.BlockSpec((tm, d), lambda i, f: (i, 0)),
            pl.BlockSpec((1, d), lambda i, f: (0, 0)),
            pl.BlockSpec((None, d, tf), lambda i, f: (layer, 0, f)),
            pl.BlockSpec((None, d, tf), lambda i, f: (layer, 0, f)),
            pl.BlockSpec((None, tf, d), lambda i, f: (layer, f, 0)),
            pl.BlockSpec((1, d), lambda i, f: (0, 0)),
        ],
        out_specs=pl.BlockSpec((tm, d), lambda i, f: (i, 0)),
        out_shape=jax.ShapeDtypeStruct((n, d), F32),
        scratch_shapes=[pltpu.VMEM((tm, d), BF16), pltpu.VMEM((tm, d), F32)],
        compiler_params=_params(("parallel", "arbitrary")),
        name="ffn",
    )(x, g.reshape(1, d), wg, wu, wd, final_g.reshape(1, d))


def _in_proj_kernel(x_ref, g_ref, w_ref, ob_ref, of_ref, h_ref, *, nb):
    j = pl.program_id(1)

    @pl.when(j == 0)
    def _():
        h_ref[...] = _rms(x_ref[...], g_ref[...]).astype(BF16)

    @pl.when(j < nb)
    def _():
        ob_ref[...] = _dot(h_ref[...], w_ref[...]).astype(BF16)

    @pl.when(j >= nb)
    def _():
        of_ref[...] = _dot(h_ref[...], w_ref[...])


def _in_proj(x, g, w, cols_bf16, *, tm=1024, tn=1024):
    n, d = x.shape
    c = w.shape[1]
    nb = cols_bf16 // tn
    return pl.pallas_call(
        functools.partial(_in_proj_kernel, nb=nb),
        grid=(n // tm, c // tn),
        in_specs=[
            pl.BlockSpec((tm, d), lambda i, j: (i, 0)),
            pl.BlockSpec((1, d), lambda i, j: (0, 0)),
            pl.BlockSpec((d, tn), lambda i, j: (0, j)),
        ],
        out_specs=[pl.BlockSpec((tm, tn), lambda i, j: (i, jnp.minimum(j, nb - 1))),
                   pl.BlockSpec((tm, tn), lambda i, j: (i, jnp.maximum(j - nb, 0)))],
        out_shape=[jax.ShapeDtypeStruct((n, cols_bf16), BF16),
                   jax.ShapeDtypeStruct((n, c - cols_bf16), F32)],
        scratch_shapes=[pltpu.VMEM((tm, d), BF16)],
        compiler_params=_params(("parallel", "arbitrary")),
        name="in_proj",
    )(x, g.reshape(1, d), w)


SB_GROUP = 8
SB_DEAD = 104.0


def _sb_kernel(*refs, n_cast):
    q_ref, k_ref, v_ref = refs[:3]
    o_ref = refs[3 + n_cast]
    kbd_ref, vbd_ref = refs[4 + 2 * n_cast:]
    _cast_slabs(refs[3:3 + n_cast], refs[4 + n_cast:4 + 2 * n_cast])
    _sb_body(q_ref, k_ref, v_ref, o_ref, kbd_ref, vbd_ref)


def _sb_body(q_ref, k_ref, v_ref, o_ref, kbd_ref, vbd_ref):
    i = pl.program_id(2)
    pairs = SB_GROUP // 2
    nkb = k_ref.shape[0] // QB
    r = lax.broadcasted_iota(jnp.int32, (QB, QB), 0)
    c = lax.broadcasted_iota(jnp.int32, (QB, QB), 1)
    strict_all = jnp.concatenate([c < r] * SB_GROUP, axis=0)
    later_ones = jnp.concatenate([(r > c).astype(BF16), jnp.ones((QB, QB), BF16)], axis=1)
    later_ones = jnp.concatenate([later_ones, later_ones], axis=0)

    @pl.when(i == 0)
    def _():
        low = c < HEAD_DIM
        zero16 = jnp.zeros((QB, QB), BF16)

        def build(j, carry):
            src = pl.ds(pl.multiple_of(j * QB, QB), QB)
            dst = pl.ds(pl.multiple_of(j * 2 * QB, 2 * QB), 2 * QB)
            for pp in range(pairs):
                for ref, out in ((k_ref, kbd_ref), (v_ref, vbd_ref)):
                    x = ref[src, pp * LANES:(pp + 1) * LANES]
                    out[pp, dst, :] = jnp.concatenate([jnp.where(low, x, zero16), jnp.where(low, zero16, x)],
                                                      axis=0)
            return carry

        lax.fori_loop(0, nkb, build, 0)

    def scores(j, diagonal):
        rows = pl.ds(pl.multiple_of(j * 2 * QB, 2 * QB), 2 * QB)
        zs = []
        for pp in range(pairs):
            z_pair = _dot_nt(q_ref[:, pp * LANES:(pp + 1) * LANES], kbd_ref[pp, rows, :])
            zs += [z_pair[:, 0:QB], z_pair[:, QB:2 * QB]]
        z = jnp.concatenate(zs, axis=0)
        soft = jnp.maximum(z, 0.0) + jnp.log(1.0 + jnp.exp(-jnp.abs(z)))
        if diagonal:
            soft = jnp.where(strict_all, soft, 0.0)
        hi = soft.astype(BF16)
        lo = (soft - hi.astype(F32)).astype(BF16)
        sums = _dot(jnp.concatenate([hi, lo], axis=1), later_ones)
        return rows, z - soft, sums

    def absorb(tails, accs, rows, log_beta, sums, diagonal):
        a = jnp.exp(log_beta - (tails + sums[:, 0:QB]))
        if diagonal:
            a = jnp.where(strict_all, a, 0.0)
        a = a.astype(BF16)
        outs = []
        for pp in range(pairs):
            pair = jnp.concatenate([a[2 * pp * QB:(2 * pp + 1) * QB], a[(2 * pp + 1) * QB:(2 * pp + 2) * QB]],
                                   axis=1)
            outs.append(_dot(pair, vbd_ref[pp, rows, :]))
        return tails + sums[:, QB:2 * QB], accs + jnp.concatenate(outs, axis=0)

    def sweep(js, carry):
        staged = [scores(j, False) for j in js]
        for st in staged:
            carry = absorb(*carry, *st, False)
        return carry

    carry = (jnp.zeros((SB_GROUP * QB, QB), F32), jnp.zeros((pairs * QB, LANES), F32))
    carry = absorb(*carry, *scores(i, True), True)
    odd = i % 2
    carry = lax.fori_loop(0, odd, lambda step, cr: sweep([i - 1], cr), carry)
    top = i - 1 - odd

    def live(state):
        step, smallest, _, _ = state
        return (step < i // 2) & (smallest < SB_DEAD)

    def pair_step(state):
        step, _, tails, accs = state
        tails, accs = sweep([top - 2 * step, top - 2 * step - 1], (tails, accs))
        return step + 1, jnp.min(tails), tails, accs

    accs = lax.while_loop(live, pair_step, (0, jnp.min(carry[0]), *carry))[3]
    for pp in range(pairs):
        o_ref[:, pp * LANES:(pp + 1) * LANES] = accs[pp * QB:(pp + 1) * QB]


def _sb_attention(pb, batch, seq, cast=(), cast_layer=()):
    nqb = seq // QB
    w = SB_GROUP * HEAD_DIM
    assert SB_HEADS == SB_GROUP
    qc, kc, vc = PB_SB_Q // w, PB_SB_K // w, PB_SB_V // w
    c_in, c_out, c_shapes = _cast_specs(cast, cast_layer, batch * nqb, lambda b, p, i: b * nqb + i)
    out = pl.pallas_call(
        functools.partial(_sb_kernel, n_cast=len(cast)),
        grid=(batch, SB_HEADS // SB_GROUP, nqb),
        in_specs=[
            pl.BlockSpec((QB, w), lambda b, p, i: (b * nqb + i, qc + p)),
            pl.BlockSpec((seq, w), lambda b, p, i: (b, kc + p)),
            pl.BlockSpec((seq, w), lambda b, p, i: (b, vc + p)),
        ] + c_in,
        out_specs=[pl.BlockSpec((QB, w), lambda b, p, i: (b * nqb + i, p))] + c_out,
        out_shape=[jax.ShapeDtypeStruct((batch * seq, SB_WIDTH), F32)] + c_shapes,
        scratch_shapes=[pltpu.VMEM((SB_GROUP // 2, 2 * seq, LANES), BF16),
                        pltpu.VMEM((SB_GROUP // 2, 2 * seq, LANES), BF16)],
        compiler_params=_params(("parallel", "parallel", "arbitrary")),
        name="sb_attention",
    )(pb, pb, pb, *cast)
    return out[0], out[1:]


CMP_TQ = 4 * QB


def _nsa_cmp_kernel(q_ref, uk_ref, uv_ref, pos_ref, w_ref, b0_ref, b1_ref, b2_ref, b3_ref, ovt_ref,
                    ocmp_ref, sel_ref, kc_ref, vc_ref, *, n_cmp, n_sel):
    i = pl.program_id(2)
    half = CMP_STRIDE * HEAD_DIM

    @pl.when(i == 0)
    def _():
        def compress(kv, u_ref):
            u = u_ref[0, 0]
            top = _dot((u + pos_ref[kv, 0:1, :]).astype(BF16), w_ref[kv, 0:half, :])
            bot = _dot((u + pos_ref[kv, 1:2, :]).astype(BF16), w_ref[kv, half:2 * half, :])
            return top + pltpu.roll(bot, QB - 1, 0)

        zeros = jnp.zeros((QB, HEAD_DIM), F32)
        kc_ref[...] = jnp.concatenate([compress(0, uk_ref), zeros], axis=1).astype(BF16)
        v = compress(1, uv_ref)
        vc_ref[...] = jnp.concatenate([jnp.concatenate([v, zeros], axis=1),
                                       jnp.concatenate([zeros, v], axis=1)], axis=0).astype(BF16)

    G = NSA_GQA
    TQ = CMP_TQ
    q4 = jnp.concatenate([q_ref[:, g * LANES:(g + 1) * LANES] for g in range(G)], axis=0)
    bias = jnp.concatenate([b[...] for b in (b0_ref, b1_ref, b2_ref, b3_ref)], axis=0)
    s = _dot_nt(q4, kc_ref[...]) + bias
    e = jnp.exp(s - jnp.max(s, axis=-1, keepdims=True))
    t_rows = i * TQ + lax.broadcasted_iota(jnp.int32, (TQ, 1), 0)
    any_valid = jnp.concatenate([t_rows >= CMP_BLOCK - 1] * G, axis=0)
    p = jnp.where(any_valid, e / jnp.sum(e, axis=-1, keepdims=True), 0.0)
    p16 = p.astype(BF16)
    for pp in range(G // 2):
        pair = jnp.concatenate([p16[2 * pp * TQ:(2 * pp + 1) * TQ], p16[(2 * pp + 1) * TQ:(2 * pp + 2) * TQ]],
                               axis=1)
        ocmp_ref[:, pp * LANES:(pp + 1) * LANES] = _dot(pair, vc_ref[...])
    p_all = p[0:TQ]
    for g in range(1, G):
        p_all = p_all + p[g * TQ:(g + 1) * TQ]

    blk = lax.broadcasted_iota(jnp.int32, (n_sel, QB), 0)
    for part in range(TQ // QB):
        rs = slice(part * QB, (part + 1) * QB)
        t0 = (i * (TQ // QB) + part) * QB
        p_sum = p_all[rs]
        hi = p_sum.astype(BF16)
        lo = (p_sum - hi.astype(F32)).astype(BF16)
        p_sel = (_dot_nt(ovt_ref[...], hi) + _dot_nt(ovt_ref[...], lo))[0:n_sel]
        t = t0 + lax.broadcasted_iota(jnp.int32, (n_sel, QB), 1)
        cur = t // SEL_BLOCK
        eligible = blk * SEL_BLOCK <= t
        forced = (blk == 0) | (blk == cur) | (blk == cur - 1)
        score = jnp.where(eligible, p_sel + jnp.where(forced, FORCE_SCORE, 0.0), NEG_INF)
        rank = jnp.zeros((n_sel, QB), F32)
        for j in range(n_sel):
            other = score[j:j + 1, :]
            ahead = (other > score) | ((other == score) & (blk > j))
            rank = rank + jnp.where(ahead, 1.0, 0.0)
        chosen = jnp.where(eligible & (rank < SEL_TOPK), 1.0, 0.0)
        chosen = jnp.concatenate([chosen, jnp.zeros((QB - n_sel, QB), F32)], axis=0)
        sel_ref[0, 0, rs, :] = chosen.T.astype(BF16)


def _nsa_cmp(pb, ucmp, pos2, cmp_w, bias_c, overlap_t, batch, seq):
    nqb = seq // CMP_TQ
    n_cmp = (seq - CMP_BLOCK) // CMP_STRIDE + 1
    n_sel = seq // SEL_BLOCK
    ng = seq // CMP_STRIDE
    assert ng == QB and n_sel <= QB and n_sel % 8 == 0
    wide = CMP_STRIDE * HEAD_DIM
    return pl.pallas_call(
        functools.partial(_nsa_cmp_kernel, n_cmp=n_cmp, n_sel=n_sel),
        grid=(batch, NSA_KV_HEADS, nqb),
        in_specs=[
            pl.BlockSpec((CMP_TQ, NSA_GQA * LANES), lambda b, h, i: (b * nqb + i, h)),
            pl.BlockSpec((1, 1, ng, wide), lambda b, h, i: (b, h, 0, 0)),
            pl.BlockSpec((1, 1, ng, wide), lambda b, h, i: (b, NSA_KV_HEADS + h, 0, 0)),
            pl.BlockSpec((2, 2, wide), lambda b, h, i: (0, 0, 0)),
            pl.BlockSpec((2, 2 * wide, HEAD_DIM), lambda b, h, i: (0, 0, 0)),
            *[pl.BlockSpec((None, CMP_TQ, QB), lambda b, h, i, g=g: (h * NSA_GQA + g, i, 0))
              for g in range(NSA_GQA)],
            pl.BlockSpec((QB, QB), lambda b, h, i: (0, 0)),
        ],
        out_specs=[
            pl.BlockSpec((CMP_TQ, 256), lambda b, h, i: (b * nqb + i, h)),
            pl.BlockSpec((1, 1, CMP_TQ, QB), lambda b, h, i: (b, h, i, 0)),
        ],
        out_shape=[
            jax.ShapeDtypeStruct((batch * seq, NSA_WIDTH), F32),
            jax.ShapeDtypeStruct((batch, NSA_KV_HEADS, seq, QB), BF16),
        ],
        scratch_shapes=[pltpu.VMEM((QB, LANES), BF16), pltpu.VMEM((2 * QB, LANES), BF16)],
        compiler_params=_params(("parallel", "parallel", "arbitrary")),
        name="nsa_compressed",
    )(pb, ucmp, ucmp, pos2, cmp_w, *[bias_c] * NSA_GQA, overlap_t)


SW_TQ = 2 * QB
SEL_CHUNK = 4 * QB
WIN_FAR = WINDOW - QB


def _nsa_sw_kernel(*refs, n_cast):
    n_in = 8
    _cast_slabs(refs[n_in:n_in + n_cast], refs[n_in + n_cast + 1:n_in + 2 * n_cast + 1])
    _nsa_sw_body(*refs[:n_in], refs[n_in + n_cast], *refs[n_in + 2 * n_cast + 1:])


def _nsa_sw_body(q_ref, ks_ref, kw_ref, ocmp_ref, misc_ref, sel_ref, exp_ref, tb_ref, o_ref, madd_ref):
    i = pl.program_id(2)
    G = NSA_GQA
    TQ = SW_TQ
    q4 = jnp.concatenate([q_ref[:, g * LANES:(g + 1) * LANES] for g in range(G)], axis=0)
    madd_ref[...] = (_dot(sel_ref[0, 0], exp_ref[...]) - 1.0) * (-NEG_INF)
    t_pos = i * TQ + lax.broadcasted_iota(jnp.int32, (TQ, 1), 0)

    low = lax.broadcasted_iota(jnp.int32, (QB, LANES), 1) < HEAD_DIM

    def tile(x, n=G):
        return jnp.concatenate([x] * n, axis=0)

    def ones_v(kv):
        return jnp.where(tile(low, kv.shape[0] // QB), jnp.ones_like(kv), kv)

    first = 2 * i - 1
    rows_m = pl.ds(pl.multiple_of(jnp.maximum(first, 0) * QB, QB), QB)
    rows_a = pl.ds(pl.multiple_of(2 * i * QB, QB), QB)
    rows_b = pl.ds(pl.multiple_of((2 * i + 1) * QB, QB), QB)
    gone = jnp.where(i >= 1, 0.0, NEG_INF)
    gone_m = jnp.concatenate([jnp.full((TQ, QB), gone, F32), jnp.zeros((TQ, 2 * QB), F32)], axis=1)
    far_end = first * QB

    kv = jnp.concatenate([ks_ref[rows_m, :], ks_ref[rows_a, :], ks_ref[rows_b, :]], axis=0)
    near_sel = jnp.concatenate([madd_ref[:, rows_m], madd_ref[:, rows_a], madd_ref[:, rows_b]], axis=1)
    s = _dot_nt(q4, kv) + (tb_ref[0] + tile(near_sel + gone_m))
    m = jnp.max(s, axis=-1, keepdims=True)
    acc = _dot(jnp.exp(s - m).astype(BF16), ones_v(kv))

    def sel_far(cidx, state):
        m, acc = state
        cols = pl.ds(pl.multiple_of(cidx * SEL_CHUNK, SEL_CHUNK), SEL_CHUNK)
        k_pos = cidx * SEL_CHUNK + lax.broadcasted_iota(jnp.int32, (TQ, SEL_CHUNK), 1)
        add = jnp.where(k_pos < far_end, madd_ref[:, cols], NEG_INF)
        kv = ks_ref[cols, :]
        s = _dot_nt(q4, kv) + tile(add)
        m_new = jnp.maximum(m, jnp.max(s, axis=-1, keepdims=True))
        return m_new, jnp.exp(m - m_new) * acc + _dot(jnp.exp(s - m_new).astype(BF16), ones_v(kv))

    _, sel_acc = lax.fori_loop(0, (2 * i + 2) // (SEL_CHUNK // QB), sel_far, (m, acc))

    start = jnp.maximum(first - WIN_FAR // QB, 0) * QB
    rows_far = pl.ds(pl.multiple_of(start, QB), WIN_FAR)
    k_pos = start + lax.broadcasted_iota(jnp.int32, (TQ, WIN_FAR), 1)
    live = (k_pos < far_end) & (k_pos > t_pos - WINDOW)
    kv = jnp.concatenate([kw_ref[rows_far, :], kw_ref[rows_m, :], kw_ref[rows_a, :], kw_ref[rows_b, :]], axis=0)
    bias = jnp.concatenate([tile(jnp.where(live, 0.0, NEG_INF)), tb_ref[0] + tile(gone_m)], axis=1)
    s = _dot_nt(q4, kv) + bias
    win_acc = _dot(jnp.exp(s - jnp.max(s, axis=-1, keepdims=True)).astype(BF16), ones_v(kv))

    gate = _sigmoid(misc_ref[...])
    low = tile(low, TQ // QB)

    def pair(acc, pp):
        a, b = acc[2 * pp * TQ:(2 * pp + 1) * TQ], acc[(2 * pp + 1) * TQ:(2 * pp + 2) * TQ]
        return (jnp.where(low, pltpu.roll(a, HEAD_DIM, 1), b)
                / jnp.where(low, a, pltpu.roll(b, HEAD_DIM, 1)))

    def pair_gate(branch, pp):
        ca, cb = 3 * (2 * pp) + branch, 3 * (2 * pp + 1) + branch
        return jnp.where(low, gate[:, ca:ca + 1], gate[:, cb:cb + 1])

    for pp in range(G // 2):
        lanes = slice(pp * LANES, (pp + 1) * LANES)
        o_ref[:, lanes] = (pair_gate(0, pp) * ocmp_ref[:, lanes] + pair_gate(1, pp) * pair(sel_acc, pp)
                           + pair_gate(2, pp) * pair(win_acc, pp))


def _nsa_sel_win(pb, pf, o_cmp, sel, expand, tb, batch, seq, cast=(), cast_layer=()):
    nqb = seq // SW_TQ
    G = NSA_GQA
    c_in, c_out, c_shapes = _cast_specs(cast, cast_layer, batch * NSA_KV_HEADS * nqb,
                                        lambda b, h, i: (b * NSA_KV_HEADS + h) * nqb + i)
    out = pl.pallas_call(
        functools.partial(_nsa_sw_kernel, n_cast=len(cast)),
        grid=(batch, NSA_KV_HEADS, nqb),
        in_specs=[
            pl.BlockSpec((SW_TQ, G * LANES), lambda b, h, i: (b * nqb + i, h)),
            pl.BlockSpec((seq, LANES), lambda b, h, i: (b, PB_SEL // LANES + h)),
            pl.BlockSpec((seq, LANES), lambda b, h, i: (b, PB_WIN // LANES + h)),
            pl.BlockSpec((SW_TQ, 256), lambda b, h, i: (b * nqb + i, h)),
            pl.BlockSpec((SW_TQ, LANES), lambda b, h, i: (b * nqb + i, PF_MISC // LANES + h)),
            pl.BlockSpec((1, 1, SW_TQ, QB), lambda b, h, i: (b, h, i, 0)),
            pl.BlockSpec((QB, seq), lambda b, h, i: (0, 0)),
            pl.BlockSpec((1, G * SW_TQ, 3 * QB), lambda b, h, i: (h, 0, 0)),
        ] + c_in,
        out_specs=[pl.BlockSpec((SW_TQ, 256), lambda b, h, i: (b * nqb + i, h))] + c_out,
        out_shape=[jax.ShapeDtypeStruct((batch * seq, NSA_WIDTH), F32)] + c_shapes,
        scratch_shapes=[pltpu.VMEM((SW_TQ, seq), F32)],
        compiler_params=_params(("parallel", "parallel", "arbitrary")),
        name="nsa_selected_window",
    )(pb, pb, pb, o_cmp, pf, sel, expand, tb, *cast)
    return out[0], out[1:]


def _ssd_kernel(*refs, n_cast):
    n_in = 12
    _cast_slabs(refs[n_in:n_in + n_cast], refs[n_in + n_cast + 1:n_in + 2 * n_cast + 1])
    _ssd_body(*refs[:n_in], refs[n_in + n_cast], *refs[n_in + 2 * n_cast + 1:])


def _ssd_body(z_ref, xs_ref, bc_ref, misc_ref, cwx_ref, cwb_ref, cbx_ref, cbb_ref, hp_ref, spread_ref,
              dskip_ref, ng_ref, o_ref, xbuf, bbuf, state):
    ci = pl.program_id(1)
    L = QB
    P = HEAD_DIM
    GN = SSD_GROUPS * SSD_STATE

    @pl.when(ci == 0)
    def _():
        xbuf[...] = jnp.zeros_like(xbuf)
        bbuf[...] = jnp.zeros_like(bbuf)
        state[...] = jnp.zeros_like(state)

    def conv_silu(buf, src_ref, w_ref, b_ref):
        cur = src_ref[...]
        ext = jnp.concatenate([buf[...], cur], axis=0)
        out = b_ref[...] + w_ref[SSD_CONV - 1:SSD_CONV, :] * cur
        for k in range(1, SSD_CONV):
            out = out + w_ref[SSD_CONV - 1 - k:SSD_CONV - k, :] * pltpu.roll(ext, k, 0)[8:8 + L]
        buf[...] = cur[L - 8:L]
        return _silu(out)

    xs = conv_silu(xbuf, xs_ref, cwx_ref, cbx_ref)
    bcs = conv_silu(bbuf, bc_ref, cwb_ref, cbb_ref)

    misc = misc_ref[...]
    pre = misc + hp_ref[0:1, :]
    dt = jnp.maximum(pre, 0.0) + jnp.log1p(jnp.exp(-jnp.abs(pre)))
    a_dt = dt * (-jnp.exp(hp_ref[1:2, :]))
    r = lax.broadcasted_iota(jnp.int32, (L, L), 0)
    c = lax.broadcasted_iota(jnp.int32, (L, L), 1)
    causal = r >= c
    a_cs = _split3_dot_left(causal.astype(BF16), a_dt)
    a_cs_t = a_cs.T

    dt_full = _split3_dot(dt, spread_ref[...])
    acs_full = _split3_dot(a_cs, spread_ref[...])
    total_full = acs_full[L - 1:L, :]
    xdt = xs * dt_full
    xdt16 = xdt.astype(BF16)
    to_end16 = (xdt * jnp.exp(total_full - acs_full)).astype(BF16)
    decay_in = jnp.exp(acs_full)
    chunk_decay = jnp.exp(total_full)
    low = c < P
    zero16 = jnp.zeros((L, 2 * P), BF16)
    width = SSD_HPG * P

    ys = []
    for g in range(SSD_GROUPS):
        bm = bcs[:, g * SSD_STATE:(g + 1) * SSD_STATE]
        cm16 = bcs[:, GN + g * SSD_STATE:GN + (g + 1) * SSD_STATE].astype(BF16)
        cb = _dot_nt(cm16, bm.astype(BF16))
        lanes = slice(g * width, (g + 1) * width)
        h_in = state[g]
        y_off = _dot(cm16, h_in.astype(BF16)) * decay_in[:, lanes]
        state[g] = h_in * chunk_decay[:, lanes] + _dot(bm.T.astype(BF16), to_end16[:, lanes])
        diag = []
        for pp in range(SSD_HPG // 2):
            hd = g * SSD_HPG + 2 * pp
            decayed = []
            for col in (MISC_DT + hd, MISC_DT + hd + 1):
                seg = jnp.exp(jnp.where(causal, a_cs[:, col:col + 1] - a_cs_t[col:col + 1, :], -jnp.inf))
                decayed.append((cb * seg).astype(BF16))
            xp = xdt16[:, hd * P:(hd + 2) * P]
            x_bd = jnp.concatenate([jnp.where(low, xp, zero16), jnp.where(low, zero16, xp)], axis=0)
            diag.append(_dot(jnp.concatenate(decayed, axis=1), x_bd))
        ys.append(jnp.concatenate(diag, axis=1) + y_off)

    y = (jnp.concatenate(ys, axis=1) + dskip_ref[...] * xs) * _silu(z_ref[...])
    for g in range(SSD_GROUPS):
        lanes = slice(g * width, (g + 1) * width)
        yg = y[:, lanes]
        o_ref[:, lanes] = yg * lax.rsqrt(jnp.mean(yg * yg, axis=-1, keepdims=True) + EPS) * ng_ref[:, lanes]


def _ssd(pf, conv_w, conv_b, head_params, d_skip, norm_g, batch, seq, cast=(), cast_layer=()):
    nc = seq // QB
    c_in, c_out, c_shapes = _cast_specs(cast, cast_layer, batch * nc, lambda b, ci: b * nc + ci)
    GN = SSD_GROUPS * SSD_STATE
    lane_head = jnp.arange(SSD_INNER)[None, :] // HEAD_DIM
    spread = (jnp.arange(LANES)[:, None] == MISC_DT + lane_head).astype(BF16)
    cwx, cwb = conv_w[:, :SSD_INNER], conv_w[:, SSD_INNER:]
    cbx, cbb = conv_b[:SSD_INNER].reshape(1, -1), conv_b[SSD_INNER:].reshape(1, -1)
    full = lambda shape: pl.BlockSpec(shape, lambda b, ci: (0,) * len(shape))
    out = pl.pallas_call(
        functools.partial(_ssd_kernel, n_cast=len(cast)),
        grid=(batch, nc),
        in_specs=[
            pl.BlockSpec((QB, SSD_INNER), lambda b, ci: (b * nc + ci, PF_Z // SSD_INNER)),
            pl.BlockSpec((QB, SSD_INNER), lambda b, ci: (b * nc + ci, PF_XS // SSD_INNER)),
            pl.BlockSpec((QB, 2 * GN), lambda b, ci: (b * nc + ci, PF_BC // (2 * GN))),
            pl.BlockSpec((QB, LANES), lambda b, ci: (b * nc + ci, PF_MISC // LANES)),
            full((SSD_CONV, SSD_INNER)),
            full((SSD_CONV, 2 * GN)),
            full((1, SSD_INNER)),
            full((1, 2 * GN)),
            full((8, LANES)),
            full((LANES, SSD_INNER)),
            full((1, SSD_INNER)),
            full((1, SSD_INNER)),
        ] + c_in,
        out_specs=[pl.BlockSpec((QB, SSD_INNER), lambda b, ci: (b * nc + ci, 0))] + c_out,
        out_shape=[jax.ShapeDtypeStruct((batch * seq, SSD_INNER), F32)] + c_shapes,
        scratch_shapes=[
            pltpu.VMEM((8, SSD_INNER), F32),
            pltpu.VMEM((8, 2 * GN), F32),
            pltpu.VMEM((SSD_GROUPS, SSD_STATE, SSD_HPG * HEAD_DIM), F32),
        ],
        compiler_params=_params(("parallel", "arbitrary")),
        name="ssd",
    )(pf, pf, pf, pf, cwx, cwb, cbx, cbb, head_params, spread,
      jnp.repeat(d_skip.astype(F32), HEAD_DIM).reshape(1, -1), norm_g.reshape(1, -1), *cast)
    return out[0], out[1:]


def _out_proj_kernel(x_ref, nsa_ref, sb_ref, ssd_ref, gn_ref, gs_ref, w_ref, o_ref, mix_ref):
    @pl.when(pl.program_id(1) == 0)
    def _():
        mix_ref[:, 0:NSA_WIDTH] = _rms(nsa_ref[...], gn_ref[...]).astype(BF16)
        mix_ref[:, NSA_WIDTH:NSA_WIDTH + SB_WIDTH] = _rms(sb_ref[...], gs_ref[...]).astype(BF16)
        mix_ref[:, NSA_WIDTH + SB_WIDTH:] = ssd_ref[...].astype(BF16)

    o_ref[...] = x_ref[...] + _dot(mix_ref[...], w_ref[...])


def _out_proj(x, o_nsa, o_sb, o_ssd, g_nsa, g_sb, w, *, tm=1024, tn=1024):
    n, d = x.shape
    dm = w.shape[0]
    return pl.pallas_call(
        _out_proj_kernel,
        grid=(n // tm, d // tn),
        in_specs=[
            pl.BlockSpec((tm, tn), lambda i, j: (i, j)),
            pl.BlockSpec((tm, NSA_WIDTH), lambda i, j: (i, 0)),
            pl.BlockSpec((tm, SB_WIDTH), lambda i, j: (i, 0)),
            pl.BlockSpec((tm, SSD_INNER), lambda i, j: (i, 0)),
            pl.BlockSpec((1, NSA_WIDTH), lambda i, j: (0, 0)),
            pl.BlockSpec((1, SB_WIDTH), lambda i, j: (0, 0)),
            pl.BlockSpec((dm, tn), lambda i, j: (0, j)),
        ],
        out_specs=pl.BlockSpec((tm, tn), lambda i, j: (i, j)),
        out_shape=jax.ShapeDtypeStruct((n, d), F32),
        scratch_shapes=[pltpu.VMEM((tm, dm), BF16)],
        compiler_params=_params(("parallel", "arbitrary")),
        name="out_proj",
    )(x, o_nsa, o_sb, o_ssd, g_nsa.reshape(1, -1), g_sb.reshape(1, -1), w)


def _rel_bucket(dist):
    dist = jnp.maximum(dist, 0)
    max_exact = REL_BUCKETS // 2
    log_ratio = jnp.log(jnp.maximum(dist, 1).astype(F32) / max_exact) / math.log(REL_MAX_DIST / max_exact)
    large = jnp.minimum(max_exact + (log_ratio * (REL_BUCKETS - max_exact)).astype(jnp.int32), REL_BUCKETS - 1)
    return jnp.where(dist < max_exact, dist, large)


def _bias_tables(rel_bias, seq):
    assert QB >= REL_MAX_DIST
    def lookup(dist):
        buckets = jnp.arange(REL_BUCKETS).reshape((-1,) + (1,) * dist.ndim)
        onehot = (_rel_bucket(dist)[None] == buckets).astype(F32)
        return jnp.einsum('kh,k...->h...', rel_bias, onehot, precision=lax.Precision.HIGHEST)

    t = jnp.arange(seq)[:, None]
    cend = jnp.arange(QB)[None, :] * CMP_STRIDE + CMP_BLOCK - 1
    n_cmp = (seq - CMP_BLOCK) // CMP_STRIDE + 1
    valid_c = (t >= cend) & (jnp.arange(QB)[None, :] < n_cmp)
    bias_c = jnp.where(valid_c, lookup(t - cend), NEG_INF)
    r = jnp.arange(QB)[:, None]
    m = jnp.arange(QB)[None, :]
    near = jnp.stack([lookup(r - m), lookup(QB + r - m)])
    near = near - rel_bias[REL_BUCKETS - 1][None, :, None, None]
    diag = near[0] + jnp.where(m > r, NEG_INF, 0.0)
    prev = near[1]
    rows_a = jnp.concatenate([prev, diag, jnp.full_like(diag, NEG_INF)], axis=2)
    rows_b = jnp.concatenate([jnp.zeros_like(diag), prev, diag], axis=2)
    near2 = jnp.concatenate([rows_a, rows_b], axis=1)
    return bias_c, near2.reshape(NSA_KV_HEADS, NSA_GQA * SW_TQ, 3 * QB)


def _expand_table(seq):
    j = jnp.arange(QB)[:, None]
    s = jnp.arange(seq)[None, :]
    return (s // SEL_BLOCK == j).astype(BF16)


def _overlap_table(seq):
    n_cmp = (seq - CMP_BLOCK) // CMP_STRIDE + 1
    cs = jnp.arange(QB)[:, None] * CMP_STRIDE
    ce = cs + CMP_BLOCK - 1
    ss = jnp.arange(QB)[None, :] * SEL_BLOCK
    ov = jnp.maximum(jnp.minimum(ce, ss + SEL_BLOCK - 1) - jnp.maximum(cs, ss) + 1, 0).astype(F32) / CMP_BLOCK
    keep = (jnp.arange(QB)[:, None] < n_cmp) & (jnp.arange(QB)[None, :] < seq // SEL_BLOCK)
    return jnp.where(keep, ov, 0.0).T.astype(BF16)


def _in_proj_weights(w):
    w = w.astype(BF16)
    scale = HEAD_DIM ** -0.5
    o_q, o_kv, o_gate = 0, NSA_WIDTH, NSA_WIDTH + 768
    o_sb = o_gate + 3 * NSA_HEADS
    o_z = o_sb + 3 * SB_WIDTH
    o_xbc = o_z + SSD_INNER
    o_dt = o_xbc + SSD_INNER + 2 * SSD_GROUPS * SSD_STATE
    d = w.shape[0]
    col = lambda a, n: w[:, a:a + n]
    zeros = lambda n: jnp.zeros((d, n), w.dtype)
    q_pad = jnp.pad(col(o_q, NSA_WIDTH).reshape(d, NSA_HEADS, HEAD_DIM) * scale,
                    ((0, 0), (0, 0), (0, LANES - HEAD_DIM))).reshape(d, NSA_HEADS * LANES)
    kv = col(o_kv + 256, 512).reshape(d, 2, 2, NSA_KV_HEADS, HEAD_DIM).transpose(0, 1, 3, 2, 4).reshape(d, 512)
    part_bf16 = [q_pad, kv, col(o_sb, SB_WIDTH) * scale, col(o_sb + SB_WIDTH, 2 * SB_WIDTH)]
    gates = 3 * NSA_GQA
    part_f32 = [col(o_z, 2 * SSD_INNER + 2 * SSD_GROUPS * SSD_STATE), col(o_kv, 256),
                col(o_gate, gates), zeros(MISC_DT - gates), col(o_dt, SSD_HEADS),
                zeros(LANES - MISC_DT - SSD_HEADS), col(o_gate + gates, gates), zeros(LANES - gates)]
    out = jnp.concatenate(part_bf16 + part_f32, axis=1)
    assert out.shape[1] == PB_COLS + PF_COLS and sum(p.shape[1] for p in part_bf16) == PB_COLS
    return out


def _head_params(dt_bias, a_log):
    rows = jnp.stack([dt_bias, a_log]).astype(F32)
    return jnp.zeros((8, LANES), F32).at[0:2, MISC_DT:MISC_DT + SSD_HEADS].set(rows)


def _mixer(x, l, tables, batch, seq, mix_norm, w_in, w_out, cmp_pos, cmp_w, nsa_norm, sb_norm,
           conv_w, conv_b, dt_bias, a_log, d_skip, ssd_norm, casts):
    hosts = [([w for w, _ in casts[k::2]], tuple(at for _, at in casts[k::2])) for k in range(2)]
    bias_c, near, overlap, expand = tables
    pb, pf = _in_proj(x, mix_norm[l], _in_proj_weights(w_in[l]), PB_COLS)

    ng = seq // CMP_STRIDE
    ucmp = pf[:, PF_CMP:PF_CMP + 256].reshape(batch, ng, CMP_STRIDE, 4, HEAD_DIM)
    ucmp = ucmp.transpose(0, 3, 1, 2, 4).reshape(batch, 4, ng, CMP_STRIDE * HEAD_DIM)
    pos2 = cmp_pos[l].reshape(2, 2, CMP_STRIDE * HEAD_DIM)
    o_cmp, sel = _nsa_cmp(pb, ucmp, pos2, cmp_w[l].astype(BF16), bias_c, overlap, batch, seq)
    o_nsa, w0 = _nsa_sel_win(pb, pf, o_cmp, sel, expand, near, batch, seq, *hosts[0])
    o_sb, w1 = _sb_attention(pb, batch, seq, *hosts[1])
    o_ssd, _ = _ssd(pf, conv_w[l], conv_b[l], _head_params(dt_bias[l], a_log[l]), d_skip[l], ssd_norm[l],
                    batch, seq)
    out = _out_proj(x, o_nsa, o_sb, o_ssd, nsa_norm[l], sb_norm[l], w_out[l].astype(BF16))
    done = [None] * len(casts)
    for k, ws in enumerate((w0, w1)):
        done[k::2] = ws
    return out, done


def kernel(x, rel_bias, ffn1_norm, ffn1_w_gate, ffn1_w_up, ffn1_w_down, mix_norm, w_in, w_out, nsa_cmp_pos, nsa_cmp_w, nsa_out_norm, sb_out_norm, ssd_conv_w, ssd_conv_b, ssd_dt_bias, ssd_a_log, ssd_d, ssd_out_norm, ffn2_norm, ffn2_w_gate, ffn2_w_up, ffn2_w_down, final_norm):
    batch, seq, d = x.shape
    depth = w_in.shape[0]
    tables = _bias_tables(rel_bias, seq) + (_overlap_table(seq), _expand_table(seq))
    h = x.reshape(batch * seq, d)

    ffn1 = (ffn1_w_gate, ffn1_w_up, ffn1_w_down)
    ffn2 = (ffn2_w_gate, ffn2_w_up, ffn2_w_down)
    widen = lambda ws: tuple(w[None] for w in ws)
    w1 = widen(_cast_layer(ffn1, 0))
    for l in range(depth):
        h = _ffn(h, ffn1_norm[l], *w1, 0, final_norm, final=False)
        casts = [(w, l) for w in ffn2] + ([(w, l + 1) for w in ffn1] if l + 1 < depth else [])
        h, done = _mixer(h, l, tables, batch, seq, mix_norm, w_in, w_out, nsa_cmp_pos, nsa_cmp_w,
                         nsa_out_norm, sb_out_norm, ssd_conv_w, ssd_conv_b, ssd_dt_bias, ssd_a_log,
                         ssd_d, ssd_out_norm, casts)
        h = _ffn(h, ffn2_norm[l], *widen(done[0:3]), 0, final_norm, final=(l == depth - 1))
        w1 = widen(done[3:6])
    return h.reshape(batch, seq, d)
```

```python
import functools
import math

import jax
import jax.numpy as jnp
from jax import lax
from jax.experimental import pallas as pl
from jax.experimental.pallas import tpu as pltpu

LANES = 128
HEAD_DIM = 64
QB = 128
NEG_INF = -1e30
EPS = 1e-6

NSA_HEADS = 8
NSA_KV_HEADS = 2
NSA_GQA = NSA_HEADS // NSA_KV_HEADS
NSA_WIDTH = NSA_HEADS * HEAD_DIM
CMP_BLOCK = 32
CMP_STRIDE = 16
SEL_BLOCK = 64
SEL_TOPK = 8
FORCE_SCORE = 1e3
WINDOW = 512
SB_HEADS = 8
SB_WIDTH = SB_HEADS * HEAD_DIM
SSD_HEADS = 16
SSD_INNER = SSD_HEADS * HEAD_DIM
SSD_GROUPS = 2
SSD_HPG = SSD_HEADS // SSD_GROUPS
SSD_STATE = 128
SSD_CONV = 4
REL_BUCKETS = 32
REL_MAX_DIST = 128

PB_NSA_Q, PB_SEL, PB_WIN, PB_SB_Q, PB_SB_K, PB_SB_V, PB_COLS = 0, 1024, 1280, 1536, 2048, 2560, 3072
PF_Z, PF_XS, PF_BC, PF_CMP, PF_MISC, PF_COLS = 0, 1024, 2048, 2560, 2816, 3072
MISC_DT = 16

VMEM_LIMIT = 52 * 1024 * 1024

BF16 = jnp.bfloat16
F32 = jnp.float32


def _dot(a, b):
    return jnp.dot(a, b, preferred_element_type=F32)


def _dot_nt(a, b):
    return lax.dot_general(a, b, (((1,), (1,)), ((), ())), preferred_element_type=F32)


def _split3_dot(x, m):
    hi = x.astype(BF16)
    r1 = x - hi.astype(F32)
    mid = r1.astype(BF16)
    lo = (r1 - mid.astype(F32)).astype(BF16)
    return _dot(hi, m) + _dot(mid, m) + _dot(lo, m)


def _split3_dot_left(m, x):
    hi = x.astype(BF16)
    r1 = x - hi.astype(F32)
    mid = r1.astype(BF16)
    lo = (r1 - mid.astype(F32)).astype(BF16)
    return _dot(m, hi) + _dot(m, mid) + _dot(m, lo)


def _rms(x, g):
    return x * lax.rsqrt(jnp.mean(x * x, axis=-1, keepdims=True) + EPS) * g


def _silu(x):
    return x / (1.0 + jnp.exp(-x))


def _sigmoid(x):
    return 1.0 / (1.0 + jnp.exp(-x))


def _params(sem):
    return pltpu.CompilerParams(dimension_semantics=sem, vmem_limit_bytes=VMEM_LIMIT)


def _cast_specs(weights, layer, steps, step_of):
    in_specs, out_specs, out_shapes = [], [], []
    for w, at in zip(weights, layer):
        _, rows, cols = w.shape
        share = 1
        while (rows * share) % (16 * steps):
            share *= 2
        slab = rows * share // steps
        in_specs.append(pl.BlockSpec((None, slab, cols),
                                     lambda *ids, k=share, at=at: (at, step_of(*ids) // k, 0)))
        out_specs.append(pl.BlockSpec((slab, cols), lambda *ids, k=share: (step_of(*ids) // k, 0)))
        out_shapes.append(jax.ShapeDtypeStruct((rows, cols), BF16))
    return in_specs, out_specs, out_shapes


def _cast_slabs(srcs, dsts):
    for src, dst in zip(srcs, dsts):
        dst[...] = src[...].astype(BF16)


def _cast_kernel(*refs):
    _cast_slabs(refs[:len(refs) // 2], refs[len(refs) // 2:])


def _cast_layer(weights, layer, steps=16):
    c_in, c_out, c_shapes = _cast_specs(weights, (layer,) * len(weights), steps, lambda s: s)
    return pl.pallas_call(_cast_kernel, grid=(steps,), in_specs=c_in, out_specs=c_out, out_shape=c_shapes,
                          compiler_params=_params(("parallel",)), name="cast_weights")(*weights)


def _ffn_kernel(x_ref, g_ref, wg_ref, wu_ref, wd_ref, fg_ref, o_ref, h_ref, acc_ref, *, final):
    f = pl.program_id(1)

    @pl.when(f == 0)
    def _():
        h_ref[...] = _rms(x_ref[...], g_ref[...]).astype(BF16)
        acc_ref[...] = jnp.zeros_like(acc_ref)

    h = h_ref[...]
    gate = _dot(h, wg_ref[...])
    up = _dot(h, wu_ref[...])
    acc_ref[...] += _dot((_silu(gate) * up).astype(BF16), wd_ref[...])

    @pl.when(f == pl.num_programs(1) - 1)
    def _():
        y = x_ref[...] + 0.5 * acc_ref[...]
        if final:
            y = _rms(y, fg_ref[...])
        o_ref[...] = y


def _ffn(x, g, wg, wu, wd, layer, final_g, *, final, tm=512, tf=512):
    n, d = x.shape
    dff = wg.shape[2]
    return pl.pallas_call(
        functools.partial(_ffn_kernel, final=final),
        grid=(n // tm, dff // tf),
        in_specs=[
            pl.BlockSpec((tm, d), lambda i, f: (i, 0)),
            pl.BlockSpec((1, d), lambda i, f: (0, 0)),
            pl.BlockSpec((None, d, tf), lambda i, f: (layer, 0, f)),
            pl.BlockSpec((None, d, tf), lambda i, f: (layer, 0, f)),
            pl.BlockSpec((None, tf, d), lambda i, f: (layer, f, 0)),
            pl.BlockSpec((1, d), lambda i, f: (0, 0)),
        ],
        out_specs=pl.BlockSpec((tm, d), lambda i, f: (i, 0)),
        out_shape=jax.ShapeDtypeStruct((n, d), F32),
        scratch_shapes=[pltpu.VMEM((tm, d), BF16), pltpu.VMEM((tm, d), F32)],
        compiler_params=_params(("parallel", "arbitrary")),
        name="ffn",
    )(x, g.reshape(1, d), wg, wu, wd, final_g.reshape(1, d))


def _in_proj_kernel(x_ref, g_ref, w_ref, ob_ref, of_ref, h_ref, *, nb):
    j = pl.program_id(1)

    @pl.when(j == 0)
    def _():
        h_ref[...] = _rms(x_ref[...], g_ref[...]).astype(BF16)

    @pl.when(j < nb)
    def _():
        ob_ref[...] = _dot(h_ref[...], w_ref[...]).astype(BF16)

    @pl.when(j >= nb)
    def _():
        of_ref[...] = _dot(h_ref[...], w_ref[...])


def _in_proj(x, g, w, cols_bf16, *, tm=1024, tn=1024):
    n, d = x.shape
    c = w.shape[1]
    nb = cols_bf16 // tn
    return pl.pallas_call(
        functools.partial(_in_proj_kernel, nb=nb),
        grid=(n // tm, c // tn),
        in_specs=[
            pl.BlockSpec((tm, d), lambda i, j: (i, 0)),
            pl.BlockSpec((1, d), lambda i, j: (0, 0)),
            pl.BlockSpec((d, tn), lambda i, j: (0, j)),
        ],
        out_specs=[pl.BlockSpec((tm, tn), lambda i, j: (i, jnp.minimum(j, nb - 1))),
                   pl.BlockSpec((tm, tn), lambda i, j: (i, jnp.maximum(j - nb, 0)))],
        out_shape=[jax.ShapeDtypeStruct((n, cols_bf16), BF16),
                   jax.ShapeDtypeStruct((n, c - cols_bf16), F32)],
        scratch_shapes=[pltpu.VMEM((tm, d), BF16)],
        compiler_params=_params(("parallel", "arbitrary")),
        name="in_proj",
    )(x, g.reshape(1, d), w)


SB_GROUP = 8
SB_DEAD = 104.0


def _sb_kernel(*refs, n_cast):
    q_ref, k_ref, v_ref = refs[:3]
    o_ref = refs[3 + n_cast]
    kbd_ref, vbd_ref = refs[4 + 2 * n_cast:]
    _cast_slabs(refs[3:3 + n_cast], refs[4 + n_cast:4 + 2 * n_cast])
    _sb_body(q_ref, k_ref, v_ref, o_ref, kbd_ref, vbd_ref)


def _sb_body(q_ref, k_ref, v_ref, o_ref, kbd_ref, vbd_ref):
    i = pl.program_id(2)
    pairs = SB_GROUP // 2
    nkb = k_ref.shape[0] // QB
    r = lax.broadcasted_iota(jnp.int32, (QB, QB), 0)
    c = lax.broadcasted_iota(jnp.int32, (QB, QB), 1)
    strict_all = jnp.concatenate([c < r] * SB_GROUP, axis=0)
    later_ones = jnp.concatenate([(r > c).astype(BF16), jnp.ones((QB, QB), BF16)], axis=1)
    later_ones = jnp.concatenate([later_ones, later_ones], axis=0)

    @pl.when(i == 0)
    def _():
        low = c < HEAD_DIM
        zero16 = jnp.zeros((QB, QB), BF16)

        def build(j, carry):
            src = pl.ds(pl.multiple_of(j * QB, QB), QB)
            dst = pl.ds(pl.multiple_of(j * 2 * QB, 2 * QB), 2 * QB)
            for pp in range(pairs):
                for ref, out in ((k_ref, kbd_ref), (v_ref, vbd_ref)):
                    x = ref[src, pp * LANES:(pp + 1) * LANES]
                    out[pp, dst, :] = jnp.concatenate([jnp.where(low, x, zero16), jnp.where(low, zero16, x)],
                                                      axis=0)
            return carry

        lax.fori_loop(0, nkb, build, 0)

    def scores(j, diagonal):
        rows = pl.ds(pl.multiple_of(j * 2 * QB, 2 * QB), 2 * QB)
        zs = []
        for pp in range(pairs):
            z_pair = _dot_nt(q_ref[:, pp * LANES:(pp + 1) * LANES], kbd_ref[pp, rows, :])
            zs += [z_pair[:, 0:QB], z_pair[:, QB:2 * QB]]
        z = jnp.concatenate(zs, axis=0)
        soft = jnp.maximum(z, 0.0) + jnp.log(1.0 + jnp.exp(-jnp.abs(z)))
        if diagonal:
            soft = jnp.where(strict_all, soft, 0.0)
        hi = soft.astype(BF16)
        lo = (soft - hi.astype(F32)).astype(BF16)
        sums = _dot(jnp.concatenate([hi, lo], axis=1), later_ones)
        return rows, z - soft, sums

    def absorb(tails, accs, rows, log_beta, sums, diagonal):
        a = jnp.exp(log_beta - (tails + sums[:, 0:QB]))
        if diagonal:
            a = jnp.where(strict_all, a, 0.0)
        a = a.astype(BF16)
        outs = []
        for pp in range(pairs):
            pair = jnp.concatenate([a[2 * pp * QB:(2 * pp + 1) * QB], a[(2 * pp + 1) * QB:(2 * pp + 2) * QB]],
                                   axis=1)
            outs.append(_dot(pair, vbd_ref[pp, rows, :]))
        return tails + sums[:, QB:2 * QB], accs + jnp.concatenate(outs, axis=0)

    def sweep(js, carry):
        staged = [scores(j, False) for j in js]
        for st in staged:
            carry = absorb(*carry, *st, False)
        return carry

    carry = (jnp.zeros((SB_GROUP * QB, QB), F32), jnp.zeros((pairs * QB, LANES), F32))
    carry = absorb(*carry, *scores(i, True), True)
    odd = i % 2
    carry = lax.fori_loop(0, odd, lambda step, cr: sweep([i - 1], cr), carry)
    top = i - 1 - odd

    def live(state):
        step, smallest, _, _ = state
        return (step < i // 2) & (smallest < SB_DEAD)

    def pair_step(state):
        step, _, tails, accs = state
        tails, accs = sweep([top - 2 * step, top - 2 * step - 1], (tails, accs))
        return step + 1, jnp.min(tails), tails, accs

    accs = lax.while_loop(live, pair_step, (0, jnp.min(carry[0]), *carry))[3]
    for pp in range(pairs):
        o_ref[:, pp * LANES:(pp + 1) * LANES] = accs[pp * QB:(pp + 1) * QB]


def _sb_attention(pb, batch, seq, cast=(), cast_layer=()):
    nqb = seq // QB
    w = SB_GROUP * HEAD_DIM
    assert SB_HEADS == SB_GROUP
    qc, kc, vc = PB_SB_Q // w, PB_SB_K // w, PB_SB_V // w
    c_in, c_out, c_shapes = _cast_specs(cast, cast_layer, batch * nqb, lambda b, p, i: b * nqb + i)
    out = pl.pallas_call(
        functools.partial(_sb_kernel, n_cast=len(cast)),
        grid=(batch, SB_HEADS // SB_GROUP, nqb),
        in_specs=[
            pl.BlockSpec((QB, w), lambda b, p, i: (b * nqb + i, qc + p)),
            pl.BlockSpec((seq, w), lambda b, p, i: (b, kc + p)),
            pl.BlockSpec((seq, w), lambda b, p, i: (b, vc + p)),
        ] + c_in,
        out_specs=[pl.BlockSpec((QB, w), lambda b, p, i: (b * nqb + i, p))] + c_out,
        out_shape=[jax.ShapeDtypeStruct((batch * seq, SB_WIDTH), F32)] + c_shapes,
        scratch_shapes=[pltpu.VMEM((SB_GROUP // 2, 2 * seq, LANES), BF16),
                        pltpu.VMEM((SB_GROUP // 2, 2 * seq, LANES), BF16)],
        compiler_params=_params(("parallel", "parallel", "arbitrary")),
        name="sb_attention",
    )(pb, pb, pb, *cast)
    return out[0], out[1:]


CMP_TQ = 4 * QB


def _nsa_cmp_kernel(q_ref, uk_ref, uv_ref, pos_ref, w_ref, b0_ref, b1_ref, b2_ref, b3_ref, ovt_ref,
                    ocmp_ref, sel_ref, kc_ref, vc_ref, *, n_cmp, n_sel):
    i = pl.program_id(2)
    half = CMP_STRIDE * HEAD_DIM

    @pl.when(i == 0)
    def _():
        def compress(kv, u_ref):
            u = u_ref[0, 0]
            top = _dot((u + pos_ref[kv, 0:1, :]).astype(BF16), w_ref[kv, 0:half, :])
            bot = _dot((u + pos_ref[kv, 1:2, :]).astype(BF16), w_ref[kv, half:2 * half, :])
            return top + pltpu.roll(bot, QB - 1, 0)

        zeros = jnp.zeros((QB, HEAD_DIM), F32)
        kc_ref[...] = jnp.concatenate([compress(0, uk_ref), zeros], axis=1).astype(BF16)
        v = compress(1, uv_ref)
        vc_ref[...] = jnp.concatenate([jnp.concatenate([v, zeros], axis=1),
                                       jnp.concatenate([zeros, v], axis=1)], axis=0).astype(BF16)

    G = NSA_GQA
    TQ = CMP_TQ
    q4 = jnp.concatenate([q_ref[:, g * LANES:(g + 1) * LANES] for g in range(G)], axis=0)
    bias = jnp.concatenate([b[...] for b in (b0_ref, b1_ref, b2_ref, b3_ref)], axis=0)
    s = _dot_nt(q4, kc_ref[...]) + bias
    e = jnp.exp(s - jnp.max(s, axis=-1, keepdims=True))
    t_rows = i * TQ + lax.broadcasted_iota(jnp.int32, (TQ, 1), 0)
    any_valid = jnp.concatenate([t_rows >= CMP_BLOCK - 1] * G, axis=0)
    p = jnp.where(any_valid, e / jnp.sum(e, axis=-1, keepdims=True), 0.0)
    p16 = p.astype(BF16)
    for pp in range(G // 2):
        pair = jnp.concatenate([p16[2 * pp * TQ:(2 * pp + 1) * TQ], p16[(2 * pp + 1) * TQ:(2 * pp + 2) * TQ]],
                               axis=1)
        ocmp_ref[:, pp * LANES:(pp + 1) * LANES] = _dot(pair, vc_ref[...])
    p_all = p[0:TQ]
    for g in range(1, G):
        p_all = p_all + p[g * TQ:(g + 1) * TQ]

    blk = lax.broadcasted_iota(jnp.int32, (n_sel, QB), 0)
    for part in range(TQ // QB):
        rs = slice(part * QB, (part + 1) * QB)
        t0 = (i * (TQ // QB) + part) * QB
        p_sum = p_all[rs]
        hi = p_sum.astype(BF16)
        lo = (p_sum - hi.astype(F32)).astype(BF16)
        p_sel = (_dot_nt(ovt_ref[...], hi) + _dot_nt(ovt_ref[...], lo))[0:n_sel]
        t = t0 + lax.broadcasted_iota(jnp.int32, (n_sel, QB), 1)
        cur = t // SEL_BLOCK
        eligible = blk * SEL_BLOCK <= t
        forced = (blk == 0) | (blk == cur) | (blk == cur - 1)
        score = jnp.where(eligible, p_sel + jnp.where(forced, FORCE_SCORE, 0.0), NEG_INF)
        rank = jnp.zeros((n_sel, QB), F32)
        for j in range(n_sel):
            other = score[j:j + 1, :]
            ahead = (other > score) | ((other == score) & (blk > j))
            rank = rank + jnp.where(ahead, 1.0, 0.0)
        chosen = jnp.where(eligible & (rank < SEL_TOPK), 1.0, 0.0)
        chosen = jnp.concatenate([chosen, jnp.zeros((QB - n_sel, QB), F32)], axis=0)
        sel_ref[0, 0, rs, :] = chosen.T.astype(BF16)


def _nsa_cmp(pb, ucmp, pos2, cmp_w, bias_c, overlap_t, batch, seq):
    nqb = seq // CMP_TQ
    n_cmp = (seq - CMP_BLOCK) // CMP_STRIDE + 1
    n_sel = seq // SEL_BLOCK
    ng = seq // CMP_STRIDE
    assert ng == QB and n_sel <= QB and n_sel % 8 == 0
    wide = CMP_STRIDE * HEAD_DIM
    return pl.pallas_call(
        functools.partial(_nsa_cmp_kernel, n_cmp=n_cmp, n_sel=n_sel),
        grid=(batch, NSA_KV_HEADS, nqb),
        in_specs=[
            pl.BlockSpec((CMP_TQ, NSA_GQA * LANES), lambda b, h, i: (b * nqb + i, h)),
            pl.BlockSpec((1, 1, ng, wide), lambda b, h, i: (b, h, 0, 0)),
            pl.BlockSpec((1, 1, ng, wide), lambda b, h, i: (b, NSA_KV_HEADS + h, 0, 0)),
            pl.BlockSpec((2, 2, wide), lambda b, h, i: (0, 0, 0)),
            pl.BlockSpec((2, 2 * wide, HEAD_DIM), lambda b, h, i: (0, 0, 0)),
            *[pl.BlockSpec((None, CMP_TQ, QB), lambda b, h, i, g=g: (h * NSA_GQA + g, i, 0))
              for g in range(NSA_GQA)],
            pl.BlockSpec((QB, QB), lambda b, h, i: (0, 0)),
        ],
        out_specs=[
            pl.BlockSpec((CMP_TQ, 256), lambda b, h, i: (b * nqb + i, h)),
            pl.BlockSpec((1, 1, CMP_TQ, QB), lambda b, h, i: (b, h, i, 0)),
        ],
        out_shape=[
            jax.ShapeDtypeStruct((batch * seq, NSA_WIDTH), F32),
            jax.ShapeDtypeStruct((batch, NSA_KV_HEADS, seq, QB), BF16),
        ],
        scratch_shapes=[pltpu.VMEM((QB, LANES), BF16), pltpu.VMEM((2 * QB, LANES), BF16)],
        compiler_params=_params(("parallel", "parallel", "arbitrary")),
        name="nsa_compressed",
    )(pb, ucmp, ucmp, pos2, cmp_w, *[bias_c] * NSA_GQA, overlap_t)


SW_TQ = 2 * QB
SEL_CHUNK = 4 * QB
WIN_FAR = WINDOW - QB


def _nsa_sw_kernel(*refs, n_cast):
    n_in = 8
    _cast_slabs(refs[n_in:n_in + n_cast], refs[n_in + n_cast + 1:n_in + 2 * n_cast + 1])
    _nsa_sw_body(*refs[:n_in], refs[n_in + n_cast], *refs[n_in + 2 * n_cast + 1:])


def _nsa_sw_body(q_ref, ks_ref, kw_ref, ocmp_ref, misc_ref, sel_ref, exp_ref, tb_ref, o_ref, madd_ref):
    i = pl.program_id(2)
    G = NSA_GQA
    TQ = SW_TQ
    q4 = jnp.concatenate([q_ref[:, g * LANES:(g + 1) * LANES] for g in range(G)], axis=0)
    madd_ref[...] = (_dot(sel_ref[0, 0], exp_ref[...]) - 1.0) * (-NEG_INF)
    t_pos = i * TQ + lax.broadcasted_iota(jnp.int32, (TQ, 1), 0)

    low = lax.broadcasted_iota(jnp.int32, (QB, LANES), 1) < HEAD_DIM

    def tile(x, n=G):
        return jnp.concatenate([x] * n, axis=0)

    def ones_v(kv):
        return jnp.where(tile(low, kv.shape[0] // QB), jnp.ones_like(kv), kv)

    first = 2 * i - 1
    rows_m = pl.ds(pl.multiple_of(jnp.maximum(first, 0) * QB, QB), QB)
    rows_a = pl.ds(pl.multiple_of(2 * i * QB, QB), QB)
    rows_b = pl.ds(pl.multiple_of((2 * i + 1) * QB, QB), QB)
    gone = jnp.where(i >= 1, 0.0, NEG_INF)
    gone_m = jnp.concatenate([jnp.full((TQ, QB), gone, F32), jnp.zeros((TQ, 2 * QB), F32)], axis=1)
    far_end = first * QB

    kv = jnp.concatenate([ks_ref[rows_m, :], ks_ref[rows_a, :], ks_ref[rows_b, :]], axis=0)
    near_sel = jnp.concatenate([madd_ref[:, rows_m], madd_ref[:, rows_a], madd_ref[:, rows_b]], axis=1)
    s = _dot_nt(q4, kv) + (tb_ref[0] + tile(near_sel + gone_m))
    m = jnp.max(s, axis=-1, keepdims=True)
    acc = _dot(jnp.exp(s - m).astype(BF16), ones_v(kv))

    def sel_far(cidx, state):
        m, acc = state
        cols = pl.ds(pl.multiple_of(cidx * SEL_CHUNK, SEL_CHUNK), SEL_CHUNK)
        k_pos = cidx * SEL_CHUNK + lax.broadcasted_iota(jnp.int32, (TQ, SEL_CHUNK), 1)
        add = jnp.where(k_pos < far_end, madd_ref[:, cols], NEG_INF)
        kv = ks_ref[cols, :]
        s = _dot_nt(q4, kv) + tile(add)
        m_new = jnp.maximum(m, jnp.max(s, axis=-1, keepdims=True))
        return m_new, jnp.exp(m - m_new) * acc + _dot(jnp.exp(s - m_new).astype(BF16), ones_v(kv))

    _, sel_acc = lax.fori_loop(0, (2 * i + 2) // (SEL_CHUNK // QB), sel_far, (m, acc))

    start = jnp.maximum(first - WIN_FAR // QB, 0) * QB
    rows_far = pl.ds(pl.multiple_of(start, QB), WIN_FAR)
    k_pos = start + lax.broadcasted_iota(jnp.int32, (TQ, WIN_FAR), 1)
    live = (k_pos < far_end) & (k_pos > t_pos - WINDOW)
    kv = jnp.concatenate([kw_ref[rows_far, :], kw_ref[rows_m, :], kw_ref[rows_a, :], kw_ref[rows_b, :]], axis=0)
    bias = jnp.concatenate([tile(jnp.where(live, 0.0, NEG_INF)), tb_ref[0] + tile(gone_m)], axis=1)
    s = _dot_nt(q4, kv) + bias
    win_acc = _dot(jnp.exp(s - jnp.max(s, axis=-1, keepdims=True)).astype(BF16), ones_v(kv))

    gate = _sigmoid(misc_ref[...])
    low = tile(low, TQ // QB)

    def pair(acc, pp):
        a, b = acc[2 * pp * TQ:(2 * pp + 1) * TQ], acc[(2 * pp + 1) * TQ:(2 * pp + 2) * TQ]
        return (jnp.where(low, pltpu.roll(a, HEAD_DIM, 1), b)
                / jnp.where(low, a, pltpu.roll(b, HEAD_DIM, 1)))

    def pair_gate(branch, pp):
        ca, cb = 3 * (2 * pp) + branch, 3 * (2 * pp + 1) + branch
        return jnp.where(low, gate[:, ca:ca + 1], gate[:, cb:cb + 1])

    for pp in range(G // 2):
        lanes = slice(pp * LANES, (pp + 1) * LANES)
        o_ref[:, lanes] = (pair_gate(0, pp) * ocmp_ref[:, lanes] + pair_gate(1, pp) * pair(sel_acc, pp)
                           + pair_gate(2, pp) * pair(win_acc, pp))


def _nsa_sel_win(pb, pf, o_cmp, sel, expand, tb, batch, seq, cast=(), cast_layer=()):
    nqb = seq // SW_TQ
    G = NSA_GQA
    c_in, c_out, c_shapes = _cast_specs(cast, cast_layer, batch * NSA_KV_HEADS * nqb,
                                        lambda b, h, i: (b * NSA_KV_HEADS + h) * nqb + i)
    out = pl.pallas_call(
        functools.partial(_nsa_sw_kernel, n_cast=len(cast)),
        grid=(batch, NSA_KV_HEADS, nqb),
        in_specs=[
            pl.BlockSpec((SW_TQ, G * LANES), lambda b, h, i: (b * nqb + i, h)),
            pl.BlockSpec((seq, LANES), lambda b, h, i: (b, PB_SEL // LANES + h)),
            pl.BlockSpec((seq, LANES), lambda b, h, i: (b, PB_WIN // LANES + h)),
            pl.BlockSpec((SW_TQ, 256), lambda b, h, i: (b * nqb + i, h)),
            pl.BlockSpec((SW_TQ, LANES), lambda b, h, i: (b * nqb + i, PF_MISC // LANES + h)),
            pl.BlockSpec((1, 1, SW_TQ, QB), lambda b, h, i: (b, h, i, 0)),
            pl.BlockSpec((QB, seq), lambda b, h, i: (0, 0)),
            pl.BlockSpec((1, G * SW_TQ, 3 * QB), lambda b, h, i: (h, 0, 0)),
        ] + c_in,
        out_specs=[pl.BlockSpec((SW_TQ, 256), lambda b, h, i: (b * nqb + i, h))] + c_out,
        out_shape=[jax.ShapeDtypeStruct((batch * seq, NSA_WIDTH), F32)] + c_shapes,
        scratch_shapes=[pltpu.VMEM((SW_TQ, seq), F32)],
        compiler_params=_params(("parallel", "parallel", "arbitrary")),
        name="nsa_selected_window",
    )(pb, pb, pb, o_cmp, pf, sel, expand, tb, *cast)
    return out[0], out[1:]


def _ssd_kernel(z_ref, xs_ref, bc_ref, misc_ref, cwx_ref, cwb_ref, cbx_ref, cbb_ref, hp_ref, spread_ref,
                dskip_ref, ng_ref, o_ref, xbuf, bbuf, state):
    ci = pl.program_id(1)
    L = QB
    P = HEAD_DIM
    GN = SSD_GROUPS * SSD_STATE

    @pl.when(ci == 0)
    def _():
        xbuf[...] = jnp.zeros_like(xbuf)
        bbuf[...] = jnp.zeros_like(bbuf)
        state[...] = jnp.zeros_like(state)

    def conv_silu(buf, src_ref, w_ref, b_ref):
        cur = src_ref[...]
        ext = jnp.concatenate([buf[...], cur], axis=0)
        out = b_ref[...] + w_ref[SSD_CONV - 1:SSD_CONV, :] * cur
        for k in range(1, SSD_CONV):
            out = out + w_ref[SSD_CONV - 1 - k:SSD_CONV - k, :] * pltpu.roll(ext, k, 0)[8:8 + L]
        buf[...] = cur[L - 8:L]
        return _silu(out)

    xs = conv_silu(xbuf, xs_ref, cwx_ref, cbx_ref)
    bcs = conv_silu(bbuf, bc_ref, cwb_ref, cbb_ref)

    misc = misc_ref[...]
    pre = misc + hp_ref[0:1, :]
    dt = jnp.maximum(pre, 0.0) + jnp.log1p(jnp.exp(-jnp.abs(pre)))
    a_dt = dt * (-jnp.exp(hp_ref[1:2, :]))
    r = lax.broadcasted_iota(jnp.int32, (L, L), 0)
    c = lax.broadcasted_iota(jnp.int32, (L, L), 1)
    causal = r >= c
    a_cs = _split3_dot_left(causal.astype(BF16), a_dt)
    a_cs_t = a_cs.T

    dt_full = _split3_dot(dt, spread_ref[...])
    acs_full = _split3_dot(a_cs, spread_ref[...])
    total_full = acs_full[L - 1:L, :]
    xdt = xs * dt_full
    xdt16 = xdt.astype(BF16)
    to_end16 = (xdt * jnp.exp(total_full - acs_full)).astype(BF16)
    decay_in = jnp.exp(acs_full)
    chunk_decay = jnp.exp(total_full)
    low = c < P
    zero16 = jnp.zeros((L, 2 * P), BF16)
    width = SSD_HPG * P

    ys = []
    for g in range(SSD_GROUPS):
        bm = bcs[:, g * SSD_STATE:(g + 1) * SSD_STATE]
        cm16 = bcs[:, GN + g * SSD_STATE:GN + (g + 1) * SSD_STATE].astype(BF16)
        cb = _dot_nt(cm16, bm.astype(BF16))
        lanes = slice(g * width, (g + 1) * width)
        h_in = state[g]
        y_off = _dot(cm16, h_in.astype(BF16)) * decay_in[:, lanes]
        state[g] = h_in * chunk_decay[:, lanes] + _dot(bm.T.astype(BF16), to_end16[:, lanes])
        diag = []
        for pp in range(SSD_HPG // 2):
            hd = g * SSD_HPG + 2 * pp
            decayed = []
            for col in (MISC_DT + hd, MISC_DT + hd + 1):
                seg = jnp.exp(jnp.where(causal, a_cs[:, col:col + 1] - a_cs_t[col:col + 1, :], -jnp.inf))
                decayed.append((cb * seg).astype(BF16))
            xp = xdt16[:, hd * P:(hd + 2) * P]
            x_bd = jnp.concatenate([jnp.where(low, xp, zero16), jnp.where(low, zero16, xp)], axis=0)
            diag.append(_dot(jnp.concatenate(decayed, axis=1), x_bd))
        ys.append(jnp.concatenate(diag, axis=1) + y_off)

    y = (jnp.concatenate(ys, axis=1) + dskip_ref[...] * xs) * _silu(z_ref[...])
    for g in range(SSD_GROUPS):
        lanes = slice(g * width, (g + 1) * width)
        yg = y[:, lanes]
        o_ref[:, lanes] = yg * lax.rsqrt(jnp.mean(yg * yg, axis=-1, keepdims=True) + EPS) * ng_ref[:, lanes]


def _ssd(pf, conv_w, conv_b, head_params, d_skip, norm_g, batch, seq):
    nc = seq // QB
    GN = SSD_GROUPS * SSD_STATE
    lane_head = jnp.arange(SSD_INNER)[None, :] // HEAD_DIM
    spread = (jnp.arange(LANES)[:, None] == MISC_DT + lane_head).astype(BF16)
    cwx, cwb = conv_w[:, :SSD_INNER], conv_w[:, SSD_INNER:]
    cbx, cbb = conv_b[:SSD_INNER].reshape(1, -1), conv_b[SSD_INNER:].reshape(1, -1)
    full = lambda shape: pl.BlockSpec(shape, lambda b, ci: (0,) * len(shape))
    return pl.pallas_call(
        _ssd_kernel,
        grid=(batch, nc),
        in_specs=[
            pl.BlockSpec((QB, SSD_INNER), lambda b, ci: (b * nc + ci, PF_Z // SSD_INNER)),
            pl.BlockSpec((QB, SSD_INNER), lambda b, ci: (b * nc + ci, PF_XS // SSD_INNER)),
            pl.BlockSpec((QB, 2 * GN), lambda b, ci: (b * nc + ci, PF_BC // (2 * GN))),
            pl.BlockSpec((QB, LANES), lambda b, ci: (b * nc + ci, PF_MISC // LANES)),
            full((SSD_CONV, SSD_INNER)),
            full((SSD_CONV, 2 * GN)),
            full((1, SSD_INNER)),
            full((1, 2 * GN)),
            full((8, LANES)),
            full((LANES, SSD_INNER)),
            full((1, SSD_INNER)),
            full((1, SSD_INNER)),
        ],
        out_specs=pl.BlockSpec((QB, SSD_INNER), lambda b, ci: (b * nc + ci, 0)),
        out_shape=jax.ShapeDtypeStruct((batch * seq, SSD_INNER), F32),
        scratch_shapes=[
            pltpu.VMEM((8, SSD_INNER), F32),
            pltpu.VMEM((8, 2 * GN), F32),
            pltpu.VMEM((SSD_GROUPS, SSD_STATE, SSD_HPG * HEAD_DIM), F32),
        ],
        compiler_params=_params(("parallel", "arbitrary")),
        name="ssd",
    )(pf, pf, pf, pf, cwx, cwb, cbx, cbb, head_params, spread,
      jnp.repeat(d_skip.astype(F32), HEAD_DIM).reshape(1, -1), norm_g.reshape(1, -1))


def _out_proj_kernel(x_ref, nsa_ref, sb_ref, ssd_ref, gn_ref, gs_ref, w_ref, o_ref, mix_ref):
    @pl.when(pl.program_id(1) == 0)
    def _():
        mix_ref[:, 0:NSA_WIDTH] = _rms(nsa_ref[...], gn_ref[...]).astype(BF16)
        mix_ref[:, NSA_WIDTH:NSA_WIDTH + SB_WIDTH] = _rms(sb_ref[...], gs_ref[...]).astype(BF16)
        mix_ref[:, NSA_WIDTH + SB_WIDTH:] = ssd_ref[...].astype(BF16)

    o_ref[...] = x_ref[...] + _dot(mix_ref[...], w_ref[...])


def _out_proj(x, o_nsa, o_sb, o_ssd, g_nsa, g_sb, w, *, tm=1024, tn=1024):
    n, d = x.shape
    dm = w.shape[0]
    return pl.pallas_call(
        _out_proj_kernel,
        grid=(n // tm, d // tn),
        in_specs=[
            pl.BlockSpec((tm, tn), lambda i, j: (i, j)),
            pl.BlockSpec((tm, NSA_WIDTH), lambda i, j: (i, 0)),
            pl.BlockSpec((tm, SB_WIDTH), lambda i, j: (i, 0)),
            pl.BlockSpec((tm, SSD_INNER), lambda i, j: (i, 0)),
            pl.BlockSpec((1, NSA_WIDTH), lambda i, j: (0, 0)),
            pl.BlockSpec((1, SB_WIDTH), lambda i, j: (0, 0)),
            pl.BlockSpec((dm, tn), lambda i, j: (0, j)),
        ],
        out_specs=pl.BlockSpec((tm, tn), lambda i, j: (i, j)),
        out_shape=jax.ShapeDtypeStruct((n, d), F32),
        scratch_shapes=[pltpu.VMEM((tm, dm), BF16)],
        compiler_params=_params(("parallel", "arbitrary")),
        name="out_proj",
    )(x, o_nsa, o_sb, o_ssd, g_nsa.reshape(1, -1), g_sb.reshape(1, -1), w)


def _rel_bucket(dist):
    dist = jnp.maximum(dist, 0)
    max_exact = REL_BUCKETS // 2
    log_ratio = jnp.log(jnp.maximum(dist, 1).astype(F32) / max_exact) / math.log(REL_MAX_DIST / max_exact)
    large = jnp.minimum(max_exact + (log_ratio * (REL_BUCKETS - max_exact)).astype(jnp.int32), REL_BUCKETS - 1)
    return jnp.where(dist < max_exact, dist, large)


def _bias_tables(rel_bias, seq):
    assert QB >= REL_MAX_DIST
    def lookup(dist):
        buckets = jnp.arange(REL_BUCKETS).reshape((-1,) + (1,) * dist.ndim)
        onehot = (_rel_bucket(dist)[None] == buckets).astype(F32)
        return jnp.einsum('kh,k...->h...', rel_bias, onehot, precision=lax.Precision.HIGHEST)

    t = jnp.arange(seq)[:, None]
    cend = jnp.arange(QB)[None, :] * CMP_STRIDE + CMP_BLOCK - 1
    n_cmp = (seq - CMP_BLOCK) // CMP_STRIDE + 1
    valid_c = (t >= cend) & (jnp.arange(QB)[None, :] < n_cmp)
    bias_c = jnp.where(valid_c, lookup(t - cend), NEG_INF)
    r = jnp.arange(QB)[:, None]
    m = jnp.arange(QB)[None, :]
    near = jnp.stack([lookup(r - m), lookup(QB + r - m)])
    near = near - rel_bias[REL_BUCKETS - 1][None, :, None, None]
    diag = near[0] + jnp.where(m > r, NEG_INF, 0.0)
    prev = near[1]
    rows_a = jnp.concatenate([prev, diag, jnp.full_like(diag, NEG_INF)], axis=2)
    rows_b = jnp.concatenate([jnp.zeros_like(diag), prev, diag], axis=2)
    near2 = jnp.concatenate([rows_a, rows_b], axis=1)
    return bias_c, near2.reshape(NSA_KV_HEADS, NSA_GQA * SW_TQ, 3 * QB)


def _expand_table(seq):
    j = jnp.arange(QB)[:, None]
    s = jnp.arange(seq)[None, :]
    return (s // SEL_BLOCK == j).astype(BF16)


def _overlap_table(seq):
    n_cmp = (seq - CMP_BLOCK) // CMP_STRIDE + 1
    cs = jnp.arange(QB)[:, None] * CMP_STRIDE
    ce = cs + CMP_BLOCK - 1
    ss = jnp.arange(QB)[None, :] * SEL_BLOCK
    ov = jnp.maximum(jnp.minimum(ce, ss + SEL_BLOCK - 1) - jnp.maximum(cs, ss) + 1, 0).astype(F32) / CMP_BLOCK
    keep = (jnp.arange(QB)[:, None] < n_cmp) & (jnp.arange(QB)[None, :] < seq // SEL_BLOCK)
    return jnp.where(keep, ov, 0.0).T.astype(BF16)


def _in_proj_weights(w):
    w = w.astype(BF16)
    scale = HEAD_DIM ** -0.5
    o_q, o_kv, o_gate = 0, NSA_WIDTH, NSA_WIDTH + 768
    o_sb = o_gate + 3 * NSA_HEADS
    o_z = o_sb + 3 * SB_WIDTH
    o_xbc = o_z + SSD_INNER
    o_dt = o_xbc + SSD_INNER + 2 * SSD_GROUPS * SSD_STATE
    d = w.shape[0]
    col = lambda a, n: w[:, a:a + n]
    zeros = lambda n: jnp.zeros((d, n), w.dtype)
    q_pad = jnp.pad(col(o_q, NSA_WIDTH).reshape(d, NSA_HEADS, HEAD_DIM) * scale,
                    ((0, 0), (0, 0), (0, LANES - HEAD_DIM))).reshape(d, NSA_HEADS * LANES)
    kv = col(o_kv + 256, 512).reshape(d, 2, 2, NSA_KV_HEADS, HEAD_DIM).transpose(0, 1, 3, 2, 4).reshape(d, 512)
    part_bf16 = [q_pad, kv, col(o_sb, SB_WIDTH) * scale, col(o_sb + SB_WIDTH, 2 * SB_WIDTH)]
    gates = 3 * NSA_GQA
    part_f32 = [col(o_z, 2 * SSD_INNER + 2 * SSD_GROUPS * SSD_STATE), col(o_kv, 256),
                col(o_gate, gates), zeros(MISC_DT - gates), col(o_dt, SSD_HEADS),
                zeros(LANES - MISC_DT - SSD_HEADS), col(o_gate + gates, gates), zeros(LANES - gates)]
    out = jnp.concatenate(part_bf16 + part_f32, axis=1)
    assert out.shape[1] == PB_COLS + PF_COLS and sum(p.shape[1] for p in part_bf16) == PB_COLS
    return out


def _head_params(dt_bias, a_log):
    rows = jnp.stack([dt_bias, a_log]).astype(F32)
    return jnp.zeros((8, LANES), F32).at[0:2, MISC_DT:MISC_DT + SSD_HEADS].set(rows)


def _mixer(x, l, tables, batch, seq, mix_norm, w_in, w_out, cmp_pos, cmp_w, nsa_norm, sb_norm,
           conv_w, conv_b, dt_bias, a_log, d_skip, ssd_norm, casts):
    hosts = [([w for w, _ in casts[k::2]], tuple(at for _, at in casts[k::2])) for k in range(2)]
    bias_c, near, overlap, expand = tables
    pb, pf = _in_proj(x, mix_norm[l], _in_proj_weights(w_in[l]), PB_COLS)

    ng = seq // CMP_STRIDE
    ucmp = pf[:, PF_CMP:PF_CMP + 256].reshape(batch, ng, CMP_STRIDE, 4, HEAD_DIM)
    ucmp = ucmp.transpose(0, 3, 1, 2, 4).reshape(batch, 4, ng, CMP_STRIDE * HEAD_DIM)
    pos2 = cmp_pos[l].reshape(2, 2, CMP_STRIDE * HEAD_DIM)
    o_cmp, sel = _nsa_cmp(pb, ucmp, pos2, cmp_w[l].astype(BF16), bias_c, overlap, batch, seq)
    o_nsa, w0 = _nsa_sel_win(pb, pf, o_cmp, sel, expand, near, batch, seq, *hosts[0])
    o_sb, w1 = _sb_attention(pb, batch, seq, *hosts[1])
    o_ssd = _ssd(pf, conv_w[l], conv_b[l], _head_params(dt_bias[l], a_log[l]), d_skip[l], ssd_norm[l],
                 batch, seq)
    out = _out_proj(x, o_nsa, o_sb, o_ssd, nsa_norm[l], sb_norm[l], w_out[l].astype(BF16))
    done = [None] * len(casts)
    for k, ws in enumerate((w0, w1)):
        done[k::2] = ws
    return out, done


def kernel(x, rel_bias, ffn1_norm, ffn1_w_gate, ffn1_w_up, ffn1_w_down, mix_norm, w_in, w_out, nsa_cmp_pos, nsa_cmp_w, nsa_out_norm, sb_out_norm, ssd_conv_w, ssd_conv_b, ssd_dt_bias, ssd_a_log, ssd_d, ssd_out_norm, ffn2_norm, ffn2_w_gate, ffn2_w_up, ffn2_w_down, final_norm):
    batch, seq, d = x.shape
    depth = w_in.shape[0]
    tables = _bias_tables(rel_bias, seq) + (_overlap_table(seq), _expand_table(seq))
    h = x.reshape(batch * seq, d)

    ffn1 = (ffn1_w_gate, ffn1_w_up, ffn1_w_down)
    ffn2 = (ffn2_w_gate, ffn2_w_up, ffn2_w_down)
    widen = lambda ws: tuple(w[None] for w in ws)
    w1 = widen(_cast_layer(ffn1, 0))
    for l in range(depth):
        h = _ffn(h, ffn1_norm[l], *w1, 0, final_norm, final=False)
        casts = [(w, l) for w in ffn2] + ([(w, l + 1) for w in ffn1] if l + 1 < depth else [])
        h, done = _mixer(h, l, tables, batch, seq, mix_norm, w_in, w_out, nsa_cmp_pos, nsa_cmp_w,
                         nsa_out_norm, sb_out_norm, ssd_conv_w, ssd_conv_b, ssd_dt_bias, ssd_a_log,
                         ssd_d, ssd_out_norm, casts)
        h = _ffn(h, ffn2_norm[l], *widen(done[0:3]), 0, final_norm, final=(l == depth - 1))
        w1 = widen(done[3:6])
    return h.reshape(batch, seq, d)
```

```python
import functools
import math

import jax
import jax.numpy as jnp
from jax import lax
from jax.experimental import pallas as pl
from jax.experimental.pallas import tpu as pltpu

LANES = 128
HEAD_DIM = 64
QB = 128
NEG_INF = -1e30
EPS = 1e-6

NSA_HEADS = 8
NSA_KV_HEADS = 2
NSA_GQA = NSA_HEADS // NSA_KV_HEADS
NSA_WIDTH = NSA_HEADS * HEAD_DIM
CMP_BLOCK = 32
CMP_STRIDE = 16
SEL_BLOCK = 64
SEL_TOPK = 8
FORCE_SCORE = 1e3
WINDOW = 512
SB_HEADS = 8
SB_WIDTH = SB_HEADS * HEAD_DIM
SSD_HEADS = 16
SSD_INNER = SSD_HEADS * HEAD_DIM
SSD_GROUPS = 2
SSD_HPG = SSD_HEADS // SSD_GROUPS
SSD_STATE = 128
SSD_CONV = 4
REL_BUCKETS = 32
REL_MAX_DIST = 128

PB_NSA_Q, PB_SEL, PB_WIN, PB_SB_Q, PB_SB_K, PB_SB_V, PB_COLS = 0, 1024, 1280, 1536, 2048, 2560, 3072
PF_Z, PF_XS, PF_BC, PF_CMP, PF_MISC, PF_COLS = 0, 1024, 2048, 2560, 2816, 3072
MISC_DT = 16

VMEM_LIMIT = 52 * 1024 * 1024

BF16 = jnp.bfloat16
F32 = jnp.float32


def _dot(a, b):
    return jnp.dot(a, b, preferred_element_type=F32)


def _dot_nt(a, b):
    return lax.dot_general(a, b, (((1,), (1,)), ((), ())), preferred_element_type=F32)


def _split3_dot(x, m):
    hi = x.astype(BF16)
    r1 = x - hi.astype(F32)
    mid = r1.astype(BF16)
    lo = (r1 - mid.astype(F32)).astype(BF16)
    return _dot(hi, m) + _dot(mid, m) + _dot(lo, m)


def _split3_dot_left(m, x):
    hi = x.astype(BF16)
    r1 = x - hi.astype(F32)
    mid = r1.astype(BF16)
    lo = (r1 - mid.astype(F32)).astype(BF16)
    return _dot(m, hi) + _dot(m, mid) + _dot(m, lo)


def _rms(x, g):
    return x * lax.rsqrt(jnp.mean(x * x, axis=-1, keepdims=True) + EPS) * g


def _silu(x):
    return x / (1.0 + jnp.exp(-x))


def _sigmoid(x):
    return 1.0 / (1.0 + jnp.exp(-x))


def _params(sem):
    return pltpu.CompilerParams(dimension_semantics=sem, vmem_limit_bytes=VMEM_LIMIT)


def _cast_specs(weights, layer, steps, step_of):
    in_specs, out_specs, out_shapes = [], [], []
    for w, at in zip(weights, layer):
        _, rows, cols = w.shape
        share = 1
        while (rows * share) % (16 * steps):
            share *= 2
        slab = rows * share // steps
        in_specs.append(pl.BlockSpec((None, slab, cols),
                                     lambda *ids, k=share, at=at: (at, step_of(*ids) // k, 0)))
        out_specs.append(pl.BlockSpec((slab, cols), lambda *ids, k=share: (step_of(*ids) // k, 0)))
        out_shapes.append(jax.ShapeDtypeStruct((rows, cols), BF16))
    return in_specs, out_specs, out_shapes


def _cast_slabs(srcs, dsts):
    for src, dst in zip(srcs, dsts):
        dst[...] = src[...].astype(BF16)


def _cast_kernel(*refs):
    _cast_slabs(refs[:len(refs) // 2], refs[len(refs) // 2:])


def _cast_layer(weights, layer, steps=16):
    c_in, c_out, c_shapes = _cast_specs(weights, (layer,) * len(weights), steps, lambda s: s)
    return pl.pallas_call(_cast_kernel, grid=(steps,), in_specs=c_in, out_specs=c_out, out_shape=c_shapes,
                          compiler_params=_params(("parallel",)), name="cast_weights")(*weights)


def _ffn_kernel(x_ref, g_ref, wg_ref, wu_ref, wd_ref, fg_ref, o_ref, h_ref, acc_ref, *, final):
    f = pl.program_id(1)

    @pl.when(f == 0)
    def _():
        h_ref[...] = _rms(x_ref[...], g_ref[...]).astype(BF16)
        acc_ref[...] = jnp.zeros_like(acc_ref)

    h = h_ref[...]
    gate = _dot(h, wg_ref[...])
    up = _dot(h, wu_ref[...])
    acc_ref[...] += _dot((_silu(gate) * up).astype(BF16), wd_ref[...])

    @pl.when(f == pl.num_programs(1) - 1)
    def _():
        y = x_ref[...] + 0.5 * acc_ref[...]
        if final:
            y = _rms(y, fg_ref[...])
        o_ref[...] = y


def _ffn(x, g, wg, wu, wd, layer, final_g, *, final, tm=512, tf=512):
    n, d = x.shape
    dff = wg.shape[2]
    return pl.pallas_call(
        functools.partial(_ffn_kernel, final=final),
        grid=(n // tm, dff // tf),
        in_specs=[
            pl.BlockSpec((tm, d), lambda i, f: (i, 0)),
            pl.BlockSpec((1, d), lambda i, f: (0, 0)),
            pl.BlockSpec((None, d, tf), lambda i, f: (layer, 0, f)),
            pl.BlockSpec((None, d, tf), lambda i, f: (layer, 0, f)),
            pl.BlockSpec((None, tf, d), lambda i, f: (layer, f, 0)),
            pl.BlockSpec((1, d), lambda i, f: (0, 0)),
        ],
        out_specs=pl.BlockSpec((tm, d), lambda i, f: (i, 0)),
        out_shape=jax.ShapeDtypeStruct((n, d), F32),
        scratch_shapes=[pltpu.VMEM((tm, d), BF16), pltpu.VMEM((tm, d), F32)],
        compiler_params=_params(("parallel", "arbitrary")),
        name="ffn",
    )(x, g.reshape(1, d), wg, wu, wd, final_g.reshape(1, d))


def _in_proj_kernel(x_ref, g_ref, w_ref, ob_ref, of_ref, h_ref, *, nb):
    j = pl.program_id(1)

    @pl.when(j == 0)
    def _():
        h_ref[...] = _rms(x_ref[...], g_ref[...]).astype(BF16)

    @pl.when(j < nb)
    def _():
        ob_ref[...] = _dot(h_ref[...], w_ref[...]).astype(BF16)

    @pl.when(j >= nb)
    def _():
        of_ref[...] = _dot(h_ref[...], w_ref[...])


def _in_proj(x, g, w, cols_bf16, *, tm=1024, tn=1024):
    n, d = x.shape
    c = w.shape[1]
    nb = cols_bf16 // tn
    return pl.pallas_call(
        functools.partial(_in_proj_kernel, nb=nb),
        grid=(n // tm, c // tn),
        in_specs=[
            pl.BlockSpec((tm, d), lambda i, j: (i, 0)),
            pl.BlockSpec((1, d), lambda i, j: (0, 0)),
            pl.BlockSpec((d, tn), lambda i, j: (0, j)),
        ],
        out_specs=[pl.BlockSpec((tm, tn), lambda i, j: (i, jnp.minimum(j, nb - 1))),
                   pl.BlockSpec((tm, tn), lambda i, j: (i, jnp.maximum(j - nb, 0)))],
        out_shape=[jax.ShapeDtypeStruct((n, cols_bf16), BF16),
                   jax.ShapeDtypeStruct((n, c - cols_bf16), F32)],
        scratch_shapes=[pltpu.VMEM((tm, d), BF16)],
        compiler_params=_params(("parallel", "arbitrary")),
        name="in_proj",
    )(x, g.reshape(1, d), w)


SB_GROUP = 8
SB_DEAD = 104.0


def _sb_kernel(*refs, n_cast):
    q_ref, k_ref, v_ref = refs[:3]
    o_ref = refs[3 + n_cast]
    kbd_ref, vbd_ref = refs[4 + 2 * n_cast:]
    _cast_slabs(refs[3:3 + n_cast], refs[4 + n_cast:4 + 2 * n_cast])
    _sb_body(q_ref, k_ref, v_ref, o_ref, kbd_ref, vbd_ref)


def _sb_body(q_ref, k_ref, v_ref, o_ref, kbd_ref, vbd_ref):
    i = pl.program_id(2)
    pairs = SB_GROUP // 2
    nkb = k_ref.shape[0] // QB
    r = lax.broadcasted_iota(jnp.int32, (QB, QB), 0)
    c = lax.broadcasted_iota(jnp.int32, (QB, QB), 1)
    strict_all = jnp.concatenate([c < r] * SB_GROUP, axis=0)
    later_ones = jnp.concatenate([(r > c).astype(BF16), jnp.ones((QB, QB), BF16)], axis=1)
    later_ones = jnp.concatenate([later_ones, later_ones], axis=0)

    @pl.when(i == 0)
    def _():
        low = c < HEAD_DIM
        zero16 = jnp.zeros((QB, QB), BF16)

        def build(j, carry):
            src = pl.ds(pl.multiple_of(j * QB, QB), QB)
            dst = pl.ds(pl.multiple_of(j * 2 * QB, 2 * QB), 2 * QB)
            for pp in range(pairs):
                for ref, out in ((k_ref, kbd_ref), (v_ref, vbd_ref)):
                    x = ref[src, pp * LANES:(pp + 1) * LANES]
                    out[pp, dst, :] = jnp.concatenate([jnp.where(low, x, zero16), jnp.where(low, zero16, x)],
                                                      axis=0)
            return carry

        lax.fori_loop(0, nkb, build, 0)

    def scores(j, diagonal):
        rows = pl.ds(pl.multiple_of(j * 2 * QB, 2 * QB), 2 * QB)
        zs = []
        for pp in range(pairs):
            z_pair = _dot_nt(q_ref[:, pp * LANES:(pp + 1) * LANES], kbd_ref[pp, rows, :])
            zs += [z_pair[:, 0:QB], z_pair[:, QB:2 * QB]]
        z = jnp.concatenate(zs, axis=0)
        soft = jnp.maximum(z, 0.0) + jnp.log(1.0 + jnp.exp(-jnp.abs(z)))
        if diagonal:
            soft = jnp.where(strict_all, soft, 0.0)
        hi = soft.astype(BF16)
        lo = (soft - hi.astype(F32)).astype(BF16)
        sums = _dot(jnp.concatenate([hi, lo], axis=1), later_ones)
        return rows, z - soft, sums

    def absorb(tails, accs, rows, log_beta, sums, diagonal):
        a = jnp.exp(log_beta - (tails + sums[:, 0:QB]))
        if diagonal:
            a = jnp.where(strict_all, a, 0.0)
        a = a.astype(BF16)
        outs = []
        for pp in range(pairs):
            pair = jnp.concatenate([a[2 * pp * QB:(2 * pp + 1) * QB], a[(2 * pp + 1) * QB:(2 * pp + 2) * QB]],
                                   axis=1)
            outs.append(_dot(pair, vbd_ref[pp, rows, :]))
        return tails + sums[:, QB:2 * QB], accs + jnp.concatenate(outs, axis=0)

    def sweep(js, carry):
        staged = [scores(j, False) for j in js]
        for st in staged:
            carry = absorb(*carry, *st, False)
        return carry

    carry = (jnp.zeros((SB_GROUP * QB, QB), F32), jnp.zeros((pairs * QB, LANES), F32))
    carry = absorb(*carry, *scores(i, True), True)
    odd = i % 2
    carry = lax.fori_loop(0, odd, lambda step, cr: sweep([i - 1], cr), carry)
    top = i - 1 - odd

    def live(state):
        step, smallest, _, _ = state
        return (step < i // 2) & (smallest < SB_DEAD)

    def pair_step(state):
        step, _, tails, accs = state
        tails, accs = sweep([top - 2 * step, top - 2 * step - 1], (tails, accs))
        return step + 1, jnp.min(tails), tails, accs

    accs = lax.while_loop(live, pair_step, (0, jnp.min(carry[0]), *carry))[3]
    for pp in range(pairs):
        o_ref[:, pp * LANES:(pp + 1) * LANES] = accs[pp * QB:(pp + 1) * QB]


def _sb_attention(pb, batch, seq, cast=(), cast_layer=()):
    nqb = seq // QB
    w = SB_GROUP * HEAD_DIM
    assert SB_HEADS == SB_GROUP
    qc, kc, vc = PB_SB_Q // w, PB_SB_K // w, PB_SB_V // w
    c_in, c_out, c_shapes = _cast_specs(cast, cast_layer, batch * nqb, lambda b, p, i: b * nqb + i)
    out = pl.pallas_call(
        functools.partial(_sb_kernel, n_cast=len(cast)),
        grid=(batch, SB_HEADS // SB_GROUP, nqb),
        in_specs=[
            pl.BlockSpec((QB, w), lambda b, p, i: (b * nqb + i, qc + p)),
            pl.BlockSpec((seq, w), lambda b, p, i: (b, kc + p)),
            pl.BlockSpec((seq, w), lambda b, p, i: (b, vc + p)),
        ] + c_in,
        out_specs=[pl.BlockSpec((QB, w), lambda b, p, i: (b * nqb + i, p))] + c_out,
        out_shape=[jax.ShapeDtypeStruct((batch * seq, SB_WIDTH), F32)] + c_shapes,
        scratch_shapes=[pltpu.VMEM((SB_GROUP // 2, 2 * seq, LANES), BF16),
                        pltpu.VMEM((SB_GROUP // 2, 2 * seq, LANES), BF16)],
        compiler_params=_params(("parallel", "parallel", "arbitrary")),
        name="sb_attention",
    )(pb, pb, pb, *cast)
    return out[0], out[1:]


CMP_TQ = 4 * QB


def _nsa_cmp_kernel(q_ref, uk_ref, uv_ref, pos_ref, w_ref, b0_ref, b1_ref, b2_ref, b3_ref, ovt_ref,
                    ocmp_ref, sel_ref, kc_ref, vc_ref, *, n_cmp, n_sel):
    i = pl.program_id(2)
    half = CMP_STRIDE * HEAD_DIM

    @pl.when(i == 0)
    def _():
        def compress(kv, u_ref):
            u = u_ref[0, 0]
            top = _dot((u + pos_ref[kv, 0:1, :]).astype(BF16), w_ref[kv, 0:half, :])
            bot = _dot((u + pos_ref[kv, 1:2, :]).astype(BF16), w_ref[kv, half:2 * half, :])
            return top + pltpu.roll(bot, QB - 1, 0)

        zeros = jnp.zeros((QB, HEAD_DIM), F32)
        kc_ref[...] = jnp.concatenate([compress(0, uk_ref), zeros], axis=1).astype(BF16)
        v = compress(1, uv_ref)
        vc_ref[...] = jnp.concatenate([jnp.concatenate([v, zeros], axis=1),
                                       jnp.concatenate([zeros, v], axis=1)], axis=0).astype(BF16)

    G = NSA_GQA
    TQ = CMP_TQ
    q4 = jnp.concatenate([q_ref[:, g * LANES:(g + 1) * LANES] for g in range(G)], axis=0)
    bias = jnp.concatenate([b[...] for b in (b0_ref, b1_ref, b2_ref, b3_ref)], axis=0)
    s = _dot_nt(q4, kc_ref[...]) + bias
    e = jnp.exp(s - jnp.max(s, axis=-1, keepdims=True))
    t_rows = i * TQ + lax.broadcasted_iota(jnp.int32, (TQ, 1), 0)
    any_valid = jnp.concatenate([t_rows >= CMP_BLOCK - 1] * G, axis=0)
    p = jnp.where(any_valid, e / jnp.sum(e, axis=-1, keepdims=True), 0.0)
    p16 = p.astype(BF16)
    for pp in range(G // 2):
        pair = jnp.concatenate([p16[2 * pp * TQ:(2 * pp + 1) * TQ], p16[(2 * pp + 1) * TQ:(2 * pp + 2) * TQ]],
                               axis=1)
        ocmp_ref[:, pp * LANES:(pp + 1) * LANES] = _dot(pair, vc_ref[...])
    p_all = p[0:TQ]
    for g in range(1, G):
        p_all = p_all + p[g * TQ:(g + 1) * TQ]

    blk = lax.broadcasted_iota(jnp.int32, (n_sel, QB), 0)
    for part in range(TQ // QB):
        rs = slice(part * QB, (part + 1) * QB)
        t0 = (i * (TQ // QB) + part) * QB
        p_sum = p_all[rs]
        hi = p_sum.astype(BF16)
        lo = (p_sum - hi.astype(F32)).astype(BF16)
        p_sel = (_dot_nt(ovt_ref[...], hi) + _dot_nt(ovt_ref[...], lo))[0:n_sel]
        t = t0 + lax.broadcasted_iota(jnp.int32, (n_sel, QB), 1)
        cur = t // SEL_BLOCK
        eligible = blk * SEL_BLOCK <= t
        forced = (blk == 0) | (blk == cur) | (blk == cur - 1)
        score = jnp.where(eligible, p_sel + jnp.where(forced, FORCE_SCORE, 0.0), NEG_INF)
        rank = jnp.zeros((n_sel, QB), F32)
        for j in range(n_sel):
            other = score[j:j + 1, :]
            ahead = (other > score) | ((other == score) & (blk > j))
            rank = rank + jnp.where(ahead, 1.0, 0.0)
        chosen = jnp.where(eligible & (rank < SEL_TOPK), 1.0, 0.0)
        chosen = jnp.concatenate([chosen, jnp.zeros((QB - n_sel, QB), F32)], axis=0)
        sel_ref[0, 0, rs, :] = chosen.T.astype(BF16)


def _nsa_cmp(pb, ucmp, pos2, cmp_w, bias_c, overlap_t, batch, seq):
    nqb = seq // CMP_TQ
    n_cmp = (seq - CMP_BLOCK) // CMP_STRIDE + 1
    n_sel = seq // SEL_BLOCK
    ng = seq // CMP_STRIDE
    assert ng == QB and n_sel <= QB and n_sel % 8 == 0
    wide = CMP_STRIDE * HEAD_DIM
    return pl.pallas_call(
        functools.partial(_nsa_cmp_kernel, n_cmp=n_cmp, n_sel=n_sel),
        grid=(batch, NSA_KV_HEADS, nqb),
        in_specs=[
            pl.BlockSpec((CMP_TQ, NSA_GQA * LANES), lambda b, h, i: (b * nqb + i, h)),
            pl.BlockSpec((1, 1, ng, wide), lambda b, h, i: (b, h, 0, 0)),
            pl.BlockSpec((1, 1, ng, wide), lambda b, h, i: (b, NSA_KV_HEADS + h, 0, 0)),
            pl.BlockSpec((2, 2, wide), lambda b, h, i: (0, 0, 0)),
            pl.BlockSpec((2, 2 * wide, HEAD_DIM), lambda b, h, i: (0, 0, 0)),
            *[pl.BlockSpec((None, CMP_TQ, QB), lambda b, h, i, g=g: (h * NSA_GQA + g, i, 0))
              for g in range(NSA_GQA)],
            pl.BlockSpec((QB, QB), lambda b, h, i: (0, 0)),
        ],
        out_specs=[
            pl.BlockSpec((CMP_TQ, 256), lambda b, h, i: (b * nqb + i, h)),
            pl.BlockSpec((1, 1, CMP_TQ, QB), lambda b, h, i: (b, h, i, 0)),
        ],
        out_shape=[
            jax.ShapeDtypeStruct((batch * seq, NSA_WIDTH), F32),
            jax.ShapeDtypeStruct((batch, NSA_KV_HEADS, seq, QB), BF16),
        ],
        scratch_shapes=[pltpu.VMEM((QB, LANES), BF16), pltpu.VMEM((2 * QB, LANES), BF16)],
        compiler_params=_params(("parallel", "parallel", "arbitrary")),
        name="nsa_compressed",
    )(pb, ucmp, ucmp, pos2, cmp_w, *[bias_c] * NSA_GQA, overlap_t)


SW_TQ = 2 * QB
SEL_CHUNK = 4 * QB
WIN_FAR = WINDOW - QB


def _nsa_sw_kernel(*refs, n_cast):
    n_in = 8
    _cast_slabs(refs[n_in:n_in + n_cast], refs[n_in + n_cast + 1:n_in + 2 * n_cast + 1])
    _nsa_sw_body(*refs[:n_in], refs[n_in + n_cast], *refs[n_in + 2 * n_cast + 1:])


def _nsa_sw_body(q_ref, ks_ref, kw_ref, ocmp_ref, misc_ref, sel_ref, exp_ref, tb_ref, o_ref, madd_ref):
    i = pl.program_id(2)
    G = NSA_GQA
    TQ = SW_TQ
    q4 = jnp.concatenate([q_ref[:, g * LANES:(g + 1) * LANES] for g in range(G)], axis=0)
    madd_ref[...] = (_dot(sel_ref[0, 0], exp_ref[...]) - 1.0) * (-NEG_INF)
    t_pos = i * TQ + lax.broadcasted_iota(jnp.int32, (TQ, 1), 0)

    low = lax.broadcasted_iota(jnp.int32, (QB, LANES), 1) < HEAD_DIM

    def tile(x, n=G):
        return jnp.concatenate([x] * n, axis=0)

    def ones_v(kv):
        return jnp.where(tile(low, kv.shape[0] // QB), jnp.ones_like(kv), kv)

    first = 2 * i - 1
    rows_m = pl.ds(pl.multiple_of(jnp.maximum(first, 0) * QB, QB), QB)
    rows_a = pl.ds(pl.multiple_of(2 * i * QB, QB), QB)
    rows_b = pl.ds(pl.multiple_of((2 * i + 1) * QB, QB), QB)
    gone = jnp.where(i >= 1, 0.0, NEG_INF)
    gone_m = jnp.concatenate([jnp.full((TQ, QB), gone, F32), jnp.zeros((TQ, 2 * QB), F32)], axis=1)
    far_end = first * QB

    kv = jnp.concatenate([ks_ref[rows_m, :], ks_ref[rows_a, :], ks_ref[rows_b, :]], axis=0)
    near_sel = jnp.concatenate([madd_ref[:, rows_m], madd_ref[:, rows_a], madd_ref[:, rows_b]], axis=1)
    s = _dot_nt(q4, kv) + (tb_ref[0] + tile(near_sel + gone_m))
    m = jnp.max(s, axis=-1, keepdims=True)
    acc = _dot(jnp.exp(s - m).astype(BF16), ones_v(kv))

    def sel_far(cidx, state):
        m, acc = state
        cols = pl.ds(pl.multiple_of(cidx * SEL_CHUNK, SEL_CHUNK), SEL_CHUNK)
        k_pos = cidx * SEL_CHUNK + lax.broadcasted_iota(jnp.int32, (TQ, SEL_CHUNK), 1)
        add = jnp.where(k_pos < far_end, madd_ref[:, cols], NEG_INF)
        kv = ks_ref[cols, :]
        s = _dot_nt(q4, kv) + tile(add)
        m_new = jnp.maximum(m, jnp.max(s, axis=-1, keepdims=True))
        return m_new, jnp.exp(m - m_new) * acc + _dot(jnp.exp(s - m_new).astype(BF16), ones_v(kv))

    _, sel_acc = lax.fori_loop(0, (2 * i + 2) // (SEL_CHUNK // QB), sel_far, (m, acc))

    start = jnp.maximum(first - WIN_FAR // QB, 0) * QB
    rows_far = pl.ds(pl.multiple_of(start, QB), WIN_FAR)
    k_pos = start + lax.broadcasted_iota(jnp.int32, (TQ, WIN_FAR), 1)
    live = (k_pos < far_end) & (k_pos > t_pos - WINDOW)
    kv = jnp.concatenate([kw_ref[rows_far, :], kw_ref[rows_m, :], kw_ref[rows_a, :], kw_ref[rows_b, :]], axis=0)
    bias = jnp.concatenate([tile(jnp.where(live, 0.0, NEG_INF)), tb_ref[0] + tile(gone_m)], axis=1)
    s = _dot_nt(q4, kv) + bias
    win_acc = _dot(jnp.exp(s - jnp.max(s, axis=-1, keepdims=True)).astype(BF16), ones_v(kv))

    gate = _sigmoid(misc_ref[...])
    low = tile(low, TQ // QB)

    def pair(acc, pp):
        a, b = acc[2 * pp * TQ:(2 * pp + 1) * TQ], acc[(2 * pp + 1) * TQ:(2 * pp + 2) * TQ]
        return (jnp.where(low, pltpu.roll(a, HEAD_DIM, 1), b)
                / jnp.where(low, a, pltpu.roll(b, HEAD_DIM, 1)))

    def pair_gate(branch, pp):
        ca, cb = 3 * (2 * pp) + branch, 3 * (2 * pp + 1) + branch
        return jnp.where(low, gate[:, ca:ca + 1], gate[:, cb:cb + 1])

    for pp in range(G // 2):
        lanes = slice(pp * LANES, (pp + 1) * LANES)
        o_ref[:, lanes] = (pair_gate(0, pp) * ocmp_ref[:, lanes] + pair_gate(1, pp) * pair(sel_acc, pp)
                           + pair_gate(2, pp) * pair(win_acc, pp))


def _nsa_sel_win(pb, pf, o_cmp, sel, expand, tb, batch, seq, cast=(), cast_layer=()):
    nqb = seq // SW_TQ
    G = NSA_GQA
    c_in, c_out, c_shapes = _cast_specs(cast, cast_layer, batch * NSA_KV_HEADS * nqb,
                                        lambda b, h, i: (b * NSA_KV_HEADS + h) * nqb + i)
    out = pl.pallas_call(
        functools.partial(_nsa_sw_kernel, n_cast=len(cast)),
        grid=(batch, NSA_KV_HEADS, nqb),
        in_specs=[
            pl.BlockSpec((SW_TQ, G * LANES), lambda b, h, i: (b * nqb + i, h)),
            pl.BlockSpec((seq, LANES), lambda b, h, i: (b, PB_SEL // LANES + h)),
            pl.BlockSpec((seq, LANES), lambda b, h, i: (b, PB_WIN // LANES + h)),
            pl.BlockSpec((SW_TQ, 256), lambda b, h, i: (b * nqb + i, h)),
            pl.BlockSpec((SW_TQ, LANES), lambda b, h, i: (b * nqb + i, PF_MISC // LANES + h)),
            pl.BlockSpec((1, 1, SW_TQ, QB), lambda b, h, i: (b, h, i, 0)),
            pl.BlockSpec((QB, seq), lambda b, h, i: (0, 0)),
            pl.BlockSpec((1, G * SW_TQ, 3 * QB), lambda b, h, i: (h, 0, 0)),
        ] + c_in,
        out_specs=[pl.BlockSpec((SW_TQ, 256), lambda b, h, i: (b * nqb + i, h))] + c_out,
        out_shape=[jax.ShapeDtypeStruct((batch * seq, NSA_WIDTH), F32)] + c_shapes,
        scratch_shapes=[pltpu.VMEM((SW_TQ, seq), F32)],
        compiler_params=_params(("parallel", "parallel", "arbitrary")),
        name="nsa_selected_window",
    )(pb, pb, pb, o_cmp, pf, sel, expand, tb, *cast)
    return out[0], out[1:]


def _ssd_kernel(z_ref, xs_ref, bc_ref, misc_ref, cwx_ref, cwb_ref, cbx_ref, cbb_ref, hp_ref, spread_ref,
                dskip_ref, ng_ref, o_ref, xbuf, bbuf, state):
    ci = pl.program_id(1)
    L = QB
    P = HEAD_DIM
    GN = SSD_GROUPS * SSD_STATE

    @pl.when(ci == 0)
    def _():
        xbuf[...] = jnp.zeros_like(xbuf)
        bbuf[...] = jnp.zeros_like(bbuf)
        state[...] = jnp.zeros_like(state)

    def conv_silu(buf, src_ref, w_ref, b_ref):
        cur = src_ref[...]
        ext = jnp.concatenate([buf[...], cur], axis=0)
        out = b_ref[...] + w_ref[SSD_CONV - 1:SSD_CONV, :] * cur
        for k in range(1, SSD_CONV):
            out = out + w_ref[SSD_CONV - 1 - k:SSD_CONV - k, :] * pltpu.roll(ext, k, 0)[8:8 + L]
        buf[...] = cur[L - 8:L]
        return _silu(out)

    xs = conv_silu(xbuf, xs_ref, cwx_ref, cbx_ref)
    bcs = conv_silu(bbuf, bc_ref, cwb_ref, cbb_ref)

    misc = misc_ref[...]
    pre = misc + hp_ref[0:1, :]
    dt = jnp.maximum(pre, 0.0) + jnp.log1p(jnp.exp(-jnp.abs(pre)))
    a_dt = dt * (-jnp.exp(hp_ref[1:2, :]))
    r = lax.broadcasted_iota(jnp.int32, (L, L), 0)
    c = lax.broadcasted_iota(jnp.int32, (L, L), 1)
    causal = r >= c
    a_cs = _split3_dot_left(causal.astype(BF16), a_dt)
    a_cs_t = a_cs.T

    dt_full = _split3_dot(dt, spread_ref[...])
    acs_full = _split3_dot(a_cs, spread_ref[...])
    total_full = acs_full[L - 1:L, :]
    xdt = xs * dt_full
    xdt16 = xdt.astype(BF16)
    to_end16 = (xdt * jnp.exp(total_full - acs_full)).astype(BF16)
    decay_in = jnp.exp(acs_full)
    chunk_decay = jnp.exp(total_full)
    low = c < P
    zero16 = jnp.zeros((L, 2 * P), BF16)
    width = SSD_HPG * P

    ys = []
    for g in range(SSD_GROUPS):
        bm = bcs[:, g * SSD_STATE:(g + 1) * SSD_STATE]
        cm16 = bcs[:, GN + g * SSD_STATE:GN + (g + 1) * SSD_STATE].astype(BF16)
        cb = _dot_nt(cm16, bm.astype(BF16))
        lanes = slice(g * width, (g + 1) * width)
        h_in = state[g]
        y_off = _dot(cm16, h_in.astype(BF16)) * decay_in[:, lanes]
        state[g] = h_in * chunk_decay[:, lanes] + _dot(bm.T.astype(BF16), to_end16[:, lanes])
        diag = []
        for pp in range(SSD_HPG // 2):
            hd = g * SSD_HPG + 2 * pp
            decayed = []
            for col in (MISC_DT + hd, MISC_DT + hd + 1):
                seg = jnp.exp(jnp.where(causal, a_cs[:, col:col + 1] - a_cs_t[col:col + 1, :], -jnp.inf))
                decayed.append((cb * seg).astype(BF16))
            xp = xdt16[:, hd * P:(hd + 2) * P]
            x_bd = jnp.concatenate([jnp.where(low, xp, zero16), jnp.where(low, zero16, xp)], axis=0)
            diag.append(_dot(jnp.concatenate(decayed, axis=1), x_bd))
        ys.append(jnp.concatenate(diag, axis=1) + y_off)

    y = (jnp.concatenate(ys, axis=1) + dskip_ref[...] * xs) * _silu(z_ref[...])
    for g in range(SSD_GROUPS):
        lanes = slice(g * width, (g + 1) * width)
        yg = y[:, lanes]
        o_ref[:, lanes] = yg * lax.rsqrt(jnp.mean(yg * yg, axis=-1, keepdims=True) + EPS) * ng_ref[:, lanes]


def _ssd(pf, conv_w, conv_b, head_params, d_skip, norm_g, batch, seq):
    nc = seq // QB
    GN = SSD_GROUPS * SSD_STATE
    lane_head = jnp.arange(SSD_INNER)[None, :] // HEAD_DIM
    spread = (jnp.arange(LANES)[:, None] == MISC_DT + lane_head).astype(BF16)
    cwx, cwb = conv_w[:, :SSD_INNER], conv_w[:, SSD_INNER:]
    cbx, cbb = conv_b[:SSD_INNER].reshape(1, -1), conv_b[SSD_INNER:].reshape(1, -1)
    full = lambda shape: pl.BlockSpec(shape, lambda b, ci: (0,) * len(shape))
    return pl.pallas_call(
        _ssd_kernel,
        grid=(batch, nc),
        in_specs=[
            pl.BlockSpec((QB, SSD_INNER), lambda b, ci: (b * nc + ci, PF_Z // SSD_INNER)),
            pl.BlockSpec((QB, SSD_INNER), lambda b, ci: (b * nc + ci, PF_XS // SSD_INNER)),
            pl.BlockSpec((QB, 2 * GN), lambda b, ci: (b * nc + ci, PF_BC // (2 * GN))),
            pl.BlockSpec((QB, LANES), lambda b, ci: (b * nc + ci, PF_MISC // LANES)),
            full((SSD_CONV, SSD_INNER)),
            full((SSD_CONV, 2 * GN)),
            full((1, SSD_INNER)),
            full((1, 2 * GN)),
            full((8, LANES)),
            full((LANES, SSD_INNER)),
            full((1, SSD_INNER)),
            full((1, SSD_INNER)),
        ],
        out_specs=pl.BlockSpec((QB, SSD_INNER), lambda b, ci: (b * nc + ci, 0)),
        out_shape=jax.ShapeDtypeStruct((batch * seq, SSD_INNER), F32),
        scratch_shapes=[
            pltpu.VMEM((8, SSD_INNER), F32),
            pltpu.VMEM((8, 2 * GN), F32),
            pltpu.VMEM((SSD_GROUPS, SSD_STATE, SSD_HPG * HEAD_DIM), F32),
        ],
        compiler_params=_params(("parallel", "arbitrary")),
        name="ssd",
    )(pf, pf, pf, pf, cwx, cwb, cbx, cbb, head_params, spread,
      jnp.repeat(d_skip.astype(F32), HEAD_DIM).reshape(1, -1), norm_g.reshape(1, -1))


def _out_proj_kernel(x_ref, nsa_ref, sb_ref, ssd_ref, gn_ref, gs_ref, w_ref, o_ref, mix_ref):
    @pl.when(pl.program_id(1) == 0)
    def _():
        mix_ref[:, 0:NSA_WIDTH] = _rms(nsa_ref[...], gn_ref[...]).astype(BF16)
        mix_ref[:, NSA_WIDTH:NSA_WIDTH + SB_WIDTH] = _rms(sb_ref[...], gs_ref[...]).astype(BF16)
        mix_ref[:, NSA_WIDTH + SB_WIDTH:] = ssd_ref[...].astype(BF16)

    o_ref[...] = x_ref[...] + _dot(mix_ref[...], w_ref[...])


def _out_proj(x, o_nsa, o_sb, o_ssd, g_nsa, g_sb, w, *, tm=1024, tn=1024):
    n, d = x.shape
    dm = w.shape[0]
    return pl.pallas_call(
        _out_proj_kernel,
        grid=(n // tm, d // tn),
        in_specs=[
            pl.BlockSpec((tm, tn), lambda i, j: (i, j)),
            pl.BlockSpec((tm, NSA_WIDTH), lambda i, j: (i, 0)),
            pl.BlockSpec((tm, SB_WIDTH), lambda i, j: (i, 0)),
            pl.BlockSpec((tm, SSD_INNER), lambda i, j: (i, 0)),
            pl.BlockSpec((1, NSA_WIDTH), lambda i, j: (0, 0)),
            pl.BlockSpec((1, SB_WIDTH), lambda i, j: (0, 0)),
            pl.BlockSpec((dm, tn), lambda i, j: (0, j)),
        ],
        out_specs=pl.BlockSpec((tm, tn), lambda i, j: (i, j)),
        out_shape=jax.ShapeDtypeStruct((n, d), F32),
        scratch_shapes=[pltpu.VMEM((tm, dm), BF16)],
        compiler_params=_params(("parallel", "arbitrary")),
        name="out_proj",
    )(x, o_nsa, o_sb, o_ssd, g_nsa.reshape(1, -1), g_sb.reshape(1, -1), w)


def _rel_bucket(dist):
    dist = jnp.maximum(dist, 0)
    max_exact = REL_BUCKETS // 2
    log_ratio = jnp.log(jnp.maximum(dist, 1).astype(F32) / max_exact) / math.log(REL_MAX_DIST / max_exact)
    large = jnp.minimum(max_exact + (log_ratio * (REL_BUCKETS - max_exact)).astype(jnp.int32), REL_BUCKETS - 1)
    return jnp.where(dist < max_exact, dist, large)


def _bias_tables(rel_bias, seq):
    assert QB >= REL_MAX_DIST
    def lookup(dist):
        buckets = jnp.arange(REL_BUCKETS).reshape((-1,) + (1,) * dist.ndim)
        onehot = (_rel_bucket(dist)[None] == buckets).astype(F32)
        return jnp.einsum('kh,k...->h...', rel_bias, onehot, precision=lax.Precision.HIGHEST)

    t = jnp.arange(seq)[:, None]
    cend = jnp.arange(QB)[None, :] * CMP_STRIDE + CMP_BLOCK - 1
    n_cmp = (seq - CMP_BLOCK) // CMP_STRIDE + 1
    valid_c = (t >= cend) & (jnp.arange(QB)[None, :] < n_cmp)
    bias_c = jnp.where(valid_c, lookup(t - cend), NEG_INF)
    r = jnp.arange(QB)[:, None]
    m = jnp.arange(QB)[None, :]
    near = jnp.stack([lookup(r - m), lookup(QB + r - m)])
    near = near - rel_bias[REL_BUCKETS - 1][None, :, None, None]
    diag = near[0] + jnp.where(m > r, NEG_INF, 0.0)
    prev = near[1]
    rows_a = jnp.concatenate([prev, diag, jnp.full_like(diag, NEG_INF)], axis=2)
    rows_b = jnp.concatenate([jnp.zeros_like(diag), prev, diag], axis=2)
    near2 = jnp.concatenate([rows_a, rows_b], axis=1)
    return bias_c, near2.reshape(NSA_KV_HEADS, NSA_GQA * SW_TQ, 3 * QB)


def _expand_table(seq):
    j = jnp.arange(QB)[:, None]
    s = jnp.arange(seq)[None, :]
    return (s // SEL_BLOCK == j).astype(BF16)


def _overlap_table(seq):
    n_cmp = (seq - CMP_BLOCK) // CMP_STRIDE + 1
    cs = jnp.arange(QB)[:, None] * CMP_STRIDE
    ce = cs + CMP_BLOCK - 1
    ss = jnp.arange(QB)[None, :] * SEL_BLOCK
    ov = jnp.maximum(jnp.minimum(ce, ss + SEL_BLOCK - 1) - jnp.maximum(cs, ss) + 1, 0).astype(F32) / CMP_BLOCK
    keep = (jnp.arange(QB)[:, None] < n_cmp) & (jnp.arange(QB)[None, :] < seq // SEL_BLOCK)
    return jnp.where(keep, ov, 0.0).T.astype(BF16)


def _in_proj_weights(w):
    w = w.astype(BF16)
    scale = HEAD_DIM ** -0.5
    o_q, o_kv, o_gate = 0, NSA_WIDTH, NSA_WIDTH + 768
    o_sb = o_gate + 3 * NSA_HEADS
    o_z = o_sb + 3 * SB_WIDTH
    o_xbc = o_z + SSD_INNER
    o_dt = o_xbc + SSD_INNER + 2 * SSD_GROUPS * SSD_STATE
    d = w.shape[0]
    col = lambda a, n: w[:, a:a + n]
    zeros = lambda n: jnp.zeros((d, n), w.dtype)
    q_pad = jnp.pad(col(o_q, NSA_WIDTH).reshape(d, NSA_HEADS, HEAD_DIM) * scale,
                    ((0, 0), (0, 0), (0, LANES - HEAD_DIM))).reshape(d, NSA_HEADS * LANES)
    kv = col(o_kv + 256, 512).reshape(d, 2, 2, NSA_KV_HEADS, HEAD_DIM).transpose(0, 1, 3, 2, 4).reshape(d, 512)
    part_bf16 = [q_pad, kv, col(o_sb, SB_WIDTH) * scale, col(o_sb + SB_WIDTH, 2 * SB_WIDTH)]
    gates = 3 * NSA_GQA
    part_f32 = [col(o_z, 2 * SSD_INNER + 2 * SSD_GROUPS * SSD_STATE), col(o_kv, 256),
                col(o_gate, gates), zeros(MISC_DT - gates), col(o_dt, SSD_HEADS),
                zeros(LANES - MISC_DT - SSD_HEADS), col(o_gate + gates, gates), zeros(LANES - gates)]
    out = jnp.concatenate(part_bf16 + part_f32, axis=1)
    assert out.shape[1] == PB_COLS + PF_COLS and sum(p.shape[1] for p in part_bf16) == PB_COLS
    return out


def _head_params(dt_bias, a_log):
    rows = jnp.stack([dt_bias, a_log]).astype(F32)
    return jnp.zeros((8, LANES), F32).at[0:2, MISC_DT:MISC_DT + SSD_HEADS].set(rows)


def _mixer(x, l, tables, batch, seq, mix_norm, w_in, w_out, cmp_pos, cmp_w, nsa_norm, sb_norm,
           conv_w, conv_b, dt_bias, a_log, d_skip, ssd_norm, casts):
    hosts = [([w for w, _ in casts], tuple(at for _, at in casts)), ([], ())]
    bias_c, near, overlap, expand = tables
    pb, pf = _in_proj(x, mix_norm[l], _in_proj_weights(w_in[l]), PB_COLS)

    ng = seq // CMP_STRIDE
    ucmp = pf[:, PF_CMP:PF_CMP + 256].reshape(batch, ng, CMP_STRIDE, 4, HEAD_DIM)
    ucmp = ucmp.transpose(0, 3, 1, 2, 4).reshape(batch, 4, ng, CMP_STRIDE * HEAD_DIM)
    pos2 = cmp_pos[l].reshape(2, 2, CMP_STRIDE * HEAD_DIM)
    o_cmp, sel = _nsa_cmp(pb, ucmp, pos2, cmp_w[l].astype(BF16), bias_c, overlap, batch, seq)
    o_nsa, w0 = _nsa_sel_win(pb, pf, o_cmp, sel, expand, near, batch, seq, *hosts[0])
    o_sb, w1 = _sb_attention(pb, batch, seq, *hosts[1])
    o_ssd = _ssd(pf, conv_w[l], conv_b[l], _head_params(dt_bias[l], a_log[l]), d_skip[l], ssd_norm[l],
                 batch, seq)
    out = _out_proj(x, o_nsa, o_sb, o_ssd, nsa_norm[l], sb_norm[l], w_out[l].astype(BF16))
    done = list(w0) + list(w1)
    return out, done


def kernel(x, rel_bias, ffn1_norm, ffn1_w_gate, ffn1_w_up, ffn1_w_down, mix_norm, w_in, w_out, nsa_cmp_pos, nsa_cmp_w, nsa_out_norm, sb_out_norm, ssd_conv_w, ssd_conv_b, ssd_dt_bias, ssd_a_log, ssd_d, ssd_out_norm, ffn2_norm, ffn2_w_gate, ffn2_w_up, ffn2_w_down, final_norm):
    batch, seq, d = x.shape
    depth = w_in.shape[0]
    tables = _bias_tables(rel_bias, seq) + (_overlap_table(seq), _expand_table(seq))
    h = x.reshape(batch * seq, d)

    ffn1 = (ffn1_w_gate, ffn1_w_up, ffn1_w_down)
    ffn2 = (ffn2_w_gate, ffn2_w_up, ffn2_w_down)
    widen = lambda ws: tuple(w[None] for w in ws)
    w1 = widen(_cast_layer(ffn1, 0))
    for l in range(depth):
        h = _ffn(h, ffn1_norm[l], *w1, 0, final_norm, final=False)
        casts = [(w, l) for w in ffn2] + ([(w, l + 1) for w in ffn1] if l + 1 < depth else [])
        h, done = _mixer(h, l, tables, batch, seq, mix_norm, w_in, w_out, nsa_cmp_pos, nsa_cmp_w,
                         nsa_out_norm, sb_out_norm, ssd_conv_w, ssd_conv_b, ssd_dt_bias, ssd_a_log,
                         ssd_d, ssd_out_norm, casts)
        h = _ffn(h, ffn2_norm[l], *widen(done[0:3]), 0, final_norm, final=(l == depth - 1))
        w1 = widen(done[3:6])
    return h.reshape(batch, seq, d)
```
